```python
import jax, jax.numpy as jnp
from jax import lax
import numpy as np

D_MODEL = 1024
BATCH = 2
SEQ = 8192
DEPTH = 2

D_A = 1024
CONV_A = 3
D_B = 1024
CONV_B = 31
D_C = 1024
N_HEADS_C = 8
HEAD_C = D_C // N_HEADS_C
CHUNK = 128
N_BRANCH = 3
SPLITS = (D_A, 2 * D_A, 3 * D_A, 3 * D_A + D_B, 3 * D_A + 2 * D_B, 3 * D_A + 2 * D_B + 2 * D_C)
D_IN = 3 * D_A + 2 * D_B + 2 * D_C + N_BRANCH * D_MODEL
N_GROUPS_MOE = 4
EXPERTS_PER_GROUP = 8
N_EXPERTS = N_GROUPS_MOE * EXPERTS_PER_GROUP
TOP_K = 2
D_FF_EXPERT = 512
MOE_BLOCK = 128
RMS_EPS = 1e-6
LN_EPS = 1e-5

kernel_name = "hybrid_conv_gmlp_hmoe_encoder"


def rms_norm(x, g):
    xf = x.astype(jnp.float32)
    y = xf * lax.rsqrt(jnp.mean(xf * xf, axis=-1, keepdims=True) + RMS_EPS)
    return (y * g.astype(jnp.float32)).astype(x.dtype)


def layer_norm(x, g, b):
    xf = x.astype(jnp.float32)
    mu = jnp.mean(xf, axis=-1, keepdims=True)
    var = jnp.mean(jnp.square(xf - mu), axis=-1, keepdims=True)
    y = (xf - mu) * lax.rsqrt(var + LN_EPS) * g.astype(jnp.float32) + b.astype(jnp.float32)
    return y.astype(x.dtype)


def depthwise_conv(x, w):
    k = w.shape[0]
    return lax.conv_general_dilated(
        x, w[:, None, :].astype(x.dtype), window_strides=(1,),
        padding=((k // 2, k // 2),), dimension_numbers=("NWC", "WIO", "NWC"),
        feature_group_count=x.shape[-1])


def short_conv_mixer(xa, ba, ca, w_conv, w_out):
    return (ba * depthwise_conv(ca * xa, w_conv)) @ w_out


def conformer_conv(val, gate, w_conv, ln_g, ln_b, w_out):
    z = val * jax.nn.sigmoid(gate)
    z = depthwise_conv(z, w_conv)
    z = jax.nn.silu(layer_norm(z, ln_g, ln_b))
    return z @ w_out


def chunked_spatial_gating(uv, ln_g, ln_b, w_s, b_s, w_out):
    uv = jax.nn.gelu(uv)
    u, v = jnp.split(uv, 2, axis=-1)
    v = layer_norm(v, ln_g, ln_b)
    bsz, s, _ = v.shape
    v = v.reshape(bsz, s // CHUNK, CHUNK, N_HEADS_C, HEAD_C)
    v = jnp.einsum("hpq,bnqhc->bnphc", w_s, v) + b_s.T[None, None, :, :, None]
    return (u * v.reshape(bsz, s, D_C)) @ w_out


def mixer_block(x, norm_g, w_in, conv_a, w_a_out, conv_b, ln_b_g, ln_b_b, w_b_out,
                ln_c_g, ln_c_b, w_s, b_s, w_c_out, w_o):
    bsz, s, d = x.shape
    h = rms_norm(x, norm_g)
    p = h @ w_in
    xa, ba, ca, val_b, gate_b, uv_c, gate_logits = jnp.split(p, SPLITS, axis=-1)
    y_a = short_conv_mixer(xa, ba, ca, conv_a, w_a_out)
    y_b = conformer_conv(val_b, gate_b, conv_b, ln_b_g, ln_b_b, w_b_out)
    y_c = chunked_spatial_gating(uv_c, ln_c_g, ln_c_b, w_s, b_s, w_c_out)
    g = jax.nn.sigmoid(gate_logits).reshape(bsz, s, N_BRANCH, d)
    merged = g[:, :, 0] * y_a + g[:, :, 1] * y_b + g[:, :, 2] * y_c
    return merged @ w_o


def hier_moe(h, w_group, b_group, w_router, b_router, w1, w3, w2):
    bsz, s, d = h.shape
    t = bsz * s
    hf = h.reshape(t, d)
    group_logits = (hf @ w_group + b_group).astype(jnp.float32)
    group_prob = jax.nn.softmax(group_logits, axis=-1)
    g_idx = jnp.argmax(group_logits, axis=-1)
    g_p = jnp.take_along_axis(group_prob, g_idx[:, None], axis=1)
    exp_logits = (hf @ w_router + b_router).astype(jnp.float32)
    exp_logits = exp_logits.reshape(t, N_GROUPS_MOE, EXPERTS_PER_GROUP)
    exp_logits = jnp.take_along_axis(exp_logits, g_idx[:, None, None], axis=1)[:, 0]
    top_logit, top_idx = lax.top_k(exp_logits, TOP_K)
    top_w = jax.nn.softmax(top_logit, axis=-1) * g_p
    n_assign = t * TOP_K
    expert = (g_idx[:, None] * EXPERTS_PER_GROUP + top_idx).reshape(-1).astype(jnp.int32)
    tok = jnp.repeat(jnp.arange(t, dtype=jnp.int32), TOP_K)
    wgt = top_w.reshape(-1).astype(h.dtype)
    order = jnp.argsort(expert)
    e_sorted = expert[order]
    tok_sorted = tok[order]
    w_sorted = wgt[order]
    counts = jnp.bincount(expert, length=N_EXPERTS)
    padded = (counts + MOE_BLOCK - 1) // MOE_BLOCK * MOE_BLOCK
    start = jnp.cumsum(counts) - counts
    padded_end = jnp.cumsum(padded)
    padded_start = padded_end - padded
    dest = padded_start[e_sorted] + jnp.arange(n_assign, dtype=jnp.int32) - start[e_sorted]
    n_blocks = n_assign // MOE_BLOCK + N_EXPERTS
    n_slots = n_blocks * MOE_BLOCK
    slot_tok = jnp.full((n_slots,), t, dtype=jnp.int32).at[dest].set(tok_sorted)
    slot_w = jnp.zeros((n_slots,), h.dtype).at[dest].set(w_sorted)
    block_start = jnp.arange(n_blocks, dtype=jnp.int32) * MOE_BLOCK
    block_expert = jnp.minimum(jnp.searchsorted(padded_end, block_start, side="right"), N_EXPERTS - 1)
    h_pad = jnp.concatenate([hf, jnp.zeros((1, d), h.dtype)], axis=0)
    xb = h_pad[slot_tok].reshape(n_blocks, MOE_BLOCK, d)

    def expert_block(args):
        xe, e = args
        return (jax.nn.silu(xe @ w1[e]) * (xe @ w3[e])) @ w2[e]

    yb = lax.map(expert_block, (xb, block_expert)).reshape(n_slots, d)
    y = jax.ops.segment_sum(yb * slot_w[:, None], slot_tok, num_segments=t + 1)[:t]
    return y.reshape(bsz, s, d)


def setup_inputs(seed: int = 0) -> dict:
    key = jax.random.key(seed)
    ks = jax.random.split(key, 26)

    def nrm(k, shape, scale):
        return jax.random.normal(k, shape, jnp.float32) * scale

    def gain(k, shape):
        return 1.0 + 0.02 * jax.random.normal(k, shape, jnp.float32)

    L, D = DEPTH, D_MODEL
    return {
        "x": nrm(ks[0], (BATCH, SEQ, D), 1.0),
        "norm_mix": gain(ks[1], (L, D)),
        "w_in": nrm(ks[2], (L, D, D_IN), D ** -0.5),
        "conv_a": nrm(ks[3], (L, CONV_A, D_A), CONV_A ** -0.5),
        "w_a_out": nrm(ks[4], (L, D_A, D), D_A ** -0.5),
        "conv_b": nrm(ks[5], (L, CONV_B, D_B), CONV_B ** -0.5),
        "ln_b_g": gain(ks[6], (L, D_B)),
        "ln_b_b": nrm(ks[7], (L, D_B), 0.02),
        "w_b_out": nrm(ks[8], (L, D_B, D), D_B ** -0.5),
        "ln_c_g": gain(ks[9], (L, D_C)),
        "ln_c_b": nrm(ks[10], (L, D_C), 0.02),
        "w_s": nrm(ks[11], (L, N_HEADS_C, CHUNK, CHUNK), CHUNK ** -0.5),
        "b_s": gain(ks[12], (L, N_HEADS_C, CHUNK)),
        "w_c_out": nrm(ks[13], (L, D_C, D), D_C ** -0.5),
        "w_o": nrm(ks[14], (L, D, D), D ** -0.5),
        "norm_ffn": gain(ks[15], (L, D)),
        "w_group": nrm(ks[16], (L, D, N_GROUPS_MOE), D ** -0.5),
        "b_group": nrm(ks[17], (L, N_GROUPS_MOE), 0.01),
        "w_router": nrm(ks[18], (L, D, N_EXPERTS), D ** -0.5),
        "b_router": nrm(ks[19], (L, N_EXPERTS), 0.01),
        "w1": nrm(ks[20], (L, N_EXPERTS, D, D_FF_EXPERT), D ** -0.5),
        "w3": nrm(ks[21], (L, N_EXPERTS, D, D_FF_EXPERT), D ** -0.5),
        "w2": nrm(ks[22], (L, N_EXPERTS, D_FF_EXPERT, D), D_FF_EXPERT ** -0.5),
        "norm_final": gain(ks[23], (D,)),
    }


def reference(x, norm_mix, w_in, conv_a, w_a_out, conv_b, ln_b_g, ln_b_b, w_b_out,
              ln_c_g, ln_c_b, w_s, b_s, w_c_out, w_o, norm_ffn, w_group, b_group,
              w_router, b_router, w1, w3, w2, norm_final):
    for l in range(DEPTH):
        x = x + mixer_block(x, norm_mix[l], w_in[l], conv_a[l], w_a_out[l], conv_b[l],
                            ln_b_g[l], ln_b_b[l], w_b_out[l], ln_c_g[l], ln_c_b[l],
                            w_s[l], b_s[l], w_c_out[l], w_o[l])
        x = x + hier_moe(rms_norm(x, norm_ffn[l]), w_group[l], b_group[l], w_router[l],
                         b_router[l], w1[l], w3[l], w2[l])
    return rms_norm(x, norm_final)
```

```python
import functools

import jax
import jax.numpy as jnp
from jax import lax
from jax.experimental import pallas as pl
from jax.experimental.pallas import tpu as pltpu

D = 1024
N_HEADS_C = 8
CHUNK = 128
CONV_A = 3
CONV_B = 31
N_GROUPS = 4
EPG = 8
N_EXPERTS = N_GROUPS * EPG
D_FF = 512
RMS_EPS = 1e-6
LN_EPS = 1e-5

C_XA, C_BA, C_CA, C_VB, C_GB, C_UV, C_G = 0, 1024, 2048, 3072, 4096, 5120, 7168
D_IN = 10240

TM = 256
HALO = 16
TE = TM + 2 * HALO
CW = 256
RB = 32
NLOG = 128
E_OFF = 8

TR = 512
BM = 256
TD = 256

VMEM_LIMIT = 56 * 1024 * 1024


def _sigmoid(x):
    return 0.5 * (jnp.tanh(0.5 * x) + 1.0)


def _gelu_tanh(x):
    return 0.5 * x * (1.0 + jnp.tanh(0.7978845608028654 * (x + 0.044715 * (x * x * x))))


def _layer_norm(x, g, b):
    mu = jnp.mean(x, axis=-1, keepdims=True)
    xc = x - mu
    var = jnp.mean(xc * xc, axis=-1, keepdims=True)
    return xc * lax.rsqrt(var + LN_EPS) * g + b


def _mixer_kernel(seq_tiles,
                  xp_ref, xc_ref, xn_ref, nm_ref, win_ref, ca_ref, wa_ref, cb_ref,
                  lnbg_ref, lnbb_ref, wb_ref, lncg_ref, lncb_ref, ws_ref, bsb_ref, wc_ref,
                  wo_ref, nf_ref, wr_ref, br_ref,
                  xo_ref, h2_ref, lg_ref,
                  hb_ref, zs_ref, cv_ref, mg_ref):
    i = pl.program_id(0)
    at_start = (i % seq_tiles) == 0
    at_end = (i % seq_tiles) == seq_tiles - 1

    def _rms(xv):
        ms = jnp.mean(xv * xv, axis=-1, keepdims=True)
        return (xv * lax.rsqrt(ms + RMS_EPS) * nm_ref[...]).astype(jnp.bfloat16)

    hb_ref[0:HALO, :] = _rms(xp_ref[...])
    hb_ref[HALO:HALO + TM, :] = _rms(xc_ref[...])
    hb_ref[HALO + TM:TE, :] = _rms(xn_ref[...])

    rows = lax.broadcasted_iota(jnp.int32, (TE, 1), 0)
    lo = jnp.where(at_start, HALO, 0)
    hi = jnp.where(at_end, HALO + TM, TE)
    valid = jnp.logical_and(rows >= lo, rows < hi)

    def proj(r0, r1, c0, width):
        return jnp.dot(hb_ref[r0:r1, :], win_ref[:, c0:c0 + width],
                       preferred_element_type=jnp.float32)

    for c in range(D // CW):
        c0 = c * CW
        xa = proj(0, TE, C_XA + c0, CW)
        cc = proj(0, TE, C_CA + c0, CW)
        t = jnp.where(valid, xa * cc, 0.0)
        conv = (ca_ref[0:1, c0:c0 + CW] * t[HALO - 1:HALO - 1 + TM]
                + ca_ref[1:2, c0:c0 + CW] * t[HALO:HALO + TM]
                + ca_ref[2:3, c0:c0 + CW] * t[HALO + 1:HALO + 1 + TM])
        ba = proj(HALO, HALO + TM, C_BA + c0, CW)
        cv_ref[:, c0:c0 + CW] = (ba * conv).astype(jnp.bfloat16)
    ya = jnp.dot(cv_ref[...], wa_ref[...], preferred_element_type=jnp.float32)
    for c in range(D // CW):
        c0 = c * CW
        ga = _sigmoid(proj(HALO, HALO + TM, C_G + c0, CW))
        mg_ref[:, c0:c0 + CW] = ga * ya[:, c0:c0 + CW]

    for c in range(D // CW):
        c0 = c * CW
        val = proj(0, TE, C_VB + c0, CW)
        gate = proj(0, TE, C_GB + c0, CW)
        z = jnp.where(valid, val * _sigmoid(gate), 0.0)
        for s in range(8):
            zs_ref[s, 0:TE - 8, :] = z[s:s + TE - 8]

        def conv_rows(rb, carry):
            r0 = pl.multiple_of(rb * RB, 8)
            for lt in range(CW // 128):
                acc = jnp.zeros((RB, 128), jnp.float32)
                for k in range(CONV_B):
                    off = k + HALO - CONV_B // 2
                    q, s = off // 8, off % 8
                    acc = acc + (cb_ref[k:k + 1, c0 + lt * 128:c0 + (lt + 1) * 128]
                                 * zs_ref[s, pl.ds(r0 + 8 * q, RB), lt * 128:(lt + 1) * 128])
                xo_ref[pl.ds(r0, RB), c0 + lt * 128:c0 + (lt + 1) * 128] = acc
            return carry

        lax.fori_loop(0, TM // RB, conv_rows, 0)
    zc = _layer_norm(xo_ref[...], lnbg_ref[...], lnbb_ref[...])
    cv_ref[...] = (zc * _sigmoid(zc)).astype(jnp.bfloat16)
    yb = jnp.dot(cv_ref[...], wb_ref[...], preferred_element_type=jnp.float32)
    for c in range(D // CW):
        c0 = c * CW
        gb = _sigmoid(proj(HALO, HALO + TM, C_G + D + c0, CW))
        mg_ref[:, c0:c0 + CW] += gb * yb[:, c0:c0 + CW]

    for c in range(D // CW):
        c0 = c * CW
        v = _gelu_tanh(proj(HALO, HALO + TM, C_UV + D + c0, CW))
        xo_ref[:, c0:c0 + CW] = v
    vn = _layer_norm(xo_ref[...], lncg_ref[...], lncb_ref[...])
    cv_ref[...] = vn.astype(jnp.bfloat16)
    for c in range(D // CW):
        c0 = c * CW
        u = _gelu_tanh(proj(HALO, HALO + TM, C_UV + c0, CW))
        for n in range(TM // CHUNK):
            for hh in range(CW // CHUNK):
                h = c * (CW // CHUNK) + hh
                sv = jnp.dot(ws_ref[h], cv_ref[n * CHUNK:(n + 1) * CHUNK, h * CHUNK:(h + 1) * CHUNK],
                             preferred_element_type=jnp.float32) + bsb_ref[h]
                xo_ref[n * CHUNK:(n + 1) * CHUNK, h * CHUNK:(h + 1) * CHUNK] = (
                    u[n * CHUNK:(n + 1) * CHUNK, hh * CHUNK:(hh + 1) * CHUNK] * sv)
    cv_ref[...] = xo_ref[...].astype(jnp.bfloat16)
    yc = jnp.dot(cv_ref[...], wc_ref[...], preferred_element_type=jnp.float32)
    for c in range(D // CW):
        c0 = c * CW
        gc = _sigmoid(proj(HALO, HALO + TM, C_G + 2 * D + c0, CW))
        mg_ref[:, c0:c0 + CW] += gc * yc[:, c0:c0 + CW]

    cv_ref[...] = mg_ref[...].astype(jnp.bfloat16)
    xnew = xc_ref[...] + jnp.dot(cv_ref[...], wo_ref[...], preferred_element_type=jnp.float32)
    xo_ref[...] = xnew
    ms = jnp.mean(xnew * xnew, axis=-1, keepdims=True)
    h2 = xnew * lax.rsqrt(ms + RMS_EPS) * nf_ref[...]
    h2_ref[...] = h2
    lg_ref[...] = jnp.dot(h2, wr_ref[...], preferred_element_type=jnp.float32,
                          precision=lax.Precision.HIGHEST) + br_ref[...]


def _const_spec(shape):
    nd = len(shape)
    return pl.BlockSpec(shape, lambda i, _n=nd: (0,) * _n, pipeline_mode=pl.Buffered(1))


def _mixer(x2, seq_len, nm, win, ca, wa, cb, lnbg, lnbb, wb, lncg, lncb, ws, bsb, wc, wo, nf, wr, br):
    T = x2.shape[0]
    n_tiles = T // TM
    hb = TM // HALO
    last_halo = T // HALO - 1
    in_specs = [
        pl.BlockSpec((HALO, D), lambda i: (jnp.maximum(i * hb - 1, 0), 0)),
        pl.BlockSpec((TM, D), lambda i: (i, 0)),
        pl.BlockSpec((HALO, D), lambda i: (jnp.minimum((i + 1) * hb, last_halo), 0)),
        _const_spec((1, D)), _const_spec((D, D_IN)), _const_spec((CONV_A, D)), _const_spec((D, D)),
        _const_spec((CONV_B, D)), _const_spec((1, D)), _const_spec((1, D)), _const_spec((D, D)),
        _const_spec((1, D)), _const_spec((1, D)), _const_spec((N_HEADS_C, CHUNK, CHUNK)),
        _const_spec((N_HEADS_C, CHUNK, CHUNK)), _const_spec((D, D)), _const_spec((D, D)),
        _const_spec((1, D)), _const_spec((D, NLOG)), _const_spec((1, NLOG)),
    ]
    out_specs = [
        pl.BlockSpec((TM, D), lambda i: (i, 0)),
        pl.BlockSpec((TM, D), lambda i: (i, 0)),
        pl.BlockSpec((TM, NLOG), lambda i: (i, 0)),
    ]
    return pl.pallas_call(
        functools.partial(_mixer_kernel, seq_len // TM),
        grid=(n_tiles,),
        in_specs=in_specs,
        out_specs=out_specs,
        out_shape=[jax.ShapeDtypeStruct((T, D), jnp.float32),
                   jax.ShapeDtypeStruct((T, D), jnp.float32),
                   jax.ShapeDtypeStruct((T, NLOG), jnp.float32)],
        scratch_shapes=[pltpu.VMEM((TE, D), jnp.bfloat16),
                        pltpu.VMEM((8, TE, CW), jnp.float32),
                        pltpu.VMEM((TM, D), jnp.bfloat16),
                        pltpu.VMEM((TM, D), jnp.float32)],
        compiler_params=pltpu.CompilerParams(dimension_semantics=("arbitrary",),
                                             vmem_limit_bytes=VMEM_LIMIT),
        name="mixer",
    )(x2, x2, x2, nm, win, ca, wa, cb, lnbg, lnbb, wb, lncg, lncb, ws, bsb, wc, wo, nf, wr, br)


def _route_kernel(lg_ref, idx_ref, wgt_ref, cnt_ref, carry_ref):
    i = pl.program_id(0)

    @pl.when(i == 0)
    def _():
        carry_ref[...] = jnp.zeros_like(carry_ref)

    lt = lg_ref[...].T
    g = [lt[j:j + 1, :] for j in range(N_GROUPS)]
    gmax = jnp.maximum(jnp.maximum(g[0], g[1]), jnp.maximum(g[2], g[3]))
    gidx = jnp.where(g[0] == gmax, 0.0, jnp.where(g[1] == gmax, 1.0, jnp.where(g[2] == gmax, 2.0, 3.0)))
    gsum = sum(jnp.exp(gj - gmax) for gj in g)
    g_p = 1.0 / gsum

    sel = lt[E_OFF + 3 * EPG:E_OFF + 4 * EPG, :]
    for j in (2, 1, 0):
        sel = jnp.where(gidx == float(j), lt[E_OFF + j * EPG:E_OFF + (j + 1) * EPG, :], sel)
    rid = lax.broadcasted_iota(jnp.int32, (EPG, TR), 0).astype(jnp.float32)
    m1 = jnp.max(sel, axis=0, keepdims=True)
    i1 = jnp.min(jnp.where(sel == m1, rid, float(EPG)), axis=0, keepdims=True)
    rest = jnp.where(rid == i1, -jnp.inf, sel)
    m2 = jnp.max(rest, axis=0, keepdims=True)
    i2 = jnp.min(jnp.where(rest == m2, rid, float(EPG)), axis=0, keepdims=True)
    e2x = jnp.exp(m2 - m1)
    den = 1.0 + e2x
    w1 = (1.0 / den) * g_p
    w2 = (e2x / den) * g_p
    e1 = gidx * float(EPG) + i1
    e2 = gidx * float(EPG) + i2

    eid = lax.broadcasted_iota(jnp.int32, (N_EXPERTS, TR), 0).astype(jnp.float32)
    oh1 = (eid == e1).astype(jnp.float32)
    oh2 = (eid == e2).astype(jnp.float32)
    oh = oh1 + oh2
    tr = lax.broadcasted_iota(jnp.int32, (TR, TR), 0)
    tc = lax.broadcasted_iota(jnp.int32, (TR, TR), 1)
    upper = (tr < tc).astype(jnp.bfloat16)
    before = jnp.dot(oh.astype(jnp.bfloat16), upper, preferred_element_type=jnp.float32)
    base = before + carry_ref[:, 0:1]
    r1 = jnp.sum(oh1 * base, axis=0, keepdims=True)
    r2 = jnp.sum(oh2 * base, axis=0, keepdims=True)
    carry_ref[...] = carry_ref[...] + jnp.sum(oh, axis=1, keepdims=True)

    idx_ref[...] = jnp.zeros_like(idx_ref)
    idx_ref[0:1, :] = e1.astype(jnp.int32)
    idx_ref[1:2, :] = e2.astype(jnp.int32)
    idx_ref[2:3, :] = r1.astype(jnp.int32)
    idx_ref[3:4, :] = r2.astype(jnp.int32)
    wgt_ref[...] = jnp.zeros_like(wgt_ref)
    wgt_ref[0:1, :] = w1
    wgt_ref[1:2, :] = w2
    cnt_ref[...] = carry_ref[...]


def _route(logits):
    T = logits.shape[0]
    return pl.pallas_call(
        _route_kernel,
        grid=(T // TR,),
        in_specs=[pl.BlockSpec((TR, NLOG), lambda i: (i, 0))],
        out_specs=[pl.BlockSpec((8, TR), lambda i: (0, i)),
                   pl.BlockSpec((8, TR), lambda i: (0, i)),
                   pl.BlockSpec((N_EXPERTS, 128), lambda i: (0, 0))],
        out_shape=[jax.ShapeDtypeStruct((8, T), jnp.int32),
                   jax.ShapeDtypeStruct((8, T), jnp.float32),
                   jax.ShapeDtypeStruct((N_EXPERTS, 128), jnp.float32)],
        scratch_shapes=[pltpu.VMEM((N_EXPERTS, 128), jnp.float32)],
        compiler_params=pltpu.CompilerParams(dimension_semantics=("arbitrary",)),
        name="route",
    )(logits)


def _dispatch_kernel(dest_ref, h_ref, xb_in_ref, xb_ref, sem):
    del xb_in_ref

    def row_copy(t, k):
        return pltpu.make_async_copy(h_ref.at[pl.ds(t, 1), :],
                                     xb_ref.at[pl.ds(dest_ref[0, k, t], 1), :], sem)

    def start(t, carry):
        row_copy(t, 0).start()
        row_copy(t, 1).start()
        return carry

    def wait(t, carry):
        row_copy(t, 0).wait()
        row_copy(t, 1).wait()
        return carry

    lax.fori_loop(0, TD, start, 0)
    lax.fori_loop(0, TD, wait, 0)


def _dispatch(h2, dest3, xb_init):
    T = h2.shape[0]
    n_slots = xb_init.shape[0]
    return pl.pallas_call(
        _dispatch_kernel,
        grid=(T // TD,),
        in_specs=[pl.BlockSpec((1, 2, TD), lambda i: (i, 0, 0), memory_space=pltpu.SMEM),
                  pl.BlockSpec((TD, D), lambda i: (i, 0)),
                  pl.BlockSpec(memory_space=pl.ANY)],
        out_specs=pl.BlockSpec(memory_space=pl.ANY),
        out_shape=jax.ShapeDtypeStruct((n_slots, D), jnp.float32),
        scratch_shapes=[pltpu.SemaphoreType.DMA(())],
        input_output_aliases={2: 0},
        compiler_params=pltpu.CompilerParams(dimension_semantics=("arbitrary",)),
        name="dispatch",
    )(dest3, h2, xb_init)


def _expert_kernel(be_ref, nu_ref, xb_ref, w1_ref, w3_ref, w2_ref, yb_ref):
    b = pl.program_id(0)

    @pl.when(b < nu_ref[0])
    def _():
        x = xb_ref[...].astype(jnp.bfloat16)
        a = jnp.dot(x, w1_ref[0], preferred_element_type=jnp.float32)
        g = jnp.dot(x, w3_ref[0], preferred_element_type=jnp.float32)
        hmid = (a * _sigmoid(a) * g).astype(jnp.bfloat16)
        yb_ref[...] = jnp.dot(hmid, w2_ref[0], preferred_element_type=jnp.float32)

    @pl.when(b >= nu_ref[0])
    def _():
        yb_ref[...] = jnp.zeros_like(yb_ref)


def _experts(block_expert, n_used, xb, w1, w3, w2):
    n_slots = xb.shape[0]
    n_blocks = n_slots // BM

    def row_map(b, be, nu):
        return (jnp.minimum(b, nu[0] - 1), 0)

    def out_map(b, be, nu):
        return (b, 0)

    def w_map(b, be, nu):
        return (be[b], 0, 0)

    grid_spec = pltpu.PrefetchScalarGridSpec(
        num_scalar_prefetch=2,
        grid=(n_blocks,),
        in_specs=[pl.BlockSpec((BM, D), row_map),
                  pl.BlockSpec((1, D, D_FF), w_map),
                  pl.BlockSpec((1, D, D_FF), w_map),
                  pl.BlockSpec((1, D_FF, D), w_map)],
        out_specs=pl.BlockSpec((BM, D), out_map),
    )
    return pl.pallas_call(
        _expert_kernel,
        grid_spec=grid_spec,
        out_shape=jax.ShapeDtypeStruct((n_slots, D), jnp.float32),
        compiler_params=pltpu.CompilerParams(dimension_semantics=("arbitrary",),
                                             vmem_limit_bytes=VMEM_LIMIT),
        name="experts",
    )(block_expert, n_used, xb, w1, w3, w2)


def _combine_kernel(final, dest_ref, x_ref, w_ref, nrm_ref, yb_ref, o_ref, rows_ref, sem):
    def row_copy(t, k):
        return pltpu.make_async_copy(yb_ref.at[pl.ds(dest_ref[0, k, t], 1), :],
                                     rows_ref.at[k, pl.ds(t, 1), :], sem)

    def start(t, carry):
        row_copy(t, 0).start()
        row_copy(t, 1).start()
        return carry

    def wait(t, carry):
        row_copy(t, 0).wait()
        row_copy(t, 1).wait()
        return carry

    lax.fori_loop(0, TD, start, 0)
    lax.fori_loop(0, TD, wait, 0)
    w = w_ref[...]
    y = x_ref[...] + w[:, 0:1] * rows_ref[0] + w[:, 1:2] * rows_ref[1]
    if final:
        ms = jnp.mean(y * y, axis=-1, keepdims=True)
        y = y * lax.rsqrt(ms + RMS_EPS) * nrm_ref[...]
    o_ref[...] = y


def _combine(dest3, x2, wgt2, nrm, yb, final):
    T = x2.shape[0]
    return pl.pallas_call(
        functools.partial(_combine_kernel, final),
        grid=(T // TD,),
        in_specs=[pl.BlockSpec((1, 2, TD), lambda i: (i, 0, 0), memory_space=pltpu.SMEM),
                  pl.BlockSpec((TD, D), lambda i: (i, 0)),
                  pl.BlockSpec((TD, 2), lambda i: (i, 0)),
                  pl.BlockSpec((1, D), lambda i: (0, 0)),
                  pl.BlockSpec(memory_space=pl.ANY)],
        out_specs=pl.BlockSpec((TD, D), lambda i: (i, 0)),
        out_shape=jax.ShapeDtypeStruct((T, D), jnp.float32),
        scratch_shapes=[pltpu.VMEM((2, TD, D), jnp.float32), pltpu.SemaphoreType.DMA(())],
        compiler_params=pltpu.CompilerParams(dimension_semantics=("arbitrary",)),
        name="combine",
    )(dest3, x2, wgt2, nrm, yb)


def _moe(x2, h2, logits, w1, w3, w2, nrm, final):
    T = x2.shape[0]
    idx, wgt, cnt = _route(logits)
    counts = cnt[:, 0].astype(jnp.int32)
    padded = (counts + BM - 1) // BM * BM
    padded_end = jnp.cumsum(padded)
    padded_start = padded_end - padded
    dest = jnp.stack([padded_start[idx[0]] + idx[2], padded_start[idx[1]] + idx[3]])
    dest3 = dest.reshape(2, T // TD, TD).transpose(1, 0, 2)
    n_blocks = (2 * T) // BM + N_EXPERTS
    block_start = jnp.arange(n_blocks, dtype=jnp.int32) * BM
    block_expert = jnp.minimum(
        jnp.searchsorted(padded_end, block_start, side="right"), N_EXPERTS - 1).astype(jnp.int32)
    n_used = (padded_end[-1] // BM).astype(jnp.int32).reshape(1)
    xb = _dispatch(h2, dest3, jnp.zeros((n_blocks * BM, D), jnp.float32))
    yb = _experts(block_expert, n_used, xb, w1, w3, w2)
    wgt2 = wgt[0:2].T
    return _combine(dest3, x2, wgt2, nrm, yb, final)


def kernel(x, norm_mix, w_in, conv_a, w_a_out, conv_b, ln_b_g, ln_b_b, w_b_out, ln_c_g, ln_c_b,
           w_s, b_s, w_c_out, w_o, norm_ffn, w_group, b_group, w_router, b_router, w1, w3, w2,
           norm_final):
    bsz, seq, d = x.shape
    depth = norm_mix.shape[0]
    bf = jnp.bfloat16
    x2 = x.reshape(bsz * seq, d)
    for l in range(depth):
        wr = jnp.zeros((d, NLOG), jnp.float32)
        wr = wr.at[:, 0:N_GROUPS].set(w_group[l]).at[:, E_OFF:E_OFF + N_EXPERTS].set(w_router[l])
        br = jnp.zeros((1, NLOG), jnp.float32)
        br = br.at[0, 0:N_GROUPS].set(b_group[l]).at[0, E_OFF:E_OFF + N_EXPERTS].set(b_router[l])
        bsb = jnp.broadcast_to(b_s[l][:, :, None], (N_HEADS_C, CHUNK, CHUNK))
        x2, h2, logits = _mixer(
            x2, seq, norm_mix[l][None], w_in[l].astype(bf), conv_a[l], w_a_out[l].astype(bf),
            conv_b[l], ln_b_g[l][None], ln_b_b[l][None], w_b_out[l].astype(bf),
            ln_c_g[l][None], ln_c_b[l][None], w_s[l].astype(bf), bsb, w_c_out[l].astype(bf),
            w_o[l].astype(bf), norm_ffn[l][None], wr, br)
        final = l == depth - 1
        x2 = _moe(x2, h2, logits, w1[l].astype(bf), w3[l].astype(bf), w2[l].astype(bf),
                  norm_final[None], final)
    return x2.reshape(bsz, seq, d)
```

```python
import functools

import jax
import jax.numpy as jnp
from jax import lax
from jax.experimental import pallas as pl
from jax.experimental.pallas import tpu as pltpu

D = 1024
N_HEADS_C = 8
CHUNK = 128
CONV_A = 3
CONV_B = 31
N_GROUPS = 4
EPG = 8
N_EXPERTS = N_GROUPS * EPG
D_FF = 512
RMS_EPS = 1e-6
LN_EPS = 1e-5

C_XA, C_BA, C_CA, C_VB, C_GB, C_UV, C_G = 0, 1024, 2048, 3072, 4096, 5120, 7168
D_IN = 10240

TM = 256
HALO = 16
TE = TM + 2 * HALO
CW = 256
RB = 32
NLOG = 128
E_OFF = 8

TR = 512
BM = 256
TD = 256

VMEM_LIMIT = 56 * 1024 * 1024


def _sigmoid(x):
    return 0.5 * (jnp.tanh(0.5 * x) + 1.0)


def _gelu_tanh(x):
    return 0.5 * x * (1.0 + jnp.tanh(0.7978845608028654 * (x + 0.044715 * (x * x * x))))


def _layer_norm(x, g, b):
    mu = jnp.mean(x, axis=-1, keepdims=True)
    xc = x - mu
    var = jnp.mean(xc * xc, axis=-1, keepdims=True)
    return xc * lax.rsqrt(var + LN_EPS) * g + b


def _mixer_kernel(seq_tiles,
                  xp_ref, xc_ref, xn_ref, nm_ref, win_ref, ca_ref, wa_ref, cb_ref,
                  lnbg_ref, lnbb_ref, wb_ref, lncg_ref, lncb_ref, ws_ref, bsb_ref, wc_ref,
                  wo_ref, nf_ref, wr_ref, br_ref,
                  xo_ref, h2_ref, lg_ref,
                  hb_ref, zs_ref, cv_ref, mg_ref):
    i = pl.program_id(0)
    at_start = (i % seq_tiles) == 0
    at_end = (i % seq_tiles) == seq_tiles - 1

    def _rms(xv):
        ms = jnp.mean(xv * xv, axis=-1, keepdims=True)
        return (xv * lax.rsqrt(ms + RMS_EPS) * nm_ref[...]).astype(jnp.bfloat16)

    hb_ref[0:HALO, :] = _rms(xp_ref[...])
    hb_ref[HALO:HALO + TM, :] = _rms(xc_ref[...])
    hb_ref[HALO + TM:TE, :] = _rms(xn_ref[...])

    rows = lax.broadcasted_iota(jnp.int32, (TE, 1), 0)
    lo = jnp.where(at_start, HALO, 0)
    hi = jnp.where(at_end, HALO + TM, TE)
    valid = jnp.logical_and(rows >= lo, rows < hi)

    def proj(r0, r1, c0, width):
        return jnp.dot(hb_ref[r0:r1, :], win_ref[:, c0:c0 + width],
                       preferred_element_type=jnp.float32)

    for c in range(D // CW):
        c0 = c * CW
        xa = proj(0, TE, C_XA + c0, CW)
        cc = proj(0, TE, C_CA + c0, CW)
        t = jnp.where(valid, xa * cc, 0.0)
        conv = (ca_ref[0:1, c0:c0 + CW] * t[HALO - 1:HALO - 1 + TM]
                + ca_ref[1:2, c0:c0 + CW] * t[HALO:HALO + TM]
                + ca_ref[2:3, c0:c0 + CW] * t[HALO + 1:HALO + 1 + TM])
        ba = proj(HALO, HALO + TM, C_BA + c0, CW)
        cv_ref[:, c0:c0 + CW] = (ba * conv).astype(jnp.bfloat16)
    ya = jnp.dot(cv_ref[...], wa_ref[...], preferred_element_type=jnp.float32)
    for c in range(D // CW):
        c0 = c * CW
        ga = _sigmoid(proj(HALO, HALO + TM, C_G + c0, CW))
        mg_ref[:, c0:c0 + CW] = ga * ya[:, c0:c0 + CW]

    for c in range(D // CW):
        c0 = c * CW
        val = proj(0, TE, C_VB + c0, CW)
        gate = proj(0, TE, C_GB + c0, CW)
        z = jnp.where(valid, val * _sigmoid(gate), 0.0)
        for s in range(8):
            zs_ref[s, 0:TE - 8, :] = z[s:s + TE - 8]

        def conv_rows(rb, carry):
            r0 = pl.multiple_of(rb * RB, 8)
            for lt in range(CW // 128):
                acc = jnp.zeros((RB, 128), jnp.float32)
                for k in range(CONV_B):
                    off = k + HALO - CONV_B // 2
                    q, s = off // 8, off % 8
                    acc = acc + (cb_ref[k:k + 1, c0 + lt * 128:c0 + (lt + 1) * 128]
                                 * zs_ref[s, pl.ds(r0 + 8 * q, RB), lt * 128:(lt + 1) * 128])
                xo_ref[pl.ds(r0, RB), c0 + lt * 128:c0 + (lt + 1) * 128] = acc
            return carry

        lax.fori_loop(0, TM // RB, conv_rows, 0)
    zc = _layer_norm(xo_ref[...], lnbg_ref[...], lnbb_ref[...])
    cv_ref[...] = (zc * _sigmoid(zc)).astype(jnp.bfloat16)
    yb = jnp.dot(cv_ref[...], wb_ref[...], preferred_element_type=jnp.float32)
    for c in range(D // CW):
        c0 = c * CW
        gb = _sigmoid(proj(HALO, HALO + TM, C_G + D + c0, CW))
        mg_ref[:, c0:c0 + CW] += gb * yb[:, c0:c0 + CW]

    for c in range(D // CW):
        c0 = c * CW
        v = _gelu_tanh(proj(HALO, HALO + TM, C_UV + D + c0, CW))
        xo_ref[:, c0:c0 + CW] = v
    vn = _layer_norm(xo_ref[...], lncg_ref[...], lncb_ref[...])
    cv_ref[...] = vn.astype(jnp.bfloat16)
    for c in range(D // CW):
        c0 = c * CW
        u = _gelu_tanh(proj(HALO, HALO + TM, C_UV + c0, CW))
        for n in range(TM // CHUNK):
            for hh in range(CW // CHUNK):
                h = c * (CW // CHUNK) + hh
                sv = jnp.dot(ws_ref[h], cv_ref[n * CHUNK:(n + 1) * CHUNK, h * CHUNK:(h + 1) * CHUNK],
                             preferred_element_type=jnp.float32) + bsb_ref[h]
                xo_ref[n * CHUNK:(n + 1) * CHUNK, h * CHUNK:(h + 1) * CHUNK] = (
                    u[n * CHUNK:(n + 1) * CHUNK, hh * CHUNK:(hh + 1) * CHUNK] * sv)
    cv_ref[...] = xo_ref[...].astype(jnp.bfloat16)
    yc = jnp.dot(cv_ref[...], wc_ref[...], preferred_element_type=jnp.float32)
    for c in range(D // CW):
        c0 = c * CW
        gc = _sigmoid(proj(HALO, HALO + TM, C_G + 2 * D + c0, CW))
        mg_ref[:, c0:c0 + CW] += gc * yc[:, c0:c0 + CW]

    cv_ref[...] = mg_ref[...].astype(jnp.bfloat16)
    xnew = xc_ref[...] + jnp.dot(cv_ref[...], wo_ref[...], preferred_element_type=jnp.float32)
    xo_ref[...] = xnew
    ms = jnp.mean(xnew * xnew, axis=-1, keepdims=True)
    h2 = xnew * lax.rsqrt(ms + RMS_EPS) * nf_ref[...]
    h2_ref[...] = h2
    lg_ref[...] = jnp.dot(h2, wr_ref[...], preferred_element_type=jnp.float32,
                          precision=lax.Precision.HIGHEST) + br_ref[...]


def _layer_spec(layer, shape):
    nd = len(shape)
    return pl.BlockSpec((None,) + tuple(shape), lambda i, _n=nd: (layer,) + (0,) * _n,
                        pipeline_mode=pl.Buffered(1))


def _mixer(layer, x2, seq_len, nm, win, ca, wa, cb, lnbg, lnbb, wb, lncg, lncb, ws, bsb, wc, wo, nf,
           wr, br):
    T = x2.shape[0]
    n_tiles = T // TM
    hb = TM // HALO
    last_halo = T // HALO - 1
    ls = functools.partial(_layer_spec, layer)
    in_specs = [
        pl.BlockSpec((HALO, D), lambda i: (jnp.maximum(i * hb - 1, 0), 0)),
        pl.BlockSpec((TM, D), lambda i: (i, 0)),
        pl.BlockSpec((HALO, D), lambda i: (jnp.minimum((i + 1) * hb, last_halo), 0)),
        ls((1, D)), ls((D, D_IN)), ls((CONV_A, D)), ls((D, D)),
        ls((CONV_B, D)), ls((1, D)), ls((1, D)), ls((D, D)),
        ls((1, D)), ls((1, D)), ls((N_HEADS_C, CHUNK, CHUNK)),
        ls((N_HEADS_C, CHUNK, CHUNK)), ls((D, D)), ls((D, D)),
        ls((1, D)), ls((D, NLOG)), ls((1, NLOG)),
    ]
    out_specs = [
        pl.BlockSpec((TM, D), lambda i: (i, 0)),
        pl.BlockSpec((TM, D), lambda i: (i, 0)),
        pl.BlockSpec((TM, NLOG), lambda i: (i, 0)),
    ]
    return pl.pallas_call(
        functools.partial(_mixer_kernel, seq_len // TM),
        grid=(n_tiles,),
        in_specs=in_specs,
        out_specs=out_specs,
        out_shape=[jax.ShapeDtypeStruct((T, D), jnp.float32),
                   jax.ShapeDtypeStruct((T, D), jnp.float32),
                   jax.ShapeDtypeStruct((T, NLOG), jnp.float32)],
        scratch_shapes=[pltpu.VMEM((TE, D), jnp.bfloat16),
                        pltpu.VMEM((8, TE, CW), jnp.float32),
                        pltpu.VMEM((TM, D), jnp.bfloat16),
                        pltpu.VMEM((TM, D), jnp.float32)],
        compiler_params=pltpu.CompilerParams(dimension_semantics=("arbitrary",),
                                             vmem_limit_bytes=VMEM_LIMIT),
        name="mixer",
    )(x2, x2, x2, nm, win, ca, wa, cb, lnbg, lnbb, wb, lncg, lncb, ws, bsb, wc, wo, nf, wr, br)


def _route_kernel(lg_ref, idx_ref, wgt_ref, cnt_ref, carry_ref):
    i = pl.program_id(0)

    @pl.when(i == 0)
    def _():
        carry_ref[...] = jnp.zeros_like(carry_ref)

    lt = lg_ref[...].T
    g = [lt[j:j + 1, :] for j in range(N_GROUPS)]
    gmax = jnp.maximum(jnp.maximum(g[0], g[1]), jnp.maximum(g[2], g[3]))
    gidx = jnp.where(g[0] == gmax, 0.0, jnp.where(g[1] == gmax, 1.0, jnp.where(g[2] == gmax, 2.0, 3.0)))
    gsum = sum(jnp.exp(gj - gmax) for gj in g)
    g_p = 1.0 / gsum

    sel = lt[E_OFF + 3 * EPG:E_OFF + 4 * EPG, :]
    for j in (2, 1, 0):
        sel = jnp.where(gidx == float(j), lt[E_OFF + j * EPG:E_OFF + (j + 1) * EPG, :], sel)
    rid = lax.broadcasted_iota(jnp.int32, (EPG, TR), 0).astype(jnp.float32)
    m1 = jnp.max(sel, axis=0, keepdims=True)
    i1 = jnp.min(jnp.where(sel == m1, rid, float(EPG)), axis=0, keepdims=True)
    rest = jnp.where(rid == i1, -jnp.inf, sel)
    m2 = jnp.max(rest, axis=0, keepdims=True)
    i2 = jnp.min(jnp.where(rest == m2, rid, float(EPG)), axis=0, keepdims=True)
    e2x = jnp.exp(m2 - m1)
    den = 1.0 + e2x
    w1 = (1.0 / den) * g_p
    w2 = (e2x / den) * g_p
    e1 = gidx * float(EPG) + i1
    e2 = gidx * float(EPG) + i2

    eid = lax.broadcasted_iota(jnp.int32, (N_EXPERTS, TR), 0).astype(jnp.float32)
    oh1 = (eid == e1).astype(jnp.float32)
    oh2 = (eid == e2).astype(jnp.float32)
    oh = oh1 + oh2
    tr = lax.broadcasted_iota(jnp.int32, (TR, TR), 0)
    tc = lax.broadcasted_iota(jnp.int32, (TR, TR), 1)
    upper = (tr < tc).astype(jnp.bfloat16)
    before = jnp.dot(oh.astype(jnp.bfloat16), upper, preferred_element_type=jnp.float32)
    base = before + carry_ref[:, 0:1]
    r1 = jnp.sum(oh1 * base, axis=0, keepdims=True)
    r2 = jnp.sum(oh2 * base, axis=0, keepdims=True)
    carry_ref[...] = carry_ref[...] + jnp.sum(oh, axis=1, keepdims=True)

    idx_ref[...] = jnp.zeros_like(idx_ref)
    idx_ref[0:1, :] = e1.astype(jnp.int32)
    idx_ref[1:2, :] = e2.astype(jnp.int32)
    idx_ref[2:3, :] = r1.astype(jnp.int32)
    idx_ref[3:4, :] = r2.astype(jnp.int32)
    wgt_ref[...] = jnp.zeros_like(wgt_ref)
    wgt_ref[0:1, :] = w1
    wgt_ref[1:2, :] = w2
    cnt_ref[...] = carry_ref[...]


def _route(logits):
    T = logits.shape[0]
    return pl.pallas_call(
        _route_kernel,
        grid=(T // TR,),
        in_specs=[pl.BlockSpec((TR, NLOG), lambda i: (i, 0))],
        out_specs=[pl.BlockSpec((8, TR), lambda i: (0, i)),
                   pl.BlockSpec((8, TR), lambda i: (0, i)),
                   pl.BlockSpec((N_EXPERTS, 128), lambda i: (0, 0))],
        out_shape=[jax.ShapeDtypeStruct((8, T), jnp.int32),
                   jax.ShapeDtypeStruct((8, T), jnp.float32),
                   jax.ShapeDtypeStruct((N_EXPERTS, 128), jnp.float32)],
        scratch_shapes=[pltpu.VMEM((N_EXPERTS, 128), jnp.float32)],
        compiler_params=pltpu.CompilerParams(dimension_semantics=("arbitrary",)),
        name="route",
    )(logits)


def _dispatch_kernel(n_blocks, lastblk_ref, nu_ref, dest_ref, h_ref, xb_ref, zero_ref, sem, zsem):
    i = pl.program_id(0)

    @pl.when(i == 0)
    def _():
        zero_ref[...] = jnp.zeros_like(zero_ref)

        def zero_copy(blk):
            return pltpu.make_async_copy(
                zero_ref, xb_ref.at[pl.ds(pl.multiple_of(blk * BM, BM), BM), :], zsem)

        def for_each_zero_block(fn):
            def per_expert(e, carry):
                @pl.when(lastblk_ref[e] >= 0)
                def _():
                    fn(zero_copy(lastblk_ref[e]))
                return carry

            def per_unused(b, carry):
                fn(zero_copy(b))
                return carry

            lax.fori_loop(0, N_EXPERTS, per_expert, 0)
            lax.fori_loop(nu_ref[0], n_blocks, per_unused, 0)

        for_each_zero_block(lambda cp: cp.start())
        for_each_zero_block(lambda cp: cp.wait())

    def start(t, carry):
        for k in range(2):
            pltpu.make_async_copy(h_ref.at[pl.ds(t, 1), :],
                                  xb_ref.at[pl.ds(dest_ref[k, t], 1), :], sem).start()
        return carry

    lax.fori_loop(0, TD, start, 0, unroll=8)
    for k in range(2):
        pltpu.make_async_copy(h_ref, xb_ref.at[pl.ds(0, TD), :], sem).wait()


def _dispatch(h2, dest, lastblk, n_used, n_blocks):
    T = h2.shape[0]
    grid_spec = pltpu.PrefetchScalarGridSpec(
        num_scalar_prefetch=2,
        grid=(T // TD,),
        in_specs=[pl.BlockSpec((2, TD), lambda i, lb, nu: (0, i), memory_space=pltpu.SMEM),
                  pl.BlockSpec((TD, D), lambda i, lb, nu: (i, 0))],
        out_specs=pl.BlockSpec(memory_space=pl.ANY),
        scratch_shapes=[pltpu.VMEM((BM, D), jnp.float32),
                        pltpu.SemaphoreType.DMA(()), pltpu.SemaphoreType.DMA(())],
    )
    return pl.pallas_call(
        functools.partial(_dispatch_kernel, n_blocks),
        grid_spec=grid_spec,
        out_shape=jax.ShapeDtypeStruct((n_blocks * BM, D), jnp.float32),
        compiler_params=pltpu.CompilerParams(dimension_semantics=("arbitrary",)),
        name="dispatch",
    )(lastblk, n_used, dest, h2)


def _expert_kernel(be_ref, nu_ref, xb_ref, w1_ref, w3_ref, w2_ref, yb_ref, w1b_ref, w3b_ref, w2b_ref):
    b = pl.program_id(0)
    used = b < nu_ref[0]
    new_expert = jnp.logical_or(b == 0, be_ref[b] != be_ref[jnp.maximum(b - 1, 0)])

    @pl.when(jnp.logical_and(used, new_expert))
    def _():
        w1b_ref[...] = w1_ref[...].astype(jnp.bfloat16)
        w3b_ref[...] = w3_ref[...].astype(jnp.bfloat16)
        w2b_ref[...] = w2_ref[...].astype(jnp.bfloat16)

    @pl.when(used)
    def _():
        x = xb_ref[...].astype(jnp.bfloat16)
        a = jnp.dot(x, w1b_ref[...], preferred_element_type=jnp.float32)
        g = jnp.dot(x, w3b_ref[...], preferred_element_type=jnp.float32)
        hmid = (a * _sigmoid(a) * g).astype(jnp.bfloat16)
        yb_ref[...] = jnp.dot(hmid, w2b_ref[...], preferred_element_type=jnp.float32)

    @pl.when(jnp.logical_not(used))
    def _():
        yb_ref[...] = jnp.zeros_like(yb_ref)


def _experts(layer, block_expert, n_used, xb, w1, w3, w2):
    n_slots = xb.shape[0]
    n_blocks = n_slots // BM

    def row_map(b, be, nu):
        return (jnp.minimum(b, nu[0] - 1), 0)

    def out_map(b, be, nu):
        return (b, 0)

    def w_map(b, be, nu):
        return (layer, be[b], 0, 0)

    grid_spec = pltpu.PrefetchScalarGridSpec(
        num_scalar_prefetch=2,
        grid=(n_blocks,),
        in_specs=[pl.BlockSpec((BM, D), row_map),
                  pl.BlockSpec((None, None, D, D_FF), w_map),
                  pl.BlockSpec((None, None, D, D_FF), w_map),
                  pl.BlockSpec((None, None, D_FF, D), w_map)],
        out_specs=pl.BlockSpec((BM, D), out_map),
        scratch_shapes=[pltpu.VMEM((D, D_FF), jnp.bfloat16),
                        pltpu.VMEM((D, D_FF), jnp.bfloat16),
                        pltpu.VMEM((D_FF, D), jnp.bfloat16)],
    )
    return pl.pallas_call(
        _expert_kernel,
        grid_spec=grid_spec,
        out_shape=jax.ShapeDtypeStruct((n_slots, D), jnp.float32),
        compiler_params=pltpu.CompilerParams(dimension_semantics=("arbitrary",),
                                             vmem_limit_bytes=VMEM_LIMIT),
        name="experts",
    )(block_expert, n_used, xb, w1, w3, w2)


def _combine_kernel(final, dest_ref, x_ref, w_ref, nrm_ref, yb_ref, o_ref, rows_ref, sem):
    def start(t, carry):
        for k in range(2):
            pltpu.make_async_copy(yb_ref.at[pl.ds(dest_ref[k, t], 1), :],
                                  rows_ref.at[k, pl.ds(t, 1), :], sem).start()
        return carry

    lax.fori_loop(0, TD, start, 0, unroll=8)
    for k in range(2):
        pltpu.make_async_copy(yb_ref.at[pl.ds(0, TD), :], rows_ref.at[k], sem).wait()
    w = w_ref[...].T
    y = x_ref[...] + w[:, 0:1] * rows_ref[0] + w[:, 1:2] * rows_ref[1]
    if final:
        ms = jnp.mean(y * y, axis=-1, keepdims=True)
        y = y * lax.rsqrt(ms + RMS_EPS) * nrm_ref[...]
    o_ref[...] = y


def _combine(dest, x2, wgt, nrm, yb, final):
    T = x2.shape[0]
    return pl.pallas_call(
        functools.partial(_combine_kernel, final),
        grid=(T // TD,),
        in_specs=[pl.BlockSpec((2, TD), lambda i: (0, i), memory_space=pltpu.SMEM),
                  pl.BlockSpec((TD, D), lambda i: (i, 0)),
                  pl.BlockSpec((8, TD), lambda i: (0, i)),
                  pl.BlockSpec((1, D), lambda i: (0, 0)),
                  pl.BlockSpec(memory_space=pl.ANY)],
        out_specs=pl.BlockSpec((TD, D), lambda i: (i, 0)),
        out_shape=jax.ShapeDtypeStruct((T, D), jnp.float32),
        scratch_shapes=[pltpu.VMEM((2, TD, D), jnp.float32), pltpu.SemaphoreType.DMA(())],
        compiler_params=pltpu.CompilerParams(dimension_semantics=("arbitrary",)),
        name="combine",
    )(dest, x2, wgt, nrm, yb)


def _moe(layer, x2, h2, logits, w1, w3, w2, nrm, final):
    T = x2.shape[0]
    idx, wgt, cnt = _route(logits)
    counts = cnt[:, 0].astype(jnp.int32)
    padded = (counts + BM - 1) // BM * BM
    padded_end = jnp.cumsum(padded)
    padded_start = padded_end - padded
    eids = jnp.arange(N_EXPERTS, dtype=jnp.int32)[:, None]
    start_of = lambda e: jnp.sum(jnp.where(e[None, :] == eids, padded_start[:, None], 0), axis=0)
    dest = jnp.stack([start_of(idx[0]) + idx[2], start_of(idx[1]) + idx[3]])
    n_blocks = (2 * T) // BM + N_EXPERTS
    block_start = jnp.arange(n_blocks, dtype=jnp.int32) * BM
    block_expert = jnp.minimum(
        jnp.sum((padded_end[None, :] <= block_start[:, None]).astype(jnp.int32), axis=1),
        N_EXPERTS - 1)
    n_used = (padded_end[-1] // BM).reshape(1)
    lastblk = jnp.where(counts > 0, padded_end // BM - 1, -1)
    xb = _dispatch(h2, dest, lastblk, n_used, n_blocks)
    yb = _experts(layer, block_expert, n_used, xb, w1, w3, w2)
    return _combine(dest, x2, wgt, nrm, yb, final)


def kernel(x, norm_mix, w_in, conv_a, w_a_out, conv_b, ln_b_g, ln_b_b, w_b_out, ln_c_g, ln_c_b,
           w_s, b_s, w_c_out, w_o, norm_ffn, w_group, b_group, w_router, b_router, w1, w3, w2,
           norm_final):
    bsz, seq, d = x.shape
    depth = norm_mix.shape[0]
    bf = jnp.bfloat16
    x2 = x.reshape(bsz * seq, d)
    wr = jnp.zeros((depth, d, NLOG), jnp.float32)
    wr = wr.at[:, :, 0:N_GROUPS].set(w_group).at[:, :, E_OFF:E_OFF + N_EXPERTS].set(w_router)
    br = jnp.zeros((depth, 1, NLOG), jnp.float32)
    br = br.at[:, 0, 0:N_GROUPS].set(b_group).at[:, 0, E_OFF:E_OFF + N_EXPERTS].set(b_router)
    bsb = jnp.broadcast_to(b_s[:, :, :, None], (depth, N_HEADS_C, CHUNK, CHUNK))
    row = lambda p: p[:, None, :]
    mixer_params = (row(norm_mix), w_in.astype(bf), conv_a, w_a_out.astype(bf), conv_b,
                    row(ln_b_g), row(ln_b_b), w_b_out.astype(bf), row(ln_c_g), row(ln_c_b),
                    w_s.astype(bf), bsb, w_c_out.astype(bf), w_o.astype(bf), row(norm_ffn), wr, br)
    for l in range(depth):
        x2, h2, logits = _mixer(l, x2, seq, *mixer_params)
        x2 = _moe(l, x2, h2, logits, w1, w3, w2, norm_final[None], l == depth - 1)
    return x2.reshape(bsz, seq, d)
```

```python
import functools

import jax
import jax.numpy as jnp
from jax import lax
from jax.experimental import pallas as pl
from jax.experimental.pallas import tpu as pltpu

D = 1024
N_HEADS_C = 8
CHUNK = 128
CONV_A = 3
CONV_B = 31
N_GROUPS = 4
EPG = 8
N_EXPERTS = N_GROUPS * EPG
D_FF = 512
RMS_EPS = 1e-6
LN_EPS = 1e-5

C_XA, C_BA, C_CA, C_VB, C_GB, C_UV, C_G = 0, 1024, 2048, 3072, 4096, 5120, 7168
D_IN = 10240

TM = 512
HALO = 16
TE = TM + 2 * HALO
CW = 256
RB = 64
NLOG = 128
E_OFF = 8

TR = 512
BM = 256
TD = 256

VMEM_LIMIT = 60 * 1024 * 1024


def _sigmoid(x):
    return 0.5 * (jnp.tanh(0.5 * x) + 1.0)


def _gelu_tanh(x):
    return 0.5 * x * (1.0 + jnp.tanh(0.7978845608028654 * (x + 0.044715 * (x * x * x))))


def _layer_norm(x, g, b):
    mu = jnp.mean(x, axis=-1, keepdims=True)
    xc = x - mu
    var = jnp.mean(xc * xc, axis=-1, keepdims=True)
    return xc * lax.rsqrt(var + LN_EPS) * g + b


def _mixer_kernel(seq_tiles,
                  xp_ref, xc_ref, xn_ref, nm_ref, win_ref, ca_ref, wa_ref, cb_ref,
                  lnbg_ref, lnbb_ref, wb_ref, lncg_ref, lncb_ref, ws_ref, bsb_ref, wc_ref,
                  wo_ref, nf_ref, wrh_ref, wrl_ref, br_ref,
                  xo_ref, h2_ref, lg_ref,
                  hb_ref, zq_ref, cv_ref, mg_ref):
    i = pl.program_id(0)
    at_start = (i % seq_tiles) == 0
    at_end = (i % seq_tiles) == seq_tiles - 1

    def _rms(xv):
        ms = jnp.mean(xv * xv, axis=-1, keepdims=True)
        return (xv * lax.rsqrt(ms + RMS_EPS) * nm_ref[...]).astype(jnp.bfloat16)

    hb_ref[0:HALO, :] = _rms(xp_ref[...])
    hb_ref[HALO:HALO + TM, :] = _rms(xc_ref[...])
    hb_ref[HALO + TM:TE, :] = _rms(xn_ref[...])

    rows = lax.broadcasted_iota(jnp.int32, (TE, 1), 0)
    lo = jnp.where(at_start, HALO, 0)
    hi = jnp.where(at_end, HALO + TM, TE)
    valid = jnp.logical_and(rows >= lo, rows < hi)

    def proj(r0, r1, c0, width):
        return jnp.dot(hb_ref[r0:r1, :], win_ref[:, c0:c0 + width],
                       preferred_element_type=jnp.float32)

    for c in range(D // CW):
        c0 = c * CW
        xa = proj(0, TE, C_XA + c0, CW)
        cc = proj(0, TE, C_CA + c0, CW)
        t = jnp.where(valid, xa * cc, 0.0)
        conv = (ca_ref[0:1, c0:c0 + CW] * t[HALO - 1:HALO - 1 + TM]
                + ca_ref[1:2, c0:c0 + CW] * t[HALO:HALO + TM]
                + ca_ref[2:3, c0:c0 + CW] * t[HALO + 1:HALO + 1 + TM])
        ba = proj(HALO, HALO + TM, C_BA + c0, CW)
        cv_ref[:, c0:c0 + CW] = (ba * conv).astype(jnp.bfloat16)
    for c in range(D // CW):
        c0 = c * CW
        ya = jnp.dot(cv_ref[...], wa_ref[:, c0:c0 + CW], preferred_element_type=jnp.float32)
        ga = _sigmoid(proj(HALO, HALO + TM, C_G + c0, CW))
        mg_ref[:, c0:c0 + CW] = ga * ya

    for c in range(D // CW):
        c0 = c * CW
        val = proj(0, TE, C_VB + c0, CW)
        gate = proj(0, TE, C_GB + c0, CW)
        zq = zq_ref.at[c % 2]
        zq[...] = jnp.where(valid, val * _sigmoid(gate), 0.0)
        for rb in range(TM // RB):
            r0 = rb * RB
            for lt in range(CW // 128):
                l0 = lt * 128
                acc = None
                for s in range(8):
                    part = None
                    for q in range(4):
                        k = 8 * q + s - (HALO - CONV_B // 2)
                        if 0 <= k < CONV_B:
                            term = (cb_ref[k:k + 1, c0 + l0:c0 + l0 + 128]
                                    * zq[r0 + 8 * q:r0 + 8 * q + RB + 8, l0:l0 + 128])
                            part = term if part is None else part + term
                    shifted = part[s:s + RB]
                    acc = shifted if acc is None else acc + shifted
                xo_ref[r0:r0 + RB, c0 + l0:c0 + l0 + 128] = acc
    zc = _layer_norm(xo_ref[...], lnbg_ref[...], lnbb_ref[...])
    cv_ref[...] = (zc * _sigmoid(zc)).astype(jnp.bfloat16)
    for c in range(D // CW):
        c0 = c * CW
        yb = jnp.dot(cv_ref[...], wb_ref[:, c0:c0 + CW], preferred_element_type=jnp.float32)
        gb = _sigmoid(proj(HALO, HALO + TM, C_G + D + c0, CW))
        mg_ref[:, c0:c0 + CW] += gb * yb

    for c in range(D // CW):
        c0 = c * CW
        v = _gelu_tanh(proj(HALO, HALO + TM, C_UV + D + c0, CW))
        xo_ref[:, c0:c0 + CW] = v
    vn = _layer_norm(xo_ref[...], lncg_ref[...], lncb_ref[...])
    cv_ref[...] = vn.astype(jnp.bfloat16)
    for c in range(D // CW):
        c0 = c * CW
        u = _gelu_tanh(proj(HALO, HALO + TM, C_UV + c0, CW))
        for n in range(TM // CHUNK):
            for hh in range(CW // CHUNK):
                h = c * (CW // CHUNK) + hh
                sv = jnp.dot(ws_ref[h], cv_ref[n * CHUNK:(n + 1) * CHUNK, h * CHUNK:(h + 1) * CHUNK],
                             preferred_element_type=jnp.float32) + bsb_ref[h]
                xo_ref[n * CHUNK:(n + 1) * CHUNK, h * CHUNK:(h + 1) * CHUNK] = (
                    u[n * CHUNK:(n + 1) * CHUNK, hh * CHUNK:(hh + 1) * CHUNK] * sv)
    cv_ref[...] = xo_ref[...].astype(jnp.bfloat16)
    for c in range(D // CW):
        c0 = c * CW
        yc = jnp.dot(cv_ref[...], wc_ref[:, c0:c0 + CW], preferred_element_type=jnp.float32)
        gc = _sigmoid(proj(HALO, HALO + TM, C_G + 2 * D + c0, CW))
        mg_ref[:, c0:c0 + CW] += gc * yc

    cv_ref[...] = mg_ref[...].astype(jnp.bfloat16)
    for c in range(D // CW):
        c0 = c * CW
        xo_ref[:, c0:c0 + CW] = xc_ref[:, c0:c0 + CW] + jnp.dot(
            cv_ref[...], wo_ref[:, c0:c0 + CW], preferred_element_type=jnp.float32)
    xnew = xo_ref[...]
    ms = jnp.mean(xnew * xnew, axis=-1, keepdims=True)
    h2 = xnew * lax.rsqrt(ms + RMS_EPS) * nf_ref[...]
    h2_ref[...] = h2
    h_hi = h2.astype(jnp.bfloat16)
    h_lo = (h2 - h_hi.astype(jnp.float32)).astype(jnp.bfloat16)
    dot32 = functools.partial(jnp.dot, preferred_element_type=jnp.float32)
    lg_ref[...] = (dot32(h_hi, wrh_ref[...]) + dot32(h_lo, wrh_ref[...])
                   + dot32(h_hi, wrl_ref[...]) + dot32(h_lo, wrl_ref[...]) + br_ref[...])


def _layer_spec(layer, shape):
    nd = len(shape)
    return pl.BlockSpec((None,) + tuple(shape), lambda i, _n=nd: (layer,) + (0,) * _n,
                        pipeline_mode=pl.Buffered(1))


def _mixer(layer, x2, seq_len, nm, win, ca, wa, cb, lnbg, lnbb, wb, lncg, lncb, ws, bsb, wc, wo, nf,
           wrh, wrl, br):
    T = x2.shape[0]
    n_tiles = T // TM
    hb = TM // HALO
    last_halo = T // HALO - 1
    ls = functools.partial(_layer_spec, layer)
    in_specs = [
        pl.BlockSpec((HALO, D), lambda i: (jnp.maximum(i * hb - 1, 0), 0)),
        pl.BlockSpec((TM, D), lambda i: (i, 0)),
        pl.BlockSpec((HALO, D), lambda i: (jnp.minimum((i + 1) * hb, last_halo), 0)),
        ls((1, D)), ls((D, D_IN)), ls((CONV_A, D)), ls((D, D)),
        ls((CONV_B, D)), ls((1, D)), ls((1, D)), ls((D, D)),
        ls((1, D)), ls((1, D)), ls((N_HEADS_C, CHUNK, CHUNK)),
        ls((N_HEADS_C, CHUNK, CHUNK)), ls((D, D)), ls((D, D)),
        ls((1, D)), ls((D, NLOG)), ls((D, NLOG)), ls((1, NLOG)),
    ]
    out_specs = [
        pl.BlockSpec((TM, D), lambda i: (i, 0)),
        pl.BlockSpec((TM, D), lambda i: (i, 0)),
        pl.BlockSpec((TM, NLOG), lambda i: (i, 0)),
    ]
    return pl.pallas_call(
        functools.partial(_mixer_kernel, seq_len // TM),
        grid=(n_tiles,),
        in_specs=in_specs,
        out_specs=out_specs,
        out_shape=[jax.ShapeDtypeStruct((T, D), jnp.float32),
                   jax.ShapeDtypeStruct((T, D), jnp.float32),
                   jax.ShapeDtypeStruct((T, NLOG), jnp.float32)],
        scratch_shapes=[pltpu.VMEM((TE, D), jnp.bfloat16),
                        pltpu.VMEM((2, TE, CW), jnp.float32),
                        pltpu.VMEM((TM, D), jnp.bfloat16),
                        pltpu.VMEM((TM, D), jnp.float32)],
        compiler_params=pltpu.CompilerParams(dimension_semantics=("arbitrary",),
                                             vmem_limit_bytes=VMEM_LIMIT),
        name="mixer",
    )(x2, x2, x2, nm, win, ca, wa, cb, lnbg, lnbb, wb, lncg, lncb, ws, bsb, wc, wo, nf, wrh, wrl, br)


def _route_kernel(lg_ref, idx_ref, wgt_ref, cnt_ref, carry_ref):
    i = pl.program_id(0)

    @pl.when(i == 0)
    def _():
        carry_ref[...] = jnp.zeros_like(carry_ref)

    lt = lg_ref[...].T
    g = [lt[j:j + 1, :] for j in range(N_GROUPS)]
    gmax = jnp.maximum(jnp.maximum(g[0], g[1]), jnp.maximum(g[2], g[3]))
    gidx = jnp.where(g[0] == gmax, 0.0, jnp.where(g[1] == gmax, 1.0, jnp.where(g[2] == gmax, 2.0, 3.0)))
    gsum = sum(jnp.exp(gj - gmax) for gj in g)
    g_p = 1.0 / gsum

    sel = lt[E_OFF + 3 * EPG:E_OFF + 4 * EPG, :]
    for j in (2, 1, 0):
        sel = jnp.where(gidx == float(j), lt[E_OFF + j * EPG:E_OFF + (j + 1) * EPG, :], sel)
    rid = lax.broadcasted_iota(jnp.int32, (EPG, TR), 0).astype(jnp.float32)
    m1 = jnp.max(sel, axis=0, keepdims=True)
    i1 = jnp.min(jnp.where(sel == m1, rid, float(EPG)), axis=0, keepdims=True)
    rest = jnp.where(rid == i1, -jnp.inf, sel)
    m2 = jnp.max(rest, axis=0, keepdims=True)
    i2 = jnp.min(jnp.where(rest == m2, rid, float(EPG)), axis=0, keepdims=True)
    e2x = jnp.exp(m2 - m1)
    den = 1.0 + e2x
    w1 = (1.0 / den) * g_p
    w2 = (e2x / den) * g_p
    e1 = gidx * float(EPG) + i1
    e2 = gidx * float(EPG) + i2

    eid = lax.broadcasted_iota(jnp.int32, (N_EXPERTS, TR), 0).astype(jnp.float32)
    oh1 = (eid == e1).astype(jnp.float32)
    oh2 = (eid == e2).astype(jnp.float32)
    oh = oh1 + oh2
    tr = lax.broadcasted_iota(jnp.int32, (TR, TR), 0)
    tc = lax.broadcasted_iota(jnp.int32, (TR, TR), 1)
    upper = (tr < tc).astype(jnp.bfloat16)
    before = jnp.dot(oh.astype(jnp.bfloat16), upper, preferred_element_type=jnp.float32)
    base = before + carry_ref[:, 0:1]
    r1 = jnp.sum(oh1 * base, axis=0, keepdims=True)
    r2 = jnp.sum(oh2 * base, axis=0, keepdims=True)
    carry_ref[...] = carry_ref[...] + jnp.sum(oh, axis=1, keepdims=True)

    idx_ref[...] = jnp.zeros_like(idx_ref)
    idx_ref[0:1, :] = e1.astype(jnp.int32)
    idx_ref[1:2, :] = e2.astype(jnp.int32)
    idx_ref[2:3, :] = r1.astype(jnp.int32)
    idx_ref[3:4, :] = r2.astype(jnp.int32)
    wgt_ref[...] = jnp.zeros_like(wgt_ref)
    wgt_ref[0:1, :] = w1
    wgt_ref[1:2, :] = w2
    cnt_ref[...] = carry_ref[...]


def _route(logits):
    T = logits.shape[0]
    return pl.pallas_call(
        _route_kernel,
        grid=(T // TR,),
        in_specs=[pl.BlockSpec((TR, NLOG), lambda i: (i, 0))],
        out_specs=[pl.BlockSpec((8, TR), lambda i: (0, i)),
                   pl.BlockSpec((8, TR), lambda i: (0, i)),
                   pl.BlockSpec((N_EXPERTS, 128), lambda i: (0, 0))],
        out_shape=[jax.ShapeDtypeStruct((8, T), jnp.int32),
                   jax.ShapeDtypeStruct((8, T), jnp.float32),
                   jax.ShapeDtypeStruct((N_EXPERTS, 128), jnp.float32)],
        scratch_shapes=[pltpu.VMEM((N_EXPERTS, 128), jnp.float32)],
        compiler_params=pltpu.CompilerParams(dimension_semantics=("arbitrary",)),
        name="route",
    )(logits)


def _dispatch_kernel(n_blocks, lastblk_ref, nu_ref, dest_ref, h_ref, xb_ref, zero_ref, sem, zsem):
    i = pl.program_id(0)

    @pl.when(i == 0)
    def _():
        zero_ref[...] = jnp.zeros_like(zero_ref)

        def zero_copy(blk):
            return pltpu.make_async_copy(
                zero_ref, xb_ref.at[pl.ds(pl.multiple_of(blk * BM, BM), BM), :], zsem)

        def for_each_zero_block(fn):
            def per_expert(e, carry):
                @pl.when(lastblk_ref[e] >= 0)
                def _():
                    fn(zero_copy(lastblk_ref[e]))
                return carry

            def per_unused(b, carry):
                fn(zero_copy(b))
                return carry

            lax.fori_loop(0, N_EXPERTS, per_expert, 0)
            lax.fori_loop(nu_ref[0], n_blocks, per_unused, 0)

        for_each_zero_block(lambda cp: cp.start())
        for_each_zero_block(lambda cp: cp.wait())

    def start(t, carry):
        for k in range(2):
            pltpu.make_async_copy(h_ref.at[pl.ds(t, 1), :],
                                  xb_ref.at[pl.ds(dest_ref[k, t], 1), :], sem).start()
        return carry

    lax.fori_loop(0, TD, start, 0, unroll=8)
    for k in range(2):
        pltpu.make_async_copy(h_ref, xb_ref.at[pl.ds(0, TD), :], sem).wait()


def _dispatch(h2, dest, lastblk, n_used, n_blocks):
    T = h2.shape[0]
    grid_spec = pltpu.PrefetchScalarGridSpec(
        num_scalar_prefetch=2,
        grid=(T // TD,),
        in_specs=[pl.BlockSpec((2, TD), lambda i, lb, nu: (0, i), memory_space=pltpu.SMEM),
                  pl.BlockSpec((TD, D), lambda i, lb, nu: (i, 0))],
        out_specs=pl.BlockSpec(memory_space=pl.ANY),
        scratch_shapes=[pltpu.VMEM((BM, D), jnp.float32),
                        pltpu.SemaphoreType.DMA(()), pltpu.SemaphoreType.DMA(())],
    )
    return pl.pallas_call(
        functools.partial(_dispatch_kernel, n_blocks),
        grid_spec=grid_spec,
        out_shape=jax.ShapeDtypeStruct((n_blocks * BM, D), jnp.float32),
        compiler_params=pltpu.CompilerParams(dimension_semantics=("arbitrary",)),
        name="dispatch",
    )(lastblk, n_used, dest, h2)


def _expert_kernel(be_ref, nu_ref, xb_ref, w1_ref, w3_ref, w2_ref, yb_ref, w1b_ref, w3b_ref, w2b_ref):
    b = pl.program_id(0)
    used = b < nu_ref[0]
    new_expert = jnp.logical_or(b == 0, be_ref[b] != be_ref[jnp.maximum(b - 1, 0)])

    @pl.when(jnp.logical_and(used, new_expert))
    def _():
        w1b_ref[...] = w1_ref[...].astype(jnp.bfloat16)
        w3b_ref[...] = w3_ref[...].astype(jnp.bfloat16)
        w2b_ref[...] = w2_ref[...].astype(jnp.bfloat16)

    @pl.when(used)
    def _():
        x = xb_ref[...].astype(jnp.bfloat16)
        a = jnp.dot(x, w1b_ref[...], preferred_element_type=jnp.float32)
        g = jnp.dot(x, w3b_ref[...], preferred_element_type=jnp.float32)
        hmid = (a * _sigmoid(a) * g).astype(jnp.bfloat16)
        yb_ref[...] = jnp.dot(hmid, w2b_ref[...], preferred_element_type=jnp.float32)

    @pl.when(jnp.logical_not(used))
    def _():
        yb_ref[...] = jnp.zeros_like(yb_ref)


def _experts(layer, block_expert, n_used, xb, w1, w3, w2):
    n_slots = xb.shape[0]
    n_blocks = n_slots // BM

    def row_map(b, be, nu):
        return (jnp.minimum(b, nu[0] - 1), 0)

    def out_map(b, be, nu):
        return (b, 0)

    def w_map(b, be, nu):
        return (layer, be[b], 0, 0)

    grid_spec = pltpu.PrefetchScalarGridSpec(
        num_scalar_prefetch=2,
        grid=(n_blocks,),
        in_specs=[pl.BlockSpec((BM, D), row_map),
                  pl.BlockSpec((None, None, D, D_FF), w_map),
                  pl.BlockSpec((None, None, D, D_FF), w_map),
                  pl.BlockSpec((None, None, D_FF, D), w_map)],
        out_specs=pl.BlockSpec((BM, D), out_map),
        scratch_shapes=[pltpu.VMEM((D, D_FF), jnp.bfloat16),
                        pltpu.VMEM((D, D_FF), jnp.bfloat16),
                        pltpu.VMEM((D_FF, D), jnp.bfloat16)],
    )
    return pl.pallas_call(
        _expert_kernel,
        grid_spec=grid_spec,
        out_shape=jax.ShapeDtypeStruct((n_slots, D), jnp.float32),
        compiler_params=pltpu.CompilerParams(dimension_semantics=("arbitrary",),
                                             vmem_limit_bytes=VMEM_LIMIT),
        name="experts",
    )(block_expert, n_used, xb, w1, w3, w2)


def _combine_kernel(final, dest_ref, x_ref, w_ref, nrm_ref, yb_ref, o_ref, rows_ref, sem):
    def start(t, carry):
        for k in range(2):
            pltpu.make_async_copy(yb_ref.at[pl.ds(dest_ref[k, t], 1), :],
                                  rows_ref.at[k, pl.ds(t, 1), :], sem).start()
        return carry

    lax.fori_loop(0, TD, start, 0, unroll=8)
    for k in range(2):
        pltpu.make_async_copy(yb_ref.at[pl.ds(0, TD), :], rows_ref.at[k], sem).wait()
    w = w_ref[...].T
    y = x_ref[...] + w[:, 0:1] * rows_ref[0] + w[:, 1:2] * rows_ref[1]
    if final:
        ms = jnp.mean(y * y, axis=-1, keepdims=True)
        y = y * lax.rsqrt(ms + RMS_EPS) * nrm_ref[...]
    o_ref[...] = y


def _combine(dest, x2, wgt, nrm, yb, final):
    T = x2.shape[0]
    return pl.pallas_call(
        functools.partial(_combine_kernel, final),
        grid=(T // TD,),
        in_specs=[pl.BlockSpec((2, TD), lambda i: (0, i), memory_space=pltpu.SMEM),
                  pl.BlockSpec((TD, D), lambda i: (i, 0)),
                  pl.BlockSpec((8, TD), lambda i: (0, i)),
                  pl.BlockSpec((1, D), lambda i: (0, 0)),
                  pl.BlockSpec(memory_space=pl.ANY)],
        out_specs=pl.BlockSpec((TD, D), lambda i: (i, 0)),
        out_shape=jax.ShapeDtypeStruct((T, D), jnp.float32),
        scratch_shapes=[pltpu.VMEM((2, TD, D), jnp.float32), pltpu.SemaphoreType.DMA(())],
        compiler_params=pltpu.CompilerParams(dimension_semantics=("arbitrary",)),
        name="combine",
    )(dest, x2, wgt, nrm, yb)


def _moe(layer, x2, h2, logits, w1, w3, w2, nrm, final):
    T = x2.shape[0]
    idx, wgt, cnt = _route(logits)
    counts = cnt[:, 0].astype(jnp.int32)
    padded = (counts + BM - 1) // BM * BM
    padded_end = jnp.cumsum(padded)
    padded_start = padded_end - padded
    eids = jnp.arange(N_EXPERTS, dtype=jnp.int32)[:, None]
    start_of = lambda e: jnp.sum(jnp.where(e[None, :] == eids, padded_start[:, None], 0), axis=0)
    dest = jnp.stack([start_of(idx[0]) + idx[2], start_of(idx[1]) + idx[3]])
    n_blocks = (2 * T) // BM + N_EXPERTS
    block_start = jnp.arange(n_blocks, dtype=jnp.int32) * BM
    block_expert = jnp.minimum(
        jnp.sum((padded_end[None, :] <= block_start[:, None]).astype(jnp.int32), axis=1),
        N_EXPERTS - 1)
    n_used = (padded_end[-1] // BM).reshape(1)
    lastblk = jnp.where(counts > 0, padded_end // BM - 1, -1)
    xb = _dispatch(h2, dest, lastblk, n_used, n_blocks)
    yb = _experts(layer, block_expert, n_used, xb, w1, w3, w2)
    return _combine(dest, x2, wgt, nrm, yb, final)


def kernel(x, norm_mix, w_in, conv_a, w_a_out, conv_b, ln_b_g, ln_b_b, w_b_out, ln_c_g, ln_c_b,
           w_s, b_s, w_c_out, w_o, norm_ffn, w_group, b_group, w_router, b_router, w1, w3, w2,
           norm_final):
    bsz, seq, d = x.shape
    depth = norm_mix.shape[0]
    bf = jnp.bfloat16
    x2 = x.reshape(bsz * seq, d)
    wr = jnp.zeros((depth, d, NLOG), jnp.float32)
    wr = wr.at[:, :, 0:N_GROUPS].set(w_group).at[:, :, E_OFF:E_OFF + N_EXPERTS].set(w_router)
    wr_hi = wr.astype(bf)
    wr_lo = (wr - wr_hi.astype(jnp.float32)).astype(bf)
    br = jnp.zeros((depth, 1, NLOG), jnp.float32)
    br = br.at[:, 0, 0:N_GROUPS].set(b_group).at[:, 0, E_OFF:E_OFF + N_EXPERTS].set(b_router)
    bsb = jnp.broadcast_to(b_s[:, :, :, None], (depth, N_HEADS_C, CHUNK, CHUNK))
    row = lambda p: p[:, None, :]
    mixer_params = (row(norm_mix), w_in.astype(bf), conv_a, w_a_out.astype(bf), conv_b,
                    row(ln_b_g), row(ln_b_b), w_b_out.astype(bf), row(ln_c_g), row(ln_c_b),
                    w_s.astype(bf), bsb, w_c_out.astype(bf), w_o.astype(bf), row(norm_ffn), wr_hi, wr_lo, br)
    for l in range(depth):
        x2, h2, logits = _mixer(l, x2, seq, *mixer_params)
        x2 = _moe(l, x2, h2, logits, w1, w3, w2, norm_final[None], l == depth - 1)
    return x2.reshape(bsz, seq, d)
```

```python
import functools

import jax
import jax.numpy as jnp
from jax import lax
from jax.experimental import pallas as pl
from jax.experimental.pallas import tpu as pltpu

D = 1024
DP = D // 2
N_HEADS_C = 8
CHUNK = 128
CONV_A = 3
CONV_B = 31
N_GROUPS = 4
EPG = 8
N_EXPERTS = N_GROUPS * EPG
D_FF = 512
RMS_EPS = 1e-6
LN_EPS = 1e-5

C_XA, C_BA, C_CA, C_VB, C_GB, C_UV, C_G = 0, 1024, 2048, 3072, 4096, 5120, 7168
D_IN = 10240

TM = 512
HALO = 16
TE = TM + 2 * HALO
CW = 256
RB = 64
NLOG = 128
E_OFF = 8

TR = 512
BM = 256
TD = 256

VMEM_LIMIT = 60 * 1024 * 1024


def _sigmoid(x):
    return 0.5 * (jnp.tanh(0.5 * x) + 1.0)


def _gelu_tanh(x):
    return 0.5 * x * (1.0 + jnp.tanh(0.7978845608028654 * (x + 0.044715 * (x * x * x))))


def _pack_bf16_pairs(x):
    c = x.shape[1] // 2
    as_bits = lambda v: lax.bitcast_convert_type(v.astype(jnp.bfloat16).astype(jnp.float32), jnp.uint32)
    return as_bits(x[:, :c]) | (as_bits(x[:, c:]) >> 16)


def _unpack_bf16_pairs(p):
    hi = lax.bitcast_convert_type(p & jnp.uint32(0xFFFF0000), jnp.float32)
    lo = lax.bitcast_convert_type(p << 16, jnp.float32)
    return hi, lo


def _layer_norm(x, g, b):
    mu = jnp.mean(x, axis=-1, keepdims=True)
    xc = x - mu
    var = jnp.mean(xc * xc, axis=-1, keepdims=True)
    return xc * lax.rsqrt(var + LN_EPS) * g + b


def _mixer_kernel(seq_tiles,
                  xp_ref, xc_ref, xn_ref, nm_ref, win_ref, ca_ref, wa_ref, cb_ref,
                  lnbg_ref, lnbb_ref, wb_ref, lncg_ref, lncb_ref, ws_ref, bsb_ref, wc_ref,
                  wo_ref, nf_ref, wrh_ref, wrl_ref, br_ref,
                  xo_ref, h2_ref, lg_ref,
                  hb_ref, zq_ref, cv_ref, mg_ref):
    i = pl.program_id(0)
    at_start = (i % seq_tiles) == 0
    at_end = (i % seq_tiles) == seq_tiles - 1

    def _rms(xv):
        ms = jnp.mean(xv * xv, axis=-1, keepdims=True)
        return (xv * lax.rsqrt(ms + RMS_EPS) * nm_ref[...]).astype(jnp.bfloat16)

    hb_ref[0:HALO, :] = _rms(xp_ref[...])
    hb_ref[HALO:HALO + TM, :] = _rms(xc_ref[...])
    hb_ref[HALO + TM:TE, :] = _rms(xn_ref[...])

    rows = lax.broadcasted_iota(jnp.int32, (TE, 1), 0)
    lo = jnp.where(at_start, HALO, 0)
    hi = jnp.where(at_end, HALO + TM, TE)
    valid = jnp.logical_and(rows >= lo, rows < hi)

    def proj(r0, r1, c0, width):
        return jnp.dot(hb_ref[r0:r1, :], win_ref[:, c0:c0 + width],
                       preferred_element_type=jnp.float32)

    for c in range(D // CW):
        c0 = c * CW
        xa = proj(0, TE, C_XA + c0, CW)
        cc = proj(0, TE, C_CA + c0, CW)
        t = jnp.where(valid, xa * cc, 0.0)
        conv = (ca_ref[0:1, c0:c0 + CW] * t[HALO - 1:HALO - 1 + TM]
                + ca_ref[1:2, c0:c0 + CW] * t[HALO:HALO + TM]
                + ca_ref[2:3, c0:c0 + CW] * t[HALO + 1:HALO + 1 + TM])
        ba = proj(HALO, HALO + TM, C_BA + c0, CW)
        cv_ref[:, c0:c0 + CW] = (ba * conv).astype(jnp.bfloat16)
    for c in range(D // CW):
        c0 = c * CW
        ya = jnp.dot(cv_ref[...], wa_ref[:, c0:c0 + CW], preferred_element_type=jnp.float32)
        ga = _sigmoid(proj(HALO, HALO + TM, C_G + c0, CW))
        mg_ref[:, c0:c0 + CW] = ga * ya

    for c in range(D // CW):
        c0 = c * CW
        val = proj(0, TE, C_VB + c0, CW)
        gate = proj(0, TE, C_GB + c0, CW)
        zq = zq_ref.at[c % 2]
        zq[...] = jnp.where(valid, val * _sigmoid(gate), 0.0)
        for rb in range(TM // RB):
            r0 = rb * RB
            for lt in range(CW // 128):
                l0 = lt * 128
                acc = None
                for s in range(8):
                    part = None
                    for q in range(4):
                        k = 8 * q + s - (HALO - CONV_B // 2)
                        if 0 <= k < CONV_B:
                            term = (cb_ref[k:k + 1, c0 + l0:c0 + l0 + 128]
                                    * zq[r0 + 8 * q:r0 + 8 * q + RB + 8, l0:l0 + 128])
                            part = term if part is None else part + term
                    shifted = part[s:s + RB]
                    acc = shifted if acc is None else acc + shifted
                xo_ref[r0:r0 + RB, c0 + l0:c0 + l0 + 128] = acc
    zc = _layer_norm(xo_ref[...], lnbg_ref[...], lnbb_ref[...])
    cv_ref[...] = (zc * _sigmoid(zc)).astype(jnp.bfloat16)
    for c in range(D // CW):
        c0 = c * CW
        yb = jnp.dot(cv_ref[...], wb_ref[:, c0:c0 + CW], preferred_element_type=jnp.float32)
        gb = _sigmoid(proj(HALO, HALO + TM, C_G + D + c0, CW))
        mg_ref[:, c0:c0 + CW] += gb * yb

    for c in range(D // CW):
        c0 = c * CW
        v = _gelu_tanh(proj(HALO, HALO + TM, C_UV + D + c0, CW))
        xo_ref[:, c0:c0 + CW] = v
    vn = _layer_norm(xo_ref[...], lncg_ref[...], lncb_ref[...])
    cv_ref[...] = vn.astype(jnp.bfloat16)
    for c in range(D // CW):
        c0 = c * CW
        u = _gelu_tanh(proj(HALO, HALO + TM, C_UV + c0, CW))
        for n in range(TM // CHUNK):
            for hh in range(CW // CHUNK):
                h = c * (CW // CHUNK) + hh
                sv = jnp.dot(ws_ref[h], cv_ref[n * CHUNK:(n + 1) * CHUNK, h * CHUNK:(h + 1) * CHUNK],
                             preferred_element_type=jnp.float32) + bsb_ref[h]
                xo_ref[n * CHUNK:(n + 1) * CHUNK, h * CHUNK:(h + 1) * CHUNK] = (
                    u[n * CHUNK:(n + 1) * CHUNK, hh * CHUNK:(hh + 1) * CHUNK] * sv)
    cv_ref[...] = xo_ref[...].astype(jnp.bfloat16)
    for c in range(D // CW):
        c0 = c * CW
        yc = jnp.dot(cv_ref[...], wc_ref[:, c0:c0 + CW], preferred_element_type=jnp.float32)
        gc = _sigmoid(proj(HALO, HALO + TM, C_G + 2 * D + c0, CW))
        mg_ref[:, c0:c0 + CW] += gc * yc

    cv_ref[...] = mg_ref[...].astype(jnp.bfloat16)
    for c in range(D // CW):
        c0 = c * CW
        xo_ref[:, c0:c0 + CW] = xc_ref[:, c0:c0 + CW] + jnp.dot(
            cv_ref[...], wo_ref[:, c0:c0 + CW], preferred_element_type=jnp.float32)
    xnew = xo_ref[...]
    ms = jnp.mean(xnew * xnew, axis=-1, keepdims=True)
    h2 = xnew * lax.rsqrt(ms + RMS_EPS) * nf_ref[...]
    h2_ref[...] = _pack_bf16_pairs(h2)
    h_hi = h2.astype(jnp.bfloat16)
    h_lo = (h2 - h_hi.astype(jnp.float32)).astype(jnp.bfloat16)
    dot32 = functools.partial(jnp.dot, preferred_element_type=jnp.float32)
    lg_ref[...] = (dot32(h_hi, wrh_ref[...]) + dot32(h_lo, wrh_ref[...])
                   + dot32(h_hi, wrl_ref[...]) + dot32(h_lo, wrl_ref[...]) + br_ref[...])


def _layer_spec(layer, shape):
    nd = len(shape)
    return pl.BlockSpec((None,) + tuple(shape), lambda i, _n=nd: (layer,) + (0,) * _n,
                        pipeline_mode=pl.Buffered(1))


def _mixer(layer, x2, seq_len, nm, win, ca, wa, cb, lnbg, lnbb, wb, lncg, lncb, ws, bsb, wc, wo, nf,
           wrh, wrl, br):
    T = x2.shape[0]
    n_tiles = T // TM
    hb = TM // HALO
    last_halo = T // HALO - 1
    ls = functools.partial(_layer_spec, layer)
    in_specs = [
        pl.BlockSpec((HALO, D), lambda i: (jnp.maximum(i * hb - 1, 0), 0)),
        pl.BlockSpec((TM, D), lambda i: (i, 0)),
        pl.BlockSpec((HALO, D), lambda i: (jnp.minimum((i + 1) * hb, last_halo), 0)),
        ls((1, D)), ls((D, D_IN)), ls((CONV_A, D)), ls((D, D)),
        ls((CONV_B, D)), ls((1, D)), ls((1, D)), ls((D, D)),
        ls((1, D)), ls((1, D)), ls((N_HEADS_C, CHUNK, CHUNK)),
        ls((N_HEADS_C, CHUNK, CHUNK)), ls((D, D)), ls((D, D)),
        ls((1, D)), ls((D, NLOG)), ls((D, NLOG)), ls((1, NLOG)),
    ]
    out_specs = [
        pl.BlockSpec((TM, D), lambda i: (i, 0)),
        pl.BlockSpec((TM, DP), lambda i: (i, 0)),
        pl.BlockSpec((TM, NLOG), lambda i: (i, 0)),
    ]
    return pl.pallas_call(
        functools.partial(_mixer_kernel, seq_len // TM),
        grid=(n_tiles,),
        in_specs=in_specs,
        out_specs=out_specs,
        out_shape=[jax.ShapeDtypeStruct((T, D), jnp.float32),
                   jax.ShapeDtypeStruct((T, DP), jnp.uint32),
                   jax.ShapeDtypeStruct((T, NLOG), jnp.float32)],
        scratch_shapes=[pltpu.VMEM((TE, D), jnp.bfloat16),
                        pltpu.VMEM((2, TE, CW), jnp.float32),
                        pltpu.VMEM((TM, D), jnp.bfloat16),
                        pltpu.VMEM((TM, D), jnp.float32)],
        compiler_params=pltpu.CompilerParams(dimension_semantics=("arbitrary",),
                                             vmem_limit_bytes=VMEM_LIMIT),
        name="mixer",
    )(x2, x2, x2, nm, win, ca, wa, cb, lnbg, lnbb, wb, lncg, lncb, ws, bsb, wc, wo, nf, wrh, wrl, br)


def _route_kernel(lg_ref, idx_ref, wgt_ref, cnt_ref, carry_ref):
    i = pl.program_id(0)

    @pl.when(i == 0)
    def _():
        carry_ref[...] = jnp.zeros_like(carry_ref)

    lt = lg_ref[...].T
    g = [lt[j:j + 1, :] for j in range(N_GROUPS)]
    gmax = jnp.maximum(jnp.maximum(g[0], g[1]), jnp.maximum(g[2], g[3]))
    gidx = jnp.where(g[0] == gmax, 0.0, jnp.where(g[1] == gmax, 1.0, jnp.where(g[2] == gmax, 2.0, 3.0)))
    gsum = sum(jnp.exp(gj - gmax) for gj in g)
    g_p = 1.0 / gsum

    sel = lt[E_OFF + 3 * EPG:E_OFF + 4 * EPG, :]
    for j in (2, 1, 0):
        sel = jnp.where(gidx == float(j), lt[E_OFF + j * EPG:E_OFF + (j + 1) * EPG, :], sel)
    rid = lax.broadcasted_iota(jnp.int32, (EPG, TR), 0).astype(jnp.float32)
    m1 = jnp.max(sel, axis=0, keepdims=True)
    i1 = jnp.min(jnp.where(sel == m1, rid, float(EPG)), axis=0, keepdims=True)
    rest = jnp.where(rid == i1, -jnp.inf, sel)
    m2 = jnp.max(rest, axis=0, keepdims=True)
    i2 = jnp.min(jnp.where(rest == m2, rid, float(EPG)), axis=0, keepdims=True)
    e2x = jnp.exp(m2 - m1)
    den = 1.0 + e2x
    w1 = (1.0 / den) * g_p
    w2 = (e2x / den) * g_p
    e1 = gidx * float(EPG) + i1
    e2 = gidx * float(EPG) + i2

    eid = lax.broadcasted_iota(jnp.int32, (N_EXPERTS, TR), 0).astype(jnp.float32)
    oh1 = (eid == e1).astype(jnp.float32)
    oh2 = (eid == e2).astype(jnp.float32)
    oh = oh1 + oh2
    tr = lax.broadcasted_iota(jnp.int32, (TR, TR), 0)
    tc = lax.broadcasted_iota(jnp.int32, (TR, TR), 1)
    upper = (tr < tc).astype(jnp.bfloat16)
    before = jnp.dot(oh.astype(jnp.bfloat16), upper, preferred_element_type=jnp.float32)
    base = before + carry_ref[:, 0:1]
    r1 = jnp.sum(oh1 * base, axis=0, keepdims=True)
    r2 = jnp.sum(oh2 * base, axis=0, keepdims=True)
    carry_ref[...] = carry_ref[...] + jnp.sum(oh, axis=1, keepdims=True)

    idx_ref[...] = jnp.zeros_like(idx_ref)
    idx_ref[0:1, :] = e1.astype(jnp.int32)
    idx_ref[1:2, :] = e2.astype(jnp.int32)
    idx_ref[2:3, :] = r1.astype(jnp.int32)
    idx_ref[3:4, :] = r2.astype(jnp.int32)
    wgt_ref[...] = jnp.zeros_like(wgt_ref)
    wgt_ref[0:1, :] = w1
    wgt_ref[1:2, :] = w2
    cnt_ref[...] = carry_ref[...]


def _route(logits):
    T = logits.shape[0]
    return pl.pallas_call(
        _route_kernel,
        grid=(T // TR,),
        in_specs=[pl.BlockSpec((TR, NLOG), lambda i: (i, 0))],
        out_specs=[pl.BlockSpec((8, TR), lambda i: (0, i)),
                   pl.BlockSpec((8, TR), lambda i: (0, i)),
                   pl.BlockSpec((N_EXPERTS, 128), lambda i: (0, 0))],
        out_shape=[jax.ShapeDtypeStruct((8, T), jnp.int32),
                   jax.ShapeDtypeStruct((8, T), jnp.float32),
                   jax.ShapeDtypeStruct((N_EXPERTS, 128), jnp.float32)],
        scratch_shapes=[pltpu.VMEM((N_EXPERTS, 128), jnp.float32)],
        compiler_params=pltpu.CompilerParams(dimension_semantics=("arbitrary",)),
        name="route",
    )(logits)


def _dispatch_kernel(n_blocks, lastblk_ref, nu_ref, dest_ref, h_ref, xb_ref, zero_ref, sem, zsem):
    i = pl.program_id(0)

    @pl.when(i == 0)
    def _():
        zero_ref[...] = jnp.zeros_like(zero_ref)

        def zero_copy(blk):
            return pltpu.make_async_copy(
                zero_ref, xb_ref.at[pl.ds(pl.multiple_of(blk * BM, BM), BM), :], zsem)

        def for_each_zero_block(fn):
            def per_expert(e, carry):
                @pl.when(lastblk_ref[e] >= 0)
                def _():
                    fn(zero_copy(lastblk_ref[e]))
                return carry

            def per_unused(b, carry):
                fn(zero_copy(b))
                return carry

            lax.fori_loop(0, N_EXPERTS, per_expert, 0)
            lax.fori_loop(nu_ref[0], n_blocks, per_unused, 0)

        for_each_zero_block(lambda cp: cp.start())
        for_each_zero_block(lambda cp: cp.wait())

    def start(t, carry):
        for k in range(2):
            pltpu.make_async_copy(h_ref.at[pl.ds(t, 1), :],
                                  xb_ref.at[pl.ds(dest_ref[k, t], 1), :], sem).start(priority=k)
        return carry

    lax.fori_loop(0, TD, start, 0, unroll=8)
    for k in range(2):
        pltpu.make_async_copy(h_ref, xb_ref.at[pl.ds(0, TD), :], sem).wait()


def _dispatch(h2, dest, lastblk, n_used, n_blocks):
    T = h2.shape[0]
    grid_spec = pltpu.PrefetchScalarGridSpec(
        num_scalar_prefetch=2,
        grid=(T // TD,),
        in_specs=[pl.BlockSpec((2, TD), lambda i, lb, nu: (0, i), memory_space=pltpu.SMEM),
                  pl.BlockSpec((TD, DP), lambda i, lb, nu: (i, 0))],
        out_specs=pl.BlockSpec(memory_space=pl.ANY),
        scratch_shapes=[pltpu.VMEM((BM, DP), jnp.uint32),
                        pltpu.SemaphoreType.DMA(()), pltpu.SemaphoreType.DMA(())],
    )
    return pl.pallas_call(
        functools.partial(_dispatch_kernel, n_blocks),
        grid_spec=grid_spec,
        out_shape=jax.ShapeDtypeStruct((n_blocks * BM, DP), jnp.uint32),
        compiler_params=pltpu.CompilerParams(dimension_semantics=("arbitrary",)),
        name="dispatch",
    )(lastblk, n_used, dest, h2)


def _expert_kernel(be_ref, nu_ref, xb_ref, w1_ref, w3_ref, w2_ref, yb_ref, w1b_ref, w3b_ref, w2b_ref):
    b = pl.program_id(0)
    used = b < nu_ref[0]
    new_expert = jnp.logical_or(b == 0, be_ref[b] != be_ref[jnp.maximum(b - 1, 0)])

    @pl.when(jnp.logical_and(used, new_expert))
    def _():
        w1b_ref[...] = w1_ref[...].astype(jnp.bfloat16)
        w3b_ref[...] = w3_ref[...].astype(jnp.bfloat16)
        w2b_ref[...] = w2_ref[...].astype(jnp.bfloat16)

    @pl.when(used)
    def _():
        x_hi, x_lo = (v.astype(jnp.bfloat16) for v in _unpack_bf16_pairs(xb_ref[...]))
        dot32 = functools.partial(jnp.dot, preferred_element_type=jnp.float32)
        a = dot32(x_hi, w1b_ref[0:DP, :]) + dot32(x_lo, w1b_ref[DP:D, :])
        g = dot32(x_hi, w3b_ref[0:DP, :]) + dot32(x_lo, w3b_ref[DP:D, :])
        hmid = (a * _sigmoid(a) * g).astype(jnp.bfloat16)
        yb_ref[...] = _pack_bf16_pairs(dot32(hmid, w2b_ref[...]))

    @pl.when(jnp.logical_not(used))
    def _():
        yb_ref[...] = jnp.zeros_like(yb_ref)


def _experts(layer, block_expert, n_used, xb, w1, w3, w2):
    n_slots = xb.shape[0]
    n_blocks = n_slots // BM

    def row_map(b, be, nu):
        return (jnp.minimum(b, nu[0] - 1), 0)

    def out_map(b, be, nu):
        return (b, 0)

    def w_map(b, be, nu):
        return (layer, be[b], 0, 0)

    grid_spec = pltpu.PrefetchScalarGridSpec(
        num_scalar_prefetch=2,
        grid=(n_blocks,),
        in_specs=[pl.BlockSpec((BM, DP), row_map),
                  pl.BlockSpec((None, None, D, D_FF), w_map),
                  pl.BlockSpec((None, None, D, D_FF), w_map),
                  pl.BlockSpec((None, None, D_FF, D), w_map)],
        out_specs=pl.BlockSpec((BM, DP), out_map),
        scratch_shapes=[pltpu.VMEM((D, D_FF), jnp.bfloat16),
                        pltpu.VMEM((D, D_FF), jnp.bfloat16),
                        pltpu.VMEM((D_FF, D), jnp.bfloat16)],
    )
    return pl.pallas_call(
        _expert_kernel,
        grid_spec=grid_spec,
        out_shape=jax.ShapeDtypeStruct((n_slots, DP), jnp.uint32),
        compiler_params=pltpu.CompilerParams(dimension_semantics=("arbitrary",),
                                             vmem_limit_bytes=VMEM_LIMIT),
        name="experts",
    )(block_expert, n_used, xb, w1, w3, w2)


def _combine_kernel(final, dest_ref, x_ref, w_ref, nrm_ref, yb_ref, o_ref, rows_ref, sem):
    def start(t, carry):
        for k in range(2):
            pltpu.make_async_copy(yb_ref.at[pl.ds(dest_ref[k, t], 1), :],
                                  rows_ref.at[k, pl.ds(t, 1), :], sem).start(priority=k)
        return carry

    lax.fori_loop(0, TD, start, 0, unroll=8)
    for k in range(2):
        pltpu.make_async_copy(yb_ref.at[pl.ds(0, TD), :], rows_ref.at[k], sem).wait()
    w = w_ref[...].T
    hi0, lo0 = _unpack_bf16_pairs(rows_ref[0])
    hi1, lo1 = _unpack_bf16_pairs(rows_ref[1])
    y_hi = x_ref[:, 0:DP] + w[:, 0:1] * hi0 + w[:, 1:2] * hi1
    y_lo = x_ref[:, DP:D] + w[:, 0:1] * lo0 + w[:, 1:2] * lo1
    if final:
        ms = (jnp.sum(y_hi * y_hi, axis=-1, keepdims=True)
              + jnp.sum(y_lo * y_lo, axis=-1, keepdims=True)) * (1.0 / D)
        scale = lax.rsqrt(ms + RMS_EPS)
        y_hi = y_hi * scale * nrm_ref[:, 0:DP]
        y_lo = y_lo * scale * nrm_ref[:, DP:D]
    o_ref[:, 0:DP] = y_hi
    o_ref[:, DP:D] = y_lo


def _combine(dest, x2, wgt, nrm, yb, final):
    T = x2.shape[0]
    return pl.pallas_call(
        functools.partial(_combine_kernel, final),
        grid=(T // TD,),
        in_specs=[pl.BlockSpec((2, TD), lambda i: (0, i), memory_space=pltpu.SMEM),
                  pl.BlockSpec((TD, D), lambda i: (i, 0)),
                  pl.BlockSpec((8, TD), lambda i: (0, i)),
                  pl.BlockSpec((1, D), lambda i: (0, 0)),
                  pl.BlockSpec(memory_space=pl.ANY)],
        out_specs=pl.BlockSpec((TD, D), lambda i: (i, 0)),
        out_shape=jax.ShapeDtypeStruct((T, D), jnp.float32),
        scratch_shapes=[pltpu.VMEM((2, TD, DP), jnp.uint32), pltpu.SemaphoreType.DMA(())],
        compiler_params=pltpu.CompilerParams(dimension_semantics=("arbitrary",)),
        name="combine",
    )(dest, x2, wgt, nrm, yb)


def _moe(layer, x2, h2, logits, w1, w3, w2, nrm, final):
    T = x2.shape[0]
    idx, wgt, cnt = _route(logits)
    counts = cnt[:, 0].astype(jnp.int32)
    padded = (counts + BM - 1) // BM * BM
    padded_end = jnp.cumsum(padded)
    padded_start = padded_end - padded
    eids = jnp.arange(N_EXPERTS, dtype=jnp.int32)[:, None]
    start_of = lambda e: jnp.sum(jnp.where(e[None, :] == eids, padded_start[:, None], 0), axis=0)
    dest = jnp.stack([start_of(idx[0]) + idx[2], start_of(idx[1]) + idx[3]])
    n_blocks = (2 * T) // BM + N_EXPERTS
    block_start = jnp.arange(n_blocks, dtype=jnp.int32) * BM
    block_expert = jnp.minimum(
        jnp.sum((padded_end[None, :] <= block_start[:, None]).astype(jnp.int32), axis=1),
        N_EXPERTS - 1)
    n_used = (padded_end[-1] // BM).reshape(1)
    lastblk = jnp.where(counts > 0, padded_end // BM - 1, -1)
    xb = _dispatch(h2, dest, lastblk, n_used, n_blocks)
    yb = _experts(layer, block_expert, n_used, xb, w1, w3, w2)
    return _combine(dest, x2, wgt, nrm, yb, final)


def kernel(x, norm_mix, w_in, conv_a, w_a_out, conv_b, ln_b_g, ln_b_b, w_b_out, ln_c_g, ln_c_b,
           w_s, b_s, w_c_out, w_o, norm_ffn, w_group, b_group, w_router, b_router, w1, w3, w2,
           norm_final):
    bsz, seq, d = x.shape
    depth = norm_mix.shape[0]
    bf = jnp.bfloat16
    x2 = x.reshape(bsz * seq, d)
    wr = jnp.zeros((depth, d, NLOG), jnp.float32)
    wr = wr.at[:, :, 0:N_GROUPS].set(w_group).at[:, :, E_OFF:E_OFF + N_EXPERTS].set(w_router)
    wr_hi = wr.astype(bf)
    wr_lo = (wr - wr_hi.astype(jnp.float32)).astype(bf)
    br = jnp.zeros((depth, 1, NLOG), jnp.float32)
    br = br.at[:, 0, 0:N_GROUPS].set(b_group).at[:, 0, E_OFF:E_OFF + N_EXPERTS].set(b_router)
    bsb = jnp.broadcast_to(b_s[:, :, :, None], (depth, N_HEADS_C, CHUNK, CHUNK))
    row = lambda p: p[:, None, :]
    mixer_params = (row(norm_mix), w_in.astype(bf), conv_a, w_a_out.astype(bf), conv_b,
                    row(ln_b_g), row(ln_b_b), w_b_out.astype(bf), row(ln_c_g), row(ln_c_b),
                    w_s.astype(bf), bsb, w_c_out.astype(bf), w_o.astype(bf), row(norm_ffn), wr_hi, wr_lo, br)
    for l in range(depth):
        x2, h2, logits = _mixer(l, x2, seq, *mixer_params)
        x2 = _moe(l, x2, h2, logits, w1, w3, w2, norm_final[None], l == depth - 1)
    return x2.reshape(bsz, seq, d)
```

```python
import functools

import jax
import jax.numpy as jnp
from jax import lax
from jax.experimental import pallas as pl
from jax.experimental.pallas import tpu as pltpu

D = 1024
DP = D // 2
N_HEADS_C = 8
CHUNK = 128
CONV_A = 3
CONV_B = 31
N_GROUPS = 4
EPG = 8
N_EXPERTS = N_GROUPS * EPG
D_FF = 512
RMS_EPS = 1e-6
LN_EPS = 1e-5

C_XA, C_BA, C_CA, C_VB, C_GB, C_UV, C_G = 0, 1024, 2048, 3072, 4096, 5120, 7168
D_IN = 10240

TM = 512
HALO = 16
TE = TM + 2 * HALO
CW = 256
RB = 64
NLOG = 128
E_OFF = 8

TR = 512
BM = 256
TD = 256

VMEM_LIMIT = 60 * 1024 * 1024


def _sigmoid(x):
    return 0.5 * (jnp.tanh(0.5 * x) + 1.0)


def _gelu_tanh(x):
    return 0.5 * x * (1.0 + jnp.tanh(0.7978845608028654 * (x + 0.044715 * (x * x * x))))


def _pack_bf16_pairs(x):
    c = x.shape[1] // 2
    as_bits = lambda v: lax.bitcast_convert_type(v.astype(jnp.bfloat16).astype(jnp.float32), jnp.uint32)
    return as_bits(x[:, :c]) | (as_bits(x[:, c:]) >> 16)


def _unpack_bf16_pairs(p):
    hi = lax.bitcast_convert_type(p & jnp.uint32(0xFFFF0000), jnp.float32)
    lo = lax.bitcast_convert_type(p << 16, jnp.float32)
    return hi, lo


def _layer_norm(x, g, b):
    mu = jnp.mean(x, axis=-1, keepdims=True)
    xc = x - mu
    var = jnp.mean(xc * xc, axis=-1, keepdims=True)
    return xc * lax.rsqrt(var + LN_EPS) * g + b


def _mixer_kernel(seq_tiles,
                  xp_ref, xc_ref, xn_ref, nm_ref, win_ref, ca_ref, wa_ref, cb_ref,
                  lnbg_ref, lnbb_ref, wb_ref, lncg_ref, lncb_ref, ws_ref, bsb_ref, wc_ref,
                  wo_ref, nf_ref, wrh_ref, wrl_ref, br_ref,
                  xo_ref, h2_ref, lg_ref,
                  hb_ref, zq_ref, cv_ref, mg_ref):
    i = pl.program_id(0)
    at_start = (i % seq_tiles) == 0
    at_end = (i % seq_tiles) == seq_tiles - 1

    def _rms(xv):
        ms = jnp.mean(xv * xv, axis=-1, keepdims=True)
        return (xv * lax.rsqrt(ms + RMS_EPS) * nm_ref[...]).astype(jnp.bfloat16)

    hb_ref[0:HALO, :] = _rms(xp_ref[...])
    hb_ref[HALO:HALO + TM, :] = _rms(xc_ref[...])
    hb_ref[HALO + TM:TE, :] = _rms(xn_ref[...])

    rows = lax.broadcasted_iota(jnp.int32, (TE, 1), 0)
    lo = jnp.where(at_start, HALO, 0)
    hi = jnp.where(at_end, HALO + TM, TE)
    valid = jnp.logical_and(rows >= lo, rows < hi)

    def proj(r0, r1, c0, width):
        return jnp.dot(hb_ref[r0:r1, :], win_ref[:, c0:c0 + width],
                       preferred_element_type=jnp.float32)

    for c in range(D // CW):
        c0 = c * CW
        xa = proj(0, TE, C_XA + c0, CW)
        cc = proj(0, TE, C_CA + c0, CW)
        t = jnp.where(valid, xa * cc, 0.0)
        conv = (ca_ref[0:1, c0:c0 + CW] * t[HALO - 1:HALO - 1 + TM]
                + ca_ref[1:2, c0:c0 + CW] * t[HALO:HALO + TM]
                + ca_ref[2:3, c0:c0 + CW] * t[HALO + 1:HALO + 1 + TM])
        ba = proj(HALO, HALO + TM, C_BA + c0, CW)
        cv_ref[:, c0:c0 + CW] = (ba * conv).astype(jnp.bfloat16)
    for c in range(D // CW):
        c0 = c * CW
        ya = jnp.dot(cv_ref[...], wa_ref[:, c0:c0 + CW], preferred_element_type=jnp.float32)
        ga = _sigmoid(proj(HALO, HALO + TM, C_G + c0, CW))
        mg_ref[:, c0:c0 + CW] = ga * ya

    for c in range(D // CW):
        c0 = c * CW
        val = proj(0, TE, C_VB + c0, CW)
        gate = proj(0, TE, C_GB + c0, CW)
        zq = zq_ref.at[c % 2]
        zq[...] = jnp.where(valid, val * _sigmoid(gate), 0.0)
        for rb in range(TM // RB):
            r0 = rb * RB
            for lt in range(CW // 128):
                l0 = lt * 128
                acc = None
                for s in range(8):
                    part = None
                    for q in range(4):
                        k = 8 * q + s - (HALO - CONV_B // 2)
                        if 0 <= k < CONV_B:
                            term = (cb_ref[k:k + 1, c0 + l0:c0 + l0 + 128]
                                    * zq[r0 + 8 * q:r0 + 8 * q + RB + 8, l0:l0 + 128])
                            part = term if part is None else part + term
                    shifted = part[s:s + RB]
                    acc = shifted if acc is None else acc + shifted
                xo_ref[r0:r0 + RB, c0 + l0:c0 + l0 + 128] = acc
    zc = _layer_norm(xo_ref[...], lnbg_ref[...], lnbb_ref[...])
    cv_ref[...] = (zc * _sigmoid(zc)).astype(jnp.bfloat16)
    for c in range(D // CW):
        c0 = c * CW
        yb = jnp.dot(cv_ref[...], wb_ref[:, c0:c0 + CW], preferred_element_type=jnp.float32)
        gb = _sigmoid(proj(HALO, HALO + TM, C_G + D + c0, CW))
        mg_ref[:, c0:c0 + CW] += gb * yb

    for c in range(D // CW):
        c0 = c * CW
        v = _gelu_tanh(proj(HALO, HALO + TM, C_UV + D + c0, CW))
        xo_ref[:, c0:c0 + CW] = v
    vn = _layer_norm(xo_ref[...], lncg_ref[...], lncb_ref[...])
    cv_ref[...] = vn.astype(jnp.bfloat16)
    for c in range(D // CW):
        c0 = c * CW
        u = _gelu_tanh(proj(HALO, HALO + TM, C_UV + c0, CW))
        for n in range(TM // CHUNK):
            for hh in range(CW // CHUNK):
                h = c * (CW // CHUNK) + hh
                sv = jnp.dot(ws_ref[h], cv_ref[n * CHUNK:(n + 1) * CHUNK, h * CHUNK:(h + 1) * CHUNK],
                             preferred_element_type=jnp.float32) + bsb_ref[h]
                xo_ref[n * CHUNK:(n + 1) * CHUNK, h * CHUNK:(h + 1) * CHUNK] = (
                    u[n * CHUNK:(n + 1) * CHUNK, hh * CHUNK:(hh + 1) * CHUNK] * sv)
    cv_ref[...] = xo_ref[...].astype(jnp.bfloat16)
    for c in range(D // CW):
        c0 = c * CW
        yc = jnp.dot(cv_ref[...], wc_ref[:, c0:c0 + CW], preferred_element_type=jnp.float32)
        gc = _sigmoid(proj(HALO, HALO + TM, C_G + 2 * D + c0, CW))
        mg_ref[:, c0:c0 + CW] += gc * yc

    cv_ref[...] = mg_ref[...].astype(jnp.bfloat16)
    for c in range(D // CW):
        c0 = c * CW
        xo_ref[:, c0:c0 + CW] = xc_ref[:, c0:c0 + CW] + jnp.dot(
            cv_ref[...], wo_ref[:, c0:c0 + CW], preferred_element_type=jnp.float32)
    xnew = xo_ref[...]
    ms = jnp.mean(xnew * xnew, axis=-1, keepdims=True)
    h2 = xnew * lax.rsqrt(ms + RMS_EPS) * nf_ref[...]
    h2_ref[...] = _pack_bf16_pairs(h2)
    h_hi = h2.astype(jnp.bfloat16)
    h_lo = (h2 - h_hi.astype(jnp.float32)).astype(jnp.bfloat16)
    dot32 = functools.partial(jnp.dot, preferred_element_type=jnp.float32)
    lg_ref[...] = (dot32(h_hi, wrh_ref[...]) + dot32(h_lo, wrh_ref[...])
                   + dot32(h_hi, wrl_ref[...]) + dot32(h_lo, wrl_ref[...]) + br_ref[...])


def _layer_spec(layer, shape):
    nd = len(shape)
    return pl.BlockSpec((None,) + tuple(shape), lambda i, _n=nd: (layer,) + (0,) * _n,
                        pipeline_mode=pl.Buffered(1))


def _mixer(layer, x2, seq_len, nm, win, ca, wa, cb, lnbg, lnbb, wb, lncg, lncb, ws, bsb, wc, wo, nf,
           wrh, wrl, br):
    T = x2.shape[0]
    n_tiles = T // TM
    hb = TM // HALO
    last_halo = T // HALO - 1
    ls = functools.partial(_layer_spec, layer)
    in_specs = [
        pl.BlockSpec((HALO, D), lambda i: (jnp.maximum(i * hb - 1, 0), 0)),
        pl.BlockSpec((TM, D), lambda i: (i, 0)),
        pl.BlockSpec((HALO, D), lambda i: (jnp.minimum((i + 1) * hb, last_halo), 0)),
        ls((1, D)), ls((D, D_IN)), ls((CONV_A, D)), ls((D, D)),
        ls((CONV_B, D)), ls((1, D)), ls((1, D)), ls((D, D)),
        ls((1, D)), ls((1, D)), ls((N_HEADS_C, CHUNK, CHUNK)),
        ls((N_HEADS_C, CHUNK, CHUNK)), ls((D, D)), ls((D, D)),
        ls((1, D)), ls((D, NLOG)), ls((D, NLOG)), ls((1, NLOG)),
    ]
    out_specs = [
        pl.BlockSpec((TM, D), lambda i: (i, 0)),
        pl.BlockSpec((TM, DP), lambda i: (i, 0)),
        pl.BlockSpec((TM, NLOG), lambda i: (i, 0)),
    ]
    return pl.pallas_call(
        functools.partial(_mixer_kernel, seq_len // TM),
        grid=(n_tiles,),
        in_specs=in_specs,
        out_specs=out_specs,
        out_shape=[jax.ShapeDtypeStruct((T, D), jnp.float32),
                   jax.ShapeDtypeStruct((T, DP), jnp.uint32),
                   jax.ShapeDtypeStruct((T, NLOG), jnp.float32)],
        scratch_shapes=[pltpu.VMEM((TE, D), jnp.bfloat16),
                        pltpu.VMEM((2, TE, CW), jnp.float32),
                        pltpu.VMEM((TM, D), jnp.bfloat16),
                        pltpu.VMEM((TM, D), jnp.float32)],
        compiler_params=pltpu.CompilerParams(dimension_semantics=("arbitrary",),
                                             vmem_limit_bytes=VMEM_LIMIT),
        name="mixer",
    )(x2, x2, x2, nm, win, ca, wa, cb, lnbg, lnbb, wb, lncg, lncb, ws, bsb, wc, wo, nf, wrh, wrl, br)


def _route_kernel(lg_ref, idx_ref, wgt_ref, cnt_ref, carry_ref):
    i = pl.program_id(0)

    @pl.when(i == 0)
    def _():
        carry_ref[...] = jnp.zeros_like(carry_ref)

    lt = lg_ref[...].T
    g = [lt[j:j + 1, :] for j in range(N_GROUPS)]
    gmax = jnp.maximum(jnp.maximum(g[0], g[1]), jnp.maximum(g[2], g[3]))
    gidx = jnp.where(g[0] == gmax, 0.0, jnp.where(g[1] == gmax, 1.0, jnp.where(g[2] == gmax, 2.0, 3.0)))
    gsum = sum(jnp.exp(gj - gmax) for gj in g)
    g_p = 1.0 / gsum

    sel = lt[E_OFF + 3 * EPG:E_OFF + 4 * EPG, :]
    for j in (2, 1, 0):
        sel = jnp.where(gidx == float(j), lt[E_OFF + j * EPG:E_OFF + (j + 1) * EPG, :], sel)
    rid = lax.broadcasted_iota(jnp.int32, (EPG, TR), 0).astype(jnp.float32)
    m1 = jnp.max(sel, axis=0, keepdims=True)
    i1 = jnp.min(jnp.where(sel == m1, rid, float(EPG)), axis=0, keepdims=True)
    rest = jnp.where(rid == i1, -jnp.inf, sel)
    m2 = jnp.max(rest, axis=0, keepdims=True)
    i2 = jnp.min(jnp.where(rest == m2, rid, float(EPG)), axis=0, keepdims=True)
    e2x = jnp.exp(m2 - m1)
    den = 1.0 + e2x
    w1 = (1.0 / den) * g_p
    w2 = (e2x / den) * g_p
    e1 = gidx * float(EPG) + i1
    e2 = gidx * float(EPG) + i2

    eid = lax.broadcasted_iota(jnp.int32, (N_EXPERTS, TR), 0).astype(jnp.float32)
    oh1 = (eid == e1).astype(jnp.float32)
    oh2 = (eid == e2).astype(jnp.float32)
    oh = oh1 + oh2
    tr = lax.broadcasted_iota(jnp.int32, (TR, TR), 0)
    tc = lax.broadcasted_iota(jnp.int32, (TR, TR), 1)
    upper = (tr < tc).astype(jnp.bfloat16)
    before = jnp.dot(oh.astype(jnp.bfloat16), upper, preferred_element_type=jnp.float32)
    base = before + carry_ref[:, 0:1]
    r1 = jnp.sum(oh1 * base, axis=0, keepdims=True)
    r2 = jnp.sum(oh2 * base, axis=0, keepdims=True)
    carry_ref[...] = carry_ref[...] + jnp.sum(oh, axis=1, keepdims=True)

    idx_ref[...] = jnp.zeros_like(idx_ref)
    idx_ref[0:1, :] = e1.astype(jnp.int32)
    idx_ref[1:2, :] = e2.astype(jnp.int32)
    idx_ref[2:3, :] = r1.astype(jnp.int32)
    idx_ref[3:4, :] = r2.astype(jnp.int32)
    wgt_ref[...] = jnp.zeros_like(wgt_ref)
    wgt_ref[0:1, :] = w1
    wgt_ref[1:2, :] = w2
    cnt_ref[...] = carry_ref[...]


def _route(logits):
    T = logits.shape[0]
    return pl.pallas_call(
        _route_kernel,
        grid=(T // TR,),
        in_specs=[pl.BlockSpec((TR, NLOG), lambda i: (i, 0))],
        out_specs=[pl.BlockSpec((8, TR), lambda i: (0, i)),
                   pl.BlockSpec((8, TR), lambda i: (0, i)),
                   pl.BlockSpec((N_EXPERTS, 128), lambda i: (0, 0))],
        out_shape=[jax.ShapeDtypeStruct((8, T), jnp.int32),
                   jax.ShapeDtypeStruct((8, T), jnp.float32),
                   jax.ShapeDtypeStruct((N_EXPERTS, 128), jnp.float32)],
        scratch_shapes=[pltpu.VMEM((N_EXPERTS, 128), jnp.float32)],
        compiler_params=pltpu.CompilerParams(dimension_semantics=("arbitrary",)),
        name="route",
    )(logits)


def _dispatch_kernel(n_blocks, lastblk_ref, nu_ref, dest_ref, h_ref, xb_ref, zero_ref, sem, zsem):
    i = pl.program_id(0)

    @pl.when(i == 0)
    def _():
        zero_ref[...] = jnp.zeros_like(zero_ref)

        def zero_copy(blk):
            return pltpu.make_async_copy(
                zero_ref, xb_ref.at[pl.ds(pl.multiple_of(blk * BM, BM), BM), :], zsem)

        def for_each_zero_block(fn):
            def per_expert(e, carry):
                @pl.when(lastblk_ref[e] >= 0)
                def _():
                    fn(zero_copy(lastblk_ref[e]))
                return carry

            def per_unused(b, carry):
                fn(zero_copy(b))
                return carry

            lax.fori_loop(0, N_EXPERTS, per_expert, 0)
            lax.fori_loop(nu_ref[0], n_blocks, per_unused, 0)

        for_each_zero_block(lambda cp: cp.start())
        for_each_zero_block(lambda cp: cp.wait())

    for t in range(TD):
        for k in range(2):
            pltpu.make_async_copy(h_ref.at[pl.ds(t, 1), :],
                                  xb_ref.at[pl.ds(dest_ref[k * TD + t], 1), :],
                                  sem).start(priority=k)
    for k in range(2):
        pltpu.make_async_copy(h_ref, xb_ref.at[pl.ds(0, TD), :], sem).wait()


def _dispatch(h2, dest, lastblk, n_used, n_blocks):
    T = h2.shape[0]
    grid_spec = pltpu.PrefetchScalarGridSpec(
        num_scalar_prefetch=2,
        grid=(T // TD,),
        in_specs=[pl.BlockSpec((2 * TD,), lambda i, lb, nu: (i,), memory_space=pltpu.SMEM),
                  pl.BlockSpec((TD, DP), lambda i, lb, nu: (i, 0))],
        out_specs=pl.BlockSpec(memory_space=pl.ANY),
        scratch_shapes=[pltpu.VMEM((BM, DP), jnp.uint32),
                        pltpu.SemaphoreType.DMA(()), pltpu.SemaphoreType.DMA(())],
    )
    return pl.pallas_call(
        functools.partial(_dispatch_kernel, n_blocks),
        grid_spec=grid_spec,
        out_shape=jax.ShapeDtypeStruct((n_blocks * BM, DP), jnp.uint32),
        compiler_params=pltpu.CompilerParams(dimension_semantics=("arbitrary",)),
        name="dispatch",
    )(lastblk, n_used, dest, h2)


def _expert_kernel(layer, be_ref, nu_ref, nxt_ref, gs_ref, xb_ref, w1_ref, w3_ref, w2_ref, yb_ref,
                   wf1_ref, wf3_ref, wf2_ref, w1b_ref, w3b_ref, w2b_ref, sem):
    b = pl.program_id(0)
    used = b < nu_ref[0]
    new_expert = jnp.logical_or(b == 0, be_ref[b] != be_ref[jnp.maximum(b - 1, 0)])

    def weight_copies(e, s):
        return [pltpu.make_async_copy(src.at[layer, e], dst.at[s], sem.at[s])
                for src, dst in ((w1_ref, wf1_ref), (w3_ref, wf3_ref), (w2_ref, wf2_ref))]

    @pl.when(b == 0)
    def _():
        for cp in weight_copies(be_ref[0], 0):
            cp.start()

    @pl.when(jnp.logical_and(used, new_expert))
    def _():
        s = gs_ref[b]
        for cp in weight_copies(be_ref[b], s):
            cp.wait()

        @pl.when(nxt_ref[b] >= 0)
        def _():
            for cp in weight_copies(nxt_ref[b], 1 - s):
                cp.start()

        w1b_ref[...] = wf1_ref[s].astype(jnp.bfloat16)
        w3b_ref[...] = wf3_ref[s].astype(jnp.bfloat16)
        w2b_ref[...] = wf2_ref[s].astype(jnp.bfloat16)

    @pl.when(used)
    def _():
        x_hi, x_lo = (v.astype(jnp.bfloat16) for v in _unpack_bf16_pairs(xb_ref[...]))
        dot32 = functools.partial(jnp.dot, preferred_element_type=jnp.float32)
        a = dot32(x_hi, w1b_ref[0:DP, :]) + dot32(x_lo, w1b_ref[DP:D, :])
        g = dot32(x_hi, w3b_ref[0:DP, :]) + dot32(x_lo, w3b_ref[DP:D, :])
        hmid = (a * _sigmoid(a) * g).astype(jnp.bfloat16)
        yb_ref[...] = _pack_bf16_pairs(dot32(hmid, w2b_ref[...]))

    @pl.when(jnp.logical_not(used))
    def _():
        yb_ref[...] = jnp.zeros_like(yb_ref)


def _experts(layer, block_expert, n_used, next_expert, group_slot, xb, w1, w3, w2):
    n_slots = xb.shape[0]
    n_blocks = n_slots // BM

    def row_map(b, be, nu, nxt, gs):
        return (jnp.minimum(b, nu[0] - 1), 0)

    def out_map(b, be, nu, nxt, gs):
        return (b, 0)

    grid_spec = pltpu.PrefetchScalarGridSpec(
        num_scalar_prefetch=4,
        grid=(n_blocks,),
        in_specs=[pl.BlockSpec((BM, DP), row_map),
                  pl.BlockSpec(memory_space=pl.ANY),
                  pl.BlockSpec(memory_space=pl.ANY),
                  pl.BlockSpec(memory_space=pl.ANY)],
        out_specs=pl.BlockSpec((BM, DP), out_map),
        scratch_shapes=[pltpu.VMEM((2, D, D_FF), jnp.float32),
                        pltpu.VMEM((2, D, D_FF), jnp.float32),
                        pltpu.VMEM((2, D_FF, D), jnp.float32),
                        pltpu.VMEM((D, D_FF), jnp.bfloat16),
                        pltpu.VMEM((D, D_FF), jnp.bfloat16),
                        pltpu.VMEM((D_FF, D), jnp.bfloat16),
                        pltpu.SemaphoreType.DMA((2,))],
    )
    return pl.pallas_call(
        functools.partial(_expert_kernel, layer),
        grid_spec=grid_spec,
        out_shape=jax.ShapeDtypeStruct((n_slots, DP), jnp.uint32),
        compiler_params=pltpu.CompilerParams(dimension_semantics=("arbitrary",),
                                             vmem_limit_bytes=VMEM_LIMIT),
        name="experts",
    )(block_expert, n_used, next_expert, group_slot, xb, w1, w3, w2)


def _combine_kernel(final, n_tiles, dest_ref, dest_next_ref, x_ref, w_ref, nrm_ref, yb_ref, o_ref,
                    rows_ref, sem):
    i = pl.program_id(0)
    slot = i % 2

    def gather_tile(d_ref, s):
        for t in range(TD):
            for k in range(2):
                pltpu.make_async_copy(yb_ref.at[pl.ds(d_ref[k * TD + t], 1), :],
                                      rows_ref.at[s, k, pl.ds(t, 1), :],
                                      sem.at[s]).start(priority=k)

    @pl.when(i == 0)
    def _():
        gather_tile(dest_ref, 0)

    @pl.when(i + 1 < n_tiles)
    def _():
        gather_tile(dest_next_ref, 1 - slot)

    for k in range(2):
        pltpu.make_async_copy(yb_ref.at[pl.ds(0, TD), :], rows_ref.at[slot, k], sem.at[slot]).wait()
    w = w_ref[...].T
    hi0, lo0 = _unpack_bf16_pairs(rows_ref[slot, 0])
    hi1, lo1 = _unpack_bf16_pairs(rows_ref[slot, 1])
    y_hi = x_ref[:, 0:DP] + w[:, 0:1] * hi0 + w[:, 1:2] * hi1
    y_lo = x_ref[:, DP:D] + w[:, 0:1] * lo0 + w[:, 1:2] * lo1
    if final:
        ms = (jnp.sum(y_hi * y_hi, axis=-1, keepdims=True)
              + jnp.sum(y_lo * y_lo, axis=-1, keepdims=True)) * (1.0 / D)
        scale = lax.rsqrt(ms + RMS_EPS)
        y_hi = y_hi * scale * nrm_ref[:, 0:DP]
        y_lo = y_lo * scale * nrm_ref[:, DP:D]
    o_ref[:, 0:DP] = y_hi
    o_ref[:, DP:D] = y_lo


def _combine(dest, x2, wgt, nrm, yb, final):
    T = x2.shape[0]
    n_tiles = T // TD
    return pl.pallas_call(
        functools.partial(_combine_kernel, final, n_tiles),
        grid=(n_tiles,),
        in_specs=[pl.BlockSpec((2 * TD,), lambda i: (i,), memory_space=pltpu.SMEM),
                  pl.BlockSpec((2 * TD,), lambda i: (jnp.minimum(i + 1, n_tiles - 1),),
                               memory_space=pltpu.SMEM),
                  pl.BlockSpec((TD, D), lambda i: (i, 0)),
                  pl.BlockSpec((8, TD), lambda i: (0, i)),
                  pl.BlockSpec((1, D), lambda i: (0, 0)),
                  pl.BlockSpec(memory_space=pl.ANY)],
        out_specs=pl.BlockSpec((TD, D), lambda i: (i, 0)),
        out_shape=jax.ShapeDtypeStruct((T, D), jnp.float32),
        scratch_shapes=[pltpu.VMEM((2, 2, TD, DP), jnp.uint32), pltpu.SemaphoreType.DMA((2,))],
        compiler_params=pltpu.CompilerParams(dimension_semantics=("arbitrary",)),
        name="combine",
    )(dest, dest, x2, wgt, nrm, yb)


def _moe(layer, x2, h2, logits, w1, w3, w2, nrm, final):
    T = x2.shape[0]
    idx, wgt, cnt = _route(logits)
    counts = cnt[:, 0].astype(jnp.int32)
    padded = (counts + BM - 1) // BM * BM
    padded_end = jnp.cumsum(padded)
    padded_start = padded_end - padded
    eids = jnp.arange(N_EXPERTS, dtype=jnp.int32)[:, None]
    start_of = lambda e: jnp.sum(jnp.where(e[None, :] == eids, padded_start[:, None], 0), axis=0)
    dest = jnp.stack([start_of(idx[0]) + idx[2], start_of(idx[1]) + idx[3]])
    n_blocks = (2 * T) // BM + N_EXPERTS
    block_start = jnp.arange(n_blocks, dtype=jnp.int32) * BM
    block_expert = jnp.minimum(
        jnp.sum((padded_end[None, :] <= block_start[:, None]).astype(jnp.int32), axis=1),
        N_EXPERTS - 1)
    n_used = (padded_end[-1] // BM).reshape(1)
    lastblk = jnp.where(counts > 0, padded_end // BM - 1, -1)
    later = jnp.logical_and(eids.T > eids, (counts > 0)[None, :])
    next_of = jnp.min(jnp.where(later, eids.T, N_EXPERTS), axis=1)
    next_of = jnp.where(next_of < N_EXPERTS, next_of, -1)
    pick = lambda table: jnp.sum(
        jnp.where(block_expert[:, None] == eids.T, table[None, :], 0), axis=1)
    next_expert = pick(next_of)
    group_slot = pick(jnp.cumsum((counts > 0).astype(jnp.int32)) - 1) % 2
    dest_tiles = dest.reshape(2, T // TD, TD).transpose(1, 0, 2).reshape(-1)
    xb = _dispatch(h2, dest_tiles, lastblk, n_used, n_blocks)
    yb = _experts(layer, block_expert, n_used, next_expert, group_slot, xb, w1, w3, w2)
    return _combine(dest_tiles, x2, wgt, nrm, yb, final)


def kernel(x, norm_mix, w_in, conv_a, w_a_out, conv_b, ln_b_g, ln_b_b, w_b_out, ln_c_g, ln_c_b,
           w_s, b_s, w_c_out, w_o, norm_ffn, w_group, b_group, w_router, b_router, w1, w3, w2,
           norm_final):
    bsz, seq, d = x.shape
    depth = norm_mix.shape[0]
    bf = jnp.bfloat16
    x2 = x.reshape(bsz * seq, d)
    wr = jnp.zeros((depth, d, NLOG), jnp.float32)
    wr = wr.at[:, :, 0:N_GROUPS].set(w_group).at[:, :, E_OFF:E_OFF + N_EXPERTS].set(w_router)
    wr_hi = wr.astype(bf)
    wr_lo = (wr - wr_hi.astype(jnp.float32)).astype(bf)
    br = jnp.zeros((depth, 1, NLOG), jnp.float32)
    br = br.at[:, 0, 0:N_GROUPS].set(b_group).at[:, 0, E_OFF:E_OFF + N_EXPERTS].set(b_router)
    bsb = jnp.broadcast_to(b_s[:, :, :, None], (depth, N_HEADS_C, CHUNK, CHUNK))
    row = lambda p: p[:, None, :]
    mixer_params = (row(norm_mix), w_in.astype(bf), conv_a, w_a_out.astype(bf), conv_b,
                    row(ln_b_g), row(ln_b_b), w_b_out.astype(bf), row(ln_c_g), row(ln_c_b),
                    w_s.astype(bf), bsb, w_c_out.astype(bf), w_o.astype(bf), row(norm_ffn), wr_hi, wr_lo, br)
    for l in range(depth):
        x2, h2, logits = _mixer(l, x2, seq, *mixer_params)
        x2 = _moe(l, x2, h2, logits, w1, w3, w2, norm_final[None], l == depth - 1)
    return x2.reshape(bsz, seq, d)
```

```python
import functools

import jax
import jax.numpy as jnp
from jax import lax
from jax.experimental import pallas as pl
from jax.experimental.pallas import tpu as pltpu

D = 1024
DP = D // 2
N_HEADS_C = 8
CHUNK = 128
CONV_A = 3
CONV_B = 31
N_GROUPS = 4
EPG = 8
N_EXPERTS = N_GROUPS * EPG
D_FF = 512
RMS_EPS = 1e-6
LN_EPS = 1e-5

C_XA, C_BA, C_CA, C_VB, C_GB, C_UV, C_G = 0, 1024, 2048, 3072, 4096, 5120, 7168
D_IN = 10240

TM = 512
HALO = 16
TE = TM + 2 * HALO
CW = 256
RB = 64
NLOG = 128
E_OFF = 8

TR = 512
BM = 256
TD = 256

VMEM_LIMIT = 60 * 1024 * 1024


def _sigmoid(x):
    return 0.5 * (jnp.tanh(0.5 * x) + 1.0)


def _gelu_tanh(x):
    return 0.5 * x * (1.0 + jnp.tanh(0.7978845608028654 * (x + 0.044715 * (x * x * x))))


def _pack_bf16_pairs(x):
    c = x.shape[1] // 2
    as_bits = lambda v: lax.bitcast_convert_type(v.astype(jnp.bfloat16).astype(jnp.float32), jnp.uint32)
    return as_bits(x[:, :c]) | (as_bits(x[:, c:]) >> 16)


def _unpack_bf16_pairs(p):
    hi = lax.bitcast_convert_type(p & jnp.uint32(0xFFFF0000), jnp.float32)
    lo = lax.bitcast_convert_type(p << 16, jnp.float32)
    return hi, lo


def _layer_norm(x, g, b):
    mu = jnp.mean(x, axis=-1, keepdims=True)
    xc = x - mu
    var = jnp.mean(xc * xc, axis=-1, keepdims=True)
    return xc * lax.rsqrt(var + LN_EPS) * g + b


def _mixer_kernel(seq_tiles,
                  xp_ref, xc_ref, xn_ref, nm_ref, win_ref, ca_ref, wa_ref, cb_ref,
                  lnbg_ref, lnbb_ref, wb_ref, lncg_ref, lncb_ref, ws_ref, bsb_ref, wc_ref,
                  wo_ref, nf_ref, wrh_ref, wrl_ref, br_ref,
                  xo_ref, h2_ref, lg_ref,
                  hb_ref, zq_ref, cva_ref, cvb_ref, cvn_ref, vst_ref, mg_ref):
    i = pl.program_id(0)
    at_start = (i % seq_tiles) == 0
    at_end = (i % seq_tiles) == seq_tiles - 1

    def _rms(xv):
        ms = jnp.mean(xv * xv, axis=-1, keepdims=True)
        return (xv * lax.rsqrt(ms + RMS_EPS) * nm_ref[...]).astype(jnp.bfloat16)

    hb_ref[0:HALO, :] = _rms(xp_ref[...])
    hb_ref[HALO:HALO + TM, :] = _rms(xc_ref[...])
    hb_ref[HALO + TM:TE, :] = _rms(xn_ref[...])

    rows = lax.broadcasted_iota(jnp.int32, (TE, 1), 0)
    lo = jnp.where(at_start, HALO, 0)
    hi = jnp.where(at_end, HALO + TM, TE)
    valid = jnp.logical_and(rows >= lo, rows < hi)

    def proj(r0, r1, c0, width):
        return jnp.dot(hb_ref[r0:r1, :], win_ref[:, c0:c0 + width],
                       preferred_element_type=jnp.float32)

    def b_proj(c):
        c0 = c * CW
        val = proj(0, TE, C_VB + c0, CW)
        gate = proj(0, TE, C_GB + c0, CW)
        zq_ref[c % 2] = jnp.where(valid, val * _sigmoid(gate), 0.0)

    def a_proj(c):
        c0 = c * CW
        xa = proj(0, TE, C_XA + c0, CW)
        cc = proj(0, TE, C_CA + c0, CW)
        t = jnp.where(valid, xa * cc, 0.0)
        conv = (ca_ref[0:1, c0:c0 + CW] * t[HALO - 1:HALO - 1 + TM]
                + ca_ref[1:2, c0:c0 + CW] * t[HALO:HALO + TM]
                + ca_ref[2:3, c0:c0 + CW] * t[HALO + 1:HALO + 1 + TM])
        ba = proj(HALO, HALO + TM, C_BA + c0, CW)
        cva_ref[:, c0:c0 + CW] = (ba * conv).astype(jnp.bfloat16)

    def c_v_proj(c):
        c0 = c * CW
        vst_ref[:, c0:c0 + CW] = _gelu_tanh(proj(HALO, HALO + TM, C_UV + D + c0, CW))

    def b_conv(c):
        c0 = c * CW
        zq = zq_ref.at[c % 2]
        for rb in range(TM // RB):
            r0 = rb * RB
            for lt in range(CW // 128):
                l0 = lt * 128
                acc = None
                for s in range(8):
                    part = None
                    for q in range(4):
                        k = 8 * q + s - (HALO - CONV_B // 2)
                        if 0 <= k < CONV_B:
                            term = (cb_ref[k:k + 1, c0 + l0:c0 + l0 + 128]
                                    * zq[r0 + 8 * q:r0 + 8 * q + RB + 8, l0:l0 + 128])
                            part = term if part is None else part + term
                    shifted = part[s:s + RB]
                    acc = shifted if acc is None else acc + shifted
                xo_ref[r0:r0 + RB, c0 + l0:c0 + l0 + 128] = acc

    n_chunks = D // CW
    b_proj(0)
    for c in range(n_chunks):
        if c + 1 < n_chunks:
            b_proj(c + 1)
        a_proj(c)
        c_v_proj(c)
        b_conv(c)

    zc = _layer_norm(xo_ref[...], lnbg_ref[...], lnbb_ref[...])
    cvb_ref[...] = (zc * _sigmoid(zc)).astype(jnp.bfloat16)
    for c in range(n_chunks):
        c0 = c * CW
        ya = jnp.dot(cva_ref[...], wa_ref[:, c0:c0 + CW], preferred_element_type=jnp.float32)
        ga = _sigmoid(proj(HALO, HALO + TM, C_G + c0, CW))
        mg_ref[:, c0:c0 + CW] = ga * ya

    vn = _layer_norm(vst_ref[...], lncg_ref[...], lncb_ref[...])
    cvn_ref[...] = vn.astype(jnp.bfloat16)
    for c in range(n_chunks):
        c0 = c * CW
        yb = jnp.dot(cvb_ref[...], wb_ref[:, c0:c0 + CW], preferred_element_type=jnp.float32)
        gb = _sigmoid(proj(HALO, HALO + TM, C_G + D + c0, CW))
        mg_ref[:, c0:c0 + CW] += gb * yb

    for c in range(n_chunks):
        c0 = c * CW
        u = _gelu_tanh(proj(HALO, HALO + TM, C_UV + c0, CW))
        for n in range(TM // CHUNK):
            for hh in range(CW // CHUNK):
                h = c * (CW // CHUNK) + hh
                sv = jnp.dot(ws_ref[h], cvn_ref[n * CHUNK:(n + 1) * CHUNK, h * CHUNK:(h + 1) * CHUNK],
                             preferred_element_type=jnp.float32) + bsb_ref[h]
                cva_ref[n * CHUNK:(n + 1) * CHUNK, h * CHUNK:(h + 1) * CHUNK] = (
                    u[n * CHUNK:(n + 1) * CHUNK, hh * CHUNK:(hh + 1) * CHUNK] * sv
                ).astype(jnp.bfloat16)
    for c in range(n_chunks):
        c0 = c * CW
        yc = jnp.dot(cva_ref[...], wc_ref[:, c0:c0 + CW], preferred_element_type=jnp.float32)
        gc = _sigmoid(proj(HALO, HALO + TM, C_G + 2 * D + c0, CW))
        mg_ref[:, c0:c0 + CW] += gc * yc

    cvb_ref[...] = mg_ref[...].astype(jnp.bfloat16)
    for c in range(n_chunks):
        c0 = c * CW
        xo_ref[:, c0:c0 + CW] = xc_ref[:, c0:c0 + CW] + jnp.dot(
            cvb_ref[...], wo_ref[:, c0:c0 + CW], preferred_element_type=jnp.float32)
    xnew = xo_ref[...]
    ms = jnp.mean(xnew * xnew, axis=-1, keepdims=True)
    h2 = xnew * lax.rsqrt(ms + RMS_EPS) * nf_ref[...]
    h2_ref[...] = _pack_bf16_pairs(h2)
    h_hi = h2.astype(jnp.bfloat16)
    h_lo = (h2 - h_hi.astype(jnp.float32)).astype(jnp.bfloat16)
    dot32 = functools.partial(jnp.dot, preferred_element_type=jnp.float32)
    lg_ref[...] = (dot32(h_hi, wrh_ref[...]) + dot32(h_lo, wrh_ref[...])
                   + dot32(h_hi, wrl_ref[...]) + dot32(h_lo, wrl_ref[...]) + br_ref[...])


def _layer_spec(layer, shape):
    nd = len(shape)
    return pl.BlockSpec((None,) + tuple(shape), lambda i, _n=nd: (layer,) + (0,) * _n,
                        pipeline_mode=pl.Buffered(1))


def _mixer(layer, x2, seq_len, nm, win, ca, wa, cb, lnbg, lnbb, wb, lncg, lncb, ws, bsb, wc, wo, nf,
           wrh, wrl, br):
    T = x2.shape[0]
    n_tiles = T // TM
    hb = TM // HALO
    last_halo = T // HALO - 1
    ls = functools.partial(_layer_spec, layer)
    in_specs = [
        pl.BlockSpec((HALO, D), lambda i: (jnp.maximum(i * hb - 1, 0), 0)),
        pl.BlockSpec((TM, D), lambda i: (i, 0)),
        pl.BlockSpec((HALO, D), lambda i: (jnp.minimum((i + 1) * hb, last_halo), 0)),
        ls((1, D)), ls((D, D_IN)), ls((CONV_A, D)), ls((D, D)),
        ls((CONV_B, D)), ls((1, D)), ls((1, D)), ls((D, D)),
        ls((1, D)), ls((1, D)), ls((N_HEADS_C, CHUNK, CHUNK)),
        ls((N_HEADS_C, CHUNK, CHUNK)), ls((D, D)), ls((D, D)),
        ls((1, D)), ls((D, NLOG)), ls((D, NLOG)), ls((1, NLOG)),
    ]
    out_specs = [
        pl.BlockSpec((TM, D), lambda i: (i, 0)),
        pl.BlockSpec((TM, DP), lambda i: (i, 0)),
        pl.BlockSpec((TM, NLOG), lambda i: (i, 0)),
    ]
    return pl.pallas_call(
        functools.partial(_mixer_kernel, seq_len // TM),
        grid=(n_tiles,),
        in_specs=in_specs,
        out_specs=out_specs,
        out_shape=[jax.ShapeDtypeStruct((T, D), jnp.float32),
                   jax.ShapeDtypeStruct((T, DP), jnp.uint32),
                   jax.ShapeDtypeStruct((T, NLOG), jnp.float32)],
        scratch_shapes=[pltpu.VMEM((TE, D), jnp.bfloat16),
                        pltpu.VMEM((2, TE, CW), jnp.float32),
                        pltpu.VMEM((TM, D), jnp.bfloat16),
                        pltpu.VMEM((TM, D), jnp.bfloat16),
                        pltpu.VMEM((TM, D), jnp.bfloat16),
                        pltpu.VMEM((TM, D), jnp.float32),
                        pltpu.VMEM((TM, D), jnp.float32)],
        compiler_params=pltpu.CompilerParams(dimension_semantics=("arbitrary",),
                                             vmem_limit_bytes=VMEM_LIMIT),
        name="mixer",
    )(x2, x2, x2, nm, win, ca, wa, cb, lnbg, lnbb, wb, lncg, lncb, ws, bsb, wc, wo, nf, wrh, wrl, br)


def _route_kernel(lg_ref, idx_ref, wgt_ref, cnt_ref, carry_ref):
    i = pl.program_id(0)

    @pl.when(i == 0)
    def _():
        carry_ref[...] = jnp.zeros_like(carry_ref)

    lt = lg_ref[...].T
    g = [lt[j:j + 1, :] for j in range(N_GROUPS)]
    gmax = jnp.maximum(jnp.maximum(g[0], g[1]), jnp.maximum(g[2], g[3]))
    gidx = jnp.where(g[0] == gmax, 0.0, jnp.where(g[1] == gmax, 1.0, jnp.where(g[2] == gmax, 2.0, 3.0)))
    gsum = sum(jnp.exp(gj - gmax) for gj in g)
    g_p = 1.0 / gsum

    sel = lt[E_OFF + 3 * EPG:E_OFF + 4 * EPG, :]
    for j in (2, 1, 0):
        sel = jnp.where(gidx == float(j), lt[E_OFF + j * EPG:E_OFF + (j + 1) * EPG, :], sel)
    rid = lax.broadcasted_iota(jnp.int32, (EPG, TR), 0).astype(jnp.float32)
    m1 = jnp.max(sel, axis=0, keepdims=True)
    i1 = jnp.min(jnp.where(sel == m1, rid, float(EPG)), axis=0, keepdims=True)
    rest = jnp.where(rid == i1, -jnp.inf, sel)
    m2 = jnp.max(rest, axis=0, keepdims=True)
    i2 = jnp.min(jnp.where(rest == m2, rid, float(EPG)), axis=0, keepdims=True)
    e2x = jnp.exp(m2 - m1)
    den = 1.0 + e2x
    w1 = (1.0 / den) * g_p
    w2 = (e2x / den) * g_p
    e1 = gidx * float(EPG) + i1
    e2 = gidx * float(EPG) + i2

    eid = lax.broadcasted_iota(jnp.int32, (N_EXPERTS, TR), 0).astype(jnp.float32)
    oh1 = (eid == e1).astype(jnp.float32)
    oh2 = (eid == e2).astype(jnp.float32)
    oh = oh1 + oh2
    tr = lax.broadcasted_iota(jnp.int32, (TR, TR), 0)
    tc = lax.broadcasted_iota(jnp.int32, (TR, TR), 1)
    upper = (tr < tc).astype(jnp.bfloat16)
    before = jnp.dot(oh.astype(jnp.bfloat16), upper, preferred_element_type=jnp.float32)
    base = before + carry_ref[:, 0:1]
    r1 = jnp.sum(oh1 * base, axis=0, keepdims=True)
    r2 = jnp.sum(oh2 * base, axis=0, keepdims=True)
    carry_ref[...] = carry_ref[...] + jnp.sum(oh, axis=1, keepdims=True)

    idx_ref[...] = jnp.zeros_like(idx_ref)
    idx_ref[0:1, :] = e1.astype(jnp.int32)
    idx_ref[1:2, :] = e2.astype(jnp.int32)
    idx_ref[2:3, :] = r1.astype(jnp.int32)
    idx_ref[3:4, :] = r2.astype(jnp.int32)
    wgt_ref[...] = jnp.zeros_like(wgt_ref)
    wgt_ref[0:1, :] = w1
    wgt_ref[1:2, :] = w2
    cnt_ref[...] = carry_ref[...]


def _route(logits):
    T = logits.shape[0]
    return pl.pallas_call(
        _route_kernel,
        grid=(T // TR,),
        in_specs=[pl.BlockSpec((TR, NLOG), lambda i: (i, 0))],
        out_specs=[pl.BlockSpec((8, TR), lambda i: (0, i)),
                   pl.BlockSpec((8, TR), lambda i: (0, i)),
                   pl.BlockSpec((N_EXPERTS, 128), lambda i: (0, 0))],
        out_shape=[jax.ShapeDtypeStruct((8, T), jnp.int32),
                   jax.ShapeDtypeStruct((8, T), jnp.float32),
                   jax.ShapeDtypeStruct((N_EXPERTS, 128), jnp.float32)],
        scratch_shapes=[pltpu.VMEM((N_EXPERTS, 128), jnp.float32)],
        compiler_params=pltpu.CompilerParams(dimension_semantics=("arbitrary",)),
        name="route",
    )(logits)


def _dispatch_kernel(n_blocks, lastblk_ref, nu_ref, dest_ref, h_ref, xb_ref, zero_ref, sem, zsem):
    i = pl.program_id(0)

    @pl.when(i == 0)
    def _():
        zero_ref[...] = jnp.zeros_like(zero_ref)

        def zero_copy(blk):
            return pltpu.make_async_copy(
                zero_ref, xb_ref.at[pl.ds(pl.multiple_of(blk * BM, BM), BM), :], zsem)

        def for_each_zero_block(fn):
            def per_expert(e, carry):
                @pl.when(lastblk_ref[e] >= 0)
                def _():
                    fn(zero_copy(lastblk_ref[e]))
                return carry

            def per_unused(b, carry):
                fn(zero_copy(b))
                return carry

            lax.fori_loop(0, N_EXPERTS, per_expert, 0)
            lax.fori_loop(nu_ref[0], n_blocks, per_unused, 0)

        for_each_zero_block(lambda cp: cp.start())
        for_each_zero_block(lambda cp: cp.wait())

    for t in range(TD):
        for k in range(2):
            pltpu.make_async_copy(h_ref.at[pl.ds(t, 1), :],
                                  xb_ref.at[pl.ds(dest_ref[k * TD + t], 1), :],
                                  sem).start(priority=k)
    for k in range(2):
        pltpu.make_async_copy(h_ref, xb_ref.at[pl.ds(0, TD), :], sem).wait()


def _dispatch(h2, dest, lastblk, n_used, n_blocks):
    T = h2.shape[0]
    grid_spec = pltpu.PrefetchScalarGridSpec(
        num_scalar_prefetch=2,
        grid=(T // TD,),
        in_specs=[pl.BlockSpec((2 * TD,), lambda i, lb, nu: (i,), memory_space=pltpu.SMEM),
                  pl.BlockSpec((TD, DP), lambda i, lb, nu: (i, 0))],
        out_specs=pl.BlockSpec(memory_space=pl.ANY),
        scratch_shapes=[pltpu.VMEM((BM, DP), jnp.uint32),
                        pltpu.SemaphoreType.DMA(()), pltpu.SemaphoreType.DMA(())],
    )
    return pl.pallas_call(
        functools.partial(_dispatch_kernel, n_blocks),
        grid_spec=grid_spec,
        out_shape=jax.ShapeDtypeStruct((n_blocks * BM, DP), jnp.uint32),
        compiler_params=pltpu.CompilerParams(dimension_semantics=("arbitrary",)),
        name="dispatch",
    )(lastblk, n_used, dest, h2)


def _expert_kernel(layer, be_ref, nu_ref, nxt_ref, gs_ref, xb_ref, w1_ref, w3_ref, w2_ref, yb_ref,
                   wf1_ref, wf3_ref, wf2_ref, w1b_ref, w3b_ref, w2b_ref, sem):
    b = pl.program_id(0)
    used = b < nu_ref[0]
    new_expert = jnp.logical_or(b == 0, be_ref[b] != be_ref[jnp.maximum(b - 1, 0)])

    def weight_copies(e, s):
        return [pltpu.make_async_copy(src.at[layer, e], dst.at[s], sem.at[s])
                for src, dst in ((w1_ref, wf1_ref), (w3_ref, wf3_ref), (w2_ref, wf2_ref))]

    @pl.when(b == 0)
    def _():
        for cp in weight_copies(be_ref[0], 0):
            cp.start()

    @pl.when(jnp.logical_and(used, new_expert))
    def _():
        s = gs_ref[b]
        for cp in weight_copies(be_ref[b], s):
            cp.wait()

        @pl.when(nxt_ref[b] >= 0)
        def _():
            for cp in weight_copies(nxt_ref[b], 1 - s):
                cp.start()

        w1b_ref[...] = wf1_ref[s].astype(jnp.bfloat16)
        w3b_ref[...] = wf3_ref[s].astype(jnp.bfloat16)
        w2b_ref[...] = wf2_ref[s].astype(jnp.bfloat16)

    @pl.when(used)
    def _():
        x_hi, x_lo = (v.astype(jnp.bfloat16) for v in _unpack_bf16_pairs(xb_ref[...]))
        dot32 = functools.partial(jnp.dot, preferred_element_type=jnp.float32)
        a = dot32(x_hi, w1b_ref[0:DP, :]) + dot32(x_lo, w1b_ref[DP:D, :])
        g = dot32(x_hi, w3b_ref[0:DP, :]) + dot32(x_lo, w3b_ref[DP:D, :])
        hmid = (a * _sigmoid(a) * g).astype(jnp.bfloat16)
        yb_ref[...] = _pack_bf16_pairs(dot32(hmid, w2b_ref[...]))

    @pl.when(jnp.logical_not(used))
    def _():
        yb_ref[...] = jnp.zeros_like(yb_ref)


def _experts(layer, block_expert, n_used, next_expert, group_slot, xb, w1, w3, w2):
    n_slots = xb.shape[0]
    n_blocks = n_slots // BM

    def row_map(b, be, nu, nxt, gs):
        return (jnp.minimum(b, nu[0] - 1), 0)

    def out_map(b, be, nu, nxt, gs):
        return (b, 0)

    grid_spec = pltpu.PrefetchScalarGridSpec(
        num_scalar_prefetch=4,
        grid=(n_blocks,),
        in_specs=[pl.BlockSpec((BM, DP), row_map),
                  pl.BlockSpec(memory_space=pl.ANY),
                  pl.BlockSpec(memory_space=pl.ANY),
                  pl.BlockSpec(memory_space=pl.ANY)],
        out_specs=pl.BlockSpec((BM, DP), out_map),
        scratch_shapes=[pltpu.VMEM((2, D, D_FF), jnp.float32),
                        pltpu.VMEM((2, D, D_FF), jnp.float32),
                        pltpu.VMEM((2, D_FF, D), jnp.float32),
                        pltpu.VMEM((D, D_FF), jnp.bfloat16),
                        pltpu.VMEM((D, D_FF), jnp.bfloat16),
                        pltpu.VMEM((D_FF, D), jnp.bfloat16),
                        pltpu.SemaphoreType.DMA((2,))],
    )
    return pl.pallas_call(
        functools.partial(_expert_kernel, layer),
        grid_spec=grid_spec,
        out_shape=jax.ShapeDtypeStruct((n_slots, DP), jnp.uint32),
        compiler_params=pltpu.CompilerParams(dimension_semantics=("arbitrary",),
                                             vmem_limit_bytes=VMEM_LIMIT),
        name="experts",
    )(block_expert, n_used, next_expert, group_slot, xb, w1, w3, w2)


def _combine_kernel(final, n_tiles, dest_ref, dest_next_ref, x_ref, w_ref, nrm_ref, yb_ref, o_ref,
                    rows_ref, sem):
    i = pl.program_id(0)
    slot = i % 2

    def gather_tile(d_ref, s):
        for t in range(TD):
            for k in range(2):
                pltpu.make_async_copy(yb_ref.at[pl.ds(d_ref[k * TD + t], 1), :],
                                      rows_ref.at[s, k, pl.ds(t, 1), :],
                                      sem.at[s]).start(priority=k)

    @pl.when(i == 0)
    def _():
        gather_tile(dest_ref, 0)

    @pl.when(i + 1 < n_tiles)
    def _():
        gather_tile(dest_next_ref, 1 - slot)

    for k in range(2):
        pltpu.make_async_copy(yb_ref.at[pl.ds(0, TD), :], rows_ref.at[slot, k], sem.at[slot]).wait()
    w = w_ref[...].T
    hi0, lo0 = _unpack_bf16_pairs(rows_ref[slot, 0])
    hi1, lo1 = _unpack_bf16_pairs(rows_ref[slot, 1])
    y_hi = x_ref[:, 0:DP] + w[:, 0:1] * hi0 + w[:, 1:2] * hi1
    y_lo = x_ref[:, DP:D] + w[:, 0:1] * lo0 + w[:, 1:2] * lo1
    if final:
        ms = (jnp.sum(y_hi * y_hi, axis=-1, keepdims=True)
              + jnp.sum(y_lo * y_lo, axis=-1, keepdims=True)) * (1.0 / D)
        scale = lax.rsqrt(ms + RMS_EPS)
        y_hi = y_hi * scale * nrm_ref[:, 0:DP]
        y_lo = y_lo * scale * nrm_ref[:, DP:D]
    o_ref[:, 0:DP] = y_hi
    o_ref[:, DP:D] = y_lo


def _combine(dest, x2, wgt, nrm, yb, final):
    T = x2.shape[0]
    n_tiles = T // TD
    return pl.pallas_call(
        functools.partial(_combine_kernel, final, n_tiles),
        grid=(n_tiles,),
        in_specs=[pl.BlockSpec((2 * TD,), lambda i: (i,), memory_space=pltpu.SMEM),
                  pl.BlockSpec((2 * TD,), lambda i: (jnp.minimum(i + 1, n_tiles - 1),),
                               memory_space=pltpu.SMEM),
                  pl.BlockSpec((TD, D), lambda i: (i, 0)),
                  pl.BlockSpec((8, TD), lambda i: (0, i)),
                  pl.BlockSpec((1, D), lambda i: (0, 0)),
                  pl.BlockSpec(memory_space=pl.ANY)],
        out_specs=pl.BlockSpec((TD, D), lambda i: (i, 0)),
        out_shape=jax.ShapeDtypeStruct((T, D), jnp.float32),
        scratch_shapes=[pltpu.VMEM((2, 2, TD, DP), jnp.uint32), pltpu.SemaphoreType.DMA((2,))],
        compiler_params=pltpu.CompilerParams(dimension_semantics=("arbitrary",)),
        name="combine",
    )(dest, dest, x2, wgt, nrm, yb)


def _moe(layer, x2, h2, logits, w1, w3, w2, nrm, final):
    T = x2.shape[0]
    idx, wgt, cnt = _route(logits)
    counts = cnt[:, 0].astype(jnp.int32)
    padded = (counts + BM - 1) // BM * BM
    padded_end = jnp.cumsum(padded)
    padded_start = padded_end - padded
    eids = jnp.arange(N_EXPERTS, dtype=jnp.int32)[:, None]
    start_of = lambda e: jnp.sum(jnp.where(e[None, :] == eids, padded_start[:, None], 0), axis=0)
    dest = jnp.stack([start_of(idx[0]) + idx[2], start_of(idx[1]) + idx[3]])
    n_blocks = (2 * T) // BM + N_EXPERTS
    block_start = jnp.arange(n_blocks, dtype=jnp.int32) * BM
    block_expert = jnp.minimum(
        jnp.sum((padded_end[None, :] <= block_start[:, None]).astype(jnp.int32), axis=1),
        N_EXPERTS - 1)
    n_used = (padded_end[-1] // BM).reshape(1)
    lastblk = jnp.where(counts > 0, padded_end // BM - 1, -1)
    later = jnp.logical_and(eids.T > eids, (counts > 0)[None, :])
    next_of = jnp.min(jnp.where(later, eids.T, N_EXPERTS), axis=1)
    next_of = jnp.where(next_of < N_EXPERTS, next_of, -1)
    pick = lambda table: jnp.sum(
        jnp.where(block_expert[:, None] == eids.T, table[None, :], 0), axis=1)
    next_expert = pick(next_of)
    group_slot = pick(jnp.cumsum((counts > 0).astype(jnp.int32)) - 1) % 2
    dest_tiles = dest.reshape(2, T // TD, TD).transpose(1, 0, 2).reshape(-1)
    xb = _dispatch(h2, dest_tiles, lastblk, n_used, n_blocks)
    yb = _experts(layer, block_expert, n_used, next_expert, group_slot, xb, w1, w3, w2)
    return _combine(dest_tiles, x2, wgt, nrm, yb, final)


def kernel(x, norm_mix, w_in, conv_a, w_a_out, conv_b, ln_b_g, ln_b_b, w_b_out, ln_c_g, ln_c_b,
           w_s, b_s, w_c_out, w_o, norm_ffn, w_group, b_group, w_router, b_router, w1, w3, w2,
           norm_final):
    bsz, seq, d = x.shape
    depth = norm_mix.shape[0]
    bf = jnp.bfloat16
    x2 = x.reshape(bsz * seq, d)
    wr = jnp.zeros((depth, d, NLOG), jnp.float32)
    wr = wr.at[:, :, 0:N_GROUPS].set(w_group).at[:, :, E_OFF:E_OFF + N_EXPERTS].set(w_router)
    wr_hi = wr.astype(bf)
    wr_lo = (wr - wr_hi.astype(jnp.float32)).astype(bf)
    br = jnp.zeros((depth, 1, NLOG), jnp.float32)
    br = br.at[:, 0, 0:N_GROUPS].set(b_group).at[:, 0, E_OFF:E_OFF + N_EXPERTS].set(b_router)
    bsb = jnp.broadcast_to(b_s[:, :, :, None], (depth, N_HEADS_C, CHUNK, CHUNK))
    row = lambda p: p[:, None, :]
    mixer_params = (row(norm_mix), w_in.astype(bf), conv_a, w_a_out.astype(bf), conv_b,
                    row(ln_b_g), row(ln_b_b), w_b_out.astype(bf), row(ln_c_g), row(ln_c_b),
                    w_s.astype(bf), bsb, w_c_out.astype(bf), w_o.astype(bf), row(norm_ffn), wr_hi, wr_lo, br)
    for l in range(depth):
        x2, h2, logits = _mixer(l, x2, seq, *mixer_params)
        x2 = _moe(l, x2, h2, logits, w1, w3, w2, norm_final[None], l == depth - 1)
    return x2.reshape(bsz, seq, d)
```

```python
import functools

import jax
import jax.numpy as jnp
from jax import lax
from jax.experimental import pallas as pl
from jax.experimental.pallas import tpu as pltpu

D = 1024
DP = D // 2
N_HEADS_C = 8
CHUNK = 128
CONV_A = 3
CONV_B = 31
N_GROUPS = 4
EPG = 8
N_EXPERTS = N_GROUPS * EPG
D_FF = 512
RMS_EPS = 1e-6
LN_EPS = 1e-5

C_XA, C_BA, C_CA, C_VB, C_GB, C_UV, C_G = 0, 1024, 2048, 3072, 4096, 5120, 7168
D_IN = 10240

TM = 512
HALO = 16
TE = TM + 2 * HALO
CW = 256
RB = 64
NLOG = 128
E_OFF = 8

TR = 512
BM = 256
TD = 256

VMEM_LIMIT = 60 * 1024 * 1024


def _sigmoid(x):
    return 0.5 * (jnp.tanh(0.5 * x) + 1.0)


def _gelu_tanh(x):
    return 0.5 * x * (1.0 + jnp.tanh(0.7978845608028654 * (x + 0.044715 * (x * x * x))))


def _pack_bf16_pairs(x):
    c = x.shape[1] // 2
    as_bits = lambda v: lax.bitcast_convert_type(v.astype(jnp.bfloat16).astype(jnp.float32), jnp.uint32)
    return as_bits(x[:, :c]) | (as_bits(x[:, c:]) >> 16)


def _unpack_bf16_pairs(p):
    hi = lax.bitcast_convert_type(p & jnp.uint32(0xFFFF0000), jnp.float32)
    lo = lax.bitcast_convert_type(p << 16, jnp.float32)
    return hi, lo


def _layer_norm(x, g, b):
    mu = jnp.mean(x, axis=-1, keepdims=True)
    xc = x - mu
    var = jnp.mean(xc * xc, axis=-1, keepdims=True)
    return xc * lax.rsqrt(var + LN_EPS) * g + b


def _mixer_kernel(seq_tiles,
                  xp_ref, xc_ref, xn_ref, nm_ref, win_ref, ca_ref, wa_ref, cb_ref,
                  lnbg_ref, lnbb_ref, wb_ref, lncg_ref, lncb_ref, ws_ref, bsb_ref, wc_ref,
                  wo_ref, nf_ref, wrh_ref, wrl_ref, br_ref,
                  xo_ref, h2_ref, lg_ref,
                  hb_ref, zq_ref, cva_ref, cvb_ref, cvn_ref, gt_ref, vst_ref, mg_ref):
    i = pl.program_id(0)
    at_start = (i % seq_tiles) == 0
    at_end = (i % seq_tiles) == seq_tiles - 1

    def _rms(xv):
        ms = jnp.mean(xv * xv, axis=-1, keepdims=True)
        return (xv * lax.rsqrt(ms + RMS_EPS) * nm_ref[...]).astype(jnp.bfloat16)

    hb_ref[0:HALO, :] = _rms(xp_ref[...])
    hb_ref[HALO:HALO + TM, :] = _rms(xc_ref[...])
    hb_ref[HALO + TM:TE, :] = _rms(xn_ref[...])

    rows = lax.broadcasted_iota(jnp.int32, (TE, 1), 0)
    lo = jnp.where(at_start, HALO, 0)
    hi = jnp.where(at_end, HALO + TM, TE)
    valid = jnp.logical_and(rows >= lo, rows < hi)

    def proj(r0, r1, c0, width):
        return jnp.dot(hb_ref[r0:r1, :], win_ref[:, c0:c0 + width],
                       preferred_element_type=jnp.float32)

    def b_proj(c):
        c0 = c * CW
        val = proj(0, TE, C_VB + c0, CW)
        gate = proj(0, TE, C_GB + c0, CW)
        zq_ref[c % 2] = jnp.where(valid, val * _sigmoid(gate), 0.0)

    def a_proj(c):
        c0 = c * CW
        xa = proj(0, TE, C_XA + c0, CW)
        cc = proj(0, TE, C_CA + c0, CW)
        t = jnp.where(valid, xa * cc, 0.0)
        conv = (ca_ref[0:1, c0:c0 + CW] * t[HALO - 1:HALO - 1 + TM]
                + ca_ref[1:2, c0:c0 + CW] * t[HALO:HALO + TM]
                + ca_ref[2:3, c0:c0 + CW] * t[HALO + 1:HALO + 1 + TM])
        ba = proj(HALO, HALO + TM, C_BA + c0, CW)
        cva_ref[:, c0:c0 + CW] = (ba * conv).astype(jnp.bfloat16)

    def c_v_proj(c):
        c0 = c * CW
        vst_ref[:, c0:c0 + CW] = _gelu_tanh(proj(HALO, HALO + TM, C_UV + D + c0, CW))

    def gate_proj(c):
        c0 = c * CW
        for j in range(3):
            gt_ref[j, :, c0:c0 + CW] = _sigmoid(
                proj(HALO, HALO + TM, C_G + j * D + c0, CW)).astype(jnp.bfloat16)

    def b_conv(c):
        c0 = c * CW
        zq = zq_ref.at[c % 2]
        for rb in range(TM // RB):
            r0 = rb * RB
            for lt in range(CW // 128):
                l0 = lt * 128
                acc = None
                for s in range(8):
                    part = None
                    for q in range(4):
                        k = 8 * q + s - (HALO - CONV_B // 2)
                        if 0 <= k < CONV_B:
                            term = (cb_ref[k:k + 1, c0 + l0:c0 + l0 + 128]
                                    * zq[r0 + 8 * q:r0 + 8 * q + RB + 8, l0:l0 + 128])
                            part = term if part is None else part + term
                    shifted = part[s:s + RB]
                    acc = shifted if acc is None else acc + shifted
                xo_ref[r0:r0 + RB, c0 + l0:c0 + l0 + 128] = acc

    n_chunks = D // CW
    b_proj(0)
    for c in range(n_chunks):
        if c + 1 < n_chunks:
            b_proj(c + 1)
        a_proj(c)
        c_v_proj(c)
        gate_proj(c)
        b_conv(c)

    zc = _layer_norm(xo_ref[...], lnbg_ref[...], lnbb_ref[...])
    cvb_ref[...] = (zc * _sigmoid(zc)).astype(jnp.bfloat16)
    for c in range(n_chunks):
        c0 = c * CW
        ya = jnp.dot(cva_ref[...], wa_ref[:, c0:c0 + CW], preferred_element_type=jnp.float32)
        mg_ref[:, c0:c0 + CW] = gt_ref[0, :, c0:c0 + CW].astype(jnp.float32) * ya

    vn = _layer_norm(vst_ref[...], lncg_ref[...], lncb_ref[...])
    cvn_ref[...] = vn.astype(jnp.bfloat16)
    for c in range(n_chunks):
        c0 = c * CW
        yb = jnp.dot(cvb_ref[...], wb_ref[:, c0:c0 + CW], preferred_element_type=jnp.float32)
        mg_ref[:, c0:c0 + CW] += gt_ref[1, :, c0:c0 + CW].astype(jnp.float32) * yb

    for c in range(n_chunks):
        c0 = c * CW
        u = _gelu_tanh(proj(HALO, HALO + TM, C_UV + c0, CW))
        for n in range(TM // CHUNK):
            for hh in range(CW // CHUNK):
                h = c * (CW // CHUNK) + hh
                sv = jnp.dot(ws_ref[h], cvn_ref[n * CHUNK:(n + 1) * CHUNK, h * CHUNK:(h + 1) * CHUNK],
                             preferred_element_type=jnp.float32) + bsb_ref[h]
                cva_ref[n * CHUNK:(n + 1) * CHUNK, h * CHUNK:(h + 1) * CHUNK] = (
                    u[n * CHUNK:(n + 1) * CHUNK, hh * CHUNK:(hh + 1) * CHUNK] * sv
                ).astype(jnp.bfloat16)
    for c in range(n_chunks):
        c0 = c * CW
        yc = jnp.dot(cva_ref[...], wc_ref[:, c0:c0 + CW], preferred_element_type=jnp.float32)
        mg_ref[:, c0:c0 + CW] += gt_ref[2, :, c0:c0 + CW].astype(jnp.float32) * yc

    cvb_ref[...] = mg_ref[...].astype(jnp.bfloat16)
    for c in range(n_chunks):
        c0 = c * CW
        xo_ref[:, c0:c0 + CW] = xc_ref[:, c0:c0 + CW] + jnp.dot(
            cvb_ref[...], wo_ref[:, c0:c0 + CW], preferred_element_type=jnp.float32)
    xnew = xo_ref[...]
    ms = jnp.mean(xnew * xnew, axis=-1, keepdims=True)
    h2 = xnew * lax.rsqrt(ms + RMS_EPS) * nf_ref[...]
    h2_ref[...] = _pack_bf16_pairs(h2)
    h_hi = h2.astype(jnp.bfloat16)
    h_lo = (h2 - h_hi.astype(jnp.float32)).astype(jnp.bfloat16)
    dot32 = functools.partial(jnp.dot, preferred_element_type=jnp.float32)
    lg_ref[...] = (dot32(h_hi, wrh_ref[...]) + dot32(h_lo, wrh_ref[...])
                   + dot32(h_hi, wrl_ref[...]) + dot32(h_lo, wrl_ref[...]) + br_ref[...])


def _layer_spec(layer, shape):
    nd = len(shape)
    return pl.BlockSpec((None,) + tuple(shape), lambda i, _n=nd: (layer,) + (0,) * _n,
                        pipeline_mode=pl.Buffered(1))


def _mixer(layer, x2, seq_len, nm, win, ca, wa, cb, lnbg, lnbb, wb, lncg, lncb, ws, bsb, wc, wo, nf,
           wrh, wrl, br):
    T = x2.shape[0]
    n_tiles = T // TM
    hb = TM // HALO
    last_halo = T // HALO - 1
    ls = functools.partial(_layer_spec, layer)
    in_specs = [
        pl.BlockSpec((HALO, D), lambda i: (jnp.maximum(i * hb - 1, 0), 0)),
        pl.BlockSpec((TM, D), lambda i: (i, 0)),
        pl.BlockSpec((HALO, D), lambda i: (jnp.minimum((i + 1) * hb, last_halo), 0)),
        ls((1, D)), ls((D, D_IN)), ls((CONV_A, D)), ls((D, D)),
        ls((CONV_B, D)), ls((1, D)), ls((1, D)), ls((D, D)),
        ls((1, D)), ls((1, D)), ls((N_HEADS_C, CHUNK, CHUNK)),
        ls((N_HEADS_C, CHUNK, CHUNK)), ls((D, D)), ls((D, D)),
        ls((1, D)), ls((D, NLOG)), ls((D, NLOG)), ls((1, NLOG)),
    ]
    out_specs = [
        pl.BlockSpec((TM, D), lambda i: (i, 0)),
        pl.BlockSpec((TM, DP), lambda i: (i, 0)),
        pl.BlockSpec((TM, NLOG), lambda i: (i, 0)),
    ]
    return pl.pallas_call(
        functools.partial(_mixer_kernel, seq_len // TM),
        grid=(n_tiles,),
        in_specs=in_specs,
        out_specs=out_specs,
        out_shape=[jax.ShapeDtypeStruct((T, D), jnp.float32),
                   jax.ShapeDtypeStruct((T, DP), jnp.uint32),
                   jax.ShapeDtypeStruct((T, NLOG), jnp.float32)],
        scratch_shapes=[pltpu.VMEM((TE, D), jnp.bfloat16),
                        pltpu.VMEM((2, TE, CW), jnp.float32),
                        pltpu.VMEM((TM, D), jnp.bfloat16),
                        pltpu.VMEM((TM, D), jnp.bfloat16),
                        pltpu.VMEM((TM, D), jnp.bfloat16),
                        pltpu.VMEM((3, TM, D), jnp.bfloat16),
                        pltpu.VMEM((TM, D), jnp.float32),
                        pltpu.VMEM((TM, D), jnp.float32)],
        compiler_params=pltpu.CompilerParams(dimension_semantics=("arbitrary",),
                                             vmem_limit_bytes=VMEM_LIMIT),
        name="mixer",
    )(x2, x2, x2, nm, win, ca, wa, cb, lnbg, lnbb, wb, lncg, lncb, ws, bsb, wc, wo, nf, wrh, wrl, br)


def _route_kernel(lg_ref, idx_ref, wgt_ref, cnt_ref, carry_ref):
    i = pl.program_id(0)

    @pl.when(i == 0)
    def _():
        carry_ref[...] = jnp.zeros_like(carry_ref)

    lt = lg_ref[...].T
    g = [lt[j:j + 1, :] for j in range(N_GROUPS)]
    gmax = jnp.maximum(jnp.maximum(g[0], g[1]), jnp.maximum(g[2], g[3]))
    gidx = jnp.where(g[0] == gmax, 0.0, jnp.where(g[1] == gmax, 1.0, jnp.where(g[2] == gmax, 2.0, 3.0)))
    gsum = sum(jnp.exp(gj - gmax) for gj in g)
    g_p = 1.0 / gsum

    sel = lt[E_OFF + 3 * EPG:E_OFF + 4 * EPG, :]
    for j in (2, 1, 0):
        sel = jnp.where(gidx == float(j), lt[E_OFF + j * EPG:E_OFF + (j + 1) * EPG, :], sel)
    rid = lax.broadcasted_iota(jnp.int32, (EPG, TR), 0).astype(jnp.float32)
    m1 = jnp.max(sel, axis=0, keepdims=True)
    i1 = jnp.min(jnp.where(sel == m1, rid, float(EPG)), axis=0, keepdims=True)
    rest = jnp.where(rid == i1, -jnp.inf, sel)
    m2 = jnp.max(rest, axis=0, keepdims=True)
    i2 = jnp.min(jnp.where(rest == m2, rid, float(EPG)), axis=0, keepdims=True)
    e2x = jnp.exp(m2 - m1)
    den = 1.0 + e2x
    w1 = (1.0 / den) * g_p
    w2 = (e2x / den) * g_p
    e1 = gidx * float(EPG) + i1
    e2 = gidx * float(EPG) + i2

    eid = lax.broadcasted_iota(jnp.int32, (N_EXPERTS, TR), 0).astype(jnp.float32)
    oh1 = (eid == e1).astype(jnp.float32)
    oh2 = (eid == e2).astype(jnp.float32)
    oh = oh1 + oh2
    tr = lax.broadcasted_iota(jnp.int32, (TR, TR), 0)
    tc = lax.broadcasted_iota(jnp.int32, (TR, TR), 1)
    upper = (tr < tc).astype(jnp.bfloat16)
    before = jnp.dot(oh.astype(jnp.bfloat16), upper, preferred_element_type=jnp.float32)
    base = before + carry_ref[:, 0:1]
    r1 = jnp.sum(oh1 * base, axis=0, keepdims=True)
    r2 = jnp.sum(oh2 * base, axis=0, keepdims=True)
    carry_ref[...] = carry_ref[...] + jnp.sum(oh, axis=1, keepdims=True)

    idx_ref[...] = jnp.zeros_like(idx_ref)
    idx_ref[0:1, :] = e1.astype(jnp.int32)
    idx_ref[1:2, :] = e2.astype(jnp.int32)
    idx_ref[2:3, :] = r1.astype(jnp.int32)
    idx_ref[3:4, :] = r2.astype(jnp.int32)
    wgt_ref[...] = jnp.zeros_like(wgt_ref)
    wgt_ref[0:1, :] = w1
    wgt_ref[1:2, :] = w2
    cnt_ref[...] = carry_ref[...]


def _route(logits):
    T = logits.shape[0]
    return pl.pallas_call(
        _route_kernel,
        grid=(T // TR,),
        in_specs=[pl.BlockSpec((TR, NLOG), lambda i: (i, 0))],
        out_specs=[pl.BlockSpec((8, TR), lambda i: (0, i)),
                   pl.BlockSpec((8, TR), lambda i: (0, i)),
                   pl.BlockSpec((N_EXPERTS, 128), lambda i: (0, 0))],
        out_shape=[jax.ShapeDtypeStruct((8, T), jnp.int32),
                   jax.ShapeDtypeStruct((8, T), jnp.float32),
                   jax.ShapeDtypeStruct((N_EXPERTS, 128), jnp.float32)],
        scratch_shapes=[pltpu.VMEM((N_EXPERTS, 128), jnp.float32)],
        compiler_params=pltpu.CompilerParams(dimension_semantics=("arbitrary",)),
        name="route",
    )(logits)


def _dispatch_kernel(n_blocks, lastblk_ref, nu_ref, dest_ref, h_ref, xb_ref, zero_ref, sem, zsem):
    i = pl.program_id(0)

    @pl.when(i == 0)
    def _():
        zero_ref[...] = jnp.zeros_like(zero_ref)

        def zero_copy(blk):
            return pltpu.make_async_copy(
                zero_ref, xb_ref.at[pl.ds(pl.multiple_of(blk * BM, BM), BM), :], zsem)

        def for_each_zero_block(fn):
            def per_expert(e, carry):
                @pl.when(lastblk_ref[e] >= 0)
                def _():
                    fn(zero_copy(lastblk_ref[e]))
                return carry

            def per_unused(b, carry):
                fn(zero_copy(b))
                return carry

            lax.fori_loop(0, N_EXPERTS, per_expert, 0)
            lax.fori_loop(nu_ref[0], n_blocks, per_unused, 0)

        for_each_zero_block(lambda cp: cp.start())
        for_each_zero_block(lambda cp: cp.wait())

    for t in range(TD):
        for k in range(2):
            pltpu.make_async_copy(h_ref.at[pl.ds(t, 1), :],
                                  xb_ref.at[pl.ds(dest_ref[k * TD + t], 1), :],
                                  sem).start(priority=k)
    for k in range(2):
        pltpu.make_async_copy(h_ref, xb_ref.at[pl.ds(0, TD), :], sem).wait()


def _dispatch(h2, dest, lastblk, n_used, n_blocks):
    T = h2.shape[0]
    grid_spec = pltpu.PrefetchScalarGridSpec(
        num_scalar_prefetch=2,
        grid=(T // TD,),
        in_specs=[pl.BlockSpec((2 * TD,), lambda i, lb, nu: (i,), memory_space=pltpu.SMEM),
                  pl.BlockSpec((TD, DP), lambda i, lb, nu: (i, 0))],
        out_specs=pl.BlockSpec(memory_space=pl.ANY),
        scratch_shapes=[pltpu.VMEM((BM, DP), jnp.uint32),
                        pltpu.SemaphoreType.DMA(()), pltpu.SemaphoreType.DMA(())],
    )
    return pl.pallas_call(
        functools.partial(_dispatch_kernel, n_blocks),
        grid_spec=grid_spec,
        out_shape=jax.ShapeDtypeStruct((n_blocks * BM, DP), jnp.uint32),
        compiler_params=pltpu.CompilerParams(dimension_semantics=("arbitrary",)),
        name="dispatch",
    )(lastblk, n_used, dest, h2)


def _expert_kernel(layer, be_ref, nu_ref, nxt_ref, gs_ref, xb_ref, w1_ref, w3_ref, w2_ref, yb_ref,
                   wf1_ref, wf3_ref, wf2_ref, w1b_ref, w3b_ref, w2b_ref, sem):
    b = pl.program_id(0)
    used = b < nu_ref[0]
    new_expert = jnp.logical_or(b == 0, be_ref[b] != be_ref[jnp.maximum(b - 1, 0)])

    def weight_copies(e, s):
        return [pltpu.make_async_copy(src.at[layer, e], dst.at[s], sem.at[s])
                for src, dst in ((w1_ref, wf1_ref), (w3_ref, wf3_ref), (w2_ref, wf2_ref))]

    @pl.when(b == 0)
    def _():
        for cp in weight_copies(be_ref[0], 0):
            cp.start()

    @pl.when(jnp.logical_and(used, new_expert))
    def _():
        s = gs_ref[b]
        for cp in weight_copies(be_ref[b], s):
            cp.wait()

        @pl.when(nxt_ref[b] >= 0)
        def _():
            for cp in weight_copies(nxt_ref[b], 1 - s):
                cp.start()

        w1b_ref[...] = wf1_ref[s].astype(jnp.bfloat16)
        w3b_ref[...] = wf3_ref[s].astype(jnp.bfloat16)
        w2b_ref[...] = wf2_ref[s].astype(jnp.bfloat16)

    @pl.when(used)
    def _():
        x_hi, x_lo = (v.astype(jnp.bfloat16) for v in _unpack_bf16_pairs(xb_ref[...]))
        dot32 = functools.partial(jnp.dot, preferred_element_type=jnp.float32)
        a = dot32(x_hi, w1b_ref[0:DP, :]) + dot32(x_lo, w1b_ref[DP:D, :])
        g = dot32(x_hi, w3b_ref[0:DP, :]) + dot32(x_lo, w3b_ref[DP:D, :])
        hmid = (a * _sigmoid(a) * g).astype(jnp.bfloat16)
        yb_ref[...] = _pack_bf16_pairs(dot32(hmid, w2b_ref[...]))

    @pl.when(jnp.logical_not(used))
    def _():
        yb_ref[...] = jnp.zeros_like(yb_ref)


def _experts(layer, block_expert, n_used, next_expert, group_slot, xb, w1, w3, w2):
    n_slots = xb.shape[0]
    n_blocks = n_slots // BM

    def row_map(b, be, nu, nxt, gs):
        return (jnp.minimum(b, nu[0] - 1), 0)

    def out_map(b, be, nu, nxt, gs):
        return (b, 0)

    grid_spec = pltpu.PrefetchScalarGridSpec(
        num_scalar_prefetch=4,
        grid=(n_blocks,),
        in_specs=[pl.BlockSpec((BM, DP), row_map),
                  pl.BlockSpec(memory_space=pl.ANY),
                  pl.BlockSpec(memory_space=pl.ANY),
                  pl.BlockSpec(memory_space=pl.ANY)],
        out_specs=pl.BlockSpec((BM, DP), out_map),
        scratch_shapes=[pltpu.VMEM((2, D, D_FF), jnp.float32),
                        pltpu.VMEM((2, D, D_FF), jnp.float32),
                        pltpu.VMEM((2, D_FF, D), jnp.float32),
                        pltpu.VMEM((D, D_FF), jnp.bfloat16),
                        pltpu.VMEM((D, D_FF), jnp.bfloat16),
                        pltpu.VMEM((D_FF, D), jnp.bfloat16),
                        pltpu.SemaphoreType.DMA((2,))],
    )
    return pl.pallas_call(
        functools.partial(_expert_kernel, layer),
        grid_spec=grid_spec,
        out_shape=jax.ShapeDtypeStruct((n_slots, DP), jnp.uint32),
        compiler_params=pltpu.CompilerParams(dimension_semantics=("arbitrary",),
                                             vmem_limit_bytes=VMEM_LIMIT),
        name="experts",
    )(block_expert, n_used, next_expert, group_slot, xb, w1, w3, w2)


def _combine_kernel(final, n_tiles, dest_ref, dest_next_ref, x_ref, w_ref, nrm_ref, yb_ref, o_ref,
                    rows_ref, sem):
    i = pl.program_id(0)
    slot = i % 2

    def gather_tile(d_ref, s):
        for t in range(TD):
            for k in range(2):
                pltpu.make_async_copy(yb_ref.at[pl.ds(d_ref[k * TD + t], 1), :],
                                      rows_ref.at[s, k, pl.ds(t, 1), :],
                                      sem.at[s]).start(priority=k)

    @pl.when(i == 0)
    def _():
        gather_tile(dest_ref, 0)

    @pl.when(i + 1 < n_tiles)
    def _():
        gather_tile(dest_next_ref, 1 - slot)

    for k in range(2):
        pltpu.make_async_copy(yb_ref.at[pl.ds(0, TD), :], rows_ref.at[slot, k], sem.at[slot]).wait()
    w = w_ref[...].T
    hi0, lo0 = _unpack_bf16_pairs(rows_ref[slot, 0])
    hi1, lo1 = _unpack_bf16_pairs(rows_ref[slot, 1])
    y_hi = x_ref[:, 0:DP] + w[:, 0:1] * hi0 + w[:, 1:2] * hi1
    y_lo = x_ref[:, DP:D] + w[:, 0:1] * lo0 + w[:, 1:2] * lo1
    if final:
        ms = (jnp.sum(y_hi * y_hi, axis=-1, keepdims=True)
              + jnp.sum(y_lo * y_lo, axis=-1, keepdims=True)) * (1.0 / D)
        scale = lax.rsqrt(ms + RMS_EPS)
        y_hi = y_hi * scale * nrm_ref[:, 0:DP]
        y_lo = y_lo * scale * nrm_ref[:, DP:D]
    o_ref[:, 0:DP] = y_hi
    o_ref[:, DP:D] = y_lo


def _combine(dest, x2, wgt, nrm, yb, final):
    T = x2.shape[0]
    n_tiles = T // TD
    return pl.pallas_call(
        functools.partial(_combine_kernel, final, n_tiles),
        grid=(n_tiles,),
        in_specs=[pl.BlockSpec((2 * TD,), lambda i: (i,), memory_space=pltpu.SMEM),
                  pl.BlockSpec((2 * TD,), lambda i: (jnp.minimum(i + 1, n_tiles - 1),),
                               memory_space=pltpu.SMEM),
                  pl.BlockSpec((TD, D), lambda i: (i, 0)),
                  pl.BlockSpec((8, TD), lambda i: (0, i)),
                  pl.BlockSpec((1, D), lambda i: (0, 0)),
                  pl.BlockSpec(memory_space=pl.ANY)],
        out_specs=pl.BlockSpec((TD, D), lambda i: (i, 0)),
        out_shape=jax.ShapeDtypeStruct((T, D), jnp.float32),
        scratch_shapes=[pltpu.VMEM((2, 2, TD, DP), jnp.uint32), pltpu.SemaphoreType.DMA((2,))],
        compiler_params=pltpu.CompilerParams(dimension_semantics=("arbitrary",)),
        name="combine",
    )(dest, dest, x2, wgt, nrm, yb)


def _moe(layer, x2, h2, logits, w1, w3, w2, nrm, final):
    T = x2.shape[0]
    idx, wgt, cnt = _route(logits)
    counts = cnt[:, 0].astype(jnp.int32)
    padded = (counts + BM - 1) // BM * BM
    padded_end = jnp.cumsum(padded)
    padded_start = padded_end - padded
    eids = jnp.arange(N_EXPERTS, dtype=jnp.int32)[:, None]
    start_of = lambda e: jnp.sum(jnp.where(e[None, :] == eids, padded_start[:, None], 0), axis=0)
    dest = jnp.stack([start_of(idx[0]) + idx[2], start_of(idx[1]) + idx[3]])
    n_blocks = (2 * T) // BM + N_EXPERTS
    block_start = jnp.arange(n_blocks, dtype=jnp.int32) * BM
    block_expert = jnp.minimum(
        jnp.sum((padded_end[None, :] <= block_start[:, None]).astype(jnp.int32), axis=1),
        N_EXPERTS - 1)
    n_used = (padded_end[-1] // BM).reshape(1)
    lastblk = jnp.where(counts > 0, padded_end // BM - 1, -1)
    later = jnp.logical_and(eids.T > eids, (counts > 0)[None, :])
    next_of = jnp.min(jnp.where(later, eids.T, N_EXPERTS), axis=1)
    next_of = jnp.where(next_of < N_EXPERTS, next_of, -1)
    pick = lambda table: jnp.sum(
        jnp.where(block_expert[:, None] == eids.T, table[None, :], 0), axis=1)
    next_expert = pick(next_of)
    group_slot = pick(jnp.cumsum((counts > 0).astype(jnp.int32)) - 1) % 2
    dest_tiles = dest.reshape(2, T // TD, TD).transpose(1, 0, 2).reshape(-1)
    xb = _dispatch(h2, dest_tiles, lastblk, n_used, n_blocks)
    yb = _experts(layer, block_expert, n_used, next_expert, group_slot, xb, w1, w3, w2)
    return _combine(dest_tiles, x2, wgt, nrm, yb, final)


def kernel(x, norm_mix, w_in, conv_a, w_a_out, conv_b, ln_b_g, ln_b_b, w_b_out, ln_c_g, ln_c_b,
           w_s, b_s, w_c_out, w_o, norm_ffn, w_group, b_group, w_router, b_router, w1, w3, w2,
           norm_final):
    bsz, seq, d = x.shape
    depth = norm_mix.shape[0]
    bf = jnp.bfloat16
    x2 = x.reshape(bsz * seq, d)
    wr = jnp.zeros((depth, d, NLOG), jnp.float32)
    wr = wr.at[:, :, 0:N_GROUPS].set(w_group).at[:, :, E_OFF:E_OFF + N_EXPERTS].set(w_router)
    wr_hi = wr.astype(bf)
    wr_lo = (wr - wr_hi.astype(jnp.float32)).astype(bf)
    br = jnp.zeros((depth, 1, NLOG), jnp.float32)
    br = br.at[:, 0, 0:N_GROUPS].set(b_group).at[:, 0, E_OFF:E_OFF + N_EXPERTS].set(b_router)
    bsb = jnp.broadcast_to(b_s[:, :, :, None], (depth, N_HEADS_C, CHUNK, CHUNK))
    row = lambda p: p[:, None, :]
    mixer_params = (row(norm_mix), w_in.astype(bf), conv_a, w_a_out.astype(bf), conv_b,
                    row(ln_b_g), row(ln_b_b), w_b_out.astype(bf), row(ln_c_g), row(ln_c_b),
                    w_s.astype(bf), bsb, w_c_out.astype(bf), w_o.astype(bf), row(norm_ffn), wr_hi, wr_lo, br)
    for l in range(depth):
        x2, h2, logits = _mixer(l, x2, seq, *mixer_params)
        x2 = _moe(l, x2, h2, logits, w1, w3, w2, norm_final[None], l == depth - 1)
    return x2.reshape(bsz, seq, d)
```

```python
import functools

import jax
import jax.numpy as jnp
from jax import lax
from jax.experimental import pallas as pl
from jax.experimental.pallas import tpu as pltpu

D = 1024
DP = D // 2
N_HEADS_C = 8
CHUNK = 128
CONV_A = 3
CONV_B = 31
N_GROUPS = 4
EPG = 8
N_EXPERTS = N_GROUPS * EPG
D_FF = 512
RMS_EPS = 1e-6
LN_EPS = 1e-5

C_XA, C_BA, C_CA, C_VB, C_GB, C_UV, C_G = 0, 1024, 2048, 3072, 4096, 5120, 7168
D_IN = 10240

TM = 512
HALO = 16
TE = TM + 2 * HALO
CW = 256
RB = 64
NLOG = 128
E_OFF = 8

TR = 512
BM = 512
TD = 256

VMEM_LIMIT = 60 * 1024 * 1024


def _sigmoid(x):
    return 0.5 * (jnp.tanh(0.5 * x) + 1.0)


def _gelu_tanh(x):
    return 0.5 * x * (1.0 + jnp.tanh(0.7978845608028654 * (x + 0.044715 * (x * x * x))))


def _pack_bf16_pairs(x):
    c = x.shape[1] // 2
    as_bits = lambda v: lax.bitcast_convert_type(v.astype(jnp.bfloat16).astype(jnp.float32), jnp.uint32)
    return as_bits(x[:, :c]) | (as_bits(x[:, c:]) >> 16)


def _unpack_bf16_pairs(p):
    hi = lax.bitcast_convert_type(p & jnp.uint32(0xFFFF0000), jnp.float32)
    lo = lax.bitcast_convert_type(p << 16, jnp.float32)
    return hi, lo


def _layer_norm(x, g, b):
    mu = jnp.mean(x, axis=-1, keepdims=True)
    xc = x - mu
    var = jnp.mean(xc * xc, axis=-1, keepdims=True)
    return xc * lax.rsqrt(var + LN_EPS) * g + b


def _mixer_kernel(seq_tiles,
                  xp_ref, xc_ref, xn_ref, nm_ref, win_ref, ca_ref, wa_ref, cb_ref,
                  lnbg_ref, lnbb_ref, wb_ref, lncg_ref, lncb_ref, ws_ref, bsb_ref, wc_ref,
                  wo_ref, nf_ref, wrh_ref, wrl_ref, br_ref,
                  xo_ref, h2_ref, lg_ref,
                  hb_ref, zq_ref, cva_ref, cvb_ref, cvn_ref, vst_ref, mg_ref):
    i = pl.program_id(0)
    at_start = (i % seq_tiles) == 0
    at_end = (i % seq_tiles) == seq_tiles - 1

    def _rms(xv):
        ms = jnp.mean(xv * xv, axis=-1, keepdims=True)
        return (xv * lax.rsqrt(ms + RMS_EPS) * nm_ref[...]).astype(jnp.bfloat16)

    hb_ref[0:HALO, :] = _rms(xp_ref[...])
    hb_ref[HALO:HALO + TM, :] = _rms(xc_ref[...])
    hb_ref[HALO + TM:TE, :] = _rms(xn_ref[...])

    rows = lax.broadcasted_iota(jnp.int32, (TE, 1), 0)
    lo = jnp.where(at_start, HALO, 0)
    hi = jnp.where(at_end, HALO + TM, TE)
    valid = jnp.logical_and(rows >= lo, rows < hi)

    def proj(r0, r1, c0, width):
        return jnp.dot(hb_ref[r0:r1, :], win_ref[:, c0:c0 + width],
                       preferred_element_type=jnp.float32)

    def b_proj(c):
        c0 = c * CW
        val = proj(0, TE, C_VB + c0, CW)
        gate = proj(0, TE, C_GB + c0, CW)
        zq_ref[c % 2] = jnp.where(valid, val * _sigmoid(gate), 0.0)

    def a_proj(c):
        c0 = c * CW
        xa = proj(0, TE, C_XA + c0, CW)
        cc = proj(0, TE, C_CA + c0, CW)
        t = jnp.where(valid, xa * cc, 0.0)
        conv = (ca_ref[0:1, c0:c0 + CW] * t[HALO - 1:HALO - 1 + TM]
                + ca_ref[1:2, c0:c0 + CW] * t[HALO:HALO + TM]
                + ca_ref[2:3, c0:c0 + CW] * t[HALO + 1:HALO + 1 + TM])
        ba = proj(HALO, HALO + TM, C_BA + c0, CW)
        cva_ref[:, c0:c0 + CW] = (ba * conv).astype(jnp.bfloat16)

    def c_v_proj(c):
        c0 = c * CW
        vst_ref[:, c0:c0 + CW] = _gelu_tanh(proj(HALO, HALO + TM, C_UV + D + c0, CW))

    def b_conv(c):
        c0 = c * CW
        zq = zq_ref.at[c % 2]
        for rb in range(TM // RB):
            r0 = rb * RB
            for lt in range(CW // 128):
                l0 = lt * 128
                acc = None
                for s in range(8):
                    part = None
                    for q in range(4):
                        k = 8 * q + s - (HALO - CONV_B // 2)
                        if 0 <= k < CONV_B:
                            term = (cb_ref[k:k + 1, c0 + l0:c0 + l0 + 128]
                                    * zq[r0 + 8 * q:r0 + 8 * q + RB + 8, l0:l0 + 128])
                            part = term if part is None else part + term
                    shifted = part[s:s + RB]
                    acc = shifted if acc is None else acc + shifted
                xo_ref[r0:r0 + RB, c0 + l0:c0 + l0 + 128] = acc

    n_chunks = D // CW
    b_proj(0)
    for c in range(n_chunks):
        if c + 1 < n_chunks:
            b_proj(c + 1)
        a_proj(c)
        c_v_proj(c)
        b_conv(c)

    zc = _layer_norm(xo_ref[...], lnbg_ref[...], lnbb_ref[...])
    cvb_ref[...] = (zc * _sigmoid(zc)).astype(jnp.bfloat16)
    for c in range(n_chunks):
        c0 = c * CW
        ya = jnp.dot(cva_ref[...], wa_ref[:, c0:c0 + CW], preferred_element_type=jnp.float32)
        ga = _sigmoid(proj(HALO, HALO + TM, C_G + c0, CW))
        mg_ref[:, c0:c0 + CW] = ga * ya

    vn = _layer_norm(vst_ref[...], lncg_ref[...], lncb_ref[...])
    cvn_ref[...] = vn.astype(jnp.bfloat16)
    for c in range(n_chunks):
        c0 = c * CW
        yb = jnp.dot(cvb_ref[...], wb_ref[:, c0:c0 + CW], preferred_element_type=jnp.float32)
        gb = _sigmoid(proj(HALO, HALO + TM, C_G + D + c0, CW))
        mg_ref[:, c0:c0 + CW] += gb * yb

    for c in range(n_chunks):
        c0 = c * CW
        u = _gelu_tanh(proj(HALO, HALO + TM, C_UV + c0, CW))
        for n in range(TM // CHUNK):
            for hh in range(CW // CHUNK):
                h = c * (CW // CHUNK) + hh
                sv = jnp.dot(ws_ref[h], cvn_ref[n * CHUNK:(n + 1) * CHUNK, h * CHUNK:(h + 1) * CHUNK],
                             preferred_element_type=jnp.float32) + bsb_ref[h]
                cva_ref[n * CHUNK:(n + 1) * CHUNK, h * CHUNK:(h + 1) * CHUNK] = (
                    u[n * CHUNK:(n + 1) * CHUNK, hh * CHUNK:(hh + 1) * CHUNK] * sv
                ).astype(jnp.bfloat16)
    for c in range(n_chunks):
        c0 = c * CW
        yc = jnp.dot(cva_ref[...], wc_ref[:, c0:c0 + CW], preferred_element_type=jnp.float32)
        gc = _sigmoid(proj(HALO, HALO + TM, C_G + 2 * D + c0, CW))
        mg_ref[:, c0:c0 + CW] += gc * yc

    cvb_ref[...] = mg_ref[...].astype(jnp.bfloat16)
    for c in range(n_chunks):
        c0 = c * CW
        xo_ref[:, c0:c0 + CW] = xc_ref[:, c0:c0 + CW] + jnp.dot(
            cvb_ref[...], wo_ref[:, c0:c0 + CW], preferred_element_type=jnp.float32)
    xnew = xo_ref[...]
    ms = jnp.mean(xnew * xnew, axis=-1, keepdims=True)
    h2 = xnew * lax.rsqrt(ms + RMS_EPS) * nf_ref[...]
    h2_ref[...] = _pack_bf16_pairs(h2)
    h_hi = h2.astype(jnp.bfloat16)
    h_lo = (h2 - h_hi.astype(jnp.float32)).astype(jnp.bfloat16)
    dot32 = functools.partial(jnp.dot, preferred_element_type=jnp.float32)
    lg_ref[...] = (dot32(h_hi, wrh_ref[...]) + dot32(h_lo, wrh_ref[...])
                   + dot32(h_hi, wrl_ref[...]) + dot32(h_lo, wrl_ref[...]) + br_ref[...])


def _layer_spec(layer, shape):
    nd = len(shape)
    return pl.BlockSpec((None,) + tuple(shape), lambda i, _n=nd: (layer,) + (0,) * _n,
                        pipeline_mode=pl.Buffered(1))


def _mixer(layer, x2, seq_len, nm, win, ca, wa, cb, lnbg, lnbb, wb, lncg, lncb, ws, bsb, wc, wo, nf,
           wrh, wrl, br):
    T = x2.shape[0]
    n_tiles = T // TM
    hb = TM // HALO
    last_halo = T // HALO - 1
    ls = functools.partial(_layer_spec, layer)
    in_specs = [
        pl.BlockSpec((HALO, D), lambda i: (jnp.maximum(i * hb - 1, 0), 0)),
        pl.BlockSpec((TM, D), lambda i: (i, 0)),
        pl.BlockSpec((HALO, D), lambda i: (jnp.minimum((i + 1) * hb, last_halo), 0)),
        ls((1, D)), ls((D, D_IN)), ls((CONV_A, D)), ls((D, D)),
        ls((CONV_B, D)), ls((1, D)), ls((1, D)), ls((D, D)),
        ls((1, D)), ls((1, D)), ls((N_HEADS_C, CHUNK, CHUNK)),
        ls((N_HEADS_C, CHUNK, CHUNK)), ls((D, D)), ls((D, D)),
        ls((1, D)), ls((D, NLOG)), ls((D, NLOG)), ls((1, NLOG)),
    ]
    out_specs = [
        pl.BlockSpec((TM, D), lambda i: (i, 0)),
        pl.BlockSpec((TM, DP), lambda i: (i, 0)),
        pl.BlockSpec((TM, NLOG), lambda i: (i, 0)),
    ]
    return pl.pallas_call(
        functools.partial(_mixer_kernel, seq_len // TM),
        grid=(n_tiles,),
        in_specs=in_specs,
        out_specs=out_specs,
        out_shape=[jax.ShapeDtypeStruct((T, D), jnp.float32),
                   jax.ShapeDtypeStruct((T, DP), jnp.uint32),
                   jax.ShapeDtypeStruct((T, NLOG), jnp.float32)],
        scratch_shapes=[pltpu.VMEM((TE, D), jnp.bfloat16),
                        pltpu.VMEM((2, TE, CW), jnp.float32),
                        pltpu.VMEM((TM, D), jnp.bfloat16),
                        pltpu.VMEM((TM, D), jnp.bfloat16),
                        pltpu.VMEM((TM, D), jnp.bfloat16),
                        pltpu.VMEM((TM, D), jnp.float32),
                        pltpu.VMEM((TM, D), jnp.float32)],
        compiler_params=pltpu.CompilerParams(dimension_semantics=("arbitrary",),
                                             vmem_limit_bytes=VMEM_LIMIT),
        name="mixer",
    )(x2, x2, x2, nm, win, ca, wa, cb, lnbg, lnbb, wb, lncg, lncb, ws, bsb, wc, wo, nf, wrh, wrl, br)


def _route_kernel(lg_ref, idx_ref, wgt_ref, cnt_ref, carry_ref):
    i = pl.program_id(0)

    @pl.when(i == 0)
    def _():
        carry_ref[...] = jnp.zeros_like(carry_ref)

    lt = lg_ref[...].T
    g = [lt[j:j + 1, :] for j in range(N_GROUPS)]
    gmax = jnp.maximum(jnp.maximum(g[0], g[1]), jnp.maximum(g[2], g[3]))
    gidx = jnp.where(g[0] == gmax, 0.0, jnp.where(g[1] == gmax, 1.0, jnp.where(g[2] == gmax, 2.0, 3.0)))
    gsum = sum(jnp.exp(gj - gmax) for gj in g)
    g_p = 1.0 / gsum

    sel = lt[E_OFF + 3 * EPG:E_OFF + 4 * EPG, :]
    for j in (2, 1, 0):
        sel = jnp.where(gidx == float(j), lt[E_OFF + j * EPG:E_OFF + (j + 1) * EPG, :], sel)
    rid = lax.broadcasted_iota(jnp.int32, (EPG, TR), 0).astype(jnp.float32)
    m1 = jnp.max(sel, axis=0, keepdims=True)
    i1 = jnp.min(jnp.where(sel == m1, rid, float(EPG)), axis=0, keepdims=True)
    rest = jnp.where(rid == i1, -jnp.inf, sel)
    m2 = jnp.max(rest, axis=0, keepdims=True)
    i2 = jnp.min(jnp.where(rest == m2, rid, float(EPG)), axis=0, keepdims=True)
    e2x = jnp.exp(m2 - m1)
    den = 1.0 + e2x
    w1 = (1.0 / den) * g_p
    w2 = (e2x / den) * g_p
    e1 = gidx * float(EPG) + i1
    e2 = gidx * float(EPG) + i2

    eid = lax.broadcasted_iota(jnp.int32, (N_EXPERTS, TR), 0).astype(jnp.float32)
    oh1 = (eid == e1).astype(jnp.float32)
    oh2 = (eid == e2).astype(jnp.float32)
    oh = oh1 + oh2
    tr = lax.broadcasted_iota(jnp.int32, (TR, TR), 0)
    tc = lax.broadcasted_iota(jnp.int32, (TR, TR), 1)
    upper = (tr < tc).astype(jnp.bfloat16)
    before = jnp.dot(oh.astype(jnp.bfloat16), upper, preferred_element_type=jnp.float32)
    base = before + carry_ref[:, 0:1]
    r1 = jnp.sum(oh1 * base, axis=0, keepdims=True)
    r2 = jnp.sum(oh2 * base, axis=0, keepdims=True)
    carry_ref[...] = carry_ref[...] + jnp.sum(oh, axis=1, keepdims=True)

    idx_ref[...] = jnp.zeros_like(idx_ref)
    idx_ref[0:1, :] = e1.astype(jnp.int32)
    idx_ref[1:2, :] = e2.astype(jnp.int32)
    idx_ref[2:3, :] = r1.astype(jnp.int32)
    idx_ref[3:4, :] = r2.astype(jnp.int32)
    wgt_ref[...] = jnp.zeros_like(wgt_ref)
    wgt_ref[0:1, :] = w1
    wgt_ref[1:2, :] = w2
    cnt_ref[...] = carry_ref[...]


def _route(logits):
    T = logits.shape[0]
    return pl.pallas_call(
        _route_kernel,
        grid=(T // TR,),
        in_specs=[pl.BlockSpec((TR, NLOG), lambda i: (i, 0))],
        out_specs=[pl.BlockSpec((8, TR), lambda i: (0, i)),
                   pl.BlockSpec((8, TR), lambda i: (0, i)),
                   pl.BlockSpec((N_EXPERTS, 128), lambda i: (0, 0))],
        out_shape=[jax.ShapeDtypeStruct((8, T), jnp.int32),
                   jax.ShapeDtypeStruct((8, T), jnp.float32),
                   jax.ShapeDtypeStruct((N_EXPERTS, 128), jnp.float32)],
        scratch_shapes=[pltpu.VMEM((N_EXPERTS, 128), jnp.float32)],
        compiler_params=pltpu.CompilerParams(dimension_semantics=("arbitrary",)),
        name="route",
    )(logits)


def _dispatch_kernel(n_blocks, lastblk_ref, nu_ref, dest_ref, h_ref, xb_ref, zero_ref, sem, zsem):
    i = pl.program_id(0)

    @pl.when(i == 0)
    def _():
        zero_ref[...] = jnp.zeros_like(zero_ref)

        def zero_copy(blk):
            return pltpu.make_async_copy(
                zero_ref, xb_ref.at[pl.ds(pl.multiple_of(blk * BM, BM), BM), :], zsem)

        def for_each_zero_block(fn):
            def per_expert(e, carry):
                @pl.when(lastblk_ref[e] >= 0)
                def _():
                    fn(zero_copy(lastblk_ref[e]))
                return carry

            def per_unused(b, carry):
                fn(zero_copy(b))
                return carry

            lax.fori_loop(0, N_EXPERTS, per_expert, 0)
            lax.fori_loop(nu_ref[0], n_blocks, per_unused, 0)

        for_each_zero_block(lambda cp: cp.start())
        for_each_zero_block(lambda cp: cp.wait())

    for t in range(TD):
        for k in range(2):
            pltpu.make_async_copy(h_ref.at[pl.ds(t, 1), :],
                                  xb_ref.at[pl.ds(dest_ref[k * TD + t], 1), :],
                                  sem).start(priority=k)
    for k in range(2):
        pltpu.make_async_copy(h_ref, xb_ref.at[pl.ds(0, TD), :], sem).wait()


def _dispatch(h2, dest, lastblk, n_used, n_blocks):
    T = h2.shape[0]
    grid_spec = pltpu.PrefetchScalarGridSpec(
        num_scalar_prefetch=2,
        grid=(T // TD,),
        in_specs=[pl.BlockSpec((2 * TD,), lambda i, lb, nu: (i,), memory_space=pltpu.SMEM),
                  pl.BlockSpec((TD, DP), lambda i, lb, nu: (i, 0))],
        out_specs=pl.BlockSpec(memory_space=pl.ANY),
        scratch_shapes=[pltpu.VMEM((BM, DP), jnp.uint32),
                        pltpu.SemaphoreType.DMA(()), pltpu.SemaphoreType.DMA(())],
    )
    return pl.pallas_call(
        functools.partial(_dispatch_kernel, n_blocks),
        grid_spec=grid_spec,
        out_shape=jax.ShapeDtypeStruct((n_blocks * BM, DP), jnp.uint32),
        compiler_params=pltpu.CompilerParams(dimension_semantics=("arbitrary",)),
        name="dispatch",
    )(lastblk, n_used, dest, h2)


def _expert_kernel(layer, be_ref, nu_ref, nxt_ref, gs_ref, xb_ref, w1_ref, w3_ref, w2_ref, yb_ref,
                   wf1_ref, wf3_ref, wf2_ref, w1b_ref, w3b_ref, w2b_ref, sem):
    b = pl.program_id(0)
    used = b < nu_ref[0]
    new_expert = jnp.logical_or(b == 0, be_ref[b] != be_ref[jnp.maximum(b - 1, 0)])

    def weight_copies(e, s):
        return [pltpu.make_async_copy(src.at[layer, e], dst.at[s], sem.at[s])
                for src, dst in ((w1_ref, wf1_ref), (w3_ref, wf3_ref), (w2_ref, wf2_ref))]

    @pl.when(b == 0)
    def _():
        for cp in weight_copies(be_ref[0], 0):
            cp.start(priority=1)

    @pl.when(jnp.logical_and(used, new_expert))
    def _():
        s = gs_ref[b]
        for cp in weight_copies(be_ref[b], s):
            cp.wait()

        @pl.when(nxt_ref[b] >= 0)
        def _():
            for cp in weight_copies(nxt_ref[b], 1 - s):
                cp.start(priority=1)

        w1b_ref[...] = wf1_ref[s].astype(jnp.bfloat16)
        w3b_ref[...] = wf3_ref[s].astype(jnp.bfloat16)
        w2b_ref[...] = wf2_ref[s].astype(jnp.bfloat16)

    @pl.when(used)
    def _():
        x_hi, x_lo = (v.astype(jnp.bfloat16) for v in _unpack_bf16_pairs(xb_ref[...]))
        dot32 = functools.partial(jnp.dot, preferred_element_type=jnp.float32)
        a = dot32(x_hi, w1b_ref[0:DP, :]) + dot32(x_lo, w1b_ref[DP:D, :])
        g = dot32(x_hi, w3b_ref[0:DP, :]) + dot32(x_lo, w3b_ref[DP:D, :])
        hmid = (a * _sigmoid(a) * g).astype(jnp.bfloat16)
        yb_ref[...] = _pack_bf16_pairs(dot32(hmid, w2b_ref[...]))

    @pl.when(jnp.logical_not(used))
    def _():
        yb_ref[...] = jnp.zeros_like(yb_ref)


def _experts(layer, block_expert, n_used, next_expert, group_slot, xb, w1, w3, w2):
    n_slots = xb.shape[0]
    n_blocks = n_slots // BM

    def row_map(b, be, nu, nxt, gs):
        return (jnp.minimum(b, nu[0] - 1), 0)

    def out_map(b, be, nu, nxt, gs):
        return (b, 0)

    grid_spec = pltpu.PrefetchScalarGridSpec(
        num_scalar_prefetch=4,
        grid=(n_blocks,),
        in_specs=[pl.BlockSpec((BM, DP), row_map),
                  pl.BlockSpec(memory_space=pl.ANY),
                  pl.BlockSpec(memory_space=pl.ANY),
                  pl.BlockSpec(memory_space=pl.ANY)],
        out_specs=pl.BlockSpec((BM, DP), out_map),
        scratch_shapes=[pltpu.VMEM((2, D, D_FF), jnp.float32),
                        pltpu.VMEM((2, D, D_FF), jnp.float32),
                        pltpu.VMEM((2, D_FF, D), jnp.float32),
                        pltpu.VMEM((D, D_FF), jnp.bfloat16),
                        pltpu.VMEM((D, D_FF), jnp.bfloat16),
                        pltpu.VMEM((D_FF, D), jnp.bfloat16),
                        pltpu.SemaphoreType.DMA((2,))],
    )
    return pl.pallas_call(
        functools.partial(_expert_kernel, layer),
        grid_spec=grid_spec,
        out_shape=jax.ShapeDtypeStruct((n_slots, DP), jnp.uint32),
        compiler_params=pltpu.CompilerParams(dimension_semantics=("arbitrary",),
                                             vmem_limit_bytes=VMEM_LIMIT),
        name="experts",
    )(block_expert, n_used, next_expert, group_slot, xb, w1, w3, w2)


def _combine_kernel(final, n_tiles, dest_ref, dest_next_ref, x_ref, w_ref, nrm_ref, yb_ref, o_ref,
                    rows_ref, sem):
    i = pl.program_id(0)
    slot = i % 2

    def gather_tile(d_ref, s):
        for t in range(TD):
            for k in range(2):
                pltpu.make_async_copy(yb_ref.at[pl.ds(d_ref[k * TD + t], 1), :],
                                      rows_ref.at[s, k, pl.ds(t, 1), :],
                                      sem.at[s]).start(priority=k)

    @pl.when(i == 0)
    def _():
        gather_tile(dest_ref, 0)

    @pl.when(i + 1 < n_tiles)
    def _():
        gather_tile(dest_next_ref, 1 - slot)

    for k in range(2):
        pltpu.make_async_copy(yb_ref.at[pl.ds(0, TD), :], rows_ref.at[slot, k], sem.at[slot]).wait()
    w = w_ref[...].T
    hi0, lo0 = _unpack_bf16_pairs(rows_ref[slot, 0])
    hi1, lo1 = _unpack_bf16_pairs(rows_ref[slot, 1])
    y_hi = x_ref[:, 0:DP] + w[:, 0:1] * hi0 + w[:, 1:2] * hi1
    y_lo = x_ref[:, DP:D] + w[:, 0:1] * lo0 + w[:, 1:2] * lo1
    if final:
        ms = (jnp.sum(y_hi * y_hi, axis=-1, keepdims=True)
              + jnp.sum(y_lo * y_lo, axis=-1, keepdims=True)) * (1.0 / D)
        scale = lax.rsqrt(ms + RMS_EPS)
        y_hi = y_hi * scale * nrm_ref[:, 0:DP]
        y_lo = y_lo * scale * nrm_ref[:, DP:D]
    o_ref[:, 0:DP] = y_hi
    o_ref[:, DP:D] = y_lo


def _combine(dest, x2, wgt, nrm, yb, final):
    T = x2.shape[0]
    n_tiles = T // TD
    return pl.pallas_call(
        functools.partial(_combine_kernel, final, n_tiles),
        grid=(n_tiles,),
        in_specs=[pl.BlockSpec((2 * TD,), lambda i: (i,), memory_space=pltpu.SMEM),
                  pl.BlockSpec((2 * TD,), lambda i: (jnp.minimum(i + 1, n_tiles - 1),),
                               memory_space=pltpu.SMEM),
                  pl.BlockSpec((TD, D), lambda i: (i, 0)),
                  pl.BlockSpec((8, TD), lambda i: (0, i)),
                  pl.BlockSpec((1, D), lambda i: (0, 0)),
                  pl.BlockSpec(memory_space=pl.ANY)],
        out_specs=pl.BlockSpec((TD, D), lambda i: (i, 0)),
        out_shape=jax.ShapeDtypeStruct((T, D), jnp.float32),
        scratch_shapes=[pltpu.VMEM((2, 2, TD, DP), jnp.uint32), pltpu.SemaphoreType.DMA((2,))],
        compiler_params=pltpu.CompilerParams(dimension_semantics=("arbitrary",)),
        name="combine",
    )(dest, dest, x2, wgt, nrm, yb)


def _moe(layer, x2, h2, logits, w1, w3, w2, nrm, final):
    T = x2.shape[0]
    idx, wgt, cnt = _route(logits)
    counts = cnt[:, 0].astype(jnp.int32)
    padded = (counts + BM - 1) // BM * BM
    padded_end = jnp.cumsum(padded)
    padded_start = padded_end - padded
    eids = jnp.arange(N_EXPERTS, dtype=jnp.int32)[:, None]
    start_of = lambda e: jnp.sum(jnp.where(e[None, :] == eids, padded_start[:, None], 0), axis=0)
    dest = jnp.stack([start_of(idx[0]) + idx[2], start_of(idx[1]) + idx[3]])
    n_blocks = (2 * T) // BM + N_EXPERTS
    block_start = jnp.arange(n_blocks, dtype=jnp.int32) * BM
    block_expert = jnp.minimum(
        jnp.sum((padded_end[None, :] <= block_start[:, None]).astype(jnp.int32), axis=1),
        N_EXPERTS - 1)
    n_used = (padded_end[-1] // BM).reshape(1)
    lastblk = jnp.where(counts > 0, padded_end // BM - 1, -1)
    later = jnp.logical_and(eids.T > eids, (counts > 0)[None, :])
    next_of = jnp.min(jnp.where(later, eids.T, N_EXPERTS), axis=1)
    next_of = jnp.where(next_of < N_EXPERTS, next_of, -1)
    pick = lambda table: jnp.sum(
        jnp.where(block_expert[:, None] == eids.T, table[None, :], 0), axis=1)
    next_expert = pick(next_of)
    group_slot = pick(jnp.cumsum((counts > 0).astype(jnp.int32)) - 1) % 2
    dest_tiles = dest.reshape(2, T // TD, TD).transpose(1, 0, 2).reshape(-1)
    xb = _dispatch(h2, dest_tiles, lastblk, n_used, n_blocks)
    yb = _experts(layer, block_expert, n_used, next_expert, group_slot, xb, w1, w3, w2)
    return _combine(dest_tiles, x2, wgt, nrm, yb, final)


def kernel(x, norm_mix, w_in, conv_a, w_a_out, conv_b, ln_b_g, ln_b_b, w_b_out, ln_c_g, ln_c_b,
           w_s, b_s, w_c_out, w_o, norm_ffn, w_group, b_group, w_router, b_router, w1, w3, w2,
           norm_final):
    bsz, seq, d = x.shape
    depth = norm_mix.shape[0]
    bf = jnp.bfloat16
    x2 = x.reshape(bsz * seq, d)
    wr = jnp.zeros((depth, d, NLOG), jnp.float32)
    wr = wr.at[:, :, 0:N_GROUPS].set(w_group).at[:, :, E_OFF:E_OFF + N_EXPERTS].set(w_router)
    wr_hi = wr.astype(bf)
    wr_lo = (wr - wr_hi.astype(jnp.float32)).astype(bf)
    br = jnp.zeros((depth, 1, NLOG), jnp.float32)
    br = br.at[:, 0, 0:N_GROUPS].set(b_group).at[:, 0, E_OFF:E_OFF + N_EXPERTS].set(b_router)
    bsb = jnp.broadcast_to(b_s[:, :, :, None], (depth, N_HEADS_C, CHUNK, CHUNK))
    row = lambda p: p[:, None, :]
    mixer_params = (row(norm_mix), w_in.astype(bf), conv_a, w_a_out.astype(bf), conv_b,
                    row(ln_b_g), row(ln_b_b), w_b_out.astype(bf), row(ln_c_g), row(ln_c_b),
                    w_s.astype(bf), bsb, w_c_out.astype(bf), w_o.astype(bf), row(norm_ffn), wr_hi, wr_lo, br)
    for l in range(depth):
        x2, h2, logits = _mixer(l, x2, seq, *mixer_params)
        x2 = _moe(l, x2, h2, logits, w1, w3, w2, norm_final[None], l == depth - 1)
    return x2.reshape(bsz, seq, d)
```

```python
import functools

import jax
import jax.numpy as jnp
from jax import lax
from jax.experimental import pallas as pl
from jax.experimental.pallas import tpu as pltpu

D = 1024
DP = D // 2
N_HEADS_C = 8
CHUNK = 128
CONV_A = 3
CONV_B = 31
N_GROUPS = 4
EPG = 8
N_EXPERTS = N_GROUPS * EPG
D_FF = 512
RMS_EPS = 1e-6
LN_EPS = 1e-5

C_XA, C_BA, C_CA, C_VB, C_GB, C_UV, C_G = 0, 1024, 2048, 3072, 4096, 5120, 7168
D_IN = 10240

TM = 512
HALO = 16
TE = TM + 2 * HALO
CW = 256
RB = 128
NLOG = 128
E_OFF = 8

TR = 512
BM = 512
TD = 256

VMEM_LIMIT = 60 * 1024 * 1024


def _sigmoid(x):
    return 0.5 * (jnp.tanh(0.5 * x) + 1.0)


def _gelu_tanh(x):
    return 0.5 * x * (1.0 + jnp.tanh(0.7978845608028654 * (x + 0.044715 * (x * x * x))))


def _pack_bf16_pairs(x):
    c = x.shape[1] // 2
    as_bits = lambda v: lax.bitcast_convert_type(v.astype(jnp.bfloat16).astype(jnp.float32), jnp.uint32)
    return as_bits(x[:, :c]) | (as_bits(x[:, c:]) >> 16)


def _unpack_bf16_pairs(p):
    hi = lax.bitcast_convert_type(p & jnp.uint32(0xFFFF0000), jnp.float32)
    lo = lax.bitcast_convert_type(p << 16, jnp.float32)
    return hi, lo


def _layer_norm(x, g, b):
    mu = jnp.mean(x, axis=-1, keepdims=True)
    xc = x - mu
    var = jnp.mean(xc * xc, axis=-1, keepdims=True)
    return xc * lax.rsqrt(var + LN_EPS) * g + b


def _mixer_kernel(seq_tiles,
                  xp_ref, xc_ref, xn_ref, nm_ref, win_ref, ca_ref, wa_ref, cbb_ref,
                  lnbg_ref, lnbb_ref, wb_ref, lncg_ref, lncb_ref, ws_ref, bsb_ref, wc_ref,
                  wo_ref, nf_ref, wrh_ref, wrl_ref, br_ref,
                  xo_ref, h2_ref, lg_ref,
                  hb_ref, zq_ref, zq8_ref, cva_ref, cvb_ref, cvn_ref, vst_ref, mg_ref):
    i = pl.program_id(0)
    at_start = (i % seq_tiles) == 0
    at_end = (i % seq_tiles) == seq_tiles - 1

    def _rms(xv):
        ms = jnp.mean(xv * xv, axis=-1, keepdims=True)
        return (xv * lax.rsqrt(ms + RMS_EPS) * nm_ref[...]).astype(jnp.bfloat16)

    hb_ref[0:HALO, :] = _rms(xp_ref[...])
    hb_ref[HALO:HALO + TM, :] = _rms(xc_ref[...])
    hb_ref[HALO + TM:TE, :] = _rms(xn_ref[...])

    rows = lax.broadcasted_iota(jnp.int32, (TE, 1), 0)
    lo = jnp.where(at_start, HALO, 0)
    hi = jnp.where(at_end, HALO + TM, TE)
    valid = jnp.logical_and(rows >= lo, rows < hi)

    def proj(r0, r1, c0, width):
        return jnp.dot(hb_ref[r0:r1, :], win_ref[:, c0:c0 + width],
                       preferred_element_type=jnp.float32)

    def b_proj(c):
        c0 = c * CW
        val = proj(0, TE, C_VB + c0, CW)
        gate = proj(0, TE, C_GB + c0, CW)
        z = jnp.where(valid, val * _sigmoid(gate), 0.0)
        zq_ref[c % 2] = z.astype(jnp.bfloat16)
        z8 = jnp.concatenate([z[8:TE], jnp.zeros((8, CW), jnp.float32)], axis=0)
        zq8_ref[c % 2] = z8.astype(jnp.bfloat16)

    def a_proj(c):
        c0 = c * CW
        xa = proj(0, TE, C_XA + c0, CW)
        cc = proj(0, TE, C_CA + c0, CW)
        t = jnp.where(valid, xa * cc, 0.0)
        conv = (ca_ref[0:1, c0:c0 + CW] * t[HALO - 1:HALO - 1 + TM]
                + ca_ref[1:2, c0:c0 + CW] * t[HALO:HALO + TM]
                + ca_ref[2:3, c0:c0 + CW] * t[HALO + 1:HALO + 1 + TM])
        ba = proj(HALO, HALO + TM, C_BA + c0, CW)
        cva_ref[:, c0:c0 + CW] = (ba * conv).astype(jnp.bfloat16)

    def c_v_proj(c):
        c0 = c * CW
        vst_ref[:, c0:c0 + CW] = _gelu_tanh(proj(HALO, HALO + TM, C_UV + D + c0, CW))

    def b_conv(c):
        c0 = c * CW
        pk = RB + 16
        for rb in range(TM // RB):
            r0 = rb * RB
            for lt in range(CW // 128):
                l0 = lt * 128
                acc = None
                for s in range(8):
                    part = None
                    for q in range(4):
                        k = 8 * q + s - (HALO - CONV_B // 2)
                        if 0 <= k < CONV_B:
                            src = zq_ref if q % 2 == 0 else zq8_ref
                            row0 = r0 + 8 * (q - q % 2)
                            rows = src[c % 2, row0:row0 + pk, l0:l0 + 128].reshape(pk // 16, 16, 128)
                            term = cbb_ref[k, :, c0 + l0:c0 + l0 + 128][None] * rows
                            part = term if part is None else part + term
                    part = part.reshape(pk, 128).astype(jnp.float32)
                    shifted = part[s:s + RB]
                    acc = shifted if acc is None else acc + shifted
                xo_ref[r0:r0 + RB, c0 + l0:c0 + l0 + 128] = acc

    n_chunks = D // CW
    b_proj(0)
    for c in range(n_chunks):
        if c + 1 < n_chunks:
            b_proj(c + 1)
        a_proj(c)
        c_v_proj(c)
        b_conv(c)

    zc = _layer_norm(xo_ref[...], lnbg_ref[...], lnbb_ref[...])
    cvb_ref[...] = (zc * _sigmoid(zc)).astype(jnp.bfloat16)
    for c in range(n_chunks):
        c0 = c * CW
        ya = jnp.dot(cva_ref[...], wa_ref[:, c0:c0 + CW], preferred_element_type=jnp.float32)
        ga = _sigmoid(proj(HALO, HALO + TM, C_G + c0, CW))
        mg_ref[:, c0:c0 + CW] = ga * ya

    vn = _layer_norm(vst_ref[...], lncg_ref[...], lncb_ref[...])
    cvn_ref[...] = vn.astype(jnp.bfloat16)
    for c in range(n_chunks):
        c0 = c * CW
        yb = jnp.dot(cvb_ref[...], wb_ref[:, c0:c0 + CW], preferred_element_type=jnp.float32)
        gb = _sigmoid(proj(HALO, HALO + TM, C_G + D + c0, CW))
        mg_ref[:, c0:c0 + CW] += gb * yb

    for c in range(n_chunks):
        c0 = c * CW
        u = _gelu_tanh(proj(HALO, HALO + TM, C_UV + c0, CW))
        for n in range(TM // CHUNK):
            for hh in range(CW // CHUNK):
                h = c * (CW // CHUNK) + hh
                sv = jnp.dot(ws_ref[h], cvn_ref[n * CHUNK:(n + 1) * CHUNK, h * CHUNK:(h + 1) * CHUNK],
                             preferred_element_type=jnp.float32) + bsb_ref[h]
                cva_ref[n * CHUNK:(n + 1) * CHUNK, h * CHUNK:(h + 1) * CHUNK] = (
                    u[n * CHUNK:(n + 1) * CHUNK, hh * CHUNK:(hh + 1) * CHUNK] * sv
                ).astype(jnp.bfloat16)
    for c in range(n_chunks):
        c0 = c * CW
        yc = jnp.dot(cva_ref[...], wc_ref[:, c0:c0 + CW], preferred_element_type=jnp.float32)
        gc = _sigmoid(proj(HALO, HALO + TM, C_G + 2 * D + c0, CW))
        mg_ref[:, c0:c0 + CW] += gc * yc

    cvb_ref[...] = mg_ref[...].astype(jnp.bfloat16)
    for c in range(n_chunks):
        c0 = c * CW
        xo_ref[:, c0:c0 + CW] = xc_ref[:, c0:c0 + CW] + jnp.dot(
            cvb_ref[...], wo_ref[:, c0:c0 + CW], preferred_element_type=jnp.float32)
    xnew = xo_ref[...]
    ms = jnp.mean(xnew * xnew, axis=-1, keepdims=True)
    h2 = xnew * lax.rsqrt(ms + RMS_EPS) * nf_ref[...]
    h2_ref[...] = _pack_bf16_pairs(h2)
    h_hi = h2.astype(jnp.bfloat16)
    h_lo = (h2 - h_hi.astype(jnp.float32)).astype(jnp.bfloat16)
    dot32 = functools.partial(jnp.dot, preferred_element_type=jnp.float32)
    lg_ref[...] = (dot32(h_hi, wrh_ref[...]) + dot32(h_lo, wrh_ref[...])
                   + dot32(h_hi, wrl_ref[...]) + dot32(h_lo, wrl_ref[...]) + br_ref[...])


def _layer_spec(layer, shape):
    nd = len(shape)
    return pl.BlockSpec((None,) + tuple(shape), lambda i, _n=nd: (layer,) + (0,) * _n,
                        pipeline_mode=pl.Buffered(1))


def _mixer(layer, x2, seq_len, nm, win, ca, wa, cb, lnbg, lnbb, wb, lncg, lncb, ws, bsb, wc, wo, nf,
           wrh, wrl, br):
    T = x2.shape[0]
    n_tiles = T // TM
    hb = TM // HALO
    last_halo = T // HALO - 1
    ls = functools.partial(_layer_spec, layer)
    in_specs = [
        pl.BlockSpec((HALO, D), lambda i: (jnp.maximum(i * hb - 1, 0), 0)),
        pl.BlockSpec((TM, D), lambda i: (i, 0)),
        pl.BlockSpec((HALO, D), lambda i: (jnp.minimum((i + 1) * hb, last_halo), 0)),
        ls((1, D)), ls((D, D_IN)), ls((CONV_A, D)), ls((D, D)),
        ls((CONV_B, 16, D)), ls((1, D)), ls((1, D)), ls((D, D)),
        ls((1, D)), ls((1, D)), ls((N_HEADS_C, CHUNK, CHUNK)),
        ls((N_HEADS_C, CHUNK, CHUNK)), ls((D, D)), ls((D, D)),
        ls((1, D)), ls((D, NLOG)), ls((D, NLOG)), ls((1, NLOG)),
    ]
    out_specs = [
        pl.BlockSpec((TM, D), lambda i: (i, 0)),
        pl.BlockSpec((TM, DP), lambda i: (i, 0)),
        pl.BlockSpec((TM, NLOG), lambda i: (i, 0)),
    ]
    return pl.pallas_call(
        functools.partial(_mixer_kernel, seq_len // TM),
        grid=(n_tiles,),
        in_specs=in_specs,
        out_specs=out_specs,
        out_shape=[jax.ShapeDtypeStruct((T, D), jnp.float32),
                   jax.ShapeDtypeStruct((T, DP), jnp.uint32),
                   jax.ShapeDtypeStruct((T, NLOG), jnp.float32)],
        scratch_shapes=[pltpu.VMEM((TE, D), jnp.bfloat16),
                        pltpu.VMEM((2, TE, CW), jnp.bfloat16),
                        pltpu.VMEM((2, TE, CW), jnp.bfloat16),
                        pltpu.VMEM((TM, D), jnp.bfloat16),
                        pltpu.VMEM((TM, D), jnp.bfloat16),
                        pltpu.VMEM((TM, D), jnp.bfloat16),
                        pltpu.VMEM((TM, D), jnp.float32),
                        pltpu.VMEM((TM, D), jnp.float32)],
        compiler_params=pltpu.CompilerParams(dimension_semantics=("arbitrary",),
                                             vmem_limit_bytes=VMEM_LIMIT),
        name="mixer",
    )(x2, x2, x2, nm, win, ca, wa, cb, lnbg, lnbb, wb, lncg, lncb, ws, bsb, wc, wo, nf, wrh, wrl, br)


def _route_kernel(lg_ref, idx_ref, wgt_ref, cnt_ref, carry_ref):
    i = pl.program_id(0)

    @pl.when(i == 0)
    def _():
        carry_ref[...] = jnp.zeros_like(carry_ref)

    lt = lg_ref[...].T
    g = [lt[j:j + 1, :] for j in range(N_GROUPS)]
    gmax = jnp.maximum(jnp.maximum(g[0], g[1]), jnp.maximum(g[2], g[3]))
    gidx = jnp.where(g[0] == gmax, 0.0, jnp.where(g[1] == gmax, 1.0, jnp.where(g[2] == gmax, 2.0, 3.0)))
    gsum = sum(jnp.exp(gj - gmax) for gj in g)
    g_p = 1.0 / gsum

    sel = lt[E_OFF + 3 * EPG:E_OFF + 4 * EPG, :]
    for j in (2, 1, 0):
        sel = jnp.where(gidx == float(j), lt[E_OFF + j * EPG:E_OFF + (j + 1) * EPG, :], sel)
    rid = lax.broadcasted_iota(jnp.int32, (EPG, TR), 0).astype(jnp.float32)
    m1 = jnp.max(sel, axis=0, keepdims=True)
    i1 = jnp.min(jnp.where(sel == m1, rid, float(EPG)), axis=0, keepdims=True)
    rest = jnp.where(rid == i1, -jnp.inf, sel)
    m2 = jnp.max(rest, axis=0, keepdims=True)
    i2 = jnp.min(jnp.where(rest == m2, rid, float(EPG)), axis=0, keepdims=True)
    e2x = jnp.exp(m2 - m1)
    den = 1.0 + e2x
    w1 = (1.0 / den) * g_p
    w2 = (e2x / den) * g_p
    e1 = gidx * float(EPG) + i1
    e2 = gidx * float(EPG) + i2

    eid = lax.broadcasted_iota(jnp.int32, (N_EXPERTS, TR), 0).astype(jnp.float32)
    oh1 = (eid == e1).astype(jnp.float32)
    oh2 = (eid == e2).astype(jnp.float32)
    oh = oh1 + oh2
    tr = lax.broadcasted_iota(jnp.int32, (TR, TR), 0)
    tc = lax.broadcasted_iota(jnp.int32, (TR, TR), 1)
    upper = (tr < tc).astype(jnp.bfloat16)
    before = jnp.dot(oh.astype(jnp.bfloat16), upper, preferred_element_type=jnp.float32)
    base = before + carry_ref[:, 0:1]
    r1 = jnp.sum(oh1 * base, axis=0, keepdims=True)
    r2 = jnp.sum(oh2 * base, axis=0, keepdims=True)
    carry_ref[...] = carry_ref[...] + jnp.sum(oh, axis=1, keepdims=True)

    idx_ref[...] = jnp.zeros_like(idx_ref)
    idx_ref[0:1, :] = e1.astype(jnp.int32)
    idx_ref[1:2, :] = e2.astype(jnp.int32)
    idx_ref[2:3, :] = r1.astype(jnp.int32)
    idx_ref[3:4, :] = r2.astype(jnp.int32)
    wgt_ref[...] = jnp.zeros_like(wgt_ref)
    wgt_ref[0:1, :] = w1
    wgt_ref[1:2, :] = w2
    cnt_ref[...] = carry_ref[...]


def _route(logits):
    T = logits.shape[0]
    return pl.pallas_call(
        _route_kernel,
        grid=(T // TR,),
        in_specs=[pl.BlockSpec((TR, NLOG), lambda i: (i, 0))],
        out_specs=[pl.BlockSpec((8, TR), lambda i: (0, i)),
                   pl.BlockSpec((8, TR), lambda i: (0, i)),
                   pl.BlockSpec((N_EXPERTS, 128), lambda i: (0, 0))],
        out_shape=[jax.ShapeDtypeStruct((8, T), jnp.int32),
                   jax.ShapeDtypeStruct((8, T), jnp.float32),
                   jax.ShapeDtypeStruct((N_EXPERTS, 128), jnp.float32)],
        scratch_shapes=[pltpu.VMEM((N_EXPERTS, 128), jnp.float32)],
        compiler_params=pltpu.CompilerParams(dimension_semantics=("arbitrary",)),
        name="route",
    )(logits)


def _dispatch_kernel(n_blocks, lastblk_ref, nu_ref, dest_ref, h_ref, xb_ref, zero_ref, sem, zsem):
    i = pl.program_id(0)

    @pl.when(i == 0)
    def _():
        zero_ref[...] = jnp.zeros_like(zero_ref)

        def zero_copy(blk):
            return pltpu.make_async_copy(
                zero_ref, xb_ref.at[pl.ds(pl.multiple_of(blk * BM, BM), BM), :], zsem)

        def for_each_zero_block(fn):
            def per_expert(e, carry):
                @pl.when(lastblk_ref[e] >= 0)
                def _():
                    fn(zero_copy(lastblk_ref[e]))
                return carry

            def per_unused(b, carry):
                fn(zero_copy(b))
                return carry

            lax.fori_loop(0, N_EXPERTS, per_expert, 0)
            lax.fori_loop(nu_ref[0], n_blocks, per_unused, 0)

        for_each_zero_block(lambda cp: cp.start())
        for_each_zero_block(lambda cp: cp.wait())

    for t in range(TD):
        for k in range(2):
            pltpu.make_async_copy(h_ref.at[pl.ds(t, 1), :],
                                  xb_ref.at[pl.ds(dest_ref[k * TD + t], 1), :],
                                  sem).start(priority=k)
    for k in range(2):
        pltpu.make_async_copy(h_ref, xb_ref.at[pl.ds(0, TD), :], sem).wait()


def _dispatch(h2, dest, lastblk, n_used, n_blocks):
    T = h2.shape[0]
    grid_spec = pltpu.PrefetchScalarGridSpec(
        num_scalar_prefetch=2,
        grid=(T // TD,),
        in_specs=[pl.BlockSpec((2 * TD,), lambda i, lb, nu: (i,), memory_space=pltpu.SMEM),
                  pl.BlockSpec((TD, DP), lambda i, lb, nu: (i, 0))],
        out_specs=pl.BlockSpec(memory_space=pl.ANY),
        scratch_shapes=[pltpu.VMEM((BM, DP), jnp.uint32),
                        pltpu.SemaphoreType.DMA(()), pltpu.SemaphoreType.DMA(())],
    )
    return pl.pallas_call(
        functools.partial(_dispatch_kernel, n_blocks),
        grid_spec=grid_spec,
        out_shape=jax.ShapeDtypeStruct((n_blocks * BM, DP), jnp.uint32),
        compiler_params=pltpu.CompilerParams(dimension_semantics=("arbitrary",)),
        name="dispatch",
    )(lastblk, n_used, dest, h2)


def _expert_kernel(layer, be_ref, nu_ref, nxt_ref, gs_ref, xb_ref, w1_ref, w3_ref, w2_ref, yb_ref,
                   wf1_ref, wf3_ref, wf2_ref, w1b_ref, w3b_ref, w2b_ref, sem):
    b = pl.program_id(0)
    used = b < nu_ref[0]
    new_expert = jnp.logical_or(b == 0, be_ref[b] != be_ref[jnp.maximum(b - 1, 0)])

    def weight_copies(e, s):
        return [pltpu.make_async_copy(src.at[layer, e], dst.at[s], sem.at[s])
                for src, dst in ((w1_ref, wf1_ref), (w3_ref, wf3_ref), (w2_ref, wf2_ref))]

    @pl.when(b == 0)
    def _():
        for cp in weight_copies(be_ref[0], 0):
            cp.start(priority=1)

    @pl.when(jnp.logical_and(used, new_expert))
    def _():
        s = gs_ref[b]
        for cp in weight_copies(be_ref[b], s):
            cp.wait()

        @pl.when(nxt_ref[b] >= 0)
        def _():
            for cp in weight_copies(nxt_ref[b], 1 - s):
                cp.start(priority=1)

        w1b_ref[...] = wf1_ref[s].astype(jnp.bfloat16)
        w3b_ref[...] = wf3_ref[s].astype(jnp.bfloat16)
        w2b_ref[...] = wf2_ref[s].astype(jnp.bfloat16)

    @pl.when(used)
    def _():
        x_hi, x_lo = (v.astype(jnp.bfloat16) for v in _unpack_bf16_pairs(xb_ref[...]))
        dot32 = functools.partial(jnp.dot, preferred_element_type=jnp.float32)
        a = dot32(x_hi, w1b_ref[0:DP, :]) + dot32(x_lo, w1b_ref[DP:D, :])
        g = dot32(x_hi, w3b_ref[0:DP, :]) + dot32(x_lo, w3b_ref[DP:D, :])
        hmid = (a * _sigmoid(a) * g).astype(jnp.bfloat16)
        yb_ref[...] = _pack_bf16_pairs(dot32(hmid, w2b_ref[...]))

    @pl.when(jnp.logical_not(used))
    def _():
        yb_ref[...] = jnp.zeros_like(yb_ref)


def _experts(layer, block_expert, n_used, next_expert, group_slot, xb, w1, w3, w2):
    n_slots = xb.shape[0]
    n_blocks = n_slots // BM

    def row_map(b, be, nu, nxt, gs):
        return (jnp.minimum(b, nu[0] - 1), 0)

    def out_map(b, be, nu, nxt, gs):
        return (b, 0)

    grid_spec = pltpu.PrefetchScalarGridSpec(
        num_scalar_prefetch=4,
        grid=(n_blocks,),
        in_specs=[pl.BlockSpec((BM, DP), row_map),
                  pl.BlockSpec(memory_space=pl.ANY),
                  pl.BlockSpec(memory_space=pl.ANY),
                  pl.BlockSpec(memory_space=pl.ANY)],
        out_specs=pl.BlockSpec((BM, DP), out_map),
        scratch_shapes=[pltpu.VMEM((2, D, D_FF), jnp.float32),
                        pltpu.VMEM((2, D, D_FF), jnp.float32),
                        pltpu.VMEM((2, D_FF, D), jnp.float32),
                        pltpu.VMEM((D, D_FF), jnp.bfloat16),
                        pltpu.VMEM((D, D_FF), jnp.bfloat16),
                        pltpu.VMEM((D_FF, D), jnp.bfloat16),
                        pltpu.SemaphoreType.DMA((2,))],
    )
    return pl.pallas_call(
        functools.partial(_expert_kernel, layer),
        grid_spec=grid_spec,
        out_shape=jax.ShapeDtypeStruct((n_slots, DP), jnp.uint32),
        compiler_params=pltpu.CompilerParams(dimension_semantics=("arbitrary",),
                                             vmem_limit_bytes=VMEM_LIMIT),
        name="experts",
    )(block_expert, n_used, next_expert, group_slot, xb, w1, w3, w2)


def _combine_kernel(final, n_tiles, dest_ref, dest_next_ref, x_ref, w_ref, nrm_ref, yb_ref, o_ref,
                    rows_ref, sem):
    i = pl.program_id(0)
    slot = i % 2

    def gather_tile(d_ref, s):
        for t in range(TD):
            for k in range(2):
                pltpu.make_async_copy(yb_ref.at[pl.ds(d_ref[k * TD + t], 1), :],
                                      rows_ref.at[s, k, pl.ds(t, 1), :],
                                      sem.at[s]).start(priority=k)

    @pl.when(i == 0)
    def _():
        gather_tile(dest_ref, 0)

    @pl.when(i + 1 < n_tiles)
    def _():
        gather_tile(dest_next_ref, 1 - slot)

    for k in range(2):
        pltpu.make_async_copy(yb_ref.at[pl.ds(0, TD), :], rows_ref.at[slot, k], sem.at[slot]).wait()
    w = w_ref[...].T
    hi0, lo0 = _unpack_bf16_pairs(rows_ref[slot, 0])
    hi1, lo1 = _unpack_bf16_pairs(rows_ref[slot, 1])
    y_hi = x_ref[:, 0:DP] + w[:, 0:1] * hi0 + w[:, 1:2] * hi1
    y_lo = x_ref[:, DP:D] + w[:, 0:1] * lo0 + w[:, 1:2] * lo1
    if final:
        ms = (jnp.sum(y_hi * y_hi, axis=-1, keepdims=True)
              + jnp.sum(y_lo * y_lo, axis=-1, keepdims=True)) * (1.0 / D)
        scale = lax.rsqrt(ms + RMS_EPS)
        y_hi = y_hi * scale * nrm_ref[:, 0:DP]
        y_lo = y_lo * scale * nrm_ref[:, DP:D]
    o_ref[:, 0:DP] = y_hi
    o_ref[:, DP:D] = y_lo


def _combine(dest, x2, wgt, nrm, yb, final):
    T = x2.shape[0]
    n_tiles = T // TD
    return pl.pallas_call(
        functools.partial(_combine_kernel, final, n_tiles),
        grid=(n_tiles,),
        in_specs=[pl.BlockSpec((2 * TD,), lambda i: (i,), memory_space=pltpu.SMEM),
                  pl.BlockSpec((2 * TD,), lambda i: (jnp.minimum(i + 1, n_tiles - 1),),
                               memory_space=pltpu.SMEM),
                  pl.BlockSpec((TD, D), lambda i: (i, 0)),
                  pl.BlockSpec((8, TD), lambda i: (0, i)),
                  pl.BlockSpec((1, D), lambda i: (0, 0)),
                  pl.BlockSpec(memory_space=pl.ANY)],
        out_specs=pl.BlockSpec((TD, D), lambda i: (i, 0)),
        out_shape=jax.ShapeDtypeStruct((T, D), jnp.float32),
        scratch_shapes=[pltpu.VMEM((2, 2, TD, DP), jnp.uint32), pltpu.SemaphoreType.DMA((2,))],
        compiler_params=pltpu.CompilerParams(dimension_semantics=("arbitrary",)),
        name="combine",
    )(dest, dest, x2, wgt, nrm, yb)


def _moe(layer, x2, h2, logits, w1, w3, w2, nrm, final):
    T = x2.shape[0]
    idx, wgt, cnt = _route(logits)
    counts = cnt[:, 0].astype(jnp.int32)
    padded = (counts + BM - 1) // BM * BM
    padded_end = jnp.cumsum(padded)
    padded_start = padded_end - padded
    eids = jnp.arange(N_EXPERTS, dtype=jnp.int32)[:, None]
    start_of = lambda e: jnp.sum(jnp.where(e[None, :] == eids, padded_start[:, None], 0), axis=0)
    dest = jnp.stack([start_of(idx[0]) + idx[2], start_of(idx[1]) + idx[3]])
    n_blocks = (2 * T) // BM + N_EXPERTS
    block_start = jnp.arange(n_blocks, dtype=jnp.int32) * BM
    block_expert = jnp.minimum(
        jnp.sum((padded_end[None, :] <= block_start[:, None]).astype(jnp.int32), axis=1),
        N_EXPERTS - 1)
    n_used = (padded_end[-1] // BM).reshape(1)
    lastblk = jnp.where(counts > 0, padded_end // BM - 1, -1)
    later = jnp.logical_and(eids.T > eids, (counts > 0)[None, :])
    next_of = jnp.min(jnp.where(later, eids.T, N_EXPERTS), axis=1)
    next_of = jnp.where(next_of < N_EXPERTS, next_of, -1)
    pick = lambda table: jnp.sum(
        jnp.where(block_expert[:, None] == eids.T, table[None, :], 0), axis=1)
    next_expert = pick(next_of)
    group_slot = pick(jnp.cumsum((counts > 0).astype(jnp.int32)) - 1) % 2
    dest_tiles = dest.reshape(2, T // TD, TD).transpose(1, 0, 2).reshape(-1)
    xb = _dispatch(h2, dest_tiles, lastblk, n_used, n_blocks)
    yb = _experts(layer, block_expert, n_used, next_expert, group_slot, xb, w1, w3, w2)
    return _combine(dest_tiles, x2, wgt, nrm, yb, final)


def kernel(x, norm_mix, w_in, conv_a, w_a_out, conv_b, ln_b_g, ln_b_b, w_b_out, ln_c_g, ln_c_b,
           w_s, b_s, w_c_out, w_o, norm_ffn, w_group, b_group, w_router, b_router, w1, w3, w2,
           norm_final):
    bsz, seq, d = x.shape
    depth = norm_mix.shape[0]
    bf = jnp.bfloat16
    x2 = x.reshape(bsz * seq, d)
    wr = jnp.zeros((depth, d, NLOG), jnp.float32)
    wr = wr.at[:, :, 0:N_GROUPS].set(w_group).at[:, :, E_OFF:E_OFF + N_EXPERTS].set(w_router)
    wr_hi = wr.astype(bf)
    wr_lo = (wr - wr_hi.astype(jnp.float32)).astype(bf)
    br = jnp.zeros((depth, 1, NLOG), jnp.float32)
    br = br.at[:, 0, 0:N_GROUPS].set(b_group).at[:, 0, E_OFF:E_OFF + N_EXPERTS].set(b_router)
    bsb = jnp.broadcast_to(b_s[:, :, :, None], (depth, N_HEADS_C, CHUNK, CHUNK))
    row = lambda p: p[:, None, :]
    cbb = jnp.broadcast_to(conv_b.astype(bf)[:, :, None, :], (depth, CONV_B, 16, d))
    mixer_params = (row(norm_mix), w_in.astype(bf), conv_a, w_a_out.astype(bf), cbb,
                    row(ln_b_g), row(ln_b_b), w_b_out.astype(bf), row(ln_c_g), row(ln_c_b),
                    w_s.astype(bf), bsb, w_c_out.astype(bf), w_o.astype(bf), row(norm_ffn), wr_hi, wr_lo, br)
    for l in range(depth):
        x2, h2, logits = _mixer(l, x2, seq, *mixer_params)
        x2 = _moe(l, x2, h2, logits, w1, w3, w2, norm_final[None], l == depth - 1)
    return x2.reshape(bsz, seq, d)
```

```python
import functools

import jax
import jax.numpy as jnp
from jax import lax
from jax.experimental import pallas as pl
from jax.experimental.pallas import tpu as pltpu

D = 1024
DP = D // 2
N_HEADS_C = 8
CHUNK = 128
CONV_A = 3
CONV_B = 31
N_GROUPS = 4
EPG = 8
N_EXPERTS = N_GROUPS * EPG
D_FF = 512
RMS_EPS = 1e-6
LN_EPS = 1e-5

C_XA, C_BA, C_CA, C_VB, C_GB, C_UV, C_G = 0, 1024, 2048, 3072, 4096, 5120, 7168
D_IN = 10240

TM = 512
HALO = 16
TE = TM + 2 * HALO
CW = 256
RB = 128
NLOG = 128
E_OFF = 8

TR = 512
BM = 512
TD = 256

VMEM_LIMIT = 60 * 1024 * 1024


def _sigmoid(x):
    return 0.5 * (jnp.tanh(0.5 * x) + 1.0)


def _gelu_tanh(x):
    return 0.5 * x * (1.0 + jnp.tanh(0.7978845608028654 * (x + 0.044715 * (x * x * x))))


def _pack_bf16_pairs(x):
    c = x.shape[1] // 2
    as_bits = lambda v: lax.bitcast_convert_type(v.astype(jnp.bfloat16).astype(jnp.float32), jnp.uint32)
    return as_bits(x[:, :c]) | (as_bits(x[:, c:]) >> 16)


def _unpack_bf16_pairs(p):
    hi = lax.bitcast_convert_type(p & jnp.uint32(0xFFFF0000), jnp.float32)
    lo = lax.bitcast_convert_type(p << 16, jnp.float32)
    return hi, lo


def _layer_norm(x, g, b):
    mu = jnp.mean(x, axis=-1, keepdims=True)
    xc = x - mu
    var = jnp.mean(xc * xc, axis=-1, keepdims=True)
    return xc * lax.rsqrt(var + LN_EPS) * g + b


def _mixer_kernel(seq_tiles,
                  xp_ref, xc_ref, xn_ref, nm_ref, win_ref, ca_ref, wa_ref, cbb_ref,
                  lnbg_ref, lnbb_ref, wb_ref, lncg_ref, lncb_ref, ws_ref, bsb_ref, wc_ref,
                  wo_ref, nf_ref, wrh_ref, wrl_ref, br_ref,
                  xo_ref, h2_ref, lg_ref,
                  hb_ref, zq_ref, zq8_ref, cva_ref, cvb_ref, cvn_ref, vst_ref, mg_ref):
    i = pl.program_id(0)
    at_start = (i % seq_tiles) == 0
    at_end = (i % seq_tiles) == seq_tiles - 1

    def _rms(xv):
        ms = jnp.mean(xv * xv, axis=-1, keepdims=True)
        return (xv * lax.rsqrt(ms + RMS_EPS) * nm_ref[...]).astype(jnp.bfloat16)

    hb_ref[0:HALO, :] = _rms(xp_ref[...])
    hb_ref[HALO:HALO + TM, :] = _rms(xc_ref[...])
    hb_ref[HALO + TM:TE, :] = _rms(xn_ref[...])

    rows = lax.broadcasted_iota(jnp.int32, (TE, 1), 0)
    lo = jnp.where(at_start, HALO, 0)
    hi = jnp.where(at_end, HALO + TM, TE)
    valid = jnp.logical_and(rows >= lo, rows < hi)

    def proj(r0, r1, c0, width):
        return jnp.dot(hb_ref[r0:r1, :], win_ref[:, c0:c0 + width],
                       preferred_element_type=jnp.float32)

    def b_proj(c):
        c0 = c * CW
        val = proj(0, TE, C_VB + c0, CW)
        gate = proj(0, TE, C_GB + c0, CW)
        z = jnp.where(valid, val * _sigmoid(gate), 0.0)
        zq_ref[c % 2] = z.astype(jnp.bfloat16)
        z8 = jnp.concatenate([z[8:TE], jnp.zeros((8, CW), jnp.float32)], axis=0)
        zq8_ref[c % 2] = z8.astype(jnp.bfloat16)

    def a_proj(c):
        c0 = c * CW
        xa = proj(0, TE, C_XA + c0, CW)
        cc = proj(0, TE, C_CA + c0, CW)
        t = jnp.where(valid, xa * cc, 0.0)
        conv = (ca_ref[0:1, c0:c0 + CW] * t[HALO - 1:HALO - 1 + TM]
                + ca_ref[1:2, c0:c0 + CW] * t[HALO:HALO + TM]
                + ca_ref[2:3, c0:c0 + CW] * t[HALO + 1:HALO + 1 + TM])
        ba = proj(HALO, HALO + TM, C_BA + c0, CW)
        cva_ref[:, c0:c0 + CW] = (ba * conv).astype(jnp.bfloat16)

    def c_v_proj(c):
        c0 = c * CW
        vst_ref[:, c0:c0 + CW] = _gelu_tanh(proj(HALO, HALO + TM, C_UV + D + c0, CW))

    def b_conv(c):
        c0 = c * CW
        pk = RB + 16
        for rb in range(TM // RB):
            r0 = rb * RB
            for lt in range(CW // 128):
                l0 = lt * 128
                acc = None
                for s in range(8):
                    taps = [(q, 8 * q + s - (HALO - CONV_B // 2)) for q in range(4)]
                    taps = [(q, k) for q, k in taps if 0 <= k < CONV_B]
                    tiles = []
                    for j in range(pk // 16):
                        tile = None
                        for q, k in taps:
                            src = zq_ref if q % 2 == 0 else zq8_ref
                            row0 = r0 + 8 * (q - q % 2) + 16 * j
                            term = (cbb_ref[k, :, c0 + l0:c0 + l0 + 128]
                                    * src[c % 2, row0:row0 + 16, l0:l0 + 128])
                            tile = term if tile is None else tile + term
                        tiles.append(tile)
                    part = jnp.concatenate(tiles, axis=0).astype(jnp.float32)
                    shifted = part[s:s + RB]
                    acc = shifted if acc is None else acc + shifted
                xo_ref[r0:r0 + RB, c0 + l0:c0 + l0 + 128] = acc

    n_chunks = D // CW
    b_proj(0)
    for c in range(n_chunks):
        if c + 1 < n_chunks:
            b_proj(c + 1)
        a_proj(c)
        c_v_proj(c)
        b_conv(c)

    zc = _layer_norm(xo_ref[...], lnbg_ref[...], lnbb_ref[...])
    cvb_ref[...] = (zc * _sigmoid(zc)).astype(jnp.bfloat16)
    for c in range(n_chunks):
        c0 = c * CW
        ya = jnp.dot(cva_ref[...], wa_ref[:, c0:c0 + CW], preferred_element_type=jnp.float32)
        ga = _sigmoid(proj(HALO, HALO + TM, C_G + c0, CW))
        mg_ref[:, c0:c0 + CW] = ga * ya

    vn = _layer_norm(vst_ref[...], lncg_ref[...], lncb_ref[...])
    cvn_ref[...] = vn.astype(jnp.bfloat16)
    for c in range(n_chunks):
        c0 = c * CW
        yb = jnp.dot(cvb_ref[...], wb_ref[:, c0:c0 + CW], preferred_element_type=jnp.float32)
        gb = _sigmoid(proj(HALO, HALO + TM, C_G + D + c0, CW))
        mg_ref[:, c0:c0 + CW] += gb * yb

    for c in range(n_chunks):
        c0 = c * CW
        u = _gelu_tanh(proj(HALO, HALO + TM, C_UV + c0, CW))
        for n in range(TM // CHUNK):
            for hh in range(CW // CHUNK):
                h = c * (CW // CHUNK) + hh
                sv = jnp.dot(ws_ref[h], cvn_ref[n * CHUNK:(n + 1) * CHUNK, h * CHUNK:(h + 1) * CHUNK],
                             preferred_element_type=jnp.float32) + bsb_ref[h]
                cva_ref[n * CHUNK:(n + 1) * CHUNK, h * CHUNK:(h + 1) * CHUNK] = (
                    u[n * CHUNK:(n + 1) * CHUNK, hh * CHUNK:(hh + 1) * CHUNK] * sv
                ).astype(jnp.bfloat16)
    for c in range(n_chunks):
        c0 = c * CW
        yc = jnp.dot(cva_ref[...], wc_ref[:, c0:c0 + CW], preferred_element_type=jnp.float32)
        gc = _sigmoid(proj(HALO, HALO + TM, C_G + 2 * D + c0, CW))
        mg_ref[:, c0:c0 + CW] += gc * yc

    cvb_ref[...] = mg_ref[...].astype(jnp.bfloat16)
    for c in range(n_chunks):
        c0 = c * CW
        xo_ref[:, c0:c0 + CW] = xc_ref[:, c0:c0 + CW] + jnp.dot(
            cvb_ref[...], wo_ref[:, c0:c0 + CW], preferred_element_type=jnp.float32)
    xnew = xo_ref[...]
    ms = jnp.mean(xnew * xnew, axis=-1, keepdims=True)
    h2 = xnew * lax.rsqrt(ms + RMS_EPS) * nf_ref[...]
    h2_ref[...] = _pack_bf16_pairs(h2)
    h_hi = h2.astype(jnp.bfloat16)
    h_lo = (h2 - h_hi.astype(jnp.float32)).astype(jnp.bfloat16)
    dot32 = functools.partial(jnp.dot, preferred_element_type=jnp.float32)
    lg_ref[...] = (dot32(h_hi, wrh_ref[...]) + dot32(h_lo, wrh_ref[...])
                   + dot32(h_hi, wrl_ref[...]) + dot32(h_lo, wrl_ref[...]) + br_ref[...])


def _layer_spec(layer, shape):
    nd = len(shape)
    return pl.BlockSpec((None,) + tuple(shape), lambda i, _n=nd: (layer,) + (0,) * _n,
                        pipeline_mode=pl.Buffered(1))


def _mixer(layer, x2, seq_len, nm, win, ca, wa, cb, lnbg, lnbb, wb, lncg, lncb, ws, bsb, wc, wo, nf,
           wrh, wrl, br):
    T = x2.shape[0]
    n_tiles = T // TM
    hb = TM // HALO
    last_halo = T // HALO - 1
    ls = functools.partial(_layer_spec, layer)
    in_specs = [
        pl.BlockSpec((HALO, D), lambda i: (jnp.maximum(i * hb - 1, 0), 0)),
        pl.BlockSpec((TM, D), lambda i: (i, 0)),
        pl.BlockSpec((HALO, D), lambda i: (jnp.minimum((i + 1) * hb, last_halo), 0)),
        ls((1, D)), ls((D, D_IN)), ls((CONV_A, D)), ls((D, D)),
        ls((CONV_B, 16, D)), ls((1, D)), ls((1, D)), ls((D, D)),
        ls((1, D)), ls((1, D)), ls((N_HEADS_C, CHUNK, CHUNK)),
        ls((N_HEADS_C, CHUNK, CHUNK)), ls((D, D)), ls((D, D)),
        ls((1, D)), ls((D, NLOG)), ls((D, NLOG)), ls((1, NLOG)),
    ]
    out_specs = [
        pl.BlockSpec((TM, D), lambda i: (i, 0)),
        pl.BlockSpec((TM, DP), lambda i: (i, 0)),
        pl.BlockSpec((TM, NLOG), lambda i: (i, 0)),
    ]
    return pl.pallas_call(
        functools.partial(_mixer_kernel, seq_len // TM),
        grid=(n_tiles,),
        in_specs=in_specs,
        out_specs=out_specs,
        out_shape=[jax.ShapeDtypeStruct((T, D), jnp.float32),
                   jax.ShapeDtypeStruct((T, DP), jnp.uint32),
                   jax.ShapeDtypeStruct((T, NLOG), jnp.float32)],
        scratch_shapes=[pltpu.VMEM((TE, D), jnp.bfloat16),
                        pltpu.VMEM((2, TE, CW), jnp.bfloat16),
                        pltpu.VMEM((2, TE, CW), jnp.bfloat16),
                        pltpu.VMEM((TM, D), jnp.bfloat16),
                        pltpu.VMEM((TM, D), jnp.bfloat16),
                        pltpu.VMEM((TM, D), jnp.bfloat16),
                        pltpu.VMEM((TM, D), jnp.float32),
                        pltpu.VMEM((TM, D), jnp.float32)],
        compiler_params=pltpu.CompilerParams(dimension_semantics=("arbitrary",),
                                             vmem_limit_bytes=VMEM_LIMIT),
        name="mixer",
    )(x2, x2, x2, nm, win, ca, wa, cb, lnbg, lnbb, wb, lncg, lncb, ws, bsb, wc, wo, nf, wrh, wrl, br)


def _route_kernel(lg_ref, idx_ref, wgt_ref, cnt_ref, carry_ref):
    i = pl.program_id(0)

    @pl.when(i == 0)
    def _():
        carry_ref[...] = jnp.zeros_like(carry_ref)

    lt = lg_ref[...].T
    g = [lt[j:j + 1, :] for j in range(N_GROUPS)]
    gmax = jnp.maximum(jnp.maximum(g[0], g[1]), jnp.maximum(g[2], g[3]))
    gidx = jnp.where(g[0] == gmax, 0.0, jnp.where(g[1] == gmax, 1.0, jnp.where(g[2] == gmax, 2.0, 3.0)))
    gsum = sum(jnp.exp(gj - gmax) for gj in g)
    g_p = 1.0 / gsum

    sel = lt[E_OFF + 3 * EPG:E_OFF + 4 * EPG, :]
    for j in (2, 1, 0):
        sel = jnp.where(gidx == float(j), lt[E_OFF + j * EPG:E_OFF + (j + 1) * EPG, :], sel)
    rid = lax.broadcasted_iota(jnp.int32, (EPG, TR), 0).astype(jnp.float32)
    m1 = jnp.max(sel, axis=0, keepdims=True)
    i1 = jnp.min(jnp.where(sel == m1, rid, float(EPG)), axis=0, keepdims=True)
    rest = jnp.where(rid == i1, -jnp.inf, sel)
    m2 = jnp.max(rest, axis=0, keepdims=True)
    i2 = jnp.min(jnp.where(rest == m2, rid, float(EPG)), axis=0, keepdims=True)
    e2x = jnp.exp(m2 - m1)
    den = 1.0 + e2x
    w1 = (1.0 / den) * g_p
    w2 = (e2x / den) * g_p
    e1 = gidx * float(EPG) + i1
    e2 = gidx * float(EPG) + i2

    eid = lax.broadcasted_iota(jnp.int32, (N_EXPERTS, TR), 0).astype(jnp.float32)
    oh1 = (eid == e1).astype(jnp.float32)
    oh2 = (eid == e2).astype(jnp.float32)
    oh = oh1 + oh2
    tr = lax.broadcasted_iota(jnp.int32, (TR, TR), 0)
    tc = lax.broadcasted_iota(jnp.int32, (TR, TR), 1)
    upper = (tr < tc).astype(jnp.bfloat16)
    before = jnp.dot(oh.astype(jnp.bfloat16), upper, preferred_element_type=jnp.float32)
    base = before + carry_ref[:, 0:1]
    r1 = jnp.sum(oh1 * base, axis=0, keepdims=True)
    r2 = jnp.sum(oh2 * base, axis=0, keepdims=True)
    carry_ref[...] = carry_ref[...] + jnp.sum(oh, axis=1, keepdims=True)

    idx_ref[...] = jnp.zeros_like(idx_ref)
    idx_ref[0:1, :] = e1.astype(jnp.int32)
    idx_ref[1:2, :] = e2.astype(jnp.int32)
    idx_ref[2:3, :] = r1.astype(jnp.int32)
    idx_ref[3:4, :] = r2.astype(jnp.int32)
    wgt_ref[...] = jnp.zeros_like(wgt_ref)
    wgt_ref[0:1, :] = w1
    wgt_ref[1:2, :] = w2
    cnt_ref[...] = carry_ref[...]


def _route(logits):
    T = logits.shape[0]
    return pl.pallas_call(
        _route_kernel,
        grid=(T // TR,),
        in_specs=[pl.BlockSpec((TR, NLOG), lambda i: (i, 0))],
        out_specs=[pl.BlockSpec((8, TR), lambda i: (0, i)),
                   pl.BlockSpec((8, TR), lambda i: (0, i)),
                   pl.BlockSpec((N_EXPERTS, 128), lambda i: (0, 0))],
        out_shape=[jax.ShapeDtypeStruct((8, T), jnp.int32),
                   jax.ShapeDtypeStruct((8, T), jnp.float32),
                   jax.ShapeDtypeStruct((N_EXPERTS, 128), jnp.float32)],
        scratch_shapes=[pltpu.VMEM((N_EXPERTS, 128), jnp.float32)],
        compiler_params=pltpu.CompilerParams(dimension_semantics=("arbitrary",)),
        name="route",
    )(logits)


def _dispatch_kernel(n_blocks, lastblk_ref, nu_ref, dest_ref, h_ref, xb_ref, zero_ref, sem, zsem):
    i = pl.program_id(0)

    @pl.when(i == 0)
    def _():
        zero_ref[...] = jnp.zeros_like(zero_ref)

        def zero_copy(blk):
            return pltpu.make_async_copy(
                zero_ref, xb_ref.at[pl.ds(pl.multiple_of(blk * BM, BM), BM), :], zsem)

        def for_each_zero_block(fn):
            def per_expert(e, carry):
                @pl.when(lastblk_ref[e] >= 0)
                def _():
                    fn(zero_copy(lastblk_ref[e]))
                return carry

            def per_unused(b, carry):
                fn(zero_copy(b))
                return carry

            lax.fori_loop(0, N_EXPERTS, per_expert, 0)
            lax.fori_loop(nu_ref[0], n_blocks, per_unused, 0)

        for_each_zero_block(lambda cp: cp.start())
        for_each_zero_block(lambda cp: cp.wait())

    for t in range(TD):
        for k in range(2):
            pltpu.make_async_copy(h_ref.at[pl.ds(t, 1), :],
                                  xb_ref.at[pl.ds(dest_ref[k * TD + t], 1), :],
                                  sem).start(priority=k)
    for k in range(2):
        pltpu.make_async_copy(h_ref, xb_ref.at[pl.ds(0, TD), :], sem).wait()


def _dispatch(h2, dest, lastblk, n_used, n_blocks):
    T = h2.shape[0]
    grid_spec = pltpu.PrefetchScalarGridSpec(
        num_scalar_prefetch=2,
        grid=(T // TD,),
        in_specs=[pl.BlockSpec((2 * TD,), lambda i, lb, nu: (i,), memory_space=pltpu.SMEM),
                  pl.BlockSpec((TD, DP), lambda i, lb, nu: (i, 0))],
        out_specs=pl.BlockSpec(memory_space=pl.ANY),
        scratch_shapes=[pltpu.VMEM((BM, DP), jnp.uint32),
                        pltpu.SemaphoreType.DMA(()), pltpu.SemaphoreType.DMA(())],
    )
    return pl.pallas_call(
        functools.partial(_dispatch_kernel, n_blocks),
        grid_spec=grid_spec,
        out_shape=jax.ShapeDtypeStruct((n_blocks * BM, DP), jnp.uint32),
        compiler_params=pltpu.CompilerParams(dimension_semantics=("arbitrary",)),
        name="dispatch",
    )(lastblk, n_used, dest, h2)


def _expert_kernel(layer, be_ref, nu_ref, nxt_ref, gs_ref, xb_ref, w1_ref, w3_ref, w2_ref, yb_ref,
                   wf1_ref, wf3_ref, wf2_ref, w1b_ref, w3b_ref, w2b_ref, sem):
    b = pl.program_id(0)
    used = b < nu_ref[0]
    new_expert = jnp.logical_or(b == 0, be_ref[b] != be_ref[jnp.maximum(b - 1, 0)])

    def weight_copies(e, s):
        return [pltpu.make_async_copy(src.at[layer, e], dst.at[s], sem.at[s])
                for src, dst in ((w1_ref, wf1_ref), (w3_ref, wf3_ref), (w2_ref, wf2_ref))]

    @pl.when(b == 0)
    def _():
        for cp in weight_copies(be_ref[0], 0):
            cp.start(priority=1)

    @pl.when(jnp.logical_and(used, new_expert))
    def _():
        s = gs_ref[b]
        for cp in weight_copies(be_ref[b], s):
            cp.wait()

        @pl.when(nxt_ref[b] >= 0)
        def _():
            for cp in weight_copies(nxt_ref[b], 1 - s):
                cp.start(priority=1)

        w1b_ref[...] = wf1_ref[s].astype(jnp.bfloat16)
        w3b_ref[...] = wf3_ref[s].astype(jnp.bfloat16)
        w2b_ref[...] = wf2_ref[s].astype(jnp.bfloat16)

    @pl.when(used)
    def _():
        x_hi, x_lo = (v.astype(jnp.bfloat16) for v in _unpack_bf16_pairs(xb_ref[...]))
        dot32 = functools.partial(jnp.dot, preferred_element_type=jnp.float32)
        a = dot32(x_hi, w1b_ref[0:DP, :]) + dot32(x_lo, w1b_ref[DP:D, :])
        g = dot32(x_hi, w3b_ref[0:DP, :]) + dot32(x_lo, w3b_ref[DP:D, :])
        hmid = (a * _sigmoid(a) * g).astype(jnp.bfloat16)
        yb_ref[...] = _pack_bf16_pairs(dot32(hmid, w2b_ref[...]))

    @pl.when(jnp.logical_not(used))
    def _():
        yb_ref[...] = jnp.zeros_like(yb_ref)


def _experts(layer, block_expert, n_used, next_expert, group_slot, xb, w1, w3, w2):
    n_slots = xb.shape[0]
    n_blocks = n_slots // BM

    def row_map(b, be, nu, nxt, gs):
        return (jnp.minimum(b, nu[0] - 1), 0)

    def out_map(b, be, nu, nxt, gs):
        return (b, 0)

    grid_spec = pltpu.PrefetchScalarGridSpec(
        num_scalar_prefetch=4,
        grid=(n_blocks,),
        in_specs=[pl.BlockSpec((BM, DP), row_map),
                  pl.BlockSpec(memory_space=pl.ANY),
                  pl.BlockSpec(memory_space=pl.ANY),
                  pl.BlockSpec(memory_space=pl.ANY)],
        out_specs=pl.BlockSpec((BM, DP), out_map),
        scratch_shapes=[pltpu.VMEM((2, D, D_FF), jnp.float32),
                        pltpu.VMEM((2, D, D_FF), jnp.float32),
                        pltpu.VMEM((2, D_FF, D), jnp.float32),
                        pltpu.VMEM((D, D_FF), jnp.bfloat16),
                        pltpu.VMEM((D, D_FF), jnp.bfloat16),
                        pltpu.VMEM((D_FF, D), jnp.bfloat16),
                        pltpu.SemaphoreType.DMA((2,))],
    )
    return pl.pallas_call(
        functools.partial(_expert_kernel, layer),
        grid_spec=grid_spec,
        out_shape=jax.ShapeDtypeStruct((n_slots, DP), jnp.uint32),
        compiler_params=pltpu.CompilerParams(dimension_semantics=("arbitrary",),
                                             vmem_limit_bytes=VMEM_LIMIT),
        name="experts",
    )(block_expert, n_used, next_expert, group_slot, xb, w1, w3, w2)


def _combine_kernel(final, n_tiles, dest_ref, dest_next_ref, x_ref, w_ref, nrm_ref, yb_ref, o_ref,
                    rows_ref, sem):
    i = pl.program_id(0)
    slot = i % 2

    def gather_tile(d_ref, s):
        for t in range(TD):
            for k in range(2):
                pltpu.make_async_copy(yb_ref.at[pl.ds(d_ref[k * TD + t], 1), :],
                                      rows_ref.at[s, k, pl.ds(t, 1), :],
                                      sem.at[s]).start(priority=k)

    @pl.when(i == 0)
    def _():
        gather_tile(dest_ref, 0)

    @pl.when(i + 1 < n_tiles)
    def _():
        gather_tile(dest_next_ref, 1 - slot)

    for k in range(2):
        pltpu.make_async_copy(yb_ref.at[pl.ds(0, TD), :], rows_ref.at[slot, k], sem.at[slot]).wait()
    w = w_ref[...].T
    hi0, lo0 = _unpack_bf16_pairs(rows_ref[slot, 0])
    hi1, lo1 = _unpack_bf16_pairs(rows_ref[slot, 1])
    y_hi = x_ref[:, 0:DP] + w[:, 0:1] * hi0 + w[:, 1:2] * hi1
    y_lo = x_ref[:, DP:D] + w[:, 0:1] * lo0 + w[:, 1:2] * lo1
    if final:
        ms = (jnp.sum(y_hi * y_hi, axis=-1, keepdims=True)
              + jnp.sum(y_lo * y_lo, axis=-1, keepdims=True)) * (1.0 / D)
        scale = lax.rsqrt(ms + RMS_EPS)
        y_hi = y_hi * scale * nrm_ref[:, 0:DP]
        y_lo = y_lo * scale * nrm_ref[:, DP:D]
    o_ref[:, 0:DP] = y_hi
    o_ref[:, DP:D] = y_lo


def _combine(dest, x2, wgt, nrm, yb, final):
    T = x2.shape[0]
    n_tiles = T // TD
    return pl.pallas_call(
        functools.partial(_combine_kernel, final, n_tiles),
        grid=(n_tiles,),
        in_specs=[pl.BlockSpec((2 * TD,), lambda i: (i,), memory_space=pltpu.SMEM),
                  pl.BlockSpec((2 * TD,), lambda i: (jnp.minimum(i + 1, n_tiles - 1),),
                               memory_space=pltpu.SMEM),
                  pl.BlockSpec((TD, D), lambda i: (i, 0)),
                  pl.BlockSpec((8, TD), lambda i: (0, i)),
                  pl.BlockSpec((1, D), lambda i: (0, 0)),
                  pl.BlockSpec(memory_space=pl.ANY)],
        out_specs=pl.BlockSpec((TD, D), lambda i: (i, 0)),
        out_shape=jax.ShapeDtypeStruct((T, D), jnp.float32),
        scratch_shapes=[pltpu.VMEM((2, 2, TD, DP), jnp.uint32), pltpu.SemaphoreType.DMA((2,))],
        compiler_params=pltpu.CompilerParams(dimension_semantics=("arbitrary",)),
        name="combine",
    )(dest, dest, x2, wgt, nrm, yb)


def _moe(layer, x2, h2, logits, w1, w3, w2, nrm, final):
    T = x2.shape[0]
    idx, wgt, cnt = _route(logits)
    counts = cnt[:, 0].astype(jnp.int32)
    padded = (counts + BM - 1) // BM * BM
    padded_end = jnp.cumsum(padded)
    padded_start = padded_end - padded
    eids = jnp.arange(N_EXPERTS, dtype=jnp.int32)[:, None]
    start_of = lambda e: jnp.sum(jnp.where(e[None, :] == eids, padded_start[:, None], 0), axis=0)
    dest = jnp.stack([start_of(idx[0]) + idx[2], start_of(idx[1]) + idx[3]])
    n_blocks = (2 * T) // BM + N_EXPERTS
    block_start = jnp.arange(n_blocks, dtype=jnp.int32) * BM
    block_expert = jnp.minimum(
        jnp.sum((padded_end[None, :] <= block_start[:, None]).astype(jnp.int32), axis=1),
        N_EXPERTS - 1)
    n_used = (padded_end[-1] // BM).reshape(1)
    lastblk = jnp.where(counts > 0, padded_end // BM - 1, -1)
    later = jnp.logical_and(eids.T > eids, (counts > 0)[None, :])
    next_of = jnp.min(jnp.where(later, eids.T, N_EXPERTS), axis=1)
    next_of = jnp.where(next_of < N_EXPERTS, next_of, -1)
    pick = lambda table: jnp.sum(
        jnp.where(block_expert[:, None] == eids.T, table[None, :], 0), axis=1)
    next_expert = pick(next_of)
    group_slot = pick(jnp.cumsum((counts > 0).astype(jnp.int32)) - 1) % 2
    dest_tiles = dest.reshape(2, T // TD, TD).transpose(1, 0, 2).reshape(-1)
    xb = _dispatch(h2, dest_tiles, lastblk, n_used, n_blocks)
    yb = _experts(layer, block_expert, n_used, next_expert, group_slot, xb, w1, w3, w2)
    return _combine(dest_tiles, x2, wgt, nrm, yb, final)


def kernel(x, norm_mix, w_in, conv_a, w_a_out, conv_b, ln_b_g, ln_b_b, w_b_out, ln_c_g, ln_c_b,
           w_s, b_s, w_c_out, w_o, norm_ffn, w_group, b_group, w_router, b_router, w1, w3, w2,
           norm_final):
    bsz, seq, d = x.shape
    depth = norm_mix.shape[0]
    bf = jnp.bfloat16
    x2 = x.reshape(bsz * seq, d)
    wr = jnp.zeros((depth, d, NLOG), jnp.float32)
    wr = wr.at[:, :, 0:N_GROUPS].set(w_group).at[:, :, E_OFF:E_OFF + N_EXPERTS].set(w_router)
    wr_hi = wr.astype(bf)
    wr_lo = (wr - wr_hi.astype(jnp.float32)).astype(bf)
    br = jnp.zeros((depth, 1, NLOG), jnp.float32)
    br = br.at[:, 0, 0:N_GROUPS].set(b_group).at[:, 0, E_OFF:E_OFF + N_EXPERTS].set(b_router)
    bsb = jnp.broadcast_to(b_s[:, :, :, None], (depth, N_HEADS_C, CHUNK, CHUNK))
    row = lambda p: p[:, None, :]
    cbb = jnp.broadcast_to(conv_b.astype(bf)[:, :, None, :], (depth, CONV_B, 16, d))
    mixer_params = (row(norm_mix), w_in.astype(bf), conv_a, w_a_out.astype(bf), cbb,
                    row(ln_b_g), row(ln_b_b), w_b_out.astype(bf), row(ln_c_g), row(ln_c_b),
                    w_s.astype(bf), bsb, w_c_out.astype(bf), w_o.astype(bf), row(norm_ffn), wr_hi, wr_lo, br)
    for l in range(depth):
        x2, h2, logits = _mixer(l, x2, seq, *mixer_params)
        x2 = _moe(l, x2, h2, logits, w1, w3, w2, norm_final[None], l == depth - 1)
    return x2.reshape(bsz, seq, d)
```

```python
import functools

import jax
import jax.numpy as jnp
from jax import lax
from jax.experimental import pallas as pl
from jax.experimental.pallas import tpu as pltpu
from jax.experimental.pallas import tpu_sc as plsc

D = 1024
DP = D // 2
NPIECE = DP // 128
N_HEADS_C = 8
CHUNK = 128
CONV_A = 3
CONV_B = 31
N_GROUPS = 4
EPG = 8
N_EXPERTS = N_GROUPS * EPG
D_FF = 512
RMS_EPS = 1e-6
LN_EPS = 1e-5

C_XA, C_BA, C_CA, C_VB, C_GB, C_UV, C_G = 0, 1024, 2048, 3072, 4096, 5120, 7168
D_IN = 10240

TM = 512
HALO = 16
TE = TM + 2 * HALO
CW = 256
RB = 64
NLOG = 128
E_OFF = 8

TR = 512
BM = 512
TD = 256

VMEM_LIMIT = 60 * 1024 * 1024


def _sigmoid(x):
    return 0.5 * (jnp.tanh(0.5 * x) + 1.0)


def _gelu_tanh(x):
    return 0.5 * x * (1.0 + jnp.tanh(0.7978845608028654 * (x + 0.044715 * (x * x * x))))


def _pack_bf16_pairs(x):
    c = x.shape[1] // 2
    as_bits = lambda v: lax.bitcast_convert_type(v.astype(jnp.bfloat16).astype(jnp.float32), jnp.uint32)
    return as_bits(x[:, :c]) | (as_bits(x[:, c:]) >> 16)


def _unpack_bf16_pairs(p):
    hi = lax.bitcast_convert_type(p & jnp.uint32(0xFFFF0000), jnp.float32)
    lo = lax.bitcast_convert_type(p << 16, jnp.float32)
    return hi, lo


def _layer_norm(x, g, b):
    mu = jnp.mean(x, axis=-1, keepdims=True)
    xc = x - mu
    var = jnp.mean(xc * xc, axis=-1, keepdims=True)
    return xc * lax.rsqrt(var + LN_EPS) * g + b


def _mixer_kernel(seq_tiles,
                  xp_ref, xc_ref, xn_ref, nm_ref, win_ref, ca_ref, wa_ref, cb_ref,
                  lnbg_ref, lnbb_ref, wb_ref, lncg_ref, lncb_ref, ws_ref, bsb_ref, wc_ref,
                  wo_ref, nf_ref, wrh_ref, wrl_ref, br_ref,
                  xo_ref, h2_ref, lg_ref,
                  hb_ref, zq_ref, cva_ref, cvb_ref, cvn_ref, vst_ref, mg_ref):
    i = pl.program_id(0)
    at_start = (i % seq_tiles) == 0
    at_end = (i % seq_tiles) == seq_tiles - 1

    def _rms(xv):
        ms = jnp.mean(xv * xv, axis=-1, keepdims=True)
        return (xv * lax.rsqrt(ms + RMS_EPS) * nm_ref[...]).astype(jnp.bfloat16)

    hb_ref[0:HALO, :] = _rms(xp_ref[...])
    hb_ref[HALO:HALO + TM, :] = _rms(xc_ref[...])
    hb_ref[HALO + TM:TE, :] = _rms(xn_ref[...])

    rows = lax.broadcasted_iota(jnp.int32, (TE, 1), 0)
    lo = jnp.where(at_start, HALO, 0)
    hi = jnp.where(at_end, HALO + TM, TE)
    valid = jnp.logical_and(rows >= lo, rows < hi)

    def proj(r0, r1, c0, width):
        return jnp.dot(hb_ref[r0:r1, :], win_ref[:, c0:c0 + width],
                       preferred_element_type=jnp.float32)

    def b_proj(c):
        c0 = c * CW
        val = proj(0, TE, C_VB + c0, CW)
        gate = proj(0, TE, C_GB + c0, CW)
        zq_ref[c % 2] = jnp.where(valid, val * _sigmoid(gate), 0.0)

    def a_proj(c):
        c0 = c * CW
        xa = proj(0, TE, C_XA + c0, CW)
        cc = proj(0, TE, C_CA + c0, CW)
        t = jnp.where(valid, xa * cc, 0.0)
        conv = (ca_ref[0:1, c0:c0 + CW] * t[HALO - 1:HALO - 1 + TM]
                + ca_ref[1:2, c0:c0 + CW] * t[HALO:HALO + TM]
                + ca_ref[2:3, c0:c0 + CW] * t[HALO + 1:HALO + 1 + TM])
        ba = proj(HALO, HALO + TM, C_BA + c0, CW)
        cva_ref[:, c0:c0 + CW] = (ba * conv).astype(jnp.bfloat16)

    def c_v_proj(c):
        c0 = c * CW
        vst_ref[:, c0:c0 + CW] = _gelu_tanh(proj(HALO, HALO + TM, C_UV + D + c0, CW))

    def b_conv(c):
        c0 = c * CW
        zq = zq_ref.at[c % 2]
        for rb in range(TM // RB):
            r0 = rb * RB
            for lt in range(CW // 128):
                l0 = lt * 128
                acc = None
                for s in range(8):
                    part = None
                    for q in range(4):
                        k = 8 * q + s - (HALO - CONV_B // 2)
                        if 0 <= k < CONV_B:
                            term = (cb_ref[k:k + 1, c0 + l0:c0 + l0 + 128]
                                    * zq[r0 + 8 * q:r0 + 8 * q + RB + 8, l0:l0 + 128])
                            part = term if part is None else part + term
                    shifted = part[s:s + RB]
                    acc = shifted if acc is None else acc + shifted
                xo_ref[r0:r0 + RB, c0 + l0:c0 + l0 + 128] = acc

    n_chunks = D // CW
    b_proj(0)
    for c in range(n_chunks):
        if c + 1 < n_chunks:
            b_proj(c + 1)
        a_proj(c)
        c_v_proj(c)
        b_conv(c)

    zc = _layer_norm(xo_ref[...], lnbg_ref[...], lnbb_ref[...])
    cvb_ref[...] = (zc * _sigmoid(zc)).astype(jnp.bfloat16)
    for c in range(n_chunks):
        c0 = c * CW
        ya = jnp.dot(cva_ref[...], wa_ref[:, c0:c0 + CW], preferred_element_type=jnp.float32)
        ga = _sigmoid(proj(HALO, HALO + TM, C_G + c0, CW))
        mg_ref[:, c0:c0 + CW] = ga * ya

    vn = _layer_norm(vst_ref[...], lncg_ref[...], lncb_ref[...])
    cvn_ref[...] = vn.astype(jnp.bfloat16)
    for c in range(n_chunks):
        c0 = c * CW
        yb = jnp.dot(cvb_ref[...], wb_ref[:, c0:c0 + CW], preferred_element_type=jnp.float32)
        gb = _sigmoid(proj(HALO, HALO + TM, C_G + D + c0, CW))
        mg_ref[:, c0:c0 + CW] += gb * yb

    for c in range(n_chunks):
        c0 = c * CW
        u = _gelu_tanh(proj(HALO, HALO + TM, C_UV + c0, CW))
        for n in range(TM // CHUNK):
            for hh in range(CW // CHUNK):
                h = c * (CW // CHUNK) + hh
                sv = jnp.dot(ws_ref[h], cvn_ref[n * CHUNK:(n + 1) * CHUNK, h * CHUNK:(h + 1) * CHUNK],
                             preferred_element_type=jnp.float32) + bsb_ref[h]
                cva_ref[n * CHUNK:(n + 1) * CHUNK, h * CHUNK:(h + 1) * CHUNK] = (
                    u[n * CHUNK:(n + 1) * CHUNK, hh * CHUNK:(hh + 1) * CHUNK] * sv
                ).astype(jnp.bfloat16)
    for c in range(n_chunks):
        c0 = c * CW
        yc = jnp.dot(cva_ref[...], wc_ref[:, c0:c0 + CW], preferred_element_type=jnp.float32)
        gc = _sigmoid(proj(HALO, HALO + TM, C_G + 2 * D + c0, CW))
        mg_ref[:, c0:c0 + CW] += gc * yc

    cvb_ref[...] = mg_ref[...].astype(jnp.bfloat16)
    for c in range(n_chunks):
        c0 = c * CW
        xo_ref[:, c0:c0 + CW] = xc_ref[:, c0:c0 + CW] + jnp.dot(
            cvb_ref[...], wo_ref[:, c0:c0 + CW], preferred_element_type=jnp.float32)
    xnew = xo_ref[...]
    ms = jnp.mean(xnew * xnew, axis=-1, keepdims=True)
    h2 = xnew * lax.rsqrt(ms + RMS_EPS) * nf_ref[...]
    h2_ref[...] = _pack_bf16_pairs(h2)
    h_hi = h2.astype(jnp.bfloat16)
    h_lo = (h2 - h_hi.astype(jnp.float32)).astype(jnp.bfloat16)
    dot32 = functools.partial(jnp.dot, preferred_element_type=jnp.float32)
    lg_ref[...] = (dot32(h_hi, wrh_ref[...]) + dot32(h_lo, wrh_ref[...])
                   + dot32(h_hi, wrl_ref[...]) + dot32(h_lo, wrl_ref[...]) + br_ref[...])


def _layer_spec(layer, shape):
    nd = len(shape)
    return pl.BlockSpec((None,) + tuple(shape), lambda i, _n=nd: (layer,) + (0,) * _n,
                        pipeline_mode=pl.Buffered(1))


def _mixer(layer, x2, seq_len, nm, win, ca, wa, cb, lnbg, lnbb, wb, lncg, lncb, ws, bsb, wc, wo, nf,
           wrh, wrl, br):
    T = x2.shape[0]
    n_tiles = T // TM
    hb = TM // HALO
    last_halo = T // HALO - 1
    ls = functools.partial(_layer_spec, layer)
    in_specs = [
        pl.BlockSpec((HALO, D), lambda i: (jnp.maximum(i * hb - 1, 0), 0)),
        pl.BlockSpec((TM, D), lambda i: (i, 0)),
        pl.BlockSpec((HALO, D), lambda i: (jnp.minimum((i + 1) * hb, last_halo), 0)),
        ls((1, D)), ls((D, D_IN)), ls((CONV_A, D)), ls((D, D)),
        ls((CONV_B, D)), ls((1, D)), ls((1, D)), ls((D, D)),
        ls((1, D)), ls((1, D)), ls((N_HEADS_C, CHUNK, CHUNK)),
        ls((N_HEADS_C, CHUNK, CHUNK)), ls((D, D)), ls((D, D)),
        ls((1, D)), ls((D, NLOG)), ls((D, NLOG)), ls((1, NLOG)),
    ]
    out_specs = [
        pl.BlockSpec((TM, D), lambda i: (i, 0)),
        pl.BlockSpec((TM, DP), lambda i: (i, 0)),
        pl.BlockSpec((TM, NLOG), lambda i: (i, 0)),
    ]
    return pl.pallas_call(
        functools.partial(_mixer_kernel, seq_len // TM),
        grid=(n_tiles,),
        in_specs=in_specs,
        out_specs=out_specs,
        out_shape=[jax.ShapeDtypeStruct((T, D), jnp.float32),
                   jax.ShapeDtypeStruct((T, DP), jnp.uint32),
                   jax.ShapeDtypeStruct((T, NLOG), jnp.float32)],
        scratch_shapes=[pltpu.VMEM((TE, D), jnp.bfloat16),
                        pltpu.VMEM((2, TE, CW), jnp.float32),
                        pltpu.VMEM((TM, D), jnp.bfloat16),
                        pltpu.VMEM((TM, D), jnp.bfloat16),
                        pltpu.VMEM((TM, D), jnp.bfloat16),
                        pltpu.VMEM((TM, D), jnp.float32),
                        pltpu.VMEM((TM, D), jnp.float32)],
        compiler_params=pltpu.CompilerParams(dimension_semantics=("arbitrary",),
                                             vmem_limit_bytes=VMEM_LIMIT),
        name="mixer",
    )(x2, x2, x2, nm, win, ca, wa, cb, lnbg, lnbb, wb, lncg, lncb, ws, bsb, wc, wo, nf, wrh, wrl, br)


def _route_kernel(lg_ref, idx_ref, wgt_ref, cnt_ref, carry_ref):
    i = pl.program_id(0)

    @pl.when(i == 0)
    def _():
        carry_ref[...] = jnp.zeros_like(carry_ref)

    lt = lg_ref[...].T
    g = [lt[j:j + 1, :] for j in range(N_GROUPS)]
    gmax = jnp.maximum(jnp.maximum(g[0], g[1]), jnp.maximum(g[2], g[3]))
    gidx = jnp.where(g[0] == gmax, 0.0, jnp.where(g[1] == gmax, 1.0, jnp.where(g[2] == gmax, 2.0, 3.0)))
    gsum = sum(jnp.exp(gj - gmax) for gj in g)
    g_p = 1.0 / gsum

    sel = lt[E_OFF + 3 * EPG:E_OFF + 4 * EPG, :]
    for j in (2, 1, 0):
        sel = jnp.where(gidx == float(j), lt[E_OFF + j * EPG:E_OFF + (j + 1) * EPG, :], sel)
    rid = lax.broadcasted_iota(jnp.int32, (EPG, TR), 0).astype(jnp.float32)
    m1 = jnp.max(sel, axis=0, keepdims=True)
    i1 = jnp.min(jnp.where(sel == m1, rid, float(EPG)), axis=0, keepdims=True)
    rest = jnp.where(rid == i1, -jnp.inf, sel)
    m2 = jnp.max(rest, axis=0, keepdims=True)
    i2 = jnp.min(jnp.where(rest == m2, rid, float(EPG)), axis=0, keepdims=True)
    e2x = jnp.exp(m2 - m1)
    den = 1.0 + e2x
    w1 = (1.0 / den) * g_p
    w2 = (e2x / den) * g_p
    e1 = gidx * float(EPG) + i1
    e2 = gidx * float(EPG) + i2

    eid = lax.broadcasted_iota(jnp.int32, (N_EXPERTS, TR), 0).astype(jnp.float32)
    oh1 = (eid == e1).astype(jnp.float32)
    oh2 = (eid == e2).astype(jnp.float32)
    oh = oh1 + oh2
    tr = lax.broadcasted_iota(jnp.int32, (TR, TR), 0)
    tc = lax.broadcasted_iota(jnp.int32, (TR, TR), 1)
    upper = (tr < tc).astype(jnp.bfloat16)
    before = jnp.dot(oh.astype(jnp.bfloat16), upper, preferred_element_type=jnp.float32)
    base = before + carry_ref[:, 0:1]
    r1 = jnp.sum(oh1 * base, axis=0, keepdims=True)
    r2 = jnp.sum(oh2 * base, axis=0, keepdims=True)
    carry_ref[...] = carry_ref[...] + jnp.sum(oh, axis=1, keepdims=True)

    idx_ref[...] = jnp.zeros_like(idx_ref)
    idx_ref[0:1, :] = e1.astype(jnp.int32)
    idx_ref[1:2, :] = e2.astype(jnp.int32)
    idx_ref[2:3, :] = r1.astype(jnp.int32)
    idx_ref[3:4, :] = r2.astype(jnp.int32)
    wgt_ref[...] = jnp.zeros_like(wgt_ref)
    wgt_ref[0:1, :] = w1
    wgt_ref[1:2, :] = w2
    cnt_ref[...] = carry_ref[...]


def _route(logits):
    T = logits.shape[0]
    return pl.pallas_call(
        _route_kernel,
        grid=(T // TR,),
        in_specs=[pl.BlockSpec((TR, NLOG), lambda i: (i, 0))],
        out_specs=[pl.BlockSpec((8, TR), lambda i: (0, i)),
                   pl.BlockSpec((8, TR), lambda i: (0, i)),
                   pl.BlockSpec((N_EXPERTS, 128), lambda i: (0, 0))],
        out_shape=[jax.ShapeDtypeStruct((8, T), jnp.int32),
                   jax.ShapeDtypeStruct((8, T), jnp.float32),
                   jax.ShapeDtypeStruct((N_EXPERTS, 128), jnp.float32)],
        scratch_shapes=[pltpu.VMEM((N_EXPERTS, 128), jnp.float32)],
        compiler_params=pltpu.CompilerParams(dimension_semantics=("arbitrary",)),
        name="route",
    )(logits)


def _dispatch_kernel(n_blocks, lastblk_ref, nu_ref, dest_ref, h_ref, xb_ref, zero_ref, sem, zsem):
    i = pl.program_id(0)

    @pl.when(i == 0)
    def _():
        zero_ref[...] = jnp.zeros_like(zero_ref)

        def zero_copy(blk):
            return pltpu.make_async_copy(
                zero_ref, xb_ref.at[pl.ds(pl.multiple_of(blk * BM, BM), BM), :], zsem)

        def for_each_zero_block(fn):
            def per_expert(e, carry):
                @pl.when(lastblk_ref[e] >= 0)
                def _():
                    fn(zero_copy(lastblk_ref[e]))
                return carry

            def per_unused(b, carry):
                fn(zero_copy(b))
                return carry

            lax.fori_loop(0, N_EXPERTS, per_expert, 0)
            lax.fori_loop(nu_ref[0], n_blocks, per_unused, 0)

        for_each_zero_block(lambda cp: cp.start())
        for_each_zero_block(lambda cp: cp.wait())

    for t in range(TD):
        for k in range(2):
            pltpu.make_async_copy(h_ref.at[pl.ds(t, 1), :],
                                  xb_ref.at[pl.ds(dest_ref[k * TD + t], 1), :],
                                  sem).start(priority=k)
    for k in range(2):
        pltpu.make_async_copy(h_ref, xb_ref.at[pl.ds(0, TD), :], sem).wait()


def _dispatch(h2, dest, lastblk, n_used, n_blocks):
    T = h2.shape[0]
    grid_spec = pltpu.PrefetchScalarGridSpec(
        num_scalar_prefetch=2,
        grid=(T // TD,),
        in_specs=[pl.BlockSpec((2 * TD,), lambda i, lb, nu: (i,), memory_space=pltpu.SMEM),
                  pl.BlockSpec((TD, DP), lambda i, lb, nu: (i, 0))],
        out_specs=pl.BlockSpec(memory_space=pl.ANY),
        scratch_shapes=[pltpu.VMEM((BM, DP), jnp.uint32),
                        pltpu.SemaphoreType.DMA(()), pltpu.SemaphoreType.DMA(())],
    )
    return pl.pallas_call(
        functools.partial(_dispatch_kernel, n_blocks),
        grid_spec=grid_spec,
        out_shape=jax.ShapeDtypeStruct((n_blocks * BM, DP), jnp.uint32),
        compiler_params=pltpu.CompilerParams(dimension_semantics=("arbitrary",)),
        name="dispatch",
    )(lastblk, n_used, dest, h2)


def _expert_kernel(layer, be_ref, nu_ref, nxt_ref, gs_ref, xb_ref, w1_ref, w3_ref, w2_ref, yb_ref,
                   wf1_ref, wf3_ref, wf2_ref, w1b_ref, w3b_ref, w2b_ref, sem):
    b = pl.program_id(0)
    used = b < nu_ref[0]
    new_expert = jnp.logical_or(b == 0, be_ref[b] != be_ref[jnp.maximum(b - 1, 0)])

    def weight_copies(e, s):
        return [pltpu.make_async_copy(src.at[layer, e], dst.at[s], sem.at[s])
                for src, dst in ((w1_ref, wf1_ref), (w3_ref, wf3_ref), (w2_ref, wf2_ref))]

    @pl.when(b == 0)
    def _():
        for cp in weight_copies(be_ref[0], 0):
            cp.start(priority=1)

    @pl.when(jnp.logical_and(used, new_expert))
    def _():
        s = gs_ref[b]
        for cp in weight_copies(be_ref[b], s):
            cp.wait()

        @pl.when(nxt_ref[b] >= 0)
        def _():
            for cp in weight_copies(nxt_ref[b], 1 - s):
                cp.start(priority=1)

        w1b_ref[...] = wf1_ref[s].astype(jnp.bfloat16)
        w3b_ref[...] = wf3_ref[s].astype(jnp.bfloat16)
        w2b_ref[...] = wf2_ref[s].astype(jnp.bfloat16)

    @pl.when(used)
    def _():
        x_hi, x_lo = (v.astype(jnp.bfloat16) for v in _unpack_bf16_pairs(xb_ref[...]))
        dot32 = functools.partial(jnp.dot, preferred_element_type=jnp.float32)
        a = dot32(x_hi, w1b_ref[0:DP, :]) + dot32(x_lo, w1b_ref[DP:D, :])
        g = dot32(x_hi, w3b_ref[0:DP, :]) + dot32(x_lo, w3b_ref[DP:D, :])
        hmid = (a * _sigmoid(a) * g).astype(jnp.bfloat16)
        packed = _pack_bf16_pairs(dot32(hmid, w2b_ref[...]))
        for j in range(NPIECE):
            yb_ref[j] = lax.bitcast_convert_type(packed[:, j * 128:(j + 1) * 128], jnp.int32)

    @pl.when(jnp.logical_not(used))
    def _():
        yb_ref[...] = jnp.zeros_like(yb_ref)


def _experts(layer, block_expert, n_used, next_expert, group_slot, xb, w1, w3, w2):
    n_slots = xb.shape[0]
    n_blocks = n_slots // BM

    def row_map(b, be, nu, nxt, gs):
        return (jnp.minimum(b, nu[0] - 1), 0)

    def out_map(b, be, nu, nxt, gs):
        return (0, b, 0)

    grid_spec = pltpu.PrefetchScalarGridSpec(
        num_scalar_prefetch=4,
        grid=(n_blocks,),
        in_specs=[pl.BlockSpec((BM, DP), row_map),
                  pl.BlockSpec(memory_space=pl.ANY),
                  pl.BlockSpec(memory_space=pl.ANY),
                  pl.BlockSpec(memory_space=pl.ANY)],
        out_specs=pl.BlockSpec((NPIECE, BM, 128), out_map),
        scratch_shapes=[pltpu.VMEM((2, D, D_FF), jnp.float32),
                        pltpu.VMEM((2, D, D_FF), jnp.float32),
                        pltpu.VMEM((2, D_FF, D), jnp.float32),
                        pltpu.VMEM((D, D_FF), jnp.bfloat16),
                        pltpu.VMEM((D, D_FF), jnp.bfloat16),
                        pltpu.VMEM((D_FF, D), jnp.bfloat16),
                        pltpu.SemaphoreType.DMA((2,))],
    )
    return pl.pallas_call(
        functools.partial(_expert_kernel, layer),
        grid_spec=grid_spec,
        out_shape=jax.ShapeDtypeStruct((NPIECE, n_slots, 128), jnp.int32),
        compiler_params=pltpu.CompilerParams(dimension_semantics=("arbitrary",),
                                             vmem_limit_bytes=VMEM_LIMIT),
        name="experts",
    )(block_expert, n_used, next_expert, group_slot, xb, w1, w3, w2)


def _sc_gather_rows(table, idx):
    n_rows = idx.shape[0]
    window = 128
    mesh = plsc.VectorSubcoreMesh(core_axis_name="c", subcore_axis_name="s")

    @functools.partial(
        pl.kernel, mesh=mesh,
        out_type=jax.ShapeDtypeStruct((n_rows, 128), table.dtype))
    def gather(table_hbm, idx_hbm, out_hbm):
        def body(idx_vmem, out_vmem):
            pltpu.sync_copy(table_hbm.at[idx_vmem.at[0]], out_vmem)

        pltpu.emit_pipeline(
            body,
            grid=(n_rows // window,),
            in_specs=[pl.BlockSpec((1, window), lambda i: (0, i))],
            out_specs=[pl.BlockSpec((window, 128), lambda i: (i, 0))],
            core_axis_name=("c", "s"),
            dimension_semantics=(pltpu.PARALLEL,),
        )(idx_hbm, out_hbm)

    return gather(table, idx.reshape(1, n_rows))


def _combine_kernel(final, g_ref, x_ref, w_ref, nrm_ref, o_ref):
    w = w_ref[...].T
    ys = {}
    for j in range(NPIECE):
        hi0, lo0 = _unpack_bf16_pairs(lax.bitcast_convert_type(g_ref[0, j], jnp.uint32))
        hi1, lo1 = _unpack_bf16_pairs(lax.bitcast_convert_type(g_ref[1, j], jnp.uint32))
        c_hi, c_lo = j * 128, DP + j * 128
        ys[c_hi] = x_ref[:, c_hi:c_hi + 128] + w[:, 0:1] * hi0 + w[:, 1:2] * hi1
        ys[c_lo] = x_ref[:, c_lo:c_lo + 128] + w[:, 0:1] * lo0 + w[:, 1:2] * lo1
    if final:
        ms = sum(jnp.sum(y * y, axis=-1, keepdims=True) for y in ys.values()) * (1.0 / D)
        scale = lax.rsqrt(ms + RMS_EPS)
        ys = {c: y * scale * nrm_ref[:, c:c + 128] for c, y in ys.items()}
    for c, y in ys.items():
        o_ref[:, c:c + 128] = y


def _combine(gathered, x2, wgt, nrm, final):
    T = x2.shape[0]
    return pl.pallas_call(
        functools.partial(_combine_kernel, final),
        grid=(T // TD,),
        in_specs=[pl.BlockSpec((2, NPIECE, TD, 128), lambda i: (0, 0, i, 0)),
                  pl.BlockSpec((TD, D), lambda i: (i, 0)),
                  pl.BlockSpec((8, TD), lambda i: (0, i)),
                  pl.BlockSpec((1, D), lambda i: (0, 0))],
        out_specs=pl.BlockSpec((TD, D), lambda i: (i, 0)),
        out_shape=jax.ShapeDtypeStruct((T, D), jnp.float32),
        compiler_params=pltpu.CompilerParams(dimension_semantics=("arbitrary",)),
        name="combine",
    )(gathered, x2, wgt, nrm)


def _moe(layer, x2, h2, logits, w1, w3, w2, nrm, final):
    T = x2.shape[0]
    idx, wgt, cnt = _route(logits)
    counts = cnt[:, 0].astype(jnp.int32)
    padded = (counts + BM - 1) // BM * BM
    padded_end = jnp.cumsum(padded)
    padded_start = padded_end - padded
    eids = jnp.arange(N_EXPERTS, dtype=jnp.int32)[:, None]
    start_of = lambda e: jnp.sum(jnp.where(e[None, :] == eids, padded_start[:, None], 0), axis=0)
    dest = jnp.stack([start_of(idx[0]) + idx[2], start_of(idx[1]) + idx[3]])
    n_blocks = (2 * T) // BM + N_EXPERTS
    block_start = jnp.arange(n_blocks, dtype=jnp.int32) * BM
    block_expert = jnp.minimum(
        jnp.sum((padded_end[None, :] <= block_start[:, None]).astype(jnp.int32), axis=1),
        N_EXPERTS - 1)
    n_used = (padded_end[-1] // BM).reshape(1)
    lastblk = jnp.where(counts > 0, padded_end // BM - 1, -1)
    later = jnp.logical_and(eids.T > eids, (counts > 0)[None, :])
    next_of = jnp.min(jnp.where(later, eids.T, N_EXPERTS), axis=1)
    next_of = jnp.where(next_of < N_EXPERTS, next_of, -1)
    pick = lambda table: jnp.sum(
        jnp.where(block_expert[:, None] == eids.T, table[None, :], 0), axis=1)
    next_expert = pick(next_of)
    group_slot = pick(jnp.cumsum((counts > 0).astype(jnp.int32)) - 1) % 2
    dest_tiles = dest.reshape(2, T // TD, TD).transpose(1, 0, 2).reshape(-1)
    xb = _dispatch(h2, dest_tiles, lastblk, n_used, n_blocks)
    yb = _experts(layer, block_expert, n_used, next_expert, group_slot, xb, w1, w3, w2)
    n_slots = n_blocks * BM
    piece_base = (jnp.arange(NPIECE, dtype=jnp.int32) * n_slots)[None, :, None]
    rows = (dest[:, None, :] + piece_base).reshape(-1)
    gathered = _sc_gather_rows(yb.reshape(NPIECE * n_slots, 128), rows)
    return _combine(gathered.reshape(2, NPIECE, T, 128), x2, wgt, nrm, final)


def kernel(x, norm_mix, w_in, conv_a, w_a_out, conv_b, ln_b_g, ln_b_b, w_b_out, ln_c_g, ln_c_b,
           w_s, b_s, w_c_out, w_o, norm_ffn, w_group, b_group, w_router, b_router, w1, w3, w2,
           norm_final):
    bsz, seq, d = x.shape
    depth = norm_mix.shape[0]
    bf = jnp.bfloat16
    x2 = x.reshape(bsz * seq, d)
    wr = jnp.zeros((depth, d, NLOG), jnp.float32)
    wr = wr.at[:, :, 0:N_GROUPS].set(w_group).at[:, :, E_OFF:E_OFF + N_EXPERTS].set(w_router)
    wr_hi = wr.astype(bf)
    wr_lo = (wr - wr_hi.astype(jnp.float32)).astype(bf)
    br = jnp.zeros((depth, 1, NLOG), jnp.float32)
    br = br.at[:, 0, 0:N_GROUPS].set(b_group).at[:, 0, E_OFF:E_OFF + N_EXPERTS].set(b_router)
    bsb = jnp.broadcast_to(b_s[:, :, :, None], (depth, N_HEADS_C, CHUNK, CHUNK))
    row = lambda p: p[:, None, :]
    mixer_params = (row(norm_mix), w_in.astype(bf), conv_a, w_a_out.astype(bf), conv_b,
                    row(ln_b_g), row(ln_b_b), w_b_out.astype(bf), row(ln_c_g), row(ln_c_b),
                    w_s.astype(bf), bsb, w_c_out.astype(bf), w_o.astype(bf), row(norm_ffn), wr_hi, wr_lo, br)
    for l in range(depth):
        x2, h2, logits = _mixer(l, x2, seq, *mixer_params)
        x2 = _moe(l, x2, h2, logits, w1, w3, w2, norm_final[None], l == depth - 1)
    return x2.reshape(bsz, seq, d)
```

```python
import functools

import jax
import jax.numpy as jnp
from jax import lax
from jax.experimental import pallas as pl
from jax.experimental.pallas import tpu as pltpu
from jax.experimental.pallas import tpu_sc as plsc

D = 1024
DP = D // 2
NPIECE = DP // 128
N_HEADS_C = 8
CHUNK = 128
CONV_A = 3
CONV_B = 31
N_GROUPS = 4
EPG = 8
N_EXPERTS = N_GROUPS * EPG
D_FF = 512
RMS_EPS = 1e-6
LN_EPS = 1e-5

C_XA, C_BA, C_CA, C_VB, C_GB, C_UV, C_G = 0, 1024, 2048, 3072, 4096, 5120, 7168
D_IN = 10240

TM = 512
HALO = 16
TE = TM + 2 * HALO
CW = 256
RB = 64
NLOG = 128
E_OFF = 8

TR = 512
BM = 512
TD = 256

VMEM_LIMIT = 60 * 1024 * 1024


def _sigmoid(x):
    return 0.5 * (jnp.tanh(0.5 * x) + 1.0)


def _gelu_tanh(x):
    return 0.5 * x * (1.0 + jnp.tanh(0.7978845608028654 * (x + 0.044715 * (x * x * x))))


def _pack_bf16_pairs(x):
    c = x.shape[1] // 2
    as_bits = lambda v: lax.bitcast_convert_type(v.astype(jnp.bfloat16).astype(jnp.float32), jnp.uint32)
    return as_bits(x[:, :c]) | (as_bits(x[:, c:]) >> 16)


def _unpack_bf16_pairs(p):
    hi = lax.bitcast_convert_type(p & jnp.uint32(0xFFFF0000), jnp.float32)
    lo = lax.bitcast_convert_type(p << 16, jnp.float32)
    return hi, lo


def _layer_norm(x, g, b):
    mu = jnp.mean(x, axis=-1, keepdims=True)
    xc = x - mu
    var = jnp.mean(xc * xc, axis=-1, keepdims=True)
    return xc * lax.rsqrt(var + LN_EPS) * g + b


def _mixer_kernel(seq_tiles,
                  xp_ref, xc_ref, xn_ref, nm_ref, win_ref, ca_ref, wa_ref, cb_ref,
                  lnbg_ref, lnbb_ref, wb_ref, lncg_ref, lncb_ref, ws_ref, bsb_ref, wc_ref,
                  wo_ref, nf_ref, wrh_ref, wrl_ref, br_ref,
                  xo_ref, h2_ref, lg_ref,
                  hb_ref, zq_ref, cva_ref, cvb_ref, cvn_ref, vst_ref, mg_ref):
    i = pl.program_id(0)
    at_start = (i % seq_tiles) == 0
    at_end = (i % seq_tiles) == seq_tiles - 1

    def _rms(xv):
        ms = jnp.mean(xv * xv, axis=-1, keepdims=True)
        return (xv * lax.rsqrt(ms + RMS_EPS) * nm_ref[...]).astype(jnp.bfloat16)

    hb_ref[0:HALO, :] = _rms(xp_ref[...])
    hb_ref[HALO:HALO + TM, :] = _rms(xc_ref[...])
    hb_ref[HALO + TM:TE, :] = _rms(xn_ref[...])

    rows = lax.broadcasted_iota(jnp.int32, (TE, 1), 0)
    lo = jnp.where(at_start, HALO, 0)
    hi = jnp.where(at_end, HALO + TM, TE)
    valid = jnp.logical_and(rows >= lo, rows < hi)

    def proj(r0, r1, c0, width):
        return jnp.dot(hb_ref[r0:r1, :], win_ref[:, c0:c0 + width],
                       preferred_element_type=jnp.float32)

    def b_proj(c):
        c0 = c * CW
        val = proj(0, TE, C_VB + c0, CW)
        gate = proj(0, TE, C_GB + c0, CW)
        zq_ref[c % 2] = jnp.where(valid, val * _sigmoid(gate), 0.0)

    def a_proj(c):
        c0 = c * CW
        xa = proj(0, TE, C_XA + c0, CW)
        cc = proj(0, TE, C_CA + c0, CW)
        t = jnp.where(valid, xa * cc, 0.0)
        conv = (ca_ref[0:1, c0:c0 + CW] * t[HALO - 1:HALO - 1 + TM]
                + ca_ref[1:2, c0:c0 + CW] * t[HALO:HALO + TM]
                + ca_ref[2:3, c0:c0 + CW] * t[HALO + 1:HALO + 1 + TM])
        ba = proj(HALO, HALO + TM, C_BA + c0, CW)
        cva_ref[:, c0:c0 + CW] = (ba * conv).astype(jnp.bfloat16)

    def c_v_proj(c):
        c0 = c * CW
        vst_ref[:, c0:c0 + CW] = _gelu_tanh(proj(HALO, HALO + TM, C_UV + D + c0, CW))

    def b_conv(c):
        c0 = c * CW
        zq = zq_ref.at[c % 2]
        for rb in range(TM // RB):
            r0 = rb * RB
            for lt in range(CW // 128):
                l0 = lt * 128
                acc = None
                for s in range(8):
                    part = None
                    for q in range(4):
                        k = 8 * q + s - (HALO - CONV_B // 2)
                        if 0 <= k < CONV_B:
                            term = (cb_ref[k:k + 1, c0 + l0:c0 + l0 + 128]
                                    * zq[r0 + 8 * q:r0 + 8 * q + RB + 8, l0:l0 + 128])
                            part = term if part is None else part + term
                    shifted = part[s:s + RB]
                    acc = shifted if acc is None else acc + shifted
                xo_ref[r0:r0 + RB, c0 + l0:c0 + l0 + 128] = acc

    n_chunks = D // CW
    b_proj(0)
    for c in range(n_chunks):
        if c + 1 < n_chunks:
            b_proj(c + 1)
        a_proj(c)
        c_v_proj(c)
        b_conv(c)

    zc = _layer_norm(xo_ref[...], lnbg_ref[...], lnbb_ref[...])
    cvb_ref[...] = (zc * _sigmoid(zc)).astype(jnp.bfloat16)
    for c in range(n_chunks):
        c0 = c * CW
        ya = jnp.dot(cva_ref[...], wa_ref[:, c0:c0 + CW], preferred_element_type=jnp.float32)
        ga = _sigmoid(proj(HALO, HALO + TM, C_G + c0, CW))
        mg_ref[:, c0:c0 + CW] = ga * ya

    vn = _layer_norm(vst_ref[...], lncg_ref[...], lncb_ref[...])
    cvn_ref[...] = vn.astype(jnp.bfloat16)
    for c in range(n_chunks):
        c0 = c * CW
        yb = jnp.dot(cvb_ref[...], wb_ref[:, c0:c0 + CW], preferred_element_type=jnp.float32)
        gb = _sigmoid(proj(HALO, HALO + TM, C_G + D + c0, CW))
        mg_ref[:, c0:c0 + CW] += gb * yb

    for c in range(n_chunks):
        c0 = c * CW
        u = _gelu_tanh(proj(HALO, HALO + TM, C_UV + c0, CW))
        for n in range(TM // CHUNK):
            for hh in range(CW // CHUNK):
                h = c * (CW // CHUNK) + hh
                sv = jnp.dot(ws_ref[h], cvn_ref[n * CHUNK:(n + 1) * CHUNK, h * CHUNK:(h + 1) * CHUNK],
                             preferred_element_type=jnp.float32) + bsb_ref[h]
                cva_ref[n * CHUNK:(n + 1) * CHUNK, h * CHUNK:(h + 1) * CHUNK] = (
                    u[n * CHUNK:(n + 1) * CHUNK, hh * CHUNK:(hh + 1) * CHUNK] * sv
                ).astype(jnp.bfloat16)
    for c in range(n_chunks):
        c0 = c * CW
        yc = jnp.dot(cva_ref[...], wc_ref[:, c0:c0 + CW], preferred_element_type=jnp.float32)
        gc = _sigmoid(proj(HALO, HALO + TM, C_G + 2 * D + c0, CW))
        mg_ref[:, c0:c0 + CW] += gc * yc

    cvb_ref[...] = mg_ref[...].astype(jnp.bfloat16)
    for c in range(n_chunks):
        c0 = c * CW
        xo_ref[:, c0:c0 + CW] = xc_ref[:, c0:c0 + CW] + jnp.dot(
            cvb_ref[...], wo_ref[:, c0:c0 + CW], preferred_element_type=jnp.float32)
    xnew = xo_ref[...]
    ms = jnp.mean(xnew * xnew, axis=-1, keepdims=True)
    h2 = xnew * lax.rsqrt(ms + RMS_EPS) * nf_ref[...]
    h2_packed = _pack_bf16_pairs(h2)
    for j in range(NPIECE):
        h2_ref[j] = lax.bitcast_convert_type(h2_packed[:, j * 128:(j + 1) * 128], jnp.int32)
    h_hi = h2.astype(jnp.bfloat16)
    h_lo = (h2 - h_hi.astype(jnp.float32)).astype(jnp.bfloat16)
    dot32 = functools.partial(jnp.dot, preferred_element_type=jnp.float32)
    lg_ref[...] = (dot32(h_hi, wrh_ref[...]) + dot32(h_lo, wrh_ref[...])
                   + dot32(h_hi, wrl_ref[...]) + dot32(h_lo, wrl_ref[...]) + br_ref[...])


def _layer_spec(layer, shape):
    nd = len(shape)
    return pl.BlockSpec((None,) + tuple(shape), lambda i, _n=nd: (layer,) + (0,) * _n,
                        pipeline_mode=pl.Buffered(1))


def _mixer(layer, x2, seq_len, nm, win, ca, wa, cb, lnbg, lnbb, wb, lncg, lncb, ws, bsb, wc, wo, nf,
           wrh, wrl, br):
    T = x2.shape[0]
    n_tiles = T // TM
    hb = TM // HALO
    last_halo = T // HALO - 1
    ls = functools.partial(_layer_spec, layer)
    in_specs = [
        pl.BlockSpec((HALO, D), lambda i: (jnp.maximum(i * hb - 1, 0), 0)),
        pl.BlockSpec((TM, D), lambda i: (i, 0)),
        pl.BlockSpec((HALO, D), lambda i: (jnp.minimum((i + 1) * hb, last_halo), 0)),
        ls((1, D)), ls((D, D_IN)), ls((CONV_A, D)), ls((D, D)),
        ls((CONV_B, D)), ls((1, D)), ls((1, D)), ls((D, D)),
        ls((1, D)), ls((1, D)), ls((N_HEADS_C, CHUNK, CHUNK)),
        ls((N_HEADS_C, CHUNK, CHUNK)), ls((D, D)), ls((D, D)),
        ls((1, D)), ls((D, NLOG)), ls((D, NLOG)), ls((1, NLOG)),
    ]
    out_specs = [
        pl.BlockSpec((TM, D), lambda i: (i, 0)),
        pl.BlockSpec((NPIECE, TM, 128), lambda i: (0, i, 0)),
        pl.BlockSpec((TM, NLOG), lambda i: (i, 0)),
    ]
    return pl.pallas_call(
        functools.partial(_mixer_kernel, seq_len // TM),
        grid=(n_tiles,),
        in_specs=in_specs,
        out_specs=out_specs,
        out_shape=[jax.ShapeDtypeStruct((T, D), jnp.float32),
                   jax.ShapeDtypeStruct((NPIECE, T, 128), jnp.int32),
                   jax.ShapeDtypeStruct((T, NLOG), jnp.float32)],
        scratch_shapes=[pltpu.VMEM((TE, D), jnp.bfloat16),
                        pltpu.VMEM((2, TE, CW), jnp.float32),
                        pltpu.VMEM((TM, D), jnp.bfloat16),
                        pltpu.VMEM((TM, D), jnp.bfloat16),
                        pltpu.VMEM((TM, D), jnp.bfloat16),
                        pltpu.VMEM((TM, D), jnp.float32),
                        pltpu.VMEM((TM, D), jnp.float32)],
        compiler_params=pltpu.CompilerParams(dimension_semantics=("arbitrary",),
                                             vmem_limit_bytes=VMEM_LIMIT),
        name="mixer",
    )(x2, x2, x2, nm, win, ca, wa, cb, lnbg, lnbb, wb, lncg, lncb, ws, bsb, wc, wo, nf, wrh, wrl, br)


def _route_kernel(lg_ref, idx_ref, wgt_ref, cnt_ref, carry_ref):
    i = pl.program_id(0)

    @pl.when(i == 0)
    def _():
        carry_ref[...] = jnp.zeros_like(carry_ref)

    lt = lg_ref[...].T
    g = [lt[j:j + 1, :] for j in range(N_GROUPS)]
    gmax = jnp.maximum(jnp.maximum(g[0], g[1]), jnp.maximum(g[2], g[3]))
    gidx = jnp.where(g[0] == gmax, 0.0, jnp.where(g[1] == gmax, 1.0, jnp.where(g[2] == gmax, 2.0, 3.0)))
    gsum = sum(jnp.exp(gj - gmax) for gj in g)
    g_p = 1.0 / gsum

    sel = lt[E_OFF + 3 * EPG:E_OFF + 4 * EPG, :]
    for j in (2, 1, 0):
        sel = jnp.where(gidx == float(j), lt[E_OFF + j * EPG:E_OFF + (j + 1) * EPG, :], sel)
    rid = lax.broadcasted_iota(jnp.int32, (EPG, TR), 0).astype(jnp.float32)
    m1 = jnp.max(sel, axis=0, keepdims=True)
    i1 = jnp.min(jnp.where(sel == m1, rid, float(EPG)), axis=0, keepdims=True)
    rest = jnp.where(rid == i1, -jnp.inf, sel)
    m2 = jnp.max(rest, axis=0, keepdims=True)
    i2 = jnp.min(jnp.where(rest == m2, rid, float(EPG)), axis=0, keepdims=True)
    e2x = jnp.exp(m2 - m1)
    den = 1.0 + e2x
    w1 = (1.0 / den) * g_p
    w2 = (e2x / den) * g_p
    e1 = gidx * float(EPG) + i1
    e2 = gidx * float(EPG) + i2

    eid = lax.broadcasted_iota(jnp.int32, (N_EXPERTS, TR), 0).astype(jnp.float32)
    oh1 = (eid == e1).astype(jnp.float32)
    oh2 = (eid == e2).astype(jnp.float32)
    oh = oh1 + oh2
    tr = lax.broadcasted_iota(jnp.int32, (TR, TR), 0)
    tc = lax.broadcasted_iota(jnp.int32, (TR, TR), 1)
    upper = (tr < tc).astype(jnp.bfloat16)
    before = jnp.dot(oh.astype(jnp.bfloat16), upper, preferred_element_type=jnp.float32)
    base = before + carry_ref[:, 0:1]
    r1 = jnp.sum(oh1 * base, axis=0, keepdims=True)
    r2 = jnp.sum(oh2 * base, axis=0, keepdims=True)
    carry_ref[...] = carry_ref[...] + jnp.sum(oh, axis=1, keepdims=True)

    idx_ref[...] = jnp.zeros_like(idx_ref)
    idx_ref[0:1, :] = e1.astype(jnp.int32)
    idx_ref[1:2, :] = e2.astype(jnp.int32)
    idx_ref[2:3, :] = r1.astype(jnp.int32)
    idx_ref[3:4, :] = r2.astype(jnp.int32)
    wgt_ref[...] = jnp.zeros_like(wgt_ref)
    wgt_ref[0:1, :] = w1
    wgt_ref[1:2, :] = w2
    cnt_ref[...] = carry_ref[...]


def _route(logits):
    T = logits.shape[0]
    return pl.pallas_call(
        _route_kernel,
        grid=(T // TR,),
        in_specs=[pl.BlockSpec((TR, NLOG), lambda i: (i, 0))],
        out_specs=[pl.BlockSpec((8, TR), lambda i: (0, i)),
                   pl.BlockSpec((8, TR), lambda i: (0, i)),
                   pl.BlockSpec((N_EXPERTS, 128), lambda i: (0, 0))],
        out_shape=[jax.ShapeDtypeStruct((8, T), jnp.int32),
                   jax.ShapeDtypeStruct((8, T), jnp.float32),
                   jax.ShapeDtypeStruct((N_EXPERTS, 128), jnp.float32)],
        scratch_shapes=[pltpu.VMEM((N_EXPERTS, 128), jnp.float32)],
        compiler_params=pltpu.CompilerParams(dimension_semantics=("arbitrary",)),
        name="route",
    )(logits)


SC_WINDOW = 128


def _sc_mesh():
    return plsc.VectorSubcoreMesh(core_axis_name="c", subcore_axis_name="s")


def _sc_scatter_rows(rows, idx, passes, n_out):
    n_src = rows.shape[0]
    src_blocks = n_src // SC_WINDOW

    @functools.partial(
        pl.kernel, mesh=_sc_mesh(),
        out_type=jax.ShapeDtypeStruct((n_out, 128), rows.dtype))
    def scatter(rows_hbm, idx_hbm, out_hbm):
        def body(rows_vmem, idx_vmem):
            pltpu.sync_copy(rows_vmem, out_hbm.at[idx_vmem.at[0]])

        pltpu.emit_pipeline(
            body,
            grid=(passes * src_blocks,),
            in_specs=[pl.BlockSpec((SC_WINDOW, 128), lambda i: (i % src_blocks, 0)),
                      pl.BlockSpec((1, SC_WINDOW), lambda i: (0, i))],
            out_specs=[],
            core_axis_name=("c", "s"),
            dimension_semantics=(pltpu.PARALLEL,),
        )(rows_hbm, idx_hbm)

    return scatter(rows, idx.reshape(1, passes * n_src))


def _expert_kernel(layer, be_ref, nu_ref, nxt_ref, gs_ref, nv_ref, xb_ref, w1_ref, w3_ref, w2_ref, yb_ref,
                   wf1_ref, wf3_ref, wf2_ref, w1b_ref, w3b_ref, w2b_ref, sem):
    b = pl.program_id(0)
    used = b < nu_ref[0]
    new_expert = jnp.logical_or(b == 0, be_ref[b] != be_ref[jnp.maximum(b - 1, 0)])

    def weight_copies(e, s):
        return [pltpu.make_async_copy(src.at[layer, e], dst.at[s], sem.at[s])
                for src, dst in ((w1_ref, wf1_ref), (w3_ref, wf3_ref), (w2_ref, wf2_ref))]

    @pl.when(b == 0)
    def _():
        for cp in weight_copies(be_ref[0], 0):
            cp.start(priority=1)

    @pl.when(jnp.logical_and(used, new_expert))
    def _():
        s = gs_ref[b]
        for cp in weight_copies(be_ref[b], s):
            cp.wait()

        @pl.when(nxt_ref[b] >= 0)
        def _():
            for cp in weight_copies(nxt_ref[b], 1 - s):
                cp.start(priority=1)

        w1b_ref[...] = wf1_ref[s].astype(jnp.bfloat16)
        w3b_ref[...] = wf3_ref[s].astype(jnp.bfloat16)
        w2b_ref[...] = wf2_ref[s].astype(jnp.bfloat16)

    @pl.when(used)
    def _():
        live = lax.broadcasted_iota(jnp.int32, (BM, 1), 0) < nv_ref[b]
        packed_in = jnp.concatenate(
            [jnp.where(live, lax.bitcast_convert_type(xb_ref[j], jnp.uint32), jnp.uint32(0))
             for j in range(NPIECE)], axis=1)
        x_hi, x_lo = (v.astype(jnp.bfloat16) for v in _unpack_bf16_pairs(packed_in))
        dot32 = functools.partial(jnp.dot, preferred_element_type=jnp.float32)
        a = dot32(x_hi, w1b_ref[0:DP, :]) + dot32(x_lo, w1b_ref[DP:D, :])
        g = dot32(x_hi, w3b_ref[0:DP, :]) + dot32(x_lo, w3b_ref[DP:D, :])
        hmid = (a * _sigmoid(a) * g).astype(jnp.bfloat16)
        packed = _pack_bf16_pairs(dot32(hmid, w2b_ref[...]))
        for j in range(NPIECE):
            yb_ref[j] = lax.bitcast_convert_type(packed[:, j * 128:(j + 1) * 128], jnp.int32)

    @pl.when(jnp.logical_not(used))
    def _():
        yb_ref[...] = jnp.zeros_like(yb_ref)


def _experts(layer, block_expert, n_used, next_expert, group_slot, block_valid, xb, w1, w3, w2):
    n_slots = xb.shape[1]
    n_blocks = n_slots // BM

    def row_map(b, be, nu, nxt, gs, nv):
        return (0, jnp.minimum(b, nu[0] - 1), 0)

    def out_map(b, be, nu, nxt, gs, nv):
        return (0, b, 0)

    grid_spec = pltpu.PrefetchScalarGridSpec(
        num_scalar_prefetch=5,
        grid=(n_blocks,),
        in_specs=[pl.BlockSpec((NPIECE, BM, 128), row_map),
                  pl.BlockSpec(memory_space=pl.ANY),
                  pl.BlockSpec(memory_space=pl.ANY),
                  pl.BlockSpec(memory_space=pl.ANY)],
        out_specs=pl.BlockSpec((NPIECE, BM, 128), out_map),
        scratch_shapes=[pltpu.VMEM((2, D, D_FF), jnp.float32),
                        pltpu.VMEM((2, D, D_FF), jnp.float32),
                        pltpu.VMEM((2, D_FF, D), jnp.float32),
                        pltpu.VMEM((D, D_FF), jnp.bfloat16),
                        pltpu.VMEM((D, D_FF), jnp.bfloat16),
                        pltpu.VMEM((D_FF, D), jnp.bfloat16),
                        pltpu.SemaphoreType.DMA((2,))],
    )
    return pl.pallas_call(
        functools.partial(_expert_kernel, layer),
        grid_spec=grid_spec,
        out_shape=jax.ShapeDtypeStruct((NPIECE, n_slots, 128), jnp.int32),
        compiler_params=pltpu.CompilerParams(dimension_semantics=("arbitrary",),
                                             vmem_limit_bytes=VMEM_LIMIT),
        name="experts",
    )(block_expert, n_used, next_expert, group_slot, block_valid, xb, w1, w3, w2)


def _sc_gather_rows(table, idx):
    n_rows = idx.shape[0]

    @functools.partial(
        pl.kernel, mesh=_sc_mesh(),
        out_type=jax.ShapeDtypeStruct((n_rows, 128), table.dtype))
    def gather(table_hbm, idx_hbm, out_hbm):
        def body(idx_vmem, out_vmem):
            pltpu.sync_copy(table_hbm.at[idx_vmem.at[0]], out_vmem)

        pltpu.emit_pipeline(
            body,
            grid=(n_rows // SC_WINDOW,),
            in_specs=[pl.BlockSpec((1, SC_WINDOW), lambda i: (0, i))],
            out_specs=[pl.BlockSpec((SC_WINDOW, 128), lambda i: (i, 0))],
            core_axis_name=("c", "s"),
            dimension_semantics=(pltpu.PARALLEL,),
        )(idx_hbm, out_hbm)

    return gather(table, idx.reshape(1, n_rows))


def _combine_kernel(final, g_ref, x_ref, w_ref, nrm_ref, o_ref):
    w = w_ref[...].T
    ys = {}
    for j in range(NPIECE):
        hi0, lo0 = _unpack_bf16_pairs(lax.bitcast_convert_type(g_ref[0, j], jnp.uint32))
        hi1, lo1 = _unpack_bf16_pairs(lax.bitcast_convert_type(g_ref[1, j], jnp.uint32))
        c_hi, c_lo = j * 128, DP + j * 128
        ys[c_hi] = x_ref[:, c_hi:c_hi + 128] + w[:, 0:1] * hi0 + w[:, 1:2] * hi1
        ys[c_lo] = x_ref[:, c_lo:c_lo + 128] + w[:, 0:1] * lo0 + w[:, 1:2] * lo1
    if final:
        ms = sum(jnp.sum(y * y, axis=-1, keepdims=True) for y in ys.values()) * (1.0 / D)
        scale = lax.rsqrt(ms + RMS_EPS)
        ys = {c: y * scale * nrm_ref[:, c:c + 128] for c, y in ys.items()}
    for c, y in ys.items():
        o_ref[:, c:c + 128] = y


def _combine(gathered, x2, wgt, nrm, final):
    T = x2.shape[0]
    return pl.pallas_call(
        functools.partial(_combine_kernel, final),
        grid=(T // TD,),
        in_specs=[pl.BlockSpec((2, NPIECE, TD, 128), lambda i: (0, 0, i, 0)),
                  pl.BlockSpec((TD, D), lambda i: (i, 0)),
                  pl.BlockSpec((8, TD), lambda i: (0, i)),
                  pl.BlockSpec((1, D), lambda i: (0, 0))],
        out_specs=pl.BlockSpec((TD, D), lambda i: (i, 0)),
        out_shape=jax.ShapeDtypeStruct((T, D), jnp.float32),
        compiler_params=pltpu.CompilerParams(dimension_semantics=("arbitrary",)),
        name="combine",
    )(gathered, x2, wgt, nrm)


def _moe(layer, x2, h2, logits, w1, w3, w2, nrm, final):
    T = x2.shape[0]
    idx, wgt, cnt = _route(logits)
    counts = cnt[:, 0].astype(jnp.int32)
    padded = (counts + BM - 1) // BM * BM
    padded_end = jnp.cumsum(padded)
    padded_start = padded_end - padded
    eids = jnp.arange(N_EXPERTS, dtype=jnp.int32)[:, None]
    start_of = lambda e: jnp.sum(jnp.where(e[None, :] == eids, padded_start[:, None], 0), axis=0)
    dest = jnp.stack([start_of(idx[0]) + idx[2], start_of(idx[1]) + idx[3]])
    n_blocks = (2 * T) // BM + N_EXPERTS
    block_start = jnp.arange(n_blocks, dtype=jnp.int32) * BM
    block_expert = jnp.minimum(
        jnp.sum((padded_end[None, :] <= block_start[:, None]).astype(jnp.int32), axis=1),
        N_EXPERTS - 1)
    n_used = (padded_end[-1] // BM).reshape(1)
    later = jnp.logical_and(eids.T > eids, (counts > 0)[None, :])
    next_of = jnp.min(jnp.where(later, eids.T, N_EXPERTS), axis=1)
    next_of = jnp.where(next_of < N_EXPERTS, next_of, -1)
    pick = lambda table: jnp.sum(
        jnp.where(block_expert[:, None] == eids.T, table[None, :], 0), axis=1)
    next_expert = pick(next_of)
    group_slot = pick(jnp.cumsum((counts > 0).astype(jnp.int32)) - 1) % 2
    block_valid = jnp.clip(pick(padded_start + counts) - block_start, 0, BM)
    n_slots = n_blocks * BM
    piece_base = (jnp.arange(NPIECE, dtype=jnp.int32) * n_slots)[None, :, None]
    rows = (dest[:, None, :] + piece_base).reshape(-1)
    xb = _sc_scatter_rows(h2.reshape(NPIECE * T, 128), rows, 2, NPIECE * n_slots)
    yb = _experts(layer, block_expert, n_used, next_expert, group_slot, block_valid,
                  xb.reshape(NPIECE, n_slots, 128), w1, w3, w2)
    gathered = _sc_gather_rows(yb.reshape(NPIECE * n_slots, 128), rows)
    return _combine(gathered.reshape(2, NPIECE, T, 128), x2, wgt, nrm, final)


def kernel(x, norm_mix, w_in, conv_a, w_a_out, conv_b, ln_b_g, ln_b_b, w_b_out, ln_c_g, ln_c_b,
           w_s, b_s, w_c_out, w_o, norm_ffn, w_group, b_group, w_router, b_router, w1, w3, w2,
           norm_final):
    bsz, seq, d = x.shape
    depth = norm_mix.shape[0]
    bf = jnp.bfloat16
    x2 = x.reshape(bsz * seq, d)
    wr = jnp.zeros((depth, d, NLOG), jnp.float32)
    wr = wr.at[:, :, 0:N_GROUPS].set(w_group).at[:, :, E_OFF:E_OFF + N_EXPERTS].set(w_router)
    wr_hi = wr.astype(bf)
    wr_lo = (wr - wr_hi.astype(jnp.float32)).astype(bf)
    br = jnp.zeros((depth, 1, NLOG), jnp.float32)
    br = br.at[:, 0, 0:N_GROUPS].set(b_group).at[:, 0, E_OFF:E_OFF + N_EXPERTS].set(b_router)
    bsb = jnp.broadcast_to(b_s[:, :, :, None], (depth, N_HEADS_C, CHUNK, CHUNK))
    row = lambda p: p[:, None, :]
    mixer_params = (row(norm_mix), w_in.astype(bf), conv_a, w_a_out.astype(bf), conv_b,
                    row(ln_b_g), row(ln_b_b), w_b_out.astype(bf), row(ln_c_g), row(ln_c_b),
                    w_s.astype(bf), bsb, w_c_out.astype(bf), w_o.astype(bf), row(norm_ffn), wr_hi, wr_lo, br)
    for l in range(depth):
        x2, h2, logits = _mixer(l, x2, seq, *mixer_params)
        x2 = _moe(l, x2, h2, logits, w1, w3, w2, norm_final[None], l == depth - 1)
    return x2.reshape(bsz, seq, d)
```

```python
import functools

import jax
import jax.numpy as jnp
from jax import lax
from jax.experimental import pallas as pl
from jax.experimental.pallas import tpu as pltpu
from jax.experimental.pallas import tpu_sc as plsc

D = 1024
DP = D // 2
NPIECE = DP // 128
N_HEADS_C = 8
CHUNK = 128
CONV_A = 3
CONV_B = 31
N_GROUPS = 4
EPG = 8
N_EXPERTS = N_GROUPS * EPG
D_FF = 512
RMS_EPS = 1e-6
LN_EPS = 1e-5

C_XA, C_BA, C_CA, C_VB, C_GB, C_UV, C_G = 0, 1024, 2048, 3072, 4096, 5120, 7168
D_IN = 10240

TM = 512
HALO = 16
TE = TM + 2 * HALO
CW = 256
RB = 64
NLOG = 128
E_OFF = 8

TR = 512
BM = 512
TD = 512

VMEM_LIMIT = 60 * 1024 * 1024


def _sigmoid(x):
    return 0.5 * (jnp.tanh(0.5 * x) + 1.0)


def _gelu_tanh(x):
    return 0.5 * x * (1.0 + jnp.tanh(0.7978845608028654 * (x + 0.044715 * (x * x * x))))


def _pack_bf16_pairs(x):
    c = x.shape[1] // 2
    as_bits = lambda v: lax.bitcast_convert_type(v.astype(jnp.bfloat16).astype(jnp.float32), jnp.uint32)
    return as_bits(x[:, :c]) | (as_bits(x[:, c:]) >> 16)


def _unpack_bf16_pairs(p):
    hi = lax.bitcast_convert_type(p & jnp.uint32(0xFFFF0000), jnp.float32)
    lo = lax.bitcast_convert_type(p << 16, jnp.float32)
    return hi, lo


def _layer_norm(x, g, b):
    mu = jnp.mean(x, axis=-1, keepdims=True)
    xc = x - mu
    var = jnp.mean(xc * xc, axis=-1, keepdims=True)
    return xc * lax.rsqrt(var + LN_EPS) * g + b


def _moe_residual(x_ref, g_ref, wt_ref):
    w0, w1 = wt_ref[:, 0:1], wt_ref[:, 1:2]
    his, los = [], []
    for j in range(NPIECE):
        hi0, lo0 = _unpack_bf16_pairs(lax.bitcast_convert_type(g_ref[0, j], jnp.uint32))
        hi1, lo1 = _unpack_bf16_pairs(lax.bitcast_convert_type(g_ref[1, j], jnp.uint32))
        his.append(w0 * hi0 + w1 * hi1)
        los.append(w0 * lo0 + w1 * lo1)
    return x_ref[...] + jnp.concatenate(his + los, axis=1)


def _mixer_kernel(seq_tiles, fused_combine, xp_ref, xc_ref, xn_ref, *refs):
    if fused_combine:
        gp_ref, gc_ref, gn_ref, wtp_ref, wtc_ref, wtn_ref = refs[:6]
        refs = refs[6:]
        x_prev = lambda: _moe_residual(xp_ref, gp_ref, wtp_ref)
        x_cur = lambda: _moe_residual(xc_ref, gc_ref, wtc_ref)
        x_next = lambda: _moe_residual(xn_ref, gn_ref, wtn_ref)
    else:
        x_prev, x_cur, x_next = (lambda: xp_ref[...]), (lambda: xc_ref[...]), (lambda: xn_ref[...])
    (nm_ref, win_ref, ca_ref, wa_ref, cb_ref, lnbg_ref, lnbb_ref, wb_ref, lncg_ref, lncb_ref,
     ws_ref, bsb_ref, wc_ref, wo_ref, nf_ref, wrh_ref, wrl_ref, br_ref,
     xo_ref, h2_ref, lg_ref,
     hb_ref, zq_ref, cva_ref, cvb_ref, cvn_ref, vst_ref, mg_ref) = refs
    i = pl.program_id(0)
    at_start = (i % seq_tiles) == 0
    at_end = (i % seq_tiles) == seq_tiles - 1

    def _rms(xv):
        ms = jnp.mean(xv * xv, axis=-1, keepdims=True)
        return (xv * lax.rsqrt(ms + RMS_EPS) * nm_ref[...]).astype(jnp.bfloat16)

    hb_ref[0:HALO, :] = _rms(x_prev())
    hb_ref[HALO:HALO + TM, :] = _rms(x_cur())
    hb_ref[HALO + TM:TE, :] = _rms(x_next())

    rows = lax.broadcasted_iota(jnp.int32, (TE, 1), 0)
    lo = jnp.where(at_start, HALO, 0)
    hi = jnp.where(at_end, HALO + TM, TE)
    valid = jnp.logical_and(rows >= lo, rows < hi)

    def proj(r0, r1, c0, width):
        return jnp.dot(hb_ref[r0:r1, :], win_ref[:, c0:c0 + width],
                       preferred_element_type=jnp.float32)

    def b_proj(c):
        c0 = c * CW
        val = proj(0, TE, C_VB + c0, CW)
        gate = proj(0, TE, C_GB + c0, CW)
        zq_ref[c % 2] = jnp.where(valid, val * _sigmoid(gate), 0.0)

    def a_proj(c):
        c0 = c * CW
        xa = proj(0, TE, C_XA + c0, CW)
        cc = proj(0, TE, C_CA + c0, CW)
        t = jnp.where(valid, xa * cc, 0.0)
        conv = (ca_ref[0:1, c0:c0 + CW] * t[HALO - 1:HALO - 1 + TM]
                + ca_ref[1:2, c0:c0 + CW] * t[HALO:HALO + TM]
                + ca_ref[2:3, c0:c0 + CW] * t[HALO + 1:HALO + 1 + TM])
        ba = proj(HALO, HALO + TM, C_BA + c0, CW)
        cva_ref[:, c0:c0 + CW] = (ba * conv).astype(jnp.bfloat16)

    def c_v_proj(c):
        c0 = c * CW
        vst_ref[:, c0:c0 + CW] = _gelu_tanh(proj(HALO, HALO + TM, C_UV + D + c0, CW))

    def b_conv(c):
        c0 = c * CW
        zq = zq_ref.at[c % 2]
        for rb in range(TM // RB):
            r0 = rb * RB
            for lt in range(CW // 128):
                l0 = lt * 128
                acc = None
                for s in range(8):
                    part = None
                    for q in range(4):
                        k = 8 * q + s - (HALO - CONV_B // 2)
                        if 0 <= k < CONV_B:
                            term = (cb_ref[k:k + 1, c0 + l0:c0 + l0 + 128]
                                    * zq[r0 + 8 * q:r0 + 8 * q + RB + 8, l0:l0 + 128])
                            part = term if part is None else part + term
                    shifted = part[s:s + RB]
                    acc = shifted if acc is None else acc + shifted
                xo_ref[r0:r0 + RB, c0 + l0:c0 + l0 + 128] = acc

    n_chunks = D // CW
    b_proj(0)
    for c in range(n_chunks):
        if c + 1 < n_chunks:
            b_proj(c + 1)
        a_proj(c)
        c_v_proj(c)
        b_conv(c)

    zc = _layer_norm(xo_ref[...], lnbg_ref[...], lnbb_ref[...])
    cvb_ref[...] = (zc * _sigmoid(zc)).astype(jnp.bfloat16)
    for c in range(n_chunks):
        c0 = c * CW
        ya = jnp.dot(cva_ref[...], wa_ref[:, c0:c0 + CW], preferred_element_type=jnp.float32)
        ga = _sigmoid(proj(HALO, HALO + TM, C_G + c0, CW))
        mg_ref[:, c0:c0 + CW] = ga * ya

    vn = _layer_norm(vst_ref[...], lncg_ref[...], lncb_ref[...])
    cvn_ref[...] = vn.astype(jnp.bfloat16)
    for c in range(n_chunks):
        c0 = c * CW
        yb = jnp.dot(cvb_ref[...], wb_ref[:, c0:c0 + CW], preferred_element_type=jnp.float32)
        gb = _sigmoid(proj(HALO, HALO + TM, C_G + D + c0, CW))
        mg_ref[:, c0:c0 + CW] += gb * yb

    for c in range(n_chunks):
        c0 = c * CW
        u = _gelu_tanh(proj(HALO, HALO + TM, C_UV + c0, CW))
        for n in range(TM // CHUNK):
            for hh in range(CW // CHUNK):
                h = c * (CW // CHUNK) + hh
                sv = jnp.dot(ws_ref[h], cvn_ref[n * CHUNK:(n + 1) * CHUNK, h * CHUNK:(h + 1) * CHUNK],
                             preferred_element_type=jnp.float32) + bsb_ref[h]
                cva_ref[n * CHUNK:(n + 1) * CHUNK, h * CHUNK:(h + 1) * CHUNK] = (
                    u[n * CHUNK:(n + 1) * CHUNK, hh * CHUNK:(hh + 1) * CHUNK] * sv
                ).astype(jnp.bfloat16)
    for c in range(n_chunks):
        c0 = c * CW
        yc = jnp.dot(cva_ref[...], wc_ref[:, c0:c0 + CW], preferred_element_type=jnp.float32)
        gc = _sigmoid(proj(HALO, HALO + TM, C_G + 2 * D + c0, CW))
        mg_ref[:, c0:c0 + CW] += gc * yc

    cvb_ref[...] = mg_ref[...].astype(jnp.bfloat16)
    x_res = x_cur()
    for c in range(n_chunks):
        c0 = c * CW
        xo_ref[:, c0:c0 + CW] = x_res[:, c0:c0 + CW] + jnp.dot(
            cvb_ref[...], wo_ref[:, c0:c0 + CW], preferred_element_type=jnp.float32)
    xnew = xo_ref[...]
    ms = jnp.mean(xnew * xnew, axis=-1, keepdims=True)
    h2 = xnew * lax.rsqrt(ms + RMS_EPS) * nf_ref[...]
    h2_packed = _pack_bf16_pairs(h2)
    for j in range(NPIECE):
        h2_ref[j] = lax.bitcast_convert_type(h2_packed[:, j * 128:(j + 1) * 128], jnp.int32)
    h_hi = h2.astype(jnp.bfloat16)
    h_lo = (h2 - h_hi.astype(jnp.float32)).astype(jnp.bfloat16)
    dot32 = functools.partial(jnp.dot, preferred_element_type=jnp.float32)
    lg_ref[...] = (dot32(h_hi, wrh_ref[...]) + dot32(h_lo, wrh_ref[...])
                   + dot32(h_hi, wrl_ref[...]) + dot32(h_lo, wrl_ref[...]) + br_ref[...])


def _layer_spec(layer, shape):
    nd = len(shape)
    return pl.BlockSpec((None,) + tuple(shape), lambda i, _n=nd: (layer,) + (0,) * _n,
                        pipeline_mode=pl.Buffered(1))


def _mixer(layer, x2, pending, seq_len, nm, win, ca, wa, cb, lnbg, lnbb, wb, lncg, lncb, ws, bsb, wc, wo,
           nf, wrh, wrl, br):
    T = x2.shape[0]
    n_tiles = T // TM
    hb = TM // HALO
    last_halo = T // HALO - 1
    ls = functools.partial(_layer_spec, layer)
    prev_rows = lambda i: jnp.maximum(i * hb - 1, 0)
    next_rows = lambda i: jnp.minimum((i + 1) * hb, last_halo)
    in_specs = [
        pl.BlockSpec((HALO, D), lambda i: (prev_rows(i), 0)),
        pl.BlockSpec((TM, D), lambda i: (i, 0)),
        pl.BlockSpec((HALO, D), lambda i: (next_rows(i), 0)),
    ]
    operands = [x2, x2, x2]
    if pending is not None:
        gathered, wgt_rows = pending
        in_specs += [
            pl.BlockSpec((2, NPIECE, HALO, 128), lambda i: (0, 0, prev_rows(i), 0)),
            pl.BlockSpec((2, NPIECE, TM, 128), lambda i: (0, 0, i, 0)),
            pl.BlockSpec((2, NPIECE, HALO, 128), lambda i: (0, 0, next_rows(i), 0)),
            pl.BlockSpec((HALO, 8), lambda i: (prev_rows(i), 0)),
            pl.BlockSpec((TM, 8), lambda i: (i, 0)),
            pl.BlockSpec((HALO, 8), lambda i: (next_rows(i), 0)),
        ]
        operands += [gathered, gathered, gathered, wgt_rows, wgt_rows, wgt_rows]
    in_specs += [
        ls((1, D)), ls((D, D_IN)), ls((CONV_A, D)), ls((D, D)),
        ls((CONV_B, D)), ls((1, D)), ls((1, D)), ls((D, D)),
        ls((1, D)), ls((1, D)), ls((N_HEADS_C, CHUNK, CHUNK)),
        ls((N_HEADS_C, CHUNK, CHUNK)), ls((D, D)), ls((D, D)),
        ls((1, D)), ls((D, NLOG)), ls((D, NLOG)), ls((1, NLOG)),
    ]
    out_specs = [
        pl.BlockSpec((TM, D), lambda i: (i, 0)),
        pl.BlockSpec((NPIECE, TM, 128), lambda i: (0, i, 0)),
        pl.BlockSpec((TM, NLOG), lambda i: (i, 0)),
    ]
    return pl.pallas_call(
        functools.partial(_mixer_kernel, seq_len // TM, pending is not None),
        grid=(n_tiles,),
        in_specs=in_specs,
        out_specs=out_specs,
        out_shape=[jax.ShapeDtypeStruct((T, D), jnp.float32),
                   jax.ShapeDtypeStruct((NPIECE, T, 128), jnp.int32),
                   jax.ShapeDtypeStruct((T, NLOG), jnp.float32)],
        scratch_shapes=[pltpu.VMEM((TE, D), jnp.bfloat16),
                        pltpu.VMEM((2, TE, CW), jnp.float32),
                        pltpu.VMEM((TM, D), jnp.bfloat16),
                        pltpu.VMEM((TM, D), jnp.bfloat16),
                        pltpu.VMEM((TM, D), jnp.bfloat16),
                        pltpu.VMEM((TM, D), jnp.float32),
                        pltpu.VMEM((TM, D), jnp.float32)],
        compiler_params=pltpu.CompilerParams(dimension_semantics=("arbitrary",),
                                             vmem_limit_bytes=VMEM_LIMIT),
        name="mixer",
    )(*operands, nm, win, ca, wa, cb, lnbg, lnbb, wb, lncg, lncb, ws, bsb, wc, wo, nf, wrh, wrl, br)


def _route_kernel(lg_ref, idx_ref, wgt_ref, cnt_ref, carry_ref):
    i = pl.program_id(0)

    @pl.when(i == 0)
    def _():
        carry_ref[...] = jnp.zeros_like(carry_ref)

    lt = lg_ref[...].T
    g = [lt[j:j + 1, :] for j in range(N_GROUPS)]
    gmax = jnp.maximum(jnp.maximum(g[0], g[1]), jnp.maximum(g[2], g[3]))
    gidx = jnp.where(g[0] == gmax, 0.0, jnp.where(g[1] == gmax, 1.0, jnp.where(g[2] == gmax, 2.0, 3.0)))
    gsum = sum(jnp.exp(gj - gmax) for gj in g)
    g_p = 1.0 / gsum

    sel = lt[E_OFF + 3 * EPG:E_OFF + 4 * EPG, :]
    for j in (2, 1, 0):
        sel = jnp.where(gidx == float(j), lt[E_OFF + j * EPG:E_OFF + (j + 1) * EPG, :], sel)
    rid = lax.broadcasted_iota(jnp.int32, (EPG, TR), 0).astype(jnp.float32)
    m1 = jnp.max(sel, axis=0, keepdims=True)
    i1 = jnp.min(jnp.where(sel == m1, rid, float(EPG)), axis=0, keepdims=True)
    rest = jnp.where(rid == i1, -jnp.inf, sel)
    m2 = jnp.max(rest, axis=0, keepdims=True)
    i2 = jnp.min(jnp.where(rest == m2, rid, float(EPG)), axis=0, keepdims=True)
    e2x = jnp.exp(m2 - m1)
    den = 1.0 + e2x
    w1 = (1.0 / den) * g_p
    w2 = (e2x / den) * g_p
    e1 = gidx * float(EPG) + i1
    e2 = gidx * float(EPG) + i2

    eid = lax.broadcasted_iota(jnp.int32, (N_EXPERTS, TR), 0).astype(jnp.float32)
    oh1 = (eid == e1).astype(jnp.float32)
    oh2 = (eid == e2).astype(jnp.float32)
    oh = oh1 + oh2
    tr = lax.broadcasted_iota(jnp.int32, (TR, TR), 0)
    tc = lax.broadcasted_iota(jnp.int32, (TR, TR), 1)
    upper = (tr < tc).astype(jnp.bfloat16)
    before = jnp.dot(oh.astype(jnp.bfloat16), upper, preferred_element_type=jnp.float32)
    base = before + carry_ref[:, 0:1]
    r1 = jnp.sum(oh1 * base, axis=0, keepdims=True)
    r2 = jnp.sum(oh2 * base, axis=0, keepdims=True)
    carry_ref[...] = carry_ref[...] + jnp.sum(oh, axis=1, keepdims=True)

    idx_ref[...] = jnp.zeros_like(idx_ref)
    idx_ref[0:1, :] = e1.astype(jnp.int32)
    idx_ref[1:2, :] = e2.astype(jnp.int32)
    idx_ref[2:3, :] = r1.astype(jnp.int32)
    idx_ref[3:4, :] = r2.astype(jnp.int32)
    wgt_ref[...] = jnp.zeros_like(wgt_ref)
    wgt_ref[0:1, :] = w1
    wgt_ref[1:2, :] = w2
    cnt_ref[...] = carry_ref[...]


def _route(logits):
    T = logits.shape[0]
    return pl.pallas_call(
        _route_kernel,
        grid=(T // TR,),
        in_specs=[pl.BlockSpec((TR, NLOG), lambda i: (i, 0))],
        out_specs=[pl.BlockSpec((8, TR), lambda i: (0, i)),
                   pl.BlockSpec((8, TR), lambda i: (0, i)),
                   pl.BlockSpec((N_EXPERTS, 128), lambda i: (0, 0))],
        out_shape=[jax.ShapeDtypeStruct((8, T), jnp.int32),
                   jax.ShapeDtypeStruct((8, T), jnp.float32),
                   jax.ShapeDtypeStruct((N_EXPERTS, 128), jnp.float32)],
        scratch_shapes=[pltpu.VMEM((N_EXPERTS, 128), jnp.float32)],
        compiler_params=pltpu.CompilerParams(dimension_semantics=("arbitrary",)),
        name="route",
    )(logits)


SC_WINDOW = 128


def _sc_mesh():
    return plsc.VectorSubcoreMesh(core_axis_name="c", subcore_axis_name="s")


def _sc_scatter_rows(rows, idx, passes, n_out):
    n_src = rows.shape[0]
    src_blocks = n_src // SC_WINDOW

    @functools.partial(
        pl.kernel, mesh=_sc_mesh(),
        out_type=jax.ShapeDtypeStruct((n_out, 128), rows.dtype))
    def scatter(rows_hbm, idx_hbm, out_hbm):
        def body(rows_vmem, idx_vmem):
            pltpu.sync_copy(rows_vmem, out_hbm.at[idx_vmem.at[0]])

        pltpu.emit_pipeline(
            body,
            grid=(passes * src_blocks,),
            in_specs=[pl.BlockSpec((SC_WINDOW, 128), lambda i: (i % src_blocks, 0)),
                      pl.BlockSpec((1, SC_WINDOW), lambda i: (0, i))],
            out_specs=[],
            core_axis_name=("c", "s"),
            dimension_semantics=(pltpu.PARALLEL,),
        )(rows_hbm, idx_hbm)

    return scatter(rows, idx.reshape(1, passes * n_src))


def _expert_kernel(layer, be_ref, nu_ref, nxt_ref, gs_ref, nv_ref, xb_ref, w1_ref, w3_ref, w2_ref, yb_ref,
                   wf1_ref, wf3_ref, wf2_ref, w1b_ref, w3b_ref, w2b_ref, sem):
    b = pl.program_id(0)
    used = b < nu_ref[0]
    new_expert = jnp.logical_or(b == 0, be_ref[b] != be_ref[jnp.maximum(b - 1, 0)])

    def weight_copies(e, s):
        return [pltpu.make_async_copy(src.at[layer, e], dst.at[s], sem.at[s])
                for src, dst in ((w1_ref, wf1_ref), (w3_ref, wf3_ref), (w2_ref, wf2_ref))]

    @pl.when(b == 0)
    def _():
        for cp in weight_copies(be_ref[0], 0):
            cp.start(priority=1)

    @pl.when(jnp.logical_and(used, new_expert))
    def _():
        s = gs_ref[b]
        for cp in weight_copies(be_ref[b], s):
            cp.wait()

        @pl.when(nxt_ref[b] >= 0)
        def _():
            for cp in weight_copies(nxt_ref[b], 1 - s):
                cp.start(priority=1)

        w1b_ref[...] = wf1_ref[s].astype(jnp.bfloat16)
        w3b_ref[...] = wf3_ref[s].astype(jnp.bfloat16)
        w2b_ref[...] = wf2_ref[s].astype(jnp.bfloat16)

    @pl.when(used)
    def _():
        live = lax.broadcasted_iota(jnp.int32, (BM, 1), 0) < nv_ref[b]
        packed_in = jnp.concatenate(
            [jnp.where(live, lax.bitcast_convert_type(xb_ref[j], jnp.uint32), jnp.uint32(0))
             for j in range(NPIECE)], axis=1)
        x_hi, x_lo = (v.astype(jnp.bfloat16) for v in _unpack_bf16_pairs(packed_in))
        dot32 = functools.partial(jnp.dot, preferred_element_type=jnp.float32)
        a = dot32(x_hi, w1b_ref[0:DP, :]) + dot32(x_lo, w1b_ref[DP:D, :])
        g = dot32(x_hi, w3b_ref[0:DP, :]) + dot32(x_lo, w3b_ref[DP:D, :])
        hmid = (a * _sigmoid(a) * g).astype(jnp.bfloat16)
        packed = _pack_bf16_pairs(dot32(hmid, w2b_ref[...]))
        for j in range(NPIECE):
            yb_ref[j] = lax.bitcast_convert_type(packed[:, j * 128:(j + 1) * 128], jnp.int32)

    @pl.when(jnp.logical_not(used))
    def _():
        yb_ref[...] = jnp.zeros_like(yb_ref)


def _experts(layer, block_expert, n_used, next_expert, group_slot, block_valid, xb, w1, w3, w2):
    n_slots = xb.shape[1]
    n_blocks = n_slots // BM

    def row_map(b, be, nu, nxt, gs, nv):
        return (0, jnp.minimum(b, nu[0] - 1), 0)

    def out_map(b, be, nu, nxt, gs, nv):
        return (0, b, 0)

    grid_spec = pltpu.PrefetchScalarGridSpec(
        num_scalar_prefetch=5,
        grid=(n_blocks,),
        in_specs=[pl.BlockSpec((NPIECE, BM, 128), row_map),
                  pl.BlockSpec(memory_space=pl.ANY),
                  pl.BlockSpec(memory_space=pl.ANY),
                  pl.BlockSpec(memory_space=pl.ANY)],
        out_specs=pl.BlockSpec((NPIECE, BM, 128), out_map),
        scratch_shapes=[pltpu.VMEM((2, D, D_FF), jnp.float32),
                        pltpu.VMEM((2, D, D_FF), jnp.float32),
                        pltpu.VMEM((2, D_FF, D), jnp.float32),
                        pltpu.VMEM((D, D_FF), jnp.bfloat16),
                        pltpu.VMEM((D, D_FF), jnp.bfloat16),
                        pltpu.VMEM((D_FF, D), jnp.bfloat16),
                        pltpu.SemaphoreType.DMA((2,))],
    )
    return pl.pallas_call(
        functools.partial(_expert_kernel, layer),
        grid_spec=grid_spec,
        out_shape=jax.ShapeDtypeStruct((NPIECE, n_slots, 128), jnp.int32),
        compiler_params=pltpu.CompilerParams(dimension_semantics=("arbitrary",),
                                             vmem_limit_bytes=VMEM_LIMIT),
        name="experts",
    )(block_expert, n_used, next_expert, group_slot, block_valid, xb, w1, w3, w2)


def _sc_gather_rows(table, idx):
    n_rows = idx.shape[0]

    @functools.partial(
        pl.kernel, mesh=_sc_mesh(),
        out_type=jax.ShapeDtypeStruct((n_rows, 128), table.dtype))
    def gather(table_hbm, idx_hbm, out_hbm):
        def body(idx_vmem, out_vmem):
            pltpu.sync_copy(table_hbm.at[idx_vmem.at[0]], out_vmem)

        pltpu.emit_pipeline(
            body,
            grid=(n_rows // SC_WINDOW,),
            in_specs=[pl.BlockSpec((1, SC_WINDOW), lambda i: (0, i))],
            out_specs=[pl.BlockSpec((SC_WINDOW, 128), lambda i: (i, 0))],
            core_axis_name=("c", "s"),
            dimension_semantics=(pltpu.PARALLEL,),
        )(idx_hbm, out_hbm)

    return gather(table, idx.reshape(1, n_rows))


def _combine_kernel(final, g_ref, x_ref, w_ref, nrm_ref, o_ref):
    w = w_ref[...].T
    ys = {}
    for j in range(NPIECE):
        hi0, lo0 = _unpack_bf16_pairs(lax.bitcast_convert_type(g_ref[0, j], jnp.uint32))
        hi1, lo1 = _unpack_bf16_pairs(lax.bitcast_convert_type(g_ref[1, j], jnp.uint32))
        c_hi, c_lo = j * 128, DP + j * 128
        ys[c_hi] = x_ref[:, c_hi:c_hi + 128] + w[:, 0:1] * hi0 + w[:, 1:2] * hi1
        ys[c_lo] = x_ref[:, c_lo:c_lo + 128] + w[:, 0:1] * lo0 + w[:, 1:2] * lo1
    if final:
        ms = sum(jnp.sum(y * y, axis=-1, keepdims=True) for y in ys.values()) * (1.0 / D)
        scale = lax.rsqrt(ms + RMS_EPS)
        ys = {c: y * scale * nrm_ref[:, c:c + 128] for c, y in ys.items()}
    for c, y in ys.items():
        o_ref[:, c:c + 128] = y


def _combine(gathered, x2, wgt, nrm, final):
    T = x2.shape[0]
    return pl.pallas_call(
        functools.partial(_combine_kernel, final),
        grid=(T // TD,),
        in_specs=[pl.BlockSpec((2, NPIECE, TD, 128), lambda i: (0, 0, i, 0)),
                  pl.BlockSpec((TD, D), lambda i: (i, 0)),
                  pl.BlockSpec((8, TD), lambda i: (0, i)),
                  pl.BlockSpec((1, D), lambda i: (0, 0))],
        out_specs=pl.BlockSpec((TD, D), lambda i: (i, 0)),
        out_shape=jax.ShapeDtypeStruct((T, D), jnp.float32),
        compiler_params=pltpu.CompilerParams(dimension_semantics=("arbitrary",)),
        name="combine",
    )(gathered, x2, wgt, nrm)


def _moe(layer, h2, logits, w1, w3, w2):
    T = logits.shape[0]
    idx, wgt, cnt = _route(logits)
    counts = cnt[:, 0].astype(jnp.int32)
    padded = (counts + BM - 1) // BM * BM
    padded_end = jnp.cumsum(padded)
    padded_start = padded_end - padded
    eids = jnp.arange(N_EXPERTS, dtype=jnp.int32)[:, None]
    start_of = lambda e: jnp.sum(jnp.where(e[None, :] == eids, padded_start[:, None], 0), axis=0)
    dest = jnp.stack([start_of(idx[0]) + idx[2], start_of(idx[1]) + idx[3]])
    n_blocks = (2 * T) // BM + N_EXPERTS
    block_start = jnp.arange(n_blocks, dtype=jnp.int32) * BM
    block_expert = jnp.minimum(
        jnp.sum((padded_end[None, :] <= block_start[:, None]).astype(jnp.int32), axis=1),
        N_EXPERTS - 1)
    n_used = (padded_end[-1] // BM).reshape(1)
    later = jnp.logical_and(eids.T > eids, (counts > 0)[None, :])
    next_of = jnp.min(jnp.where(later, eids.T, N_EXPERTS), axis=1)
    next_of = jnp.where(next_of < N_EXPERTS, next_of, -1)
    pick = lambda table: jnp.sum(
        jnp.where(block_expert[:, None] == eids.T, table[None, :], 0), axis=1)
    next_expert = pick(next_of)
    group_slot = pick(jnp.cumsum((counts > 0).astype(jnp.int32)) - 1) % 2
    block_valid = jnp.clip(pick(padded_start + counts) - block_start, 0, BM)
    n_slots = n_blocks * BM
    piece_base = (jnp.arange(NPIECE, dtype=jnp.int32) * n_slots)[None, :, None]
    rows = (dest[:, None, :] + piece_base).reshape(-1)
    xb = _sc_scatter_rows(h2.reshape(NPIECE * T, 128), rows, 2, NPIECE * n_slots)
    yb = _experts(layer, block_expert, n_used, next_expert, group_slot, block_valid,
                  xb.reshape(NPIECE, n_slots, 128), w1, w3, w2)
    gathered = _sc_gather_rows(yb.reshape(NPIECE * n_slots, 128), rows)
    return gathered.reshape(2, NPIECE, T, 128), wgt


def kernel(x, norm_mix, w_in, conv_a, w_a_out, conv_b, ln_b_g, ln_b_b, w_b_out, ln_c_g, ln_c_b,
           w_s, b_s, w_c_out, w_o, norm_ffn, w_group, b_group, w_router, b_router, w1, w3, w2,
           norm_final):
    bsz, seq, d = x.shape
    depth = norm_mix.shape[0]
    bf = jnp.bfloat16
    x2 = x.reshape(bsz * seq, d)
    wr = jnp.zeros((depth, d, NLOG), jnp.float32)
    wr = wr.at[:, :, 0:N_GROUPS].set(w_group).at[:, :, E_OFF:E_OFF + N_EXPERTS].set(w_router)
    wr_hi = wr.astype(bf)
    wr_lo = (wr - wr_hi.astype(jnp.float32)).astype(bf)
    br = jnp.zeros((depth, 1, NLOG), jnp.float32)
    br = br.at[:, 0, 0:N_GROUPS].set(b_group).at[:, 0, E_OFF:E_OFF + N_EXPERTS].set(b_router)
    bsb = jnp.broadcast_to(b_s[:, :, :, None], (depth, N_HEADS_C, CHUNK, CHUNK))
    row = lambda p: p[:, None, :]
    mixer_params = (row(norm_mix), w_in.astype(bf), conv_a, w_a_out.astype(bf), conv_b,
                    row(ln_b_g), row(ln_b_b), w_b_out.astype(bf), row(ln_c_g), row(ln_c_b),
                    w_s.astype(bf), bsb, w_c_out.astype(bf), w_o.astype(bf), row(norm_ffn), wr_hi, wr_lo, br)
    pending = None
    for l in range(depth):
        x2, h2, logits = _mixer(l, x2, pending, seq, *mixer_params)
        gathered, wgt = _moe(l, h2, logits, w1, w3, w2)
        pending = (gathered, wgt.T)
    out = _combine(gathered, x2, wgt, norm_final[None], True)
    return out.reshape(bsz, seq, d)
```

```python
import functools

import jax
import jax.numpy as jnp
from jax import lax
from jax.experimental import pallas as pl
from jax.experimental.pallas import tpu as pltpu
from jax.experimental.pallas import tpu_sc as plsc

D = 1024
DP = D // 2
NPIECE = DP // 128
N_HEADS_C = 8
CHUNK = 128
CONV_A = 3
CONV_B = 31
N_GROUPS = 4
EPG = 8
N_EXPERTS = N_GROUPS * EPG
D_FF = 512
RMS_EPS = 1e-6
LN_EPS = 1e-5

C_XA, C_BA, C_CA, C_VB, C_GB, C_UV, C_G = 0, 1024, 2048, 3072, 4096, 5120, 7168
D_IN = 10240

TM = 512
HALO = 16
TE = TM + 2 * HALO
CW = 256
RB = 64
NLOG = 128
E_OFF = 8

TR = 512
BM = 512
TD = 512

VMEM_LIMIT = 60 * 1024 * 1024


def _sigmoid(x):
    return 0.5 * (jnp.tanh(0.5 * x) + 1.0)


def _gelu_tanh(x):
    return 0.5 * x * (1.0 + jnp.tanh(0.7978845608028654 * (x + 0.044715 * (x * x * x))))


def _pack_bf16_pairs(x):
    c = x.shape[1] // 2
    as_bits = lambda v: lax.bitcast_convert_type(v.astype(jnp.bfloat16).astype(jnp.float32), jnp.uint32)
    return as_bits(x[:, :c]) | (as_bits(x[:, c:]) >> 16)


def _unpack_bf16_pairs(p):
    hi = lax.bitcast_convert_type(p & jnp.uint32(0xFFFF0000), jnp.float32)
    lo = lax.bitcast_convert_type(p << 16, jnp.float32)
    return hi, lo


def _layer_norm(x, g, b):
    mu = jnp.mean(x, axis=-1, keepdims=True)
    xc = x - mu
    var = jnp.mean(xc * xc, axis=-1, keepdims=True)
    return xc * lax.rsqrt(var + LN_EPS) * g + b


def _moe_residual(x_ref, g_ref, wt_ref):
    w0, w1 = wt_ref[:, 0:1], wt_ref[:, 1:2]
    his, los = [], []
    for j in range(NPIECE):
        hi0, lo0 = _unpack_bf16_pairs(lax.bitcast_convert_type(g_ref[0, j], jnp.uint32))
        hi1, lo1 = _unpack_bf16_pairs(lax.bitcast_convert_type(g_ref[1, j], jnp.uint32))
        his.append(w0 * hi0 + w1 * hi1)
        los.append(w0 * lo0 + w1 * lo1)
    return x_ref[...] + jnp.concatenate(his + los, axis=1)


def _mixer_kernel(seq_tiles, fused_combine, xp_ref, xc_ref, xn_ref, *refs):
    if fused_combine:
        gp_ref, gc_ref, gn_ref, wtp_ref, wtc_ref, wtn_ref = refs[:6]
        refs = refs[6:]
        x_prev = lambda: _moe_residual(xp_ref, gp_ref, wtp_ref)
        x_cur = lambda: _moe_residual(xc_ref, gc_ref, wtc_ref)
        x_next = lambda: _moe_residual(xn_ref, gn_ref, wtn_ref)
    else:
        x_prev, x_cur, x_next = (lambda: xp_ref[...]), (lambda: xc_ref[...]), (lambda: xn_ref[...])
    (nm_ref, win_ref, ca_ref, wa_ref, cb_ref, lnbg_ref, lnbb_ref, wb_ref, lncg_ref, lncb_ref,
     ws_ref, bsb_ref, wc_ref, wo_ref, nf_ref, wrh_ref, wrl_ref, br_ref,
     xo_ref, h2_ref, lg_ref,
     hb_ref, zq_ref, cva_ref, cvb_ref, cvn_ref, vst_ref, mg_ref) = refs
    i = pl.program_id(0)
    at_start = (i % seq_tiles) == 0
    at_end = (i % seq_tiles) == seq_tiles - 1

    def _rms(xv):
        ms = jnp.mean(xv * xv, axis=-1, keepdims=True)
        return (xv * lax.rsqrt(ms + RMS_EPS) * nm_ref[...]).astype(jnp.bfloat16)

    hb_ref[0:HALO, :] = _rms(x_prev())
    hb_ref[HALO:HALO + TM, :] = _rms(x_cur())
    hb_ref[HALO + TM:TE, :] = _rms(x_next())

    rows = lax.broadcasted_iota(jnp.int32, (TE, 1), 0)
    lo = jnp.where(at_start, HALO, 0)
    hi = jnp.where(at_end, HALO + TM, TE)
    valid = jnp.logical_and(rows >= lo, rows < hi)

    def proj(r0, r1, c0, width):
        return jnp.dot(hb_ref[r0:r1, :], win_ref[:, c0:c0 + width],
                       preferred_element_type=jnp.float32)

    def b_proj(c):
        c0 = c * CW
        val = proj(0, TE, C_VB + c0, CW)
        gate = proj(0, TE, C_GB + c0, CW)
        zq_ref[c % 2] = jnp.where(valid, val * _sigmoid(gate), 0.0)

    def a_proj(c):
        c0 = c * CW
        xa = proj(0, TE, C_XA + c0, CW)
        cc = proj(0, TE, C_CA + c0, CW)
        t = jnp.where(valid, xa * cc, 0.0)
        conv = (ca_ref[0:1, c0:c0 + CW] * t[HALO - 1:HALO - 1 + TM]
                + ca_ref[1:2, c0:c0 + CW] * t[HALO:HALO + TM]
                + ca_ref[2:3, c0:c0 + CW] * t[HALO + 1:HALO + 1 + TM])
        ba = proj(HALO, HALO + TM, C_BA + c0, CW)
        cva_ref[:, c0:c0 + CW] = (ba * conv).astype(jnp.bfloat16)

    def c_v_proj(c):
        c0 = c * CW
        vst_ref[:, c0:c0 + CW] = _gelu_tanh(proj(HALO, HALO + TM, C_UV + D + c0, CW))

    def b_conv(c, row_blocks):
        c0 = c * CW
        zq = zq_ref.at[c % 2]
        for rb in row_blocks:
            r0 = rb * RB
            for lt in range(CW // 128):
                l0 = lt * 128
                acc = None
                for s in range(8):
                    part = None
                    for q in range(4):
                        k = 8 * q + s - (HALO - CONV_B // 2)
                        if 0 <= k < CONV_B:
                            term = (cb_ref[k:k + 1, c0 + l0:c0 + l0 + 128]
                                    * zq[r0 + 8 * q:r0 + 8 * q + RB + 8, l0:l0 + 128])
                            part = term if part is None else part + term
                    shifted = part[s:s + RB]
                    acc = shifted if acc is None else acc + shifted
                xo_ref[r0:r0 + RB, c0 + l0:c0 + l0 + 128] = acc

    n_chunks = D // CW
    n_rb = TM // RB
    rb_step = n_rb // n_chunks
    early = n_chunks // 2
    b_proj(0)
    for c in range(n_chunks):
        if c + 1 < n_chunks:
            b_proj(c + 1)
        a_proj(c)
        c_v_proj(c)
        if c < early:
            b_conv(c, range(n_rb))

    vn = _layer_norm(vst_ref[...], lncg_ref[...], lncb_ref[...])
    cvn_ref[...] = vn.astype(jnp.bfloat16)
    for c in range(n_chunks):
        c0 = c * CW
        ya = jnp.dot(cva_ref[...], wa_ref[:, c0:c0 + CW], preferred_element_type=jnp.float32)
        ga = _sigmoid(proj(HALO, HALO + TM, C_G + c0, CW))
        mg_ref[:, c0:c0 + CW] = ga * ya
        b_conv(early, range(c * rb_step, (c + 1) * rb_step))

    for c in range(n_chunks):
        c0 = c * CW
        u = _gelu_tanh(proj(HALO, HALO + TM, C_UV + c0, CW))
        for n in range(TM // CHUNK):
            for hh in range(CW // CHUNK):
                h = c * (CW // CHUNK) + hh
                sv = jnp.dot(ws_ref[h], cvn_ref[n * CHUNK:(n + 1) * CHUNK, h * CHUNK:(h + 1) * CHUNK],
                             preferred_element_type=jnp.float32) + bsb_ref[h]
                cva_ref[n * CHUNK:(n + 1) * CHUNK, h * CHUNK:(h + 1) * CHUNK] = (
                    u[n * CHUNK:(n + 1) * CHUNK, hh * CHUNK:(hh + 1) * CHUNK] * sv
                ).astype(jnp.bfloat16)
        for cc in range(early + 1, n_chunks):
            b_conv(cc, range(c * rb_step, (c + 1) * rb_step))
    for c in range(n_chunks):
        c0 = c * CW
        yc = jnp.dot(cva_ref[...], wc_ref[:, c0:c0 + CW], preferred_element_type=jnp.float32)
        gc = _sigmoid(proj(HALO, HALO + TM, C_G + 2 * D + c0, CW))
        mg_ref[:, c0:c0 + CW] += gc * yc

    zc = _layer_norm(xo_ref[...], lnbg_ref[...], lnbb_ref[...])
    cvb_ref[...] = (zc * _sigmoid(zc)).astype(jnp.bfloat16)
    for c in range(n_chunks):
        c0 = c * CW
        yb = jnp.dot(cvb_ref[...], wb_ref[:, c0:c0 + CW], preferred_element_type=jnp.float32)
        gb = _sigmoid(proj(HALO, HALO + TM, C_G + D + c0, CW))
        mg_ref[:, c0:c0 + CW] += gb * yb

    cvb_ref[...] = mg_ref[...].astype(jnp.bfloat16)
    x_res = x_cur()
    for c in range(n_chunks):
        c0 = c * CW
        xo_ref[:, c0:c0 + CW] = x_res[:, c0:c0 + CW] + jnp.dot(
            cvb_ref[...], wo_ref[:, c0:c0 + CW], preferred_element_type=jnp.float32)
    xnew = xo_ref[...]
    ms = jnp.mean(xnew * xnew, axis=-1, keepdims=True)
    h2 = xnew * lax.rsqrt(ms + RMS_EPS) * nf_ref[...]
    h2_packed = _pack_bf16_pairs(h2)
    for j in range(NPIECE):
        h2_ref[j] = lax.bitcast_convert_type(h2_packed[:, j * 128:(j + 1) * 128], jnp.int32)
    h_hi = h2.astype(jnp.bfloat16)
    h_lo = (h2 - h_hi.astype(jnp.float32)).astype(jnp.bfloat16)
    dot32 = functools.partial(jnp.dot, preferred_element_type=jnp.float32)
    lg_ref[...] = (dot32(h_hi, wrh_ref[...]) + dot32(h_lo, wrh_ref[...])
                   + dot32(h_hi, wrl_ref[...]) + dot32(h_lo, wrl_ref[...]) + br_ref[...])


def _layer_spec(layer, shape):
    nd = len(shape)
    return pl.BlockSpec((None,) + tuple(shape), lambda i, _n=nd: (layer,) + (0,) * _n,
                        pipeline_mode=pl.Buffered(1))


def _mixer(layer, x2, pending, seq_len, nm, win, ca, wa, cb, lnbg, lnbb, wb, lncg, lncb, ws, bsb, wc, wo,
           nf, wrh, wrl, br):
    T = x2.shape[0]
    n_tiles = T // TM
    hb = TM // HALO
    last_halo = T // HALO - 1
    ls = functools.partial(_layer_spec, layer)
    prev_rows = lambda i: jnp.maximum(i * hb - 1, 0)
    next_rows = lambda i: jnp.minimum((i + 1) * hb, last_halo)
    in_specs = [
        pl.BlockSpec((HALO, D), lambda i: (prev_rows(i), 0)),
        pl.BlockSpec((TM, D), lambda i: (i, 0)),
        pl.BlockSpec((HALO, D), lambda i: (next_rows(i), 0)),
    ]
    operands = [x2, x2, x2]
    if pending is not None:
        gathered, wgt_rows = pending
        in_specs += [
            pl.BlockSpec((2, NPIECE, HALO, 128), lambda i: (0, 0, prev_rows(i), 0)),
            pl.BlockSpec((2, NPIECE, TM, 128), lambda i: (0, 0, i, 0)),
            pl.BlockSpec((2, NPIECE, HALO, 128), lambda i: (0, 0, next_rows(i), 0)),
            pl.BlockSpec((HALO, 8), lambda i: (prev_rows(i), 0)),
            pl.BlockSpec((TM, 8), lambda i: (i, 0)),
            pl.BlockSpec((HALO, 8), lambda i: (next_rows(i), 0)),
        ]
        operands += [gathered, gathered, gathered, wgt_rows, wgt_rows, wgt_rows]
    in_specs += [
        ls((1, D)), ls((D, D_IN)), ls((CONV_A, D)), ls((D, D)),
        ls((CONV_B, D)), ls((1, D)), ls((1, D)), ls((D, D)),
        ls((1, D)), ls((1, D)), ls((N_HEADS_C, CHUNK, CHUNK)),
        ls((N_HEADS_C, CHUNK, CHUNK)), ls((D, D)), ls((D, D)),
        ls((1, D)), ls((D, NLOG)), ls((D, NLOG)), ls((1, NLOG)),
    ]
    out_specs = [
        pl.BlockSpec((TM, D), lambda i: (i, 0)),
        pl.BlockSpec((NPIECE, TM, 128), lambda i: (0, i, 0)),
        pl.BlockSpec((TM, NLOG), lambda i: (i, 0)),
    ]
    return pl.pallas_call(
        functools.partial(_mixer_kernel, seq_len // TM, pending is not None),
        grid=(n_tiles,),
        in_specs=in_specs,
        out_specs=out_specs,
        out_shape=[jax.ShapeDtypeStruct((T, D), jnp.float32),
                   jax.ShapeDtypeStruct((NPIECE, T, 128), jnp.int32),
                   jax.ShapeDtypeStruct((T, NLOG), jnp.float32)],
        scratch_shapes=[pltpu.VMEM((TE, D), jnp.bfloat16),
                        pltpu.VMEM((2, TE, CW), jnp.float32),
                        pltpu.VMEM((TM, D), jnp.bfloat16),
                        pltpu.VMEM((TM, D), jnp.bfloat16),
                        pltpu.VMEM((TM, D), jnp.bfloat16),
                        pltpu.VMEM((TM, D), jnp.float32),
                        pltpu.VMEM((TM, D), jnp.float32)],
        compiler_params=pltpu.CompilerParams(dimension_semantics=("arbitrary",),
                                             vmem_limit_bytes=VMEM_LIMIT),
        name="mixer",
    )(*operands, nm, win, ca, wa, cb, lnbg, lnbb, wb, lncg, lncb, ws, bsb, wc, wo, nf, wrh, wrl, br)


def _route_kernel(lg_ref, idx_ref, wgt_ref, cnt_ref, carry_ref):
    i = pl.program_id(0)

    @pl.when(i == 0)
    def _():
        carry_ref[...] = jnp.zeros_like(carry_ref)

    lt = lg_ref[...].T
    g = [lt[j:j + 1, :] for j in range(N_GROUPS)]
    gmax = jnp.maximum(jnp.maximum(g[0], g[1]), jnp.maximum(g[2], g[3]))
    gidx = jnp.where(g[0] == gmax, 0.0, jnp.where(g[1] == gmax, 1.0, jnp.where(g[2] == gmax, 2.0, 3.0)))
    gsum = sum(jnp.exp(gj - gmax) for gj in g)
    g_p = 1.0 / gsum

    sel = lt[E_OFF + 3 * EPG:E_OFF + 4 * EPG, :]
    for j in (2, 1, 0):
        sel = jnp.where(gidx == float(j), lt[E_OFF + j * EPG:E_OFF + (j + 1) * EPG, :], sel)
    rid = lax.broadcasted_iota(jnp.int32, (EPG, TR), 0).astype(jnp.float32)
    m1 = jnp.max(sel, axis=0, keepdims=True)
    i1 = jnp.min(jnp.where(sel == m1, rid, float(EPG)), axis=0, keepdims=True)
    rest = jnp.where(rid == i1, -jnp.inf, sel)
    m2 = jnp.max(rest, axis=0, keepdims=True)
    i2 = jnp.min(jnp.where(rest == m2, rid, float(EPG)), axis=0, keepdims=True)
    e2x = jnp.exp(m2 - m1)
    den = 1.0 + e2x
    w1 = (1.0 / den) * g_p
    w2 = (e2x / den) * g_p
    e1 = gidx * float(EPG) + i1
    e2 = gidx * float(EPG) + i2

    eid = lax.broadcasted_iota(jnp.int32, (N_EXPERTS, TR), 0).astype(jnp.float32)
    oh1 = (eid == e1).astype(jnp.float32)
    oh2 = (eid == e2).astype(jnp.float32)
    oh = oh1 + oh2
    tr = lax.broadcasted_iota(jnp.int32, (TR, TR), 0)
    tc = lax.broadcasted_iota(jnp.int32, (TR, TR), 1)
    upper = (tr < tc).astype(jnp.bfloat16)
    before = jnp.dot(oh.astype(jnp.bfloat16), upper, preferred_element_type=jnp.float32)
    base = before + carry_ref[:, 0:1]
    r1 = jnp.sum(oh1 * base, axis=0, keepdims=True)
    r2 = jnp.sum(oh2 * base, axis=0, keepdims=True)
    carry_ref[...] = carry_ref[...] + jnp.sum(oh, axis=1, keepdims=True)

    idx_ref[...] = jnp.zeros_like(idx_ref)
    idx_ref[0:1, :] = e1.astype(jnp.int32)
    idx_ref[1:2, :] = e2.astype(jnp.int32)
    idx_ref[2:3, :] = r1.astype(jnp.int32)
    idx_ref[3:4, :] = r2.astype(jnp.int32)
    wgt_ref[...] = jnp.zeros_like(wgt_ref)
    wgt_ref[0:1, :] = w1
    wgt_ref[1:2, :] = w2
    cnt_ref[...] = carry_ref[...]


def _route(logits):
    T = logits.shape[0]
    return pl.pallas_call(
        _route_kernel,
        grid=(T // TR,),
        in_specs=[pl.BlockSpec((TR, NLOG), lambda i: (i, 0))],
        out_specs=[pl.BlockSpec((8, TR), lambda i: (0, i)),
                   pl.BlockSpec((8, TR), lambda i: (0, i)),
                   pl.BlockSpec((N_EXPERTS, 128), lambda i: (0, 0))],
        out_shape=[jax.ShapeDtypeStruct((8, T), jnp.int32),
                   jax.ShapeDtypeStruct((8, T), jnp.float32),
                   jax.ShapeDtypeStruct((N_EXPERTS, 128), jnp.float32)],
        scratch_shapes=[pltpu.VMEM((N_EXPERTS, 128), jnp.float32)],
        compiler_params=pltpu.CompilerParams(dimension_semantics=("arbitrary",)),
        name="route",
    )(logits)


SC_WINDOW = 128


def _sc_mesh():
    return plsc.VectorSubcoreMesh(core_axis_name="c", subcore_axis_name="s")


def _sc_scatter_rows(rows, idx, passes, n_out):
    n_src = rows.shape[0]
    src_blocks = n_src // SC_WINDOW

    @functools.partial(
        pl.kernel, mesh=_sc_mesh(),
        out_type=jax.ShapeDtypeStruct((n_out, 128), rows.dtype))
    def scatter(rows_hbm, idx_hbm, out_hbm):
        def body(rows_vmem, idx_vmem):
            pltpu.sync_copy(rows_vmem, out_hbm.at[idx_vmem.at[0]])

        pltpu.emit_pipeline(
            body,
            grid=(passes * src_blocks,),
            in_specs=[pl.BlockSpec((SC_WINDOW, 128), lambda i: (i % src_blocks, 0)),
                      pl.BlockSpec((1, SC_WINDOW), lambda i: (0, i))],
            out_specs=[],
            core_axis_name=("c", "s"),
            dimension_semantics=(pltpu.PARALLEL,),
        )(rows_hbm, idx_hbm)

    return scatter(rows, idx.reshape(1, passes * n_src))


def _expert_kernel(layer, be_ref, nu_ref, nxt_ref, gs_ref, nv_ref, xb_ref, w1_ref, w3_ref, w2_ref, yb_ref,
                   wf1_ref, wf3_ref, wf2_ref, w1b_ref, w3b_ref, w2b_ref, sem):
    b = pl.program_id(0)
    used = b < nu_ref[0]
    new_expert = jnp.logical_or(b == 0, be_ref[b] != be_ref[jnp.maximum(b - 1, 0)])

    def weight_copies(e, s):
        return [pltpu.make_async_copy(src.at[layer, e], dst.at[s], sem.at[s])
                for src, dst in ((w1_ref, wf1_ref), (w3_ref, wf3_ref), (w2_ref, wf2_ref))]

    @pl.when(b == 0)
    def _():
        for cp in weight_copies(be_ref[0], 0):
            cp.start(priority=1)

    @pl.when(jnp.logical_and(used, new_expert))
    def _():
        s = gs_ref[b]
        for cp in weight_copies(be_ref[b], s):
            cp.wait()

        @pl.when(nxt_ref[b] >= 0)
        def _():
            for cp in weight_copies(nxt_ref[b], 1 - s):
                cp.start(priority=1)

        w1b_ref[...] = wf1_ref[s].astype(jnp.bfloat16)
        w3b_ref[...] = wf3_ref[s].astype(jnp.bfloat16)
        w2b_ref[...] = wf2_ref[s].astype(jnp.bfloat16)

    @pl.when(used)
    def _():
        live = lax.broadcasted_iota(jnp.int32, (BM, 1), 0) < nv_ref[b]
        packed_in = jnp.concatenate(
            [jnp.where(live, lax.bitcast_convert_type(xb_ref[j], jnp.uint32), jnp.uint32(0))
             for j in range(NPIECE)], axis=1)
        x_hi, x_lo = (v.astype(jnp.bfloat16) for v in _unpack_bf16_pairs(packed_in))
        dot32 = functools.partial(jnp.dot, preferred_element_type=jnp.float32)
        a = dot32(x_hi, w1b_ref[0:DP, :]) + dot32(x_lo, w1b_ref[DP:D, :])
        g = dot32(x_hi, w3b_ref[0:DP, :]) + dot32(x_lo, w3b_ref[DP:D, :])
        hmid = (a * _sigmoid(a) * g).astype(jnp.bfloat16)
        packed = _pack_bf16_pairs(dot32(hmid, w2b_ref[...]))
        for j in range(NPIECE):
            yb_ref[j] = lax.bitcast_convert_type(packed[:, j * 128:(j + 1) * 128], jnp.int32)

    @pl.when(jnp.logical_not(used))
    def _():
        yb_ref[...] = jnp.zeros_like(yb_ref)


def _experts(layer, block_expert, n_used, next_expert, group_slot, block_valid, xb, w1, w3, w2):
    n_slots = xb.shape[1]
    n_blocks = n_slots // BM

    def row_map(b, be, nu, nxt, gs, nv):
        return (0, jnp.minimum(b, nu[0] - 1), 0)

    def out_map(b, be, nu, nxt, gs, nv):
        return (0, b, 0)

    grid_spec = pltpu.PrefetchScalarGridSpec(
        num_scalar_prefetch=5,
        grid=(n_blocks,),
        in_specs=[pl.BlockSpec((NPIECE, BM, 128), row_map),
                  pl.BlockSpec(memory_space=pl.ANY),
                  pl.BlockSpec(memory_space=pl.ANY),
                  pl.BlockSpec(memory_space=pl.ANY)],
        out_specs=pl.BlockSpec((NPIECE, BM, 128), out_map),
        scratch_shapes=[pltpu.VMEM((2, D, D_FF), jnp.float32),
                        pltpu.VMEM((2, D, D_FF), jnp.float32),
                        pltpu.VMEM((2, D_FF, D), jnp.float32),
                        pltpu.VMEM((D, D_FF), jnp.bfloat16),
                        pltpu.VMEM((D, D_FF), jnp.bfloat16),
                        pltpu.VMEM((D_FF, D), jnp.bfloat16),
                        pltpu.SemaphoreType.DMA((2,))],
    )
    return pl.pallas_call(
        functools.partial(_expert_kernel, layer),
        grid_spec=grid_spec,
        out_shape=jax.ShapeDtypeStruct((NPIECE, n_slots, 128), jnp.int32),
        compiler_params=pltpu.CompilerParams(dimension_semantics=("arbitrary",),
                                             vmem_limit_bytes=VMEM_LIMIT),
        name="experts",
    )(block_expert, n_used, next_expert, group_slot, block_valid, xb, w1, w3, w2)


def _sc_gather_rows(table, idx):
    n_rows = idx.shape[0]

    @functools.partial(
        pl.kernel, mesh=_sc_mesh(),
        out_type=jax.ShapeDtypeStruct((n_rows, 128), table.dtype))
    def gather(table_hbm, idx_hbm, out_hbm):
        def body(idx_vmem, out_vmem):
            pltpu.sync_copy(table_hbm.at[idx_vmem.at[0]], out_vmem)

        pltpu.emit_pipeline(
            body,
            grid=(n_rows // SC_WINDOW,),
            in_specs=[pl.BlockSpec((1, SC_WINDOW), lambda i: (0, i))],
            out_specs=[pl.BlockSpec((SC_WINDOW, 128), lambda i: (i, 0))],
            core_axis_name=("c", "s"),
            dimension_semantics=(pltpu.PARALLEL,),
        )(idx_hbm, out_hbm)

    return gather(table, idx.reshape(1, n_rows))


def _combine_kernel(final, g_ref, x_ref, w_ref, nrm_ref, o_ref):
    w = w_ref[...].T
    ys = {}
    for j in range(NPIECE):
        hi0, lo0 = _unpack_bf16_pairs(lax.bitcast_convert_type(g_ref[0, j], jnp.uint32))
        hi1, lo1 = _unpack_bf16_pairs(lax.bitcast_convert_type(g_ref[1, j], jnp.uint32))
        c_hi, c_lo = j * 128, DP + j * 128
        ys[c_hi] = x_ref[:, c_hi:c_hi + 128] + w[:, 0:1] * hi0 + w[:, 1:2] * hi1
        ys[c_lo] = x_ref[:, c_lo:c_lo + 128] + w[:, 0:1] * lo0 + w[:, 1:2] * lo1
    if final:
        ms = sum(jnp.sum(y * y, axis=-1, keepdims=True) for y in ys.values()) * (1.0 / D)
        scale = lax.rsqrt(ms + RMS_EPS)
        ys = {c: y * scale * nrm_ref[:, c:c + 128] for c, y in ys.items()}
    for c, y in ys.items():
        o_ref[:, c:c + 128] = y


def _combine(gathered, x2, wgt, nrm, final):
    T = x2.shape[0]
    return pl.pallas_call(
        functools.partial(_combine_kernel, final),
        grid=(T // TD,),
        in_specs=[pl.BlockSpec((2, NPIECE, TD, 128), lambda i: (0, 0, i, 0)),
                  pl.BlockSpec((TD, D), lambda i: (i, 0)),
                  pl.BlockSpec((8, TD), lambda i: (0, i)),
                  pl.BlockSpec((1, D), lambda i: (0, 0))],
        out_specs=pl.BlockSpec((TD, D), lambda i: (i, 0)),
        out_shape=jax.ShapeDtypeStruct((T, D), jnp.float32),
        compiler_params=pltpu.CompilerParams(dimension_semantics=("arbitrary",)),
        name="combine",
    )(gathered, x2, wgt, nrm)


def _moe(layer, h2, logits, w1, w3, w2):
    T = logits.shape[0]
    idx, wgt, cnt = _route(logits)
    counts = cnt[:, 0].astype(jnp.int32)
    padded = (counts + BM - 1) // BM * BM
    padded_end = jnp.cumsum(padded)
    padded_start = padded_end - padded
    eids = jnp.arange(N_EXPERTS, dtype=jnp.int32)[:, None]
    start_of = lambda e: jnp.sum(jnp.where(e[None, :] == eids, padded_start[:, None], 0), axis=0)
    dest = jnp.stack([start_of(idx[0]) + idx[2], start_of(idx[1]) + idx[3]])
    n_blocks = (2 * T) // BM + N_EXPERTS
    block_start = jnp.arange(n_blocks, dtype=jnp.int32) * BM
    block_expert = jnp.minimum(
        jnp.sum((padded_end[None, :] <= block_start[:, None]).astype(jnp.int32), axis=1),
        N_EXPERTS - 1)
    n_used = (padded_end[-1] // BM).reshape(1)
    later = jnp.logical_and(eids.T > eids, (counts > 0)[None, :])
    next_of = jnp.min(jnp.where(later, eids.T, N_EXPERTS), axis=1)
    next_of = jnp.where(next_of < N_EXPERTS, next_of, -1)
    pick = lambda table: jnp.sum(
        jnp.where(block_expert[:, None] == eids.T, table[None, :], 0), axis=1)
    next_expert = pick(next_of)
    group_slot = pick(jnp.cumsum((counts > 0).astype(jnp.int32)) - 1) % 2
    block_valid = jnp.clip(pick(padded_start + counts) - block_start, 0, BM)
    n_slots = n_blocks * BM
    piece_base = (jnp.arange(NPIECE, dtype=jnp.int32) * n_slots)[None, :, None]
    rows = (dest[:, None, :] + piece_base).reshape(-1)
    xb = _sc_scatter_rows(h2.reshape(NPIECE * T, 128), rows, 2, NPIECE * n_slots)
    yb = _experts(layer, block_expert, n_used, next_expert, group_slot, block_valid,
                  xb.reshape(NPIECE, n_slots, 128), w1, w3, w2)
    gathered = _sc_gather_rows(yb.reshape(NPIECE * n_slots, 128), rows)
    return gathered.reshape(2, NPIECE, T, 128), wgt


def kernel(x, norm_mix, w_in, conv_a, w_a_out, conv_b, ln_b_g, ln_b_b, w_b_out, ln_c_g, ln_c_b,
           w_s, b_s, w_c_out, w_o, norm_ffn, w_group, b_group, w_router, b_router, w1, w3, w2,
           norm_final):
    bsz, seq, d = x.shape
    depth = norm_mix.shape[0]
    bf = jnp.bfloat16
    x2 = x.reshape(bsz * seq, d)
    wr = jnp.zeros((depth, d, NLOG), jnp.float32)
    wr = wr.at[:, :, 0:N_GROUPS].set(w_group).at[:, :, E_OFF:E_OFF + N_EXPERTS].set(w_router)
    wr_hi = wr.astype(bf)
    wr_lo = (wr - wr_hi.astype(jnp.float32)).astype(bf)
    br = jnp.zeros((depth, 1, NLOG), jnp.float32)
    br = br.at[:, 0, 0:N_GROUPS].set(b_group).at[:, 0, E_OFF:E_OFF + N_EXPERTS].set(b_router)
    bsb = jnp.broadcast_to(b_s[:, :, :, None], (depth, N_HEADS_C, CHUNK, CHUNK))
    row = lambda p: p[:, None, :]
    mixer_params = (row(norm_mix), w_in.astype(bf), conv_a, w_a_out.astype(bf), conv_b,
                    row(ln_b_g), row(ln_b_b), w_b_out.astype(bf), row(ln_c_g), row(ln_c_b),
                    w_s.astype(bf), bsb, w_c_out.astype(bf), w_o.astype(bf), row(norm_ffn), wr_hi, wr_lo, br)
    pending = None
    for l in range(depth):
        x2, h2, logits = _mixer(l, x2, pending, seq, *mixer_params)
        gathered, wgt = _moe(l, h2, logits, w1, w3, w2)
        pending = (gathered, wgt.T)
    out = _combine(gathered, x2, wgt, norm_final[None], True)
    return out.reshape(bsz, seq, d)
```

```python
import functools

import jax
import jax.numpy as jnp
from jax import lax
from jax.experimental import pallas as pl
from jax.experimental.pallas import tpu as pltpu
from jax.experimental.pallas import tpu_sc as plsc

D = 1024
DP = D // 2
NPIECE = DP // 128
N_HEADS_C = 8
CHUNK = 128
CONV_A = 3
CONV_B = 31
N_GROUPS = 4
EPG = 8
N_EXPERTS = N_GROUPS * EPG
D_FF = 512
RMS_EPS = 1e-6
LN_EPS = 1e-5

C_XA, C_BA, C_CA, C_VB, C_GB, C_UV, C_G = 0, 1024, 2048, 3072, 4096, 5120, 7168
D_IN = 10240

TM = 512
HALO = 16
TE = TM + 2 * HALO
CW = 256
RB = 64
NLOG = 128
E_OFF = 8

TR = 512
BM = 512
TD = 512

VMEM_LIMIT = 60 * 1024 * 1024


def _sigmoid(x):
    return 0.5 * (jnp.tanh(0.5 * x) + 1.0)


def _gelu_tanh(x):
    return 0.5 * x * (1.0 + jnp.tanh(0.7978845608028654 * (x + 0.044715 * (x * x * x))))


def _pack_bf16_pairs(x):
    c = x.shape[1] // 2
    as_bits = lambda v: lax.bitcast_convert_type(v.astype(jnp.bfloat16).astype(jnp.float32), jnp.uint32)
    return as_bits(x[:, :c]) | (as_bits(x[:, c:]) >> 16)


def _unpack_bf16_pairs(p):
    hi = lax.bitcast_convert_type(p & jnp.uint32(0xFFFF0000), jnp.float32)
    lo = lax.bitcast_convert_type(p << 16, jnp.float32)
    return hi, lo


def _layer_norm(x, g, b):
    mu = jnp.mean(x, axis=-1, keepdims=True)
    xc = x - mu
    var = jnp.mean(xc * xc, axis=-1, keepdims=True)
    return xc * lax.rsqrt(var + LN_EPS) * g + b


def _moe_residual(x_ref, g_ref, wt_ref):
    w0, w1 = wt_ref[:, 0:1], wt_ref[:, 1:2]
    his, los = [], []
    for j in range(NPIECE):
        hi0, lo0 = _unpack_bf16_pairs(lax.bitcast_convert_type(g_ref[0, j], jnp.uint32))
        hi1, lo1 = _unpack_bf16_pairs(lax.bitcast_convert_type(g_ref[1, j], jnp.uint32))
        his.append(w0 * hi0 + w1 * hi1)
        los.append(w0 * lo0 + w1 * lo1)
    return x_ref[...] + jnp.concatenate(his + los, axis=1)


def _mixer_kernel(seq_tiles, fused_combine, xp_ref, xc_ref, xn_ref, *refs):
    if fused_combine:
        gp_ref, gc_ref, gn_ref, wtp_ref, wtc_ref, wtn_ref = refs[:6]
        refs = refs[6:]
        x_prev = lambda: _moe_residual(xp_ref, gp_ref, wtp_ref)
        x_cur = lambda: _moe_residual(xc_ref, gc_ref, wtc_ref)
        x_next = lambda: _moe_residual(xn_ref, gn_ref, wtn_ref)
    else:
        x_prev, x_cur, x_next = (lambda: xp_ref[...]), (lambda: xc_ref[...]), (lambda: xn_ref[...])
    (nm_ref, win_ref, ca_ref, wa_ref, cb_ref, lnbg_ref, lnbb_ref, wb_ref, lncg_ref, lncb_ref,
     ws_ref, bsb_ref, wc_ref, wo_ref, nf_ref, wrh_ref, wrl_ref, br_ref,
     xo_ref, h2_ref, lg_ref,
     hb_ref, zq_ref, cva_ref, cvb_ref, cvn_ref, vst_ref, mg_ref) = refs
    i = pl.program_id(0)
    at_start = (i % seq_tiles) == 0
    at_end = (i % seq_tiles) == seq_tiles - 1

    def _rms(xv):
        ms = jnp.mean(xv * xv, axis=-1, keepdims=True)
        return (xv * lax.rsqrt(ms + RMS_EPS) * nm_ref[...]).astype(jnp.bfloat16)

    hb_ref[0:HALO, :] = _rms(x_prev())
    hb_ref[HALO:HALO + TM, :] = _rms(x_cur())
    hb_ref[HALO + TM:TE, :] = _rms(x_next())

    rows = lax.broadcasted_iota(jnp.int32, (TE, 1), 0)
    lo = jnp.where(at_start, HALO, 0)
    hi = jnp.where(at_end, HALO + TM, TE)
    valid = jnp.logical_and(rows >= lo, rows < hi)

    def proj(r0, r1, c0, width):
        return jnp.dot(hb_ref[r0:r1, :], win_ref[:, c0:c0 + width],
                       preferred_element_type=jnp.float32)

    def b_proj(c):
        c0 = c * CW
        val = proj(0, TE, C_VB + c0, CW)
        gate = proj(0, TE, C_GB + c0, CW)
        zq_ref[c] = jnp.where(valid, val * _sigmoid(gate), 0.0)

    def a_proj(c):
        c0 = c * CW
        xa = proj(0, TE, C_XA + c0, CW)
        cc = proj(0, TE, C_CA + c0, CW)
        t = jnp.where(valid, xa * cc, 0.0)
        conv = (ca_ref[0:1, c0:c0 + CW] * t[HALO - 1:HALO - 1 + TM]
                + ca_ref[1:2, c0:c0 + CW] * t[HALO:HALO + TM]
                + ca_ref[2:3, c0:c0 + CW] * t[HALO + 1:HALO + 1 + TM])
        ba = proj(HALO, HALO + TM, C_BA + c0, CW)
        cva_ref[:, c0:c0 + CW] = (ba * conv).astype(jnp.bfloat16)

    def c_v_proj(c):
        c0 = c * CW
        vst_ref[:, c0:c0 + CW] = _gelu_tanh(proj(HALO, HALO + TM, C_UV + D + c0, CW))

    def b_conv(c, row_blocks):
        c0 = c * CW
        zq = zq_ref.at[c]
        for rb in row_blocks:
            r0 = rb * RB
            for lt in range(CW // 128):
                l0 = lt * 128
                acc = None
                for s in range(8):
                    part = None
                    for q in range(4):
                        k = 8 * q + s - (HALO - CONV_B // 2)
                        if 0 <= k < CONV_B:
                            term = (cb_ref[k:k + 1, c0 + l0:c0 + l0 + 128]
                                    * zq[r0 + 8 * q:r0 + 8 * q + RB + 8, l0:l0 + 128])
                            part = term if part is None else part + term
                    shifted = part[s:s + RB]
                    acc = shifted if acc is None else acc + shifted
                xo_ref[r0:r0 + RB, c0 + l0:c0 + l0 + 128] = acc

    n_chunks = D // CW
    n_rb = TM // RB
    rb_step = n_rb // n_chunks
    late_rows = lambda c: range(c * rb_step, (c + 1) * rb_step)
    b_proj(0)
    for c in range(n_chunks):
        if c + 1 < n_chunks:
            b_proj(c + 1)
        a_proj(c)
        c_v_proj(c)
    b_conv(0, range(n_rb))

    vn = _layer_norm(vst_ref[...], lncg_ref[...], lncb_ref[...])
    cvn_ref[...] = vn.astype(jnp.bfloat16)
    for c in range(n_chunks):
        c0 = c * CW
        ya = jnp.dot(cva_ref[...], wa_ref[:, c0:c0 + CW], preferred_element_type=jnp.float32)
        ga = _sigmoid(proj(HALO, HALO + TM, C_G + c0, CW))
        mg_ref[:, c0:c0 + CW] = ga * ya
        b_conv(1, late_rows(c))

    for c in range(n_chunks):
        c0 = c * CW
        u = _gelu_tanh(proj(HALO, HALO + TM, C_UV + c0, CW))
        for n in range(TM // CHUNK):
            for hh in range(CW // CHUNK):
                h = c * (CW // CHUNK) + hh
                sv = jnp.dot(ws_ref[h], cvn_ref[n * CHUNK:(n + 1) * CHUNK, h * CHUNK:(h + 1) * CHUNK],
                             preferred_element_type=jnp.float32) + bsb_ref[h]
                cva_ref[n * CHUNK:(n + 1) * CHUNK, h * CHUNK:(h + 1) * CHUNK] = (
                    u[n * CHUNK:(n + 1) * CHUNK, hh * CHUNK:(hh + 1) * CHUNK] * sv
                ).astype(jnp.bfloat16)
        b_conv(2, late_rows(c))
    for c in range(n_chunks):
        c0 = c * CW
        yc = jnp.dot(cva_ref[...], wc_ref[:, c0:c0 + CW], preferred_element_type=jnp.float32)
        gc = _sigmoid(proj(HALO, HALO + TM, C_G + 2 * D + c0, CW))
        mg_ref[:, c0:c0 + CW] += gc * yc
        b_conv(3, late_rows(c))

    zc = _layer_norm(xo_ref[...], lnbg_ref[...], lnbb_ref[...])
    cvb_ref[...] = (zc * _sigmoid(zc)).astype(jnp.bfloat16)
    for c in range(n_chunks):
        c0 = c * CW
        yb = jnp.dot(cvb_ref[...], wb_ref[:, c0:c0 + CW], preferred_element_type=jnp.float32)
        gb = _sigmoid(proj(HALO, HALO + TM, C_G + D + c0, CW))
        mg_ref[:, c0:c0 + CW] += gb * yb

    cvb_ref[...] = mg_ref[...].astype(jnp.bfloat16)
    x_res = x_cur()
    for c in range(n_chunks):
        c0 = c * CW
        xo_ref[:, c0:c0 + CW] = x_res[:, c0:c0 + CW] + jnp.dot(
            cvb_ref[...], wo_ref[:, c0:c0 + CW], preferred_element_type=jnp.float32)
    xnew = xo_ref[...]
    ms = jnp.mean(xnew * xnew, axis=-1, keepdims=True)
    h2 = xnew * lax.rsqrt(ms + RMS_EPS) * nf_ref[...]
    h2_packed = _pack_bf16_pairs(h2)
    for j in range(NPIECE):
        h2_ref[j] = lax.bitcast_convert_type(h2_packed[:, j * 128:(j + 1) * 128], jnp.int32)
    h_hi = h2.astype(jnp.bfloat16)
    h_lo = (h2 - h_hi.astype(jnp.float32)).astype(jnp.bfloat16)
    dot32 = functools.partial(jnp.dot, preferred_element_type=jnp.float32)
    lg_ref[...] = (dot32(h_hi, wrh_ref[...]) + dot32(h_lo, wrh_ref[...])
                   + dot32(h_hi, wrl_ref[...]) + dot32(h_lo, wrl_ref[...]) + br_ref[...])


def _layer_spec(layer, shape):
    nd = len(shape)
    return pl.BlockSpec((None,) + tuple(shape), lambda i, _n=nd: (layer,) + (0,) * _n,
                        pipeline_mode=pl.Buffered(1))


def _mixer(layer, x2, pending, seq_len, nm, win, ca, wa, cb, lnbg, lnbb, wb, lncg, lncb, ws, bsb, wc, wo,
           nf, wrh, wrl, br):
    T = x2.shape[0]
    n_tiles = T // TM
    hb = TM // HALO
    last_halo = T // HALO - 1
    ls = functools.partial(_layer_spec, layer)
    prev_rows = lambda i: jnp.maximum(i * hb - 1, 0)
    next_rows = lambda i: jnp.minimum((i + 1) * hb, last_halo)
    in_specs = [
        pl.BlockSpec((HALO, D), lambda i: (prev_rows(i), 0)),
        pl.BlockSpec((TM, D), lambda i: (i, 0)),
        pl.BlockSpec((HALO, D), lambda i: (next_rows(i), 0)),
    ]
    operands = [x2, x2, x2]
    if pending is not None:
        gathered, wgt_rows = pending
        in_specs += [
            pl.BlockSpec((2, NPIECE, HALO, 128), lambda i: (0, 0, prev_rows(i), 0)),
            pl.BlockSpec((2, NPIECE, TM, 128), lambda i: (0, 0, i, 0)),
            pl.BlockSpec((2, NPIECE, HALO, 128), lambda i: (0, 0, next_rows(i), 0)),
            pl.BlockSpec((HALO, 8), lambda i: (prev_rows(i), 0)),
            pl.BlockSpec((TM, 8), lambda i: (i, 0)),
            pl.BlockSpec((HALO, 8), lambda i: (next_rows(i), 0)),
        ]
        operands += [gathered, gathered, gathered, wgt_rows, wgt_rows, wgt_rows]
    in_specs += [
        ls((1, D)), ls((D, D_IN)), ls((CONV_A, D)), ls((D, D)),
        ls((CONV_B, D)), ls((1, D)), ls((1, D)), ls((D, D)),
        ls((1, D)), ls((1, D)), ls((N_HEADS_C, CHUNK, CHUNK)),
        ls((N_HEADS_C, CHUNK, CHUNK)), ls((D, D)), ls((D, D)),
        ls((1, D)), ls((D, NLOG)), ls((D, NLOG)), ls((1, NLOG)),
    ]
    out_specs = [
        pl.BlockSpec((TM, D), lambda i: (i, 0)),
        pl.BlockSpec((NPIECE, TM, 128), lambda i: (0, i, 0)),
        pl.BlockSpec((TM, NLOG), lambda i: (i, 0)),
    ]
    return pl.pallas_call(
        functools.partial(_mixer_kernel, seq_len // TM, pending is not None),
        grid=(n_tiles,),
        in_specs=in_specs,
        out_specs=out_specs,
        out_shape=[jax.ShapeDtypeStruct((T, D), jnp.float32),
                   jax.ShapeDtypeStruct((NPIECE, T, 128), jnp.int32),
                   jax.ShapeDtypeStruct((T, NLOG), jnp.float32)],
        scratch_shapes=[pltpu.VMEM((TE, D), jnp.bfloat16),
                        pltpu.VMEM((D // CW, TE, CW), jnp.float32),
                        pltpu.VMEM((TM, D), jnp.bfloat16),
                        pltpu.VMEM((TM, D), jnp.bfloat16),
                        pltpu.VMEM((TM, D), jnp.bfloat16),
                        pltpu.VMEM((TM, D), jnp.float32),
                        pltpu.VMEM((TM, D), jnp.float32)],
        compiler_params=pltpu.CompilerParams(dimension_semantics=("arbitrary",),
                                             vmem_limit_bytes=VMEM_LIMIT),
        name="mixer",
    )(*operands, nm, win, ca, wa, cb, lnbg, lnbb, wb, lncg, lncb, ws, bsb, wc, wo, nf, wrh, wrl, br)


def _route_kernel(lg_ref, idx_ref, wgt_ref, cnt_ref, carry_ref):
    i = pl.program_id(0)

    @pl.when(i == 0)
    def _():
        carry_ref[...] = jnp.zeros_like(carry_ref)

    lt = lg_ref[...].T
    g = [lt[j:j + 1, :] for j in range(N_GROUPS)]
    gmax = jnp.maximum(jnp.maximum(g[0], g[1]), jnp.maximum(g[2], g[3]))
    gidx = jnp.where(g[0] == gmax, 0.0, jnp.where(g[1] == gmax, 1.0, jnp.where(g[2] == gmax, 2.0, 3.0)))
    gsum = sum(jnp.exp(gj - gmax) for gj in g)
    g_p = 1.0 / gsum

    sel = lt[E_OFF + 3 * EPG:E_OFF + 4 * EPG, :]
    for j in (2, 1, 0):
        sel = jnp.where(gidx == float(j), lt[E_OFF + j * EPG:E_OFF + (j + 1) * EPG, :], sel)
    rid = lax.broadcasted_iota(jnp.int32, (EPG, TR), 0).astype(jnp.float32)
    m1 = jnp.max(sel, axis=0, keepdims=True)
    i1 = jnp.min(jnp.where(sel == m1, rid, float(EPG)), axis=0, keepdims=True)
    rest = jnp.where(rid == i1, -jnp.inf, sel)
    m2 = jnp.max(rest, axis=0, keepdims=True)
    i2 = jnp.min(jnp.where(rest == m2, rid, float(EPG)), axis=0, keepdims=True)
    e2x = jnp.exp(m2 - m1)
    den = 1.0 + e2x
    w1 = (1.0 / den) * g_p
    w2 = (e2x / den) * g_p
    e1 = gidx * float(EPG) + i1
    e2 = gidx * float(EPG) + i2

    eid = lax.broadcasted_iota(jnp.int32, (N_EXPERTS, TR), 0).astype(jnp.float32)
    oh1 = (eid == e1).astype(jnp.float32)
    oh2 = (eid == e2).astype(jnp.float32)
    oh = oh1 + oh2
    tr = lax.broadcasted_iota(jnp.int32, (TR, TR), 0)
    tc = lax.broadcasted_iota(jnp.int32, (TR, TR), 1)
    upper = (tr < tc).astype(jnp.bfloat16)
    before = jnp.dot(oh.astype(jnp.bfloat16), upper, preferred_element_type=jnp.float32)
    base = before + carry_ref[:, 0:1]
    r1 = jnp.sum(oh1 * base, axis=0, keepdims=True)
    r2 = jnp.sum(oh2 * base, axis=0, keepdims=True)
    carry_ref[...] = carry_ref[...] + jnp.sum(oh, axis=1, keepdims=True)

    idx_ref[...] = jnp.zeros_like(idx_ref)
    idx_ref[0:1, :] = e1.astype(jnp.int32)
    idx_ref[1:2, :] = e2.astype(jnp.int32)
    idx_ref[2:3, :] = r1.astype(jnp.int32)
    idx_ref[3:4, :] = r2.astype(jnp.int32)
    wgt_ref[...] = jnp.zeros_like(wgt_ref)
    wgt_ref[0:1, :] = w1
    wgt_ref[1:2, :] = w2
    cnt_ref[...] = carry_ref[...]


def _route(logits):
    T = logits.shape[0]
    return pl.pallas_call(
        _route_kernel,
        grid=(T // TR,),
        in_specs=[pl.BlockSpec((TR, NLOG), lambda i: (i, 0))],
        out_specs=[pl.BlockSpec((8, TR), lambda i: (0, i)),
                   pl.BlockSpec((8, TR), lambda i: (0, i)),
                   pl.BlockSpec((N_EXPERTS, 128), lambda i: (0, 0))],
        out_shape=[jax.ShapeDtypeStruct((8, T), jnp.int32),
                   jax.ShapeDtypeStruct((8, T), jnp.float32),
                   jax.ShapeDtypeStruct((N_EXPERTS, 128), jnp.float32)],
        scratch_shapes=[pltpu.VMEM((N_EXPERTS, 128), jnp.float32)],
        compiler_params=pltpu.CompilerParams(dimension_semantics=("arbitrary",)),
        name="route",
    )(logits)


SC_WINDOW = 128


def _sc_mesh():
    return plsc.VectorSubcoreMesh(core_axis_name="c", subcore_axis_name="s")


def _sc_scatter_rows(rows, idx, passes, n_out):
    n_src = rows.shape[0]
    src_blocks = n_src // SC_WINDOW

    @functools.partial(
        pl.kernel, mesh=_sc_mesh(),
        out_type=jax.ShapeDtypeStruct((n_out, 128), rows.dtype))
    def scatter(rows_hbm, idx_hbm, out_hbm):
        def body(rows_vmem, idx_vmem):
            pltpu.sync_copy(rows_vmem, out_hbm.at[idx_vmem.at[0]])

        pltpu.emit_pipeline(
            body,
            grid=(passes * src_blocks,),
            in_specs=[pl.BlockSpec((SC_WINDOW, 128), lambda i: (i % src_blocks, 0)),
                      pl.BlockSpec((1, SC_WINDOW), lambda i: (0, i))],
            out_specs=[],
            core_axis_name=("c", "s"),
            dimension_semantics=(pltpu.PARALLEL,),
        )(rows_hbm, idx_hbm)

    return scatter(rows, idx.reshape(1, passes * n_src))


def _expert_kernel(layer, be_ref, nu_ref, nxt_ref, gs_ref, nv_ref, xb_ref, w1_ref, w3_ref, w2_ref, yb_ref,
                   wf1_ref, wf3_ref, wf2_ref, w1b_ref, w3b_ref, w2b_ref, sem):
    b = pl.program_id(0)
    used = b < nu_ref[0]
    new_expert = jnp.logical_or(b == 0, be_ref[b] != be_ref[jnp.maximum(b - 1, 0)])

    def weight_copies(e, s):
        return [pltpu.make_async_copy(src.at[layer, e], dst.at[s], sem.at[s])
                for src, dst in ((w1_ref, wf1_ref), (w3_ref, wf3_ref), (w2_ref, wf2_ref))]

    @pl.when(b == 0)
    def _():
        for cp in weight_copies(be_ref[0], 0):
            cp.start(priority=1)

    @pl.when(jnp.logical_and(used, new_expert))
    def _():
        s = gs_ref[b]
        for cp in weight_copies(be_ref[b], s):
            cp.wait()

        @pl.when(nxt_ref[b] >= 0)
        def _():
            for cp in weight_copies(nxt_ref[b], 1 - s):
                cp.start(priority=1)

        w1b_ref[...] = wf1_ref[s].astype(jnp.bfloat16)
        w3b_ref[...] = wf3_ref[s].astype(jnp.bfloat16)
        w2b_ref[...] = wf2_ref[s].astype(jnp.bfloat16)

    @pl.when(used)
    def _():
        live = lax.broadcasted_iota(jnp.int32, (BM, 1), 0) < nv_ref[b]
        packed_in = jnp.concatenate(
            [jnp.where(live, lax.bitcast_convert_type(xb_ref[j], jnp.uint32), jnp.uint32(0))
             for j in range(NPIECE)], axis=1)
        x_hi, x_lo = (v.astype(jnp.bfloat16) for v in _unpack_bf16_pairs(packed_in))
        dot32 = functools.partial(jnp.dot, preferred_element_type=jnp.float32)
        a = dot32(x_hi, w1b_ref[0:DP, :]) + dot32(x_lo, w1b_ref[DP:D, :])
        g = dot32(x_hi, w3b_ref[0:DP, :]) + dot32(x_lo, w3b_ref[DP:D, :])
        hmid = (a * _sigmoid(a) * g).astype(jnp.bfloat16)
        packed = _pack_bf16_pairs(dot32(hmid, w2b_ref[...]))
        for j in range(NPIECE):
            yb_ref[j] = lax.bitcast_convert_type(packed[:, j * 128:(j + 1) * 128], jnp.int32)

    @pl.when(jnp.logical_not(used))
    def _():
        yb_ref[...] = jnp.zeros_like(yb_ref)


def _experts(layer, block_expert, n_used, next_expert, group_slot, block_valid, xb, w1, w3, w2):
    n_slots = xb.shape[1]
    n_blocks = n_slots // BM

    def row_map(b, be, nu, nxt, gs, nv):
        return (0, jnp.minimum(b, nu[0] - 1), 0)

    def out_map(b, be, nu, nxt, gs, nv):
        return (0, b, 0)

    grid_spec = pltpu.PrefetchScalarGridSpec(
        num_scalar_prefetch=5,
        grid=(n_blocks,),
        in_specs=[pl.BlockSpec((NPIECE, BM, 128), row_map),
                  pl.BlockSpec(memory_space=pl.ANY),
                  pl.BlockSpec(memory_space=pl.ANY),
                  pl.BlockSpec(memory_space=pl.ANY)],
        out_specs=pl.BlockSpec((NPIECE, BM, 128), out_map),
        scratch_shapes=[pltpu.VMEM((2, D, D_FF), jnp.float32),
                        pltpu.VMEM((2, D, D_FF), jnp.float32),
                        pltpu.VMEM((2, D_FF, D), jnp.float32),
                        pltpu.VMEM((D, D_FF), jnp.bfloat16),
                        pltpu.VMEM((D, D_FF), jnp.bfloat16),
                        pltpu.VMEM((D_FF, D), jnp.bfloat16),
                        pltpu.SemaphoreType.DMA((2,))],
    )
    return pl.pallas_call(
        functools.partial(_expert_kernel, layer),
        grid_spec=grid_spec,
        out_shape=jax.ShapeDtypeStruct((NPIECE, n_slots, 128), jnp.int32),
        compiler_params=pltpu.CompilerParams(dimension_semantics=("arbitrary",),
                                             vmem_limit_bytes=VMEM_LIMIT),
        name="experts",
    )(block_expert, n_used, next_expert, group_slot, block_valid, xb, w1, w3, w2)


def _sc_gather_rows(table, idx):
    n_rows = idx.shape[0]

    @functools.partial(
        pl.kernel, mesh=_sc_mesh(),
        out_type=jax.ShapeDtypeStruct((n_rows, 128), table.dtype))
    def gather(table_hbm, idx_hbm, out_hbm):
        def body(idx_vmem, out_vmem):
            pltpu.sync_copy(table_hbm.at[idx_vmem.at[0]], out_vmem)

        pltpu.emit_pipeline(
            body,
            grid=(n_rows // SC_WINDOW,),
            in_specs=[pl.BlockSpec((1, SC_WINDOW), lambda i: (0, i))],
            out_specs=[pl.BlockSpec((SC_WINDOW, 128), lambda i: (i, 0))],
            core_axis_name=("c", "s"),
            dimension_semantics=(pltpu.PARALLEL,),
        )(idx_hbm, out_hbm)

    return gather(table, idx.reshape(1, n_rows))


def _combine_kernel(final, g_ref, x_ref, w_ref, nrm_ref, o_ref):
    w = w_ref[...].T
    ys = {}
    for j in range(NPIECE):
        hi0, lo0 = _unpack_bf16_pairs(lax.bitcast_convert_type(g_ref[0, j], jnp.uint32))
        hi1, lo1 = _unpack_bf16_pairs(lax.bitcast_convert_type(g_ref[1, j], jnp.uint32))
        c_hi, c_lo = j * 128, DP + j * 128
        ys[c_hi] = x_ref[:, c_hi:c_hi + 128] + w[:, 0:1] * hi0 + w[:, 1:2] * hi1
        ys[c_lo] = x_ref[:, c_lo:c_lo + 128] + w[:, 0:1] * lo0 + w[:, 1:2] * lo1
    if final:
        ms = sum(jnp.sum(y * y, axis=-1, keepdims=True) for y in ys.values()) * (1.0 / D)
        scale = lax.rsqrt(ms + RMS_EPS)
        ys = {c: y * scale * nrm_ref[:, c:c + 128] for c, y in ys.items()}
    for c, y in ys.items():
        o_ref[:, c:c + 128] = y


def _combine(gathered, x2, wgt, nrm, final):
    T = x2.shape[0]
    return pl.pallas_call(
        functools.partial(_combine_kernel, final),
        grid=(T // TD,),
        in_specs=[pl.BlockSpec((2, NPIECE, TD, 128), lambda i: (0, 0, i, 0)),
                  pl.BlockSpec((TD, D), lambda i: (i, 0)),
                  pl.BlockSpec((8, TD), lambda i: (0, i)),
                  pl.BlockSpec((1, D), lambda i: (0, 0))],
        out_specs=pl.BlockSpec((TD, D), lambda i: (i, 0)),
        out_shape=jax.ShapeDtypeStruct((T, D), jnp.float32),
        compiler_params=pltpu.CompilerParams(dimension_semantics=("arbitrary",)),
        name="combine",
    )(gathered, x2, wgt, nrm)


def _moe(layer, h2, logits, w1, w3, w2):
    T = logits.shape[0]
    idx, wgt, cnt = _route(logits)
    counts = cnt[:, 0].astype(jnp.int32)
    padded = (counts + BM - 1) // BM * BM
    padded_end = jnp.cumsum(padded)
    padded_start = padded_end - padded
    eids = jnp.arange(N_EXPERTS, dtype=jnp.int32)[:, None]
    start_of = lambda e: jnp.sum(jnp.where(e[None, :] == eids, padded_start[:, None], 0), axis=0)
    dest = jnp.stack([start_of(idx[0]) + idx[2], start_of(idx[1]) + idx[3]])
    n_blocks = (2 * T) // BM + N_EXPERTS
    block_start = jnp.arange(n_blocks, dtype=jnp.int32) * BM
    block_expert = jnp.minimum(
        jnp.sum((padded_end[None, :] <= block_start[:, None]).astype(jnp.int32), axis=1),
        N_EXPERTS - 1)
    n_used = (padded_end[-1] // BM).reshape(1)
    later = jnp.logical_and(eids.T > eids, (counts > 0)[None, :])
    next_of = jnp.min(jnp.where(later, eids.T, N_EXPERTS), axis=1)
    next_of = jnp.where(next_of < N_EXPERTS, next_of, -1)
    pick = lambda table: jnp.sum(
        jnp.where(block_expert[:, None] == eids.T, table[None, :], 0), axis=1)
    next_expert = pick(next_of)
    group_slot = pick(jnp.cumsum((counts > 0).astype(jnp.int32)) - 1) % 2
    block_valid = jnp.clip(pick(padded_start + counts) - block_start, 0, BM)
    n_slots = n_blocks * BM
    piece_base = (jnp.arange(NPIECE, dtype=jnp.int32) * n_slots)[None, :, None]
    rows = (dest[:, None, :] + piece_base).reshape(-1)
    xb = _sc_scatter_rows(h2.reshape(NPIECE * T, 128), rows, 2, NPIECE * n_slots)
    yb = _experts(layer, block_expert, n_used, next_expert, group_slot, block_valid,
                  xb.reshape(NPIECE, n_slots, 128), w1, w3, w2)
    gathered = _sc_gather_rows(yb.reshape(NPIECE * n_slots, 128), rows)
    return gathered.reshape(2, NPIECE, T, 128), wgt


def kernel(x, norm_mix, w_in, conv_a, w_a_out, conv_b, ln_b_g, ln_b_b, w_b_out, ln_c_g, ln_c_b,
           w_s, b_s, w_c_out, w_o, norm_ffn, w_group, b_group, w_router, b_router, w1, w3, w2,
           norm_final):
    bsz, seq, d = x.shape
    depth = norm_mix.shape[0]
    bf = jnp.bfloat16
    x2 = x.reshape(bsz * seq, d)
    wr = jnp.zeros((depth, d, NLOG), jnp.float32)
    wr = wr.at[:, :, 0:N_GROUPS].set(w_group).at[:, :, E_OFF:E_OFF + N_EXPERTS].set(w_router)
    wr_hi = wr.astype(bf)
    wr_lo = (wr - wr_hi.astype(jnp.float32)).astype(bf)
    br = jnp.zeros((depth, 1, NLOG), jnp.float32)
    br = br.at[:, 0, 0:N_GROUPS].set(b_group).at[:, 0, E_OFF:E_OFF + N_EXPERTS].set(b_router)
    bsb = jnp.broadcast_to(b_s[:, :, :, None], (depth, N_HEADS_C, CHUNK, CHUNK))
    row = lambda p: p[:, None, :]
    mixer_params = (row(norm_mix), w_in.astype(bf), conv_a, w_a_out.astype(bf), conv_b,
                    row(ln_b_g), row(ln_b_b), w_b_out.astype(bf), row(ln_c_g), row(ln_c_b),
                    w_s.astype(bf), bsb, w_c_out.astype(bf), w_o.astype(bf), row(norm_ffn), wr_hi, wr_lo, br)
    pending = None
    for l in range(depth):
        x2, h2, logits = _mixer(l, x2, pending, seq, *mixer_params)
        gathered, wgt = _moe(l, h2, logits, w1, w3, w2)
        pending = (gathered, wgt.T)
    out = _combine(gathered, x2, wgt, norm_final[None], True)
    return out.reshape(bsz, seq, d)
```

```python
import functools

import jax
import jax.numpy as jnp
from jax import lax
from jax.experimental import pallas as pl
from jax.experimental.pallas import tpu as pltpu
from jax.experimental.pallas import tpu_sc as plsc

D = 1024
DP = D // 2
NPIECE = DP // 128
N_HEADS_C = 8
CHUNK = 128
CONV_A = 3
CONV_B = 31
N_GROUPS = 4
EPG = 8
N_EXPERTS = N_GROUPS * EPG
D_FF = 512
RMS_EPS = 1e-6
LN_EPS = 1e-5

C_XA, C_BA, C_CA, C_VB, C_GB, C_UV, C_G = 0, 1024, 2048, 3072, 4096, 5120, 7168
D_IN = 10240

TM = 512
HALO = 16
TE = TM + 2 * HALO
CW = 256
RB = 64
NLOG = 128
E_OFF = 8

TR = TM
BM = 512
TD = 512

VMEM_LIMIT = 60 * 1024 * 1024


def _sigmoid(x):
    return 0.5 * (jnp.tanh(0.5 * x) + 1.0)


def _gelu_tanh(x):
    return 0.5 * x * (1.0 + jnp.tanh(0.7978845608028654 * (x + 0.044715 * (x * x * x))))


def _pack_bf16_pairs(x):
    c = x.shape[1] // 2
    as_bits = lambda v: lax.bitcast_convert_type(v.astype(jnp.bfloat16).astype(jnp.float32), jnp.uint32)
    return as_bits(x[:, :c]) | (as_bits(x[:, c:]) >> 16)


def _unpack_bf16_pairs(p):
    hi = lax.bitcast_convert_type(p & jnp.uint32(0xFFFF0000), jnp.float32)
    lo = lax.bitcast_convert_type(p << 16, jnp.float32)
    return hi, lo


def _layer_norm(x, g, b):
    mu = jnp.mean(x, axis=-1, keepdims=True)
    xc = x - mu
    var = jnp.mean(xc * xc, axis=-1, keepdims=True)
    return xc * lax.rsqrt(var + LN_EPS) * g + b


def _moe_residual(x_ref, g_ref, wt_ref):
    w0, w1 = wt_ref[:, 0:1], wt_ref[:, 1:2]
    his, los = [], []
    for j in range(NPIECE):
        hi0, lo0 = _unpack_bf16_pairs(lax.bitcast_convert_type(g_ref[0, j], jnp.uint32))
        hi1, lo1 = _unpack_bf16_pairs(lax.bitcast_convert_type(g_ref[1, j], jnp.uint32))
        his.append(w0 * hi0 + w1 * hi1)
        los.append(w0 * lo0 + w1 * lo1)
    return x_ref[...] + jnp.concatenate(his + los, axis=1)


def _mixer_kernel(seq_tiles, fused_combine, xp_ref, xc_ref, xn_ref, *refs):
    if fused_combine:
        gp_ref, gc_ref, gn_ref, wtp_ref, wtc_ref, wtn_ref = refs[:6]
        refs = refs[6:]
        x_prev = lambda: _moe_residual(xp_ref, gp_ref, wtp_ref)
        x_cur = lambda: _moe_residual(xc_ref, gc_ref, wtc_ref)
        x_next = lambda: _moe_residual(xn_ref, gn_ref, wtn_ref)
    else:
        x_prev, x_cur, x_next = (lambda: xp_ref[...]), (lambda: xc_ref[...]), (lambda: xn_ref[...])
    (nm_ref, win_ref, ca_ref, wa_ref, cb_ref, lnbg_ref, lnbb_ref, wb_ref, lncg_ref, lncb_ref,
     ws_ref, bsb_ref, wc_ref, wo_ref, nf_ref, wrh_ref, wrl_ref, br_ref,
     xo_ref, h2_ref, idx_ref, wgt_ref, cnt_ref,
     hb_ref, zq_ref, cva_ref, cvb_ref, cvn_ref, vst_ref, mg_ref, carry_ref) = refs
    i = pl.program_id(0)
    at_start = (i % seq_tiles) == 0
    at_end = (i % seq_tiles) == seq_tiles - 1

    def _rms(xv):
        ms = jnp.mean(xv * xv, axis=-1, keepdims=True)
        return (xv * lax.rsqrt(ms + RMS_EPS) * nm_ref[...]).astype(jnp.bfloat16)

    hb_ref[0:HALO, :] = _rms(x_prev())
    hb_ref[HALO:HALO + TM, :] = _rms(x_cur())
    hb_ref[HALO + TM:TE, :] = _rms(x_next())

    rows = lax.broadcasted_iota(jnp.int32, (TE, 1), 0)
    lo = jnp.where(at_start, HALO, 0)
    hi = jnp.where(at_end, HALO + TM, TE)
    valid = jnp.logical_and(rows >= lo, rows < hi)

    def proj(r0, r1, c0, width):
        return jnp.dot(hb_ref[r0:r1, :], win_ref[:, c0:c0 + width],
                       preferred_element_type=jnp.float32)

    def b_proj(c):
        c0 = c * CW
        val = proj(0, TE, C_VB + c0, CW)
        gate = proj(0, TE, C_GB + c0, CW)
        zq_ref[c] = jnp.where(valid, val * _sigmoid(gate), 0.0)

    def a_proj(c):
        c0 = c * CW
        xa = proj(0, TE, C_XA + c0, CW)
        cc = proj(0, TE, C_CA + c0, CW)
        t = jnp.where(valid, xa * cc, 0.0)
        conv = (ca_ref[0:1, c0:c0 + CW] * t[HALO - 1:HALO - 1 + TM]
                + ca_ref[1:2, c0:c0 + CW] * t[HALO:HALO + TM]
                + ca_ref[2:3, c0:c0 + CW] * t[HALO + 1:HALO + 1 + TM])
        ba = proj(HALO, HALO + TM, C_BA + c0, CW)
        cva_ref[:, c0:c0 + CW] = (ba * conv).astype(jnp.bfloat16)

    def c_v_proj(c):
        c0 = c * CW
        vst_ref[:, c0:c0 + CW] = _gelu_tanh(proj(HALO, HALO + TM, C_UV + D + c0, CW))

    def b_conv(c, row_blocks):
        c0 = c * CW
        zq = zq_ref.at[c]
        for rb in row_blocks:
            r0 = rb * RB
            for lt in range(CW // 128):
                l0 = lt * 128
                acc = None
                for s in range(8):
                    part = None
                    for q in range(4):
                        k = 8 * q + s - (HALO - CONV_B // 2)
                        if 0 <= k < CONV_B:
                            term = (cb_ref[k:k + 1, c0 + l0:c0 + l0 + 128]
                                    * zq[r0 + 8 * q:r0 + 8 * q + RB + 8, l0:l0 + 128])
                            part = term if part is None else part + term
                    shifted = part[s:s + RB]
                    acc = shifted if acc is None else acc + shifted
                xo_ref[r0:r0 + RB, c0 + l0:c0 + l0 + 128] = acc

    n_chunks = D // CW
    n_rb = TM // RB
    rb_step = n_rb // n_chunks
    late_rows = lambda c: range(c * rb_step, (c + 1) * rb_step)
    b_proj(0)
    for c in range(n_chunks):
        if c + 1 < n_chunks:
            b_proj(c + 1)
        a_proj(c)
        c_v_proj(c)
    b_conv(0, range(n_rb))

    vn = _layer_norm(vst_ref[...], lncg_ref[...], lncb_ref[...])
    cvn_ref[...] = vn.astype(jnp.bfloat16)
    for c in range(n_chunks):
        c0 = c * CW
        ya = jnp.dot(cva_ref[...], wa_ref[:, c0:c0 + CW], preferred_element_type=jnp.float32)
        ga = _sigmoid(proj(HALO, HALO + TM, C_G + c0, CW))
        mg_ref[:, c0:c0 + CW] = ga * ya
        b_conv(1, late_rows(c))

    for c in range(n_chunks):
        c0 = c * CW
        u = _gelu_tanh(proj(HALO, HALO + TM, C_UV + c0, CW))
        for n in range(TM // CHUNK):
            for hh in range(CW // CHUNK):
                h = c * (CW // CHUNK) + hh
                sv = jnp.dot(ws_ref[h], cvn_ref[n * CHUNK:(n + 1) * CHUNK, h * CHUNK:(h + 1) * CHUNK],
                             preferred_element_type=jnp.float32) + bsb_ref[h]
                cva_ref[n * CHUNK:(n + 1) * CHUNK, h * CHUNK:(h + 1) * CHUNK] = (
                    u[n * CHUNK:(n + 1) * CHUNK, hh * CHUNK:(hh + 1) * CHUNK] * sv
                ).astype(jnp.bfloat16)
        b_conv(2, late_rows(c))
    for c in range(n_chunks):
        c0 = c * CW
        yc = jnp.dot(cva_ref[...], wc_ref[:, c0:c0 + CW], preferred_element_type=jnp.float32)
        gc = _sigmoid(proj(HALO, HALO + TM, C_G + 2 * D + c0, CW))
        mg_ref[:, c0:c0 + CW] += gc * yc
        b_conv(3, late_rows(c))

    zc = _layer_norm(xo_ref[...], lnbg_ref[...], lnbb_ref[...])
    cvb_ref[...] = (zc * _sigmoid(zc)).astype(jnp.bfloat16)
    for c in range(n_chunks):
        c0 = c * CW
        yb = jnp.dot(cvb_ref[...], wb_ref[:, c0:c0 + CW], preferred_element_type=jnp.float32)
        gb = _sigmoid(proj(HALO, HALO + TM, C_G + D + c0, CW))
        mg_ref[:, c0:c0 + CW] += gb * yb

    cvb_ref[...] = mg_ref[...].astype(jnp.bfloat16)
    x_res = x_cur()
    for c in range(n_chunks):
        c0 = c * CW
        xo_ref[:, c0:c0 + CW] = x_res[:, c0:c0 + CW] + jnp.dot(
            cvb_ref[...], wo_ref[:, c0:c0 + CW], preferred_element_type=jnp.float32)
    xnew = xo_ref[...]
    ms = jnp.mean(xnew * xnew, axis=-1, keepdims=True)
    h2 = xnew * lax.rsqrt(ms + RMS_EPS) * nf_ref[...]
    h2_packed = _pack_bf16_pairs(h2)
    for j in range(NPIECE):
        h2_ref[j] = lax.bitcast_convert_type(h2_packed[:, j * 128:(j + 1) * 128], jnp.int32)
    h_hi = h2.astype(jnp.bfloat16)
    h_lo = (h2 - h_hi.astype(jnp.float32)).astype(jnp.bfloat16)
    dot32 = functools.partial(jnp.dot, preferred_element_type=jnp.float32)
    logits = (dot32(h_hi, wrh_ref[...]) + dot32(h_lo, wrh_ref[...])
              + dot32(h_hi, wrl_ref[...]) + dot32(h_lo, wrl_ref[...]) + br_ref[...])
    _route_tile(logits, idx_ref, wgt_ref, cnt_ref, carry_ref)


def _layer_spec(layer, shape):
    nd = len(shape)
    return pl.BlockSpec((None,) + tuple(shape), lambda i, _n=nd: (layer,) + (0,) * _n,
                        pipeline_mode=pl.Buffered(1))


def _mixer(layer, x2, pending, seq_len, nm, win, ca, wa, cb, lnbg, lnbb, wb, lncg, lncb, ws, bsb, wc, wo,
           nf, wrh, wrl, br):
    T = x2.shape[0]
    n_tiles = T // TM
    hb = TM // HALO
    last_halo = T // HALO - 1
    ls = functools.partial(_layer_spec, layer)
    prev_rows = lambda i: jnp.maximum(i * hb - 1, 0)
    next_rows = lambda i: jnp.minimum((i + 1) * hb, last_halo)
    in_specs = [
        pl.BlockSpec((HALO, D), lambda i: (prev_rows(i), 0)),
        pl.BlockSpec((TM, D), lambda i: (i, 0)),
        pl.BlockSpec((HALO, D), lambda i: (next_rows(i), 0)),
    ]
    operands = [x2, x2, x2]
    if pending is not None:
        gathered, wgt_rows = pending
        in_specs += [
            pl.BlockSpec((2, NPIECE, HALO, 128), lambda i: (0, 0, prev_rows(i), 0)),
            pl.BlockSpec((2, NPIECE, TM, 128), lambda i: (0, 0, i, 0)),
            pl.BlockSpec((2, NPIECE, HALO, 128), lambda i: (0, 0, next_rows(i), 0)),
            pl.BlockSpec((HALO, 8), lambda i: (prev_rows(i), 0)),
            pl.BlockSpec((TM, 8), lambda i: (i, 0)),
            pl.BlockSpec((HALO, 8), lambda i: (next_rows(i), 0)),
        ]
        operands += [gathered, gathered, gathered, wgt_rows, wgt_rows, wgt_rows]
    in_specs += [
        ls((1, D)), ls((D, D_IN)), ls((CONV_A, D)), ls((D, D)),
        ls((CONV_B, D)), ls((1, D)), ls((1, D)), ls((D, D)),
        ls((1, D)), ls((1, D)), ls((N_HEADS_C, CHUNK, CHUNK)),
        ls((N_HEADS_C, CHUNK, CHUNK)), ls((D, D)), ls((D, D)),
        ls((1, D)), ls((D, NLOG)), ls((D, NLOG)), ls((1, NLOG)),
    ]
    out_specs = [
        pl.BlockSpec((TM, D), lambda i: (i, 0)),
        pl.BlockSpec((NPIECE, TM, 128), lambda i: (0, i, 0)),
        pl.BlockSpec((8, TM), lambda i: (0, i)),
        pl.BlockSpec((8, TM), lambda i: (0, i)),
        pl.BlockSpec((N_EXPERTS, 128), lambda i: (0, 0)),
    ]
    return pl.pallas_call(
        functools.partial(_mixer_kernel, seq_len // TM, pending is not None),
        grid=(n_tiles,),
        in_specs=in_specs,
        out_specs=out_specs,
        out_shape=[jax.ShapeDtypeStruct((T, D), jnp.float32),
                   jax.ShapeDtypeStruct((NPIECE, T, 128), jnp.int32),
                   jax.ShapeDtypeStruct((8, T), jnp.int32),
                   jax.ShapeDtypeStruct((8, T), jnp.float32),
                   jax.ShapeDtypeStruct((N_EXPERTS, 128), jnp.float32)],
        scratch_shapes=[pltpu.VMEM((TE, D), jnp.bfloat16),
                        pltpu.VMEM((D // CW, TE, CW), jnp.float32),
                        pltpu.VMEM((TM, D), jnp.bfloat16),
                        pltpu.VMEM((TM, D), jnp.bfloat16),
                        pltpu.VMEM((TM, D), jnp.bfloat16),
                        pltpu.VMEM((TM, D), jnp.float32),
                        pltpu.VMEM((TM, D), jnp.float32),
                        pltpu.VMEM((N_EXPERTS, 128), jnp.float32)],
        compiler_params=pltpu.CompilerParams(dimension_semantics=("arbitrary",),
                                             vmem_limit_bytes=VMEM_LIMIT),
        name="mixer",
    )(*operands, nm, win, ca, wa, cb, lnbg, lnbb, wb, lncg, lncb, ws, bsb, wc, wo, nf, wrh, wrl, br)


def _route_tile(lg, idx_ref, wgt_ref, cnt_ref, carry_ref):
    i = pl.program_id(0)

    @pl.when(i == 0)
    def _():
        carry_ref[...] = jnp.zeros_like(carry_ref)

    lt = lg.T
    g = [lt[j:j + 1, :] for j in range(N_GROUPS)]
    gmax = jnp.maximum(jnp.maximum(g[0], g[1]), jnp.maximum(g[2], g[3]))
    gidx = jnp.where(g[0] == gmax, 0.0, jnp.where(g[1] == gmax, 1.0, jnp.where(g[2] == gmax, 2.0, 3.0)))
    gsum = sum(jnp.exp(gj - gmax) for gj in g)
    g_p = 1.0 / gsum

    sel = lt[E_OFF + 3 * EPG:E_OFF + 4 * EPG, :]
    for j in (2, 1, 0):
        sel = jnp.where(gidx == float(j), lt[E_OFF + j * EPG:E_OFF + (j + 1) * EPG, :], sel)
    rid = lax.broadcasted_iota(jnp.int32, (EPG, TR), 0).astype(jnp.float32)
    m1 = jnp.max(sel, axis=0, keepdims=True)
    i1 = jnp.min(jnp.where(sel == m1, rid, float(EPG)), axis=0, keepdims=True)
    rest = jnp.where(rid == i1, -jnp.inf, sel)
    m2 = jnp.max(rest, axis=0, keepdims=True)
    i2 = jnp.min(jnp.where(rest == m2, rid, float(EPG)), axis=0, keepdims=True)
    e2x = jnp.exp(m2 - m1)
    den = 1.0 + e2x
    w1 = (1.0 / den) * g_p
    w2 = (e2x / den) * g_p
    e1 = gidx * float(EPG) + i1
    e2 = gidx * float(EPG) + i2

    eid = lax.broadcasted_iota(jnp.int32, (N_EXPERTS, TR), 0).astype(jnp.float32)
    oh1 = (eid == e1).astype(jnp.float32)
    oh2 = (eid == e2).astype(jnp.float32)
    oh = oh1 + oh2
    tr = lax.broadcasted_iota(jnp.int32, (TR, TR), 0)
    tc = lax.broadcasted_iota(jnp.int32, (TR, TR), 1)
    upper = (tr < tc).astype(jnp.bfloat16)
    before = jnp.dot(oh.astype(jnp.bfloat16), upper, preferred_element_type=jnp.float32)
    base = before + carry_ref[:, 0:1]
    r1 = jnp.sum(oh1 * base, axis=0, keepdims=True)
    r2 = jnp.sum(oh2 * base, axis=0, keepdims=True)
    carry_ref[...] = carry_ref[...] + jnp.sum(oh, axis=1, keepdims=True)

    idx_ref[...] = jnp.zeros_like(idx_ref)
    idx_ref[0:1, :] = e1.astype(jnp.int32)
    idx_ref[1:2, :] = e2.astype(jnp.int32)
    idx_ref[2:3, :] = r1.astype(jnp.int32)
    idx_ref[3:4, :] = r2.astype(jnp.int32)
    wgt_ref[...] = jnp.zeros_like(wgt_ref)
    wgt_ref[0:1, :] = w1
    wgt_ref[1:2, :] = w2
    cnt_ref[...] = carry_ref[...]


SC_WINDOW = 128


def _sc_mesh():
    return plsc.VectorSubcoreMesh(core_axis_name="c", subcore_axis_name="s")


def _sc_scatter_rows(rows, idx, passes, n_out):
    n_src = rows.shape[0]
    src_blocks = n_src // SC_WINDOW

    @functools.partial(
        pl.kernel, mesh=_sc_mesh(),
        out_type=jax.ShapeDtypeStruct((n_out, 128), rows.dtype))
    def scatter(rows_hbm, idx_hbm, out_hbm):
        def body(rows_vmem, idx_vmem):
            pltpu.sync_copy(rows_vmem, out_hbm.at[idx_vmem.at[0]])

        pltpu.emit_pipeline(
            body,
            grid=(passes * src_blocks,),
            in_specs=[pl.BlockSpec((SC_WINDOW, 128), lambda i: (i % src_blocks, 0)),
                      pl.BlockSpec((1, SC_WINDOW), lambda i: (0, i))],
            out_specs=[],
            core_axis_name=("c", "s"),
            dimension_semantics=(pltpu.PARALLEL,),
        )(rows_hbm, idx_hbm)

    return scatter(rows, idx.reshape(1, passes * n_src))


def _expert_kernel(layer, be_ref, nu_ref, nxt_ref, gs_ref, nv_ref, xb_ref, w1_ref, w3_ref, w2_ref, yb_ref,
                   wf1_ref, wf3_ref, wf2_ref, w1b_ref, w3b_ref, w2b_ref, sem):
    b = pl.program_id(0)
    used = b < nu_ref[0]
    new_expert = jnp.logical_or(b == 0, be_ref[b] != be_ref[jnp.maximum(b - 1, 0)])

    def weight_copies(e, s):
        return [pltpu.make_async_copy(src.at[layer, e], dst.at[s], sem.at[s])
                for src, dst in ((w1_ref, wf1_ref), (w3_ref, wf3_ref), (w2_ref, wf2_ref))]

    @pl.when(b == 0)
    def _():
        for cp in weight_copies(be_ref[0], 0):
            cp.start(priority=1)

    @pl.when(jnp.logical_and(used, new_expert))
    def _():
        s = gs_ref[b]
        for cp in weight_copies(be_ref[b], s):
            cp.wait()

        @pl.when(nxt_ref[b] >= 0)
        def _():
            for cp in weight_copies(nxt_ref[b], 1 - s):
                cp.start(priority=1)

        w1b_ref[...] = wf1_ref[s].astype(jnp.bfloat16)
        w3b_ref[...] = wf3_ref[s].astype(jnp.bfloat16)
        w2b_ref[...] = wf2_ref[s].astype(jnp.bfloat16)

    @pl.when(used)
    def _():
        live = lax.broadcasted_iota(jnp.int32, (BM, 1), 0) < nv_ref[b]
        packed_in = jnp.concatenate(
            [jnp.where(live, lax.bitcast_convert_type(xb_ref[j], jnp.uint32), jnp.uint32(0))
             for j in range(NPIECE)], axis=1)
        x_hi, x_lo = (v.astype(jnp.bfloat16) for v in _unpack_bf16_pairs(packed_in))
        dot32 = functools.partial(jnp.dot, preferred_element_type=jnp.float32)
        a = dot32(x_hi, w1b_ref[0:DP, :]) + dot32(x_lo, w1b_ref[DP:D, :])
        g = dot32(x_hi, w3b_ref[0:DP, :]) + dot32(x_lo, w3b_ref[DP:D, :])
        hmid = (a * _sigmoid(a) * g).astype(jnp.bfloat16)
        packed = _pack_bf16_pairs(dot32(hmid, w2b_ref[...]))
        for j in range(NPIECE):
            yb_ref[j] = lax.bitcast_convert_type(packed[:, j * 128:(j + 1) * 128], jnp.int32)

    @pl.when(jnp.logical_not(used))
    def _():
        yb_ref[...] = jnp.zeros_like(yb_ref)


def _experts(layer, block_expert, n_used, next_expert, group_slot, block_valid, xb, w1, w3, w2):
    n_slots = xb.shape[1]
    n_blocks = n_slots // BM

    def row_map(b, be, nu, nxt, gs, nv):
        return (0, jnp.minimum(b, nu[0] - 1), 0)

    def out_map(b, be, nu, nxt, gs, nv):
        return (0, b, 0)

    grid_spec = pltpu.PrefetchScalarGridSpec(
        num_scalar_prefetch=5,
        grid=(n_blocks,),
        in_specs=[pl.BlockSpec((NPIECE, BM, 128), row_map),
                  pl.BlockSpec(memory_space=pl.ANY),
                  pl.BlockSpec(memory_space=pl.ANY),
                  pl.BlockSpec(memory_space=pl.ANY)],
        out_specs=pl.BlockSpec((NPIECE, BM, 128), out_map),
        scratch_shapes=[pltpu.VMEM((2, D, D_FF), jnp.float32),
                        pltpu.VMEM((2, D, D_FF), jnp.float32),
                        pltpu.VMEM((2, D_FF, D), jnp.float32),
                        pltpu.VMEM((D, D_FF), jnp.bfloat16),
                        pltpu.VMEM((D, D_FF), jnp.bfloat16),
                        pltpu.VMEM((D_FF, D), jnp.bfloat16),
                        pltpu.SemaphoreType.DMA((2,))],
    )
    return pl.pallas_call(
        functools.partial(_expert_kernel, layer),
        grid_spec=grid_spec,
        out_shape=jax.ShapeDtypeStruct((NPIECE, n_slots, 128), jnp.int32),
        compiler_params=pltpu.CompilerParams(dimension_semantics=("arbitrary",),
                                             vmem_limit_bytes=VMEM_LIMIT),
        name="experts",
    )(block_expert, n_used, next_expert, group_slot, block_valid, xb, w1, w3, w2)


def _sc_gather_rows(table, idx):
    n_rows = idx.shape[0]

    @functools.partial(
        pl.kernel, mesh=_sc_mesh(),
        out_type=jax.ShapeDtypeStruct((n_rows, 128), table.dtype))
    def gather(table_hbm, idx_hbm, out_hbm):
        def body(idx_vmem, out_vmem):
            pltpu.sync_copy(table_hbm.at[idx_vmem.at[0]], out_vmem)

        pltpu.emit_pipeline(
            body,
            grid=(n_rows // SC_WINDOW,),
            in_specs=[pl.BlockSpec((1, SC_WINDOW), lambda i: (0, i))],
            out_specs=[pl.BlockSpec((SC_WINDOW, 128), lambda i: (i, 0))],
            core_axis_name=("c", "s"),
            dimension_semantics=(pltpu.PARALLEL,),
        )(idx_hbm, out_hbm)

    return gather(table, idx.reshape(1, n_rows))


def _combine_kernel(final, g_ref, x_ref, w_ref, nrm_ref, o_ref):
    w = w_ref[...].T
    ys = {}
    for j in range(NPIECE):
        hi0, lo0 = _unpack_bf16_pairs(lax.bitcast_convert_type(g_ref[0, j], jnp.uint32))
        hi1, lo1 = _unpack_bf16_pairs(lax.bitcast_convert_type(g_ref[1, j], jnp.uint32))
        c_hi, c_lo = j * 128, DP + j * 128
        ys[c_hi] = x_ref[:, c_hi:c_hi + 128] + w[:, 0:1] * hi0 + w[:, 1:2] * hi1
        ys[c_lo] = x_ref[:, c_lo:c_lo + 128] + w[:, 0:1] * lo0 + w[:, 1:2] * lo1
    if final:
        ms = sum(jnp.sum(y * y, axis=-1, keepdims=True) for y in ys.values()) * (1.0 / D)
        scale = lax.rsqrt(ms + RMS_EPS)
        ys = {c: y * scale * nrm_ref[:, c:c + 128] for c, y in ys.items()}
    for c, y in ys.items():
        o_ref[:, c:c + 128] = y


def _combine(gathered, x2, wgt, nrm, final):
    T = x2.shape[0]
    return pl.pallas_call(
        functools.partial(_combine_kernel, final),
        grid=(T // TD,),
        in_specs=[pl.BlockSpec((2, NPIECE, TD, 128), lambda i: (0, 0, i, 0)),
                  pl.BlockSpec((TD, D), lambda i: (i, 0)),
                  pl.BlockSpec((8, TD), lambda i: (0, i)),
                  pl.BlockSpec((1, D), lambda i: (0, 0))],
        out_specs=pl.BlockSpec((TD, D), lambda i: (i, 0)),
        out_shape=jax.ShapeDtypeStruct((T, D), jnp.float32),
        compiler_params=pltpu.CompilerParams(dimension_semantics=("arbitrary",)),
        name="combine",
    )(gathered, x2, wgt, nrm)


def _moe(layer, h2, idx, cnt, w1, w3, w2):
    T = idx.shape[1]
    counts = cnt[:, 0].astype(jnp.int32)
    padded = (counts + BM - 1) // BM * BM
    padded_end = jnp.cumsum(padded)
    padded_start = padded_end - padded
    eids = jnp.arange(N_EXPERTS, dtype=jnp.int32)[:, None]
    start_of = lambda e: jnp.sum(jnp.where(e[None, :] == eids, padded_start[:, None], 0), axis=0)
    dest = jnp.stack([start_of(idx[0]) + idx[2], start_of(idx[1]) + idx[3]])
    n_blocks = (2 * T) // BM + N_EXPERTS
    block_start = jnp.arange(n_blocks, dtype=jnp.int32) * BM
    block_expert = jnp.minimum(
        jnp.sum((padded_end[None, :] <= block_start[:, None]).astype(jnp.int32), axis=1),
        N_EXPERTS - 1)
    n_used = (padded_end[-1] // BM).reshape(1)
    later = jnp.logical_and(eids.T > eids, (counts > 0)[None, :])
    next_of = jnp.min(jnp.where(later, eids.T, N_EXPERTS), axis=1)
    next_of = jnp.where(next_of < N_EXPERTS, next_of, -1)
    pick = lambda table: jnp.sum(
        jnp.where(block_expert[:, None] == eids.T, table[None, :], 0), axis=1)
    next_expert = pick(next_of)
    group_slot = pick(jnp.cumsum((counts > 0).astype(jnp.int32)) - 1) % 2
    block_valid = jnp.clip(pick(padded_start + counts) - block_start, 0, BM)
    n_slots = n_blocks * BM
    piece_base = (jnp.arange(NPIECE, dtype=jnp.int32) * n_slots)[None, :, None]
    rows = (dest[:, None, :] + piece_base).reshape(-1)
    xb = _sc_scatter_rows(h2.reshape(NPIECE * T, 128), rows, 2, NPIECE * n_slots)
    yb = _experts(layer, block_expert, n_used, next_expert, group_slot, block_valid,
                  xb.reshape(NPIECE, n_slots, 128), w1, w3, w2)
    gathered = _sc_gather_rows(yb.reshape(NPIECE * n_slots, 128), rows)
    return gathered.reshape(2, NPIECE, T, 128)


def kernel(x, norm_mix, w_in, conv_a, w_a_out, conv_b, ln_b_g, ln_b_b, w_b_out, ln_c_g, ln_c_b,
           w_s, b_s, w_c_out, w_o, norm_ffn, w_group, b_group, w_router, b_router, w1, w3, w2,
           norm_final):
    bsz, seq, d = x.shape
    depth = norm_mix.shape[0]
    bf = jnp.bfloat16
    x2 = x.reshape(bsz * seq, d)
    wr = jnp.zeros((depth, d, NLOG), jnp.float32)
    wr = wr.at[:, :, 0:N_GROUPS].set(w_group).at[:, :, E_OFF:E_OFF + N_EXPERTS].set(w_router)
    wr_hi = wr.astype(bf)
    wr_lo = (wr - wr_hi.astype(jnp.float32)).astype(bf)
    br = jnp.zeros((depth, 1, NLOG), jnp.float32)
    br = br.at[:, 0, 0:N_GROUPS].set(b_group).at[:, 0, E_OFF:E_OFF + N_EXPERTS].set(b_router)
    bsb = jnp.broadcast_to(b_s[:, :, :, None], (depth, N_HEADS_C, CHUNK, CHUNK))
    row = lambda p: p[:, None, :]
    mixer_params = (row(norm_mix), w_in.astype(bf), conv_a, w_a_out.astype(bf), conv_b,
                    row(ln_b_g), row(ln_b_b), w_b_out.astype(bf), row(ln_c_g), row(ln_c_b),
                    w_s.astype(bf), bsb, w_c_out.astype(bf), w_o.astype(bf), row(norm_ffn), wr_hi, wr_lo, br)
    pending = None
    for l in range(depth):
        x2, h2, idx, wgt, cnt = _mixer(l, x2, pending, seq, *mixer_params)
        gathered = _moe(l, h2, idx, cnt, w1, w3, w2)
        pending = (gathered, wgt.T)
    out = _combine(gathered, x2, wgt, norm_final[None], True)
    return out.reshape(bsz, seq, d)
```

```python
import functools

import jax
import jax.numpy as jnp
from jax import lax
from jax.experimental import pallas as pl
from jax.experimental.pallas import tpu as pltpu
from jax.experimental.pallas import tpu_sc as plsc

D = 1024
DP = D // 2
NPIECE = DP // 128
N_HEADS_C = 8
CHUNK = 128
CONV_A = 3
CONV_B = 31
N_GROUPS = 4
EPG = 8
N_EXPERTS = N_GROUPS * EPG
D_FF = 512
RMS_EPS = 1e-6
LN_EPS = 1e-5

C_XA, C_BA, C_CA, C_VB, C_GB, C_UV, C_G = 0, 1024, 2048, 3072, 4096, 5120, 7168
D_IN = 10240

TM = 512
HALO = 16
TE = TM + 2 * HALO
CW = 256
RB = 64
NLOG = 128
E_OFF = 8

TR = TM
BM = 512
TD = 512

VMEM_LIMIT = 60 * 1024 * 1024


def _sigmoid(x):
    return 0.5 * (jnp.tanh(0.5 * x) + 1.0)


def _gelu_tanh(x):
    return 0.5 * x * (1.0 + jnp.tanh(0.7978845608028654 * (x + 0.044715 * (x * x * x))))


def _pack_bf16_pairs(x):
    c = x.shape[1] // 2
    as_bits = lambda v: lax.bitcast_convert_type(v.astype(jnp.bfloat16).astype(jnp.float32), jnp.uint32)
    return as_bits(x[:, :c]) | (as_bits(x[:, c:]) >> 16)


def _unpack_bf16_pairs(p):
    hi = lax.bitcast_convert_type(p & jnp.uint32(0xFFFF0000), jnp.float32)
    lo = lax.bitcast_convert_type(p << 16, jnp.float32)
    return hi, lo


def _layer_norm(x, g, b):
    mu = jnp.mean(x, axis=-1, keepdims=True)
    xc = x - mu
    var = jnp.mean(xc * xc, axis=-1, keepdims=True)
    return xc * lax.rsqrt(var + LN_EPS) * g + b


def _moe_residual(x_ref, g_ref, wt_ref):
    w0, w1 = wt_ref[:, 0:1], wt_ref[:, 1:2]
    his, los = [], []
    for j in range(NPIECE):
        hi0, lo0 = _unpack_bf16_pairs(lax.bitcast_convert_type(g_ref[0, j], jnp.uint32))
        hi1, lo1 = _unpack_bf16_pairs(lax.bitcast_convert_type(g_ref[1, j], jnp.uint32))
        his.append(w0 * hi0 + w1 * hi1)
        los.append(w0 * lo0 + w1 * lo1)
    return x_ref[...] + jnp.concatenate(his + los, axis=1)


def _mixer_kernel(seq_tiles, fused_combine, xp_ref, xc_ref, xn_ref, *refs):
    if fused_combine:
        gp_ref, gc_ref, gn_ref, wtp_ref, wtc_ref, wtn_ref = refs[:6]
        refs = refs[6:]
        x_prev = lambda: _moe_residual(xp_ref, gp_ref, wtp_ref)
        x_cur = lambda: _moe_residual(xc_ref, gc_ref, wtc_ref)
        x_next = lambda: _moe_residual(xn_ref, gn_ref, wtn_ref)
    else:
        x_prev, x_cur, x_next = (lambda: xp_ref[...]), (lambda: xc_ref[...]), (lambda: xn_ref[...])
    (nm_ref, win_ref, ca_ref, wa_ref, cb_ref, lnbg_ref, lnbb_ref, wb_ref, lncg_ref, lncb_ref,
     ws_ref, bsb_ref, wc_ref, wo_ref, nf_ref, wrh_ref, wrl_ref, br_ref,
     xo_ref, h2_ref, idx_ref, wgt_ref, cnt_ref,
     hb_ref, zq_ref, cva_ref, cvb_ref, cvn_ref, vst_ref, mg_ref, carry_ref) = refs
    i = pl.program_id(0)
    at_start = (i % seq_tiles) == 0
    at_end = (i % seq_tiles) == seq_tiles - 1

    def _rms(xv):
        ms = jnp.mean(xv * xv, axis=-1, keepdims=True)
        return (xv * lax.rsqrt(ms + RMS_EPS) * nm_ref[...]).astype(jnp.bfloat16)

    hb_ref[0:HALO, :] = _rms(x_prev())
    hb_ref[HALO:HALO + TM, :] = _rms(x_cur())
    hb_ref[HALO + TM:TE, :] = _rms(x_next())

    rows = lax.broadcasted_iota(jnp.int32, (TE, 1), 0)
    lo = jnp.where(at_start, HALO, 0)
    hi = jnp.where(at_end, HALO + TM, TE)
    valid = jnp.logical_and(rows >= lo, rows < hi)

    def proj(r0, r1, c0, width):
        return jnp.dot(hb_ref[r0:r1, :], win_ref[:, c0:c0 + width],
                       preferred_element_type=jnp.float32)

    def b_proj(c):
        c0 = c * CW
        val = proj(0, TE, C_VB + c0, CW)
        gate = proj(0, TE, C_GB + c0, CW)
        zq_ref[c] = jnp.where(valid, val * _sigmoid(gate), 0.0)

    def a_proj(c):
        c0 = c * CW
        xa = proj(0, TE, C_XA + c0, CW)
        cc = proj(0, TE, C_CA + c0, CW)
        t = jnp.where(valid, xa * cc, 0.0)
        conv = (ca_ref[0:1, c0:c0 + CW] * t[HALO - 1:HALO - 1 + TM]
                + ca_ref[1:2, c0:c0 + CW] * t[HALO:HALO + TM]
                + ca_ref[2:3, c0:c0 + CW] * t[HALO + 1:HALO + 1 + TM])
        ba = proj(HALO, HALO + TM, C_BA + c0, CW)
        cva_ref[:, c0:c0 + CW] = (ba * conv).astype(jnp.bfloat16)

    def c_v_proj(c):
        c0 = c * CW
        vst_ref[:, c0:c0 + CW] = _gelu_tanh(proj(HALO, HALO + TM, C_UV + D + c0, CW))

    def b_conv(c, row_blocks):
        c0 = c * CW
        zq = zq_ref.at[c]
        for rb in row_blocks:
            r0 = rb * RB
            for lt in range(CW // 128):
                l0 = lt * 128
                acc = None
                for s in range(8):
                    part = None
                    for q in range(4):
                        k = 8 * q + s - (HALO - CONV_B // 2)
                        if 0 <= k < CONV_B:
                            term = (cb_ref[k:k + 1, c0 + l0:c0 + l0 + 128]
                                    * zq[r0 + 8 * q:r0 + 8 * q + RB + 8, l0:l0 + 128])
                            part = term if part is None else part + term
                    shifted = part[s:s + RB]
                    acc = shifted if acc is None else acc + shifted
                xo_ref[r0:r0 + RB, c0 + l0:c0 + l0 + 128] = acc

    n_chunks = D // CW
    n_rb = TM // RB
    rb_step = n_rb // n_chunks
    late_rows = lambda c: range(c * rb_step, (c + 1) * rb_step)
    b_proj(0)
    for c in range(n_chunks):
        if c + 1 < n_chunks:
            b_proj(c + 1)
        a_proj(c)
        c_v_proj(c)
    b_conv(0, range(n_rb))

    vn = _layer_norm(vst_ref[...], lncg_ref[...], lncb_ref[...])
    cvn_ref[...] = vn.astype(jnp.bfloat16)
    for c in range(n_chunks):
        c0 = c * CW
        ya = jnp.dot(cva_ref[...], wa_ref[:, c0:c0 + CW], preferred_element_type=jnp.float32)
        ga = _sigmoid(proj(HALO, HALO + TM, C_G + c0, CW))
        mg_ref[:, c0:c0 + CW] = ga * ya
        b_conv(1, late_rows(c))

    for c in range(n_chunks):
        c0 = c * CW
        u = _gelu_tanh(proj(HALO, HALO + TM, C_UV + c0, CW))
        for n in range(TM // CHUNK):
            for hh in range(CW // CHUNK):
                h = c * (CW // CHUNK) + hh
                sv = jnp.dot(ws_ref[h], cvn_ref[n * CHUNK:(n + 1) * CHUNK, h * CHUNK:(h + 1) * CHUNK],
                             preferred_element_type=jnp.float32) + bsb_ref[h]
                cva_ref[n * CHUNK:(n + 1) * CHUNK, h * CHUNK:(h + 1) * CHUNK] = (
                    u[n * CHUNK:(n + 1) * CHUNK, hh * CHUNK:(hh + 1) * CHUNK] * sv
                ).astype(jnp.bfloat16)
        b_conv(2, late_rows(c))
    for c in range(n_chunks):
        c0 = c * CW
        yc = jnp.dot(cva_ref[...], wc_ref[:, c0:c0 + CW], preferred_element_type=jnp.float32)
        gc = _sigmoid(proj(HALO, HALO + TM, C_G + 2 * D + c0, CW))
        mg_ref[:, c0:c0 + CW] += gc * yc
        b_conv(3, late_rows(c))

    zc = _layer_norm(xo_ref[...], lnbg_ref[...], lnbb_ref[...])
    cvb_ref[...] = (zc * _sigmoid(zc)).astype(jnp.bfloat16)
    for c in range(n_chunks):
        c0 = c * CW
        yb = jnp.dot(cvb_ref[...], wb_ref[:, c0:c0 + CW], preferred_element_type=jnp.float32)
        gb = _sigmoid(proj(HALO, HALO + TM, C_G + D + c0, CW))
        mg_ref[:, c0:c0 + CW] += gb * yb

    cvb_ref[...] = mg_ref[...].astype(jnp.bfloat16)
    x_res = x_cur()
    for c in range(n_chunks):
        c0 = c * CW
        xo_ref[:, c0:c0 + CW] = x_res[:, c0:c0 + CW] + jnp.dot(
            cvb_ref[...], wo_ref[:, c0:c0 + CW], preferred_element_type=jnp.float32)
    xnew = xo_ref[...]
    ms = jnp.mean(xnew * xnew, axis=-1, keepdims=True)
    h2 = xnew * lax.rsqrt(ms + RMS_EPS) * nf_ref[...]
    h2_packed = _pack_bf16_pairs(h2)
    for j in range(NPIECE):
        h2_ref[j] = lax.bitcast_convert_type(h2_packed[:, j * 128:(j + 1) * 128], jnp.int32)
    h_hi = h2.astype(jnp.bfloat16)
    h_lo = (h2 - h_hi.astype(jnp.float32)).astype(jnp.bfloat16)
    dot32 = functools.partial(jnp.dot, preferred_element_type=jnp.float32)
    w_hl = jnp.concatenate([wrh_ref[...], wrl_ref[...]], axis=1)
    parts = dot32(h_hi, w_hl) + dot32(h_lo, w_hl)
    logits = parts[:, 0:NLOG] + parts[:, NLOG:2 * NLOG] + br_ref[...]
    _route_tile(logits, idx_ref, wgt_ref, cnt_ref, carry_ref)


def _layer_spec(layer, shape):
    nd = len(shape)
    return pl.BlockSpec((None,) + tuple(shape), lambda i, _n=nd: (layer,) + (0,) * _n,
                        pipeline_mode=pl.Buffered(1))


def _mixer(layer, x2, pending, seq_len, nm, win, ca, wa, cb, lnbg, lnbb, wb, lncg, lncb, ws, bsb, wc, wo,
           nf, wrh, wrl, br):
    T = x2.shape[0]
    n_tiles = T // TM
    hb = TM // HALO
    last_halo = T // HALO - 1
    ls = functools.partial(_layer_spec, layer)
    prev_rows = lambda i: jnp.maximum(i * hb - 1, 0)
    next_rows = lambda i: jnp.minimum((i + 1) * hb, last_halo)
    in_specs = [
        pl.BlockSpec((HALO, D), lambda i: (prev_rows(i), 0)),
        pl.BlockSpec((TM, D), lambda i: (i, 0)),
        pl.BlockSpec((HALO, D), lambda i: (next_rows(i), 0)),
    ]
    operands = [x2, x2, x2]
    if pending is not None:
        gathered, wgt_rows = pending
        in_specs += [
            pl.BlockSpec((2, NPIECE, HALO, 128), lambda i: (0, 0, prev_rows(i), 0)),
            pl.BlockSpec((2, NPIECE, TM, 128), lambda i: (0, 0, i, 0)),
            pl.BlockSpec((2, NPIECE, HALO, 128), lambda i: (0, 0, next_rows(i), 0)),
            pl.BlockSpec((HALO, 8), lambda i: (prev_rows(i), 0)),
            pl.BlockSpec((TM, 8), lambda i: (i, 0)),
            pl.BlockSpec((HALO, 8), lambda i: (next_rows(i), 0)),
        ]
        operands += [gathered, gathered, gathered, wgt_rows, wgt_rows, wgt_rows]
    in_specs += [
        ls((1, D)), ls((D, D_IN)), ls((CONV_A, D)), ls((D, D)),
        ls((CONV_B, D)), ls((1, D)), ls((1, D)), ls((D, D)),
        ls((1, D)), ls((1, D)), ls((N_HEADS_C, CHUNK, CHUNK)),
        ls((N_HEADS_C, CHUNK, CHUNK)), ls((D, D)), ls((D, D)),
        ls((1, D)), ls((D, NLOG)), ls((D, NLOG)), ls((1, NLOG)),
    ]
    out_specs = [
        pl.BlockSpec((TM, D), lambda i: (i, 0)),
        pl.BlockSpec((NPIECE, TM, 128), lambda i: (0, i, 0)),
        pl.BlockSpec((8, TM), lambda i: (0, i)),
        pl.BlockSpec((8, TM), lambda i: (0, i)),
        pl.BlockSpec((N_EXPERTS, 128), lambda i: (0, 0)),
    ]
    return pl.pallas_call(
        functools.partial(_mixer_kernel, seq_len // TM, pending is not None),
        grid=(n_tiles,),
        in_specs=in_specs,
        out_specs=out_specs,
        out_shape=[jax.ShapeDtypeStruct((T, D), jnp.float32),
                   jax.ShapeDtypeStruct((NPIECE, T, 128), jnp.int32),
                   jax.ShapeDtypeStruct((8, T), jnp.int32),
                   jax.ShapeDtypeStruct((8, T), jnp.float32),
                   jax.ShapeDtypeStruct((N_EXPERTS, 128), jnp.float32)],
        scratch_shapes=[pltpu.VMEM((TE, D), jnp.bfloat16),
                        pltpu.VMEM((D // CW, TE, CW), jnp.float32),
                        pltpu.VMEM((TM, D), jnp.bfloat16),
                        pltpu.VMEM((TM, D), jnp.bfloat16),
                        pltpu.VMEM((TM, D), jnp.bfloat16),
                        pltpu.VMEM((TM, D), jnp.float32),
                        pltpu.VMEM((TM, D), jnp.float32),
                        pltpu.VMEM((N_EXPERTS, 128), jnp.float32)],
        compiler_params=pltpu.CompilerParams(dimension_semantics=("arbitrary",),
                                             vmem_limit_bytes=VMEM_LIMIT),
        name="mixer",
    )(*operands, nm, win, ca, wa, cb, lnbg, lnbb, wb, lncg, lncb, ws, bsb, wc, wo, nf, wrh, wrl, br)


def _route_tile(lg, idx_ref, wgt_ref, cnt_ref, carry_ref):
    i = pl.program_id(0)

    @pl.when(i == 0)
    def _():
        carry_ref[...] = jnp.zeros_like(carry_ref)

    lt = lg.T
    g = [lt[j:j + 1, :] for j in range(N_GROUPS)]
    gmax = jnp.maximum(jnp.maximum(g[0], g[1]), jnp.maximum(g[2], g[3]))
    gidx = jnp.where(g[0] == gmax, 0.0, jnp.where(g[1] == gmax, 1.0, jnp.where(g[2] == gmax, 2.0, 3.0)))
    gsum = sum(jnp.exp(gj - gmax) for gj in g)
    g_p = 1.0 / gsum

    sel = lt[E_OFF + 3 * EPG:E_OFF + 4 * EPG, :]
    for j in (2, 1, 0):
        sel = jnp.where(gidx == float(j), lt[E_OFF + j * EPG:E_OFF + (j + 1) * EPG, :], sel)
    rid = lax.broadcasted_iota(jnp.int32, (EPG, TR), 0).astype(jnp.float32)
    m1 = jnp.max(sel, axis=0, keepdims=True)
    i1 = jnp.min(jnp.where(sel == m1, rid, float(EPG)), axis=0, keepdims=True)
    rest = jnp.where(rid == i1, -jnp.inf, sel)
    m2 = jnp.max(rest, axis=0, keepdims=True)
    i2 = jnp.min(jnp.where(rest == m2, rid, float(EPG)), axis=0, keepdims=True)
    e2x = jnp.exp(m2 - m1)
    den = 1.0 + e2x
    w1 = (1.0 / den) * g_p
    w2 = (e2x / den) * g_p
    e1 = gidx * float(EPG) + i1
    e2 = gidx * float(EPG) + i2

    eid = lax.broadcasted_iota(jnp.int32, (N_EXPERTS, TR), 0).astype(jnp.float32)
    oh1 = (eid == e1).astype(jnp.float32)
    oh2 = (eid == e2).astype(jnp.float32)
    oh = oh1 + oh2
    tr = lax.broadcasted_iota(jnp.int32, (TR, TR), 0)
    tc = lax.broadcasted_iota(jnp.int32, (TR, TR), 1)
    upper = (tr < tc).astype(jnp.bfloat16)
    before = jnp.dot(oh.astype(jnp.bfloat16), upper, preferred_element_type=jnp.float32)
    base = before + carry_ref[:, 0:1]
    r1 = jnp.sum(oh1 * base, axis=0, keepdims=True)
    r2 = jnp.sum(oh2 * base, axis=0, keepdims=True)
    carry_ref[...] = carry_ref[...] + jnp.sum(oh, axis=1, keepdims=True)

    idx_ref[...] = jnp.zeros_like(idx_ref)
    idx_ref[0:1, :] = e1.astype(jnp.int32)
    idx_ref[1:2, :] = e2.astype(jnp.int32)
    idx_ref[2:3, :] = r1.astype(jnp.int32)
    idx_ref[3:4, :] = r2.astype(jnp.int32)
    wgt_ref[...] = jnp.zeros_like(wgt_ref)
    wgt_ref[0:1, :] = w1
    wgt_ref[1:2, :] = w2
    cnt_ref[...] = carry_ref[...]


SC_WINDOW = 128


def _sc_mesh():
    return plsc.VectorSubcoreMesh(core_axis_name="c", subcore_axis_name="s")


def _sc_scatter_rows(rows, idx, passes, n_out):
    n_src = rows.shape[0]
    src_blocks = n_src // SC_WINDOW

    @functools.partial(
        pl.kernel, mesh=_sc_mesh(),
        out_type=jax.ShapeDtypeStruct((n_out, 128), rows.dtype))
    def scatter(rows_hbm, idx_hbm, out_hbm):
        def body(rows_vmem, idx_vmem):
            pltpu.sync_copy(rows_vmem, out_hbm.at[idx_vmem.at[0]])

        pltpu.emit_pipeline(
            body,
            grid=(passes * src_blocks,),
            in_specs=[pl.BlockSpec((SC_WINDOW, 128), lambda i: (i % src_blocks, 0)),
                      pl.BlockSpec((1, SC_WINDOW), lambda i: (0, i))],
            out_specs=[],
            core_axis_name=("c", "s"),
            dimension_semantics=(pltpu.PARALLEL,),
        )(rows_hbm, idx_hbm)

    return scatter(rows, idx.reshape(1, passes * n_src))


def _expert_kernel(layer, be_ref, nu_ref, nxt_ref, gs_ref, nv_ref, xb_ref, w1_ref, w3_ref, w2_ref, yb_ref,
                   wf1_ref, wf3_ref, wf2_ref, w1b_ref, w3b_ref, w2b_ref, sem):
    b = pl.program_id(0)
    used = b < nu_ref[0]
    new_expert = jnp.logical_or(b == 0, be_ref[b] != be_ref[jnp.maximum(b - 1, 0)])

    def weight_copies(e, s):
        return [pltpu.make_async_copy(src.at[layer, e], dst.at[s], sem.at[s])
                for src, dst in ((w1_ref, wf1_ref), (w3_ref, wf3_ref), (w2_ref, wf2_ref))]

    @pl.when(b == 0)
    def _():
        for cp in weight_copies(be_ref[0], 0):
            cp.start(priority=1)

    @pl.when(jnp.logical_and(used, new_expert))
    def _():
        s = gs_ref[b]
        for cp in weight_copies(be_ref[b], s):
            cp.wait()

        @pl.when(nxt_ref[b] >= 0)
        def _():
            for cp in weight_copies(nxt_ref[b], 1 - s):
                cp.start(priority=1)

        w1b_ref[...] = wf1_ref[s].astype(jnp.bfloat16)
        w3b_ref[...] = wf3_ref[s].astype(jnp.bfloat16)
        w2b_ref[...] = wf2_ref[s].astype(jnp.bfloat16)

    def expert_mlp(rows):
        live = lax.broadcasted_iota(jnp.int32, (rows, 1), 0) < nv_ref[b]
        packed_in = jnp.concatenate(
            [jnp.where(live, lax.bitcast_convert_type(xb_ref[j, 0:rows], jnp.uint32), jnp.uint32(0))
             for j in range(NPIECE)], axis=1)
        x = jnp.concatenate(_unpack_bf16_pairs(packed_in), axis=1).astype(jnp.bfloat16)
        dot32 = functools.partial(jnp.dot, preferred_element_type=jnp.float32)
        a = dot32(x, w1b_ref[...])
        g = dot32(x, w3b_ref[...])
        hmid = (a * _sigmoid(a) * g).astype(jnp.bfloat16)
        packed = _pack_bf16_pairs(dot32(hmid, w2b_ref[...]))
        for j in range(NPIECE):
            yb_ref[j, 0:rows] = lax.bitcast_convert_type(packed[:, j * 128:(j + 1) * 128], jnp.int32)

    half = BM // 2
    half_full = nv_ref[b] <= half

    @pl.when(jnp.logical_and(used, jnp.logical_not(half_full)))
    def _():
        expert_mlp(BM)

    @pl.when(jnp.logical_and(used, half_full))
    def _():
        expert_mlp(half)
        yb_ref[:, half:BM, :] = jnp.zeros((NPIECE, BM - half, 128), jnp.int32)

    @pl.when(jnp.logical_not(used))
    def _():
        yb_ref[...] = jnp.zeros_like(yb_ref)


def _experts(layer, block_expert, n_used, next_expert, group_slot, block_valid, xb, w1, w3, w2):
    n_slots = xb.shape[1]
    n_blocks = n_slots // BM

    def row_map(b, be, nu, nxt, gs, nv):
        return (0, jnp.minimum(b, nu[0] - 1), 0)

    def out_map(b, be, nu, nxt, gs, nv):
        return (0, b, 0)

    grid_spec = pltpu.PrefetchScalarGridSpec(
        num_scalar_prefetch=5,
        grid=(n_blocks,),
        in_specs=[pl.BlockSpec((NPIECE, BM, 128), row_map),
                  pl.BlockSpec(memory_space=pl.ANY),
                  pl.BlockSpec(memory_space=pl.ANY),
                  pl.BlockSpec(memory_space=pl.ANY)],
        out_specs=pl.BlockSpec((NPIECE, BM, 128), out_map),
        scratch_shapes=[pltpu.VMEM((2, D, D_FF), jnp.float32),
                        pltpu.VMEM((2, D, D_FF), jnp.float32),
                        pltpu.VMEM((2, D_FF, D), jnp.float32),
                        pltpu.VMEM((D, D_FF), jnp.bfloat16),
                        pltpu.VMEM((D, D_FF), jnp.bfloat16),
                        pltpu.VMEM((D_FF, D), jnp.bfloat16),
                        pltpu.SemaphoreType.DMA((2,))],
    )
    return pl.pallas_call(
        functools.partial(_expert_kernel, layer),
        grid_spec=grid_spec,
        out_shape=jax.ShapeDtypeStruct((NPIECE, n_slots, 128), jnp.int32),
        compiler_params=pltpu.CompilerParams(dimension_semantics=("arbitrary",),
                                             vmem_limit_bytes=VMEM_LIMIT),
        name="experts",
    )(block_expert, n_used, next_expert, group_slot, block_valid, xb, w1, w3, w2)


def _sc_gather_rows(table, idx):
    n_rows = idx.shape[0]

    @functools.partial(
        pl.kernel, mesh=_sc_mesh(),
        out_type=jax.ShapeDtypeStruct((n_rows, 128), table.dtype))
    def gather(table_hbm, idx_hbm, out_hbm):
        def body(idx_vmem, out_vmem):
            pltpu.sync_copy(table_hbm.at[idx_vmem.at[0]], out_vmem)

        pltpu.emit_pipeline(
            body,
            grid=(n_rows // SC_WINDOW,),
            in_specs=[pl.BlockSpec((1, SC_WINDOW), lambda i: (0, i))],
            out_specs=[pl.BlockSpec((SC_WINDOW, 128), lambda i: (i, 0))],
            core_axis_name=("c", "s"),
            dimension_semantics=(pltpu.PARALLEL,),
        )(idx_hbm, out_hbm)

    return gather(table, idx.reshape(1, n_rows))


def _combine_kernel(final, g_ref, x_ref, w_ref, nrm_ref, o_ref):
    w = w_ref[...].T
    ys = {}
    for j in range(NPIECE):
        hi0, lo0 = _unpack_bf16_pairs(lax.bitcast_convert_type(g_ref[0, j], jnp.uint32))
        hi1, lo1 = _unpack_bf16_pairs(lax.bitcast_convert_type(g_ref[1, j], jnp.uint32))
        c_hi, c_lo = j * 128, DP + j * 128
        ys[c_hi] = x_ref[:, c_hi:c_hi + 128] + w[:, 0:1] * hi0 + w[:, 1:2] * hi1
        ys[c_lo] = x_ref[:, c_lo:c_lo + 128] + w[:, 0:1] * lo0 + w[:, 1:2] * lo1
    if final:
        ms = sum(jnp.sum(y * y, axis=-1, keepdims=True) for y in ys.values()) * (1.0 / D)
        scale = lax.rsqrt(ms + RMS_EPS)
        ys = {c: y * scale * nrm_ref[:, c:c + 128] for c, y in ys.items()}
    for c, y in ys.items():
        o_ref[:, c:c + 128] = y


def _combine(gathered, x2, wgt, nrm, final):
    T = x2.shape[0]
    return pl.pallas_call(
        functools.partial(_combine_kernel, final),
        grid=(T // TD,),
        in_specs=[pl.BlockSpec((2, NPIECE, TD, 128), lambda i: (0, 0, i, 0)),
                  pl.BlockSpec((TD, D), lambda i: (i, 0)),
                  pl.BlockSpec((8, TD), lambda i: (0, i)),
                  pl.BlockSpec((1, D), lambda i: (0, 0))],
        out_specs=pl.BlockSpec((TD, D), lambda i: (i, 0)),
        out_shape=jax.ShapeDtypeStruct((T, D), jnp.float32),
        compiler_params=pltpu.CompilerParams(dimension_semantics=("arbitrary",)),
        name="combine",
    )(gathered, x2, wgt, nrm)


def _moe(layer, h2, idx, cnt, w1, w3, w2):
    T = idx.shape[1]
    counts = cnt[:, 0].astype(jnp.int32)
    padded = (counts + BM - 1) // BM * BM
    padded_end = jnp.cumsum(padded)
    padded_start = padded_end - padded
    eids = jnp.arange(N_EXPERTS, dtype=jnp.int32)[:, None]
    start_of = lambda e: jnp.sum(jnp.where(e[None, :] == eids, padded_start[:, None], 0), axis=0)
    dest = jnp.stack([start_of(idx[0]) + idx[2], start_of(idx[1]) + idx[3]])
    n_blocks = (2 * T) // BM + N_EXPERTS
    block_start = jnp.arange(n_blocks, dtype=jnp.int32) * BM
    block_expert = jnp.minimum(
        jnp.sum((padded_end[None, :] <= block_start[:, None]).astype(jnp.int32), axis=1),
        N_EXPERTS - 1)
    n_used = (padded_end[-1] // BM).reshape(1)
    later = jnp.logical_and(eids.T > eids, (counts > 0)[None, :])
    next_of = jnp.min(jnp.where(later, eids.T, N_EXPERTS), axis=1)
    next_of = jnp.where(next_of < N_EXPERTS, next_of, -1)
    pick = lambda table: jnp.sum(
        jnp.where(block_expert[:, None] == eids.T, table[None, :], 0), axis=1)
    next_expert = pick(next_of)
    group_slot = pick(jnp.cumsum((counts > 0).astype(jnp.int32)) - 1) % 2
    block_valid = jnp.clip(pick(padded_start + counts) - block_start, 0, BM)
    n_slots = n_blocks * BM
    piece_base = (jnp.arange(NPIECE, dtype=jnp.int32) * n_slots)[None, :, None]
    rows = (dest[:, None, :] + piece_base).reshape(-1)
    xb = _sc_scatter_rows(h2.reshape(NPIECE * T, 128), rows, 2, NPIECE * n_slots)
    yb = _experts(layer, block_expert, n_used, next_expert, group_slot, block_valid,
                  xb.reshape(NPIECE, n_slots, 128), w1, w3, w2)
    gathered = _sc_gather_rows(yb.reshape(NPIECE * n_slots, 128), rows)
    return gathered.reshape(2, NPIECE, T, 128)


def kernel(x, norm_mix, w_in, conv_a, w_a_out, conv_b, ln_b_g, ln_b_b, w_b_out, ln_c_g, ln_c_b,
           w_s, b_s, w_c_out, w_o, norm_ffn, w_group, b_group, w_router, b_router, w1, w3, w2,
           norm_final):
    bsz, seq, d = x.shape
    depth = norm_mix.shape[0]
    bf = jnp.bfloat16
    x2 = x.reshape(bsz * seq, d)
    wr = jnp.zeros((depth, d, NLOG), jnp.float32)
    wr = wr.at[:, :, 0:N_GROUPS].set(w_group).at[:, :, E_OFF:E_OFF + N_EXPERTS].set(w_router)
    wr_hi = wr.astype(bf)
    wr_lo = (wr - wr_hi.astype(jnp.float32)).astype(bf)
    br = jnp.zeros((depth, 1, NLOG), jnp.float32)
    br = br.at[:, 0, 0:N_GROUPS].set(b_group).at[:, 0, E_OFF:E_OFF + N_EXPERTS].set(b_router)
    bsb = jnp.broadcast_to(b_s[:, :, :, None], (depth, N_HEADS_C, CHUNK, CHUNK))
    row = lambda p: p[:, None, :]
    mixer_params = (row(norm_mix), w_in.astype(bf), conv_a, w_a_out.astype(bf), conv_b,
                    row(ln_b_g), row(ln_b_b), w_b_out.astype(bf), row(ln_c_g), row(ln_c_b),
                    w_s.astype(bf), bsb, w_c_out.astype(bf), w_o.astype(bf), row(norm_ffn), wr_hi, wr_lo, br)
    pending = None
    for l in range(depth):
        x2, h2, idx, wgt, cnt = _mixer(l, x2, pending, seq, *mixer_params)
        gathered = _moe(l, h2, idx, cnt, w1, w3, w2)
        pending = (gathered, wgt.T)
    out = _combine(gathered, x2, wgt, norm_final[None], True)
    return out.reshape(bsz, seq, d)
```

```python
import functools

import jax
import jax.numpy as jnp
from jax import lax
from jax.experimental import pallas as pl
from jax.experimental.pallas import tpu as pltpu
from jax.experimental.pallas import tpu_sc as plsc

D = 1024
DP = D // 2
NPIECE = DP // 128
N_HEADS_C = 8
CHUNK = 128
CONV_A = 3
CONV_B = 31
N_GROUPS = 4
EPG = 8
N_EXPERTS = N_GROUPS * EPG
D_FF = 512
RMS_EPS = 1e-6
LN_EPS = 1e-5

C_XA, C_BA, C_CA, C_VB, C_GB, C_UV, C_G = 0, 1024, 2048, 3072, 4096, 5120, 7168
D_IN = 10240

TM = 512
HALO = 16
TE = TM + 2 * HALO
CW = 256
RB = 64
NLOG = 128
E_OFF = 8

TR = TM
BM = 512
TD = 512

VMEM_LIMIT = 60 * 1024 * 1024


def _sigmoid(x):
    return 0.5 * (jnp.tanh(0.5 * x) + 1.0)


def _gelu_tanh(x):
    return 0.5 * x * (1.0 + jnp.tanh(0.7978845608028654 * (x + 0.044715 * (x * x * x))))


def _pack_bf16_pairs(x):
    c = x.shape[1] // 2
    as_bits = lambda v: lax.bitcast_convert_type(v.astype(jnp.bfloat16).astype(jnp.float32), jnp.uint32)
    return as_bits(x[:, :c]) | (as_bits(x[:, c:]) >> 16)


def _unpack_bf16_pairs(p):
    hi = lax.bitcast_convert_type(p & jnp.uint32(0xFFFF0000), jnp.float32)
    lo = lax.bitcast_convert_type(p << 16, jnp.float32)
    return hi, lo


def _layer_norm(x, g, b):
    mu = jnp.mean(x, axis=-1, keepdims=True)
    xc = x - mu
    var = jnp.mean(xc * xc, axis=-1, keepdims=True)
    return xc * lax.rsqrt(var + LN_EPS) * g + b


def _moe_residual(x_ref, g_ref, wt_ref):
    w0, w1 = wt_ref[:, 0:1], wt_ref[:, 1:2]
    his, los = [], []
    for j in range(NPIECE):
        hi0, lo0 = _unpack_bf16_pairs(lax.bitcast_convert_type(g_ref[0, j], jnp.uint32))
        hi1, lo1 = _unpack_bf16_pairs(lax.bitcast_convert_type(g_ref[1, j], jnp.uint32))
        his.append(w0 * hi0 + w1 * hi1)
        los.append(w0 * lo0 + w1 * lo1)
    return x_ref[...] + jnp.concatenate(his + los, axis=1)


def _mixer_kernel(seq_tiles, fused_combine, xp_ref, xc_ref, xn_ref, *refs):
    if fused_combine:
        gp_ref, gc_ref, gn_ref, wtp_ref, wtc_ref, wtn_ref = refs[:6]
        refs = refs[6:]
        x_prev = lambda: _moe_residual(xp_ref, gp_ref, wtp_ref)
        x_cur = lambda: _moe_residual(xc_ref, gc_ref, wtc_ref)
        x_next = lambda: _moe_residual(xn_ref, gn_ref, wtn_ref)
    else:
        x_prev, x_cur, x_next = (lambda: xp_ref[...]), (lambda: xc_ref[...]), (lambda: xn_ref[...])
    (nm_ref, win_ref, ca_ref, wa_ref, cb_ref, lnbg_ref, lnbb_ref, wb_ref, lncg_ref, lncb_ref,
     ws_ref, bsb_ref, wc_ref, wo_ref, nf_ref, wrh_ref, wrl_ref, br_ref,
     xo_ref, h2_ref, idx_ref, wgt_ref, cnt_ref,
     hb_ref, zq_ref, cva_ref, cvb_ref, cvn_ref, vst_ref, mg_ref, carry_ref) = refs
    i = pl.program_id(0)
    at_start = (i % seq_tiles) == 0
    at_end = (i % seq_tiles) == seq_tiles - 1

    def _rms(xv):
        ms = jnp.mean(xv * xv, axis=-1, keepdims=True)
        return (xv * lax.rsqrt(ms + RMS_EPS) * nm_ref[...]).astype(jnp.bfloat16)

    hb_ref[0:HALO, :] = _rms(x_prev())
    hb_ref[HALO:HALO + TM, :] = _rms(x_cur())
    hb_ref[HALO + TM:TE, :] = _rms(x_next())

    rows = lax.broadcasted_iota(jnp.int32, (TE, 1), 0)
    lo = jnp.where(at_start, HALO, 0)
    hi = jnp.where(at_end, HALO + TM, TE)
    valid = jnp.logical_and(rows >= lo, rows < hi)

    def proj(r0, r1, c0, width):
        return jnp.dot(hb_ref[r0:r1, :], win_ref[:, c0:c0 + width],
                       preferred_element_type=jnp.float32)

    def b_proj(c):
        c0 = c * CW
        val = proj(0, TE, C_VB + c0, CW)
        gate = proj(0, TE, C_GB + c0, CW)
        zq_ref[c] = jnp.where(valid, val * _sigmoid(gate), 0.0)

    def a_proj(c):
        c0 = c * CW
        xa = proj(0, TE, C_XA + c0, CW)
        cc = proj(0, TE, C_CA + c0, CW)
        t = jnp.where(valid, xa * cc, 0.0)
        conv = (ca_ref[0:1, c0:c0 + CW] * t[HALO - 1:HALO - 1 + TM]
                + ca_ref[1:2, c0:c0 + CW] * t[HALO:HALO + TM]
                + ca_ref[2:3, c0:c0 + CW] * t[HALO + 1:HALO + 1 + TM])
        ba = proj(HALO, HALO + TM, C_BA + c0, CW)
        cva_ref[:, c0:c0 + CW] = (ba * conv).astype(jnp.bfloat16)

    def c_v_proj(c):
        c0 = c * CW
        vst_ref[:, c0:c0 + CW] = _gelu_tanh(proj(HALO, HALO + TM, C_UV + D + c0, CW))

    def b_conv(c, row_blocks):
        c0 = c * CW
        zq = zq_ref.at[c]
        for rb in row_blocks:
            r0 = rb * RB
            for lt in range(CW // 128):
                l0 = lt * 128
                acc = None
                for s in range(8):
                    part = None
                    for q in range(4):
                        k = 8 * q + s - (HALO - CONV_B // 2)
                        if 0 <= k < CONV_B:
                            term = (cb_ref[k:k + 1, c0 + l0:c0 + l0 + 128]
                                    * zq[r0 + 8 * q:r0 + 8 * q + RB + 8, l0:l0 + 128])
                            part = term if part is None else part + term
                    shifted = part[s:s + RB]
                    acc = shifted if acc is None else acc + shifted
                xo_ref[r0:r0 + RB, c0 + l0:c0 + l0 + 128] = acc

    n_chunks = D // CW
    n_rb = TM // RB
    rb_step = n_rb // n_chunks
    late_rows = lambda c: range(c * rb_step, (c + 1) * rb_step)
    b_proj(0)
    for c in range(n_chunks):
        if c + 1 < n_chunks:
            b_proj(c + 1)
        a_proj(c)
        c_v_proj(c)
    b_conv(0, range(n_rb))

    vn = _layer_norm(vst_ref[...], lncg_ref[...], lncb_ref[...])
    cvn_ref[...] = vn.astype(jnp.bfloat16)
    for c in range(n_chunks):
        c0 = c * CW
        ya = jnp.dot(cva_ref[...], wa_ref[:, c0:c0 + CW], preferred_element_type=jnp.float32)
        ga = _sigmoid(proj(HALO, HALO + TM, C_G + c0, CW))
        mg_ref[:, c0:c0 + CW] = ga * ya
        b_conv(1, late_rows(c))

    for c in range(n_chunks):
        c0 = c * CW
        u = _gelu_tanh(proj(HALO, HALO + TM, C_UV + c0, CW))
        for n in range(TM // CHUNK):
            for hh in range(CW // CHUNK):
                h = c * (CW // CHUNK) + hh
                sv = jnp.dot(ws_ref[h], cvn_ref[n * CHUNK:(n + 1) * CHUNK, h * CHUNK:(h + 1) * CHUNK],
                             preferred_element_type=jnp.float32) + bsb_ref[h]
                cva_ref[n * CHUNK:(n + 1) * CHUNK, h * CHUNK:(h + 1) * CHUNK] = (
                    u[n * CHUNK:(n + 1) * CHUNK, hh * CHUNK:(hh + 1) * CHUNK] * sv
                ).astype(jnp.bfloat16)
        b_conv(2, late_rows(c))
    for c in range(n_chunks):
        c0 = c * CW
        yc = jnp.dot(cva_ref[...], wc_ref[:, c0:c0 + CW], preferred_element_type=jnp.float32)
        gc = _sigmoid(proj(HALO, HALO + TM, C_G + 2 * D + c0, CW))
        mg_ref[:, c0:c0 + CW] += gc * yc
        b_conv(3, late_rows(c))

    zc = _layer_norm(xo_ref[...], lnbg_ref[...], lnbb_ref[...])
    cvb_ref[...] = (zc * _sigmoid(zc)).astype(jnp.bfloat16)
    for c in range(n_chunks):
        c0 = c * CW
        yb = jnp.dot(cvb_ref[...], wb_ref[:, c0:c0 + CW], preferred_element_type=jnp.float32)
        gb = _sigmoid(proj(HALO, HALO + TM, C_G + D + c0, CW))
        mg_ref[:, c0:c0 + CW] += gb * yb

    cvb_ref[...] = mg_ref[...].astype(jnp.bfloat16)
    x_res = x_cur()
    for c in range(n_chunks):
        c0 = c * CW
        xo_ref[:, c0:c0 + CW] = x_res[:, c0:c0 + CW] + jnp.dot(
            cvb_ref[...], wo_ref[:, c0:c0 + CW], preferred_element_type=jnp.float32)
    xnew = xo_ref[...]
    ms = jnp.mean(xnew * xnew, axis=-1, keepdims=True)
    h2 = xnew * lax.rsqrt(ms + RMS_EPS) * nf_ref[...]
    h2_packed = _pack_bf16_pairs(h2)
    for j in range(NPIECE):
        h2_ref[j] = lax.bitcast_convert_type(h2_packed[:, j * 128:(j + 1) * 128], jnp.int32)
    h_hi = h2.astype(jnp.bfloat16)
    h_lo = (h2 - h_hi.astype(jnp.float32)).astype(jnp.bfloat16)
    dot32 = functools.partial(jnp.dot, preferred_element_type=jnp.float32)
    logits = (dot32(h_hi, wrh_ref[...]) + dot32(h_lo, wrh_ref[...])
              + dot32(h_hi, wrl_ref[...]) + dot32(h_lo, wrl_ref[...]) + br_ref[...])
    _route_tile(logits, idx_ref, wgt_ref, cnt_ref, carry_ref)


def _layer_spec(layer, shape):
    nd = len(shape)
    return pl.BlockSpec((None,) + tuple(shape), lambda i, _n=nd: (layer,) + (0,) * _n,
                        pipeline_mode=pl.Buffered(1))


def _mixer(layer, x2, pending, seq_len, nm, win, ca, wa, cb, lnbg, lnbb, wb, lncg, lncb, ws, bsb, wc, wo,
           nf, wrh, wrl, br):
    T = x2.shape[0]
    n_tiles = T // TM
    hb = TM // HALO
    last_halo = T // HALO - 1
    ls = functools.partial(_layer_spec, layer)
    prev_rows = lambda i: jnp.maximum(i * hb - 1, 0)
    next_rows = lambda i: jnp.minimum((i + 1) * hb, last_halo)
    in_specs = [
        pl.BlockSpec((HALO, D), lambda i: (prev_rows(i), 0)),
        pl.BlockSpec((TM, D), lambda i: (i, 0)),
        pl.BlockSpec((HALO, D), lambda i: (next_rows(i), 0)),
    ]
    operands = [x2, x2, x2]
    if pending is not None:
        gathered, wgt_rows = pending
        in_specs += [
            pl.BlockSpec((2, NPIECE, HALO, 128), lambda i: (0, 0, prev_rows(i), 0)),
            pl.BlockSpec((2, NPIECE, TM, 128), lambda i: (0, 0, i, 0)),
            pl.BlockSpec((2, NPIECE, HALO, 128), lambda i: (0, 0, next_rows(i), 0)),
            pl.BlockSpec((HALO, 8), lambda i: (prev_rows(i), 0)),
            pl.BlockSpec((TM, 8), lambda i: (i, 0)),
            pl.BlockSpec((HALO, 8), lambda i: (next_rows(i), 0)),
        ]
        operands += [gathered, gathered, gathered, wgt_rows, wgt_rows, wgt_rows]
    in_specs += [
        ls((1, D)), ls((D, D_IN)), ls((CONV_A, D)), ls((D, D)),
        ls((CONV_B, D)), ls((1, D)), ls((1, D)), ls((D, D)),
        ls((1, D)), ls((1, D)), ls((N_HEADS_C, CHUNK, CHUNK)),
        ls((N_HEADS_C, CHUNK, CHUNK)), ls((D, D)), ls((D, D)),
        ls((1, D)), ls((D, NLOG)), ls((D, NLOG)), ls((1, NLOG)),
    ]
    out_specs = [
        pl.BlockSpec((TM, D), lambda i: (i, 0)),
        pl.BlockSpec((NPIECE, TM, 128), lambda i: (0, i, 0)),
        pl.BlockSpec((8, TM), lambda i: (0, i)),
        pl.BlockSpec((8, TM), lambda i: (0, i)),
        pl.BlockSpec((N_EXPERTS, 128), lambda i: (0, 0)),
    ]
    return pl.pallas_call(
        functools.partial(_mixer_kernel, seq_len // TM, pending is not None),
        grid=(n_tiles,),
        in_specs=in_specs,
        out_specs=out_specs,
        out_shape=[jax.ShapeDtypeStruct((T, D), jnp.float32),
                   jax.ShapeDtypeStruct((NPIECE, T, 128), jnp.int32),
                   jax.ShapeDtypeStruct((8, T), jnp.int32),
                   jax.ShapeDtypeStruct((8, T), jnp.float32),
                   jax.ShapeDtypeStruct((N_EXPERTS, 128), jnp.float32)],
        scratch_shapes=[pltpu.VMEM((TE, D), jnp.bfloat16),
                        pltpu.VMEM((D // CW, TE, CW), jnp.float32),
                        pltpu.VMEM((TM, D), jnp.bfloat16),
                        pltpu.VMEM((TM, D), jnp.bfloat16),
                        pltpu.VMEM((TM, D), jnp.bfloat16),
                        pltpu.VMEM((TM, D), jnp.float32),
                        pltpu.VMEM((TM, D), jnp.float32),
                        pltpu.VMEM((N_EXPERTS, 128), jnp.float32)],
        compiler_params=pltpu.CompilerParams(dimension_semantics=("arbitrary",),
                                             vmem_limit_bytes=VMEM_LIMIT),
        name="mixer",
    )(*operands, nm, win, ca, wa, cb, lnbg, lnbb, wb, lncg, lncb, ws, bsb, wc, wo, nf, wrh, wrl, br)


def _route_tile(lg, idx_ref, wgt_ref, cnt_ref, carry_ref):
    i = pl.program_id(0)

    @pl.when(i == 0)
    def _():
        carry_ref[...] = jnp.zeros_like(carry_ref)

    lt = lg.T
    g = [lt[j:j + 1, :] for j in range(N_GROUPS)]
    gmax = jnp.maximum(jnp.maximum(g[0], g[1]), jnp.maximum(g[2], g[3]))
    gidx = jnp.where(g[0] == gmax, 0.0, jnp.where(g[1] == gmax, 1.0, jnp.where(g[2] == gmax, 2.0, 3.0)))
    gsum = sum(jnp.exp(gj - gmax) for gj in g)
    g_p = 1.0 / gsum

    sel = lt[E_OFF + 3 * EPG:E_OFF + 4 * EPG, :]
    for j in (2, 1, 0):
        sel = jnp.where(gidx == float(j), lt[E_OFF + j * EPG:E_OFF + (j + 1) * EPG, :], sel)
    rid = lax.broadcasted_iota(jnp.int32, (EPG, TR), 0).astype(jnp.float32)
    m1 = jnp.max(sel, axis=0, keepdims=True)
    i1 = jnp.min(jnp.where(sel == m1, rid, float(EPG)), axis=0, keepdims=True)
    rest = jnp.where(rid == i1, -jnp.inf, sel)
    m2 = jnp.max(rest, axis=0, keepdims=True)
    i2 = jnp.min(jnp.where(rest == m2, rid, float(EPG)), axis=0, keepdims=True)
    e2x = jnp.exp(m2 - m1)
    den = 1.0 + e2x
    w1 = (1.0 / den) * g_p
    w2 = (e2x / den) * g_p
    e1 = gidx * float(EPG) + i1
    e2 = gidx * float(EPG) + i2

    eid = lax.broadcasted_iota(jnp.int32, (N_EXPERTS, TR), 0).astype(jnp.float32)
    oh1 = (eid == e1).astype(jnp.float32)
    oh2 = (eid == e2).astype(jnp.float32)
    oh = oh1 + oh2
    tr = lax.broadcasted_iota(jnp.int32, (TR, TR), 0)
    tc = lax.broadcasted_iota(jnp.int32, (TR, TR), 1)
    upper = (tr < tc).astype(jnp.bfloat16)
    before = jnp.dot(oh.astype(jnp.bfloat16), upper, preferred_element_type=jnp.float32)
    base = before + carry_ref[:, 0:1]
    r1 = jnp.sum(oh1 * base, axis=0, keepdims=True)
    r2 = jnp.sum(oh2 * base, axis=0, keepdims=True)
    carry_ref[...] = carry_ref[...] + jnp.sum(oh, axis=1, keepdims=True)

    idx_ref[...] = jnp.zeros_like(idx_ref)
    idx_ref[0:1, :] = e1.astype(jnp.int32)
    idx_ref[1:2, :] = e2.astype(jnp.int32)
    idx_ref[2:3, :] = r1.astype(jnp.int32)
    idx_ref[3:4, :] = r2.astype(jnp.int32)
    wgt_ref[...] = jnp.zeros_like(wgt_ref)
    wgt_ref[0:1, :] = w1
    wgt_ref[1:2, :] = w2
    cnt_ref[...] = carry_ref[...]


SC_WINDOW = 128


def _sc_mesh():
    return plsc.VectorSubcoreMesh(core_axis_name="c", subcore_axis_name="s")


def _sc_scatter_rows(rows, idx, passes, n_out):
    n_src = rows.shape[0]
    src_blocks = n_src // SC_WINDOW

    @functools.partial(
        pl.kernel, mesh=_sc_mesh(),
        out_type=jax.ShapeDtypeStruct((n_out, 128), rows.dtype))
    def scatter(rows_hbm, idx_hbm, out_hbm):
        def body(rows_vmem, idx_vmem):
            pltpu.sync_copy(rows_vmem, out_hbm.at[idx_vmem.at[0]])

        pltpu.emit_pipeline(
            body,
            grid=(passes * src_blocks,),
            in_specs=[pl.BlockSpec((SC_WINDOW, 128), lambda i: (i % src_blocks, 0)),
                      pl.BlockSpec((1, SC_WINDOW), lambda i: (0, i))],
            out_specs=[],
            core_axis_name=("c", "s"),
            dimension_semantics=(pltpu.PARALLEL,),
        )(rows_hbm, idx_hbm)

    return scatter(rows, idx.reshape(1, passes * n_src))


def _expert_kernel(layer, be_ref, nu_ref, nxt_ref, gs_ref, nv_ref, xb_ref, w1_ref, w3_ref, w2_ref, yb_ref,
                   wf1_ref, wf3_ref, wf2_ref, w1b_ref, w3b_ref, w2b_ref, sem):
    b = pl.program_id(0)
    used = b < nu_ref[0]
    new_expert = jnp.logical_or(b == 0, be_ref[b] != be_ref[jnp.maximum(b - 1, 0)])

    def weight_copies(e, s):
        return [pltpu.make_async_copy(src.at[layer, e], dst.at[s], sem.at[s])
                for src, dst in ((w1_ref, wf1_ref), (w3_ref, wf3_ref), (w2_ref, wf2_ref))]

    @pl.when(b == 0)
    def _():
        for cp in weight_copies(be_ref[0], 0):
            cp.start(priority=1)

    @pl.when(jnp.logical_and(used, new_expert))
    def _():
        s = gs_ref[b]
        for cp in weight_copies(be_ref[b], s):
            cp.wait()

        @pl.when(nxt_ref[b] >= 0)
        def _():
            for cp in weight_copies(nxt_ref[b], 1 - s):
                cp.start(priority=1)

        w1b_ref[...] = wf1_ref[s].astype(jnp.bfloat16)
        w3b_ref[...] = wf3_ref[s].astype(jnp.bfloat16)
        w2b_ref[...] = wf2_ref[s].astype(jnp.bfloat16)

    def expert_mlp(rows):
        live = lax.broadcasted_iota(jnp.int32, (rows, 1), 0) < nv_ref[b]
        packed_in = jnp.concatenate(
            [jnp.where(live, lax.bitcast_convert_type(xb_ref[j, 0:rows], jnp.uint32), jnp.uint32(0))
             for j in range(NPIECE)], axis=1)
        x = jnp.concatenate(_unpack_bf16_pairs(packed_in), axis=1).astype(jnp.bfloat16)
        dot32 = functools.partial(jnp.dot, preferred_element_type=jnp.float32)
        a = dot32(x, w1b_ref[...])
        g = dot32(x, w3b_ref[...])
        hmid = (a * _sigmoid(a) * g).astype(jnp.bfloat16)
        packed = _pack_bf16_pairs(dot32(hmid, w2b_ref[...]))
        for j in range(NPIECE):
            yb_ref[j, 0:rows] = lax.bitcast_convert_type(packed[:, j * 128:(j + 1) * 128], jnp.int32)

    half = BM // 2
    half_full = nv_ref[b] <= half

    @pl.when(jnp.logical_and(used, jnp.logical_not(half_full)))
    def _():
        expert_mlp(BM)

    @pl.when(jnp.logical_and(used, half_full))
    def _():
        expert_mlp(half)
        yb_ref[:, half:BM, :] = jnp.zeros((NPIECE, BM - half, 128), jnp.int32)

    @pl.when(jnp.logical_not(used))
    def _():
        yb_ref[...] = jnp.zeros_like(yb_ref)


def _experts(layer, block_expert, n_used, next_expert, group_slot, block_valid, xb, w1, w3, w2):
    n_slots = xb.shape[1]
    n_blocks = n_slots // BM

    def row_map(b, be, nu, nxt, gs, nv):
        return (0, jnp.minimum(b, nu[0] - 1), 0)

    def out_map(b, be, nu, nxt, gs, nv):
        return (0, b, 0)

    grid_spec = pltpu.PrefetchScalarGridSpec(
        num_scalar_prefetch=5,
        grid=(n_blocks,),
        in_specs=[pl.BlockSpec((NPIECE, BM, 128), row_map),
                  pl.BlockSpec(memory_space=pl.ANY),
                  pl.BlockSpec(memory_space=pl.ANY),
                  pl.BlockSpec(memory_space=pl.ANY)],
        out_specs=pl.BlockSpec((NPIECE, BM, 128), out_map),
        scratch_shapes=[pltpu.VMEM((2, D, D_FF), jnp.float32),
                        pltpu.VMEM((2, D, D_FF), jnp.float32),
                        pltpu.VMEM((2, D_FF, D), jnp.float32),
                        pltpu.VMEM((D, D_FF), jnp.bfloat16),
                        pltpu.VMEM((D, D_FF), jnp.bfloat16),
                        pltpu.VMEM((D_FF, D), jnp.bfloat16),
                        pltpu.SemaphoreType.DMA((2,))],
    )
    return pl.pallas_call(
        functools.partial(_expert_kernel, layer),
        grid_spec=grid_spec,
        out_shape=jax.ShapeDtypeStruct((NPIECE, n_slots, 128), jnp.int32),
        compiler_params=pltpu.CompilerParams(dimension_semantics=("arbitrary",),
                                             vmem_limit_bytes=VMEM_LIMIT),
        name="experts",
    )(block_expert, n_used, next_expert, group_slot, block_valid, xb, w1, w3, w2)


def _sc_gather_rows(table, idx):
    n_rows = idx.shape[0]

    @functools.partial(
        pl.kernel, mesh=_sc_mesh(),
        out_type=jax.ShapeDtypeStruct((n_rows, 128), table.dtype))
    def gather(table_hbm, idx_hbm, out_hbm):
        def body(idx_vmem, out_vmem):
            pltpu.sync_copy(table_hbm.at[idx_vmem.at[0]], out_vmem)

        pltpu.emit_pipeline(
            body,
            grid=(n_rows // SC_WINDOW,),
            in_specs=[pl.BlockSpec((1, SC_WINDOW), lambda i: (0, i))],
            out_specs=[pl.BlockSpec((SC_WINDOW, 128), lambda i: (i, 0))],
            core_axis_name=("c", "s"),
            dimension_semantics=(pltpu.PARALLEL,),
        )(idx_hbm, out_hbm)

    return gather(table, idx.reshape(1, n_rows))


def _combine_kernel(final, g_ref, x_ref, w_ref, nrm_ref, o_ref):
    w = w_ref[...].T
    ys = {}
    for j in range(NPIECE):
        hi0, lo0 = _unpack_bf16_pairs(lax.bitcast_convert_type(g_ref[0, j], jnp.uint32))
        hi1, lo1 = _unpack_bf16_pairs(lax.bitcast_convert_type(g_ref[1, j], jnp.uint32))
        c_hi, c_lo = j * 128, DP + j * 128
        ys[c_hi] = x_ref[:, c_hi:c_hi + 128] + w[:, 0:1] * hi0 + w[:, 1:2] * hi1
        ys[c_lo] = x_ref[:, c_lo:c_lo + 128] + w[:, 0:1] * lo0 + w[:, 1:2] * lo1
    if final:
        ms = sum(jnp.sum(y * y, axis=-1, keepdims=True) for y in ys.values()) * (1.0 / D)
        scale = lax.rsqrt(ms + RMS_EPS)
        ys = {c: y * scale * nrm_ref[:, c:c + 128] for c, y in ys.items()}
    for c, y in ys.items():
        o_ref[:, c:c + 128] = y


def _combine(gathered, x2, wgt, nrm, final):
    T = x2.shape[0]
    return pl.pallas_call(
        functools.partial(_combine_kernel, final),
        grid=(T // TD,),
        in_specs=[pl.BlockSpec((2, NPIECE, TD, 128), lambda i: (0, 0, i, 0)),
                  pl.BlockSpec((TD, D), lambda i: (i, 0)),
                  pl.BlockSpec((8, TD), lambda i: (0, i)),
                  pl.BlockSpec((1, D), lambda i: (0, 0))],
        out_specs=pl.BlockSpec((TD, D), lambda i: (i, 0)),
        out_shape=jax.ShapeDtypeStruct((T, D), jnp.float32),
        compiler_params=pltpu.CompilerParams(dimension_semantics=("arbitrary",)),
        name="combine",
    )(gathered, x2, wgt, nrm)


def _moe(layer, h2, idx, cnt, w1, w3, w2):
    T = idx.shape[1]
    counts = cnt[:, 0].astype(jnp.int32)
    padded = (counts + BM - 1) // BM * BM
    padded_end = jnp.cumsum(padded)
    padded_start = padded_end - padded
    eids = jnp.arange(N_EXPERTS, dtype=jnp.int32)[:, None]
    start_of = lambda e: jnp.sum(jnp.where(e[None, :] == eids, padded_start[:, None], 0), axis=0)
    dest = jnp.stack([start_of(idx[0]) + idx[2], start_of(idx[1]) + idx[3]])
    n_blocks = (2 * T) // BM + N_EXPERTS
    block_start = jnp.arange(n_blocks, dtype=jnp.int32) * BM
    block_expert = jnp.minimum(
        jnp.sum((padded_end[None, :] <= block_start[:, None]).astype(jnp.int32), axis=1),
        N_EXPERTS - 1)
    n_used = (padded_end[-1] // BM).reshape(1)
    later = jnp.logical_and(eids.T > eids, (counts > 0)[None, :])
    next_of = jnp.min(jnp.where(later, eids.T, N_EXPERTS), axis=1)
    next_of = jnp.where(next_of < N_EXPERTS, next_of, -1)
    pick = lambda table: jnp.sum(
        jnp.where(block_expert[:, None] == eids.T, table[None, :], 0), axis=1)
    next_expert = pick(next_of)
    group_slot = pick(jnp.cumsum((counts > 0).astype(jnp.int32)) - 1) % 2
    block_valid = jnp.clip(pick(padded_start + counts) - block_start, 0, BM)
    n_slots = n_blocks * BM
    piece_base = (jnp.arange(NPIECE, dtype=jnp.int32) * n_slots)[None, :, None]
    rows = (dest[:, None, :] + piece_base).reshape(-1)
    xb = _sc_scatter_rows(h2.reshape(NPIECE * T, 128), rows, 2, NPIECE * n_slots)
    yb = _experts(layer, block_expert, n_used, next_expert, group_slot, block_valid,
                  xb.reshape(NPIECE, n_slots, 128), w1, w3, w2)
    gathered = _sc_gather_rows(yb.reshape(NPIECE * n_slots, 128), rows)
    return gathered.reshape(2, NPIECE, T, 128)


def kernel(x, norm_mix, w_in, conv_a, w_a_out, conv_b, ln_b_g, ln_b_b, w_b_out, ln_c_g, ln_c_b,
           w_s, b_s, w_c_out, w_o, norm_ffn, w_group, b_group, w_router, b_router, w1, w3, w2,
           norm_final):
    bsz, seq, d = x.shape
    depth = norm_mix.shape[0]
    bf = jnp.bfloat16
    x2 = x.reshape(bsz * seq, d)
    wr = jnp.zeros((depth, d, NLOG), jnp.float32)
    wr = wr.at[:, :, 0:N_GROUPS].set(w_group).at[:, :, E_OFF:E_OFF + N_EXPERTS].set(w_router)
    wr_hi = wr.astype(bf)
    wr_lo = (wr - wr_hi.astype(jnp.float32)).astype(bf)
    br = jnp.zeros((depth, 1, NLOG), jnp.float32)
    br = br.at[:, 0, 0:N_GROUPS].set(b_group).at[:, 0, E_OFF:E_OFF + N_EXPERTS].set(b_router)
    bsb = jnp.broadcast_to(b_s[:, :, :, None], (depth, N_HEADS_C, CHUNK, CHUNK))
    row = lambda p: p[:, None, :]
    mixer_params = (row(norm_mix), w_in.astype(bf), conv_a, w_a_out.astype(bf), conv_b,
                    row(ln_b_g), row(ln_b_b), w_b_out.astype(bf), row(ln_c_g), row(ln_c_b),
                    w_s.astype(bf), bsb, w_c_out.astype(bf), w_o.astype(bf), row(norm_ffn), wr_hi, wr_lo, br)
    pending = None
    for l in range(depth):
        x2, h2, idx, wgt, cnt = _mixer(l, x2, pending, seq, *mixer_params)
        gathered = _moe(l, h2, idx, cnt, w1, w3, w2)
        pending = (gathered, wgt.T)
    out = _combine(gathered, x2, wgt, norm_final[None], True)
    return out.reshape(bsz, seq, d)
```

```python
import functools

import jax
import jax.numpy as jnp
from jax import lax
from jax.experimental import pallas as pl
from jax.experimental.pallas import tpu as pltpu
from jax.experimental.pallas import tpu_sc as plsc

D = 1024
DP = D // 2
NPIECE = DP // 128
N_HEADS_C = 8
CHUNK = 128
CONV_A = 3
CONV_B = 31
N_GROUPS = 4
EPG = 8
N_EXPERTS = N_GROUPS * EPG
D_FF = 512
RMS_EPS = 1e-6
LN_EPS = 1e-5

C_XA, C_BA, C_CA, C_VB, C_GB, C_UV, C_G = 0, 1024, 2048, 3072, 4096, 5120, 7168
D_IN = 10240

TM = 512
HALO = 16
TE = TM + 2 * HALO
CW = 256
RB = 64
NLOG = 128
E_OFF = 8

TR = TM
BM = 512
TD = 512

VMEM_LIMIT = 60 * 1024 * 1024


def _sigmoid(x):
    return 0.5 * (jnp.tanh(0.5 * x) + 1.0)


def _gelu_tanh(x):
    return 0.5 * x * (1.0 + jnp.tanh(0.7978845608028654 * (x + 0.044715 * (x * x * x))))


def _pack_bf16_pairs(x):
    c = x.shape[1] // 2
    as_bits = lambda v: lax.bitcast_convert_type(v.astype(jnp.bfloat16).astype(jnp.float32), jnp.uint32)
    return as_bits(x[:, :c]) | (as_bits(x[:, c:]) >> 16)


def _unpack_bf16_pairs(p):
    hi = lax.bitcast_convert_type(p & jnp.uint32(0xFFFF0000), jnp.float32)
    lo = lax.bitcast_convert_type(p << 16, jnp.float32)
    return hi, lo


def _layer_norm(x, g, b):
    mu = jnp.mean(x, axis=-1, keepdims=True)
    xc = x - mu
    var = jnp.mean(xc * xc, axis=-1, keepdims=True)
    return xc * lax.rsqrt(var + LN_EPS) * g + b


def _moe_residual(x_ref, g_ref, wt_ref):
    w0, w1 = wt_ref[:, 0:1], wt_ref[:, 1:2]
    his, los = [], []
    for j in range(NPIECE):
        hi0, lo0 = _unpack_bf16_pairs(lax.bitcast_convert_type(g_ref[0, j], jnp.uint32))
        hi1, lo1 = _unpack_bf16_pairs(lax.bitcast_convert_type(g_ref[1, j], jnp.uint32))
        his.append(w0 * hi0 + w1 * hi1)
        los.append(w0 * lo0 + w1 * lo1)
    return x_ref[...] + jnp.concatenate(his + los, axis=1)


def _mixer_kernel(seq_tiles, fused_combine, xp_ref, xc_ref, xn_ref, *refs):
    if fused_combine:
        gp_ref, gc_ref, gn_ref, wtp_ref, wtc_ref, wtn_ref = refs[:6]
        refs = refs[6:]
        x_prev = lambda: _moe_residual(xp_ref, gp_ref, wtp_ref)
        x_cur = lambda: _moe_residual(xc_ref, gc_ref, wtc_ref)
        x_next = lambda: _moe_residual(xn_ref, gn_ref, wtn_ref)
    else:
        x_prev, x_cur, x_next = (lambda: xp_ref[...]), (lambda: xc_ref[...]), (lambda: xn_ref[...])
    (nm_ref, win_ref, ca_ref, wa_ref, cb_ref, lnbg_ref, lnbb_ref, wb_ref, lncg_ref, lncb_ref,
     ws_ref, bsb_ref, wc_ref, wo_ref, nf_ref, wrh_ref, wrl_ref, br_ref,
     xo_ref, h2_ref, idx_ref, wgt_ref, cnt_ref,
     hb_ref, zq_ref, cva_ref, cvb_ref, cvn_ref, gbs_ref, vst_ref, mg_ref, carry_ref) = refs
    i = pl.program_id(0)
    at_start = (i % seq_tiles) == 0
    at_end = (i % seq_tiles) == seq_tiles - 1

    def _rms(xv):
        ms = jnp.mean(xv * xv, axis=-1, keepdims=True)
        return (xv * lax.rsqrt(ms + RMS_EPS) * nm_ref[...]).astype(jnp.bfloat16)

    hb_ref[0:HALO, :] = _rms(x_prev())
    hb_ref[HALO:HALO + TM, :] = _rms(x_cur())
    hb_ref[HALO + TM:TE, :] = _rms(x_next())

    rows = lax.broadcasted_iota(jnp.int32, (TE, 1), 0)
    lo = jnp.where(at_start, HALO, 0)
    hi = jnp.where(at_end, HALO + TM, TE)
    valid = jnp.logical_and(rows >= lo, rows < hi)

    def proj(r0, r1, c0, width):
        return jnp.dot(hb_ref[r0:r1, :], win_ref[:, c0:c0 + width],
                       preferred_element_type=jnp.float32)

    def b_proj(c):
        c0 = c * CW
        val = proj(0, TE, C_VB + c0, CW)
        gate = proj(0, TE, C_GB + c0, CW)
        zq_ref[c] = jnp.where(valid, val * _sigmoid(gate), 0.0)

    def a_proj(c):
        c0 = c * CW
        xa = proj(0, TE, C_XA + c0, CW)
        cc = proj(0, TE, C_CA + c0, CW)
        t = jnp.where(valid, xa * cc, 0.0)
        conv = (ca_ref[0:1, c0:c0 + CW] * t[HALO - 1:HALO - 1 + TM]
                + ca_ref[1:2, c0:c0 + CW] * t[HALO:HALO + TM]
                + ca_ref[2:3, c0:c0 + CW] * t[HALO + 1:HALO + 1 + TM])
        ba = proj(HALO, HALO + TM, C_BA + c0, CW)
        cva_ref[:, c0:c0 + CW] = (ba * conv).astype(jnp.bfloat16)

    def c_v_proj(c):
        c0 = c * CW
        vst_ref[:, c0:c0 + CW] = _gelu_tanh(proj(HALO, HALO + TM, C_UV + D + c0, CW))

    def b_conv(c, row_blocks):
        c0 = c * CW
        zq = zq_ref.at[c]
        for rb in row_blocks:
            r0 = rb * RB
            for lt in range(CW // 128):
                l0 = lt * 128
                acc = None
                for s in range(8):
                    part = None
                    for q in range(4):
                        k = 8 * q + s - (HALO - CONV_B // 2)
                        if 0 <= k < CONV_B:
                            term = (cb_ref[k:k + 1, c0 + l0:c0 + l0 + 128]
                                    * zq[r0 + 8 * q:r0 + 8 * q + RB + 8, l0:l0 + 128])
                            part = term if part is None else part + term
                    shifted = part[s:s + RB]
                    acc = shifted if acc is None else acc + shifted
                xo_ref[r0:r0 + RB, c0 + l0:c0 + l0 + 128] = acc

    n_chunks = D // CW
    n_rb = TM // RB
    rb_step = n_rb // n_chunks
    late_rows = lambda c: range(c * rb_step, (c + 1) * rb_step)
    b_proj(0)
    for c in range(n_chunks):
        if c + 1 < n_chunks:
            b_proj(c + 1)
        a_proj(c)
        c_v_proj(c)
    b_conv(0, range(n_rb))

    vn = _layer_norm(vst_ref[...], lncg_ref[...], lncb_ref[...])
    cvn_ref[...] = vn.astype(jnp.bfloat16)
    for c in range(n_chunks):
        c0 = c * CW
        ya = jnp.dot(cva_ref[...], wa_ref[:, c0:c0 + CW], preferred_element_type=jnp.float32)
        ga = _sigmoid(proj(HALO, HALO + TM, C_G + c0, CW))
        mg_ref[:, c0:c0 + CW] = ga * ya
        gbs_ref[:, c0:c0 + CW] = _sigmoid(
            proj(HALO, HALO + TM, C_G + D + c0, CW)).astype(jnp.bfloat16)
        b_conv(1, late_rows(c))

    for c in range(n_chunks):
        c0 = c * CW
        u = _gelu_tanh(proj(HALO, HALO + TM, C_UV + c0, CW))
        for n in range(TM // CHUNK):
            for hh in range(CW // CHUNK):
                h = c * (CW // CHUNK) + hh
                sv = jnp.dot(ws_ref[h], cvn_ref[n * CHUNK:(n + 1) * CHUNK, h * CHUNK:(h + 1) * CHUNK],
                             preferred_element_type=jnp.float32) + bsb_ref[h]
                cva_ref[n * CHUNK:(n + 1) * CHUNK, h * CHUNK:(h + 1) * CHUNK] = (
                    u[n * CHUNK:(n + 1) * CHUNK, hh * CHUNK:(hh + 1) * CHUNK] * sv
                ).astype(jnp.bfloat16)
        b_conv(2, late_rows(c))
    for c in range(n_chunks):
        c0 = c * CW
        yc = jnp.dot(cva_ref[...], wc_ref[:, c0:c0 + CW], preferred_element_type=jnp.float32)
        gc = _sigmoid(proj(HALO, HALO + TM, C_G + 2 * D + c0, CW))
        mg_ref[:, c0:c0 + CW] += gc * yc
        b_conv(3, late_rows(c))

    zc = _layer_norm(xo_ref[...], lnbg_ref[...], lnbb_ref[...])
    cvb_ref[...] = (zc * _sigmoid(zc)).astype(jnp.bfloat16)
    for c in range(n_chunks):
        c0 = c * CW
        yb = jnp.dot(cvb_ref[...], wb_ref[:, c0:c0 + CW], preferred_element_type=jnp.float32)
        mg_ref[:, c0:c0 + CW] += gbs_ref[:, c0:c0 + CW].astype(jnp.float32) * yb

    cvb_ref[...] = mg_ref[...].astype(jnp.bfloat16)
    x_res = x_cur()
    for c in range(n_chunks):
        c0 = c * CW
        xo_ref[:, c0:c0 + CW] = x_res[:, c0:c0 + CW] + jnp.dot(
            cvb_ref[...], wo_ref[:, c0:c0 + CW], preferred_element_type=jnp.float32)
    xnew = xo_ref[...]
    ms = jnp.mean(xnew * xnew, axis=-1, keepdims=True)
    h2 = xnew * lax.rsqrt(ms + RMS_EPS) * nf_ref[...]
    h2_packed = _pack_bf16_pairs(h2)
    for j in range(NPIECE):
        h2_ref[j] = lax.bitcast_convert_type(h2_packed[:, j * 128:(j + 1) * 128], jnp.int32)
    h_hi = h2.astype(jnp.bfloat16)
    h_lo = (h2 - h_hi.astype(jnp.float32)).astype(jnp.bfloat16)
    dot32 = functools.partial(jnp.dot, preferred_element_type=jnp.float32)
    logits = (dot32(h_hi, wrh_ref[...]) + dot32(h_lo, wrh_ref[...])
              + dot32(h_hi, wrl_ref[...]) + dot32(h_lo, wrl_ref[...]) + br_ref[...])
    _route_tile(logits, idx_ref, wgt_ref, cnt_ref, carry_ref)


def _layer_spec(layer, shape):
    nd = len(shape)
    return pl.BlockSpec((None,) + tuple(shape), lambda i, _n=nd: (layer,) + (0,) * _n,
                        pipeline_mode=pl.Buffered(1))


def _mixer(layer, x2, pending, seq_len, nm, win, ca, wa, cb, lnbg, lnbb, wb, lncg, lncb, ws, bsb, wc, wo,
           nf, wrh, wrl, br):
    T = x2.shape[0]
    n_tiles = T // TM
    hb = TM // HALO
    last_halo = T // HALO - 1
    ls = functools.partial(_layer_spec, layer)
    prev_rows = lambda i: jnp.maximum(i * hb - 1, 0)
    next_rows = lambda i: jnp.minimum((i + 1) * hb, last_halo)
    in_specs = [
        pl.BlockSpec((HALO, D), lambda i: (prev_rows(i), 0)),
        pl.BlockSpec((TM, D), lambda i: (i, 0)),
        pl.BlockSpec((HALO, D), lambda i: (next_rows(i), 0)),
    ]
    operands = [x2, x2, x2]
    if pending is not None:
        gathered, wgt_rows = pending
        in_specs += [
            pl.BlockSpec((2, NPIECE, HALO, 128), lambda i: (0, 0, prev_rows(i), 0)),
            pl.BlockSpec((2, NPIECE, TM, 128), lambda i: (0, 0, i, 0)),
            pl.BlockSpec((2, NPIECE, HALO, 128), lambda i: (0, 0, next_rows(i), 0)),
            pl.BlockSpec((HALO, 8), lambda i: (prev_rows(i), 0)),
            pl.BlockSpec((TM, 8), lambda i: (i, 0)),
            pl.BlockSpec((HALO, 8), lambda i: (next_rows(i), 0)),
        ]
        operands += [gathered, gathered, gathered, wgt_rows, wgt_rows, wgt_rows]
    in_specs += [
        ls((1, D)), ls((D, D_IN)), ls((CONV_A, D)), ls((D, D)),
        ls((CONV_B, D)), ls((1, D)), ls((1, D)), ls((D, D)),
        ls((1, D)), ls((1, D)), ls((N_HEADS_C, CHUNK, CHUNK)),
        ls((N_HEADS_C, CHUNK, CHUNK)), ls((D, D)), ls((D, D)),
        ls((1, D)), ls((D, NLOG)), ls((D, NLOG)), ls((1, NLOG)),
    ]
    out_specs = [
        pl.BlockSpec((TM, D), lambda i: (i, 0)),
        pl.BlockSpec((NPIECE, TM, 128), lambda i: (0, i, 0)),
        pl.BlockSpec((8, TM), lambda i: (0, i)),
        pl.BlockSpec((8, TM), lambda i: (0, i)),
        pl.BlockSpec((N_EXPERTS, 128), lambda i: (0, 0)),
    ]
    return pl.pallas_call(
        functools.partial(_mixer_kernel, seq_len // TM, pending is not None),
        grid=(n_tiles,),
        in_specs=in_specs,
        out_specs=out_specs,
        out_shape=[jax.ShapeDtypeStruct((T, D), jnp.float32),
                   jax.ShapeDtypeStruct((NPIECE, T, 128), jnp.int32),
                   jax.ShapeDtypeStruct((8, T), jnp.int32),
                   jax.ShapeDtypeStruct((8, T), jnp.float32),
                   jax.ShapeDtypeStruct((N_EXPERTS, 128), jnp.float32)],
        scratch_shapes=[pltpu.VMEM((TE, D), jnp.bfloat16),
                        pltpu.VMEM((D // CW, TE, CW), jnp.float32),
                        pltpu.VMEM((TM, D), jnp.bfloat16),
                        pltpu.VMEM((TM, D), jnp.bfloat16),
                        pltpu.VMEM((TM, D), jnp.bfloat16),
                        pltpu.VMEM((TM, D), jnp.bfloat16),
                        pltpu.VMEM((TM, D), jnp.float32),
                        pltpu.VMEM((TM, D), jnp.float32),
                        pltpu.VMEM((N_EXPERTS, 128), jnp.float32)],
        compiler_params=pltpu.CompilerParams(dimension_semantics=("arbitrary",),
                                             vmem_limit_bytes=VMEM_LIMIT),
        name="mixer",
    )(*operands, nm, win, ca, wa, cb, lnbg, lnbb, wb, lncg, lncb, ws, bsb, wc, wo, nf, wrh, wrl, br)


def _route_tile(lg, idx_ref, wgt_ref, cnt_ref, carry_ref):
    i = pl.program_id(0)

    @pl.when(i == 0)
    def _():
        carry_ref[...] = jnp.zeros_like(carry_ref)

    lt = lg.T
    g = [lt[j:j + 1, :] for j in range(N_GROUPS)]
    gmax = jnp.maximum(jnp.maximum(g[0], g[1]), jnp.maximum(g[2], g[3]))
    gidx = jnp.where(g[0] == gmax, 0.0, jnp.where(g[1] == gmax, 1.0, jnp.where(g[2] == gmax, 2.0, 3.0)))
    gsum = sum(jnp.exp(gj - gmax) for gj in g)
    g_p = 1.0 / gsum

    sel = lt[E_OFF + 3 * EPG:E_OFF + 4 * EPG, :]
    for j in (2, 1, 0):
        sel = jnp.where(gidx == float(j), lt[E_OFF + j * EPG:E_OFF + (j + 1) * EPG, :], sel)
    rid = lax.broadcasted_iota(jnp.int32, (EPG, TR), 0).astype(jnp.float32)
    m1 = jnp.max(sel, axis=0, keepdims=True)
    i1 = jnp.min(jnp.where(sel == m1, rid, float(EPG)), axis=0, keepdims=True)
    rest = jnp.where(rid == i1, -jnp.inf, sel)
    m2 = jnp.max(rest, axis=0, keepdims=True)
    i2 = jnp.min(jnp.where(rest == m2, rid, float(EPG)), axis=0, keepdims=True)
    e2x = jnp.exp(m2 - m1)
    den = 1.0 + e2x
    w1 = (1.0 / den) * g_p
    w2 = (e2x / den) * g_p
    e1 = gidx * float(EPG) + i1
    e2 = gidx * float(EPG) + i2

    eid = lax.broadcasted_iota(jnp.int32, (N_EXPERTS, TR), 0).astype(jnp.float32)
    oh1 = (eid == e1).astype(jnp.float32)
    oh2 = (eid == e2).astype(jnp.float32)
    oh = oh1 + oh2
    tr = lax.broadcasted_iota(jnp.int32, (TR, TR), 0)
    tc = lax.broadcasted_iota(jnp.int32, (TR, TR), 1)
    upper = (tr < tc).astype(jnp.bfloat16)
    before = jnp.dot(oh.astype(jnp.bfloat16), upper, preferred_element_type=jnp.float32)
    base = before + carry_ref[:, 0:1]
    r1 = jnp.sum(oh1 * base, axis=0, keepdims=True)
    r2 = jnp.sum(oh2 * base, axis=0, keepdims=True)
    carry_ref[...] = carry_ref[...] + jnp.sum(oh, axis=1, keepdims=True)

    idx_ref[...] = jnp.zeros_like(idx_ref)
    idx_ref[0:1, :] = e1.astype(jnp.int32)
    idx_ref[1:2, :] = e2.astype(jnp.int32)
    idx_ref[2:3, :] = r1.astype(jnp.int32)
    idx_ref[3:4, :] = r2.astype(jnp.int32)
    wgt_ref[...] = jnp.zeros_like(wgt_ref)
    wgt_ref[0:1, :] = w1
    wgt_ref[1:2, :] = w2
    cnt_ref[...] = carry_ref[...]


SC_WINDOW = 128


def _sc_mesh():
    return plsc.VectorSubcoreMesh(core_axis_name="c", subcore_axis_name="s")


def _sc_scatter_rows(rows, idx, passes, n_out):
    n_src = rows.shape[0]
    src_blocks = n_src // SC_WINDOW

    @functools.partial(
        pl.kernel, mesh=_sc_mesh(),
        out_type=jax.ShapeDtypeStruct((n_out, 128), rows.dtype))
    def scatter(rows_hbm, idx_hbm, out_hbm):
        def body(rows_vmem, idx_vmem):
            pltpu.sync_copy(rows_vmem, out_hbm.at[idx_vmem.at[0]])

        pltpu.emit_pipeline(
            body,
            grid=(passes * src_blocks,),
            in_specs=[pl.BlockSpec((SC_WINDOW, 128), lambda i: (i % src_blocks, 0)),
                      pl.BlockSpec((1, SC_WINDOW), lambda i: (0, i))],
            out_specs=[],
            core_axis_name=("c", "s"),
            dimension_semantics=(pltpu.PARALLEL,),
        )(rows_hbm, idx_hbm)

    return scatter(rows, idx.reshape(1, passes * n_src))


def _expert_kernel(layer, be_ref, nu_ref, nxt_ref, gs_ref, nv_ref, xb_ref, w1_ref, w3_ref, w2_ref, yb_ref,
                   wf1_ref, wf3_ref, wf2_ref, w1b_ref, w3b_ref, w2b_ref, sem):
    b = pl.program_id(0)
    used = b < nu_ref[0]
    new_expert = jnp.logical_or(b == 0, be_ref[b] != be_ref[jnp.maximum(b - 1, 0)])

    def weight_copies(e, s):
        return [pltpu.make_async_copy(src.at[layer, e], dst.at[s], sem.at[s])
                for src, dst in ((w1_ref, wf1_ref), (w3_ref, wf3_ref), (w2_ref, wf2_ref))]

    @pl.when(b == 0)
    def _():
        for cp in weight_copies(be_ref[0], 0):
            cp.start(priority=1)

    @pl.when(jnp.logical_and(used, new_expert))
    def _():
        s = gs_ref[b]
        for cp in weight_copies(be_ref[b], s):
            cp.wait()

        @pl.when(nxt_ref[b] >= 0)
        def _():
            for cp in weight_copies(nxt_ref[b], 1 - s):
                cp.start(priority=1)

        w1b_ref[...] = wf1_ref[s].astype(jnp.bfloat16)
        w3b_ref[...] = wf3_ref[s].astype(jnp.bfloat16)
        w2b_ref[...] = wf2_ref[s].astype(jnp.bfloat16)

    def expert_mlp(rows):
        live = lax.broadcasted_iota(jnp.int32, (rows, 1), 0) < nv_ref[b]
        packed_in = jnp.concatenate(
            [jnp.where(live, lax.bitcast_convert_type(xb_ref[j, 0:rows], jnp.uint32), jnp.uint32(0))
             for j in range(NPIECE)], axis=1)
        x = jnp.concatenate(_unpack_bf16_pairs(packed_in), axis=1).astype(jnp.bfloat16)
        dot32 = functools.partial(jnp.dot, preferred_element_type=jnp.float32)
        a = dot32(x, w1b_ref[...])
        g = dot32(x, w3b_ref[...])
        hmid = (a * _sigmoid(a) * g).astype(jnp.bfloat16)
        packed = _pack_bf16_pairs(dot32(hmid, w2b_ref[...]))
        for j in range(NPIECE):
            yb_ref[j, 0:rows] = lax.bitcast_convert_type(packed[:, j * 128:(j + 1) * 128], jnp.int32)

    half = BM // 2
    half_full = nv_ref[b] <= half

    @pl.when(jnp.logical_and(used, jnp.logical_not(half_full)))
    def _():
        expert_mlp(BM)

    @pl.when(jnp.logical_and(used, half_full))
    def _():
        expert_mlp(half)
        yb_ref[:, half:BM, :] = jnp.zeros((NPIECE, BM - half, 128), jnp.int32)

    @pl.when(jnp.logical_not(used))
    def _():
        yb_ref[...] = jnp.zeros_like(yb_ref)


def _experts(layer, block_expert, n_used, next_expert, group_slot, block_valid, xb, w1, w3, w2):
    n_slots = xb.shape[1]
    n_blocks = n_slots // BM

    def row_map(b, be, nu, nxt, gs, nv):
        return (0, jnp.minimum(b, nu[0] - 1), 0)

    def out_map(b, be, nu, nxt, gs, nv):
        return (0, b, 0)

    grid_spec = pltpu.PrefetchScalarGridSpec(
        num_scalar_prefetch=5,
        grid=(n_blocks,),
        in_specs=[pl.BlockSpec((NPIECE, BM, 128), row_map),
                  pl.BlockSpec(memory_space=pl.ANY),
                  pl.BlockSpec(memory_space=pl.ANY),
                  pl.BlockSpec(memory_space=pl.ANY)],
        out_specs=pl.BlockSpec((NPIECE, BM, 128), out_map),
        scratch_shapes=[pltpu.VMEM((2, D, D_FF), jnp.float32),
                        pltpu.VMEM((2, D, D_FF), jnp.float32),
                        pltpu.VMEM((2, D_FF, D), jnp.float32),
                        pltpu.VMEM((D, D_FF), jnp.bfloat16),
                        pltpu.VMEM((D, D_FF), jnp.bfloat16),
                        pltpu.VMEM((D_FF, D), jnp.bfloat16),
                        pltpu.SemaphoreType.DMA((2,))],
    )
    return pl.pallas_call(
        functools.partial(_expert_kernel, layer),
        grid_spec=grid_spec,
        out_shape=jax.ShapeDtypeStruct((NPIECE, n_slots, 128), jnp.int32),
        compiler_params=pltpu.CompilerParams(dimension_semantics=("arbitrary",),
                                             vmem_limit_bytes=VMEM_LIMIT),
        name="experts",
    )(block_expert, n_used, next_expert, group_slot, block_valid, xb, w1, w3, w2)


def _sc_gather_rows(table, idx):
    n_rows = idx.shape[0]

    @functools.partial(
        pl.kernel, mesh=_sc_mesh(),
        out_type=jax.ShapeDtypeStruct((n_rows, 128), table.dtype))
    def gather(table_hbm, idx_hbm, out_hbm):
        def body(idx_vmem, out_vmem):
            pltpu.sync_copy(table_hbm.at[idx_vmem.at[0]], out_vmem)

        pltpu.emit_pipeline(
            body,
            grid=(n_rows // SC_WINDOW,),
            in_specs=[pl.BlockSpec((1, SC_WINDOW), lambda i: (0, i))],
            out_specs=[pl.BlockSpec((SC_WINDOW, 128), lambda i: (i, 0))],
            core_axis_name=("c", "s"),
            dimension_semantics=(pltpu.PARALLEL,),
        )(idx_hbm, out_hbm)

    return gather(table, idx.reshape(1, n_rows))


def _combine_kernel(final, g_ref, x_ref, w_ref, nrm_ref, o_ref):
    w = w_ref[...].T
    ys = {}
    for j in range(NPIECE):
        hi0, lo0 = _unpack_bf16_pairs(lax.bitcast_convert_type(g_ref[0, j], jnp.uint32))
        hi1, lo1 = _unpack_bf16_pairs(lax.bitcast_convert_type(g_ref[1, j], jnp.uint32))
        c_hi, c_lo = j * 128, DP + j * 128
        ys[c_hi] = x_ref[:, c_hi:c_hi + 128] + w[:, 0:1] * hi0 + w[:, 1:2] * hi1
        ys[c_lo] = x_ref[:, c_lo:c_lo + 128] + w[:, 0:1] * lo0 + w[:, 1:2] * lo1
    if final:
        ms = sum(jnp.sum(y * y, axis=-1, keepdims=True) for y in ys.values()) * (1.0 / D)
        scale = lax.rsqrt(ms + RMS_EPS)
        ys = {c: y * scale * nrm_ref[:, c:c + 128] for c, y in ys.items()}
    for c, y in ys.items():
        o_ref[:, c:c + 128] = y


def _combine(gathered, x2, wgt, nrm, final):
    T = x2.shape[0]
    return pl.pallas_call(
        functools.partial(_combine_kernel, final),
        grid=(T // TD,),
        in_specs=[pl.BlockSpec((2, NPIECE, TD, 128), lambda i: (0, 0, i, 0)),
                  pl.BlockSpec((TD, D), lambda i: (i, 0)),
                  pl.BlockSpec((8, TD), lambda i: (0, i)),
                  pl.BlockSpec((1, D), lambda i: (0, 0))],
        out_specs=pl.BlockSpec((TD, D), lambda i: (i, 0)),
        out_shape=jax.ShapeDtypeStruct((T, D), jnp.float32),
        compiler_params=pltpu.CompilerParams(dimension_semantics=("arbitrary",)),
        name="combine",
    )(gathered, x2, wgt, nrm)


def _moe(layer, h2, idx, cnt, w1, w3, w2):
    T = idx.shape[1]
    counts = cnt[:, 0].astype(jnp.int32)
    padded = (counts + BM - 1) // BM * BM
    padded_end = jnp.cumsum(padded)
    padded_start = padded_end - padded
    eids = jnp.arange(N_EXPERTS, dtype=jnp.int32)[:, None]
    start_of = lambda e: jnp.sum(jnp.where(e[None, :] == eids, padded_start[:, None], 0), axis=0)
    dest = jnp.stack([start_of(idx[0]) + idx[2], start_of(idx[1]) + idx[3]])
    n_blocks = (2 * T) // BM + N_EXPERTS
    block_start = jnp.arange(n_blocks, dtype=jnp.int32) * BM
    block_expert = jnp.minimum(
        jnp.sum((padded_end[None, :] <= block_start[:, None]).astype(jnp.int32), axis=1),
        N_EXPERTS - 1)
    n_used = (padded_end[-1] // BM).reshape(1)
    later = jnp.logical_and(eids.T > eids, (counts > 0)[None, :])
    next_of = jnp.min(jnp.where(later, eids.T, N_EXPERTS), axis=1)
    next_of = jnp.where(next_of < N_EXPERTS, next_of, -1)
    pick = lambda table: jnp.sum(
        jnp.where(block_expert[:, None] == eids.T, table[None, :], 0), axis=1)
    next_expert = pick(next_of)
    group_slot = pick(jnp.cumsum((counts > 0).astype(jnp.int32)) - 1) % 2
    block_valid = jnp.clip(pick(padded_start + counts) - block_start, 0, BM)
    n_slots = n_blocks * BM
    piece_base = (jnp.arange(NPIECE, dtype=jnp.int32) * n_slots)[None, :, None]
    rows = (dest[:, None, :] + piece_base).reshape(-1)
    xb = _sc_scatter_rows(h2.reshape(NPIECE * T, 128), rows, 2, NPIECE * n_slots)
    yb = _experts(layer, block_expert, n_used, next_expert, group_slot, block_valid,
                  xb.reshape(NPIECE, n_slots, 128), w1, w3, w2)
    gathered = _sc_gather_rows(yb.reshape(NPIECE * n_slots, 128), rows)
    return gathered.reshape(2, NPIECE, T, 128)


def kernel(x, norm_mix, w_in, conv_a, w_a_out, conv_b, ln_b_g, ln_b_b, w_b_out, ln_c_g, ln_c_b,
           w_s, b_s, w_c_out, w_o, norm_ffn, w_group, b_group, w_router, b_router, w1, w3, w2,
           norm_final):
    bsz, seq, d = x.shape
    depth = norm_mix.shape[0]
    bf = jnp.bfloat16
    x2 = x.reshape(bsz * seq, d)
    wr = jnp.zeros((depth, d, NLOG), jnp.float32)
    wr = wr.at[:, :, 0:N_GROUPS].set(w_group).at[:, :, E_OFF:E_OFF + N_EXPERTS].set(w_router)
    wr_hi = wr.astype(bf)
    wr_lo = (wr - wr_hi.astype(jnp.float32)).astype(bf)
    br = jnp.zeros((depth, 1, NLOG), jnp.float32)
    br = br.at[:, 0, 0:N_GROUPS].set(b_group).at[:, 0, E_OFF:E_OFF + N_EXPERTS].set(b_router)
    bsb = jnp.broadcast_to(b_s[:, :, :, None], (depth, N_HEADS_C, CHUNK, CHUNK))
    row = lambda p: p[:, None, :]
    mixer_params = (row(norm_mix), w_in.astype(bf), conv_a, w_a_out.astype(bf), conv_b,
                    row(ln_b_g), row(ln_b_b), w_b_out.astype(bf), row(ln_c_g), row(ln_c_b),
                    w_s.astype(bf), bsb, w_c_out.astype(bf), w_o.astype(bf), row(norm_ffn), wr_hi, wr_lo, br)
    pending = None
    for l in range(depth):
        x2, h2, idx, wgt, cnt = _mixer(l, x2, pending, seq, *mixer_params)
        gathered = _moe(l, h2, idx, cnt, w1, w3, w2)
        pending = (gathered, wgt.T)
    out = _combine(gathered, x2, wgt, norm_final[None], True)
    return out.reshape(bsz, seq, d)
```

```python
import functools

import jax
import jax.numpy as jnp
from jax import lax
from jax.experimental import pallas as pl
from jax.experimental.pallas import tpu as pltpu
from jax.experimental.pallas import tpu_sc as plsc

D = 1024
DP = D // 2
NPIECE = DP // 128
N_HEADS_C = 8
CHUNK = 128
CONV_A = 3
CONV_B = 31
N_GROUPS = 4
EPG = 8
N_EXPERTS = N_GROUPS * EPG
D_FF = 512
RMS_EPS = 1e-6
LN_EPS = 1e-5

C_XA, C_BA, C_CA, C_VB, C_GB, C_UV, C_G = 0, 1024, 2048, 3072, 4096, 5120, 7168
D_IN = 10240

TM = 512
HALO = 16
TE = TM + 2 * HALO
CW = 256
RB = 64
PR = RB + 16
NLOG = 128
E_OFF = 8

TR = TM
BM = 512
TD = 512

VMEM_LIMIT = 60 * 1024 * 1024


def _sigmoid(x):
    return 0.5 * (jnp.tanh(0.5 * x) + 1.0)


def _gelu_tanh(x):
    return 0.5 * x * (1.0 + jnp.tanh(0.7978845608028654 * (x + 0.044715 * (x * x * x))))


def _pack_bf16_pairs(x):
    c = x.shape[1] // 2
    as_bits = lambda v: lax.bitcast_convert_type(v.astype(jnp.bfloat16).astype(jnp.float32), jnp.uint32)
    return as_bits(x[:, :c]) | (as_bits(x[:, c:]) >> 16)


def _unpack_bf16_pairs(p):
    hi = lax.bitcast_convert_type(p & jnp.uint32(0xFFFF0000), jnp.float32)
    lo = lax.bitcast_convert_type(p << 16, jnp.float32)
    return hi, lo


def _layer_norm(x, g, b):
    mu = jnp.mean(x, axis=-1, keepdims=True)
    xc = x - mu
    var = jnp.mean(xc * xc, axis=-1, keepdims=True)
    return xc * lax.rsqrt(var + LN_EPS) * g + b


def _moe_residual(x_ref, g_ref, wt_ref):
    w0, w1 = wt_ref[:, 0:1], wt_ref[:, 1:2]
    his, los = [], []
    for j in range(NPIECE):
        hi0, lo0 = _unpack_bf16_pairs(lax.bitcast_convert_type(g_ref[0, j], jnp.uint32))
        hi1, lo1 = _unpack_bf16_pairs(lax.bitcast_convert_type(g_ref[1, j], jnp.uint32))
        his.append(w0 * hi0 + w1 * hi1)
        los.append(w0 * lo0 + w1 * lo1)
    return x_ref[...] + jnp.concatenate(his + los, axis=1)


def _mixer_kernel(seq_tiles, fused_combine, xp_ref, xc_ref, xn_ref, *refs):
    if fused_combine:
        gp_ref, gc_ref, gn_ref, wtp_ref, wtc_ref, wtn_ref = refs[:6]
        refs = refs[6:]
        x_prev = lambda: _moe_residual(xp_ref, gp_ref, wtp_ref)
        x_cur = lambda: _moe_residual(xc_ref, gc_ref, wtc_ref)
        x_next = lambda: _moe_residual(xn_ref, gn_ref, wtn_ref)
    else:
        x_prev, x_cur, x_next = (lambda: xp_ref[...]), (lambda: xc_ref[...]), (lambda: xn_ref[...])
    (nm_ref, win_ref, ca_ref, wa_ref, cb_ref, lnbg_ref, lnbb_ref, wb_ref, lncg_ref, lncb_ref,
     ws_ref, bsb_ref, wc_ref, wo_ref, nf_ref, wrh_ref, wrl_ref, br_ref, sh_ref,
     xo_ref, h2_ref, idx_ref, wgt_ref, cnt_ref,
     hb_ref, zq_ref, cva_ref, cvb_ref, cvn_ref, vst_ref, mg_ref, carry_ref) = refs
    i = pl.program_id(0)
    at_start = (i % seq_tiles) == 0
    at_end = (i % seq_tiles) == seq_tiles - 1

    def _rms(xv):
        ms = jnp.mean(xv * xv, axis=-1, keepdims=True)
        return (xv * lax.rsqrt(ms + RMS_EPS) * nm_ref[...]).astype(jnp.bfloat16)

    hb_ref[0:HALO, :] = _rms(x_prev())
    hb_ref[HALO:HALO + TM, :] = _rms(x_cur())
    hb_ref[HALO + TM:TE, :] = _rms(x_next())

    rows = lax.broadcasted_iota(jnp.int32, (TE, 1), 0)
    lo = jnp.where(at_start, HALO, 0)
    hi = jnp.where(at_end, HALO + TM, TE)
    valid = jnp.logical_and(rows >= lo, rows < hi)

    def proj(r0, r1, c0, width):
        return jnp.dot(hb_ref[r0:r1, :], win_ref[:, c0:c0 + width],
                       preferred_element_type=jnp.float32)

    def b_proj(c):
        c0 = c * CW
        val = proj(0, TE, C_VB + c0, CW)
        gate = proj(0, TE, C_GB + c0, CW)
        zq_ref[c, 0:TE] = jnp.where(valid, val * _sigmoid(gate), 0.0)
        zq_ref[c, TE:TE + PR - RB - 8] = jnp.zeros((PR - RB - 8, CW), jnp.float32)

    def a_proj(c):
        c0 = c * CW
        xa = proj(0, TE, C_XA + c0, CW)
        cc = proj(0, TE, C_CA + c0, CW)
        t = jnp.where(valid, xa * cc, 0.0)
        conv = (ca_ref[0:1, c0:c0 + CW] * t[HALO - 1:HALO - 1 + TM]
                + ca_ref[1:2, c0:c0 + CW] * t[HALO:HALO + TM]
                + ca_ref[2:3, c0:c0 + CW] * t[HALO + 1:HALO + 1 + TM])
        ba = proj(HALO, HALO + TM, C_BA + c0, CW)
        cva_ref[:, c0:c0 + CW] = (ba * conv).astype(jnp.bfloat16)

    def c_v_proj(c):
        c0 = c * CW
        vst_ref[:, c0:c0 + CW] = _gelu_tanh(proj(HALO, HALO + TM, C_UV + D + c0, CW))

    def b_conv(c, row_blocks):
        c0 = c * CW
        zq = zq_ref.at[c]
        for rb in row_blocks:
            r0 = rb * RB
            for lt in range(CW // 128):
                l0 = lt * 128
                parts = []
                for s in range(8):
                    part = None
                    for q in range(4):
                        k = 8 * q + s - (HALO - CONV_B // 2)
                        if 0 <= k < CONV_B:
                            term = (cb_ref[k:k + 1, c0 + l0:c0 + l0 + 128]
                                    * zq[r0 + 8 * q:r0 + 8 * q + PR, l0:l0 + 128])
                            part = term if part is None else part + term
                    parts.append(part.astype(jnp.bfloat16))
                xo_ref[r0:r0 + RB, c0 + l0:c0 + l0 + 128] = jnp.dot(
                    sh_ref[...], jnp.concatenate(parts, axis=0), preferred_element_type=jnp.float32)

    n_chunks = D // CW
    n_rb = TM // RB
    rb_step = n_rb // n_chunks
    late_rows = lambda c: range(c * rb_step, (c + 1) * rb_step)
    b_proj(0)
    for c in range(n_chunks):
        if c + 1 < n_chunks:
            b_proj(c + 1)
        a_proj(c)
        c_v_proj(c)
    b_conv(0, range(n_rb))

    vn = _layer_norm(vst_ref[...], lncg_ref[...], lncb_ref[...])
    cvn_ref[...] = vn.astype(jnp.bfloat16)
    for c in range(n_chunks):
        c0 = c * CW
        ya = jnp.dot(cva_ref[...], wa_ref[:, c0:c0 + CW], preferred_element_type=jnp.float32)
        ga = _sigmoid(proj(HALO, HALO + TM, C_G + c0, CW))
        mg_ref[:, c0:c0 + CW] = ga * ya
        b_conv(1, late_rows(c))

    for c in range(n_chunks):
        c0 = c * CW
        u = _gelu_tanh(proj(HALO, HALO + TM, C_UV + c0, CW))
        for n in range(TM // CHUNK):
            for hh in range(CW // CHUNK):
                h = c * (CW // CHUNK) + hh
                sv = jnp.dot(ws_ref[h], cvn_ref[n * CHUNK:(n + 1) * CHUNK, h * CHUNK:(h + 1) * CHUNK],
                             preferred_element_type=jnp.float32) + bsb_ref[h]
                cva_ref[n * CHUNK:(n + 1) * CHUNK, h * CHUNK:(h + 1) * CHUNK] = (
                    u[n * CHUNK:(n + 1) * CHUNK, hh * CHUNK:(hh + 1) * CHUNK] * sv
                ).astype(jnp.bfloat16)
        b_conv(2, late_rows(c))
    for c in range(n_chunks):
        c0 = c * CW
        yc = jnp.dot(cva_ref[...], wc_ref[:, c0:c0 + CW], preferred_element_type=jnp.float32)
        gc = _sigmoid(proj(HALO, HALO + TM, C_G + 2 * D + c0, CW))
        mg_ref[:, c0:c0 + CW] += gc * yc
        b_conv(3, late_rows(c))

    zc = _layer_norm(xo_ref[...], lnbg_ref[...], lnbb_ref[...])
    cvb_ref[...] = (zc * _sigmoid(zc)).astype(jnp.bfloat16)
    for c in range(n_chunks):
        c0 = c * CW
        yb = jnp.dot(cvb_ref[...], wb_ref[:, c0:c0 + CW], preferred_element_type=jnp.float32)
        gb = _sigmoid(proj(HALO, HALO + TM, C_G + D + c0, CW))
        mg_ref[:, c0:c0 + CW] += gb * yb

    cvb_ref[...] = mg_ref[...].astype(jnp.bfloat16)
    x_res = x_cur()
    for c in range(n_chunks):
        c0 = c * CW
        xo_ref[:, c0:c0 + CW] = x_res[:, c0:c0 + CW] + jnp.dot(
            cvb_ref[...], wo_ref[:, c0:c0 + CW], preferred_element_type=jnp.float32)
    xnew = xo_ref[...]
    ms = jnp.mean(xnew * xnew, axis=-1, keepdims=True)
    h2 = xnew * lax.rsqrt(ms + RMS_EPS) * nf_ref[...]
    h2_packed = _pack_bf16_pairs(h2)
    for j in range(NPIECE):
        h2_ref[j] = lax.bitcast_convert_type(h2_packed[:, j * 128:(j + 1) * 128], jnp.int32)
    h_hi = h2.astype(jnp.bfloat16)
    h_lo = (h2 - h_hi.astype(jnp.float32)).astype(jnp.bfloat16)
    dot32 = functools.partial(jnp.dot, preferred_element_type=jnp.float32)
    logits = (dot32(h_hi, wrh_ref[...]) + dot32(h_lo, wrh_ref[...])
              + dot32(h_hi, wrl_ref[...]) + dot32(h_lo, wrl_ref[...]) + br_ref[...])
    _route_tile(logits, idx_ref, wgt_ref, cnt_ref, carry_ref)


def _layer_spec(layer, shape):
    nd = len(shape)
    return pl.BlockSpec((None,) + tuple(shape), lambda i, _n=nd: (layer,) + (0,) * _n,
                        pipeline_mode=pl.Buffered(1))


def _mixer(layer, x2, pending, seq_len, nm, win, ca, wa, cb, lnbg, lnbb, wb, lncg, lncb, ws, bsb, wc, wo,
           nf, wrh, wrl, br, sh):
    T = x2.shape[0]
    n_tiles = T // TM
    hb = TM // HALO
    last_halo = T // HALO - 1
    ls = functools.partial(_layer_spec, layer)
    prev_rows = lambda i: jnp.maximum(i * hb - 1, 0)
    next_rows = lambda i: jnp.minimum((i + 1) * hb, last_halo)
    in_specs = [
        pl.BlockSpec((HALO, D), lambda i: (prev_rows(i), 0)),
        pl.BlockSpec((TM, D), lambda i: (i, 0)),
        pl.BlockSpec((HALO, D), lambda i: (next_rows(i), 0)),
    ]
    operands = [x2, x2, x2]
    if pending is not None:
        gathered, wgt_rows = pending
        in_specs += [
            pl.BlockSpec((2, NPIECE, HALO, 128), lambda i: (0, 0, prev_rows(i), 0)),
            pl.BlockSpec((2, NPIECE, TM, 128), lambda i: (0, 0, i, 0)),
            pl.BlockSpec((2, NPIECE, HALO, 128), lambda i: (0, 0, next_rows(i), 0)),
            pl.BlockSpec((HALO, 8), lambda i: (prev_rows(i), 0)),
            pl.BlockSpec((TM, 8), lambda i: (i, 0)),
            pl.BlockSpec((HALO, 8), lambda i: (next_rows(i), 0)),
        ]
        operands += [gathered, gathered, gathered, wgt_rows, wgt_rows, wgt_rows]
    in_specs += [
        ls((1, D)), ls((D, D_IN)), ls((CONV_A, D)), ls((D, D)),
        ls((CONV_B, D)), ls((1, D)), ls((1, D)), ls((D, D)),
        ls((1, D)), ls((1, D)), ls((N_HEADS_C, CHUNK, CHUNK)),
        ls((N_HEADS_C, CHUNK, CHUNK)), ls((D, D)), ls((D, D)),
        ls((1, D)), ls((D, NLOG)), ls((D, NLOG)), ls((1, NLOG)),
        pl.BlockSpec((RB, 8 * PR), lambda i: (0, 0), pipeline_mode=pl.Buffered(1)),
    ]
    out_specs = [
        pl.BlockSpec((TM, D), lambda i: (i, 0)),
        pl.BlockSpec((NPIECE, TM, 128), lambda i: (0, i, 0)),
        pl.BlockSpec((8, TM), lambda i: (0, i)),
        pl.BlockSpec((8, TM), lambda i: (0, i)),
        pl.BlockSpec((N_EXPERTS, 128), lambda i: (0, 0)),
    ]
    return pl.pallas_call(
        functools.partial(_mixer_kernel, seq_len // TM, pending is not None),
        grid=(n_tiles,),
        in_specs=in_specs,
        out_specs=out_specs,
        out_shape=[jax.ShapeDtypeStruct((T, D), jnp.float32),
                   jax.ShapeDtypeStruct((NPIECE, T, 128), jnp.int32),
                   jax.ShapeDtypeStruct((8, T), jnp.int32),
                   jax.ShapeDtypeStruct((8, T), jnp.float32),
                   jax.ShapeDtypeStruct((N_EXPERTS, 128), jnp.float32)],
        scratch_shapes=[pltpu.VMEM((TE, D), jnp.bfloat16),
                        pltpu.VMEM((D // CW, TE + PR - RB - 8, CW), jnp.float32),
                        pltpu.VMEM((TM, D), jnp.bfloat16),
                        pltpu.VMEM((TM, D), jnp.bfloat16),
                        pltpu.VMEM((TM, D), jnp.bfloat16),
                        pltpu.VMEM((TM, D), jnp.float32),
                        pltpu.VMEM((TM, D), jnp.float32),
                        pltpu.VMEM((N_EXPERTS, 128), jnp.float32)],
        compiler_params=pltpu.CompilerParams(dimension_semantics=("arbitrary",),
                                             vmem_limit_bytes=VMEM_LIMIT),
        name="mixer",
    )(*operands, nm, win, ca, wa, cb, lnbg, lnbb, wb, lncg, lncb, ws, bsb, wc, wo, nf, wrh, wrl, br, sh)


def _route_tile(lg, idx_ref, wgt_ref, cnt_ref, carry_ref):
    i = pl.program_id(0)

    @pl.when(i == 0)
    def _():
        carry_ref[...] = jnp.zeros_like(carry_ref)

    lt = lg.T
    g = [lt[j:j + 1, :] for j in range(N_GROUPS)]
    gmax = jnp.maximum(jnp.maximum(g[0], g[1]), jnp.maximum(g[2], g[3]))
    gidx = jnp.where(g[0] == gmax, 0.0, jnp.where(g[1] == gmax, 1.0, jnp.where(g[2] == gmax, 2.0, 3.0)))
    gsum = sum(jnp.exp(gj - gmax) for gj in g)
    g_p = 1.0 / gsum

    sel = lt[E_OFF + 3 * EPG:E_OFF + 4 * EPG, :]
    for j in (2, 1, 0):
        sel = jnp.where(gidx == float(j), lt[E_OFF + j * EPG:E_OFF + (j + 1) * EPG, :], sel)
    rid = lax.broadcasted_iota(jnp.int32, (EPG, TR), 0).astype(jnp.float32)
    m1 = jnp.max(sel, axis=0, keepdims=True)
    i1 = jnp.min(jnp.where(sel == m1, rid, float(EPG)), axis=0, keepdims=True)
    rest = jnp.where(rid == i1, -jnp.inf, sel)
    m2 = jnp.max(rest, axis=0, keepdims=True)
    i2 = jnp.min(jnp.where(rest == m2, rid, float(EPG)), axis=0, keepdims=True)
    e2x = jnp.exp(m2 - m1)
    den = 1.0 + e2x
    w1 = (1.0 / den) * g_p
    w2 = (e2x / den) * g_p
    e1 = gidx * float(EPG) + i1
    e2 = gidx * float(EPG) + i2

    eid = lax.broadcasted_iota(jnp.int32, (N_EXPERTS, TR), 0).astype(jnp.float32)
    oh1 = (eid == e1).astype(jnp.float32)
    oh2 = (eid == e2).astype(jnp.float32)
    oh = oh1 + oh2
    tr = lax.broadcasted_iota(jnp.int32, (TR, TR), 0)
    tc = lax.broadcasted_iota(jnp.int32, (TR, TR), 1)
    upper = (tr < tc).astype(jnp.bfloat16)
    before = jnp.dot(oh.astype(jnp.bfloat16), upper, preferred_element_type=jnp.float32)
    base = before + carry_ref[:, 0:1]
    r1 = jnp.sum(oh1 * base, axis=0, keepdims=True)
    r2 = jnp.sum(oh2 * base, axis=0, keepdims=True)
    carry_ref[...] = carry_ref[...] + jnp.sum(oh, axis=1, keepdims=True)

    idx_ref[...] = jnp.zeros_like(idx_ref)
    idx_ref[0:1, :] = e1.astype(jnp.int32)
    idx_ref[1:2, :] = e2.astype(jnp.int32)
    idx_ref[2:3, :] = r1.astype(jnp.int32)
    idx_ref[3:4, :] = r2.astype(jnp.int32)
    wgt_ref[...] = jnp.zeros_like(wgt_ref)
    wgt_ref[0:1, :] = w1
    wgt_ref[1:2, :] = w2
    cnt_ref[...] = carry_ref[...]


SC_WINDOW = 128


def _sc_mesh():
    return plsc.VectorSubcoreMesh(core_axis_name="c", subcore_axis_name="s")


def _sc_scatter_rows(rows, idx, passes, n_out):
    n_src = rows.shape[0]
    src_blocks = n_src // SC_WINDOW

    @functools.partial(
        pl.kernel, mesh=_sc_mesh(),
        out_type=jax.ShapeDtypeStruct((n_out, 128), rows.dtype))
    def scatter(rows_hbm, idx_hbm, out_hbm):
        def body(rows_vmem, idx_vmem):
            pltpu.sync_copy(rows_vmem, out_hbm.at[idx_vmem.at[0]])

        pltpu.emit_pipeline(
            body,
            grid=(passes * src_blocks,),
            in_specs=[pl.BlockSpec((SC_WINDOW, 128), lambda i: (i % src_blocks, 0)),
                      pl.BlockSpec((1, SC_WINDOW), lambda i: (0, i))],
            out_specs=[],
            core_axis_name=("c", "s"),
            dimension_semantics=(pltpu.PARALLEL,),
        )(rows_hbm, idx_hbm)

    return scatter(rows, idx.reshape(1, passes * n_src))


def _expert_kernel(layer, be_ref, nu_ref, nxt_ref, gs_ref, nv_ref, xb_ref, w1_ref, w3_ref, w2_ref, yb_ref,
                   wf1_ref, wf3_ref, wf2_ref, w1b_ref, w3b_ref, w2b_ref, sem):
    b = pl.program_id(0)
    used = b < nu_ref[0]
    new_expert = jnp.logical_or(b == 0, be_ref[b] != be_ref[jnp.maximum(b - 1, 0)])

    def weight_copies(e, s):
        return [pltpu.make_async_copy(src.at[layer, e], dst.at[s], sem.at[s])
                for src, dst in ((w1_ref, wf1_ref), (w3_ref, wf3_ref), (w2_ref, wf2_ref))]

    @pl.when(b == 0)
    def _():
        for cp in weight_copies(be_ref[0], 0):
            cp.start(priority=1)

    @pl.when(jnp.logical_and(used, new_expert))
    def _():
        s = gs_ref[b]
        for cp in weight_copies(be_ref[b], s):
            cp.wait()

        @pl.when(nxt_ref[b] >= 0)
        def _():
            for cp in weight_copies(nxt_ref[b], 1 - s):
                cp.start(priority=1)

        w1b_ref[...] = wf1_ref[s].astype(jnp.bfloat16)
        w3b_ref[...] = wf3_ref[s].astype(jnp.bfloat16)
        w2b_ref[...] = wf2_ref[s].astype(jnp.bfloat16)

    def expert_mlp(rows):
        live = lax.broadcasted_iota(jnp.int32, (rows, 1), 0) < nv_ref[b]
        packed_in = jnp.concatenate(
            [jnp.where(live, lax.bitcast_convert_type(xb_ref[j, 0:rows], jnp.uint32), jnp.uint32(0))
             for j in range(NPIECE)], axis=1)
        x = jnp.concatenate(_unpack_bf16_pairs(packed_in), axis=1).astype(jnp.bfloat16)
        dot32 = functools.partial(jnp.dot, preferred_element_type=jnp.float32)
        a = dot32(x, w1b_ref[...])
        g = dot32(x, w3b_ref[...])
        hmid = (a * _sigmoid(a) * g).astype(jnp.bfloat16)
        packed = _pack_bf16_pairs(dot32(hmid, w2b_ref[...]))
        for j in range(NPIECE):
            yb_ref[j, 0:rows] = lax.bitcast_convert_type(packed[:, j * 128:(j + 1) * 128], jnp.int32)

    half = BM // 2
    half_full = nv_ref[b] <= half

    @pl.when(jnp.logical_and(used, jnp.logical_not(half_full)))
    def _():
        expert_mlp(BM)

    @pl.when(jnp.logical_and(used, half_full))
    def _():
        expert_mlp(half)
        yb_ref[:, half:BM, :] = jnp.zeros((NPIECE, BM - half, 128), jnp.int32)

    @pl.when(jnp.logical_not(used))
    def _():
        yb_ref[...] = jnp.zeros_like(yb_ref)


def _experts(layer, block_expert, n_used, next_expert, group_slot, block_valid, xb, w1, w3, w2):
    n_slots = xb.shape[1]
    n_blocks = n_slots // BM

    def row_map(b, be, nu, nxt, gs, nv):
        return (0, jnp.minimum(b, nu[0] - 1), 0)

    def out_map(b, be, nu, nxt, gs, nv):
        return (0, b, 0)

    grid_spec = pltpu.PrefetchScalarGridSpec(
        num_scalar_prefetch=5,
        grid=(n_blocks,),
        in_specs=[pl.BlockSpec((NPIECE, BM, 128), row_map),
                  pl.BlockSpec(memory_space=pl.ANY),
                  pl.BlockSpec(memory_space=pl.ANY),
                  pl.BlockSpec(memory_space=pl.ANY)],
        out_specs=pl.BlockSpec((NPIECE, BM, 128), out_map),
        scratch_shapes=[pltpu.VMEM((2, D, D_FF), jnp.float32),
                        pltpu.VMEM((2, D, D_FF), jnp.float32),
                        pltpu.VMEM((2, D_FF, D), jnp.float32),
                        pltpu.VMEM((D, D_FF), jnp.bfloat16),
                        pltpu.VMEM((D, D_FF), jnp.bfloat16),
                        pltpu.VMEM((D_FF, D), jnp.bfloat16),
                        pltpu.SemaphoreType.DMA((2,))],
    )
    return pl.pallas_call(
        functools.partial(_expert_kernel, layer),
        grid_spec=grid_spec,
        out_shape=jax.ShapeDtypeStruct((NPIECE, n_slots, 128), jnp.int32),
        compiler_params=pltpu.CompilerParams(dimension_semantics=("arbitrary",),
                                             vmem_limit_bytes=VMEM_LIMIT),
        name="experts",
    )(block_expert, n_used, next_expert, group_slot, block_valid, xb, w1, w3, w2)


def _sc_gather_rows(table, idx):
    n_rows = idx.shape[0]

    @functools.partial(
        pl.kernel, mesh=_sc_mesh(),
        out_type=jax.ShapeDtypeStruct((n_rows, 128), table.dtype))
    def gather(table_hbm, idx_hbm, out_hbm):
        def body(idx_vmem, out_vmem):
            pltpu.sync_copy(table_hbm.at[idx_vmem.at[0]], out_vmem)

        pltpu.emit_pipeline(
            body,
            grid=(n_rows // SC_WINDOW,),
            in_specs=[pl.BlockSpec((1, SC_WINDOW), lambda i: (0, i))],
            out_specs=[pl.BlockSpec((SC_WINDOW, 128), lambda i: (i, 0))],
            core_axis_name=("c", "s"),
            dimension_semantics=(pltpu.PARALLEL,),
        )(idx_hbm, out_hbm)

    return gather(table, idx.reshape(1, n_rows))


def _combine_kernel(final, g_ref, x_ref, w_ref, nrm_ref, o_ref):
    w = w_ref[...].T
    ys = {}
    for j in range(NPIECE):
        hi0, lo0 = _unpack_bf16_pairs(lax.bitcast_convert_type(g_ref[0, j], jnp.uint32))
        hi1, lo1 = _unpack_bf16_pairs(lax.bitcast_convert_type(g_ref[1, j], jnp.uint32))
        c_hi, c_lo = j * 128, DP + j * 128
        ys[c_hi] = x_ref[:, c_hi:c_hi + 128] + w[:, 0:1] * hi0 + w[:, 1:2] * hi1
        ys[c_lo] = x_ref[:, c_lo:c_lo + 128] + w[:, 0:1] * lo0 + w[:, 1:2] * lo1
    if final:
        ms = sum(jnp.sum(y * y, axis=-1, keepdims=True) for y in ys.values()) * (1.0 / D)
        scale = lax.rsqrt(ms + RMS_EPS)
        ys = {c: y * scale * nrm_ref[:, c:c + 128] for c, y in ys.items()}
    for c, y in ys.items():
        o_ref[:, c:c + 128] = y


def _combine(gathered, x2, wgt, nrm, final):
    T = x2.shape[0]
    return pl.pallas_call(
        functools.partial(_combine_kernel, final),
        grid=(T // TD,),
        in_specs=[pl.BlockSpec((2, NPIECE, TD, 128), lambda i: (0, 0, i, 0)),
                  pl.BlockSpec((TD, D), lambda i: (i, 0)),
                  pl.BlockSpec((8, TD), lambda i: (0, i)),
                  pl.BlockSpec((1, D), lambda i: (0, 0))],
        out_specs=pl.BlockSpec((TD, D), lambda i: (i, 0)),
        out_shape=jax.ShapeDtypeStruct((T, D), jnp.float32),
        compiler_params=pltpu.CompilerParams(dimension_semantics=("arbitrary",)),
        name="combine",
    )(gathered, x2, wgt, nrm)


def _moe(layer, h2, idx, cnt, w1, w3, w2):
    T = idx.shape[1]
    counts = cnt[:, 0].astype(jnp.int32)
    padded = (counts + BM - 1) // BM * BM
    padded_end = jnp.cumsum(padded)
    padded_start = padded_end - padded
    eids = jnp.arange(N_EXPERTS, dtype=jnp.int32)[:, None]
    start_of = lambda e: jnp.sum(jnp.where(e[None, :] == eids, padded_start[:, None], 0), axis=0)
    dest = jnp.stack([start_of(idx[0]) + idx[2], start_of(idx[1]) + idx[3]])
    n_blocks = (2 * T) // BM + N_EXPERTS
    block_start = jnp.arange(n_blocks, dtype=jnp.int32) * BM
    block_expert = jnp.minimum(
        jnp.sum((padded_end[None, :] <= block_start[:, None]).astype(jnp.int32), axis=1),
        N_EXPERTS - 1)
    n_used = (padded_end[-1] // BM).reshape(1)
    later = jnp.logical_and(eids.T > eids, (counts > 0)[None, :])
    next_of = jnp.min(jnp.where(later, eids.T, N_EXPERTS), axis=1)
    next_of = jnp.where(next_of < N_EXPERTS, next_of, -1)
    pick = lambda table: jnp.sum(
        jnp.where(block_expert[:, None] == eids.T, table[None, :], 0), axis=1)
    next_expert = pick(next_of)
    group_slot = pick(jnp.cumsum((counts > 0).astype(jnp.int32)) - 1) % 2
    block_valid = jnp.clip(pick(padded_start + counts) - block_start, 0, BM)
    n_slots = n_blocks * BM
    piece_base = (jnp.arange(NPIECE, dtype=jnp.int32) * n_slots)[None, :, None]
    rows = (dest[:, None, :] + piece_base).reshape(-1)
    xb = _sc_scatter_rows(h2.reshape(NPIECE * T, 128), rows, 2, NPIECE * n_slots)
    yb = _experts(layer, block_expert, n_used, next_expert, group_slot, block_valid,
                  xb.reshape(NPIECE, n_slots, 128), w1, w3, w2)
    gathered = _sc_gather_rows(yb.reshape(NPIECE * n_slots, 128), rows)
    return gathered.reshape(2, NPIECE, T, 128)


def kernel(x, norm_mix, w_in, conv_a, w_a_out, conv_b, ln_b_g, ln_b_b, w_b_out, ln_c_g, ln_c_b,
           w_s, b_s, w_c_out, w_o, norm_ffn, w_group, b_group, w_router, b_router, w1, w3, w2,
           norm_final):
    bsz, seq, d = x.shape
    depth = norm_mix.shape[0]
    bf = jnp.bfloat16
    x2 = x.reshape(bsz * seq, d)
    wr = jnp.zeros((depth, d, NLOG), jnp.float32)
    wr = wr.at[:, :, 0:N_GROUPS].set(w_group).at[:, :, E_OFF:E_OFF + N_EXPERTS].set(w_router)
    wr_hi = wr.astype(bf)
    wr_lo = (wr - wr_hi.astype(jnp.float32)).astype(bf)
    br = jnp.zeros((depth, 1, NLOG), jnp.float32)
    br = br.at[:, 0, 0:N_GROUPS].set(b_group).at[:, 0, E_OFF:E_OFF + N_EXPERTS].set(b_router)
    bsb = jnp.broadcast_to(b_s[:, :, :, None], (depth, N_HEADS_C, CHUNK, CHUNK))
    row = lambda p: p[:, None, :]
    mixer_params = (row(norm_mix), w_in.astype(bf), conv_a, w_a_out.astype(bf), conv_b,
                    row(ln_b_g), row(ln_b_b), w_b_out.astype(bf), row(ln_c_g), row(ln_c_b),
                    w_s.astype(bf), bsb, w_c_out.astype(bf), w_o.astype(bf), row(norm_ffn), wr_hi, wr_lo, br)
    col = jnp.arange(8 * PR, dtype=jnp.int32)[None, :]
    shift_sum = (col % PR == jnp.arange(RB, dtype=jnp.int32)[:, None] + col // PR).astype(bf)
    pending = None
    for l in range(depth):
        x2, h2, idx, wgt, cnt = _mixer(l, x2, pending, seq, *mixer_params, shift_sum)
        gathered = _moe(l, h2, idx, cnt, w1, w3, w2)
        pending = (gathered, wgt.T)
    out = _combine(gathered, x2, wgt, norm_final[None], True)
    return out.reshape(bsz, seq, d)
```

```python
import functools

import jax
import jax.numpy as jnp
from jax import lax
from jax.experimental import pallas as pl
from jax.experimental.pallas import tpu as pltpu
from jax.experimental.pallas import tpu_sc as plsc

D = 1024
DP = D // 2
NPIECE = DP // 128
N_HEADS_C = 8
CHUNK = 128
CONV_A = 3
CONV_B = 31
N_GROUPS = 4
EPG = 8
N_EXPERTS = N_GROUPS * EPG
D_FF = 512
RMS_EPS = 1e-6
LN_EPS = 1e-5

C_XA, C_BA, C_CA, C_VB, C_GB, C_UV, C_G = 0, 1024, 2048, 3072, 4096, 5120, 7168
D_IN = 10240

TM = 512
HALO = 16
TE = TM + 2 * HALO
CW = 256
RB = 64
PR = RB + 16
NLOG = 128
E_OFF = 8

TR = TM
BM = 512
TD = 512

VMEM_LIMIT = 60 * 1024 * 1024


def _sigmoid(x):
    return 1.0 / (1.0 + jnp.exp(-x))


def _gelu_tanh(x):
    two_u = x * (1.5957691216057308 + 0.07135481627133298 * (x * x))
    return x * _sigmoid(two_u)


def _pack_bf16_pairs(x):
    c = x.shape[1] // 2
    as_bits = lambda v: lax.bitcast_convert_type(v.astype(jnp.bfloat16).astype(jnp.float32), jnp.uint32)
    return as_bits(x[:, :c]) | (as_bits(x[:, c:]) >> 16)


def _unpack_bf16_pairs(p):
    hi = lax.bitcast_convert_type(p & jnp.uint32(0xFFFF0000), jnp.float32)
    lo = lax.bitcast_convert_type(p << 16, jnp.float32)
    return hi, lo


def _layer_norm(x, g, b):
    mu = jnp.mean(x, axis=-1, keepdims=True)
    xc = x - mu
    var = jnp.mean(xc * xc, axis=-1, keepdims=True)
    return xc * lax.rsqrt(var + LN_EPS) * g + b


def _moe_residual(x_ref, g_ref, wt_ref):
    w0, w1 = wt_ref[:, 0:1], wt_ref[:, 1:2]
    his, los = [], []
    for j in range(NPIECE):
        hi0, lo0 = _unpack_bf16_pairs(lax.bitcast_convert_type(g_ref[0, j], jnp.uint32))
        hi1, lo1 = _unpack_bf16_pairs(lax.bitcast_convert_type(g_ref[1, j], jnp.uint32))
        his.append(w0 * hi0 + w1 * hi1)
        los.append(w0 * lo0 + w1 * lo1)
    return x_ref[...] + jnp.concatenate(his + los, axis=1)


def _mixer_kernel(seq_tiles, fused_combine, xp_ref, xc_ref, xn_ref, *refs):
    if fused_combine:
        gp_ref, gc_ref, gn_ref, wtp_ref, wtc_ref, wtn_ref = refs[:6]
        refs = refs[6:]
        x_prev = lambda: _moe_residual(xp_ref, gp_ref, wtp_ref)
        x_cur = lambda: _moe_residual(xc_ref, gc_ref, wtc_ref)
        x_next = lambda: _moe_residual(xn_ref, gn_ref, wtn_ref)
    else:
        x_prev, x_cur, x_next = (lambda: xp_ref[...]), (lambda: xc_ref[...]), (lambda: xn_ref[...])
    (nm_ref, win_ref, ca_ref, wa_ref, cb_ref, lnbg_ref, lnbb_ref, wb_ref, lncg_ref, lncb_ref,
     ws_ref, bsb_ref, wc_ref, wo_ref, nf_ref, wrh_ref, wrl_ref, br_ref, sh_ref,
     xo_ref, h2_ref, idx_ref, wgt_ref, cnt_ref,
     hb_ref, zq_ref, cva_ref, cvb_ref, cvn_ref, vst_ref, mg_ref, carry_ref) = refs
    i = pl.program_id(0)
    at_start = (i % seq_tiles) == 0
    at_end = (i % seq_tiles) == seq_tiles - 1

    def _rms(xv):
        ms = jnp.mean(xv * xv, axis=-1, keepdims=True)
        return (xv * lax.rsqrt(ms + RMS_EPS) * nm_ref[...]).astype(jnp.bfloat16)

    hb_ref[0:HALO, :] = _rms(x_prev())
    hb_ref[HALO:HALO + TM, :] = _rms(x_cur())
    hb_ref[HALO + TM:TE, :] = _rms(x_next())

    rows = lax.broadcasted_iota(jnp.int32, (TE, 1), 0)
    lo = jnp.where(at_start, HALO, 0)
    hi = jnp.where(at_end, HALO + TM, TE)
    valid = jnp.logical_and(rows >= lo, rows < hi)

    def proj(r0, r1, c0, width):
        return jnp.dot(hb_ref[r0:r1, :], win_ref[:, c0:c0 + width],
                       preferred_element_type=jnp.float32)

    def b_proj(c):
        c0 = c * CW
        val = proj(0, TE, C_VB + c0, CW)
        gate = proj(0, TE, C_GB + c0, CW)
        zq_ref[c, 0:TE] = jnp.where(valid, val * _sigmoid(gate), 0.0)
        zq_ref[c, TE:TE + PR - RB - 8] = jnp.zeros((PR - RB - 8, CW), jnp.float32)

    def a_proj(c):
        c0 = c * CW
        xa = proj(0, TE, C_XA + c0, CW)
        cc = proj(0, TE, C_CA + c0, CW)
        t = jnp.where(valid, xa * cc, 0.0)
        conv = (ca_ref[0:1, c0:c0 + CW] * t[HALO - 1:HALO - 1 + TM]
                + ca_ref[1:2, c0:c0 + CW] * t[HALO:HALO + TM]
                + ca_ref[2:3, c0:c0 + CW] * t[HALO + 1:HALO + 1 + TM])
        ba = proj(HALO, HALO + TM, C_BA + c0, CW)
        cva_ref[:, c0:c0 + CW] = (ba * conv).astype(jnp.bfloat16)

    def c_v_proj(c):
        c0 = c * CW
        vst_ref[:, c0:c0 + CW] = _gelu_tanh(proj(HALO, HALO + TM, C_UV + D + c0, CW))

    def b_conv(c, row_blocks):
        c0 = c * CW
        zq = zq_ref.at[c]
        for rb in row_blocks:
            r0 = rb * RB
            parts = []
            for s in range(8):
                part = None
                for q in range(4):
                    k = 8 * q + s - (HALO - CONV_B // 2)
                    if 0 <= k < CONV_B:
                        term = cb_ref[k:k + 1, c0:c0 + CW] * zq[r0 + 8 * q:r0 + 8 * q + PR, :]
                        part = term if part is None else part + term
                parts.append(part.astype(jnp.bfloat16))
            xo_ref[r0:r0 + RB, c0:c0 + CW] = jnp.dot(
                sh_ref[...], jnp.concatenate(parts, axis=0), preferred_element_type=jnp.float32)

    n_chunks = D // CW
    n_rb = TM // RB
    rb_step = n_rb // n_chunks
    late_rows = lambda c: range(c * rb_step, (c + 1) * rb_step)
    b_proj(0)
    for c in range(n_chunks):
        if c + 1 < n_chunks:
            b_proj(c + 1)
        a_proj(c)
        c_v_proj(c)
    b_conv(0, range(n_rb))

    vn = _layer_norm(vst_ref[...], lncg_ref[...], lncb_ref[...])
    cvn_ref[...] = vn.astype(jnp.bfloat16)
    for c in range(n_chunks):
        c0 = c * CW
        ya = jnp.dot(cva_ref[...], wa_ref[:, c0:c0 + CW], preferred_element_type=jnp.float32)
        ga = _sigmoid(proj(HALO, HALO + TM, C_G + c0, CW))
        mg_ref[:, c0:c0 + CW] = ga * ya
        b_conv(1, late_rows(c))

    for c in range(n_chunks):
        c0 = c * CW
        u = _gelu_tanh(proj(HALO, HALO + TM, C_UV + c0, CW))
        for n in range(TM // CHUNK):
            for hh in range(CW // CHUNK):
                h = c * (CW // CHUNK) + hh
                sv = jnp.dot(ws_ref[h], cvn_ref[n * CHUNK:(n + 1) * CHUNK, h * CHUNK:(h + 1) * CHUNK],
                             preferred_element_type=jnp.float32) + bsb_ref[h]
                cva_ref[n * CHUNK:(n + 1) * CHUNK, h * CHUNK:(h + 1) * CHUNK] = (
                    u[n * CHUNK:(n + 1) * CHUNK, hh * CHUNK:(hh + 1) * CHUNK] * sv
                ).astype(jnp.bfloat16)
        b_conv(2, late_rows(c))
    for c in range(n_chunks):
        c0 = c * CW
        yc = jnp.dot(cva_ref[...], wc_ref[:, c0:c0 + CW], preferred_element_type=jnp.float32)
        gc = _sigmoid(proj(HALO, HALO + TM, C_G + 2 * D + c0, CW))
        mg_ref[:, c0:c0 + CW] += gc * yc
        b_conv(3, late_rows(c))

    zc = _layer_norm(xo_ref[...], lnbg_ref[...], lnbb_ref[...])
    cvb_ref[...] = (zc * _sigmoid(zc)).astype(jnp.bfloat16)
    for c in range(n_chunks):
        c0 = c * CW
        yb = jnp.dot(cvb_ref[...], wb_ref[:, c0:c0 + CW], preferred_element_type=jnp.float32)
        gb = _sigmoid(proj(HALO, HALO + TM, C_G + D + c0, CW))
        mg_ref[:, c0:c0 + CW] += gb * yb

    cvb_ref[...] = mg_ref[...].astype(jnp.bfloat16)
    x_res = x_cur()
    for c in range(n_chunks):
        c0 = c * CW
        xo_ref[:, c0:c0 + CW] = x_res[:, c0:c0 + CW] + jnp.dot(
            cvb_ref[...], wo_ref[:, c0:c0 + CW], preferred_element_type=jnp.float32)
    xnew = xo_ref[...]
    ms = jnp.mean(xnew * xnew, axis=-1, keepdims=True)
    h2 = xnew * lax.rsqrt(ms + RMS_EPS) * nf_ref[...]
    h2_packed = _pack_bf16_pairs(h2)
    for j in range(NPIECE):
        h2_ref[j] = lax.bitcast_convert_type(h2_packed[:, j * 128:(j + 1) * 128], jnp.int32)
    h_hi = h2.astype(jnp.bfloat16)
    h_lo = (h2 - h_hi.astype(jnp.float32)).astype(jnp.bfloat16)
    dot32 = functools.partial(jnp.dot, preferred_element_type=jnp.float32)
    logits = (dot32(h_hi, wrh_ref[...]) + dot32(h_lo, wrh_ref[...])
              + dot32(h_hi, wrl_ref[...]) + dot32(h_lo, wrl_ref[...]) + br_ref[...])
    _route_tile(logits, idx_ref, wgt_ref, cnt_ref, carry_ref)


def _layer_spec(layer, shape):
    nd = len(shape)
    return pl.BlockSpec((None,) + tuple(shape), lambda i, _n=nd: (layer,) + (0,) * _n,
                        pipeline_mode=pl.Buffered(1))


def _mixer(layer, x2, pending, seq_len, nm, win, ca, wa, cb, lnbg, lnbb, wb, lncg, lncb, ws, bsb, wc, wo,
           nf, wrh, wrl, br, sh):
    T = x2.shape[0]
    n_tiles = T // TM
    hb = TM // HALO
    last_halo = T // HALO - 1
    ls = functools.partial(_layer_spec, layer)
    prev_rows = lambda i: jnp.maximum(i * hb - 1, 0)
    next_rows = lambda i: jnp.minimum((i + 1) * hb, last_halo)
    in_specs = [
        pl.BlockSpec((HALO, D), lambda i: (prev_rows(i), 0)),
        pl.BlockSpec((TM, D), lambda i: (i, 0)),
        pl.BlockSpec((HALO, D), lambda i: (next_rows(i), 0)),
    ]
    operands = [x2, x2, x2]
    if pending is not None:
        gathered, wgt_rows = pending
        in_specs += [
            pl.BlockSpec((2, NPIECE, HALO, 128), lambda i: (0, 0, prev_rows(i), 0)),
            pl.BlockSpec((2, NPIECE, TM, 128), lambda i: (0, 0, i, 0)),
            pl.BlockSpec((2, NPIECE, HALO, 128), lambda i: (0, 0, next_rows(i), 0)),
            pl.BlockSpec((HALO, 8), lambda i: (prev_rows(i), 0)),
            pl.BlockSpec((TM, 8), lambda i: (i, 0)),
            pl.BlockSpec((HALO, 8), lambda i: (next_rows(i), 0)),
        ]
        operands += [gathered, gathered, gathered, wgt_rows, wgt_rows, wgt_rows]
    in_specs += [
        ls((1, D)), ls((D, D_IN)), ls((CONV_A, D)), ls((D, D)),
        ls((CONV_B, D)), ls((1, D)), ls((1, D)), ls((D, D)),
        ls((1, D)), ls((1, D)), ls((N_HEADS_C, CHUNK, CHUNK)),
        ls((N_HEADS_C, CHUNK, CHUNK)), ls((D, D)), ls((D, D)),
        ls((1, D)), ls((D, NLOG)), ls((D, NLOG)), ls((1, NLOG)),
        pl.BlockSpec((RB, 8 * PR), lambda i: (0, 0), pipeline_mode=pl.Buffered(1)),
    ]
    out_specs = [
        pl.BlockSpec((TM, D), lambda i: (i, 0)),
        pl.BlockSpec((NPIECE, TM, 128), lambda i: (0, i, 0)),
        pl.BlockSpec((8, TM), lambda i: (0, i)),
        pl.BlockSpec((8, TM), lambda i: (0, i)),
        pl.BlockSpec((N_EXPERTS, 128), lambda i: (0, 0)),
    ]
    return pl.pallas_call(
        functools.partial(_mixer_kernel, seq_len // TM, pending is not None),
        grid=(n_tiles,),
        in_specs=in_specs,
        out_specs=out_specs,
        out_shape=[jax.ShapeDtypeStruct((T, D), jnp.float32),
                   jax.ShapeDtypeStruct((NPIECE, T, 128), jnp.int32),
                   jax.ShapeDtypeStruct((8, T), jnp.int32),
                   jax.ShapeDtypeStruct((8, T), jnp.float32),
                   jax.ShapeDtypeStruct((N_EXPERTS, 128), jnp.float32)],
        scratch_shapes=[pltpu.VMEM((TE, D), jnp.bfloat16),
                        pltpu.VMEM((D // CW, TE + PR - RB - 8, CW), jnp.float32),
                        pltpu.VMEM((TM, D), jnp.bfloat16),
                        pltpu.VMEM((TM, D), jnp.bfloat16),
                        pltpu.VMEM((TM, D), jnp.bfloat16),
                        pltpu.VMEM((TM, D), jnp.float32),
                        pltpu.VMEM((TM, D), jnp.float32),
                        pltpu.VMEM((N_EXPERTS, 128), jnp.float32)],
        compiler_params=pltpu.CompilerParams(dimension_semantics=("arbitrary",),
                                             vmem_limit_bytes=VMEM_LIMIT),
        name="mixer",
    )(*operands, nm, win, ca, wa, cb, lnbg, lnbb, wb, lncg, lncb, ws, bsb, wc, wo, nf, wrh, wrl, br, sh)


def _route_tile(lg, idx_ref, wgt_ref, cnt_ref, carry_ref):
    i = pl.program_id(0)

    @pl.when(i == 0)
    def _():
        carry_ref[...] = jnp.zeros_like(carry_ref)

    lt = lg.T
    g = [lt[j:j + 1, :] for j in range(N_GROUPS)]
    gmax = jnp.maximum(jnp.maximum(g[0], g[1]), jnp.maximum(g[2], g[3]))
    gidx = jnp.where(g[0] == gmax, 0.0, jnp.where(g[1] == gmax, 1.0, jnp.where(g[2] == gmax, 2.0, 3.0)))
    gsum = sum(jnp.exp(gj - gmax) for gj in g)
    g_p = 1.0 / gsum

    sel = lt[E_OFF + 3 * EPG:E_OFF + 4 * EPG, :]
    for j in (2, 1, 0):
        sel = jnp.where(gidx == float(j), lt[E_OFF + j * EPG:E_OFF + (j + 1) * EPG, :], sel)
    rid = lax.broadcasted_iota(jnp.int32, (EPG, TR), 0).astype(jnp.float32)
    m1 = jnp.max(sel, axis=0, keepdims=True)
    i1 = jnp.min(jnp.where(sel == m1, rid, float(EPG)), axis=0, keepdims=True)
    rest = jnp.where(rid == i1, -jnp.inf, sel)
    m2 = jnp.max(rest, axis=0, keepdims=True)
    i2 = jnp.min(jnp.where(rest == m2, rid, float(EPG)), axis=0, keepdims=True)
    e2x = jnp.exp(m2 - m1)
    den = 1.0 + e2x
    w1 = (1.0 / den) * g_p
    w2 = (e2x / den) * g_p
    e1 = gidx * float(EPG) + i1
    e2 = gidx * float(EPG) + i2

    eid = lax.broadcasted_iota(jnp.int32, (N_EXPERTS, TR), 0).astype(jnp.float32)
    oh1 = (eid == e1).astype(jnp.float32)
    oh2 = (eid == e2).astype(jnp.float32)
    oh = oh1 + oh2
    tr = lax.broadcasted_iota(jnp.int32, (TR, TR), 0)
    tc = lax.broadcasted_iota(jnp.int32, (TR, TR), 1)
    upper = (tr < tc).astype(jnp.bfloat16)
    before = jnp.dot(oh.astype(jnp.bfloat16), upper, preferred_element_type=jnp.float32)
    base = before + carry_ref[:, 0:1]
    r1 = jnp.sum(oh1 * base, axis=0, keepdims=True)
    r2 = jnp.sum(oh2 * base, axis=0, keepdims=True)
    carry_ref[...] = carry_ref[...] + jnp.sum(oh, axis=1, keepdims=True)

    idx_ref[...] = jnp.zeros_like(idx_ref)
    idx_ref[0:1, :] = e1.astype(jnp.int32)
    idx_ref[1:2, :] = e2.astype(jnp.int32)
    idx_ref[2:3, :] = r1.astype(jnp.int32)
    idx_ref[3:4, :] = r2.astype(jnp.int32)
    wgt_ref[...] = jnp.zeros_like(wgt_ref)
    wgt_ref[0:1, :] = w1
    wgt_ref[1:2, :] = w2
    cnt_ref[...] = carry_ref[...]


SC_WINDOW = 128


def _sc_mesh():
    return plsc.VectorSubcoreMesh(core_axis_name="c", subcore_axis_name="s")


def _sc_scatter_rows(rows, idx, passes, n_out):
    n_src = rows.shape[0]
    src_blocks = n_src // SC_WINDOW

    @functools.partial(
        pl.kernel, mesh=_sc_mesh(),
        out_type=jax.ShapeDtypeStruct((n_out, 128), rows.dtype))
    def scatter(rows_hbm, idx_hbm, out_hbm):
        def body(rows_vmem, idx_vmem):
            pltpu.sync_copy(rows_vmem, out_hbm.at[idx_vmem.at[0]])

        pltpu.emit_pipeline(
            body,
            grid=(passes * src_blocks,),
            in_specs=[pl.BlockSpec((SC_WINDOW, 128), lambda i: (i % src_blocks, 0)),
                      pl.BlockSpec((1, SC_WINDOW), lambda i: (0, i))],
            out_specs=[],
            core_axis_name=("c", "s"),
            dimension_semantics=(pltpu.PARALLEL,),
        )(rows_hbm, idx_hbm)

    return scatter(rows, idx.reshape(1, passes * n_src))


def _expert_kernel(layer, be_ref, nu_ref, nxt_ref, gs_ref, nv_ref, xb_ref, w1_ref, w3_ref, w2_ref, yb_ref,
                   wf1_ref, wf3_ref, wf2_ref, w1b_ref, w3b_ref, w2b_ref, sem):
    b = pl.program_id(0)
    used = b < nu_ref[0]
    new_expert = jnp.logical_or(b == 0, be_ref[b] != be_ref[jnp.maximum(b - 1, 0)])

    def weight_copies(e, s):
        return [pltpu.make_async_copy(src.at[layer, e], dst.at[s], sem.at[s])
                for src, dst in ((w1_ref, wf1_ref), (w3_ref, wf3_ref), (w2_ref, wf2_ref))]

    @pl.when(b == 0)
    def _():
        for cp in weight_copies(be_ref[0], 0):
            cp.start(priority=1)

    @pl.when(jnp.logical_and(used, new_expert))
    def _():
        s = gs_ref[b]
        for cp in weight_copies(be_ref[b], s):
            cp.wait()

        @pl.when(nxt_ref[b] >= 0)
        def _():
            for cp in weight_copies(nxt_ref[b], 1 - s):
                cp.start(priority=1)

        w1b_ref[...] = wf1_ref[s].astype(jnp.bfloat16)
        w3b_ref[...] = wf3_ref[s].astype(jnp.bfloat16)
        w2b_ref[...] = wf2_ref[s].astype(jnp.bfloat16)

    def expert_mlp(rows):
        live = lax.broadcasted_iota(jnp.int32, (rows, 1), 0) < nv_ref[b]
        packed_in = jnp.concatenate(
            [jnp.where(live, lax.bitcast_convert_type(xb_ref[j, 0:rows], jnp.uint32), jnp.uint32(0))
             for j in range(NPIECE)], axis=1)
        x = jnp.concatenate(_unpack_bf16_pairs(packed_in), axis=1).astype(jnp.bfloat16)
        dot32 = functools.partial(jnp.dot, preferred_element_type=jnp.float32)
        a = dot32(x, w1b_ref[...])
        g = dot32(x, w3b_ref[...])
        hmid = (a * _sigmoid(a) * g).astype(jnp.bfloat16)
        packed = _pack_bf16_pairs(dot32(hmid, w2b_ref[...]))
        for j in range(NPIECE):
            yb_ref[j, 0:rows] = lax.bitcast_convert_type(packed[:, j * 128:(j + 1) * 128], jnp.int32)

    half = BM // 2
    half_full = nv_ref[b] <= half

    @pl.when(jnp.logical_and(used, jnp.logical_not(half_full)))
    def _():
        expert_mlp(BM)

    @pl.when(jnp.logical_and(used, half_full))
    def _():
        expert_mlp(half)
        yb_ref[:, half:BM, :] = jnp.zeros((NPIECE, BM - half, 128), jnp.int32)

    @pl.when(jnp.logical_not(used))
    def _():
        yb_ref[...] = jnp.zeros_like(yb_ref)


def _experts(layer, block_expert, n_used, next_expert, group_slot, block_valid, xb, w1, w3, w2):
    n_slots = xb.shape[1]
    n_blocks = n_slots // BM

    def row_map(b, be, nu, nxt, gs, nv):
        return (0, jnp.minimum(b, nu[0] - 1), 0)

    def out_map(b, be, nu, nxt, gs, nv):
        return (0, b, 0)

    grid_spec = pltpu.PrefetchScalarGridSpec(
        num_scalar_prefetch=5,
        grid=(n_blocks,),
        in_specs=[pl.BlockSpec((NPIECE, BM, 128), row_map),
                  pl.BlockSpec(memory_space=pl.ANY),
                  pl.BlockSpec(memory_space=pl.ANY),
                  pl.BlockSpec(memory_space=pl.ANY)],
        out_specs=pl.BlockSpec((NPIECE, BM, 128), out_map),
        scratch_shapes=[pltpu.VMEM((2, D, D_FF), jnp.float32),
                        pltpu.VMEM((2, D, D_FF), jnp.float32),
                        pltpu.VMEM((2, D_FF, D), jnp.float32),
                        pltpu.VMEM((D, D_FF), jnp.bfloat16),
                        pltpu.VMEM((D, D_FF), jnp.bfloat16),
                        pltpu.VMEM((D_FF, D), jnp.bfloat16),
                        pltpu.SemaphoreType.DMA((2,))],
    )
    return pl.pallas_call(
        functools.partial(_expert_kernel, layer),
        grid_spec=grid_spec,
        out_shape=jax.ShapeDtypeStruct((NPIECE, n_slots, 128), jnp.int32),
        compiler_params=pltpu.CompilerParams(dimension_semantics=("arbitrary",),
                                             vmem_limit_bytes=VMEM_LIMIT),
        name="experts",
    )(block_expert, n_used, next_expert, group_slot, block_valid, xb, w1, w3, w2)


def _sc_gather_rows(table, idx):
    n_rows = idx.shape[0]

    @functools.partial(
        pl.kernel, mesh=_sc_mesh(),
        out_type=jax.ShapeDtypeStruct((n_rows, 128), table.dtype))
    def gather(table_hbm, idx_hbm, out_hbm):
        def body(idx_vmem, out_vmem):
            pltpu.sync_copy(table_hbm.at[idx_vmem.at[0]], out_vmem)

        pltpu.emit_pipeline(
            body,
            grid=(n_rows // SC_WINDOW,),
            in_specs=[pl.BlockSpec((1, SC_WINDOW), lambda i: (0, i))],
            out_specs=[pl.BlockSpec((SC_WINDOW, 128), lambda i: (i, 0))],
            core_axis_name=("c", "s"),
            dimension_semantics=(pltpu.PARALLEL,),
        )(idx_hbm, out_hbm)

    return gather(table, idx.reshape(1, n_rows))


def _combine_kernel(final, g_ref, x_ref, w_ref, nrm_ref, o_ref):
    w = w_ref[...].T
    ys = {}
    for j in range(NPIECE):
        hi0, lo0 = _unpack_bf16_pairs(lax.bitcast_convert_type(g_ref[0, j], jnp.uint32))
        hi1, lo1 = _unpack_bf16_pairs(lax.bitcast_convert_type(g_ref[1, j], jnp.uint32))
        c_hi, c_lo = j * 128, DP + j * 128
        ys[c_hi] = x_ref[:, c_hi:c_hi + 128] + w[:, 0:1] * hi0 + w[:, 1:2] * hi1
        ys[c_lo] = x_ref[:, c_lo:c_lo + 128] + w[:, 0:1] * lo0 + w[:, 1:2] * lo1
    if final:
        ms = sum(jnp.sum(y * y, axis=-1, keepdims=True) for y in ys.values()) * (1.0 / D)
        scale = lax.rsqrt(ms + RMS_EPS)
        ys = {c: y * scale * nrm_ref[:, c:c + 128] for c, y in ys.items()}
    for c, y in ys.items():
        o_ref[:, c:c + 128] = y


def _combine(gathered, x2, wgt, nrm, final):
    T = x2.shape[0]
    return pl.pallas_call(
        functools.partial(_combine_kernel, final),
        grid=(T // TD,),
        in_specs=[pl.BlockSpec((2, NPIECE, TD, 128), lambda i: (0, 0, i, 0)),
                  pl.BlockSpec((TD, D), lambda i: (i, 0)),
                  pl.BlockSpec((8, TD), lambda i: (0, i)),
                  pl.BlockSpec((1, D), lambda i: (0, 0))],
        out_specs=pl.BlockSpec((TD, D), lambda i: (i, 0)),
        out_shape=jax.ShapeDtypeStruct((T, D), jnp.float32),
        compiler_params=pltpu.CompilerParams(dimension_semantics=("arbitrary",)),
        name="combine",
    )(gathered, x2, wgt, nrm)


def _moe(layer, h2, idx, cnt, w1, w3, w2):
    T = idx.shape[1]
    counts = cnt[:, 0].astype(jnp.int32)
    padded = (counts + BM - 1) // BM * BM
    padded_end = jnp.cumsum(padded)
    padded_start = padded_end - padded
    eids = jnp.arange(N_EXPERTS, dtype=jnp.int32)[:, None]
    start_of = lambda e: jnp.sum(jnp.where(e[None, :] == eids, padded_start[:, None], 0), axis=0)
    dest = jnp.stack([start_of(idx[0]) + idx[2], start_of(idx[1]) + idx[3]])
    n_blocks = (2 * T) // BM + N_EXPERTS
    block_start = jnp.arange(n_blocks, dtype=jnp.int32) * BM
    block_expert = jnp.minimum(
        jnp.sum((padded_end[None, :] <= block_start[:, None]).astype(jnp.int32), axis=1),
        N_EXPERTS - 1)
    n_used = (padded_end[-1] // BM).reshape(1)
    later = jnp.logical_and(eids.T > eids, (counts > 0)[None, :])
    next_of = jnp.min(jnp.where(later, eids.T, N_EXPERTS), axis=1)
    next_of = jnp.where(next_of < N_EXPERTS, next_of, -1)
    pick = lambda table: jnp.sum(
        jnp.where(block_expert[:, None] == eids.T, table[None, :], 0), axis=1)
    next_expert = pick(next_of)
    group_slot = pick(jnp.cumsum((counts > 0).astype(jnp.int32)) - 1) % 2
    block_valid = jnp.clip(pick(padded_start + counts) - block_start, 0, BM)
    n_slots = n_blocks * BM
    piece_base = (jnp.arange(NPIECE, dtype=jnp.int32) * n_slots)[None, :, None]
    rows = (dest[:, None, :] + piece_base).reshape(-1)
    xb = _sc_scatter_rows(h2.reshape(NPIECE * T, 128), rows, 2, NPIECE * n_slots)
    yb = _experts(layer, block_expert, n_used, next_expert, group_slot, block_valid,
                  xb.reshape(NPIECE, n_slots, 128), w1, w3, w2)
    gathered = _sc_gather_rows(yb.reshape(NPIECE * n_slots, 128), rows)
    return gathered.reshape(2, NPIECE, T, 128)


def kernel(x, norm_mix, w_in, conv_a, w_a_out, conv_b, ln_b_g, ln_b_b, w_b_out, ln_c_g, ln_c_b,
           w_s, b_s, w_c_out, w_o, norm_ffn, w_group, b_group, w_router, b_router, w1, w3, w2,
           norm_final):
    bsz, seq, d = x.shape
    depth = norm_mix.shape[0]
    bf = jnp.bfloat16
    x2 = x.reshape(bsz * seq, d)
    wr = jnp.zeros((depth, d, NLOG), jnp.float32)
    wr = wr.at[:, :, 0:N_GROUPS].set(w_group).at[:, :, E_OFF:E_OFF + N_EXPERTS].set(w_router)
    wr_hi = wr.astype(bf)
    wr_lo = (wr - wr_hi.astype(jnp.float32)).astype(bf)
    br = jnp.zeros((depth, 1, NLOG), jnp.float32)
    br = br.at[:, 0, 0:N_GROUPS].set(b_group).at[:, 0, E_OFF:E_OFF + N_EXPERTS].set(b_router)
    bsb = jnp.broadcast_to(b_s[:, :, :, None], (depth, N_HEADS_C, CHUNK, CHUNK))
    row = lambda p: p[:, None, :]
    mixer_params = (row(norm_mix), w_in.astype(bf), conv_a, w_a_out.astype(bf), conv_b,
                    row(ln_b_g), row(ln_b_b), w_b_out.astype(bf), row(ln_c_g), row(ln_c_b),
                    w_s.astype(bf), bsb, w_c_out.astype(bf), w_o.astype(bf), row(norm_ffn), wr_hi, wr_lo, br)
    col = jnp.arange(8 * PR, dtype=jnp.int32)[None, :]
    shift_sum = (col % PR == jnp.arange(RB, dtype=jnp.int32)[:, None] + col // PR).astype(bf)
    pending = None
    for l in range(depth):
        x2, h2, idx, wgt, cnt = _mixer(l, x2, pending, seq, *mixer_params, shift_sum)
        gathered = _moe(l, h2, idx, cnt, w1, w3, w2)
        pending = (gathered, wgt.T)
    out = _combine(gathered, x2, wgt, norm_final[None], True)
    return out.reshape(bsz, seq, d)
```

```python
import functools

import jax
import jax.numpy as jnp
from jax import lax
from jax.experimental import pallas as pl
from jax.experimental.pallas import tpu as pltpu
from jax.experimental.pallas import tpu_sc as plsc

D = 1024
DP = D // 2
NPIECE = DP // 128
N_HEADS_C = 8
CHUNK = 128
CONV_A = 3
CONV_B = 31
N_GROUPS = 4
EPG = 8
N_EXPERTS = N_GROUPS * EPG
D_FF = 512
RMS_EPS = 1e-6
LN_EPS = 1e-5

C_XA, C_BA, C_CA, C_VB, C_GB, C_UV, C_G = 0, 1024, 2048, 3072, 4096, 5120, 7168
D_IN = 10240

TM = 512
HALO = 16
TE = TM + 2 * HALO
CW = 256
RB = 64
PR = RB + 8
SK = -(-8 * PR // 128) * 128
NLOG = 128
E_OFF = 8

TR = TM
BM = 512
TD = 512

VMEM_LIMIT = 60 * 1024 * 1024


def _sigmoid(x):
    return 0.5 * (jnp.tanh(0.5 * x) + 1.0)


def _gelu_tanh(x):
    return 0.5 * x * (1.0 + jnp.tanh(0.7978845608028654 * (x + 0.044715 * (x * x * x))))


def _pack_bf16_pairs(x):
    c = x.shape[1] // 2
    as_bits = lambda v: lax.bitcast_convert_type(v.astype(jnp.bfloat16).astype(jnp.float32), jnp.uint32)
    return as_bits(x[:, :c]) | (as_bits(x[:, c:]) >> 16)


def _unpack_bf16_pairs(p):
    hi = lax.bitcast_convert_type(p & jnp.uint32(0xFFFF0000), jnp.float32)
    lo = lax.bitcast_convert_type(p << 16, jnp.float32)
    return hi, lo


def _layer_norm(x, g, b):
    mu = jnp.mean(x, axis=-1, keepdims=True)
    xc = x - mu
    var = jnp.mean(xc * xc, axis=-1, keepdims=True)
    return xc * lax.rsqrt(var + LN_EPS) * g + b


def _moe_residual(x_ref, g_ref, wt_ref):
    w0, w1 = wt_ref[:, 0:1], wt_ref[:, 1:2]
    his, los = [], []
    for j in range(NPIECE):
        hi0, lo0 = _unpack_bf16_pairs(lax.bitcast_convert_type(g_ref[0, j], jnp.uint32))
        hi1, lo1 = _unpack_bf16_pairs(lax.bitcast_convert_type(g_ref[1, j], jnp.uint32))
        his.append(w0 * hi0 + w1 * hi1)
        los.append(w0 * lo0 + w1 * lo1)
    return x_ref[...] + jnp.concatenate(his + los, axis=1)


def _mixer_kernel(seq_tiles, fused_combine, xp_ref, xc_ref, xn_ref, *refs):
    if fused_combine:
        gp_ref, gc_ref, gn_ref, wtp_ref, wtc_ref, wtn_ref = refs[:6]
        refs = refs[6:]
        x_prev = lambda: _moe_residual(xp_ref, gp_ref, wtp_ref)
        x_cur = lambda: _moe_residual(xc_ref, gc_ref, wtc_ref)
        x_next = lambda: _moe_residual(xn_ref, gn_ref, wtn_ref)
    else:
        x_prev, x_cur, x_next = (lambda: xp_ref[...]), (lambda: xc_ref[...]), (lambda: xn_ref[...])
    (nm_ref, win_ref, ca_ref, wa_ref, cb_ref, lnbg_ref, lnbb_ref, wb_ref, lncg_ref, lncb_ref,
     ws_ref, bsb_ref, wc_ref, wo_ref, nf_ref, wrh_ref, wrl_ref, br_ref, sh_ref,
     xo_ref, h2_ref, idx_ref, wgt_ref, cnt_ref,
     hb_ref, zq_ref, cva_ref, cvb_ref, cvn_ref, vst_ref, mg_ref, carry_ref) = refs
    i = pl.program_id(0)
    at_start = (i % seq_tiles) == 0
    at_end = (i % seq_tiles) == seq_tiles - 1

    def _rms(xv):
        ms = jnp.mean(xv * xv, axis=-1, keepdims=True)
        return (xv * lax.rsqrt(ms + RMS_EPS) * nm_ref[...]).astype(jnp.bfloat16)

    hb_ref[0:HALO, :] = _rms(x_prev())
    hb_ref[HALO:HALO + TM, :] = _rms(x_cur())
    hb_ref[HALO + TM:TE, :] = _rms(x_next())

    rows = lax.broadcasted_iota(jnp.int32, (TE, 1), 0)
    lo = jnp.where(at_start, HALO, 0)
    hi = jnp.where(at_end, HALO + TM, TE)
    valid = jnp.logical_and(rows >= lo, rows < hi)

    def proj(r0, r1, c0, width):
        return jnp.dot(hb_ref[r0:r1, :], win_ref[:, c0:c0 + width],
                       preferred_element_type=jnp.float32)

    def b_proj(c):
        c0 = c * CW
        val = proj(0, TE, C_VB + c0, CW)
        gate = proj(0, TE, C_GB + c0, CW)
        zq_ref[c] = jnp.where(valid, val * _sigmoid(gate), 0.0)

    def a_proj(c):
        c0 = c * CW
        xa = proj(0, TE, C_XA + c0, CW)
        cc = proj(0, TE, C_CA + c0, CW)
        t = jnp.where(valid, xa * cc, 0.0)
        conv = (ca_ref[0:1, c0:c0 + CW] * t[HALO - 1:HALO - 1 + TM]
                + ca_ref[1:2, c0:c0 + CW] * t[HALO:HALO + TM]
                + ca_ref[2:3, c0:c0 + CW] * t[HALO + 1:HALO + 1 + TM])
        ba = proj(HALO, HALO + TM, C_BA + c0, CW)
        cva_ref[:, c0:c0 + CW] = (ba * conv).astype(jnp.bfloat16)

    def c_v_proj(c):
        c0 = c * CW
        vst_ref[:, c0:c0 + CW] = _gelu_tanh(proj(HALO, HALO + TM, C_UV + D + c0, CW))

    def b_conv(c, row_blocks):
        c0 = c * CW
        zq = zq_ref.at[c]
        for rb in row_blocks:
            r0 = rb * RB
            for lt in range(CW // 128):
                l0 = lt * 128
                parts = []
                for s in range(8):
                    part = None
                    for q in range(4):
                        k = 8 * q + s - (HALO - CONV_B // 2)
                        if 0 <= k < CONV_B:
                            term = (cb_ref[k:k + 1, c0 + l0:c0 + l0 + 128]
                                    * zq[r0 + 8 * q:r0 + 8 * q + PR, l0:l0 + 128])
                            part = term if part is None else part + term
                    parts.append(part)
                if SK > 8 * PR:
                    parts.append(jnp.zeros((SK - 8 * PR, 128), jnp.float32))
                stacked = jnp.concatenate(parts, axis=0).astype(jnp.bfloat16)
                xo_ref[r0:r0 + RB, c0 + l0:c0 + l0 + 128] = jnp.dot(
                    sh_ref[...], stacked, preferred_element_type=jnp.float32)

    n_chunks = D // CW
    n_rb = TM // RB
    rb_step = n_rb // n_chunks
    late_rows = lambda c: range(c * rb_step, (c + 1) * rb_step)
    b_proj(0)
    for c in range(n_chunks):
        if c + 1 < n_chunks:
            b_proj(c + 1)
        a_proj(c)
        c_v_proj(c)
    b_conv(0, range(n_rb))

    vn = _layer_norm(vst_ref[...], lncg_ref[...], lncb_ref[...])
    cvn_ref[...] = vn.astype(jnp.bfloat16)
    for c in range(n_chunks):
        c0 = c * CW
        ya = jnp.dot(cva_ref[...], wa_ref[:, c0:c0 + CW], preferred_element_type=jnp.float32)
        ga = _sigmoid(proj(HALO, HALO + TM, C_G + c0, CW))
        mg_ref[:, c0:c0 + CW] = ga * ya
        b_conv(1, late_rows(c))

    for c in range(n_chunks):
        c0 = c * CW
        u = _gelu_tanh(proj(HALO, HALO + TM, C_UV + c0, CW))
        for n in range(TM // CHUNK):
            for hh in range(CW // CHUNK):
                h = c * (CW // CHUNK) + hh
                sv = jnp.dot(ws_ref[h], cvn_ref[n * CHUNK:(n + 1) * CHUNK, h * CHUNK:(h + 1) * CHUNK],
                             preferred_element_type=jnp.float32) + bsb_ref[h]
                cva_ref[n * CHUNK:(n + 1) * CHUNK, h * CHUNK:(h + 1) * CHUNK] = (
                    u[n * CHUNK:(n + 1) * CHUNK, hh * CHUNK:(hh + 1) * CHUNK] * sv
                ).astype(jnp.bfloat16)
        b_conv(2, late_rows(c))
    for c in range(n_chunks):
        c0 = c * CW
        yc = jnp.dot(cva_ref[...], wc_ref[:, c0:c0 + CW], preferred_element_type=jnp.float32)
        gc = _sigmoid(proj(HALO, HALO + TM, C_G + 2 * D + c0, CW))
        mg_ref[:, c0:c0 + CW] += gc * yc
        b_conv(3, late_rows(c))

    zc = _layer_norm(xo_ref[...], lnbg_ref[...], lnbb_ref[...])
    cvb_ref[...] = (zc * _sigmoid(zc)).astype(jnp.bfloat16)
    for c in range(n_chunks):
        c0 = c * CW
        yb = jnp.dot(cvb_ref[...], wb_ref[:, c0:c0 + CW], preferred_element_type=jnp.float32)
        gb = _sigmoid(proj(HALO, HALO + TM, C_G + D + c0, CW))
        mg_ref[:, c0:c0 + CW] += gb * yb

    cvb_ref[...] = mg_ref[...].astype(jnp.bfloat16)
    x_res = x_cur()
    for c in range(n_chunks):
        c0 = c * CW
        xo_ref[:, c0:c0 + CW] = x_res[:, c0:c0 + CW] + jnp.dot(
            cvb_ref[...], wo_ref[:, c0:c0 + CW], preferred_element_type=jnp.float32)
    xnew = xo_ref[...]
    ms = jnp.mean(xnew * xnew, axis=-1, keepdims=True)
    h2 = xnew * lax.rsqrt(ms + RMS_EPS) * nf_ref[...]
    h2_packed = _pack_bf16_pairs(h2)
    for j in range(NPIECE):
        h2_ref[j] = lax.bitcast_convert_type(h2_packed[:, j * 128:(j + 1) * 128], jnp.int32)
    h_hi = h2.astype(jnp.bfloat16)
    h_lo = (h2 - h_hi.astype(jnp.float32)).astype(jnp.bfloat16)
    dot32 = functools.partial(jnp.dot, preferred_element_type=jnp.float32)
    logits = (dot32(h_hi, wrh_ref[...]) + dot32(h_lo, wrh_ref[...])
              + dot32(h_hi, wrl_ref[...]) + dot32(h_lo, wrl_ref[...]) + br_ref[...])
    _route_tile(logits, idx_ref, wgt_ref, cnt_ref, carry_ref)


def _layer_spec(layer, shape):
    nd = len(shape)
    return pl.BlockSpec((None,) + tuple(shape), lambda i, _n=nd: (layer,) + (0,) * _n,
                        pipeline_mode=pl.Buffered(1))


def _mixer(layer, x2, pending, seq_len, nm, win, ca, wa, cb, lnbg, lnbb, wb, lncg, lncb, ws, bsb, wc, wo,
           nf, wrh, wrl, br, sh):
    T = x2.shape[0]
    n_tiles = T // TM
    hb = TM // HALO
    last_halo = T // HALO - 1
    ls = functools.partial(_layer_spec, layer)
    prev_rows = lambda i: jnp.maximum(i * hb - 1, 0)
    next_rows = lambda i: jnp.minimum((i + 1) * hb, last_halo)
    in_specs = [
        pl.BlockSpec((HALO, D), lambda i: (prev_rows(i), 0)),
        pl.BlockSpec((TM, D), lambda i: (i, 0)),
        pl.BlockSpec((HALO, D), lambda i: (next_rows(i), 0)),
    ]
    operands = [x2, x2, x2]
    if pending is not None:
        gathered, wgt_rows = pending
        in_specs += [
            pl.BlockSpec((2, NPIECE, HALO, 128), lambda i: (0, 0, prev_rows(i), 0)),
            pl.BlockSpec((2, NPIECE, TM, 128), lambda i: (0, 0, i, 0)),
            pl.BlockSpec((2, NPIECE, HALO, 128), lambda i: (0, 0, next_rows(i), 0)),
            pl.BlockSpec((HALO, 8), lambda i: (prev_rows(i), 0)),
            pl.BlockSpec((TM, 8), lambda i: (i, 0)),
            pl.BlockSpec((HALO, 8), lambda i: (next_rows(i), 0)),
        ]
        operands += [gathered, gathered, gathered, wgt_rows, wgt_rows, wgt_rows]
    in_specs += [
        ls((1, D)), ls((D, D_IN)), ls((CONV_A, D)), ls((D, D)),
        ls((CONV_B, D)), ls((1, D)), ls((1, D)), ls((D, D)),
        ls((1, D)), ls((1, D)), ls((N_HEADS_C, CHUNK, CHUNK)),
        ls((N_HEADS_C, CHUNK, CHUNK)), ls((D, D)), ls((D, D)),
        ls((1, D)), ls((D, NLOG)), ls((D, NLOG)), ls((1, NLOG)),
        pl.BlockSpec((RB, SK), lambda i: (0, 0), pipeline_mode=pl.Buffered(1)),
    ]
    out_specs = [
        pl.BlockSpec((TM, D), lambda i: (i, 0)),
        pl.BlockSpec((NPIECE, TM, 128), lambda i: (0, i, 0)),
        pl.BlockSpec((8, TM), lambda i: (0, i)),
        pl.BlockSpec((8, TM), lambda i: (0, i)),
        pl.BlockSpec((N_EXPERTS, 128), lambda i: (0, 0)),
    ]
    return pl.pallas_call(
        functools.partial(_mixer_kernel, seq_len // TM, pending is not None),
        grid=(n_tiles,),
        in_specs=in_specs,
        out_specs=out_specs,
        out_shape=[jax.ShapeDtypeStruct((T, D), jnp.float32),
                   jax.ShapeDtypeStruct((NPIECE, T, 128), jnp.int32),
                   jax.ShapeDtypeStruct((8, T), jnp.int32),
                   jax.ShapeDtypeStruct((8, T), jnp.float32),
                   jax.ShapeDtypeStruct((N_EXPERTS, 128), jnp.float32)],
        scratch_shapes=[pltpu.VMEM((TE, D), jnp.bfloat16),
                        pltpu.VMEM((D // CW, TE, CW), jnp.float32),
                        pltpu.VMEM((TM, D), jnp.bfloat16),
                        pltpu.VMEM((TM, D), jnp.bfloat16),
                        pltpu.VMEM((TM, D), jnp.bfloat16),
                        pltpu.VMEM((TM, D), jnp.float32),
                        pltpu.VMEM((TM, D), jnp.float32),
                        pltpu.VMEM((N_EXPERTS, 128), jnp.float32)],
        compiler_params=pltpu.CompilerParams(dimension_semantics=("arbitrary",),
                                             vmem_limit_bytes=VMEM_LIMIT),
        name="mixer",
    )(*operands, nm, win, ca, wa, cb, lnbg, lnbb, wb, lncg, lncb, ws, bsb, wc, wo, nf, wrh, wrl, br, sh)


def _route_tile(lg, idx_ref, wgt_ref, cnt_ref, carry_ref):
    i = pl.program_id(0)

    @pl.when(i == 0)
    def _():
        carry_ref[...] = jnp.zeros_like(carry_ref)

    lt = lg.T
    g = [lt[j:j + 1, :] for j in range(N_GROUPS)]
    gmax = jnp.maximum(jnp.maximum(g[0], g[1]), jnp.maximum(g[2], g[3]))
    gidx = jnp.where(g[0] == gmax, 0.0, jnp.where(g[1] == gmax, 1.0, jnp.where(g[2] == gmax, 2.0, 3.0)))
    gsum = sum(jnp.exp(gj - gmax) for gj in g)
    g_p = 1.0 / gsum

    sel = lt[E_OFF + 3 * EPG:E_OFF + 4 * EPG, :]
    for j in (2, 1, 0):
        sel = jnp.where(gidx == float(j), lt[E_OFF + j * EPG:E_OFF + (j + 1) * EPG, :], sel)
    rid = lax.broadcasted_iota(jnp.int32, (EPG, TR), 0).astype(jnp.float32)
    m1 = jnp.max(sel, axis=0, keepdims=True)
    i1 = jnp.min(jnp.where(sel == m1, rid, float(EPG)), axis=0, keepdims=True)
    rest = jnp.where(rid == i1, -jnp.inf, sel)
    m2 = jnp.max(rest, axis=0, keepdims=True)
    i2 = jnp.min(jnp.where(rest == m2, rid, float(EPG)), axis=0, keepdims=True)
    e2x = jnp.exp(m2 - m1)
    den = 1.0 + e2x
    w1 = (1.0 / den) * g_p
    w2 = (e2x / den) * g_p
    e1 = gidx * float(EPG) + i1
    e2 = gidx * float(EPG) + i2

    eid = lax.broadcasted_iota(jnp.int32, (N_EXPERTS, TR), 0).astype(jnp.float32)
    oh1 = (eid == e1).astype(jnp.float32)
    oh2 = (eid == e2).astype(jnp.float32)
    oh = oh1 + oh2
    tr = lax.broadcasted_iota(jnp.int32, (TR, TR), 0)
    tc = lax.broadcasted_iota(jnp.int32, (TR, TR), 1)
    upper = (tr < tc).astype(jnp.bfloat16)
    before = jnp.dot(oh.astype(jnp.bfloat16), upper, preferred_element_type=jnp.float32)
    base = before + carry_ref[:, 0:1]
    r1 = jnp.sum(oh1 * base, axis=0, keepdims=True)
    r2 = jnp.sum(oh2 * base, axis=0, keepdims=True)
    carry_ref[...] = carry_ref[...] + jnp.sum(oh, axis=1, keepdims=True)

    idx_ref[...] = jnp.zeros_like(idx_ref)
    idx_ref[0:1, :] = e1.astype(jnp.int32)
    idx_ref[1:2, :] = e2.astype(jnp.int32)
    idx_ref[2:3, :] = r1.astype(jnp.int32)
    idx_ref[3:4, :] = r2.astype(jnp.int32)
    wgt_ref[...] = jnp.zeros_like(wgt_ref)
    wgt_ref[0:1, :] = w1
    wgt_ref[1:2, :] = w2
    cnt_ref[...] = carry_ref[...]


SC_WINDOW = 128


def _sc_mesh():
    return plsc.VectorSubcoreMesh(core_axis_name="c", subcore_axis_name="s")


def _sc_scatter_rows(rows, idx, passes, n_out):
    n_src = rows.shape[0]
    src_blocks = n_src // SC_WINDOW

    @functools.partial(
        pl.kernel, mesh=_sc_mesh(),
        out_type=jax.ShapeDtypeStruct((n_out, 128), rows.dtype))
    def scatter(rows_hbm, idx_hbm, out_hbm):
        def body(rows_vmem, idx_vmem):
            pltpu.sync_copy(rows_vmem, out_hbm.at[idx_vmem.at[0]])

        pltpu.emit_pipeline(
            body,
            grid=(passes * src_blocks,),
            in_specs=[pl.BlockSpec((SC_WINDOW, 128), lambda i: (i % src_blocks, 0)),
                      pl.BlockSpec((1, SC_WINDOW), lambda i: (0, i))],
            out_specs=[],
            core_axis_name=("c", "s"),
            dimension_semantics=(pltpu.PARALLEL,),
        )(rows_hbm, idx_hbm)

    return scatter(rows, idx.reshape(1, passes * n_src))


def _expert_kernel(layer, be_ref, nu_ref, nxt_ref, gs_ref, nv_ref, xb_ref, w1_ref, w3_ref, w2_ref, yb_ref,
                   wf1_ref, wf3_ref, wf2_ref, w1b_ref, w3b_ref, w2b_ref, sem):
    b = pl.program_id(0)
    used = b < nu_ref[0]
    new_expert = jnp.logical_or(b == 0, be_ref[b] != be_ref[jnp.maximum(b - 1, 0)])

    def weight_copies(e, s):
        return [pltpu.make_async_copy(src.at[layer, e], dst.at[s], sem.at[s])
                for src, dst in ((w1_ref, wf1_ref), (w3_ref, wf3_ref), (w2_ref, wf2_ref))]

    @pl.when(b == 0)
    def _():
        for cp in weight_copies(be_ref[0], 0):
            cp.start(priority=1)

    @pl.when(jnp.logical_and(used, new_expert))
    def _():
        s = gs_ref[b]
        for cp in weight_copies(be_ref[b], s):
            cp.wait()

        @pl.when(nxt_ref[b] >= 0)
        def _():
            for cp in weight_copies(nxt_ref[b], 1 - s):
                cp.start(priority=1)

        w1b_ref[...] = wf1_ref[s].astype(jnp.bfloat16)
        w3b_ref[...] = wf3_ref[s].astype(jnp.bfloat16)
        w2b_ref[...] = wf2_ref[s].astype(jnp.bfloat16)

    def expert_mlp(rows):
        live = lax.broadcasted_iota(jnp.int32, (rows, 1), 0) < nv_ref[b]
        packed_in = jnp.concatenate(
            [jnp.where(live, lax.bitcast_convert_type(xb_ref[j, 0:rows], jnp.uint32), jnp.uint32(0))
             for j in range(NPIECE)], axis=1)
        x = jnp.concatenate(_unpack_bf16_pairs(packed_in), axis=1).astype(jnp.bfloat16)
        dot32 = functools.partial(jnp.dot, preferred_element_type=jnp.float32)
        a = dot32(x, w1b_ref[...])
        g = dot32(x, w3b_ref[...])
        hmid = (a * _sigmoid(a) * g).astype(jnp.bfloat16)
        packed = _pack_bf16_pairs(dot32(hmid, w2b_ref[...]))
        for j in range(NPIECE):
            yb_ref[j, 0:rows] = lax.bitcast_convert_type(packed[:, j * 128:(j + 1) * 128], jnp.int32)

    half = BM // 2
    half_full = nv_ref[b] <= half

    @pl.when(jnp.logical_and(used, jnp.logical_not(half_full)))
    def _():
        expert_mlp(BM)

    @pl.when(jnp.logical_and(used, half_full))
    def _():
        expert_mlp(half)
        yb_ref[:, half:BM, :] = jnp.zeros((NPIECE, BM - half, 128), jnp.int32)

    @pl.when(jnp.logical_not(used))
    def _():
        yb_ref[...] = jnp.zeros_like(yb_ref)


def _experts(layer, block_expert, n_used, next_expert, group_slot, block_valid, xb, w1, w3, w2):
    n_slots = xb.shape[1]
    n_blocks = n_slots // BM

    def row_map(b, be, nu, nxt, gs, nv):
        return (0, jnp.minimum(b, nu[0] - 1), 0)

    def out_map(b, be, nu, nxt, gs, nv):
        return (0, b, 0)

    grid_spec = pltpu.PrefetchScalarGridSpec(
        num_scalar_prefetch=5,
        grid=(n_blocks,),
        in_specs=[pl.BlockSpec((NPIECE, BM, 128), row_map),
                  pl.BlockSpec(memory_space=pl.ANY),
                  pl.BlockSpec(memory_space=pl.ANY),
                  pl.BlockSpec(memory_space=pl.ANY)],
        out_specs=pl.BlockSpec((NPIECE, BM, 128), out_map),
        scratch_shapes=[pltpu.VMEM((2, D, D_FF), jnp.float32),
                        pltpu.VMEM((2, D, D_FF), jnp.float32),
                        pltpu.VMEM((2, D_FF, D), jnp.float32),
                        pltpu.VMEM((D, D_FF), jnp.bfloat16),
                        pltpu.VMEM((D, D_FF), jnp.bfloat16),
                        pltpu.VMEM((D_FF, D), jnp.bfloat16),
                        pltpu.SemaphoreType.DMA((2,))],
    )
    return pl.pallas_call(
        functools.partial(_expert_kernel, layer),
        grid_spec=grid_spec,
        out_shape=jax.ShapeDtypeStruct((NPIECE, n_slots, 128), jnp.int32),
        compiler_params=pltpu.CompilerParams(dimension_semantics=("arbitrary",),
                                             vmem_limit_bytes=VMEM_LIMIT),
        name="experts",
    )(block_expert, n_used, next_expert, group_slot, block_valid, xb, w1, w3, w2)


def _sc_gather_rows(table, idx):
    n_rows = idx.shape[0]

    @functools.partial(
        pl.kernel, mesh=_sc_mesh(),
        out_type=jax.ShapeDtypeStruct((n_rows, 128), table.dtype))
    def gather(table_hbm, idx_hbm, out_hbm):
        def body(idx_vmem, out_vmem):
            pltpu.sync_copy(table_hbm.at[idx_vmem.at[0]], out_vmem)

        pltpu.emit_pipeline(
            body,
            grid=(n_rows // SC_WINDOW,),
            in_specs=[pl.BlockSpec((1, SC_WINDOW), lambda i: (0, i))],
            out_specs=[pl.BlockSpec((SC_WINDOW, 128), lambda i: (i, 0))],
            core_axis_name=("c", "s"),
            dimension_semantics=(pltpu.PARALLEL,),
        )(idx_hbm, out_hbm)

    return gather(table, idx.reshape(1, n_rows))


def _combine_kernel(final, g_ref, x_ref, w_ref, nrm_ref, o_ref):
    w = w_ref[...].T
    ys = {}
    for j in range(NPIECE):
        hi0, lo0 = _unpack_bf16_pairs(lax.bitcast_convert_type(g_ref[0, j], jnp.uint32))
        hi1, lo1 = _unpack_bf16_pairs(lax.bitcast_convert_type(g_ref[1, j], jnp.uint32))
        c_hi, c_lo = j * 128, DP + j * 128
        ys[c_hi] = x_ref[:, c_hi:c_hi + 128] + w[:, 0:1] * hi0 + w[:, 1:2] * hi1
        ys[c_lo] = x_ref[:, c_lo:c_lo + 128] + w[:, 0:1] * lo0 + w[:, 1:2] * lo1
    if final:
        ms = sum(jnp.sum(y * y, axis=-1, keepdims=True) for y in ys.values()) * (1.0 / D)
        scale = lax.rsqrt(ms + RMS_EPS)
        ys = {c: y * scale * nrm_ref[:, c:c + 128] for c, y in ys.items()}
    for c, y in ys.items():
        o_ref[:, c:c + 128] = y


def _combine(gathered, x2, wgt, nrm, final):
    T = x2.shape[0]
    return pl.pallas_call(
        functools.partial(_combine_kernel, final),
        grid=(T // TD,),
        in_specs=[pl.BlockSpec((2, NPIECE, TD, 128), lambda i: (0, 0, i, 0)),
                  pl.BlockSpec((TD, D), lambda i: (i, 0)),
                  pl.BlockSpec((8, TD), lambda i: (0, i)),
                  pl.BlockSpec((1, D), lambda i: (0, 0))],
        out_specs=pl.BlockSpec((TD, D), lambda i: (i, 0)),
        out_shape=jax.ShapeDtypeStruct((T, D), jnp.float32),
        compiler_params=pltpu.CompilerParams(dimension_semantics=("arbitrary",)),
        name="combine",
    )(gathered, x2, wgt, nrm)


def _moe(layer, h2, idx, cnt, w1, w3, w2):
    T = idx.shape[1]
    counts = cnt[:, 0].astype(jnp.int32)
    padded = (counts + BM - 1) // BM * BM
    padded_end = jnp.cumsum(padded)
    padded_start = padded_end - padded
    eids = jnp.arange(N_EXPERTS, dtype=jnp.int32)[:, None]
    start_of = lambda e: jnp.sum(jnp.where(e[None, :] == eids, padded_start[:, None], 0), axis=0)
    dest = jnp.stack([start_of(idx[0]) + idx[2], start_of(idx[1]) + idx[3]])
    n_blocks = (2 * T) // BM + N_EXPERTS
    block_start = jnp.arange(n_blocks, dtype=jnp.int32) * BM
    block_expert = jnp.minimum(
        jnp.sum((padded_end[None, :] <= block_start[:, None]).astype(jnp.int32), axis=1),
        N_EXPERTS - 1)
    n_used = (padded_end[-1] // BM).reshape(1)
    later = jnp.logical_and(eids.T > eids, (counts > 0)[None, :])
    next_of = jnp.min(jnp.where(later, eids.T, N_EXPERTS), axis=1)
    next_of = jnp.where(next_of < N_EXPERTS, next_of, -1)
    pick = lambda table: jnp.sum(
        jnp.where(block_expert[:, None] == eids.T, table[None, :], 0), axis=1)
    next_expert = pick(next_of)
    group_slot = pick(jnp.cumsum((counts > 0).astype(jnp.int32)) - 1) % 2
    block_valid = jnp.clip(pick(padded_start + counts) - block_start, 0, BM)
    n_slots = n_blocks * BM
    piece_base = (jnp.arange(NPIECE, dtype=jnp.int32) * n_slots)[None, :, None]
    rows = (dest[:, None, :] + piece_base).reshape(-1)
    xb = _sc_scatter_rows(h2.reshape(NPIECE * T, 128), rows, 2, NPIECE * n_slots)
    yb = _experts(layer, block_expert, n_used, next_expert, group_slot, block_valid,
                  xb.reshape(NPIECE, n_slots, 128), w1, w3, w2)
    gathered = _sc_gather_rows(yb.reshape(NPIECE * n_slots, 128), rows)
    return gathered.reshape(2, NPIECE, T, 128)


def kernel(x, norm_mix, w_in, conv_a, w_a_out, conv_b, ln_b_g, ln_b_b, w_b_out, ln_c_g, ln_c_b,
           w_s, b_s, w_c_out, w_o, norm_ffn, w_group, b_group, w_router, b_router, w1, w3, w2,
           norm_final):
    bsz, seq, d = x.shape
    depth = norm_mix.shape[0]
    bf = jnp.bfloat16
    x2 = x.reshape(bsz * seq, d)
    wr = jnp.zeros((depth, d, NLOG), jnp.float32)
    wr = wr.at[:, :, 0:N_GROUPS].set(w_group).at[:, :, E_OFF:E_OFF + N_EXPERTS].set(w_router)
    wr_hi = wr.astype(bf)
    wr_lo = (wr - wr_hi.astype(jnp.float32)).astype(bf)
    br = jnp.zeros((depth, 1, NLOG), jnp.float32)
    br = br.at[:, 0, 0:N_GROUPS].set(b_group).at[:, 0, E_OFF:E_OFF + N_EXPERTS].set(b_router)
    bsb = jnp.broadcast_to(b_s[:, :, :, None], (depth, N_HEADS_C, CHUNK, CHUNK))
    row = lambda p: p[:, None, :]
    mixer_params = (row(norm_mix), w_in.astype(bf), conv_a, w_a_out.astype(bf), conv_b,
                    row(ln_b_g), row(ln_b_b), w_b_out.astype(bf), row(ln_c_g), row(ln_c_b),
                    w_s.astype(bf), bsb, w_c_out.astype(bf), w_o.astype(bf), row(norm_ffn), wr_hi, wr_lo, br)
    col = jnp.arange(SK, dtype=jnp.int32)[None, :]
    shift_sum = jnp.logical_and(
        col % PR == jnp.arange(RB, dtype=jnp.int32)[:, None] + col // PR, col < 8 * PR).astype(bf)
    pending = None
    for l in range(depth):
        x2, h2, idx, wgt, cnt = _mixer(l, x2, pending, seq, *mixer_params, shift_sum)
        gathered = _moe(l, h2, idx, cnt, w1, w3, w2)
        pending = (gathered, wgt.T)
    out = _combine(gathered, x2, wgt, norm_final[None], True)
    return out.reshape(bsz, seq, d)
```

```python
import functools

import jax
import jax.numpy as jnp
from jax import lax
from jax.experimental import pallas as pl
from jax.experimental.pallas import tpu as pltpu
from jax.experimental.pallas import tpu_sc as plsc

D = 1024
DP = D // 2
NPIECE = DP // 128
N_HEADS_C = 8
CHUNK = 128
CONV_A = 3
CONV_B = 31
N_GROUPS = 4
EPG = 8
N_EXPERTS = N_GROUPS * EPG
D_FF = 512
RMS_EPS = 1e-6
LN_EPS = 1e-5

C_XA, C_BA, C_CA, C_VB, C_GB, C_UV, C_G = 0, 1024, 2048, 3072, 4096, 5120, 7168
D_IN = 10240

TM = 512
HALO = 16
TE = TM + 2 * HALO
CW = 256
RB = 64
PR = RB + 8
SK = -(-8 * PR // 128) * 128
NLOG = 128
E_OFF = 8

TR = TM
BM = 512
TD = 512

VMEM_LIMIT = 60 * 1024 * 1024


def _sigmoid(x):
    return 0.5 * (jnp.tanh(0.5 * x) + 1.0)


def _gelu_tanh(x):
    return 0.5 * x * (1.0 + jnp.tanh(0.7978845608028654 * (x + 0.044715 * (x * x * x))))


def _pack_bf16_pairs(x):
    c = x.shape[1] // 2
    as_bits = lambda v: lax.bitcast_convert_type(v.astype(jnp.bfloat16).astype(jnp.float32), jnp.uint32)
    return as_bits(x[:, :c]) | (as_bits(x[:, c:]) >> 16)


def _unpack_bf16_pairs(p):
    hi = lax.bitcast_convert_type(p & jnp.uint32(0xFFFF0000), jnp.float32)
    lo = lax.bitcast_convert_type(p << 16, jnp.float32)
    return hi, lo


def _layer_norm(x, g, b):
    mu = jnp.mean(x, axis=-1, keepdims=True)
    xc = x - mu
    var = jnp.mean(xc * xc, axis=-1, keepdims=True)
    return xc * lax.rsqrt(var + LN_EPS) * g + b


def _moe_residual(x_ref, g_ref, wt_ref):
    w0, w1 = wt_ref[:, 0:1], wt_ref[:, 1:2]
    his, los = [], []
    for j in range(NPIECE):
        hi0, lo0 = _unpack_bf16_pairs(lax.bitcast_convert_type(g_ref[0, j], jnp.uint32))
        hi1, lo1 = _unpack_bf16_pairs(lax.bitcast_convert_type(g_ref[1, j], jnp.uint32))
        his.append(w0 * hi0 + w1 * hi1)
        los.append(w0 * lo0 + w1 * lo1)
    return x_ref[...] + jnp.concatenate(his + los, axis=1)


def _mixer_kernel(seq_tiles, fused_combine, xp_ref, xc_ref, xn_ref, *refs):
    if fused_combine:
        gp_ref, gc_ref, gn_ref, wtp_ref, wtc_ref, wtn_ref = refs[:6]
        refs = refs[6:]
        x_prev = lambda: _moe_residual(xp_ref, gp_ref, wtp_ref)
        x_cur = lambda: _moe_residual(xc_ref, gc_ref, wtc_ref)
        x_next = lambda: _moe_residual(xn_ref, gn_ref, wtn_ref)
    else:
        x_prev, x_cur, x_next = (lambda: xp_ref[...]), (lambda: xc_ref[...]), (lambda: xn_ref[...])
    (nm_ref, win_ref, ca_ref, wa_ref, cb_ref, lnbg_ref, lnbb_ref, wb_ref, lncg_ref, lncb_ref,
     ws_ref, bsb_ref, wc_ref, wo_ref, nf_ref, wrh_ref, wrl_ref, br_ref, sh_ref,
     xo_ref, h2_ref, idx_ref, wgt_ref, cnt_ref,
     hb_ref, zq_ref, cva_ref, cvb_ref, cvn_ref, vst_ref, mg_ref, carry_ref) = refs
    i = pl.program_id(0)
    at_start = (i % seq_tiles) == 0
    at_end = (i % seq_tiles) == seq_tiles - 1

    def _rms(xv):
        ms = jnp.mean(xv * xv, axis=-1, keepdims=True)
        return (xv * lax.rsqrt(ms + RMS_EPS) * nm_ref[...]).astype(jnp.bfloat16)

    hb_ref[0:HALO, :] = _rms(x_prev())
    hb_ref[HALO:HALO + TM, :] = _rms(x_cur())
    hb_ref[HALO + TM:TE, :] = _rms(x_next())

    rows = lax.broadcasted_iota(jnp.int32, (TE, 1), 0)
    lo = jnp.where(at_start, HALO, 0)
    hi = jnp.where(at_end, HALO + TM, TE)
    valid = jnp.logical_and(rows >= lo, rows < hi)

    def proj(r0, r1, c0, width):
        return jnp.dot(hb_ref[r0:r1, :], win_ref[:, c0:c0 + width],
                       preferred_element_type=jnp.float32)

    def b_proj(c):
        c0 = c * CW
        val = proj(0, TE, C_VB + c0, CW)
        gate = proj(0, TE, C_GB + c0, CW)
        zq_ref[c] = jnp.where(valid, val * _sigmoid(gate), 0.0)

    def a_proj(c):
        c0 = c * CW
        xa = proj(0, TE, C_XA + c0, CW)
        cc = proj(0, TE, C_CA + c0, CW)
        t = jnp.where(valid, xa * cc, 0.0)
        conv = (ca_ref[0:1, c0:c0 + CW] * t[HALO - 1:HALO - 1 + TM]
                + ca_ref[1:2, c0:c0 + CW] * t[HALO:HALO + TM]
                + ca_ref[2:3, c0:c0 + CW] * t[HALO + 1:HALO + 1 + TM])
        ba = proj(HALO, HALO + TM, C_BA + c0, CW)
        cva_ref[:, c0:c0 + CW] = (ba * conv).astype(jnp.bfloat16)

    def c_v_proj(c):
        c0 = c * CW
        vst_ref[:, c0:c0 + CW] = _gelu_tanh(proj(HALO, HALO + TM, C_UV + D + c0, CW))

    def b_conv(c, row_blocks):
        c0 = c * CW
        zq = zq_ref.at[c]
        for rb in row_blocks:
            r0 = rb * RB
            for lt in range(CW // 128):
                l0 = lt * 128
                parts = []
                for s in range(8):
                    part = None
                    for q in range(4):
                        k = 8 * q + s - (HALO - CONV_B // 2)
                        if 0 <= k < CONV_B:
                            term = (cb_ref[k:k + 1, c0 + l0:c0 + l0 + 128]
                                    * zq[r0 + 8 * q:r0 + 8 * q + PR, l0:l0 + 128])
                            part = term if part is None else part + term
                    parts.append(part)
                if SK > 8 * PR:
                    parts.append(jnp.zeros((SK - 8 * PR, 128), jnp.float32))
                stacked = jnp.concatenate(parts, axis=0).astype(jnp.bfloat16)
                xo_ref[r0:r0 + RB, c0 + l0:c0 + l0 + 128] = jnp.dot(
                    sh_ref[...], stacked, preferred_element_type=jnp.float32)

    n_chunks = D // CW
    n_rb = TM // RB
    rb_step = n_rb // n_chunks
    late_rows = lambda c: range(c * rb_step, (c + 1) * rb_step)
    b_proj(0)
    for c in range(n_chunks):
        if c + 1 < n_chunks:
            b_proj(c + 1)
        a_proj(c)
        c_v_proj(c)
        b_conv(0, late_rows(c))
        b_conv(1, late_rows(c))

    vn = _layer_norm(vst_ref[...], lncg_ref[...], lncb_ref[...])
    cvn_ref[...] = vn.astype(jnp.bfloat16)
    for c in range(n_chunks):
        c0 = c * CW
        ya = jnp.dot(cva_ref[...], wa_ref[:, c0:c0 + CW], preferred_element_type=jnp.float32)
        ga = _sigmoid(proj(HALO, HALO + TM, C_G + c0, CW))
        mg_ref[:, c0:c0 + CW] = ga * ya
        b_conv(2, late_rows(c))

    for c in range(n_chunks):
        c0 = c * CW
        u = _gelu_tanh(proj(HALO, HALO + TM, C_UV + c0, CW))
        for n in range(TM // CHUNK):
            for hh in range(CW // CHUNK):
                h = c * (CW // CHUNK) + hh
                sv = jnp.dot(ws_ref[h], cvn_ref[n * CHUNK:(n + 1) * CHUNK, h * CHUNK:(h + 1) * CHUNK],
                             preferred_element_type=jnp.float32) + bsb_ref[h]
                cva_ref[n * CHUNK:(n + 1) * CHUNK, h * CHUNK:(h + 1) * CHUNK] = (
                    u[n * CHUNK:(n + 1) * CHUNK, hh * CHUNK:(hh + 1) * CHUNK] * sv
                ).astype(jnp.bfloat16)
        b_conv(3, late_rows(c))
    for c in range(n_chunks):
        c0 = c * CW
        yc = jnp.dot(cva_ref[...], wc_ref[:, c0:c0 + CW], preferred_element_type=jnp.float32)
        gc = _sigmoid(proj(HALO, HALO + TM, C_G + 2 * D + c0, CW))
        mg_ref[:, c0:c0 + CW] += gc * yc

    zc = _layer_norm(xo_ref[...], lnbg_ref[...], lnbb_ref[...])
    cvb_ref[...] = (zc * _sigmoid(zc)).astype(jnp.bfloat16)
    for c in range(n_chunks):
        c0 = c * CW
        yb = jnp.dot(cvb_ref[...], wb_ref[:, c0:c0 + CW], preferred_element_type=jnp.float32)
        gb = _sigmoid(proj(HALO, HALO + TM, C_G + D + c0, CW))
        mg_ref[:, c0:c0 + CW] += gb * yb

    cvb_ref[...] = mg_ref[...].astype(jnp.bfloat16)
    x_res = x_cur()
    for c in range(n_chunks):
        c0 = c * CW
        xo_ref[:, c0:c0 + CW] = x_res[:, c0:c0 + CW] + jnp.dot(
            cvb_ref[...], wo_ref[:, c0:c0 + CW], preferred_element_type=jnp.float32)
    xnew = xo_ref[...]
    ms = jnp.mean(xnew * xnew, axis=-1, keepdims=True)
    h2 = xnew * lax.rsqrt(ms + RMS_EPS) * nf_ref[...]
    h2_packed = _pack_bf16_pairs(h2)
    for j in range(NPIECE):
        h2_ref[j] = lax.bitcast_convert_type(h2_packed[:, j * 128:(j + 1) * 128], jnp.int32)
    h_hi = h2.astype(jnp.bfloat16)
    h_lo = (h2 - h_hi.astype(jnp.float32)).astype(jnp.bfloat16)
    dot32 = functools.partial(jnp.dot, preferred_element_type=jnp.float32)
    logits = (dot32(h_hi, wrh_ref[...]) + dot32(h_lo, wrh_ref[...])
              + dot32(h_hi, wrl_ref[...]) + dot32(h_lo, wrl_ref[...]) + br_ref[...])
    _route_tile(logits, idx_ref, wgt_ref, cnt_ref, carry_ref)


def _layer_spec(layer, shape):
    nd = len(shape)
    return pl.BlockSpec((None,) + tuple(shape), lambda i, _n=nd: (layer,) + (0,) * _n,
                        pipeline_mode=pl.Buffered(1))


def _mixer(layer, x2, pending, seq_len, nm, win, ca, wa, cb, lnbg, lnbb, wb, lncg, lncb, ws, bsb, wc, wo,
           nf, wrh, wrl, br, sh):
    T = x2.shape[0]
    n_tiles = T // TM
    hb = TM // HALO
    last_halo = T // HALO - 1
    ls = functools.partial(_layer_spec, layer)
    prev_rows = lambda i: jnp.maximum(i * hb - 1, 0)
    next_rows = lambda i: jnp.minimum((i + 1) * hb, last_halo)
    in_specs = [
        pl.BlockSpec((HALO, D), lambda i: (prev_rows(i), 0)),
        pl.BlockSpec((TM, D), lambda i: (i, 0)),
        pl.BlockSpec((HALO, D), lambda i: (next_rows(i), 0)),
    ]
    operands = [x2, x2, x2]
    if pending is not None:
        gathered, wgt_rows = pending
        in_specs += [
            pl.BlockSpec((2, NPIECE, HALO, 128), lambda i: (0, 0, prev_rows(i), 0)),
            pl.BlockSpec((2, NPIECE, TM, 128), lambda i: (0, 0, i, 0)),
            pl.BlockSpec((2, NPIECE, HALO, 128), lambda i: (0, 0, next_rows(i), 0)),
            pl.BlockSpec((HALO, 8), lambda i: (prev_rows(i), 0)),
            pl.BlockSpec((TM, 8), lambda i: (i, 0)),
            pl.BlockSpec((HALO, 8), lambda i: (next_rows(i), 0)),
        ]
        operands += [gathered, gathered, gathered, wgt_rows, wgt_rows, wgt_rows]
    in_specs += [
        ls((1, D)), ls((D, D_IN)), ls((CONV_A, D)), ls((D, D)),
        ls((CONV_B, D)), ls((1, D)), ls((1, D)), ls((D, D)),
        ls((1, D)), ls((1, D)), ls((N_HEADS_C, CHUNK, CHUNK)),
        ls((N_HEADS_C, CHUNK, CHUNK)), ls((D, D)), ls((D, D)),
        ls((1, D)), ls((D, NLOG)), ls((D, NLOG)), ls((1, NLOG)),
        pl.BlockSpec((RB, SK), lambda i: (0, 0), pipeline_mode=pl.Buffered(1)),
    ]
    out_specs = [
        pl.BlockSpec((TM, D), lambda i: (i, 0)),
        pl.BlockSpec((NPIECE, TM, 128), lambda i: (0, i, 0)),
        pl.BlockSpec((8, TM), lambda i: (0, i)),
        pl.BlockSpec((8, TM), lambda i: (0, i)),
        pl.BlockSpec((N_EXPERTS, 128), lambda i: (0, 0)),
    ]
    return pl.pallas_call(
        functools.partial(_mixer_kernel, seq_len // TM, pending is not None),
        grid=(n_tiles,),
        in_specs=in_specs,
        out_specs=out_specs,
        out_shape=[jax.ShapeDtypeStruct((T, D), jnp.float32),
                   jax.ShapeDtypeStruct((NPIECE, T, 128), jnp.int32),
                   jax.ShapeDtypeStruct((8, T), jnp.int32),
                   jax.ShapeDtypeStruct((8, T), jnp.float32),
                   jax.ShapeDtypeStruct((N_EXPERTS, 128), jnp.float32)],
        scratch_shapes=[pltpu.VMEM((TE, D), jnp.bfloat16),
                        pltpu.VMEM((D // CW, TE, CW), jnp.float32),
                        pltpu.VMEM((TM, D), jnp.bfloat16),
                        pltpu.VMEM((TM, D), jnp.bfloat16),
                        pltpu.VMEM((TM, D), jnp.bfloat16),
                        pltpu.VMEM((TM, D), jnp.float32),
                        pltpu.VMEM((TM, D), jnp.float32),
                        pltpu.VMEM((N_EXPERTS, 128), jnp.float32)],
        compiler_params=pltpu.CompilerParams(dimension_semantics=("arbitrary",),
                                             vmem_limit_bytes=VMEM_LIMIT),
        name="mixer",
    )(*operands, nm, win, ca, wa, cb, lnbg, lnbb, wb, lncg, lncb, ws, bsb, wc, wo, nf, wrh, wrl, br, sh)


def _route_tile(lg, idx_ref, wgt_ref, cnt_ref, carry_ref):
    i = pl.program_id(0)

    @pl.when(i == 0)
    def _():
        carry_ref[...] = jnp.zeros_like(carry_ref)

    lt = lg.T
    g = [lt[j:j + 1, :] for j in range(N_GROUPS)]
    gmax = jnp.maximum(jnp.maximum(g[0], g[1]), jnp.maximum(g[2], g[3]))
    gidx = jnp.where(g[0] == gmax, 0.0, jnp.where(g[1] == gmax, 1.0, jnp.where(g[2] == gmax, 2.0, 3.0)))
    gsum = sum(jnp.exp(gj - gmax) for gj in g)
    g_p = 1.0 / gsum

    sel = lt[E_OFF + 3 * EPG:E_OFF + 4 * EPG, :]
    for j in (2, 1, 0):
        sel = jnp.where(gidx == float(j), lt[E_OFF + j * EPG:E_OFF + (j + 1) * EPG, :], sel)
    rid = lax.broadcasted_iota(jnp.int32, (EPG, TR), 0).astype(jnp.float32)
    m1 = jnp.max(sel, axis=0, keepdims=True)
    i1 = jnp.min(jnp.where(sel == m1, rid, float(EPG)), axis=0, keepdims=True)
    rest = jnp.where(rid == i1, -jnp.inf, sel)
    m2 = jnp.max(rest, axis=0, keepdims=True)
    i2 = jnp.min(jnp.where(rest == m2, rid, float(EPG)), axis=0, keepdims=True)
    e2x = jnp.exp(m2 - m1)
    den = 1.0 + e2x
    w1 = (1.0 / den) * g_p
    w2 = (e2x / den) * g_p
    e1 = gidx * float(EPG) + i1
    e2 = gidx * float(EPG) + i2

    eid = lax.broadcasted_iota(jnp.int32, (N_EXPERTS, TR), 0).astype(jnp.float32)
    oh1 = (eid == e1).astype(jnp.float32)
    oh2 = (eid == e2).astype(jnp.float32)
    oh = oh1 + oh2
    tr = lax.broadcasted_iota(jnp.int32, (TR, TR), 0)
    tc = lax.broadcasted_iota(jnp.int32, (TR, TR), 1)
    upper = (tr < tc).astype(jnp.bfloat16)
    before = jnp.dot(oh.astype(jnp.bfloat16), upper, preferred_element_type=jnp.float32)
    base = before + carry_ref[:, 0:1]
    r1 = jnp.sum(oh1 * base, axis=0, keepdims=True)
    r2 = jnp.sum(oh2 * base, axis=0, keepdims=True)
    carry_ref[...] = carry_ref[...] + jnp.sum(oh, axis=1, keepdims=True)

    idx_ref[...] = jnp.zeros_like(idx_ref)
    idx_ref[0:1, :] = e1.astype(jnp.int32)
    idx_ref[1:2, :] = e2.astype(jnp.int32)
    idx_ref[2:3, :] = r1.astype(jnp.int32)
    idx_ref[3:4, :] = r2.astype(jnp.int32)
    wgt_ref[...] = jnp.zeros_like(wgt_ref)
    wgt_ref[0:1, :] = w1
    wgt_ref[1:2, :] = w2
    cnt_ref[...] = carry_ref[...]


SC_WINDOW = 128


def _sc_mesh():
    return plsc.VectorSubcoreMesh(core_axis_name="c", subcore_axis_name="s")


def _sc_scatter_rows(rows, idx, passes, n_out):
    n_src = rows.shape[0]
    src_blocks = n_src // SC_WINDOW

    @functools.partial(
        pl.kernel, mesh=_sc_mesh(),
        out_type=jax.ShapeDtypeStruct((n_out, 128), rows.dtype))
    def scatter(rows_hbm, idx_hbm, out_hbm):
        def body(rows_vmem, idx_vmem):
            pltpu.sync_copy(rows_vmem, out_hbm.at[idx_vmem.at[0]])

        pltpu.emit_pipeline(
            body,
            grid=(passes * src_blocks,),
            in_specs=[pl.BlockSpec((SC_WINDOW, 128), lambda i: (i % src_blocks, 0)),
                      pl.BlockSpec((1, SC_WINDOW), lambda i: (0, i))],
            out_specs=[],
            core_axis_name=("c", "s"),
            dimension_semantics=(pltpu.PARALLEL,),
        )(rows_hbm, idx_hbm)

    return scatter(rows, idx.reshape(1, passes * n_src))


def _expert_kernel(layer, be_ref, nu_ref, nxt_ref, gs_ref, nv_ref, xb_ref, w1_ref, w3_ref, w2_ref, yb_ref,
                   wf1_ref, wf3_ref, wf2_ref, w1b_ref, w3b_ref, w2b_ref, sem):
    b = pl.program_id(0)
    used = b < nu_ref[0]
    new_expert = jnp.logical_or(b == 0, be_ref[b] != be_ref[jnp.maximum(b - 1, 0)])

    def weight_copies(e, s):
        return [pltpu.make_async_copy(src.at[layer, e], dst.at[s], sem.at[s])
                for src, dst in ((w1_ref, wf1_ref), (w3_ref, wf3_ref), (w2_ref, wf2_ref))]

    @pl.when(b == 0)
    def _():
        for cp in weight_copies(be_ref[0], 0):
            cp.start(priority=1)

    @pl.when(jnp.logical_and(used, new_expert))
    def _():
        s = gs_ref[b]
        for cp in weight_copies(be_ref[b], s):
            cp.wait()

        @pl.when(nxt_ref[b] >= 0)
        def _():
            for cp in weight_copies(nxt_ref[b], 1 - s):
                cp.start(priority=1)

        w1b_ref[...] = wf1_ref[s].astype(jnp.bfloat16)
        w3b_ref[...] = wf3_ref[s].astype(jnp.bfloat16)
        w2b_ref[...] = wf2_ref[s].astype(jnp.bfloat16)

    def expert_mlp(rows):
        live = lax.broadcasted_iota(jnp.int32, (rows, 1), 0) < nv_ref[b]
        packed_in = jnp.concatenate(
            [jnp.where(live, lax.bitcast_convert_type(xb_ref[j, 0:rows], jnp.uint32), jnp.uint32(0))
             for j in range(NPIECE)], axis=1)
        x = jnp.concatenate(_unpack_bf16_pairs(packed_in), axis=1).astype(jnp.bfloat16)
        dot32 = functools.partial(jnp.dot, preferred_element_type=jnp.float32)
        a = dot32(x, w1b_ref[...])
        g = dot32(x, w3b_ref[...])
        hmid = (a * _sigmoid(a) * g).astype(jnp.bfloat16)
        packed = _pack_bf16_pairs(dot32(hmid, w2b_ref[...]))
        for j in range(NPIECE):
            yb_ref[j, 0:rows] = lax.bitcast_convert_type(packed[:, j * 128:(j + 1) * 128], jnp.int32)

    half = BM // 2
    half_full = nv_ref[b] <= half

    @pl.when(jnp.logical_and(used, jnp.logical_not(half_full)))
    def _():
        expert_mlp(BM)

    @pl.when(jnp.logical_and(used, half_full))
    def _():
        expert_mlp(half)
        yb_ref[:, half:BM, :] = jnp.zeros((NPIECE, BM - half, 128), jnp.int32)

    @pl.when(jnp.logical_not(used))
    def _():
        yb_ref[...] = jnp.zeros_like(yb_ref)


def _experts(layer, block_expert, n_used, next_expert, group_slot, block_valid, xb, w1, w3, w2):
    n_slots = xb.shape[1]
    n_blocks = n_slots // BM

    def row_map(b, be, nu, nxt, gs, nv):
        return (0, jnp.minimum(b, nu[0] - 1), 0)

    def out_map(b, be, nu, nxt, gs, nv):
        return (0, b, 0)

    grid_spec = pltpu.PrefetchScalarGridSpec(
        num_scalar_prefetch=5,
        grid=(n_blocks,),
        in_specs=[pl.BlockSpec((NPIECE, BM, 128), row_map),
                  pl.BlockSpec(memory_space=pl.ANY),
                  pl.BlockSpec(memory_space=pl.ANY),
                  pl.BlockSpec(memory_space=pl.ANY)],
        out_specs=pl.BlockSpec((NPIECE, BM, 128), out_map),
        scratch_shapes=[pltpu.VMEM((2, D, D_FF), jnp.float32),
                        pltpu.VMEM((2, D, D_FF), jnp.float32),
                        pltpu.VMEM((2, D_FF, D), jnp.float32),
                        pltpu.VMEM((D, D_FF), jnp.bfloat16),
                        pltpu.VMEM((D, D_FF), jnp.bfloat16),
                        pltpu.VMEM((D_FF, D), jnp.bfloat16),
                        pltpu.SemaphoreType.DMA((2,))],
    )
    return pl.pallas_call(
        functools.partial(_expert_kernel, layer),
        grid_spec=grid_spec,
        out_shape=jax.ShapeDtypeStruct((NPIECE, n_slots, 128), jnp.int32),
        compiler_params=pltpu.CompilerParams(dimension_semantics=("arbitrary",),
                                             vmem_limit_bytes=VMEM_LIMIT),
        name="experts",
    )(block_expert, n_used, next_expert, group_slot, block_valid, xb, w1, w3, w2)


def _sc_gather_rows(table, idx):
    n_rows = idx.shape[0]

    @functools.partial(
        pl.kernel, mesh=_sc_mesh(),
        out_type=jax.ShapeDtypeStruct((n_rows, 128), table.dtype))
    def gather(table_hbm, idx_hbm, out_hbm):
        def body(idx_vmem, out_vmem):
            pltpu.sync_copy(table_hbm.at[idx_vmem.at[0]], out_vmem)

        pltpu.emit_pipeline(
            body,
            grid=(n_rows // SC_WINDOW,),
            in_specs=[pl.BlockSpec((1, SC_WINDOW), lambda i: (0, i))],
            out_specs=[pl.BlockSpec((SC_WINDOW, 128), lambda i: (i, 0))],
            core_axis_name=("c", "s"),
            dimension_semantics=(pltpu.PARALLEL,),
        )(idx_hbm, out_hbm)

    return gather(table, idx.reshape(1, n_rows))


def _combine_kernel(final, g_ref, x_ref, w_ref, nrm_ref, o_ref):
    w = w_ref[...].T
    ys = {}
    for j in range(NPIECE):
        hi0, lo0 = _unpack_bf16_pairs(lax.bitcast_convert_type(g_ref[0, j], jnp.uint32))
        hi1, lo1 = _unpack_bf16_pairs(lax.bitcast_convert_type(g_ref[1, j], jnp.uint32))
        c_hi, c_lo = j * 128, DP + j * 128
        ys[c_hi] = x_ref[:, c_hi:c_hi + 128] + w[:, 0:1] * hi0 + w[:, 1:2] * hi1
        ys[c_lo] = x_ref[:, c_lo:c_lo + 128] + w[:, 0:1] * lo0 + w[:, 1:2] * lo1
    if final:
        ms = sum(jnp.sum(y * y, axis=-1, keepdims=True) for y in ys.values()) * (1.0 / D)
        scale = lax.rsqrt(ms + RMS_EPS)
        ys = {c: y * scale * nrm_ref[:, c:c + 128] for c, y in ys.items()}
    for c, y in ys.items():
        o_ref[:, c:c + 128] = y


def _combine(gathered, x2, wgt, nrm, final):
    T = x2.shape[0]
    return pl.pallas_call(
        functools.partial(_combine_kernel, final),
        grid=(T // TD,),
        in_specs=[pl.BlockSpec((2, NPIECE, TD, 128), lambda i: (0, 0, i, 0)),
                  pl.BlockSpec((TD, D), lambda i: (i, 0)),
                  pl.BlockSpec((8, TD), lambda i: (0, i)),
                  pl.BlockSpec((1, D), lambda i: (0, 0))],
        out_specs=pl.BlockSpec((TD, D), lambda i: (i, 0)),
        out_shape=jax.ShapeDtypeStruct((T, D), jnp.float32),
        compiler_params=pltpu.CompilerParams(dimension_semantics=("arbitrary",)),
        name="combine",
    )(gathered, x2, wgt, nrm)


def _moe(layer, h2, idx, cnt, w1, w3, w2):
    T = idx.shape[1]
    counts = cnt[:, 0].astype(jnp.int32)
    padded = (counts + BM - 1) // BM * BM
    padded_end = jnp.cumsum(padded)
    padded_start = padded_end - padded
    eids = jnp.arange(N_EXPERTS, dtype=jnp.int32)[:, None]
    start_of = lambda e: jnp.sum(jnp.where(e[None, :] == eids, padded_start[:, None], 0), axis=0)
    dest = jnp.stack([start_of(idx[0]) + idx[2], start_of(idx[1]) + idx[3]])
    n_blocks = (2 * T) // BM + N_EXPERTS
    block_start = jnp.arange(n_blocks, dtype=jnp.int32) * BM
    block_expert = jnp.minimum(
        jnp.sum((padded_end[None, :] <= block_start[:, None]).astype(jnp.int32), axis=1),
        N_EXPERTS - 1)
    n_used = (padded_end[-1] // BM).reshape(1)
    later = jnp.logical_and(eids.T > eids, (counts > 0)[None, :])
    next_of = jnp.min(jnp.where(later, eids.T, N_EXPERTS), axis=1)
    next_of = jnp.where(next_of < N_EXPERTS, next_of, -1)
    pick = lambda table: jnp.sum(
        jnp.where(block_expert[:, None] == eids.T, table[None, :], 0), axis=1)
    next_expert = pick(next_of)
    group_slot = pick(jnp.cumsum((counts > 0).astype(jnp.int32)) - 1) % 2
    block_valid = jnp.clip(pick(padded_start + counts) - block_start, 0, BM)
    n_slots = n_blocks * BM
    piece_base = (jnp.arange(NPIECE, dtype=jnp.int32) * n_slots)[None, :, None]
    rows = (dest[:, None, :] + piece_base).reshape(-1)
    xb = _sc_scatter_rows(h2.reshape(NPIECE * T, 128), rows, 2, NPIECE * n_slots)
    yb = _experts(layer, block_expert, n_used, next_expert, group_slot, block_valid,
                  xb.reshape(NPIECE, n_slots, 128), w1, w3, w2)
    gathered = _sc_gather_rows(yb.reshape(NPIECE * n_slots, 128), rows)
    return gathered.reshape(2, NPIECE, T, 128)


def kernel(x, norm_mix, w_in, conv_a, w_a_out, conv_b, ln_b_g, ln_b_b, w_b_out, ln_c_g, ln_c_b,
           w_s, b_s, w_c_out, w_o, norm_ffn, w_group, b_group, w_router, b_router, w1, w3, w2,
           norm_final):
    bsz, seq, d = x.shape
    depth = norm_mix.shape[0]
    bf = jnp.bfloat16
    x2 = x.reshape(bsz * seq, d)
    wr = jnp.zeros((depth, d, NLOG), jnp.float32)
    wr = wr.at[:, :, 0:N_GROUPS].set(w_group).at[:, :, E_OFF:E_OFF + N_EXPERTS].set(w_router)
    wr_hi = wr.astype(bf)
    wr_lo = (wr - wr_hi.astype(jnp.float32)).astype(bf)
    br = jnp.zeros((depth, 1, NLOG), jnp.float32)
    br = br.at[:, 0, 0:N_GROUPS].set(b_group).at[:, 0, E_OFF:E_OFF + N_EXPERTS].set(b_router)
    bsb = jnp.broadcast_to(b_s[:, :, :, None], (depth, N_HEADS_C, CHUNK, CHUNK))
    row = lambda p: p[:, None, :]
    mixer_params = (row(norm_mix), w_in.astype(bf), conv_a, w_a_out.astype(bf), conv_b,
                    row(ln_b_g), row(ln_b_b), w_b_out.astype(bf), row(ln_c_g), row(ln_c_b),
                    w_s.astype(bf), bsb, w_c_out.astype(bf), w_o.astype(bf), row(norm_ffn), wr_hi, wr_lo, br)
    col = jnp.arange(SK, dtype=jnp.int32)[None, :]
    shift_sum = jnp.logical_and(
        col % PR == jnp.arange(RB, dtype=jnp.int32)[:, None] + col // PR, col < 8 * PR).astype(bf)
    pending = None
    for l in range(depth):
        x2, h2, idx, wgt, cnt = _mixer(l, x2, pending, seq, *mixer_params, shift_sum)
        gathered = _moe(l, h2, idx, cnt, w1, w3, w2)
        pending = (gathered, wgt.T)
    out = _combine(gathered, x2, wgt, norm_final[None], True)
    return out.reshape(bsz, seq, d)
```

```python
import functools

import jax
import jax.numpy as jnp
from jax import lax
from jax.experimental import pallas as pl
from jax.experimental.pallas import tpu as pltpu
from jax.experimental.pallas import tpu_sc as plsc

D = 1024
DP = D // 2
NPIECE = DP // 128
N_HEADS_C = 8
CHUNK = 128
CONV_A = 3
CONV_B = 31
N_GROUPS = 4
EPG = 8
N_EXPERTS = N_GROUPS * EPG
D_FF = 512
RMS_EPS = 1e-6
LN_EPS = 1e-5

C_XA, C_BA, C_CA, C_VB, C_GB, C_UV, C_G = 0, 1024, 2048, 3072, 4096, 5120, 7168
D_IN = 10240

TM = 512
HALO = 16
TE = TM + 2 * HALO
CW = 256
RB = 64
PR = RB + 8
SK = -(-8 * PR // 128) * 128
NLOG = 128
E_OFF = 8

TR = TM
BM = 512
TD = 512

VMEM_LIMIT = 60 * 1024 * 1024


def _sigmoid(x):
    return 0.5 * (jnp.tanh(0.5 * x) + 1.0)


def _gelu_tanh(x):
    return 0.5 * x * (1.0 + jnp.tanh(0.7978845608028654 * (x + 0.044715 * (x * x * x))))


def _pack_bf16_pairs(x):
    c = x.shape[1] // 2
    as_bits = lambda v: lax.bitcast_convert_type(v.astype(jnp.bfloat16).astype(jnp.float32), jnp.uint32)
    return as_bits(x[:, :c]) | (as_bits(x[:, c:]) >> 16)


def _unpack_bf16_pairs(p):
    hi = lax.bitcast_convert_type(p & jnp.uint32(0xFFFF0000), jnp.float32)
    lo = lax.bitcast_convert_type(p << 16, jnp.float32)
    return hi, lo


def _layer_norm(x, g, b):
    mu = jnp.mean(x, axis=-1, keepdims=True)
    xc = x - mu
    var = jnp.mean(xc * xc, axis=-1, keepdims=True)
    return xc * lax.rsqrt(var + LN_EPS) * g + b


def _moe_residual(x_ref, g_ref, wt_ref):
    w0, w1 = wt_ref[:, 0:1], wt_ref[:, 1:2]
    his, los = [], []
    for j in range(NPIECE):
        hi0, lo0 = _unpack_bf16_pairs(lax.bitcast_convert_type(g_ref[0, j], jnp.uint32))
        hi1, lo1 = _unpack_bf16_pairs(lax.bitcast_convert_type(g_ref[1, j], jnp.uint32))
        his.append(w0 * hi0 + w1 * hi1)
        los.append(w0 * lo0 + w1 * lo1)
    return x_ref[...] + jnp.concatenate(his + los, axis=1)


def _mixer_kernel(seq_tiles, fused_combine, xp_ref, xc_ref, xn_ref, *refs):
    if fused_combine:
        gp_ref, gc_ref, gn_ref, wtp_ref, wtc_ref, wtn_ref = refs[:6]
        refs = refs[6:]
        x_prev = lambda: _moe_residual(xp_ref, gp_ref, wtp_ref)
        x_cur = lambda: _moe_residual(xc_ref, gc_ref, wtc_ref)
        x_next = lambda: _moe_residual(xn_ref, gn_ref, wtn_ref)
    else:
        x_prev, x_cur, x_next = (lambda: xp_ref[...]), (lambda: xc_ref[...]), (lambda: xn_ref[...])
    (nm_ref, win_ref, ca_ref, wa_ref, cb_ref, lnbg_ref, lnbb_ref, wb_ref, lncg_ref, lncb_ref,
     ws_ref, bsb_ref, wc_ref, wo_ref, nf_ref, wrh_ref, wrl_ref, br_ref, sh_ref,
     xo_ref, h2_ref, idx_ref, wgt_ref, cnt_ref,
     hb_ref, zq_ref, cva_ref, cvb_ref, cvn_ref, vst_ref, mg_ref, carry_ref) = refs
    i = pl.program_id(0)
    at_start = (i % seq_tiles) == 0
    at_end = (i % seq_tiles) == seq_tiles - 1

    def _rms(xv):
        ms = jnp.mean(xv * xv, axis=-1, keepdims=True)
        return (xv * lax.rsqrt(ms + RMS_EPS) * nm_ref[...]).astype(jnp.bfloat16)

    hb_ref[0:HALO, :] = _rms(x_prev())
    hb_ref[HALO:HALO + TM, :] = _rms(x_cur())
    hb_ref[HALO + TM:TE, :] = _rms(x_next())

    rows = lax.broadcasted_iota(jnp.int32, (TE, 1), 0)
    lo = jnp.where(at_start, HALO, 0)
    hi = jnp.where(at_end, HALO + TM, TE)
    valid = jnp.logical_and(rows >= lo, rows < hi)

    def proj(r0, r1, c0, width):
        return jnp.dot(hb_ref[r0:r1, :], win_ref[:, c0:c0 + width],
                       preferred_element_type=jnp.float32)

    def b_proj(c):
        c0 = c * CW
        val = proj(0, TE, C_VB + c0, CW)
        gate = proj(0, TE, C_GB + c0, CW)
        zq_ref[c] = jnp.where(valid, val * _sigmoid(gate), 0.0)

    def a_proj(c):
        c0 = c * CW
        xa = proj(0, TE, C_XA + c0, CW)
        cc = proj(0, TE, C_CA + c0, CW)
        t = jnp.where(valid, xa * cc, 0.0)
        conv = (ca_ref[0:1, c0:c0 + CW] * t[HALO - 1:HALO - 1 + TM]
                + ca_ref[1:2, c0:c0 + CW] * t[HALO:HALO + TM]
                + ca_ref[2:3, c0:c0 + CW] * t[HALO + 1:HALO + 1 + TM])
        ba = proj(HALO, HALO + TM, C_BA + c0, CW)
        cva_ref[:, c0:c0 + CW] = (ba * conv).astype(jnp.bfloat16)

    def c_v_proj(c):
        c0 = c * CW
        vst_ref[:, c0:c0 + CW] = _gelu_tanh(proj(HALO, HALO + TM, C_UV + D + c0, CW))

    def b_conv(c, row_blocks):
        c0 = c * CW
        zq = zq_ref.at[c]
        for rb in row_blocks:
            r0 = rb * RB
            for lt in range(CW // 128):
                l0 = lt * 128
                parts = []
                for s in range(8):
                    part = None
                    for q in range(4):
                        k = 8 * q + s - (HALO - CONV_B // 2)
                        if 0 <= k < CONV_B:
                            term = (cb_ref[k:k + 1, c0 + l0:c0 + l0 + 128]
                                    * zq[r0 + 8 * q:r0 + 8 * q + PR, l0:l0 + 128])
                            part = term if part is None else part + term
                    parts.append(part)
                if SK > 8 * PR:
                    parts.append(jnp.zeros((SK - 8 * PR, 128), jnp.float32))
                stacked = jnp.concatenate(parts, axis=0).astype(jnp.bfloat16)
                xo_ref[r0:r0 + RB, c0 + l0:c0 + l0 + 128] = jnp.dot(
                    sh_ref[...], stacked, preferred_element_type=jnp.float32)

    n_chunks = D // CW
    n_rb = TM // RB
    rb_step = n_rb // n_chunks
    late_rows = lambda c: range(c * rb_step, (c + 1) * rb_step)
    b_proj(0)
    for c in range(n_chunks):
        if c + 1 < n_chunks:
            b_proj(c + 1)
        a_proj(c)
        c_v_proj(c)
    b_conv(0, range(n_rb))

    vn = _layer_norm(vst_ref[...], lncg_ref[...], lncb_ref[...])
    cvn_ref[...] = vn.astype(jnp.bfloat16)
    for c in range(n_chunks):
        c0 = c * CW
        ya = jnp.dot(cva_ref[...], wa_ref[:, c0:c0 + CW], preferred_element_type=jnp.float32)
        ga = _sigmoid(proj(HALO, HALO + TM, C_G + c0, CW))
        mg_ref[:, c0:c0 + CW] = ga * ya
        b_conv(1, late_rows(c))

    for c in range(n_chunks):
        c0 = c * CW
        u = _gelu_tanh(proj(HALO, HALO + TM, C_UV + c0, CW))
        for n in range(TM // CHUNK):
            for hh in range(CW // CHUNK):
                h = c * (CW // CHUNK) + hh
                sv = jnp.dot(ws_ref[h], cvn_ref[n * CHUNK:(n + 1) * CHUNK, h * CHUNK:(h + 1) * CHUNK],
                             preferred_element_type=jnp.float32) + bsb_ref[h]
                cva_ref[n * CHUNK:(n + 1) * CHUNK, h * CHUNK:(h + 1) * CHUNK] = (
                    u[n * CHUNK:(n + 1) * CHUNK, hh * CHUNK:(hh + 1) * CHUNK] * sv
                ).astype(jnp.bfloat16)
        b_conv(2, late_rows(c))
    for c in range(n_chunks):
        c0 = c * CW
        yc = jnp.dot(cva_ref[...], wc_ref[:, c0:c0 + CW], preferred_element_type=jnp.float32)
        gc = _sigmoid(proj(HALO, HALO + TM, C_G + 2 * D + c0, CW))
        mg_ref[:, c0:c0 + CW] += gc * yc
        b_conv(3, late_rows(c))

    zc = _layer_norm(xo_ref[...], lnbg_ref[...], lnbb_ref[...])
    cvb_ref[...] = (zc * _sigmoid(zc)).astype(jnp.bfloat16)
    for c in range(n_chunks):
        c0 = c * CW
        yb = jnp.dot(cvb_ref[...], wb_ref[:, c0:c0 + CW], preferred_element_type=jnp.float32)
        gb = _sigmoid(proj(HALO, HALO + TM, C_G + D + c0, CW))
        mg_ref[:, c0:c0 + CW] += gb * yb

    cvb_ref[...] = mg_ref[...].astype(jnp.bfloat16)
    x_res = x_cur()
    for c in range(n_chunks):
        c0 = c * CW
        xo_ref[:, c0:c0 + CW] = x_res[:, c0:c0 + CW] + jnp.dot(
            cvb_ref[...], wo_ref[:, c0:c0 + CW], preferred_element_type=jnp.float32)
    xnew = xo_ref[...]
    ms = jnp.mean(xnew * xnew, axis=-1, keepdims=True)
    h2 = xnew * lax.rsqrt(ms + RMS_EPS) * nf_ref[...]
    h2_packed = _pack_bf16_pairs(h2)
    for j in range(NPIECE):
        h2_ref[j] = lax.bitcast_convert_type(h2_packed[:, j * 128:(j + 1) * 128], jnp.int32)
    h_hi = h2.astype(jnp.bfloat16)
    h_lo = (h2 - h_hi.astype(jnp.float32)).astype(jnp.bfloat16)
    dot32 = functools.partial(jnp.dot, preferred_element_type=jnp.float32)
    logits = (dot32(h_hi, wrh_ref[...]) + dot32(h_lo, wrh_ref[...])
              + dot32(h_hi, wrl_ref[...]) + dot32(h_lo, wrl_ref[...]) + br_ref[...])
    _route_tile(logits, idx_ref, wgt_ref, cnt_ref, carry_ref)


def _layer_spec(layer, shape):
    nd = len(shape)
    return pl.BlockSpec((None,) + tuple(shape), lambda i, _n=nd: (layer,) + (0,) * _n,
                        pipeline_mode=pl.Buffered(1))


def _mixer(layer, x2, pending, seq_len, nm, win, ca, wa, cb, lnbg, lnbb, wb, lncg, lncb, ws, bsb, wc, wo,
           nf, wrh, wrl, br, sh):
    T = x2.shape[0]
    n_tiles = T // TM
    hb = TM // HALO
    last_halo = T // HALO - 1
    ls = functools.partial(_layer_spec, layer)
    prev_rows = lambda i: jnp.maximum(i * hb - 1, 0)
    next_rows = lambda i: jnp.minimum((i + 1) * hb, last_halo)
    in_specs = [
        pl.BlockSpec((HALO, D), lambda i: (prev_rows(i), 0)),
        pl.BlockSpec((TM, D), lambda i: (i, 0)),
        pl.BlockSpec((HALO, D), lambda i: (next_rows(i), 0)),
    ]
    operands = [x2, x2, x2]
    if pending is not None:
        gathered, wgt_rows = pending
        in_specs += [
            pl.BlockSpec((2, NPIECE, HALO, 128), lambda i: (0, 0, prev_rows(i), 0)),
            pl.BlockSpec((2, NPIECE, TM, 128), lambda i: (0, 0, i, 0)),
            pl.BlockSpec((2, NPIECE, HALO, 128), lambda i: (0, 0, next_rows(i), 0)),
            pl.BlockSpec((HALO, 8), lambda i: (prev_rows(i), 0)),
            pl.BlockSpec((TM, 8), lambda i: (i, 0)),
            pl.BlockSpec((HALO, 8), lambda i: (next_rows(i), 0)),
        ]
        operands += [gathered, gathered, gathered, wgt_rows, wgt_rows, wgt_rows]
    in_specs += [
        ls((1, D)), ls((D, D_IN)), ls((CONV_A, D)), ls((D, D)),
        ls((CONV_B, D)), ls((1, D)), ls((1, D)), ls((D, D)),
        ls((1, D)), ls((1, D)), ls((N_HEADS_C, CHUNK, CHUNK)),
        ls((N_HEADS_C, CHUNK, CHUNK)), ls((D, D)), ls((D, D)),
        ls((1, D)), ls((D, NLOG)), ls((D, NLOG)), ls((1, NLOG)),
        pl.BlockSpec((RB, SK), lambda i: (0, 0), pipeline_mode=pl.Buffered(1)),
    ]
    out_specs = [
        pl.BlockSpec((TM, D), lambda i: (i, 0)),
        pl.BlockSpec((NPIECE, TM, 128), lambda i: (0, i, 0)),
        pl.BlockSpec((8, TM), lambda i: (0, i)),
        pl.BlockSpec((8, TM), lambda i: (0, i)),
        pl.BlockSpec((N_EXPERTS, 128), lambda i: (0, 0)),
    ]
    return pl.pallas_call(
        functools.partial(_mixer_kernel, seq_len // TM, pending is not None),
        grid=(n_tiles,),
        in_specs=in_specs,
        out_specs=out_specs,
        out_shape=[jax.ShapeDtypeStruct((T, D), jnp.float32),
                   jax.ShapeDtypeStruct((NPIECE, T, 128), jnp.int32),
                   jax.ShapeDtypeStruct((8, T), jnp.int32),
                   jax.ShapeDtypeStruct((8, T), jnp.float32),
                   jax.ShapeDtypeStruct((N_EXPERTS, 128), jnp.float32)],
        scratch_shapes=[pltpu.VMEM((TE, D), jnp.bfloat16),
                        pltpu.VMEM((D // CW, TE, CW), jnp.float32),
                        pltpu.VMEM((TM, D), jnp.bfloat16),
                        pltpu.VMEM((TM, D), jnp.bfloat16),
                        pltpu.VMEM((TM, D), jnp.bfloat16),
                        pltpu.VMEM((TM, D), jnp.float32),
                        pltpu.VMEM((TM, D), jnp.float32),
                        pltpu.VMEM((N_EXPERTS, 128), jnp.float32)],
        compiler_params=pltpu.CompilerParams(dimension_semantics=("arbitrary",),
                                             vmem_limit_bytes=VMEM_LIMIT),
        name="mixer",
    )(*operands, nm, win, ca, wa, cb, lnbg, lnbb, wb, lncg, lncb, ws, bsb, wc, wo, nf, wrh, wrl, br, sh)


def _route_tile(lg, idx_ref, wgt_ref, cnt_ref, carry_ref):
    i = pl.program_id(0)

    @pl.when(i == 0)
    def _():
        carry_ref[...] = jnp.zeros_like(carry_ref)

    lt = lg.T
    g = [lt[j:j + 1, :] for j in range(N_GROUPS)]
    gmax = jnp.maximum(jnp.maximum(g[0], g[1]), jnp.maximum(g[2], g[3]))
    gidx = jnp.where(g[0] == gmax, 0.0, jnp.where(g[1] == gmax, 1.0, jnp.where(g[2] == gmax, 2.0, 3.0)))
    gsum = sum(jnp.exp(gj - gmax) for gj in g)
    g_p = 1.0 / gsum

    sel = lt[E_OFF + 3 * EPG:E_OFF + 4 * EPG, :]
    for j in (2, 1, 0):
        sel = jnp.where(gidx == float(j), lt[E_OFF + j * EPG:E_OFF + (j + 1) * EPG, :], sel)
    rid = lax.broadcasted_iota(jnp.int32, (EPG, TR), 0).astype(jnp.float32)
    m1 = jnp.max(sel, axis=0, keepdims=True)
    i1 = jnp.min(jnp.where(sel == m1, rid, float(EPG)), axis=0, keepdims=True)
    rest = jnp.where(rid == i1, -jnp.inf, sel)
    m2 = jnp.max(rest, axis=0, keepdims=True)
    i2 = jnp.min(jnp.where(rest == m2, rid, float(EPG)), axis=0, keepdims=True)
    e2x = jnp.exp(m2 - m1)
    den = 1.0 + e2x
    w1 = (1.0 / den) * g_p
    w2 = (e2x / den) * g_p
    e1 = gidx * float(EPG) + i1
    e2 = gidx * float(EPG) + i2

    eid = lax.broadcasted_iota(jnp.int32, (N_EXPERTS, TR), 0).astype(jnp.float32)
    oh1 = (eid == e1).astype(jnp.float32)
    oh2 = (eid == e2).astype(jnp.float32)
    oh = oh1 + oh2
    tr = lax.broadcasted_iota(jnp.int32, (TR, TR), 0)
    tc = lax.broadcasted_iota(jnp.int32, (TR, TR), 1)
    upper = (tr < tc).astype(jnp.bfloat16)
    before = jnp.dot(oh.astype(jnp.bfloat16), upper, preferred_element_type=jnp.float32)
    base = before + carry_ref[:, 0:1]
    r1 = jnp.sum(oh1 * base, axis=0, keepdims=True)
    r2 = jnp.sum(oh2 * base, axis=0, keepdims=True)
    carry_ref[...] = carry_ref[...] + jnp.sum(oh, axis=1, keepdims=True)

    idx_ref[...] = jnp.zeros_like(idx_ref)
    idx_ref[0:1, :] = e1.astype(jnp.int32)
    idx_ref[1:2, :] = e2.astype(jnp.int32)
    idx_ref[2:3, :] = r1.astype(jnp.int32)
    idx_ref[3:4, :] = r2.astype(jnp.int32)
    wgt_ref[...] = jnp.zeros_like(wgt_ref)
    wgt_ref[0:1, :] = w1
    wgt_ref[1:2, :] = w2
    cnt_ref[...] = carry_ref[...]


SC_WINDOW = 128


def _sc_mesh():
    return plsc.VectorSubcoreMesh(core_axis_name="c", subcore_axis_name="s")


def _sc_scatter_rows(rows, idx, passes, n_out):
    n_src = rows.shape[0]
    src_blocks = n_src // SC_WINDOW

    @functools.partial(
        pl.kernel, mesh=_sc_mesh(),
        out_type=jax.ShapeDtypeStruct((n_out, 128), rows.dtype))
    def scatter(rows_hbm, idx_hbm, out_hbm):
        def body(rows_vmem, idx_vmem):
            pltpu.sync_copy(rows_vmem, out_hbm.at[idx_vmem.at[0]])

        pltpu.emit_pipeline(
            body,
            grid=(passes * src_blocks,),
            in_specs=[pl.BlockSpec((SC_WINDOW, 128), lambda i: (i % src_blocks, 0)),
                      pl.BlockSpec((1, SC_WINDOW), lambda i: (0, i))],
            out_specs=[],
            core_axis_name=("c", "s"),
            dimension_semantics=(pltpu.PARALLEL,),
        )(rows_hbm, idx_hbm)

    return scatter(rows, idx.reshape(1, passes * n_src))


def _expert_kernel(layer, be_ref, nu_ref, nxt_ref, gs_ref, nv_ref, xb_ref, w1_ref, w3_ref, w2_ref, yb_ref,
                   wf1_ref, wf3_ref, wf2_ref, w1b_ref, w3b_ref, w2b_ref, sem):
    b = pl.program_id(0)
    used = b < nu_ref[0]
    new_expert = jnp.logical_or(b == 0, be_ref[b] != be_ref[jnp.maximum(b - 1, 0)])

    def weight_copies(e, s):
        return [pltpu.make_async_copy(src.at[layer, e], dst.at[s], sem.at[s])
                for src, dst in ((w1_ref, wf1_ref), (w3_ref, wf3_ref), (w2_ref, wf2_ref))]

    @pl.when(b == 0)
    def _():
        for cp in weight_copies(be_ref[0], 0):
            cp.start(priority=1)

    @pl.when(jnp.logical_and(used, new_expert))
    def _():
        s = gs_ref[b]
        for cp in weight_copies(be_ref[b], s):
            cp.wait()

        @pl.when(nxt_ref[b] >= 0)
        def _():
            for cp in weight_copies(nxt_ref[b], 1 - s):
                cp.start(priority=1)

        w1b_ref[...] = wf1_ref[s].astype(jnp.bfloat16)
        w3b_ref[...] = wf3_ref[s].astype(jnp.bfloat16)
        w2b_ref[...] = wf2_ref[s].astype(jnp.bfloat16)

    def expert_mlp(rows):
        live = lax.broadcasted_iota(jnp.int32, (rows, 1), 0) < nv_ref[b]
        packed_in = jnp.concatenate(
            [jnp.where(live, lax.bitcast_convert_type(xb_ref[j, 0:rows], jnp.uint32), jnp.uint32(0))
             for j in range(NPIECE)], axis=1)
        x = jnp.concatenate(_unpack_bf16_pairs(packed_in), axis=1).astype(jnp.bfloat16)
        dot32 = functools.partial(jnp.dot, preferred_element_type=jnp.float32)
        a = dot32(x, w1b_ref[...])
        g = dot32(x, w3b_ref[...])
        hmid = (a * _sigmoid(a) * g).astype(jnp.bfloat16)
        packed = _pack_bf16_pairs(dot32(hmid, w2b_ref[...]))
        for j in range(NPIECE):
            yb_ref[j, 0:rows] = lax.bitcast_convert_type(packed[:, j * 128:(j + 1) * 128], jnp.int32)

    half = BM // 2
    half_full = nv_ref[b] <= half

    @pl.when(jnp.logical_and(used, jnp.logical_not(half_full)))
    def _():
        expert_mlp(BM)

    @pl.when(jnp.logical_and(used, half_full))
    def _():
        expert_mlp(half)
        yb_ref[:, half:BM, :] = jnp.zeros((NPIECE, BM - half, 128), jnp.int32)

    @pl.when(jnp.logical_not(used))
    def _():
        yb_ref[...] = jnp.zeros_like(yb_ref)


def _experts(layer, block_expert, n_used, next_expert, group_slot, block_valid, xb, w1, w3, w2):
    n_slots = xb.shape[1]
    n_blocks = n_slots // BM

    def row_map(b, be, nu, nxt, gs, nv):
        return (0, jnp.minimum(b, nu[0] - 1), 0)

    def out_map(b, be, nu, nxt, gs, nv):
        return (0, b, 0)

    grid_spec = pltpu.PrefetchScalarGridSpec(
        num_scalar_prefetch=5,
        grid=(n_blocks,),
        in_specs=[pl.BlockSpec((NPIECE, BM, 128), row_map),
                  pl.BlockSpec(memory_space=pl.ANY),
                  pl.BlockSpec(memory_space=pl.ANY),
                  pl.BlockSpec(memory_space=pl.ANY)],
        out_specs=pl.BlockSpec((NPIECE, BM, 128), out_map),
        scratch_shapes=[pltpu.VMEM((2, D, D_FF), jnp.float32),
                        pltpu.VMEM((2, D, D_FF), jnp.float32),
                        pltpu.VMEM((2, D_FF, D), jnp.float32),
                        pltpu.VMEM((D, D_FF), jnp.bfloat16),
                        pltpu.VMEM((D, D_FF), jnp.bfloat16),
                        pltpu.VMEM((D_FF, D), jnp.bfloat16),
                        pltpu.SemaphoreType.DMA((2,))],
    )
    return pl.pallas_call(
        functools.partial(_expert_kernel, layer),
        grid_spec=grid_spec,
        out_shape=jax.ShapeDtypeStruct((NPIECE, n_slots, 128), jnp.int32),
        compiler_params=pltpu.CompilerParams(dimension_semantics=("arbitrary",),
                                             vmem_limit_bytes=VMEM_LIMIT),
        name="experts",
    )(block_expert, n_used, next_expert, group_slot, block_valid, xb, w1, w3, w2)


def _sc_gather_rows(table, idx):
    n_rows = idx.shape[0]

    @functools.partial(
        pl.kernel, mesh=_sc_mesh(),
        out_type=jax.ShapeDtypeStruct((n_rows, 128), table.dtype))
    def gather(table_hbm, idx_hbm, out_hbm):
        def body(idx_vmem, out_vmem):
            pltpu.sync_copy(table_hbm.at[idx_vmem.at[0]], out_vmem)

        pltpu.emit_pipeline(
            body,
            grid=(n_rows // SC_WINDOW,),
            in_specs=[pl.BlockSpec((1, SC_WINDOW), lambda i: (0, i))],
            out_specs=[pl.BlockSpec((SC_WINDOW, 128), lambda i: (i, 0))],
            core_axis_name=("c", "s"),
            dimension_semantics=(pltpu.PARALLEL,),
        )(idx_hbm, out_hbm)

    return gather(table, idx.reshape(1, n_rows))


def _combine_kernel(final, g_ref, x_ref, w_ref, nrm_ref, o_ref):
    w = w_ref[...].T
    ys = {}
    for j in range(NPIECE):
        hi0, lo0 = _unpack_bf16_pairs(lax.bitcast_convert_type(g_ref[0, j], jnp.uint32))
        hi1, lo1 = _unpack_bf16_pairs(lax.bitcast_convert_type(g_ref[1, j], jnp.uint32))
        c_hi, c_lo = j * 128, DP + j * 128
        ys[c_hi] = x_ref[:, c_hi:c_hi + 128] + w[:, 0:1] * hi0 + w[:, 1:2] * hi1
        ys[c_lo] = x_ref[:, c_lo:c_lo + 128] + w[:, 0:1] * lo0 + w[:, 1:2] * lo1
    if final:
        ms = sum(jnp.sum(y * y, axis=-1, keepdims=True) for y in ys.values()) * (1.0 / D)
        scale = lax.rsqrt(ms + RMS_EPS)
        ys = {c: y * scale * nrm_ref[:, c:c + 128] for c, y in ys.items()}
    for c, y in ys.items():
        o_ref[:, c:c + 128] = y


def _combine(gathered, x2, wgt, nrm, final):
    T = x2.shape[0]
    return pl.pallas_call(
        functools.partial(_combine_kernel, final),
        grid=(T // TD,),
        in_specs=[pl.BlockSpec((2, NPIECE, TD, 128), lambda i: (0, 0, i, 0)),
                  pl.BlockSpec((TD, D), lambda i: (i, 0)),
                  pl.BlockSpec((8, TD), lambda i: (0, i)),
                  pl.BlockSpec((1, D), lambda i: (0, 0))],
        out_specs=pl.BlockSpec((TD, D), lambda i: (i, 0)),
        out_shape=jax.ShapeDtypeStruct((T, D), jnp.float32),
        compiler_params=pltpu.CompilerParams(dimension_semantics=("arbitrary",)),
        name="combine",
    )(gathered, x2, wgt, nrm)


def _moe(layer, h2, idx, cnt, w1, w3, w2):
    T = idx.shape[1]
    counts = cnt[:, 0].astype(jnp.int32)
    padded = (counts + BM - 1) // BM * BM
    padded_end = jnp.cumsum(padded)
    padded_start = padded_end - padded
    eids = jnp.arange(N_EXPERTS, dtype=jnp.int32)[:, None]
    start_of = lambda e: jnp.sum(jnp.where(e[None, :] == eids, padded_start[:, None], 0), axis=0)
    dest = jnp.stack([start_of(idx[0]) + idx[2], start_of(idx[1]) + idx[3]])
    n_blocks = (2 * T) // BM + N_EXPERTS
    block_start = jnp.arange(n_blocks, dtype=jnp.int32) * BM
    block_expert = jnp.minimum(
        jnp.sum((padded_end[None, :] <= block_start[:, None]).astype(jnp.int32), axis=1),
        N_EXPERTS - 1)
    n_used = (padded_end[-1] // BM).reshape(1)
    later = jnp.logical_and(eids.T > eids, (counts > 0)[None, :])
    next_of = jnp.min(jnp.where(later, eids.T, N_EXPERTS), axis=1)
    next_of = jnp.where(next_of < N_EXPERTS, next_of, -1)
    pick = lambda table: jnp.sum(
        jnp.where(block_expert[:, None] == eids.T, table[None, :], 0), axis=1)
    next_expert = pick(next_of)
    group_slot = pick(jnp.cumsum((counts > 0).astype(jnp.int32)) - 1) % 2
    block_valid = jnp.clip(pick(padded_start + counts) - block_start, 0, BM)
    n_slots = n_blocks * BM
    piece_base = (jnp.arange(NPIECE, dtype=jnp.int32) * n_slots)[None, :, None]
    rows = (dest[:, None, :] + piece_base).reshape(-1)
    xb = _sc_scatter_rows(h2.reshape(NPIECE * T, 128), rows, 2, NPIECE * n_slots)
    yb = _experts(layer, block_expert, n_used, next_expert, group_slot, block_valid,
                  xb.reshape(NPIECE, n_slots, 128), w1, w3, w2)
    gathered = _sc_gather_rows(yb.reshape(NPIECE * n_slots, 128), rows)
    return gathered.reshape(2, NPIECE, T, 128)


def kernel(x, norm_mix, w_in, conv_a, w_a_out, conv_b, ln_b_g, ln_b_b, w_b_out, ln_c_g, ln_c_b,
           w_s, b_s, w_c_out, w_o, norm_ffn, w_group, b_group, w_router, b_router, w1, w3, w2,
           norm_final):
    bsz, seq, d = x.shape
    depth = norm_mix.shape[0]
    bf = jnp.bfloat16
    x2 = x.reshape(bsz * seq, d)
    wr = jnp.zeros((depth, d, NLOG), jnp.float32)
    wr = wr.at[:, :, 0:N_GROUPS].set(w_group).at[:, :, E_OFF:E_OFF + N_EXPERTS].set(w_router)
    wr_hi = wr.astype(bf)
    wr_lo = (wr - wr_hi.astype(jnp.float32)).astype(bf)
    br = jnp.zeros((depth, 1, NLOG), jnp.float32)
    br = br.at[:, 0, 0:N_GROUPS].set(b_group).at[:, 0, E_OFF:E_OFF + N_EXPERTS].set(b_router)
    bsb = jnp.broadcast_to(b_s[:, :, :, None], (depth, N_HEADS_C, CHUNK, CHUNK))
    row = lambda p: p[:, None, :]
    mixer_params = (row(norm_mix), w_in.astype(bf), conv_a, w_a_out.astype(bf), conv_b,
                    row(ln_b_g), row(ln_b_b), w_b_out.astype(bf), row(ln_c_g), row(ln_c_b),
                    w_s.astype(bf), bsb, w_c_out.astype(bf), w_o.astype(bf), row(norm_ffn), wr_hi, wr_lo, br)
    col = jnp.arange(SK, dtype=jnp.int32)[None, :]
    shift_sum = jnp.logical_and(
        col % PR == jnp.arange(RB, dtype=jnp.int32)[:, None] + col // PR, col < 8 * PR).astype(bf)
    pending = None
    for l in range(depth):
        x2, h2, idx, wgt, cnt = _mixer(l, x2, pending, seq, *mixer_params, shift_sum)
        gathered = _moe(l, h2, idx, cnt, w1, w3, w2)
        pending = (gathered, wgt.T)
    out = _combine(gathered, x2, wgt, norm_final[None], True)
    return out.reshape(bsz, seq, d)
```

```python
import functools

import jax
import jax.numpy as jnp
from jax import lax
from jax.experimental import pallas as pl
from jax.experimental.pallas import tpu as pltpu
from jax.experimental.pallas import tpu_sc as plsc

D = 1024
DP = D // 2
NPIECE = DP // 128
N_HEADS_C = 8
CHUNK = 128
CONV_A = 3
CONV_B = 31
N_GROUPS = 4
EPG = 8
N_EXPERTS = N_GROUPS * EPG
D_FF = 512
RMS_EPS = 1e-6
LN_EPS = 1e-5

C_XA, C_BA, C_CA, C_VB, C_GB, C_UV, C_G = 0, 1024, 2048, 3072, 4096, 5120, 7168
D_IN = 10240

TM = 512
HALO = 16
TE = TM + 2 * HALO
CW = 256
RB = 64
PR = RB + 8
SK = -(-8 * PR // 128) * 128
NLOG = 128
E_OFF = 8

TR = TM
BM = 512
TD = 512

VMEM_LIMIT = 60 * 1024 * 1024


def _sigmoid(x):
    return 0.5 * (jnp.tanh(0.5 * x) + 1.0)


def _gelu_tanh(x):
    return 0.5 * x * (1.0 + jnp.tanh(0.7978845608028654 * (x + 0.044715 * (x * x * x))))


def _pack_bf16_pairs(x):
    c = x.shape[1] // 2
    as_bits = lambda v: lax.bitcast_convert_type(v.astype(jnp.bfloat16).astype(jnp.float32), jnp.uint32)
    return as_bits(x[:, :c]) | (as_bits(x[:, c:]) >> 16)


def _unpack_bf16_pairs(p):
    hi = lax.bitcast_convert_type(p & jnp.uint32(0xFFFF0000), jnp.float32)
    lo = lax.bitcast_convert_type(p << 16, jnp.float32)
    return hi, lo


def _layer_norm(x, g, b):
    mu = jnp.mean(x, axis=-1, keepdims=True)
    xc = x - mu
    var = jnp.mean(xc * xc, axis=-1, keepdims=True)
    return xc * lax.rsqrt(var + LN_EPS) * g + b


def _moe_residual(x_ref, g_ref, wt_ref):
    w0, w1 = wt_ref[:, 0:1], wt_ref[:, 1:2]
    his, los = [], []
    for j in range(NPIECE):
        hi0, lo0 = _unpack_bf16_pairs(lax.bitcast_convert_type(g_ref[0, j], jnp.uint32))
        hi1, lo1 = _unpack_bf16_pairs(lax.bitcast_convert_type(g_ref[1, j], jnp.uint32))
        his.append(w0 * hi0 + w1 * hi1)
        los.append(w0 * lo0 + w1 * lo1)
    return x_ref[...] + jnp.concatenate(his + los, axis=1)


def _mixer_kernel(seq_tiles, fused_combine, xp_ref, xc_ref, xn_ref, *refs):
    if fused_combine:
        gp_ref, gc_ref, gn_ref, wtp_ref, wtc_ref, wtn_ref = refs[:6]
        refs = refs[6:]
        x_prev = lambda: _moe_residual(xp_ref, gp_ref, wtp_ref)
        x_cur = lambda: _moe_residual(xc_ref, gc_ref, wtc_ref)
        x_next = lambda: _moe_residual(xn_ref, gn_ref, wtn_ref)
    else:
        x_prev, x_cur, x_next = (lambda: xp_ref[...]), (lambda: xc_ref[...]), (lambda: xn_ref[...])
    (nm_ref, win_ref, ca_ref, wa_ref, cb_ref, lnbg_ref, lnbb_ref, wb_ref, lncg_ref, lncb_ref,
     ws_ref, bsb_ref, wc_ref, wo_ref, nf_ref, wr_ref, br_ref, sh_ref,
     xo_ref, h2_ref, idx_ref, wgt_ref, cnt_ref,
     hb_ref, zq_ref, cva_ref, cvb_ref, cvn_ref, vst_ref, mg_ref, carry_ref) = refs
    i = pl.program_id(0)
    at_start = (i % seq_tiles) == 0
    at_end = (i % seq_tiles) == seq_tiles - 1

    def _rms(xv):
        ms = jnp.mean(xv * xv, axis=-1, keepdims=True)
        return (xv * lax.rsqrt(ms + RMS_EPS) * nm_ref[...]).astype(jnp.bfloat16)

    hb_ref[0:HALO, :] = _rms(x_prev())
    hb_ref[HALO:HALO + TM, :] = _rms(x_cur())
    hb_ref[HALO + TM:TE, :] = _rms(x_next())

    rows = lax.broadcasted_iota(jnp.int32, (TE, 1), 0)
    lo = jnp.where(at_start, HALO, 0)
    hi = jnp.where(at_end, HALO + TM, TE)
    valid = jnp.logical_and(rows >= lo, rows < hi)

    def proj(r0, r1, c0, width):
        return jnp.dot(hb_ref[r0:r1, :], win_ref[:, c0:c0 + width],
                       preferred_element_type=jnp.float32)

    def b_proj(c):
        c0 = c * CW
        val = proj(0, TE, C_VB + c0, CW)
        gate = proj(0, TE, C_GB + c0, CW)
        zq_ref[c] = jnp.where(valid, val * _sigmoid(gate), 0.0)

    def a_proj(c):
        c0 = c * CW
        xa = proj(0, TE, C_XA + c0, CW)
        cc = proj(0, TE, C_CA + c0, CW)
        t = jnp.where(valid, xa * cc, 0.0)
        conv = (ca_ref[0:1, c0:c0 + CW] * t[HALO - 1:HALO - 1 + TM]
                + ca_ref[1:2, c0:c0 + CW] * t[HALO:HALO + TM]
                + ca_ref[2:3, c0:c0 + CW] * t[HALO + 1:HALO + 1 + TM])
        ba = proj(HALO, HALO + TM, C_BA + c0, CW)
        cva_ref[:, c0:c0 + CW] = (ba * conv).astype(jnp.bfloat16)

    def c_v_proj(c):
        c0 = c * CW
        vst_ref[:, c0:c0 + CW] = _gelu_tanh(proj(HALO, HALO + TM, C_UV + D + c0, CW))

    def b_conv(c, row_blocks):
        c0 = c * CW
        zq = zq_ref.at[c]
        for rb in row_blocks:
            r0 = rb * RB
            for lt in range(CW // 128):
                l0 = lt * 128
                parts = []
                for s in range(8):
                    part = None
                    for q in range(4):
                        k = 8 * q + s - (HALO - CONV_B // 2)
                        if 0 <= k < CONV_B:
                            term = (cb_ref[k:k + 1, c0 + l0:c0 + l0 + 128]
                                    * zq[r0 + 8 * q:r0 + 8 * q + PR, l0:l0 + 128])
                            part = term if part is None else part + term
                    parts.append(part)
                if SK > 8 * PR:
                    parts.append(jnp.zeros((SK - 8 * PR, 128), jnp.float32))
                stacked = jnp.concatenate(parts, axis=0).astype(jnp.bfloat16)
                xo_ref[r0:r0 + RB, c0 + l0:c0 + l0 + 128] = jnp.dot(
                    sh_ref[...], stacked, preferred_element_type=jnp.float32)

    n_chunks = D // CW
    n_rb = TM // RB
    rb_step = n_rb // n_chunks
    late_rows = lambda c: range(c * rb_step, (c + 1) * rb_step)
    b_proj(0)
    for c in range(n_chunks):
        if c + 1 < n_chunks:
            b_proj(c + 1)
        a_proj(c)
        c_v_proj(c)
    b_conv(0, range(n_rb))

    vn = _layer_norm(vst_ref[...], lncg_ref[...], lncb_ref[...])
    cvn_ref[...] = vn.astype(jnp.bfloat16)
    for c in range(n_chunks):
        c0 = c * CW
        ya = jnp.dot(cva_ref[...], wa_ref[:, c0:c0 + CW], preferred_element_type=jnp.float32)
        ga = _sigmoid(proj(HALO, HALO + TM, C_G + c0, CW))
        mg_ref[:, c0:c0 + CW] = ga * ya
        b_conv(1, late_rows(c))

    for c in range(n_chunks):
        c0 = c * CW
        u = _gelu_tanh(proj(HALO, HALO + TM, C_UV + c0, CW))
        for n in range(TM // CHUNK):
            for hh in range(CW // CHUNK):
                h = c * (CW // CHUNK) + hh
                sv = jnp.dot(ws_ref[h], cvn_ref[n * CHUNK:(n + 1) * CHUNK, h * CHUNK:(h + 1) * CHUNK],
                             preferred_element_type=jnp.float32) + bsb_ref[h]
                cva_ref[n * CHUNK:(n + 1) * CHUNK, h * CHUNK:(h + 1) * CHUNK] = (
                    u[n * CHUNK:(n + 1) * CHUNK, hh * CHUNK:(hh + 1) * CHUNK] * sv
                ).astype(jnp.bfloat16)
        b_conv(2, late_rows(c))
    for c in range(n_chunks):
        c0 = c * CW
        yc = jnp.dot(cva_ref[...], wc_ref[:, c0:c0 + CW], preferred_element_type=jnp.float32)
        gc = _sigmoid(proj(HALO, HALO + TM, C_G + 2 * D + c0, CW))
        mg_ref[:, c0:c0 + CW] += gc * yc
        b_conv(3, late_rows(c))

    zc = _layer_norm(xo_ref[...], lnbg_ref[...], lnbb_ref[...])
    cvb_ref[...] = (zc * _sigmoid(zc)).astype(jnp.bfloat16)
    for c in range(n_chunks):
        c0 = c * CW
        yb = jnp.dot(cvb_ref[...], wb_ref[:, c0:c0 + CW], preferred_element_type=jnp.float32)
        gb = _sigmoid(proj(HALO, HALO + TM, C_G + D + c0, CW))
        mg_ref[:, c0:c0 + CW] += gb * yb

    cvb_ref[...] = mg_ref[...].astype(jnp.bfloat16)
    x_res = x_cur()
    for c in range(n_chunks):
        c0 = c * CW
        xo_ref[:, c0:c0 + CW] = x_res[:, c0:c0 + CW] + jnp.dot(
            cvb_ref[...], wo_ref[:, c0:c0 + CW], preferred_element_type=jnp.float32)
    xnew = xo_ref[...]
    ms = jnp.mean(xnew * xnew, axis=-1, keepdims=True)
    h2 = xnew * lax.rsqrt(ms + RMS_EPS) * nf_ref[...]
    h2_packed = _pack_bf16_pairs(h2)
    for j in range(NPIECE):
        h2_ref[j] = lax.bitcast_convert_type(h2_packed[:, j * 128:(j + 1) * 128], jnp.int32)
    h_hi = h2.astype(jnp.bfloat16)
    h_lo = (h2 - h_hi.astype(jnp.float32)).astype(jnp.bfloat16)
    dot32 = functools.partial(jnp.dot, preferred_element_type=jnp.float32)
    parts = dot32(h_hi, wr_ref[...]) + dot32(h_lo, wr_ref[...])
    logits = parts[:, 0:NLOG] + parts[:, NLOG:2 * NLOG] + br_ref[...]
    _route_tile(logits, idx_ref, wgt_ref, cnt_ref, carry_ref)


def _layer_spec(layer, shape):
    nd = len(shape)
    return pl.BlockSpec((None,) + tuple(shape), lambda i, _n=nd: (layer,) + (0,) * _n,
                        pipeline_mode=pl.Buffered(1))


def _mixer(layer, x2, pending, seq_len, nm, win, ca, wa, cb, lnbg, lnbb, wb, lncg, lncb, ws, bsb, wc, wo,
           nf, wr, br, sh):
    T = x2.shape[0]
    n_tiles = T // TM
    hb = TM // HALO
    last_halo = T // HALO - 1
    ls = functools.partial(_layer_spec, layer)
    prev_rows = lambda i: jnp.maximum(i * hb - 1, 0)
    next_rows = lambda i: jnp.minimum((i + 1) * hb, last_halo)
    in_specs = [
        pl.BlockSpec((HALO, D), lambda i: (prev_rows(i), 0)),
        pl.BlockSpec((TM, D), lambda i: (i, 0)),
        pl.BlockSpec((HALO, D), lambda i: (next_rows(i), 0)),
    ]
    operands = [x2, x2, x2]
    if pending is not None:
        gathered, wgt_rows = pending
        in_specs += [
            pl.BlockSpec((2, NPIECE, HALO, 128), lambda i: (0, 0, prev_rows(i), 0)),
            pl.BlockSpec((2, NPIECE, TM, 128), lambda i: (0, 0, i, 0)),
            pl.BlockSpec((2, NPIECE, HALO, 128), lambda i: (0, 0, next_rows(i), 0)),
            pl.BlockSpec((HALO, 8), lambda i: (prev_rows(i), 0)),
            pl.BlockSpec((TM, 8), lambda i: (i, 0)),
            pl.BlockSpec((HALO, 8), lambda i: (next_rows(i), 0)),
        ]
        operands += [gathered, gathered, gathered, wgt_rows, wgt_rows, wgt_rows]
    in_specs += [
        ls((1, D)), ls((D, D_IN)), ls((CONV_A, D)), ls((D, D)),
        ls((CONV_B, D)), ls((1, D)), ls((1, D)), ls((D, D)),
        ls((1, D)), ls((1, D)), ls((N_HEADS_C, CHUNK, CHUNK)),
        ls((N_HEADS_C, CHUNK, CHUNK)), ls((D, D)), ls((D, D)),
        ls((1, D)), ls((D, 2 * NLOG)), ls((1, NLOG)),
        pl.BlockSpec((RB, SK), lambda i: (0, 0), pipeline_mode=pl.Buffered(1)),
    ]
    out_specs = [
        pl.BlockSpec((TM, D), lambda i: (i, 0)),
        pl.BlockSpec((NPIECE, TM, 128), lambda i: (0, i, 0)),
        pl.BlockSpec((8, TM), lambda i: (0, i)),
        pl.BlockSpec((8, TM), lambda i: (0, i)),
        pl.BlockSpec((N_EXPERTS, 128), lambda i: (0, 0)),
    ]
    return pl.pallas_call(
        functools.partial(_mixer_kernel, seq_len // TM, pending is not None),
        grid=(n_tiles,),
        in_specs=in_specs,
        out_specs=out_specs,
        out_shape=[jax.ShapeDtypeStruct((T, D), jnp.float32),
                   jax.ShapeDtypeStruct((NPIECE, T, 128), jnp.int32),
                   jax.ShapeDtypeStruct((8, T), jnp.int32),
                   jax.ShapeDtypeStruct((8, T), jnp.float32),
                   jax.ShapeDtypeStruct((N_EXPERTS, 128), jnp.float32)],
        scratch_shapes=[pltpu.VMEM((TE, D), jnp.bfloat16),
                        pltpu.VMEM((D // CW, TE, CW), jnp.float32),
                        pltpu.VMEM((TM, D), jnp.bfloat16),
                        pltpu.VMEM((TM, D), jnp.bfloat16),
                        pltpu.VMEM((TM, D), jnp.bfloat16),
                        pltpu.VMEM((TM, D), jnp.float32),
                        pltpu.VMEM((TM, D), jnp.float32),
                        pltpu.VMEM((N_EXPERTS, 128), jnp.float32)],
        compiler_params=pltpu.CompilerParams(dimension_semantics=("arbitrary",),
                                             vmem_limit_bytes=VMEM_LIMIT),
        name="mixer",
    )(*operands, nm, win, ca, wa, cb, lnbg, lnbb, wb, lncg, lncb, ws, bsb, wc, wo, nf, wr, br, sh)


def _route_tile(lg, idx_ref, wgt_ref, cnt_ref, carry_ref):
    i = pl.program_id(0)

    @pl.when(i == 0)
    def _():
        carry_ref[...] = jnp.zeros_like(carry_ref)

    lt = lg.T
    g = [lt[j:j + 1, :] for j in range(N_GROUPS)]
    gmax = jnp.maximum(jnp.maximum(g[0], g[1]), jnp.maximum(g[2], g[3]))
    gidx = jnp.where(g[0] == gmax, 0.0, jnp.where(g[1] == gmax, 1.0, jnp.where(g[2] == gmax, 2.0, 3.0)))
    gsum = sum(jnp.exp(gj - gmax) for gj in g)
    g_p = 1.0 / gsum

    sel = lt[E_OFF + 3 * EPG:E_OFF + 4 * EPG, :]
    for j in (2, 1, 0):
        sel = jnp.where(gidx == float(j), lt[E_OFF + j * EPG:E_OFF + (j + 1) * EPG, :], sel)
    rid = lax.broadcasted_iota(jnp.int32, (EPG, TR), 0).astype(jnp.float32)
    m1 = jnp.max(sel, axis=0, keepdims=True)
    i1 = jnp.min(jnp.where(sel == m1, rid, float(EPG)), axis=0, keepdims=True)
    rest = jnp.where(rid == i1, -jnp.inf, sel)
    m2 = jnp.max(rest, axis=0, keepdims=True)
    i2 = jnp.min(jnp.where(rest == m2, rid, float(EPG)), axis=0, keepdims=True)
    e2x = jnp.exp(m2 - m1)
    den = 1.0 + e2x
    w1 = (1.0 / den) * g_p
    w2 = (e2x / den) * g_p
    e1 = gidx * float(EPG) + i1
    e2 = gidx * float(EPG) + i2

    eid = lax.broadcasted_iota(jnp.int32, (N_EXPERTS, TR), 0).astype(jnp.float32)
    oh1 = (eid == e1).astype(jnp.float32)
    oh2 = (eid == e2).astype(jnp.float32)
    oh = oh1 + oh2
    tr = lax.broadcasted_iota(jnp.int32, (TR, TR), 0)
    tc = lax.broadcasted_iota(jnp.int32, (TR, TR), 1)
    upper = (tr < tc).astype(jnp.bfloat16)
    before = jnp.dot(oh.astype(jnp.bfloat16), upper, preferred_element_type=jnp.float32)
    base = before + carry_ref[:, 0:1]
    r1 = jnp.sum(oh1 * base, axis=0, keepdims=True)
    r2 = jnp.sum(oh2 * base, axis=0, keepdims=True)
    carry_ref[...] = carry_ref[...] + jnp.sum(oh, axis=1, keepdims=True)

    idx_ref[...] = jnp.zeros_like(idx_ref)
    idx_ref[0:1, :] = e1.astype(jnp.int32)
    idx_ref[1:2, :] = e2.astype(jnp.int32)
    idx_ref[2:3, :] = r1.astype(jnp.int32)
    idx_ref[3:4, :] = r2.astype(jnp.int32)
    wgt_ref[...] = jnp.zeros_like(wgt_ref)
    wgt_ref[0:1, :] = w1
    wgt_ref[1:2, :] = w2
    cnt_ref[...] = carry_ref[...]


SC_WINDOW = 128


def _sc_mesh():
    return plsc.VectorSubcoreMesh(core_axis_name="c", subcore_axis_name="s")


def _sc_scatter_rows(rows, idx, passes, n_out):
    n_src = rows.shape[0]
    src_blocks = n_src // SC_WINDOW

    @functools.partial(
        pl.kernel, mesh=_sc_mesh(),
        out_type=jax.ShapeDtypeStruct((n_out, 128), rows.dtype))
    def scatter(rows_hbm, idx_hbm, out_hbm):
        def body(rows_vmem, idx_vmem):
            pltpu.sync_copy(rows_vmem, out_hbm.at[idx_vmem.at[0]])

        pltpu.emit_pipeline(
            body,
            grid=(passes * src_blocks,),
            in_specs=[pl.BlockSpec((SC_WINDOW, 128), lambda i: (i % src_blocks, 0)),
                      pl.BlockSpec((1, SC_WINDOW), lambda i: (0, i))],
            out_specs=[],
            core_axis_name=("c", "s"),
            dimension_semantics=(pltpu.PARALLEL,),
        )(rows_hbm, idx_hbm)

    return scatter(rows, idx.reshape(1, passes * n_src))


def _expert_kernel(layer, be_ref, nu_ref, nxt_ref, gs_ref, nv_ref, xb_ref, w1_ref, w3_ref, w2_ref, yb_ref,
                   wf1_ref, wf3_ref, wf2_ref, w1b_ref, w3b_ref, w2b_ref, sem):
    b = pl.program_id(0)
    used = b < nu_ref[0]
    new_expert = jnp.logical_or(b == 0, be_ref[b] != be_ref[jnp.maximum(b - 1, 0)])

    def weight_copies(e, s):
        return [pltpu.make_async_copy(src.at[layer, e], dst.at[s], sem.at[s])
                for src, dst in ((w1_ref, wf1_ref), (w3_ref, wf3_ref), (w2_ref, wf2_ref))]

    @pl.when(b == 0)
    def _():
        for cp in weight_copies(be_ref[0], 0):
            cp.start(priority=1)

    @pl.when(jnp.logical_and(used, new_expert))
    def _():
        s = gs_ref[b]
        for cp in weight_copies(be_ref[b], s):
            cp.wait()

        @pl.when(nxt_ref[b] >= 0)
        def _():
            for cp in weight_copies(nxt_ref[b], 1 - s):
                cp.start(priority=1)

        w1b_ref[...] = wf1_ref[s].astype(jnp.bfloat16)
        w3b_ref[...] = wf3_ref[s].astype(jnp.bfloat16)
        w2b_ref[...] = wf2_ref[s].astype(jnp.bfloat16)

    def expert_mlp(rows):
        live = lax.broadcasted_iota(jnp.int32, (rows, 1), 0) < nv_ref[b]
        packed_in = jnp.concatenate(
            [jnp.where(live, lax.bitcast_convert_type(xb_ref[j, 0:rows], jnp.uint32), jnp.uint32(0))
             for j in range(NPIECE)], axis=1)
        x = jnp.concatenate(_unpack_bf16_pairs(packed_in), axis=1).astype(jnp.bfloat16)
        dot32 = functools.partial(jnp.dot, preferred_element_type=jnp.float32)
        a = dot32(x, w1b_ref[...])
        g = dot32(x, w3b_ref[...])
        hmid = (a * _sigmoid(a) * g).astype(jnp.bfloat16)
        packed = _pack_bf16_pairs(dot32(hmid, w2b_ref[...]))
        for j in range(NPIECE):
            yb_ref[j, 0:rows] = lax.bitcast_convert_type(packed[:, j * 128:(j + 1) * 128], jnp.int32)

    half = BM // 2
    half_full = nv_ref[b] <= half

    @pl.when(jnp.logical_and(used, jnp.logical_not(half_full)))
    def _():
        expert_mlp(BM)

    @pl.when(jnp.logical_and(used, half_full))
    def _():
        expert_mlp(half)
        yb_ref[:, half:BM, :] = jnp.zeros((NPIECE, BM - half, 128), jnp.int32)

    @pl.when(jnp.logical_not(used))
    def _():
        yb_ref[...] = jnp.zeros_like(yb_ref)


def _experts(layer, block_expert, n_used, next_expert, group_slot, block_valid, xb, w1, w3, w2):
    n_slots = xb.shape[1]
    n_blocks = n_slots // BM

    def row_map(b, be, nu, nxt, gs, nv):
        return (0, jnp.minimum(b, nu[0] - 1), 0)

    def out_map(b, be, nu, nxt, gs, nv):
        return (0, b, 0)

    grid_spec = pltpu.PrefetchScalarGridSpec(
        num_scalar_prefetch=5,
        grid=(n_blocks,),
        in_specs=[pl.BlockSpec((NPIECE, BM, 128), row_map),
                  pl.BlockSpec(memory_space=pl.ANY),
                  pl.BlockSpec(memory_space=pl.ANY),
                  pl.BlockSpec(memory_space=pl.ANY)],
        out_specs=pl.BlockSpec((NPIECE, BM, 128), out_map),
        scratch_shapes=[pltpu.VMEM((2, D, D_FF), jnp.float32),
                        pltpu.VMEM((2, D, D_FF), jnp.float32),
                        pltpu.VMEM((2, D_FF, D), jnp.float32),
                        pltpu.VMEM((D, D_FF), jnp.bfloat16),
                        pltpu.VMEM((D, D_FF), jnp.bfloat16),
                        pltpu.VMEM((D_FF, D), jnp.bfloat16),
                        pltpu.SemaphoreType.DMA((2,))],
    )
    return pl.pallas_call(
        functools.partial(_expert_kernel, layer),
        grid_spec=grid_spec,
        out_shape=jax.ShapeDtypeStruct((NPIECE, n_slots, 128), jnp.int32),
        compiler_params=pltpu.CompilerParams(dimension_semantics=("arbitrary",),
                                             vmem_limit_bytes=VMEM_LIMIT),
        name="experts",
    )(block_expert, n_used, next_expert, group_slot, block_valid, xb, w1, w3, w2)


def _sc_gather_rows(table, idx):
    n_rows = idx.shape[0]

    @functools.partial(
        pl.kernel, mesh=_sc_mesh(),
        out_type=jax.ShapeDtypeStruct((n_rows, 128), table.dtype))
    def gather(table_hbm, idx_hbm, out_hbm):
        def body(idx_vmem, out_vmem):
            pltpu.sync_copy(table_hbm.at[idx_vmem.at[0]], out_vmem)

        pltpu.emit_pipeline(
            body,
            grid=(n_rows // SC_WINDOW,),
            in_specs=[pl.BlockSpec((1, SC_WINDOW), lambda i: (0, i))],
            out_specs=[pl.BlockSpec((SC_WINDOW, 128), lambda i: (i, 0))],
            core_axis_name=("c", "s"),
            dimension_semantics=(pltpu.PARALLEL,),
        )(idx_hbm, out_hbm)

    return gather(table, idx.reshape(1, n_rows))


def _combine_kernel(final, g_ref, x_ref, w_ref, nrm_ref, o_ref):
    w = w_ref[...].T
    ys = {}
    for j in range(NPIECE):
        hi0, lo0 = _unpack_bf16_pairs(lax.bitcast_convert_type(g_ref[0, j], jnp.uint32))
        hi1, lo1 = _unpack_bf16_pairs(lax.bitcast_convert_type(g_ref[1, j], jnp.uint32))
        c_hi, c_lo = j * 128, DP + j * 128
        ys[c_hi] = x_ref[:, c_hi:c_hi + 128] + w[:, 0:1] * hi0 + w[:, 1:2] * hi1
        ys[c_lo] = x_ref[:, c_lo:c_lo + 128] + w[:, 0:1] * lo0 + w[:, 1:2] * lo1
    if final:
        ms = sum(jnp.sum(y * y, axis=-1, keepdims=True) for y in ys.values()) * (1.0 / D)
        scale = lax.rsqrt(ms + RMS_EPS)
        ys = {c: y * scale * nrm_ref[:, c:c + 128] for c, y in ys.items()}
    for c, y in ys.items():
        o_ref[:, c:c + 128] = y


def _combine(gathered, x2, wgt, nrm, final):
    T = x2.shape[0]
    return pl.pallas_call(
        functools.partial(_combine_kernel, final),
        grid=(T // TD,),
        in_specs=[pl.BlockSpec((2, NPIECE, TD, 128), lambda i: (0, 0, i, 0)),
                  pl.BlockSpec((TD, D), lambda i: (i, 0)),
                  pl.BlockSpec((8, TD), lambda i: (0, i)),
                  pl.BlockSpec((1, D), lambda i: (0, 0))],
        out_specs=pl.BlockSpec((TD, D), lambda i: (i, 0)),
        out_shape=jax.ShapeDtypeStruct((T, D), jnp.float32),
        compiler_params=pltpu.CompilerParams(dimension_semantics=("arbitrary",)),
        name="combine",
    )(gathered, x2, wgt, nrm)


def _moe(layer, h2, idx, cnt, w1, w3, w2):
    T = idx.shape[1]
    counts = cnt[:, 0].astype(jnp.int32)
    padded = (counts + BM - 1) // BM * BM
    padded_end = jnp.cumsum(padded)
    padded_start = padded_end - padded
    eids = jnp.arange(N_EXPERTS, dtype=jnp.int32)[:, None]
    start_of = lambda e: jnp.sum(jnp.where(e[None, :] == eids, padded_start[:, None], 0), axis=0)
    dest = jnp.stack([start_of(idx[0]) + idx[2], start_of(idx[1]) + idx[3]])
    n_blocks = (2 * T) // BM + N_EXPERTS
    block_start = jnp.arange(n_blocks, dtype=jnp.int32) * BM
    block_expert = jnp.minimum(
        jnp.sum((padded_end[None, :] <= block_start[:, None]).astype(jnp.int32), axis=1),
        N_EXPERTS - 1)
    n_used = (padded_end[-1] // BM).reshape(1)
    later = jnp.logical_and(eids.T > eids, (counts > 0)[None, :])
    next_of = jnp.min(jnp.where(later, eids.T, N_EXPERTS), axis=1)
    next_of = jnp.where(next_of < N_EXPERTS, next_of, -1)
    pick = lambda table: jnp.sum(
        jnp.where(block_expert[:, None] == eids.T, table[None, :], 0), axis=1)
    next_expert = pick(next_of)
    group_slot = pick(jnp.cumsum((counts > 0).astype(jnp.int32)) - 1) % 2
    block_valid = jnp.clip(pick(padded_start + counts) - block_start, 0, BM)
    n_slots = n_blocks * BM
    piece_base = (jnp.arange(NPIECE, dtype=jnp.int32) * n_slots)[None, :, None]
    rows = (dest[:, None, :] + piece_base).reshape(-1)
    xb = _sc_scatter_rows(h2.reshape(NPIECE * T, 128), rows, 2, NPIECE * n_slots)
    yb = _experts(layer, block_expert, n_used, next_expert, group_slot, block_valid,
                  xb.reshape(NPIECE, n_slots, 128), w1, w3, w2)
    gathered = _sc_gather_rows(yb.reshape(NPIECE * n_slots, 128), rows)
    return gathered.reshape(2, NPIECE, T, 128)


def kernel(x, norm_mix, w_in, conv_a, w_a_out, conv_b, ln_b_g, ln_b_b, w_b_out, ln_c_g, ln_c_b,
           w_s, b_s, w_c_out, w_o, norm_ffn, w_group, b_group, w_router, b_router, w1, w3, w2,
           norm_final):
    bsz, seq, d = x.shape
    depth = norm_mix.shape[0]
    bf = jnp.bfloat16
    x2 = x.reshape(bsz * seq, d)
    wr = jnp.zeros((depth, d, NLOG), jnp.float32)
    wr = wr.at[:, :, 0:N_GROUPS].set(w_group).at[:, :, E_OFF:E_OFF + N_EXPERTS].set(w_router)
    wr_hi = wr.astype(bf)
    wr_lo = (wr - wr_hi.astype(jnp.float32)).astype(bf)
    br = jnp.zeros((depth, 1, NLOG), jnp.float32)
    br = br.at[:, 0, 0:N_GROUPS].set(b_group).at[:, 0, E_OFF:E_OFF + N_EXPERTS].set(b_router)
    bsb = jnp.broadcast_to(b_s[:, :, :, None], (depth, N_HEADS_C, CHUNK, CHUNK))
    row = lambda p: p[:, None, :]
    mixer_params = (row(norm_mix), w_in.astype(bf), conv_a, w_a_out.astype(bf), conv_b,
                    row(ln_b_g), row(ln_b_b), w_b_out.astype(bf), row(ln_c_g), row(ln_c_b),
                    w_s.astype(bf), bsb, w_c_out.astype(bf), w_o.astype(bf), row(norm_ffn),
                    jnp.concatenate([wr_hi, wr_lo], axis=2), br)
    col = jnp.arange(SK, dtype=jnp.int32)[None, :]
    shift_sum = jnp.logical_and(
        col % PR == jnp.arange(RB, dtype=jnp.int32)[:, None] + col // PR, col < 8 * PR).astype(bf)
    pending = None
    for l in range(depth):
        x2, h2, idx, wgt, cnt = _mixer(l, x2, pending, seq, *mixer_params, shift_sum)
        gathered = _moe(l, h2, idx, cnt, w1, w3, w2)
        pending = (gathered, wgt.T)
    out = _combine(gathered, x2, wgt, norm_final[None], True)
    return out.reshape(bsz, seq, d)
```

```python
import functools

import jax
import jax.numpy as jnp
from jax import lax
from jax.experimental import pallas as pl
from jax.experimental.pallas import tpu as pltpu
from jax.experimental.pallas import tpu_sc as plsc

D = 1024
DP = D // 2
NPIECE = DP // 128
N_HEADS_C = 8
CHUNK = 128
CONV_A = 3
CONV_B = 31
N_GROUPS = 4
EPG = 8
N_EXPERTS = N_GROUPS * EPG
D_FF = 512
RMS_EPS = 1e-6
LN_EPS = 1e-5

C_XA, C_BA, C_CA, C_VB, C_GB, C_UV, C_G = 0, 1024, 2048, 3072, 4096, 5120, 7168
D_IN = 10240

TM = 512
HALO = 16
TE = TM + 2 * HALO
CW = 256
RB = 64
PR = RB + 8
SK = -(-8 * PR // 128) * 128
NLOG = 128
E_OFF = 8

TR = TM
BM = 512
TD = 512

VMEM_LIMIT = 60 * 1024 * 1024


def _sigmoid(x):
    return 0.5 * (jnp.tanh(0.5 * x) + 1.0)


def _gelu_tanh(x):
    return 0.5 * x * (1.0 + jnp.tanh(0.7978845608028654 * (x + 0.044715 * (x * x * x))))


def _pack_bf16_pairs(x):
    c = x.shape[1] // 2
    as_bits = lambda v: lax.bitcast_convert_type(v.astype(jnp.bfloat16).astype(jnp.float32), jnp.uint32)
    return as_bits(x[:, :c]) | (as_bits(x[:, c:]) >> 16)


def _unpack_bf16_pairs(p):
    hi = lax.bitcast_convert_type(p & jnp.uint32(0xFFFF0000), jnp.float32)
    lo = lax.bitcast_convert_type(p << 16, jnp.float32)
    return hi, lo


def _layer_norm(x, g, b):
    mu = jnp.mean(x, axis=-1, keepdims=True)
    xc = x - mu
    var = jnp.mean(xc * xc, axis=-1, keepdims=True)
    return xc * lax.rsqrt(var + LN_EPS) * g + b


def _moe_residual(x_ref, g_ref, wt_ref):
    w0, w1 = wt_ref[:, 0:1], wt_ref[:, 1:2]
    his, los = [], []
    for j in range(NPIECE):
        hi0, lo0 = _unpack_bf16_pairs(lax.bitcast_convert_type(g_ref[0, j], jnp.uint32))
        hi1, lo1 = _unpack_bf16_pairs(lax.bitcast_convert_type(g_ref[1, j], jnp.uint32))
        his.append(w0 * hi0 + w1 * hi1)
        los.append(w0 * lo0 + w1 * lo1)
    return x_ref[...] + jnp.concatenate(his + los, axis=1)


def _mixer_kernel(seq_tiles, fused_combine, xp_ref, xc_ref, xn_ref, *refs):
    if fused_combine:
        gp_ref, gc_ref, gn_ref, wtp_ref, wtc_ref, wtn_ref = refs[:6]
        refs = refs[6:]
        x_prev = lambda: _moe_residual(xp_ref, gp_ref, wtp_ref)
        x_cur = lambda: _moe_residual(xc_ref, gc_ref, wtc_ref)
        x_next = lambda: _moe_residual(xn_ref, gn_ref, wtn_ref)
    else:
        x_prev, x_cur, x_next = (lambda: xp_ref[...]), (lambda: xc_ref[...]), (lambda: xn_ref[...])
    (nm_ref, win_ref, ca_ref, wa_ref, cb_ref, lnbg_ref, lnbb_ref, wb_ref, lncg_ref, lncb_ref,
     ws_ref, bsb_ref, wc_ref, wo_ref, nf_ref, wrh_ref, wrl_ref, br_ref, sh_ref,
     xo_ref, h2_ref, idx_ref, wgt_ref, cnt_ref,
     hb_ref, zq_ref, cva_ref, cvb_ref, cvn_ref, vst_ref, mg_ref, carry_ref) = refs
    i = pl.program_id(0)
    at_start = (i % seq_tiles) == 0
    at_end = (i % seq_tiles) == seq_tiles - 1

    def _rms(xv):
        ms = jnp.mean(xv * xv, axis=-1, keepdims=True)
        return (xv * lax.rsqrt(ms + RMS_EPS) * nm_ref[...]).astype(jnp.bfloat16)

    hb_ref[0:HALO, :] = _rms(x_prev())
    hb_ref[HALO:HALO + TM, :] = _rms(x_cur())
    hb_ref[HALO + TM:TE, :] = _rms(x_next())

    rows = lax.broadcasted_iota(jnp.int32, (TE, 1), 0)
    lo = jnp.where(at_start, HALO, 0)
    hi = jnp.where(at_end, HALO + TM, TE)
    valid = jnp.logical_and(rows >= lo, rows < hi)

    def proj(r0, r1, c0, width):
        return jnp.dot(hb_ref[r0:r1, :], win_ref[:, c0:c0 + width],
                       preferred_element_type=jnp.float32)

    def b_proj(c):
        c0 = c * CW
        val = proj(0, TE, C_VB + c0, CW)
        gate = proj(0, TE, C_GB + c0, CW)
        zq_ref[c] = jnp.where(valid, val * _sigmoid(gate), 0.0)

    def a_proj(c):
        c0 = c * CW
        xa = proj(0, TE, C_XA + c0, CW)
        cc = proj(0, TE, C_CA + c0, CW)
        t = jnp.where(valid, xa * cc, 0.0)
        conv = (ca_ref[0:1, c0:c0 + CW] * t[HALO - 1:HALO - 1 + TM]
                + ca_ref[1:2, c0:c0 + CW] * t[HALO:HALO + TM]
                + ca_ref[2:3, c0:c0 + CW] * t[HALO + 1:HALO + 1 + TM])
        ba = proj(HALO, HALO + TM, C_BA + c0, CW)
        cva_ref[:, c0:c0 + CW] = (ba * conv).astype(jnp.bfloat16)

    def c_v_proj(c):
        c0 = c * CW
        vst_ref[:, c0:c0 + CW] = _gelu_tanh(proj(HALO, HALO + TM, C_UV + D + c0, CW))

    def b_conv(c, row_blocks):
        c0 = c * CW
        zq = zq_ref.at[c]
        for rb in row_blocks:
            r0 = rb * RB
            for lt in range(CW // 128):
                l0 = lt * 128
                parts = []
                for s in range(8):
                    part = None
                    for q in range(4):
                        k = 8 * q + s - (HALO - CONV_B // 2)
                        if 0 <= k < CONV_B:
                            term = (cb_ref[k:k + 1, c0 + l0:c0 + l0 + 128]
                                    * zq[r0 + 8 * q:r0 + 8 * q + PR, l0:l0 + 128])
                            part = term if part is None else part + term
                    parts.append(part)
                if SK > 8 * PR:
                    parts.append(jnp.zeros((SK - 8 * PR, 128), jnp.float32))
                stacked = jnp.concatenate(parts, axis=0).astype(jnp.bfloat16)
                xo_ref[r0:r0 + RB, c0 + l0:c0 + l0 + 128] = jnp.dot(
                    sh_ref[...], stacked, preferred_element_type=jnp.float32)

    n_chunks = D // CW
    n_rb = TM // RB
    rb_step = n_rb // n_chunks
    late_rows = lambda c: range(c * rb_step, (c + 1) * rb_step)
    b_proj(0)
    for c in range(n_chunks):
        if c + 1 < n_chunks:
            b_proj(c + 1)
        a_proj(c)
        c_v_proj(c)
    b_conv(0, range(n_rb))

    vn = _layer_norm(vst_ref[...], lncg_ref[...], lncb_ref[...])
    cvn_ref[...] = vn.astype(jnp.bfloat16)
    for c in range(n_chunks):
        c0 = c * CW
        ya = jnp.dot(cva_ref[...], wa_ref[:, c0:c0 + CW], preferred_element_type=jnp.float32)
        ga = _sigmoid(proj(HALO, HALO + TM, C_G + c0, CW))
        mg_ref[:, c0:c0 + CW] = ga * ya
        b_conv(1, late_rows(c))

    for c in range(n_chunks):
        c0 = c * CW
        u = _gelu_tanh(proj(HALO, HALO + TM, C_UV + c0, CW))
        for n in range(TM // CHUNK):
            for hh in range(CW // CHUNK):
                h = c * (CW // CHUNK) + hh
                sv = jnp.dot(ws_ref[h], cvn_ref[n * CHUNK:(n + 1) * CHUNK, h * CHUNK:(h + 1) * CHUNK],
                             preferred_element_type=jnp.float32) + bsb_ref[h]
                cva_ref[n * CHUNK:(n + 1) * CHUNK, h * CHUNK:(h + 1) * CHUNK] = (
                    u[n * CHUNK:(n + 1) * CHUNK, hh * CHUNK:(hh + 1) * CHUNK] * sv
                ).astype(jnp.bfloat16)
        b_conv(2, late_rows(c))
    for c in range(n_chunks):
        c0 = c * CW
        yc = jnp.dot(cva_ref[...], wc_ref[:, c0:c0 + CW], preferred_element_type=jnp.float32)
        gc = _sigmoid(proj(HALO, HALO + TM, C_G + 2 * D + c0, CW))
        mg_ref[:, c0:c0 + CW] += gc * yc
        b_conv(3, late_rows(c))

    zc = _layer_norm(xo_ref[...], lnbg_ref[...], lnbb_ref[...])
    cvb_ref[...] = (zc * _sigmoid(zc)).astype(jnp.bfloat16)
    for c in range(n_chunks):
        c0 = c * CW
        yb = jnp.dot(cvb_ref[...], wb_ref[:, c0:c0 + CW], preferred_element_type=jnp.float32)
        gb = _sigmoid(proj(HALO, HALO + TM, C_G + D + c0, CW))
        mg_ref[:, c0:c0 + CW] += gb * yb

    cvb_ref[...] = mg_ref[...].astype(jnp.bfloat16)
    x_res = x_cur()
    for c in range(n_chunks):
        c0 = c * CW
        xo_ref[:, c0:c0 + CW] = x_res[:, c0:c0 + CW] + jnp.dot(
            cvb_ref[...], wo_ref[:, c0:c0 + CW], preferred_element_type=jnp.float32)
    xnew = xo_ref[...]
    ms = jnp.mean(xnew * xnew, axis=-1, keepdims=True)
    h2 = xnew * lax.rsqrt(ms + RMS_EPS) * nf_ref[...]
    h2_packed = _pack_bf16_pairs(h2)
    for j in range(NPIECE):
        h2_ref[j] = lax.bitcast_convert_type(h2_packed[:, j * 128:(j + 1) * 128], jnp.int32)
    h_hi = h2.astype(jnp.bfloat16)
    h_lo = (h2 - h_hi.astype(jnp.float32)).astype(jnp.bfloat16)
    dot32 = functools.partial(jnp.dot, preferred_element_type=jnp.float32)
    logits = (dot32(h_hi, wrh_ref[...]) + dot32(h_lo, wrh_ref[...])
              + dot32(h_hi, wrl_ref[...]) + dot32(h_lo, wrl_ref[...]) + br_ref[...])
    _route_tile(logits, idx_ref, wgt_ref, cnt_ref, carry_ref)


def _layer_spec(layer, shape):
    nd = len(shape)
    return pl.BlockSpec((None,) + tuple(shape), lambda i, _n=nd: (layer,) + (0,) * _n,
                        pipeline_mode=pl.Buffered(1))


def _mixer(layer, x2, pending, seq_len, nm, win, ca, wa, cb, lnbg, lnbb, wb, lncg, lncb, ws, bsb, wc, wo,
           nf, wrh, wrl, br, sh):
    T = x2.shape[0]
    n_tiles = T // TM
    hb = TM // HALO
    last_halo = T // HALO - 1
    ls = functools.partial(_layer_spec, layer)
    prev_rows = lambda i: jnp.maximum(i * hb - 1, 0)
    next_rows = lambda i: jnp.minimum((i + 1) * hb, last_halo)
    in_specs = [
        pl.BlockSpec((HALO, D), lambda i: (prev_rows(i), 0)),
        pl.BlockSpec((TM, D), lambda i: (i, 0)),
        pl.BlockSpec((HALO, D), lambda i: (next_rows(i), 0)),
    ]
    operands = [x2, x2, x2]
    if pending is not None:
        gathered, wgt_rows = pending
        in_specs += [
            pl.BlockSpec((2, NPIECE, HALO, 128), lambda i: (0, 0, prev_rows(i), 0)),
            pl.BlockSpec((2, NPIECE, TM, 128), lambda i: (0, 0, i, 0)),
            pl.BlockSpec((2, NPIECE, HALO, 128), lambda i: (0, 0, next_rows(i), 0)),
            pl.BlockSpec((HALO, 8), lambda i: (prev_rows(i), 0)),
            pl.BlockSpec((TM, 8), lambda i: (i, 0)),
            pl.BlockSpec((HALO, 8), lambda i: (next_rows(i), 0)),
        ]
        operands += [gathered, gathered, gathered, wgt_rows, wgt_rows, wgt_rows]
    in_specs += [
        ls((1, D)), ls((D, D_IN)), ls((CONV_A, D)), ls((D, D)),
        ls((CONV_B, D)), ls((1, D)), ls((1, D)), ls((D, D)),
        ls((1, D)), ls((1, D)), ls((N_HEADS_C, CHUNK, CHUNK)),
        ls((N_HEADS_C, CHUNK, CHUNK)), ls((D, D)), ls((D, D)),
        ls((1, D)), ls((D, NLOG)), ls((D, NLOG)), ls((1, NLOG)),
        pl.BlockSpec((RB, SK), lambda i: (0, 0), pipeline_mode=pl.Buffered(1)),
    ]
    out_specs = [
        pl.BlockSpec((TM, D), lambda i: (i, 0)),
        pl.BlockSpec((NPIECE, TM, 128), lambda i: (0, i, 0)),
        pl.BlockSpec((8, TM), lambda i: (0, i)),
        pl.BlockSpec((8, TM), lambda i: (0, i)),
        pl.BlockSpec((N_EXPERTS, 128), lambda i: (0, 0)),
    ]
    return pl.pallas_call(
        functools.partial(_mixer_kernel, seq_len // TM, pending is not None),
        grid=(n_tiles,),
        in_specs=in_specs,
        out_specs=out_specs,
        out_shape=[jax.ShapeDtypeStruct((T, D), jnp.float32),
                   jax.ShapeDtypeStruct((NPIECE, T, 128), jnp.int32),
                   jax.ShapeDtypeStruct((8, T), jnp.int32),
                   jax.ShapeDtypeStruct((8, T), jnp.float32),
                   jax.ShapeDtypeStruct((N_EXPERTS, 128), jnp.float32)],
        scratch_shapes=[pltpu.VMEM((TE, D), jnp.bfloat16),
                        pltpu.VMEM((D // CW, TE, CW), jnp.float32),
                        pltpu.VMEM((TM, D), jnp.bfloat16),
                        pltpu.VMEM((TM, D), jnp.bfloat16),
                        pltpu.VMEM((TM, D), jnp.bfloat16),
                        pltpu.VMEM((TM, D), jnp.float32),
                        pltpu.VMEM((TM, D), jnp.float32),
                        pltpu.VMEM((N_EXPERTS, 128), jnp.float32)],
        compiler_params=pltpu.CompilerParams(dimension_semantics=("arbitrary",),
                                             vmem_limit_bytes=VMEM_LIMIT),
        name="mixer",
    )(*operands, nm, win, ca, wa, cb, lnbg, lnbb, wb, lncg, lncb, ws, bsb, wc, wo, nf, wrh, wrl, br, sh)


def _route_tile(lg, idx_ref, wgt_ref, cnt_ref, carry_ref):
    i = pl.program_id(0)

    @pl.when(i == 0)
    def _():
        carry_ref[...] = jnp.zeros_like(carry_ref)

    lt = lg.T
    g = [lt[j:j + 1, :] for j in range(N_GROUPS)]
    gmax = jnp.maximum(jnp.maximum(g[0], g[1]), jnp.maximum(g[2], g[3]))
    gidx = jnp.where(g[0] == gmax, 0.0, jnp.where(g[1] == gmax, 1.0, jnp.where(g[2] == gmax, 2.0, 3.0)))
    gsum = sum(jnp.exp(gj - gmax) for gj in g)
    g_p = 1.0 / gsum

    sel = lt[E_OFF + 3 * EPG:E_OFF + 4 * EPG, :]
    for j in (2, 1, 0):
        sel = jnp.where(gidx == float(j), lt[E_OFF + j * EPG:E_OFF + (j + 1) * EPG, :], sel)
    rid = lax.broadcasted_iota(jnp.int32, (EPG, TR), 0).astype(jnp.float32)
    m1 = jnp.max(sel, axis=0, keepdims=True)
    i1 = jnp.min(jnp.where(sel == m1, rid, float(EPG)), axis=0, keepdims=True)
    rest = jnp.where(rid == i1, -jnp.inf, sel)
    m2 = jnp.max(rest, axis=0, keepdims=True)
    i2 = jnp.min(jnp.where(rest == m2, rid, float(EPG)), axis=0, keepdims=True)
    e2x = jnp.exp(m2 - m1)
    den = 1.0 + e2x
    w1 = (1.0 / den) * g_p
    w2 = (e2x / den) * g_p
    e1 = gidx * float(EPG) + i1
    e2 = gidx * float(EPG) + i2

    eid = lax.broadcasted_iota(jnp.int32, (N_EXPERTS, TR), 0).astype(jnp.float32)
    oh1 = (eid == e1).astype(jnp.float32)
    oh2 = (eid == e2).astype(jnp.float32)
    oh = oh1 + oh2
    tr = lax.broadcasted_iota(jnp.int32, (TR, TR), 0)
    tc = lax.broadcasted_iota(jnp.int32, (TR, TR), 1)
    upper = (tr < tc).astype(jnp.bfloat16)
    before = jnp.dot(oh.astype(jnp.bfloat16), upper, preferred_element_type=jnp.float32)
    base = before + carry_ref[:, 0:1]
    r1 = jnp.sum(oh1 * base, axis=0, keepdims=True)
    r2 = jnp.sum(oh2 * base, axis=0, keepdims=True)
    carry_ref[...] = carry_ref[...] + jnp.sum(oh, axis=1, keepdims=True)

    idx_ref[...] = jnp.zeros_like(idx_ref)
    idx_ref[0:1, :] = e1.astype(jnp.int32)
    idx_ref[1:2, :] = e2.astype(jnp.int32)
    idx_ref[2:3, :] = r1.astype(jnp.int32)
    idx_ref[3:4, :] = r2.astype(jnp.int32)
    wgt_ref[...] = jnp.zeros_like(wgt_ref)
    wgt_ref[0:1, :] = w1
    wgt_ref[1:2, :] = w2
    cnt_ref[...] = carry_ref[...]


SC_WINDOW = 128


def _sc_mesh():
    return plsc.VectorSubcoreMesh(core_axis_name="c", subcore_axis_name="s")


def _sc_scatter_rows(rows, idx, passes, n_out):
    n_src = rows.shape[0]
    idx_lists = [idx[p * n_src:(p + 1) * n_src].reshape(1, n_src) for p in range(passes)]

    @functools.partial(
        pl.kernel, mesh=_sc_mesh(),
        out_type=jax.ShapeDtypeStruct((n_out, 128), rows.dtype))
    def scatter(rows_hbm, *refs):
        idx_hbms, out_hbm = refs[:passes], refs[passes]

        def body(rows_vmem, *idx_vmems):
            for idx_vmem in idx_vmems:
                pltpu.sync_copy(rows_vmem, out_hbm.at[idx_vmem.at[0]])

        pltpu.emit_pipeline(
            body,
            grid=(n_src // SC_WINDOW,),
            in_specs=[pl.BlockSpec((SC_WINDOW, 128), lambda i: (i, 0))]
            + [pl.BlockSpec((1, SC_WINDOW), lambda i: (0, i))] * passes,
            out_specs=[],
            core_axis_name=("c", "s"),
            dimension_semantics=(pltpu.PARALLEL,),
        )(rows_hbm, *idx_hbms)

    return scatter(rows, *idx_lists)


def _expert_kernel(layer, be_ref, nu_ref, nxt_ref, gs_ref, nv_ref, xb_ref, w1_ref, w3_ref, w2_ref, yb_ref,
                   wf1_ref, wf3_ref, wf2_ref, w1b_ref, w3b_ref, w2b_ref, sem):
    b = pl.program_id(0)
    used = b < nu_ref[0]
    new_expert = jnp.logical_or(b == 0, be_ref[b] != be_ref[jnp.maximum(b - 1, 0)])

    def weight_copies(e, s):
        return [pltpu.make_async_copy(src.at[layer, e], dst.at[s], sem.at[s])
                for src, dst in ((w1_ref, wf1_ref), (w3_ref, wf3_ref), (w2_ref, wf2_ref))]

    @pl.when(b == 0)
    def _():
        for cp in weight_copies(be_ref[0], 0):
            cp.start(priority=1)

    @pl.when(jnp.logical_and(used, new_expert))
    def _():
        s = gs_ref[b]
        for cp in weight_copies(be_ref[b], s):
            cp.wait()

        @pl.when(nxt_ref[b] >= 0)
        def _():
            for cp in weight_copies(nxt_ref[b], 1 - s):
                cp.start(priority=1)

        w1b_ref[...] = wf1_ref[s].astype(jnp.bfloat16)
        w3b_ref[...] = wf3_ref[s].astype(jnp.bfloat16)
        w2b_ref[...] = wf2_ref[s].astype(jnp.bfloat16)

    def expert_mlp(rows):
        live = lax.broadcasted_iota(jnp.int32, (rows, 1), 0) < nv_ref[b]
        packed_in = jnp.concatenate(
            [jnp.where(live, lax.bitcast_convert_type(xb_ref[j, 0:rows], jnp.uint32), jnp.uint32(0))
             for j in range(NPIECE)], axis=1)
        x = jnp.concatenate(_unpack_bf16_pairs(packed_in), axis=1).astype(jnp.bfloat16)
        dot32 = functools.partial(jnp.dot, preferred_element_type=jnp.float32)
        a = dot32(x, w1b_ref[...])
        g = dot32(x, w3b_ref[...])
        hmid = (a * _sigmoid(a) * g).astype(jnp.bfloat16)
        packed = _pack_bf16_pairs(dot32(hmid, w2b_ref[...]))
        for j in range(NPIECE):
            yb_ref[j, 0:rows] = lax.bitcast_convert_type(packed[:, j * 128:(j + 1) * 128], jnp.int32)

    half = BM // 2
    half_full = nv_ref[b] <= half

    @pl.when(jnp.logical_and(used, jnp.logical_not(half_full)))
    def _():
        expert_mlp(BM)

    @pl.when(jnp.logical_and(used, half_full))
    def _():
        expert_mlp(half)
        yb_ref[:, half:BM, :] = jnp.zeros((NPIECE, BM - half, 128), jnp.int32)

    @pl.when(jnp.logical_not(used))
    def _():
        yb_ref[...] = jnp.zeros_like(yb_ref)


def _experts(layer, block_expert, n_used, next_expert, group_slot, block_valid, xb, w1, w3, w2):
    n_slots = xb.shape[1]
    n_blocks = n_slots // BM

    def row_map(b, be, nu, nxt, gs, nv):
        return (0, jnp.minimum(b, nu[0] - 1), 0)

    def out_map(b, be, nu, nxt, gs, nv):
        return (0, b, 0)

    grid_spec = pltpu.PrefetchScalarGridSpec(
        num_scalar_prefetch=5,
        grid=(n_blocks,),
        in_specs=[pl.BlockSpec((NPIECE, BM, 128), row_map),
                  pl.BlockSpec(memory_space=pl.ANY),
                  pl.BlockSpec(memory_space=pl.ANY),
                  pl.BlockSpec(memory_space=pl.ANY)],
        out_specs=pl.BlockSpec((NPIECE, BM, 128), out_map),
        scratch_shapes=[pltpu.VMEM((2, D, D_FF), jnp.float32),
                        pltpu.VMEM((2, D, D_FF), jnp.float32),
                        pltpu.VMEM((2, D_FF, D), jnp.float32),
                        pltpu.VMEM((D, D_FF), jnp.bfloat16),
                        pltpu.VMEM((D, D_FF), jnp.bfloat16),
                        pltpu.VMEM((D_FF, D), jnp.bfloat16),
                        pltpu.SemaphoreType.DMA((2,))],
    )
    return pl.pallas_call(
        functools.partial(_expert_kernel, layer),
        grid_spec=grid_spec,
        out_shape=jax.ShapeDtypeStruct((NPIECE, n_slots, 128), jnp.int32),
        compiler_params=pltpu.CompilerParams(dimension_semantics=("arbitrary",),
                                             vmem_limit_bytes=VMEM_LIMIT),
        name="experts",
    )(block_expert, n_used, next_expert, group_slot, block_valid, xb, w1, w3, w2)


def _sc_gather_rows(table, idx):
    n_rows = idx.shape[0]

    @functools.partial(
        pl.kernel, mesh=_sc_mesh(),
        out_type=jax.ShapeDtypeStruct((n_rows, 128), table.dtype))
    def gather(table_hbm, idx_hbm, out_hbm):
        def body(idx_vmem, out_vmem):
            pltpu.sync_copy(table_hbm.at[idx_vmem.at[0]], out_vmem)

        pltpu.emit_pipeline(
            body,
            grid=(n_rows // SC_WINDOW,),
            in_specs=[pl.BlockSpec((1, SC_WINDOW), lambda i: (0, i))],
            out_specs=[pl.BlockSpec((SC_WINDOW, 128), lambda i: (i, 0))],
            core_axis_name=("c", "s"),
            dimension_semantics=(pltpu.PARALLEL,),
        )(idx_hbm, out_hbm)

    return gather(table, idx.reshape(1, n_rows))


def _combine_kernel(final, g_ref, x_ref, w_ref, nrm_ref, o_ref):
    w = w_ref[...].T
    ys = {}
    for j in range(NPIECE):
        hi0, lo0 = _unpack_bf16_pairs(lax.bitcast_convert_type(g_ref[0, j], jnp.uint32))
        hi1, lo1 = _unpack_bf16_pairs(lax.bitcast_convert_type(g_ref[1, j], jnp.uint32))
        c_hi, c_lo = j * 128, DP + j * 128
        ys[c_hi] = x_ref[:, c_hi:c_hi + 128] + w[:, 0:1] * hi0 + w[:, 1:2] * hi1
        ys[c_lo] = x_ref[:, c_lo:c_lo + 128] + w[:, 0:1] * lo0 + w[:, 1:2] * lo1
    if final:
        ms = sum(jnp.sum(y * y, axis=-1, keepdims=True) for y in ys.values()) * (1.0 / D)
        scale = lax.rsqrt(ms + RMS_EPS)
        ys = {c: y * scale * nrm_ref[:, c:c + 128] for c, y in ys.items()}
    for c, y in ys.items():
        o_ref[:, c:c + 128] = y


def _combine(gathered, x2, wgt, nrm, final):
    T = x2.shape[0]
    return pl.pallas_call(
        functools.partial(_combine_kernel, final),
        grid=(T // TD,),
        in_specs=[pl.BlockSpec((2, NPIECE, TD, 128), lambda i: (0, 0, i, 0)),
                  pl.BlockSpec((TD, D), lambda i: (i, 0)),
                  pl.BlockSpec((8, TD), lambda i: (0, i)),
                  pl.BlockSpec((1, D), lambda i: (0, 0))],
        out_specs=pl.BlockSpec((TD, D), lambda i: (i, 0)),
        out_shape=jax.ShapeDtypeStruct((T, D), jnp.float32),
        compiler_params=pltpu.CompilerParams(dimension_semantics=("arbitrary",)),
        name="combine",
    )(gathered, x2, wgt, nrm)


def _moe(layer, h2, idx, cnt, w1, w3, w2):
    T = idx.shape[1]
    counts = cnt[:, 0].astype(jnp.int32)
    padded = (counts + BM - 1) // BM * BM
    padded_end = jnp.cumsum(padded)
    padded_start = padded_end - padded
    eids = jnp.arange(N_EXPERTS, dtype=jnp.int32)[:, None]
    start_of = lambda e: jnp.sum(jnp.where(e[None, :] == eids, padded_start[:, None], 0), axis=0)
    dest = jnp.stack([start_of(idx[0]) + idx[2], start_of(idx[1]) + idx[3]])
    n_blocks = (2 * T) // BM + N_EXPERTS
    block_start = jnp.arange(n_blocks, dtype=jnp.int32) * BM
    block_expert = jnp.minimum(
        jnp.sum((padded_end[None, :] <= block_start[:, None]).astype(jnp.int32), axis=1),
        N_EXPERTS - 1)
    n_used = (padded_end[-1] // BM).reshape(1)
    later = jnp.logical_and(eids.T > eids, (counts > 0)[None, :])
    next_of = jnp.min(jnp.where(later, eids.T, N_EXPERTS), axis=1)
    next_of = jnp.where(next_of < N_EXPERTS, next_of, -1)
    pick = lambda table: jnp.sum(
        jnp.where(block_expert[:, None] == eids.T, table[None, :], 0), axis=1)
    next_expert = pick(next_of)
    group_slot = pick(jnp.cumsum((counts > 0).astype(jnp.int32)) - 1) % 2
    block_valid = jnp.clip(pick(padded_start + counts) - block_start, 0, BM)
    n_slots = n_blocks * BM
    piece_base = (jnp.arange(NPIECE, dtype=jnp.int32) * n_slots)[None, :, None]
    rows = (dest[:, None, :] + piece_base).reshape(-1)
    xb = _sc_scatter_rows(h2.reshape(NPIECE * T, 128), rows, 2, NPIECE * n_slots)
    yb = _experts(layer, block_expert, n_used, next_expert, group_slot, block_valid,
                  xb.reshape(NPIECE, n_slots, 128), w1, w3, w2)
    gathered = _sc_gather_rows(yb.reshape(NPIECE * n_slots, 128), rows)
    return gathered.reshape(2, NPIECE, T, 128)


def kernel(x, norm_mix, w_in, conv_a, w_a_out, conv_b, ln_b_g, ln_b_b, w_b_out, ln_c_g, ln_c_b,
           w_s, b_s, w_c_out, w_o, norm_ffn, w_group, b_group, w_router, b_router, w1, w3, w2,
           norm_final):
    bsz, seq, d = x.shape
    depth = norm_mix.shape[0]
    bf = jnp.bfloat16
    x2 = x.reshape(bsz * seq, d)
    wr = jnp.zeros((depth, d, NLOG), jnp.float32)
    wr = wr.at[:, :, 0:N_GROUPS].set(w_group).at[:, :, E_OFF:E_OFF + N_EXPERTS].set(w_router)
    wr_hi = wr.astype(bf)
    wr_lo = (wr - wr_hi.astype(jnp.float32)).astype(bf)
    br = jnp.zeros((depth, 1, NLOG), jnp.float32)
    br = br.at[:, 0, 0:N_GROUPS].set(b_group).at[:, 0, E_OFF:E_OFF + N_EXPERTS].set(b_router)
    bsb = jnp.broadcast_to(b_s[:, :, :, None], (depth, N_HEADS_C, CHUNK, CHUNK))
    row = lambda p: p[:, None, :]
    mixer_params = (row(norm_mix), w_in.astype(bf), conv_a, w_a_out.astype(bf), conv_b,
                    row(ln_b_g), row(ln_b_b), w_b_out.astype(bf), row(ln_c_g), row(ln_c_b),
                    w_s.astype(bf), bsb, w_c_out.astype(bf), w_o.astype(bf), row(norm_ffn), wr_hi, wr_lo, br)
    col = jnp.arange(SK, dtype=jnp.int32)[None, :]
    shift_sum = jnp.logical_and(
        col % PR == jnp.arange(RB, dtype=jnp.int32)[:, None] + col // PR, col < 8 * PR).astype(bf)
    pending = None
    for l in range(depth):
        x2, h2, idx, wgt, cnt = _mixer(l, x2, pending, seq, *mixer_params, shift_sum)
        gathered = _moe(l, h2, idx, cnt, w1, w3, w2)
        pending = (gathered, wgt.T)
    out = _combine(gathered, x2, wgt, norm_final[None], True)
    return out.reshape(bsz, seq, d)
```

```python
import functools

import jax
import jax.numpy as jnp
from jax import lax
from jax.experimental import pallas as pl
from jax.experimental.pallas import tpu as pltpu
from jax.experimental.pallas import tpu_sc as plsc

D = 1024
DP = D // 2
NPIECE = DP // 128
N_HEADS_C = 8
CHUNK = 128
CONV_A = 3
CONV_B = 31
N_GROUPS = 4
EPG = 8
N_EXPERTS = N_GROUPS * EPG
D_FF = 512
RMS_EPS = 1e-6
LN_EPS = 1e-5

C_XA, C_BA, C_CA, C_VB, C_GB, C_UV, C_G = 0, 1024, 2048, 3072, 4096, 5120, 7168
D_IN = 10240

TM = 512
HALO = 16
TE = TM + 2 * HALO
CW = 256
RB = 64
PR = RB + 8
SK = -(-8 * PR // 128) * 128
NLOG = 128
E_OFF = 8

TR = TM
BM = 512
TD = 512

VMEM_LIMIT = 60 * 1024 * 1024


def _sigmoid(x):
    return 0.5 * (jnp.tanh(0.5 * x) + 1.0)


def _gelu_tanh(x):
    return 0.5 * x * (1.0 + jnp.tanh(0.7978845608028654 * (x + 0.044715 * (x * x * x))))


def _pack_bf16_pairs(x):
    c = x.shape[1] // 2
    as_bits = lambda v: lax.bitcast_convert_type(v.astype(jnp.bfloat16).astype(jnp.float32), jnp.uint32)
    return as_bits(x[:, :c]) | (as_bits(x[:, c:]) >> 16)


def _unpack_bf16_pairs(p):
    hi = lax.bitcast_convert_type(p & jnp.uint32(0xFFFF0000), jnp.float32)
    lo = lax.bitcast_convert_type(p << 16, jnp.float32)
    return hi, lo


def _layer_norm(x, g, b):
    mu = jnp.mean(x, axis=-1, keepdims=True)
    xc = x - mu
    var = jnp.mean(xc * xc, axis=-1, keepdims=True)
    return xc * lax.rsqrt(var + LN_EPS) * g + b


def _moe_residual(x_ref, g_ref, wt_ref):
    w0, w1 = wt_ref[:, 0:1], wt_ref[:, 1:2]
    his, los = [], []
    for j in range(NPIECE):
        hi0, lo0 = _unpack_bf16_pairs(lax.bitcast_convert_type(g_ref[0, j], jnp.uint32))
        hi1, lo1 = _unpack_bf16_pairs(lax.bitcast_convert_type(g_ref[1, j], jnp.uint32))
        his.append(w0 * hi0 + w1 * hi1)
        los.append(w0 * lo0 + w1 * lo1)
    return x_ref[...] + jnp.concatenate(his + los, axis=1)


def _mixer_kernel(seq_tiles, fused_combine, xp_ref, xc_ref, xn_ref, *refs):
    if fused_combine:
        gp_ref, gc_ref, gn_ref, wtp_ref, wtc_ref, wtn_ref = refs[:6]
        refs = refs[6:]
        x_prev = lambda: _moe_residual(xp_ref, gp_ref, wtp_ref)
        x_cur = lambda: _moe_residual(xc_ref, gc_ref, wtc_ref)
        x_next = lambda: _moe_residual(xn_ref, gn_ref, wtn_ref)
    else:
        x_prev, x_cur, x_next = (lambda: xp_ref[...]), (lambda: xc_ref[...]), (lambda: xn_ref[...])
    (nm_ref, win_ref, ca_ref, wa_ref, cb_ref, lnbg_ref, lnbb_ref, wb_ref, lncg_ref, lncb_ref,
     ws_ref, bsb_ref, wc_ref, wo_ref, nf_ref, wrh_ref, wrl_ref, br_ref, sh_ref,
     xo_ref, h2_ref, idx_ref, wgt_ref, cnt_ref,
     hb_ref, zq_ref, cva_ref, cvb_ref, cvn_ref, vst_ref, mg_ref, carry_ref) = refs
    i = pl.program_id(0)
    at_start = (i % seq_tiles) == 0
    at_end = (i % seq_tiles) == seq_tiles - 1

    def _rms(xv):
        ms = jnp.mean(xv * xv, axis=-1, keepdims=True)
        return (xv * lax.rsqrt(ms + RMS_EPS) * nm_ref[...]).astype(jnp.bfloat16)

    hb_ref[0:HALO, :] = _rms(x_prev())
    hb_ref[HALO:HALO + TM, :] = _rms(x_cur())
    hb_ref[HALO + TM:TE, :] = _rms(x_next())

    rows = lax.broadcasted_iota(jnp.int32, (TE, 1), 0)
    lo = jnp.where(at_start, HALO, 0)
    hi = jnp.where(at_end, HALO + TM, TE)
    valid = jnp.logical_and(rows >= lo, rows < hi)

    def proj(r0, r1, c0, width):
        return jnp.dot(hb_ref[r0:r1, :], win_ref[:, c0:c0 + width],
                       preferred_element_type=jnp.float32)

    def b_proj(c):
        c0 = c * CW
        val = proj(0, TE, C_VB + c0, CW)
        gate = proj(0, TE, C_GB + c0, CW)
        zq_ref[c] = jnp.where(valid, val * _sigmoid(gate), 0.0)

    def a_proj(c):
        c0 = c * CW
        xa = proj(0, TE, C_XA + c0, CW)
        cc = proj(0, TE, C_CA + c0, CW)
        t = jnp.where(valid, xa * cc, 0.0)
        conv = (ca_ref[0:1, c0:c0 + CW] * t[HALO - 1:HALO - 1 + TM]
                + ca_ref[1:2, c0:c0 + CW] * t[HALO:HALO + TM]
                + ca_ref[2:3, c0:c0 + CW] * t[HALO + 1:HALO + 1 + TM])
        ba = proj(HALO, HALO + TM, C_BA + c0, CW)
        cva_ref[:, c0:c0 + CW] = (ba * conv).astype(jnp.bfloat16)

    def c_v_proj(c):
        c0 = c * CW
        vst_ref[:, c0:c0 + CW] = _gelu_tanh(proj(HALO, HALO + TM, C_UV + D + c0, CW))

    def b_conv(c, row_blocks):
        c0 = c * CW
        zq = zq_ref.at[c]
        for rb in row_blocks:
            r0 = rb * RB
            for lt in range(CW // 128):
                l0 = lt * 128
                parts = []
                for s in range(8):
                    part = None
                    for q in range(4):
                        k = 8 * q + s - (HALO - CONV_B // 2)
                        if 0 <= k < CONV_B:
                            term = (cb_ref[k:k + 1, c0 + l0:c0 + l0 + 128]
                                    * zq[r0 + 8 * q:r0 + 8 * q + PR, l0:l0 + 128])
                            part = term if part is None else part + term
                    parts.append(part)
                if SK > 8 * PR:
                    parts.append(jnp.zeros((SK - 8 * PR, 128), jnp.float32))
                stacked = jnp.concatenate(parts, axis=0).astype(jnp.bfloat16)
                xo_ref[r0:r0 + RB, c0 + l0:c0 + l0 + 128] = jnp.dot(
                    sh_ref[...], stacked, preferred_element_type=jnp.float32)

    n_chunks = D // CW
    n_rb = TM // RB
    rb_step = n_rb // n_chunks
    late_rows = lambda c: range(c * rb_step, (c + 1) * rb_step)
    b_proj(0)
    for c in range(n_chunks):
        if c + 1 < n_chunks:
            b_proj(c + 1)
        a_proj(c)
        c_v_proj(c)
    b_conv(0, range(n_rb))

    vn = _layer_norm(vst_ref[...], lncg_ref[...], lncb_ref[...])
    cvn_ref[...] = vn.astype(jnp.bfloat16)
    for c in range(n_chunks):
        c0 = c * CW
        ya = jnp.dot(cva_ref[...], wa_ref[:, c0:c0 + CW], preferred_element_type=jnp.float32)
        ga = _sigmoid(proj(HALO, HALO + TM, C_G + c0, CW))
        mg_ref[:, c0:c0 + CW] = ga * ya
        b_conv(1, late_rows(c))

    for c in range(n_chunks):
        c0 = c * CW
        u = _gelu_tanh(proj(HALO, HALO + TM, C_UV + c0, CW))
        for n in range(TM // CHUNK):
            for hh in range(CW // CHUNK):
                h = c * (CW // CHUNK) + hh
                sv = jnp.dot(ws_ref[h], cvn_ref[n * CHUNK:(n + 1) * CHUNK, h * CHUNK:(h + 1) * CHUNK],
                             preferred_element_type=jnp.float32) + bsb_ref[h]
                cva_ref[n * CHUNK:(n + 1) * CHUNK, h * CHUNK:(h + 1) * CHUNK] = (
                    u[n * CHUNK:(n + 1) * CHUNK, hh * CHUNK:(hh + 1) * CHUNK] * sv
                ).astype(jnp.bfloat16)
        b_conv(2, late_rows(c))
    for c in range(n_chunks):
        c0 = c * CW
        yc = jnp.dot(cva_ref[...], wc_ref[:, c0:c0 + CW], preferred_element_type=jnp.float32)
        gc = _sigmoid(proj(HALO, HALO + TM, C_G + 2 * D + c0, CW))
        mg_ref[:, c0:c0 + CW] += gc * yc
        b_conv(3, late_rows(c))

    zc = _layer_norm(xo_ref[...], lnbg_ref[...], lnbb_ref[...])
    cvb_ref[...] = (zc * _sigmoid(zc)).astype(jnp.bfloat16)
    for c in range(n_chunks):
        c0 = c * CW
        yb = jnp.dot(cvb_ref[...], wb_ref[:, c0:c0 + CW], preferred_element_type=jnp.float32)
        gb = _sigmoid(proj(HALO, HALO + TM, C_G + D + c0, CW))
        mg_ref[:, c0:c0 + CW] += gb * yb

    cvb_ref[...] = mg_ref[...].astype(jnp.bfloat16)
    x_res = x_cur()
    for c in range(n_chunks):
        c0 = c * CW
        xo_ref[:, c0:c0 + CW] = x_res[:, c0:c0 + CW] + jnp.dot(
            cvb_ref[...], wo_ref[:, c0:c0 + CW], preferred_element_type=jnp.float32)
    xnew = xo_ref[...]
    ms = jnp.mean(xnew * xnew, axis=-1, keepdims=True)
    h2 = xnew * lax.rsqrt(ms + RMS_EPS) * nf_ref[...]
    h2_packed = _pack_bf16_pairs(h2)
    for j in range(NPIECE):
        h2_ref[j] = lax.bitcast_convert_type(h2_packed[:, j * 128:(j + 1) * 128], jnp.int32)
    h_hi = h2.astype(jnp.bfloat16)
    h_lo = (h2 - h_hi.astype(jnp.float32)).astype(jnp.bfloat16)
    dot32 = functools.partial(jnp.dot, preferred_element_type=jnp.float32)
    logits = (dot32(h_hi, wrh_ref[...]) + dot32(h_lo, wrh_ref[...])
              + dot32(h_hi, wrl_ref[...]) + dot32(h_lo, wrl_ref[...]) + br_ref[...])
    _route_tile(logits, idx_ref, wgt_ref, cnt_ref, carry_ref)


def _layer_spec(layer, shape):
    nd = len(shape)
    return pl.BlockSpec((None,) + tuple(shape), lambda i, _n=nd: (layer,) + (0,) * _n,
                        pipeline_mode=pl.Buffered(1))


def _mixer(layer, x2, pending, seq_len, nm, win, ca, wa, cb, lnbg, lnbb, wb, lncg, lncb, ws, bsb, wc, wo,
           nf, wrh, wrl, br, sh):
    T = x2.shape[0]
    n_tiles = T // TM
    hb = TM // HALO
    last_halo = T // HALO - 1
    ls = functools.partial(_layer_spec, layer)
    prev_rows = lambda i: jnp.maximum(i * hb - 1, 0)
    next_rows = lambda i: jnp.minimum((i + 1) * hb, last_halo)
    in_specs = [
        pl.BlockSpec((HALO, D), lambda i: (prev_rows(i), 0)),
        pl.BlockSpec((TM, D), lambda i: (i, 0)),
        pl.BlockSpec((HALO, D), lambda i: (next_rows(i), 0)),
    ]
    operands = [x2, x2, x2]
    if pending is not None:
        gathered, wgt_rows = pending
        in_specs += [
            pl.BlockSpec((2, NPIECE, HALO, 128), lambda i: (0, 0, prev_rows(i), 0)),
            pl.BlockSpec((2, NPIECE, TM, 128), lambda i: (0, 0, i, 0)),
            pl.BlockSpec((2, NPIECE, HALO, 128), lambda i: (0, 0, next_rows(i), 0)),
            pl.BlockSpec((HALO, 8), lambda i: (prev_rows(i), 0)),
            pl.BlockSpec((TM, 8), lambda i: (i, 0)),
            pl.BlockSpec((HALO, 8), lambda i: (next_rows(i), 0)),
        ]
        operands += [gathered, gathered, gathered, wgt_rows, wgt_rows, wgt_rows]
    in_specs += [
        ls((1, D)), ls((D, D_IN)), ls((CONV_A, D)), ls((D, D)),
        ls((CONV_B, D)), ls((1, D)), ls((1, D)), ls((D, D)),
        ls((1, D)), ls((1, D)), ls((N_HEADS_C, CHUNK, CHUNK)),
        ls((N_HEADS_C, CHUNK, CHUNK)), ls((D, D)), ls((D, D)),
        ls((1, D)), ls((D, NLOG)), ls((D, NLOG)), ls((1, NLOG)),
        pl.BlockSpec((RB, SK), lambda i: (0, 0), pipeline_mode=pl.Buffered(1)),
    ]
    out_specs = [
        pl.BlockSpec((TM, D), lambda i: (i, 0)),
        pl.BlockSpec((NPIECE, TM, 128), lambda i: (0, i, 0)),
        pl.BlockSpec((8, TM), lambda i: (0, i)),
        pl.BlockSpec((8, TM), lambda i: (0, i)),
        pl.BlockSpec((N_EXPERTS, 128), lambda i: (0, 0)),
    ]
    return pl.pallas_call(
        functools.partial(_mixer_kernel, seq_len // TM, pending is not None),
        grid=(n_tiles,),
        in_specs=in_specs,
        out_specs=out_specs,
        out_shape=[jax.ShapeDtypeStruct((T, D), jnp.float32),
                   jax.ShapeDtypeStruct((NPIECE, T, 128), jnp.int32),
                   jax.ShapeDtypeStruct((8, T), jnp.int32),
                   jax.ShapeDtypeStruct((8, T), jnp.float32),
                   jax.ShapeDtypeStruct((N_EXPERTS, 128), jnp.float32)],
        scratch_shapes=[pltpu.VMEM((TE, D), jnp.bfloat16),
                        pltpu.VMEM((D // CW, TE, CW), jnp.float32),
                        pltpu.VMEM((TM, D), jnp.bfloat16),
                        pltpu.VMEM((TM, D), jnp.bfloat16),
                        pltpu.VMEM((TM, D), jnp.bfloat16),
                        pltpu.VMEM((TM, D), jnp.float32),
                        pltpu.VMEM((TM, D), jnp.float32),
                        pltpu.VMEM((N_EXPERTS, 128), jnp.float32)],
        compiler_params=pltpu.CompilerParams(dimension_semantics=("arbitrary",),
                                             vmem_limit_bytes=VMEM_LIMIT),
        name="mixer",
    )(*operands, nm, win, ca, wa, cb, lnbg, lnbb, wb, lncg, lncb, ws, bsb, wc, wo, nf, wrh, wrl, br, sh)


def _route_tile(lg, idx_ref, wgt_ref, cnt_ref, carry_ref):
    i = pl.program_id(0)

    @pl.when(i == 0)
    def _():
        carry_ref[...] = jnp.zeros_like(carry_ref)

    lt = lg.T
    g = [lt[j:j + 1, :] for j in range(N_GROUPS)]
    gmax = jnp.maximum(jnp.maximum(g[0], g[1]), jnp.maximum(g[2], g[3]))
    gidx = jnp.where(g[0] == gmax, 0.0, jnp.where(g[1] == gmax, 1.0, jnp.where(g[2] == gmax, 2.0, 3.0)))
    gsum = sum(jnp.exp(gj - gmax) for gj in g)
    g_p = 1.0 / gsum

    sel = lt[E_OFF + 3 * EPG:E_OFF + 4 * EPG, :]
    for j in (2, 1, 0):
        sel = jnp.where(gidx == float(j), lt[E_OFF + j * EPG:E_OFF + (j + 1) * EPG, :], sel)
    rid = lax.broadcasted_iota(jnp.int32, (EPG, TR), 0).astype(jnp.float32)
    m1 = jnp.max(sel, axis=0, keepdims=True)
    i1 = jnp.min(jnp.where(sel == m1, rid, float(EPG)), axis=0, keepdims=True)
    rest = jnp.where(rid == i1, -jnp.inf, sel)
    m2 = jnp.max(rest, axis=0, keepdims=True)
    i2 = jnp.min(jnp.where(rest == m2, rid, float(EPG)), axis=0, keepdims=True)
    e2x = jnp.exp(m2 - m1)
    den = 1.0 + e2x
    w1 = (1.0 / den) * g_p
    w2 = (e2x / den) * g_p
    e1 = gidx * float(EPG) + i1
    e2 = gidx * float(EPG) + i2

    eid = lax.broadcasted_iota(jnp.int32, (N_EXPERTS, TR), 0).astype(jnp.float32)
    oh1 = (eid == e1).astype(jnp.float32)
    oh2 = (eid == e2).astype(jnp.float32)
    oh = oh1 + oh2
    tr = lax.broadcasted_iota(jnp.int32, (TR, TR), 0)
    tc = lax.broadcasted_iota(jnp.int32, (TR, TR), 1)
    upper = (tr < tc).astype(jnp.bfloat16)
    before = jnp.dot(oh.astype(jnp.bfloat16), upper, preferred_element_type=jnp.float32)
    base = before + carry_ref[:, 0:1]
    r1 = jnp.sum(oh1 * base, axis=0, keepdims=True)
    r2 = jnp.sum(oh2 * base, axis=0, keepdims=True)
    carry_ref[...] = carry_ref[...] + jnp.sum(oh, axis=1, keepdims=True)

    idx_ref[...] = jnp.zeros_like(idx_ref)
    idx_ref[0:1, :] = e1.astype(jnp.int32)
    idx_ref[1:2, :] = e2.astype(jnp.int32)
    idx_ref[2:3, :] = r1.astype(jnp.int32)
    idx_ref[3:4, :] = r2.astype(jnp.int32)
    wgt_ref[...] = jnp.zeros_like(wgt_ref)
    wgt_ref[0:1, :] = w1
    wgt_ref[1:2, :] = w2
    cnt_ref[...] = carry_ref[...]


SC_WINDOW = 128


def _sc_mesh():
    return plsc.VectorSubcoreMesh(core_axis_name="c", subcore_axis_name="s")


def _sc_scatter_rows(rows, idx, passes, n_out):
    n_src = rows.shape[0]
    idx_lists = [idx[p * n_src:(p + 1) * n_src].reshape(1, n_src) for p in range(passes)]

    @functools.partial(
        pl.kernel, mesh=_sc_mesh(),
        out_type=jax.ShapeDtypeStruct((n_out, 128), rows.dtype))
    def scatter(rows_hbm, *refs):
        idx_hbms, out_hbm = refs[:passes], refs[passes]

        def body(rows_vmem, *idx_vmems):
            for idx_vmem in idx_vmems:
                pltpu.sync_copy(rows_vmem, out_hbm.at[idx_vmem.at[0]])

        pltpu.emit_pipeline(
            body,
            grid=(n_src // SC_WINDOW,),
            in_specs=[pl.BlockSpec((SC_WINDOW, 128), lambda i: (i, 0))]
            + [pl.BlockSpec((1, SC_WINDOW), lambda i: (0, i))] * passes,
            out_specs=[],
            core_axis_name=("c", "s"),
            dimension_semantics=(pltpu.PARALLEL,),
        )(rows_hbm, *idx_hbms)

    return scatter(rows, *idx_lists)


def _expert_kernel(layer, be_ref, nu_ref, nxt_ref, gs_ref, nv_ref, xb_ref, w1_ref, w3_ref, w2_ref, yb_ref,
                   wf1_ref, wf3_ref, wf2_ref, w1b_ref, w3b_ref, w2b_ref, sem):
    b = pl.program_id(0)
    used = b < nu_ref[0]
    new_expert = jnp.logical_or(b == 0, be_ref[b] != be_ref[jnp.maximum(b - 1, 0)])

    def weight_copies(e, s):
        return [pltpu.make_async_copy(src.at[layer, e], dst.at[s], sem.at[s])
                for src, dst in ((w1_ref, wf1_ref), (w3_ref, wf3_ref), (w2_ref, wf2_ref))]

    @pl.when(b == 0)
    def _():
        for cp in weight_copies(be_ref[0], 0):
            cp.start(priority=1)

    @pl.when(jnp.logical_and(used, new_expert))
    def _():
        s = gs_ref[b]
        for cp in weight_copies(be_ref[b], s):
            cp.wait()

        @pl.when(nxt_ref[b] >= 0)
        def _():
            for cp in weight_copies(nxt_ref[b], 1 - s):
                cp.start(priority=1)

        w1b_ref[...] = wf1_ref[s].astype(jnp.bfloat16)
        w3b_ref[...] = wf3_ref[s].astype(jnp.bfloat16)
        w2b_ref[...] = wf2_ref[s].astype(jnp.bfloat16)

    def expert_mlp(rows):
        live = lax.broadcasted_iota(jnp.int32, (rows, 1), 0) < nv_ref[b]
        packed_in = jnp.concatenate(
            [jnp.where(live, lax.bitcast_convert_type(xb_ref[j, 0:rows], jnp.uint32), jnp.uint32(0))
             for j in range(NPIECE)], axis=1)
        x = jnp.concatenate(_unpack_bf16_pairs(packed_in), axis=1).astype(jnp.bfloat16)
        dot32 = functools.partial(jnp.dot, preferred_element_type=jnp.float32)
        a = dot32(x, w1b_ref[...])
        g = dot32(x, w3b_ref[...])
        hmid = (a * _sigmoid(a) * g).astype(jnp.bfloat16)
        packed = _pack_bf16_pairs(dot32(hmid, w2b_ref[...]))
        for j in range(NPIECE):
            yb_ref[j, 0:rows] = lax.bitcast_convert_type(packed[:, j * 128:(j + 1) * 128], jnp.int32)

    half = BM // 2
    half_full = nv_ref[b] <= half

    @pl.when(jnp.logical_and(used, jnp.logical_not(half_full)))
    def _():
        expert_mlp(BM)

    @pl.when(jnp.logical_and(used, half_full))
    def _():
        expert_mlp(half)
        yb_ref[:, half:BM, :] = jnp.zeros((NPIECE, BM - half, 128), jnp.int32)

    @pl.when(jnp.logical_not(used))
    def _():
        yb_ref[...] = jnp.zeros_like(yb_ref)


def _experts(layer, block_expert, n_used, next_expert, group_slot, block_valid, xb, w1, w3, w2):
    n_slots = xb.shape[1]
    n_blocks = n_slots // BM

    def row_map(b, be, nu, nxt, gs, nv):
        return (0, jnp.minimum(b, nu[0] - 1), 0)

    def out_map(b, be, nu, nxt, gs, nv):
        return (0, b, 0)

    grid_spec = pltpu.PrefetchScalarGridSpec(
        num_scalar_prefetch=5,
        grid=(n_blocks,),
        in_specs=[pl.BlockSpec((NPIECE, BM, 128), row_map),
                  pl.BlockSpec(memory_space=pl.ANY),
                  pl.BlockSpec(memory_space=pl.ANY),
                  pl.BlockSpec(memory_space=pl.ANY)],
        out_specs=pl.BlockSpec((NPIECE, BM, 128), out_map),
        scratch_shapes=[pltpu.VMEM((2, D, D_FF), jnp.float32),
                        pltpu.VMEM((2, D, D_FF), jnp.float32),
                        pltpu.VMEM((2, D_FF, D), jnp.float32),
                        pltpu.VMEM((D, D_FF), jnp.bfloat16),
                        pltpu.VMEM((D, D_FF), jnp.bfloat16),
                        pltpu.VMEM((D_FF, D), jnp.bfloat16),
                        pltpu.SemaphoreType.DMA((2,))],
    )
    return pl.pallas_call(
        functools.partial(_expert_kernel, layer),
        grid_spec=grid_spec,
        out_shape=jax.ShapeDtypeStruct((NPIECE, n_slots, 128), jnp.int32),
        compiler_params=pltpu.CompilerParams(dimension_semantics=("arbitrary",),
                                             vmem_limit_bytes=VMEM_LIMIT),
        name="experts",
    )(block_expert, n_used, next_expert, group_slot, block_valid, xb, w1, w3, w2)


def _sc_gather_rows(table, idx):
    n_rows = idx.shape[0]

    @functools.partial(
        pl.kernel, mesh=_sc_mesh(),
        out_type=jax.ShapeDtypeStruct((n_rows, 128), table.dtype),
        scratch_types=[pltpu.SemaphoreType.DMA((2,))])
    def gather(table_hbm, idx_hbm, out_hbm, sem):
        def body(idx0_vmem, idx1_vmem, out0_vmem, out1_vmem):
            first = pltpu.async_copy(table_hbm.at[idx0_vmem.at[0]], out0_vmem, sem.at[0])
            second = pltpu.async_copy(table_hbm.at[idx1_vmem.at[0]], out1_vmem, sem.at[1])
            first.wait()
            second.wait()

        pltpu.emit_pipeline(
            body,
            grid=(n_rows // (2 * SC_WINDOW),),
            in_specs=[pl.BlockSpec((1, SC_WINDOW), lambda i: (0, 2 * i)),
                      pl.BlockSpec((1, SC_WINDOW), lambda i: (0, 2 * i + 1))],
            out_specs=[pl.BlockSpec((SC_WINDOW, 128), lambda i: (2 * i, 0)),
                       pl.BlockSpec((SC_WINDOW, 128), lambda i: (2 * i + 1, 0))],
            core_axis_name=("c", "s"),
            dimension_semantics=(pltpu.PARALLEL,),
        )(idx_hbm, idx_hbm, out_hbm, out_hbm)

    return gather(table, idx.reshape(1, n_rows))


def _combine_kernel(final, g_ref, x_ref, w_ref, nrm_ref, o_ref):
    w = w_ref[...].T
    ys = {}
    for j in range(NPIECE):
        hi0, lo0 = _unpack_bf16_pairs(lax.bitcast_convert_type(g_ref[0, j], jnp.uint32))
        hi1, lo1 = _unpack_bf16_pairs(lax.bitcast_convert_type(g_ref[1, j], jnp.uint32))
        c_hi, c_lo = j * 128, DP + j * 128
        ys[c_hi] = x_ref[:, c_hi:c_hi + 128] + w[:, 0:1] * hi0 + w[:, 1:2] * hi1
        ys[c_lo] = x_ref[:, c_lo:c_lo + 128] + w[:, 0:1] * lo0 + w[:, 1:2] * lo1
    if final:
        ms = sum(jnp.sum(y * y, axis=-1, keepdims=True) for y in ys.values()) * (1.0 / D)
        scale = lax.rsqrt(ms + RMS_EPS)
        ys = {c: y * scale * nrm_ref[:, c:c + 128] for c, y in ys.items()}
    for c, y in ys.items():
        o_ref[:, c:c + 128] = y


def _combine(gathered, x2, wgt, nrm, final):
    T = x2.shape[0]
    return pl.pallas_call(
        functools.partial(_combine_kernel, final),
        grid=(T // TD,),
        in_specs=[pl.BlockSpec((2, NPIECE, TD, 128), lambda i: (0, 0, i, 0)),
                  pl.BlockSpec((TD, D), lambda i: (i, 0)),
                  pl.BlockSpec((8, TD), lambda i: (0, i)),
                  pl.BlockSpec((1, D), lambda i: (0, 0))],
        out_specs=pl.BlockSpec((TD, D), lambda i: (i, 0)),
        out_shape=jax.ShapeDtypeStruct((T, D), jnp.float32),
        compiler_params=pltpu.CompilerParams(dimension_semantics=("arbitrary",)),
        name="combine",
    )(gathered, x2, wgt, nrm)


def _moe(layer, h2, idx, cnt, w1, w3, w2):
    T = idx.shape[1]
    counts = cnt[:, 0].astype(jnp.int32)
    padded = (counts + BM - 1) // BM * BM
    padded_end = jnp.cumsum(padded)
    padded_start = padded_end - padded
    eids = jnp.arange(N_EXPERTS, dtype=jnp.int32)[:, None]
    start_of = lambda e: jnp.sum(jnp.where(e[None, :] == eids, padded_start[:, None], 0), axis=0)
    dest = jnp.stack([start_of(idx[0]) + idx[2], start_of(idx[1]) + idx[3]])
    n_blocks = (2 * T) // BM + N_EXPERTS
    block_start = jnp.arange(n_blocks, dtype=jnp.int32) * BM
    block_expert = jnp.minimum(
        jnp.sum((padded_end[None, :] <= block_start[:, None]).astype(jnp.int32), axis=1),
        N_EXPERTS - 1)
    n_used = (padded_end[-1] // BM).reshape(1)
    later = jnp.logical_and(eids.T > eids, (counts > 0)[None, :])
    next_of = jnp.min(jnp.where(later, eids.T, N_EXPERTS), axis=1)
    next_of = jnp.where(next_of < N_EXPERTS, next_of, -1)
    pick = lambda table: jnp.sum(
        jnp.where(block_expert[:, None] == eids.T, table[None, :], 0), axis=1)
    next_expert = pick(next_of)
    group_slot = pick(jnp.cumsum((counts > 0).astype(jnp.int32)) - 1) % 2
    block_valid = jnp.clip(pick(padded_start + counts) - block_start, 0, BM)
    n_slots = n_blocks * BM
    piece_base = (jnp.arange(NPIECE, dtype=jnp.int32) * n_slots)[None, :, None]
    rows = (dest[:, None, :] + piece_base).reshape(-1)
    xb = _sc_scatter_rows(h2.reshape(NPIECE * T, 128), rows, 2, NPIECE * n_slots)
    yb = _experts(layer, block_expert, n_used, next_expert, group_slot, block_valid,
                  xb.reshape(NPIECE, n_slots, 128), w1, w3, w2)
    gathered = _sc_gather_rows(yb.reshape(NPIECE * n_slots, 128), rows)
    return gathered.reshape(2, NPIECE, T, 128)


def kernel(x, norm_mix, w_in, conv_a, w_a_out, conv_b, ln_b_g, ln_b_b, w_b_out, ln_c_g, ln_c_b,
           w_s, b_s, w_c_out, w_o, norm_ffn, w_group, b_group, w_router, b_router, w1, w3, w2,
           norm_final):
    bsz, seq, d = x.shape
    depth = norm_mix.shape[0]
    bf = jnp.bfloat16
    x2 = x.reshape(bsz * seq, d)
    wr = jnp.zeros((depth, d, NLOG), jnp.float32)
    wr = wr.at[:, :, 0:N_GROUPS].set(w_group).at[:, :, E_OFF:E_OFF + N_EXPERTS].set(w_router)
    wr_hi = wr.astype(bf)
    wr_lo = (wr - wr_hi.astype(jnp.float32)).astype(bf)
    br = jnp.zeros((depth, 1, NLOG), jnp.float32)
    br = br.at[:, 0, 0:N_GROUPS].set(b_group).at[:, 0, E_OFF:E_OFF + N_EXPERTS].set(b_router)
    bsb = jnp.broadcast_to(b_s[:, :, :, None], (depth, N_HEADS_C, CHUNK, CHUNK))
    row = lambda p: p[:, None, :]
    mixer_params = (row(norm_mix), w_in.astype(bf), conv_a, w_a_out.astype(bf), conv_b,
                    row(ln_b_g), row(ln_b_b), w_b_out.astype(bf), row(ln_c_g), row(ln_c_b),
                    w_s.astype(bf), bsb, w_c_out.astype(bf), w_o.astype(bf), row(norm_ffn), wr_hi, wr_lo, br)
    col = jnp.arange(SK, dtype=jnp.int32)[None, :]
    shift_sum = jnp.logical_and(
        col % PR == jnp.arange(RB, dtype=jnp.int32)[:, None] + col // PR, col < 8 * PR).astype(bf)
    pending = None
    for l in range(depth):
        x2, h2, idx, wgt, cnt = _mixer(l, x2, pending, seq, *mixer_params, shift_sum)
        gathered = _moe(l, h2, idx, cnt, w1, w3, w2)
        pending = (gathered, wgt.T)
    out = _combine(gathered, x2, wgt, norm_final[None], True)
    return out.reshape(bsz, seq, d)
```

```python
import functools

import jax
import jax.numpy as jnp
from jax import lax
from jax.experimental import pallas as pl
from jax.experimental.pallas import tpu as pltpu
from jax.experimental.pallas import tpu_sc as plsc

D = 1024
DP = D // 2
NPIECE = DP // 128
N_HEADS_C = 8
CHUNK = 128
CONV_A = 3
CONV_B = 31
N_GROUPS = 4
EPG = 8
N_EXPERTS = N_GROUPS * EPG
D_FF = 512
RMS_EPS = 1e-6
LN_EPS = 1e-5

C_XA, C_BA, C_CA, C_VB, C_GB, C_UV, C_G = 0, 1024, 2048, 3072, 4096, 5120, 7168
D_IN = 10240

TM = 512
HALO = 16
TE = TM + 2 * HALO
CW = 256
RB = 64
PR = RB + 8
SK = -(-8 * PR // 128) * 128
NLOG = 128
E_OFF = 8

TR = TM
BM = 512
TD = 512

VMEM_LIMIT = 60 * 1024 * 1024


def _sigmoid(x):
    return 0.5 * (jnp.tanh(0.5 * x) + 1.0)


def _gelu_tanh(x):
    return 0.5 * x * (1.0 + jnp.tanh(0.7978845608028654 * (x + 0.044715 * (x * x * x))))


def _pack_bf16_pairs(x):
    c = x.shape[1] // 2
    as_bits = lambda v: lax.bitcast_convert_type(v.astype(jnp.bfloat16).astype(jnp.float32), jnp.uint32)
    return as_bits(x[:, :c]) | (as_bits(x[:, c:]) >> 16)


def _unpack_bf16_pairs(p):
    hi = lax.bitcast_convert_type(p & jnp.uint32(0xFFFF0000), jnp.float32)
    lo = lax.bitcast_convert_type(p << 16, jnp.float32)
    return hi, lo


def _layer_norm(x, g, b):
    mu = jnp.mean(x, axis=-1, keepdims=True)
    xc = x - mu
    var = jnp.mean(xc * xc, axis=-1, keepdims=True)
    return xc * lax.rsqrt(var + LN_EPS) * g + b


def _moe_residual(x_ref, g_ref, wt_ref):
    w0, w1 = wt_ref[:, 0:1], wt_ref[:, 1:2]
    his, los = [], []
    for j in range(NPIECE):
        hi0, lo0 = _unpack_bf16_pairs(lax.bitcast_convert_type(g_ref[0, j], jnp.uint32))
        hi1, lo1 = _unpack_bf16_pairs(lax.bitcast_convert_type(g_ref[1, j], jnp.uint32))
        his.append(w0 * hi0 + w1 * hi1)
        los.append(w0 * lo0 + w1 * lo1)
    return x_ref[...] + jnp.concatenate(his + los, axis=1)


def _mixer_kernel(seq_tiles, fused_combine, xp_ref, xc_ref, xn_ref, *refs):
    if fused_combine:
        gp_ref, gc_ref, gn_ref, wtp_ref, wtc_ref, wtn_ref = refs[:6]
        refs = refs[6:]
        x_prev = lambda: _moe_residual(xp_ref, gp_ref, wtp_ref)
        x_cur = lambda: _moe_residual(xc_ref, gc_ref, wtc_ref)
        x_next = lambda: _moe_residual(xn_ref, gn_ref, wtn_ref)
    else:
        x_prev, x_cur, x_next = (lambda: xp_ref[...]), (lambda: xc_ref[...]), (lambda: xn_ref[...])
    (nm_ref, win_ref, ca_ref, wa_ref, cb_ref, lnbg_ref, lnbb_ref, wb_ref, lncg_ref, lncb_ref,
     ws_ref, bsb_ref, wc_ref, wo_ref, nf_ref, wrh_ref, wrl_ref, br_ref, sh_ref,
     xo_ref, h2_ref, idx_ref, wgt_ref, cnt_ref,
     hb_ref, zq_ref, cva_ref, cvb_ref, cvn_ref, vst_ref, mg_ref, carry_ref) = refs
    i = pl.program_id(0)
    at_start = (i % seq_tiles) == 0
    at_end = (i % seq_tiles) == seq_tiles - 1

    def _rms(xv):
        ms = jnp.mean(xv * xv, axis=-1, keepdims=True)
        return (xv * lax.rsqrt(ms + RMS_EPS) * nm_ref[...]).astype(jnp.bfloat16)

    hb_ref[0:HALO, :] = _rms(x_prev())
    hb_ref[HALO:HALO + TM, :] = _rms(x_cur())
    hb_ref[HALO + TM:TE, :] = _rms(x_next())

    rows = lax.broadcasted_iota(jnp.int32, (TE, 1), 0)
    lo = jnp.where(at_start, HALO, 0)
    hi = jnp.where(at_end, HALO + TM, TE)
    valid = jnp.logical_and(rows >= lo, rows < hi)

    def proj(r0, r1, c0, width):
        return jnp.dot(hb_ref[r0:r1, :], win_ref[:, c0:c0 + width],
                       preferred_element_type=jnp.float32)

    def b_proj(c):
        c0 = c * CW
        val = proj(0, TE, C_VB + c0, CW)
        gate = proj(0, TE, C_GB + c0, CW)
        zq_ref[c] = jnp.where(valid, val * _sigmoid(gate), 0.0)

    def a_proj(c):
        c0 = c * CW
        xa = proj(0, TE, C_XA + c0, CW)
        cc = proj(0, TE, C_CA + c0, CW)
        t = jnp.where(valid, xa * cc, 0.0)
        conv = (ca_ref[0:1, c0:c0 + CW] * t[HALO - 1:HALO - 1 + TM]
                + ca_ref[1:2, c0:c0 + CW] * t[HALO:HALO + TM]
                + ca_ref[2:3, c0:c0 + CW] * t[HALO + 1:HALO + 1 + TM])
        ba = proj(HALO, HALO + TM, C_BA + c0, CW)
        cva_ref[:, c0:c0 + CW] = (ba * conv).astype(jnp.bfloat16)

    def c_v_proj(c):
        c0 = c * CW
        vst_ref[:, c0:c0 + CW] = _gelu_tanh(proj(HALO, HALO + TM, C_UV + D + c0, CW))

    def b_conv(c, row_blocks):
        c0 = c * CW
        zq = zq_ref.at[c]
        for rb in row_blocks:
            r0 = rb * RB
            for lt in range(CW // 128):
                l0 = lt * 128
                parts = []
                for s in range(8):
                    part = None
                    for q in range(4):
                        k = 8 * q + s - (HALO - CONV_B // 2)
                        if 0 <= k < CONV_B:
                            term = (cb_ref[k:k + 1, c0 + l0:c0 + l0 + 128]
                                    * zq[r0 + 8 * q:r0 + 8 * q + PR, l0:l0 + 128])
                            part = term if part is None else part + term
                    parts.append(part)
                if SK > 8 * PR:
                    parts.append(jnp.zeros((SK - 8 * PR, 128), jnp.float32))
                stacked = jnp.concatenate(parts, axis=0).astype(jnp.bfloat16)
                xo_ref[r0:r0 + RB, c0 + l0:c0 + l0 + 128] = jnp.dot(
                    sh_ref[...], stacked, preferred_element_type=jnp.float32)

    n_chunks = D // CW
    n_rb = TM // RB
    rb_step = n_rb // n_chunks
    late_rows = lambda c: range(c * rb_step, (c + 1) * rb_step)
    b_proj(0)
    for c in range(n_chunks):
        if c + 1 < n_chunks:
            b_proj(c + 1)
        a_proj(c)
        c_v_proj(c)
    b_conv(0, range(n_rb))

    vn = _layer_norm(vst_ref[...], lncg_ref[...], lncb_ref[...])
    cvn_ref[...] = vn.astype(jnp.bfloat16)
    for c in range(n_chunks):
        c0 = c * CW
        ya = jnp.dot(cva_ref[...], wa_ref[:, c0:c0 + CW], preferred_element_type=jnp.float32)
        ga = _sigmoid(proj(HALO, HALO + TM, C_G + c0, CW))
        mg_ref[:, c0:c0 + CW] = ga * ya
        b_conv(1, late_rows(c))

    for c in range(n_chunks):
        c0 = c * CW
        u = _gelu_tanh(proj(HALO, HALO + TM, C_UV + c0, CW))
        for n in range(TM // CHUNK):
            for hh in range(CW // CHUNK):
                h = c * (CW // CHUNK) + hh
                sv = jnp.dot(ws_ref[h], cvn_ref[n * CHUNK:(n + 1) * CHUNK, h * CHUNK:(h + 1) * CHUNK],
                             preferred_element_type=jnp.float32) + bsb_ref[h]
                cva_ref[n * CHUNK:(n + 1) * CHUNK, h * CHUNK:(h + 1) * CHUNK] = (
                    u[n * CHUNK:(n + 1) * CHUNK, hh * CHUNK:(hh + 1) * CHUNK] * sv
                ).astype(jnp.bfloat16)
        b_conv(2, late_rows(c))
    for c in range(n_chunks):
        c0 = c * CW
        yc = jnp.dot(cva_ref[...], wc_ref[:, c0:c0 + CW], preferred_element_type=jnp.float32)
        gc = _sigmoid(proj(HALO, HALO + TM, C_G + 2 * D + c0, CW))
        mg_ref[:, c0:c0 + CW] += gc * yc
        b_conv(3, late_rows(c))

    cva_ref[...] = mg_ref[...].astype(jnp.bfloat16)
    x_res = x_cur()
    for c in range(n_chunks):
        c0 = c * CW
        vst_ref[:, c0:c0 + CW] = x_res[:, c0:c0 + CW] + jnp.dot(
            cva_ref[...], wo_ref[:, c0:c0 + CW], preferred_element_type=jnp.float32)

    zc = _layer_norm(xo_ref[...], lnbg_ref[...], lnbb_ref[...])
    cvb_ref[...] = (zc * _sigmoid(zc)).astype(jnp.bfloat16)
    for c in range(n_chunks):
        c0 = c * CW
        yb = jnp.dot(cvb_ref[...], wb_ref[:, c0:c0 + CW], preferred_element_type=jnp.float32)
        gb = _sigmoid(proj(HALO, HALO + TM, C_G + D + c0, CW))
        mg_ref[:, c0:c0 + CW] = gb * yb
    cvb_ref[...] = mg_ref[...].astype(jnp.bfloat16)
    for c in range(n_chunks):
        c0 = c * CW
        xo_ref[:, c0:c0 + CW] = vst_ref[:, c0:c0 + CW] + jnp.dot(
            cvb_ref[...], wo_ref[:, c0:c0 + CW], preferred_element_type=jnp.float32)
    xnew = xo_ref[...]
    ms = jnp.mean(xnew * xnew, axis=-1, keepdims=True)
    h2 = xnew * lax.rsqrt(ms + RMS_EPS) * nf_ref[...]
    h2_packed = _pack_bf16_pairs(h2)
    for j in range(NPIECE):
        h2_ref[j] = lax.bitcast_convert_type(h2_packed[:, j * 128:(j + 1) * 128], jnp.int32)
    h_hi = h2.astype(jnp.bfloat16)
    h_lo = (h2 - h_hi.astype(jnp.float32)).astype(jnp.bfloat16)
    dot32 = functools.partial(jnp.dot, preferred_element_type=jnp.float32)
    logits = (dot32(h_hi, wrh_ref[...]) + dot32(h_lo, wrh_ref[...])
              + dot32(h_hi, wrl_ref[...]) + dot32(h_lo, wrl_ref[...]) + br_ref[...])
    _route_tile(logits, idx_ref, wgt_ref, cnt_ref, carry_ref)


def _layer_spec(layer, shape):
    nd = len(shape)
    return pl.BlockSpec((None,) + tuple(shape), lambda i, _n=nd: (layer,) + (0,) * _n,
                        pipeline_mode=pl.Buffered(1))


def _mixer(layer, x2, pending, seq_len, nm, win, ca, wa, cb, lnbg, lnbb, wb, lncg, lncb, ws, bsb, wc, wo,
           nf, wrh, wrl, br, sh):
    T = x2.shape[0]
    n_tiles = T // TM
    hb = TM // HALO
    last_halo = T // HALO - 1
    ls = functools.partial(_layer_spec, layer)
    prev_rows = lambda i: jnp.maximum(i * hb - 1, 0)
    next_rows = lambda i: jnp.minimum((i + 1) * hb, last_halo)
    in_specs = [
        pl.BlockSpec((HALO, D), lambda i: (prev_rows(i), 0)),
        pl.BlockSpec((TM, D), lambda i: (i, 0)),
        pl.BlockSpec((HALO, D), lambda i: (next_rows(i), 0)),
    ]
    operands = [x2, x2, x2]
    if pending is not None:
        gathered, wgt_rows = pending
        in_specs += [
            pl.BlockSpec((2, NPIECE, HALO, 128), lambda i: (0, 0, prev_rows(i), 0)),
            pl.BlockSpec((2, NPIECE, TM, 128), lambda i: (0, 0, i, 0)),
            pl.BlockSpec((2, NPIECE, HALO, 128), lambda i: (0, 0, next_rows(i), 0)),
            pl.BlockSpec((HALO, 8), lambda i: (prev_rows(i), 0)),
            pl.BlockSpec((TM, 8), lambda i: (i, 0)),
            pl.BlockSpec((HALO, 8), lambda i: (next_rows(i), 0)),
        ]
        operands += [gathered, gathered, gathered, wgt_rows, wgt_rows, wgt_rows]
    in_specs += [
        ls((1, D)), ls((D, D_IN)), ls((CONV_A, D)), ls((D, D)),
        ls((CONV_B, D)), ls((1, D)), ls((1, D)), ls((D, D)),
        ls((1, D)), ls((1, D)), ls((N_HEADS_C, CHUNK, CHUNK)),
        ls((N_HEADS_C, CHUNK, CHUNK)), ls((D, D)), ls((D, D)),
        ls((1, D)), ls((D, NLOG)), ls((D, NLOG)), ls((1, NLOG)),
        pl.BlockSpec((RB, SK), lambda i: (0, 0), pipeline_mode=pl.Buffered(1)),
    ]
    out_specs = [
        pl.BlockSpec((TM, D), lambda i: (i, 0)),
        pl.BlockSpec((NPIECE, TM, 128), lambda i: (0, i, 0)),
        pl.BlockSpec((8, TM), lambda i: (0, i)),
        pl.BlockSpec((8, TM), lambda i: (0, i)),
        pl.BlockSpec((N_EXPERTS, 128), lambda i: (0, 0)),
    ]
    return pl.pallas_call(
        functools.partial(_mixer_kernel, seq_len // TM, pending is not None),
        grid=(n_tiles,),
        in_specs=in_specs,
        out_specs=out_specs,
        out_shape=[jax.ShapeDtypeStruct((T, D), jnp.float32),
                   jax.ShapeDtypeStruct((NPIECE, T, 128), jnp.int32),
                   jax.ShapeDtypeStruct((8, T), jnp.int32),
                   jax.ShapeDtypeStruct((8, T), jnp.float32),
                   jax.ShapeDtypeStruct((N_EXPERTS, 128), jnp.float32)],
        scratch_shapes=[pltpu.VMEM((TE, D), jnp.bfloat16),
                        pltpu.VMEM((D // CW, TE, CW), jnp.float32),
                        pltpu.VMEM((TM, D), jnp.bfloat16),
                        pltpu.VMEM((TM, D), jnp.bfloat16),
                        pltpu.VMEM((TM, D), jnp.bfloat16),
                        pltpu.VMEM((TM, D), jnp.float32),
                        pltpu.VMEM((TM, D), jnp.float32),
                        pltpu.VMEM((N_EXPERTS, 128), jnp.float32)],
        compiler_params=pltpu.CompilerParams(dimension_semantics=("arbitrary",),
                                             vmem_limit_bytes=VMEM_LIMIT),
        name="mixer",
    )(*operands, nm, win, ca, wa, cb, lnbg, lnbb, wb, lncg, lncb, ws, bsb, wc, wo, nf, wrh, wrl, br, sh)


def _route_tile(lg, idx_ref, wgt_ref, cnt_ref, carry_ref):
    i = pl.program_id(0)

    @pl.when(i == 0)
    def _():
        carry_ref[...] = jnp.zeros_like(carry_ref)

    lt = lg.T
    g = [lt[j:j + 1, :] for j in range(N_GROUPS)]
    gmax = jnp.maximum(jnp.maximum(g[0], g[1]), jnp.maximum(g[2], g[3]))
    gidx = jnp.where(g[0] == gmax, 0.0, jnp.where(g[1] == gmax, 1.0, jnp.where(g[2] == gmax, 2.0, 3.0)))
    gsum = sum(jnp.exp(gj - gmax) for gj in g)
    g_p = 1.0 / gsum

    sel = lt[E_OFF + 3 * EPG:E_OFF + 4 * EPG, :]
    for j in (2, 1, 0):
        sel = jnp.where(gidx == float(j), lt[E_OFF + j * EPG:E_OFF + (j + 1) * EPG, :], sel)
    rid = lax.broadcasted_iota(jnp.int32, (EPG, TR), 0).astype(jnp.float32)
    m1 = jnp.max(sel, axis=0, keepdims=True)
    i1 = jnp.min(jnp.where(sel == m1, rid, float(EPG)), axis=0, keepdims=True)
    rest = jnp.where(rid == i1, -jnp.inf, sel)
    m2 = jnp.max(rest, axis=0, keepdims=True)
    i2 = jnp.min(jnp.where(rest == m2, rid, float(EPG)), axis=0, keepdims=True)
    e2x = jnp.exp(m2 - m1)
    den = 1.0 + e2x
    w1 = (1.0 / den) * g_p
    w2 = (e2x / den) * g_p
    e1 = gidx * float(EPG) + i1
    e2 = gidx * float(EPG) + i2

    eid = lax.broadcasted_iota(jnp.int32, (N_EXPERTS, TR), 0).astype(jnp.float32)
    oh1 = (eid == e1).astype(jnp.float32)
    oh2 = (eid == e2).astype(jnp.float32)
    oh = oh1 + oh2
    tr = lax.broadcasted_iota(jnp.int32, (TR, TR), 0)
    tc = lax.broadcasted_iota(jnp.int32, (TR, TR), 1)
    upper = (tr < tc).astype(jnp.bfloat16)
    before = jnp.dot(oh.astype(jnp.bfloat16), upper, preferred_element_type=jnp.float32)
    base = before + carry_ref[:, 0:1]
    r1 = jnp.sum(oh1 * base, axis=0, keepdims=True)
    r2 = jnp.sum(oh2 * base, axis=0, keepdims=True)
    carry_ref[...] = carry_ref[...] + jnp.sum(oh, axis=1, keepdims=True)

    idx_ref[...] = jnp.zeros_like(idx_ref)
    idx_ref[0:1, :] = e1.astype(jnp.int32)
    idx_ref[1:2, :] = e2.astype(jnp.int32)
    idx_ref[2:3, :] = r1.astype(jnp.int32)
    idx_ref[3:4, :] = r2.astype(jnp.int32)
    wgt_ref[...] = jnp.zeros_like(wgt_ref)
    wgt_ref[0:1, :] = w1
    wgt_ref[1:2, :] = w2
    cnt_ref[...] = carry_ref[...]


SC_WINDOW = 128


def _sc_mesh():
    return plsc.VectorSubcoreMesh(core_axis_name="c", subcore_axis_name="s")


def _sc_scatter_rows(rows, idx, passes, n_out):
    n_src = rows.shape[0]
    idx_lists = [idx[p * n_src:(p + 1) * n_src].reshape(1, n_src) for p in range(passes)]

    @functools.partial(
        pl.kernel, mesh=_sc_mesh(),
        out_type=jax.ShapeDtypeStruct((n_out, 128), rows.dtype))
    def scatter(rows_hbm, *refs):
        idx_hbms, out_hbm = refs[:passes], refs[passes]

        def body(rows_vmem, *idx_vmems):
            for idx_vmem in idx_vmems:
                pltpu.sync_copy(rows_vmem, out_hbm.at[idx_vmem.at[0]])

        pltpu.emit_pipeline(
            body,
            grid=(n_src // SC_WINDOW,),
            in_specs=[pl.BlockSpec((SC_WINDOW, 128), lambda i: (i, 0))]
            + [pl.BlockSpec((1, SC_WINDOW), lambda i: (0, i))] * passes,
            out_specs=[],
            core_axis_name=("c", "s"),
            dimension_semantics=(pltpu.PARALLEL,),
        )(rows_hbm, *idx_hbms)

    return scatter(rows, *idx_lists)


def _expert_kernel(layer, be_ref, nu_ref, nxt_ref, gs_ref, nv_ref, xb_ref, w1_ref, w3_ref, w2_ref, yb_ref,
                   wf1_ref, wf3_ref, wf2_ref, w1b_ref, w3b_ref, w2b_ref, sem):
    b = pl.program_id(0)
    used = b < nu_ref[0]
    new_expert = jnp.logical_or(b == 0, be_ref[b] != be_ref[jnp.maximum(b - 1, 0)])

    def weight_copies(e, s):
        return [pltpu.make_async_copy(src.at[layer, e], dst.at[s], sem.at[s])
                for src, dst in ((w1_ref, wf1_ref), (w3_ref, wf3_ref), (w2_ref, wf2_ref))]

    @pl.when(b == 0)
    def _():
        for cp in weight_copies(be_ref[0], 0):
            cp.start(priority=1)

    @pl.when(jnp.logical_and(used, new_expert))
    def _():
        s = gs_ref[b]
        for cp in weight_copies(be_ref[b], s):
            cp.wait()

        @pl.when(nxt_ref[b] >= 0)
        def _():
            for cp in weight_copies(nxt_ref[b], 1 - s):
                cp.start(priority=1)

        w1b_ref[...] = wf1_ref[s].astype(jnp.bfloat16)
        w3b_ref[...] = wf3_ref[s].astype(jnp.bfloat16)
        w2b_ref[...] = wf2_ref[s].astype(jnp.bfloat16)

    def expert_mlp(rows):
        live = lax.broadcasted_iota(jnp.int32, (rows, 1), 0) < nv_ref[b]
        packed_in = jnp.concatenate(
            [jnp.where(live, lax.bitcast_convert_type(xb_ref[j, 0:rows], jnp.uint32), jnp.uint32(0))
             for j in range(NPIECE)], axis=1)
        x = jnp.concatenate(_unpack_bf16_pairs(packed_in), axis=1).astype(jnp.bfloat16)
        dot32 = functools.partial(jnp.dot, preferred_element_type=jnp.float32)
        a = dot32(x, w1b_ref[...])
        g = dot32(x, w3b_ref[...])
        hmid = (a * _sigmoid(a) * g).astype(jnp.bfloat16)
        packed = _pack_bf16_pairs(dot32(hmid, w2b_ref[...]))
        for j in range(NPIECE):
            yb_ref[j, 0:rows] = lax.bitcast_convert_type(packed[:, j * 128:(j + 1) * 128], jnp.int32)

    half = BM // 2
    half_full = nv_ref[b] <= half

    @pl.when(jnp.logical_and(used, jnp.logical_not(half_full)))
    def _():
        expert_mlp(BM)

    @pl.when(jnp.logical_and(used, half_full))
    def _():
        expert_mlp(half)
        yb_ref[:, half:BM, :] = jnp.zeros((NPIECE, BM - half, 128), jnp.int32)

    @pl.when(jnp.logical_not(used))
    def _():
        yb_ref[...] = jnp.zeros_like(yb_ref)


def _experts(layer, block_expert, n_used, next_expert, group_slot, block_valid, xb, w1, w3, w2):
    n_slots = xb.shape[1]
    n_blocks = n_slots // BM

    def row_map(b, be, nu, nxt, gs, nv):
        return (0, jnp.minimum(b, nu[0] - 1), 0)

    def out_map(b, be, nu, nxt, gs, nv):
        return (0, b, 0)

    grid_spec = pltpu.PrefetchScalarGridSpec(
        num_scalar_prefetch=5,
        grid=(n_blocks,),
        in_specs=[pl.BlockSpec((NPIECE, BM, 128), row_map),
                  pl.BlockSpec(memory_space=pl.ANY),
                  pl.BlockSpec(memory_space=pl.ANY),
                  pl.BlockSpec(memory_space=pl.ANY)],
        out_specs=pl.BlockSpec((NPIECE, BM, 128), out_map),
        scratch_shapes=[pltpu.VMEM((2, D, D_FF), jnp.float32),
                        pltpu.VMEM((2, D, D_FF), jnp.float32),
                        pltpu.VMEM((2, D_FF, D), jnp.float32),
                        pltpu.VMEM((D, D_FF), jnp.bfloat16),
                        pltpu.VMEM((D, D_FF), jnp.bfloat16),
                        pltpu.VMEM((D_FF, D), jnp.bfloat16),
                        pltpu.SemaphoreType.DMA((2,))],
    )
    return pl.pallas_call(
        functools.partial(_expert_kernel, layer),
        grid_spec=grid_spec,
        out_shape=jax.ShapeDtypeStruct((NPIECE, n_slots, 128), jnp.int32),
        compiler_params=pltpu.CompilerParams(dimension_semantics=("arbitrary",),
                                             vmem_limit_bytes=VMEM_LIMIT),
        name="experts",
    )(block_expert, n_used, next_expert, group_slot, block_valid, xb, w1, w3, w2)


def _sc_gather_rows(table, idx):
    n_rows = idx.shape[0]

    @functools.partial(
        pl.kernel, mesh=_sc_mesh(),
        out_type=jax.ShapeDtypeStruct((n_rows, 128), table.dtype),
        scratch_types=[pltpu.SemaphoreType.DMA((2,))])
    def gather(table_hbm, idx_hbm, out_hbm, sem):
        def body(idx0_vmem, idx1_vmem, out0_vmem, out1_vmem):
            first = pltpu.async_copy(table_hbm.at[idx0_vmem.at[0]], out0_vmem, sem.at[0])
            second = pltpu.async_copy(table_hbm.at[idx1_vmem.at[0]], out1_vmem, sem.at[1])
            first.wait()
            second.wait()

        pltpu.emit_pipeline(
            body,
            grid=(n_rows // (2 * SC_WINDOW),),
            in_specs=[pl.BlockSpec((1, SC_WINDOW), lambda i: (0, 2 * i)),
                      pl.BlockSpec((1, SC_WINDOW), lambda i: (0, 2 * i + 1))],
            out_specs=[pl.BlockSpec((SC_WINDOW, 128), lambda i: (2 * i, 0)),
                       pl.BlockSpec((SC_WINDOW, 128), lambda i: (2 * i + 1, 0))],
            core_axis_name=("c", "s"),
            dimension_semantics=(pltpu.PARALLEL,),
        )(idx_hbm, idx_hbm, out_hbm, out_hbm)

    return gather(table, idx.reshape(1, n_rows))


def _combine_kernel(final, g_ref, x_ref, w_ref, nrm_ref, o_ref):
    w = w_ref[...].T
    ys = {}
    for j in range(NPIECE):
        hi0, lo0 = _unpack_bf16_pairs(lax.bitcast_convert_type(g_ref[0, j], jnp.uint32))
        hi1, lo1 = _unpack_bf16_pairs(lax.bitcast_convert_type(g_ref[1, j], jnp.uint32))
        c_hi, c_lo = j * 128, DP + j * 128
        ys[c_hi] = x_ref[:, c_hi:c_hi + 128] + w[:, 0:1] * hi0 + w[:, 1:2] * hi1
        ys[c_lo] = x_ref[:, c_lo:c_lo + 128] + w[:, 0:1] * lo0 + w[:, 1:2] * lo1
    if final:
        ms = sum(jnp.sum(y * y, axis=-1, keepdims=True) for y in ys.values()) * (1.0 / D)
        scale = lax.rsqrt(ms + RMS_EPS)
        ys = {c: y * scale * nrm_ref[:, c:c + 128] for c, y in ys.items()}
    for c, y in ys.items():
        o_ref[:, c:c + 128] = y


def _combine(gathered, x2, wgt, nrm, final):
    T = x2.shape[0]
    return pl.pallas_call(
        functools.partial(_combine_kernel, final),
        grid=(T // TD,),
        in_specs=[pl.BlockSpec((2, NPIECE, TD, 128), lambda i: (0, 0, i, 0)),
                  pl.BlockSpec((TD, D), lambda i: (i, 0)),
                  pl.BlockSpec((8, TD), lambda i: (0, i)),
                  pl.BlockSpec((1, D), lambda i: (0, 0))],
        out_specs=pl.BlockSpec((TD, D), lambda i: (i, 0)),
        out_shape=jax.ShapeDtypeStruct((T, D), jnp.float32),
        compiler_params=pltpu.CompilerParams(dimension_semantics=("arbitrary",)),
        name="combine",
    )(gathered, x2, wgt, nrm)


def _moe(layer, h2, idx, cnt, w1, w3, w2):
    T = idx.shape[1]
    counts = cnt[:, 0].astype(jnp.int32)
    padded = (counts + BM - 1) // BM * BM
    padded_end = jnp.cumsum(padded)
    padded_start = padded_end - padded
    eids = jnp.arange(N_EXPERTS, dtype=jnp.int32)[:, None]
    start_of = lambda e: jnp.sum(jnp.where(e[None, :] == eids, padded_start[:, None], 0), axis=0)
    dest = jnp.stack([start_of(idx[0]) + idx[2], start_of(idx[1]) + idx[3]])
    n_blocks = (2 * T) // BM + N_EXPERTS
    block_start = jnp.arange(n_blocks, dtype=jnp.int32) * BM
    block_expert = jnp.minimum(
        jnp.sum((padded_end[None, :] <= block_start[:, None]).astype(jnp.int32), axis=1),
        N_EXPERTS - 1)
    n_used = (padded_end[-1] // BM).reshape(1)
    later = jnp.logical_and(eids.T > eids, (counts > 0)[None, :])
    next_of = jnp.min(jnp.where(later, eids.T, N_EXPERTS), axis=1)
    next_of = jnp.where(next_of < N_EXPERTS, next_of, -1)
    pick = lambda table: jnp.sum(
        jnp.where(block_expert[:, None] == eids.T, table[None, :], 0), axis=1)
    next_expert = pick(next_of)
    group_slot = pick(jnp.cumsum((counts > 0).astype(jnp.int32)) - 1) % 2
    block_valid = jnp.clip(pick(padded_start + counts) - block_start, 0, BM)
    n_slots = n_blocks * BM
    piece_base = (jnp.arange(NPIECE, dtype=jnp.int32) * n_slots)[None, :, None]
    rows = (dest[:, None, :] + piece_base).reshape(-1)
    xb = _sc_scatter_rows(h2.reshape(NPIECE * T, 128), rows, 2, NPIECE * n_slots)
    yb = _experts(layer, block_expert, n_used, next_expert, group_slot, block_valid,
                  xb.reshape(NPIECE, n_slots, 128), w1, w3, w2)
    gathered = _sc_gather_rows(yb.reshape(NPIECE * n_slots, 128), rows)
    return gathered.reshape(2, NPIECE, T, 128)


def kernel(x, norm_mix, w_in, conv_a, w_a_out, conv_b, ln_b_g, ln_b_b, w_b_out, ln_c_g, ln_c_b,
           w_s, b_s, w_c_out, w_o, norm_ffn, w_group, b_group, w_router, b_router, w1, w3, w2,
           norm_final):
    bsz, seq, d = x.shape
    depth = norm_mix.shape[0]
    bf = jnp.bfloat16
    x2 = x.reshape(bsz * seq, d)
    wr = jnp.zeros((depth, d, NLOG), jnp.float32)
    wr = wr.at[:, :, 0:N_GROUPS].set(w_group).at[:, :, E_OFF:E_OFF + N_EXPERTS].set(w_router)
    wr_hi = wr.astype(bf)
    wr_lo = (wr - wr_hi.astype(jnp.float32)).astype(bf)
    br = jnp.zeros((depth, 1, NLOG), jnp.float32)
    br = br.at[:, 0, 0:N_GROUPS].set(b_group).at[:, 0, E_OFF:E_OFF + N_EXPERTS].set(b_router)
    bsb = jnp.broadcast_to(b_s[:, :, :, None], (depth, N_HEADS_C, CHUNK, CHUNK))
    row = lambda p: p[:, None, :]
    mixer_params = (row(norm_mix), w_in.astype(bf), conv_a, w_a_out.astype(bf), conv_b,
                    row(ln_b_g), row(ln_b_b), w_b_out.astype(bf), row(ln_c_g), row(ln_c_b),
                    w_s.astype(bf), bsb, w_c_out.astype(bf), w_o.astype(bf), row(norm_ffn), wr_hi, wr_lo, br)
    col = jnp.arange(SK, dtype=jnp.int32)[None, :]
    shift_sum = jnp.logical_and(
        col % PR == jnp.arange(RB, dtype=jnp.int32)[:, None] + col // PR, col < 8 * PR).astype(bf)
    pending = None
    for l in range(depth):
        x2, h2, idx, wgt, cnt = _mixer(l, x2, pending, seq, *mixer_params, shift_sum)
        gathered = _moe(l, h2, idx, cnt, w1, w3, w2)
        pending = (gathered, wgt.T)
    out = _combine(gathered, x2, wgt, norm_final[None], True)
    return out.reshape(bsz, seq, d)
```

```python
import functools

import jax
import jax.numpy as jnp
from jax import lax
from jax.experimental import pallas as pl
from jax.experimental.pallas import tpu as pltpu
from jax.experimental.pallas import tpu_sc as plsc

D = 1024
DP = D // 2
NPIECE = DP // 128
N_HEADS_C = 8
CHUNK = 128
CONV_A = 3
CONV_B = 31
N_GROUPS = 4
EPG = 8
N_EXPERTS = N_GROUPS * EPG
D_FF = 512
RMS_EPS = 1e-6
LN_EPS = 1e-5

C_XA, C_BA, C_CA, C_VB, C_GB, C_UV, C_G = 0, 1024, 2048, 3072, 4096, 5120, 7168
D_IN = 10240

TM = 512
HALO = 16
TE = TM + 2 * HALO
CW = 256
RB = 128
PR = RB + 8
SK = -(-8 * PR // 128) * 128
NLOG = 128
E_OFF = 8

TR = TM
BM = 512
TD = 512

VMEM_LIMIT = 60 * 1024 * 1024


def _sigmoid(x):
    return 0.5 * (jnp.tanh(0.5 * x) + 1.0)


def _gelu_tanh(x):
    return 0.5 * x * (1.0 + jnp.tanh(0.7978845608028654 * (x + 0.044715 * (x * x * x))))


def _pack_bf16_pairs(x):
    c = x.shape[1] // 2
    as_bits = lambda v: lax.bitcast_convert_type(v.astype(jnp.bfloat16).astype(jnp.float32), jnp.uint32)
    return as_bits(x[:, :c]) | (as_bits(x[:, c:]) >> 16)


def _unpack_bf16_pairs(p):
    hi = lax.bitcast_convert_type(p & jnp.uint32(0xFFFF0000), jnp.float32)
    lo = lax.bitcast_convert_type(p << 16, jnp.float32)
    return hi, lo


def _layer_norm(x, g, b):
    mu = jnp.mean(x, axis=-1, keepdims=True)
    xc = x - mu
    var = jnp.mean(xc * xc, axis=-1, keepdims=True)
    return xc * lax.rsqrt(var + LN_EPS) * g + b


def _moe_residual(x_ref, g_ref, wt_ref):
    w0, w1 = wt_ref[:, 0:1], wt_ref[:, 1:2]
    his, los = [], []
    for j in range(NPIECE):
        hi0, lo0 = _unpack_bf16_pairs(lax.bitcast_convert_type(g_ref[0, j], jnp.uint32))
        hi1, lo1 = _unpack_bf16_pairs(lax.bitcast_convert_type(g_ref[1, j], jnp.uint32))
        his.append(w0 * hi0 + w1 * hi1)
        los.append(w0 * lo0 + w1 * lo1)
    return x_ref[...] + jnp.concatenate(his + los, axis=1)


def _mixer_kernel(seq_tiles, fused_combine, xp_ref, xc_ref, xn_ref, *refs):
    if fused_combine:
        gp_ref, gc_ref, gn_ref, wtp_ref, wtc_ref, wtn_ref = refs[:6]
        refs = refs[6:]
        x_prev = lambda: _moe_residual(xp_ref, gp_ref, wtp_ref)
        x_cur = lambda: _moe_residual(xc_ref, gc_ref, wtc_ref)
        x_next = lambda: _moe_residual(xn_ref, gn_ref, wtn_ref)
    else:
        x_prev, x_cur, x_next = (lambda: xp_ref[...]), (lambda: xc_ref[...]), (lambda: xn_ref[...])
    (nm_ref, win_ref, ca_ref, wa_ref, cb_ref, lnbg_ref, lnbb_ref, wb_ref, lncg_ref, lncb_ref,
     ws_ref, bsb_ref, wc_ref, wo_ref, nf_ref, wrh_ref, wrl_ref, br_ref, sh_ref,
     xo_ref, h2_ref, idx_ref, wgt_ref, cnt_ref,
     hb_ref, zq_ref, cva_ref, cvb_ref, cvn_ref, vst_ref, mg_ref, carry_ref) = refs
    i = pl.program_id(0)
    at_start = (i % seq_tiles) == 0
    at_end = (i % seq_tiles) == seq_tiles - 1

    def _rms(xv):
        ms = jnp.mean(xv * xv, axis=-1, keepdims=True)
        return (xv * lax.rsqrt(ms + RMS_EPS) * nm_ref[...]).astype(jnp.bfloat16)

    hb_ref[0:HALO, :] = _rms(x_prev())
    hb_ref[HALO:HALO + TM, :] = _rms(x_cur())
    hb_ref[HALO + TM:TE, :] = _rms(x_next())

    rows = lax.broadcasted_iota(jnp.int32, (TE, 1), 0)
    lo = jnp.where(at_start, HALO, 0)
    hi = jnp.where(at_end, HALO + TM, TE)
    valid = jnp.logical_and(rows >= lo, rows < hi)

    def proj(r0, r1, c0, width):
        return jnp.dot(hb_ref[r0:r1, :], win_ref[:, c0:c0 + width],
                       preferred_element_type=jnp.float32)

    def b_proj(c):
        c0 = c * CW
        val = proj(0, TE, C_VB + c0, CW)
        gate = proj(0, TE, C_GB + c0, CW)
        zq_ref[c] = jnp.where(valid, val * _sigmoid(gate), 0.0)

    def a_proj(c):
        c0 = c * CW
        xa = proj(0, TE, C_XA + c0, CW)
        cc = proj(0, TE, C_CA + c0, CW)
        t = jnp.where(valid, xa * cc, 0.0)
        conv = (ca_ref[0:1, c0:c0 + CW] * t[HALO - 1:HALO - 1 + TM]
                + ca_ref[1:2, c0:c0 + CW] * t[HALO:HALO + TM]
                + ca_ref[2:3, c0:c0 + CW] * t[HALO + 1:HALO + 1 + TM])
        ba = proj(HALO, HALO + TM, C_BA + c0, CW)
        cva_ref[:, c0:c0 + CW] = (ba * conv).astype(jnp.bfloat16)

    def c_v_proj(c):
        c0 = c * CW
        vst_ref[:, c0:c0 + CW] = _gelu_tanh(proj(HALO, HALO + TM, C_UV + D + c0, CW))

    def b_conv(c, row_blocks):
        c0 = c * CW
        zq = zq_ref.at[c]
        for rb in row_blocks:
            r0 = rb * RB
            for lt in range(CW // 128):
                l0 = lt * 128
                parts = []
                for s in range(8):
                    part = None
                    for q in range(4):
                        k = 8 * q + s - (HALO - CONV_B // 2)
                        if 0 <= k < CONV_B:
                            term = (cb_ref[k:k + 1, c0 + l0:c0 + l0 + 128]
                                    * zq[r0 + 8 * q:r0 + 8 * q + PR, l0:l0 + 128])
                            part = term if part is None else part + term
                    parts.append(part)
                if SK > 8 * PR:
                    parts.append(jnp.zeros((SK - 8 * PR, 128), jnp.float32))
                stacked = jnp.concatenate(parts, axis=0).astype(jnp.bfloat16)
                xo_ref[r0:r0 + RB, c0 + l0:c0 + l0 + 128] = jnp.dot(
                    sh_ref[...], stacked, preferred_element_type=jnp.float32)

    n_chunks = D // CW
    n_rb = TM // RB
    rb_step = n_rb // n_chunks
    late_rows = lambda c: range(c * rb_step, (c + 1) * rb_step)
    b_proj(0)
    for c in range(n_chunks):
        if c + 1 < n_chunks:
            b_proj(c + 1)
        a_proj(c)
        c_v_proj(c)
    b_conv(0, range(n_rb))

    vn = _layer_norm(vst_ref[...], lncg_ref[...], lncb_ref[...])
    cvn_ref[...] = vn.astype(jnp.bfloat16)
    for c in range(n_chunks):
        c0 = c * CW
        ya = jnp.dot(cva_ref[...], wa_ref[:, c0:c0 + CW], preferred_element_type=jnp.float32)
        ga = _sigmoid(proj(HALO, HALO + TM, C_G + c0, CW))
        mg_ref[:, c0:c0 + CW] = ga * ya
        b_conv(1, late_rows(c))

    for c in range(n_chunks):
        c0 = c * CW
        u = _gelu_tanh(proj(HALO, HALO + TM, C_UV + c0, CW))
        for n in range(TM // CHUNK):
            for hh in range(CW // CHUNK):
                h = c * (CW // CHUNK) + hh
                sv = jnp.dot(ws_ref[h], cvn_ref[n * CHUNK:(n + 1) * CHUNK, h * CHUNK:(h + 1) * CHUNK],
                             preferred_element_type=jnp.float32) + bsb_ref[h]
                cva_ref[n * CHUNK:(n + 1) * CHUNK, h * CHUNK:(h + 1) * CHUNK] = (
                    u[n * CHUNK:(n + 1) * CHUNK, hh * CHUNK:(hh + 1) * CHUNK] * sv
                ).astype(jnp.bfloat16)
        b_conv(2, late_rows(c))
    for c in range(n_chunks):
        c0 = c * CW
        yc = jnp.dot(cva_ref[...], wc_ref[:, c0:c0 + CW], preferred_element_type=jnp.float32)
        gc = _sigmoid(proj(HALO, HALO + TM, C_G + 2 * D + c0, CW))
        mg_ref[:, c0:c0 + CW] += gc * yc
        b_conv(3, late_rows(c))

    zc = _layer_norm(xo_ref[...], lnbg_ref[...], lnbb_ref[...])
    cvb_ref[...] = (zc * _sigmoid(zc)).astype(jnp.bfloat16)
    for c in range(n_chunks):
        c0 = c * CW
        yb = jnp.dot(cvb_ref[...], wb_ref[:, c0:c0 + CW], preferred_element_type=jnp.float32)
        gb = _sigmoid(proj(HALO, HALO + TM, C_G + D + c0, CW))
        mg_ref[:, c0:c0 + CW] += gb * yb

    cvb_ref[...] = mg_ref[...].astype(jnp.bfloat16)
    x_res = x_cur()
    for c in range(n_chunks):
        c0 = c * CW
        xo_ref[:, c0:c0 + CW] = x_res[:, c0:c0 + CW] + jnp.dot(
            cvb_ref[...], wo_ref[:, c0:c0 + CW], preferred_element_type=jnp.float32)
    xnew = xo_ref[...]
    ms = jnp.mean(xnew * xnew, axis=-1, keepdims=True)
    h2 = xnew * lax.rsqrt(ms + RMS_EPS) * nf_ref[...]
    h2_packed = _pack_bf16_pairs(h2)
    for j in range(NPIECE):
        h2_ref[j] = lax.bitcast_convert_type(h2_packed[:, j * 128:(j + 1) * 128], jnp.int32)
    h_hi = h2.astype(jnp.bfloat16)
    h_lo = (h2 - h_hi.astype(jnp.float32)).astype(jnp.bfloat16)
    dot32 = functools.partial(jnp.dot, preferred_element_type=jnp.float32)
    logits = (dot32(h_hi, wrh_ref[...]) + dot32(h_lo, wrh_ref[...])
              + dot32(h_hi, wrl_ref[...]) + dot32(h_lo, wrl_ref[...]) + br_ref[...])
    _route_tile(logits, idx_ref, wgt_ref, cnt_ref, carry_ref)


def _layer_spec(layer, shape):
    nd = len(shape)
    return pl.BlockSpec((None,) + tuple(shape), lambda i, _n=nd: (layer,) + (0,) * _n,
                        pipeline_mode=pl.Buffered(1))


def _mixer(layer, x2, pending, seq_len, nm, win, ca, wa, cb, lnbg, lnbb, wb, lncg, lncb, ws, bsb, wc, wo,
           nf, wrh, wrl, br, sh):
    T = x2.shape[0]
    n_tiles = T // TM
    hb = TM // HALO
    last_halo = T // HALO - 1
    ls = functools.partial(_layer_spec, layer)
    prev_rows = lambda i: jnp.maximum(i * hb - 1, 0)
    next_rows = lambda i: jnp.minimum((i + 1) * hb, last_halo)
    in_specs = [
        pl.BlockSpec((HALO, D), lambda i: (prev_rows(i), 0)),
        pl.BlockSpec((TM, D), lambda i: (i, 0)),
        pl.BlockSpec((HALO, D), lambda i: (next_rows(i), 0)),
    ]
    operands = [x2, x2, x2]
    if pending is not None:
        gathered, wgt_rows = pending
        in_specs += [
            pl.BlockSpec((2, NPIECE, HALO, 128), lambda i: (0, 0, prev_rows(i), 0)),
            pl.BlockSpec((2, NPIECE, TM, 128), lambda i: (0, 0, i, 0)),
            pl.BlockSpec((2, NPIECE, HALO, 128), lambda i: (0, 0, next_rows(i), 0)),
            pl.BlockSpec((HALO, 8), lambda i: (prev_rows(i), 0)),
            pl.BlockSpec((TM, 8), lambda i: (i, 0)),
            pl.BlockSpec((HALO, 8), lambda i: (next_rows(i), 0)),
        ]
        operands += [gathered, gathered, gathered, wgt_rows, wgt_rows, wgt_rows]
    in_specs += [
        ls((1, D)), ls((D, D_IN)), ls((CONV_A, D)), ls((D, D)),
        ls((CONV_B, D)), ls((1, D)), ls((1, D)), ls((D, D)),
        ls((1, D)), ls((1, D)), ls((N_HEADS_C, CHUNK, CHUNK)),
        ls((N_HEADS_C, CHUNK, CHUNK)), ls((D, D)), ls((D, D)),
        ls((1, D)), ls((D, NLOG)), ls((D, NLOG)), ls((1, NLOG)),
        pl.BlockSpec((RB, SK), lambda i: (0, 0), pipeline_mode=pl.Buffered(1)),
    ]
    out_specs = [
        pl.BlockSpec((TM, D), lambda i: (i, 0)),
        pl.BlockSpec((NPIECE, TM, 128), lambda i: (0, i, 0)),
        pl.BlockSpec((8, TM), lambda i: (0, i)),
        pl.BlockSpec((8, TM), lambda i: (0, i)),
        pl.BlockSpec((N_EXPERTS, 128), lambda i: (0, 0)),
    ]
    return pl.pallas_call(
        functools.partial(_mixer_kernel, seq_len // TM, pending is not None),
        grid=(n_tiles,),
        in_specs=in_specs,
        out_specs=out_specs,
        out_shape=[jax.ShapeDtypeStruct((T, D), jnp.float32),
                   jax.ShapeDtypeStruct((NPIECE, T, 128), jnp.int32),
                   jax.ShapeDtypeStruct((8, T), jnp.int32),
                   jax.ShapeDtypeStruct((8, T), jnp.float32),
                   jax.ShapeDtypeStruct((N_EXPERTS, 128), jnp.float32)],
        scratch_shapes=[pltpu.VMEM((TE, D), jnp.bfloat16),
                        pltpu.VMEM((D // CW, TE, CW), jnp.float32),
                        pltpu.VMEM((TM, D), jnp.bfloat16),
                        pltpu.VMEM((TM, D), jnp.bfloat16),
                        pltpu.VMEM((TM, D), jnp.bfloat16),
                        pltpu.VMEM((TM, D), jnp.float32),
                        pltpu.VMEM((TM, D), jnp.float32),
                        pltpu.VMEM((N_EXPERTS, 128), jnp.float32)],
        compiler_params=pltpu.CompilerParams(dimension_semantics=("arbitrary",),
                                             vmem_limit_bytes=VMEM_LIMIT),
        name="mixer",
    )(*operands, nm, win, ca, wa, cb, lnbg, lnbb, wb, lncg, lncb, ws, bsb, wc, wo, nf, wrh, wrl, br, sh)


def _route_tile(lg, idx_ref, wgt_ref, cnt_ref, carry_ref):
    i = pl.program_id(0)

    @pl.when(i == 0)
    def _():
        carry_ref[...] = jnp.zeros_like(carry_ref)

    lt = lg.T
    g = [lt[j:j + 1, :] for j in range(N_GROUPS)]
    gmax = jnp.maximum(jnp.maximum(g[0], g[1]), jnp.maximum(g[2], g[3]))
    gidx = jnp.where(g[0] == gmax, 0.0, jnp.where(g[1] == gmax, 1.0, jnp.where(g[2] == gmax, 2.0, 3.0)))
    gsum = sum(jnp.exp(gj - gmax) for gj in g)
    g_p = 1.0 / gsum

    sel = lt[E_OFF + 3 * EPG:E_OFF + 4 * EPG, :]
    for j in (2, 1, 0):
        sel = jnp.where(gidx == float(j), lt[E_OFF + j * EPG:E_OFF + (j + 1) * EPG, :], sel)
    rid = lax.broadcasted_iota(jnp.int32, (EPG, TR), 0).astype(jnp.float32)
    m1 = jnp.max(sel, axis=0, keepdims=True)
    i1 = jnp.min(jnp.where(sel == m1, rid, float(EPG)), axis=0, keepdims=True)
    rest = jnp.where(rid == i1, -jnp.inf, sel)
    m2 = jnp.max(rest, axis=0, keepdims=True)
    i2 = jnp.min(jnp.where(rest == m2, rid, float(EPG)), axis=0, keepdims=True)
    e2x = jnp.exp(m2 - m1)
    den = 1.0 + e2x
    w1 = (1.0 / den) * g_p
    w2 = (e2x / den) * g_p
    e1 = gidx * float(EPG) + i1
    e2 = gidx * float(EPG) + i2

    eid = lax.broadcasted_iota(jnp.int32, (N_EXPERTS, TR), 0).astype(jnp.float32)
    oh1 = (eid == e1).astype(jnp.float32)
    oh2 = (eid == e2).astype(jnp.float32)
    oh = oh1 + oh2
    tr = lax.broadcasted_iota(jnp.int32, (TR, TR), 0)
    tc = lax.broadcasted_iota(jnp.int32, (TR, TR), 1)
    upper = (tr < tc).astype(jnp.bfloat16)
    before = jnp.dot(oh.astype(jnp.bfloat16), upper, preferred_element_type=jnp.float32)
    base = before + carry_ref[:, 0:1]
    r1 = jnp.sum(oh1 * base, axis=0, keepdims=True)
    r2 = jnp.sum(oh2 * base, axis=0, keepdims=True)
    carry_ref[...] = carry_ref[...] + jnp.sum(oh, axis=1, keepdims=True)

    idx_ref[...] = jnp.zeros_like(idx_ref)
    idx_ref[0:1, :] = e1.astype(jnp.int32)
    idx_ref[1:2, :] = e2.astype(jnp.int32)
    idx_ref[2:3, :] = r1.astype(jnp.int32)
    idx_ref[3:4, :] = r2.astype(jnp.int32)
    wgt_ref[...] = jnp.zeros_like(wgt_ref)
    wgt_ref[0:1, :] = w1
    wgt_ref[1:2, :] = w2
    cnt_ref[...] = carry_ref[...]


SC_WINDOW = 128


def _sc_mesh():
    return plsc.VectorSubcoreMesh(core_axis_name="c", subcore_axis_name="s")


def _sc_scatter_rows(rows, idx, passes, n_out):
    n_src = rows.shape[0]
    idx_lists = [idx[p * n_src:(p + 1) * n_src].reshape(1, n_src) for p in range(passes)]

    @functools.partial(
        pl.kernel, mesh=_sc_mesh(),
        out_type=jax.ShapeDtypeStruct((n_out, 128), rows.dtype))
    def scatter(rows_hbm, *refs):
        idx_hbms, out_hbm = refs[:passes], refs[passes]

        def body(rows_vmem, *idx_vmems):
            for idx_vmem in idx_vmems:
                pltpu.sync_copy(rows_vmem, out_hbm.at[idx_vmem.at[0]])

        pltpu.emit_pipeline(
            body,
            grid=(n_src // SC_WINDOW,),
            in_specs=[pl.BlockSpec((SC_WINDOW, 128), lambda i: (i, 0))]
            + [pl.BlockSpec((1, SC_WINDOW), lambda i: (0, i))] * passes,
            out_specs=[],
            core_axis_name=("c", "s"),
            dimension_semantics=(pltpu.PARALLEL,),
        )(rows_hbm, *idx_hbms)

    return scatter(rows, *idx_lists)


def _expert_kernel(layer, be_ref, nu_ref, nxt_ref, gs_ref, nv_ref, xb_ref, w1_ref, w3_ref, w2_ref, yb_ref,
                   wf1_ref, wf3_ref, wf2_ref, w1b_ref, w3b_ref, w2b_ref, sem):
    b = pl.program_id(0)
    used = b < nu_ref[0]
    new_expert = jnp.logical_or(b == 0, be_ref[b] != be_ref[jnp.maximum(b - 1, 0)])

    def weight_copies(e, s):
        return [pltpu.make_async_copy(src.at[layer, e], dst.at[s], sem.at[s])
                for src, dst in ((w1_ref, wf1_ref), (w3_ref, wf3_ref), (w2_ref, wf2_ref))]

    @pl.when(b == 0)
    def _():
        for cp in weight_copies(be_ref[0], 0):
            cp.start(priority=1)

    @pl.when(jnp.logical_and(used, new_expert))
    def _():
        s = gs_ref[b]
        for cp in weight_copies(be_ref[b], s):
            cp.wait()

        @pl.when(nxt_ref[b] >= 0)
        def _():
            for cp in weight_copies(nxt_ref[b], 1 - s):
                cp.start(priority=1)

        w1b_ref[...] = wf1_ref[s].astype(jnp.bfloat16)
        w3b_ref[...] = wf3_ref[s].astype(jnp.bfloat16)
        w2b_ref[...] = wf2_ref[s].astype(jnp.bfloat16)

    def expert_mlp(rows):
        live = lax.broadcasted_iota(jnp.int32, (rows, 1), 0) < nv_ref[b]
        packed_in = jnp.concatenate(
            [jnp.where(live, lax.bitcast_convert_type(xb_ref[j, 0:rows], jnp.uint32), jnp.uint32(0))
             for j in range(NPIECE)], axis=1)
        x = jnp.concatenate(_unpack_bf16_pairs(packed_in), axis=1).astype(jnp.bfloat16)
        dot32 = functools.partial(jnp.dot, preferred_element_type=jnp.float32)
        a = dot32(x, w1b_ref[...])
        g = dot32(x, w3b_ref[...])
        hmid = (a * _sigmoid(a) * g).astype(jnp.bfloat16)
        packed = _pack_bf16_pairs(dot32(hmid, w2b_ref[...]))
        for j in range(NPIECE):
            yb_ref[j, 0:rows] = lax.bitcast_convert_type(packed[:, j * 128:(j + 1) * 128], jnp.int32)

    half = BM // 2
    half_full = nv_ref[b] <= half

    @pl.when(jnp.logical_and(used, jnp.logical_not(half_full)))
    def _():
        expert_mlp(BM)

    @pl.when(jnp.logical_and(used, half_full))
    def _():
        expert_mlp(half)
        yb_ref[:, half:BM, :] = jnp.zeros((NPIECE, BM - half, 128), jnp.int32)

    @pl.when(jnp.logical_not(used))
    def _():
        yb_ref[...] = jnp.zeros_like(yb_ref)


def _experts(layer, block_expert, n_used, next_expert, group_slot, block_valid, xb, w1, w3, w2):
    n_slots = xb.shape[1]
    n_blocks = n_slots // BM

    def row_map(b, be, nu, nxt, gs, nv):
        return (0, jnp.minimum(b, nu[0] - 1), 0)

    def out_map(b, be, nu, nxt, gs, nv):
        return (0, b, 0)

    grid_spec = pltpu.PrefetchScalarGridSpec(
        num_scalar_prefetch=5,
        grid=(n_blocks,),
        in_specs=[pl.BlockSpec((NPIECE, BM, 128), row_map),
                  pl.BlockSpec(memory_space=pl.ANY),
                  pl.BlockSpec(memory_space=pl.ANY),
                  pl.BlockSpec(memory_space=pl.ANY)],
        out_specs=pl.BlockSpec((NPIECE, BM, 128), out_map),
        scratch_shapes=[pltpu.VMEM((2, D, D_FF), jnp.float32),
                        pltpu.VMEM((2, D, D_FF), jnp.float32),
                        pltpu.VMEM((2, D_FF, D), jnp.float32),
                        pltpu.VMEM((D, D_FF), jnp.bfloat16),
                        pltpu.VMEM((D, D_FF), jnp.bfloat16),
                        pltpu.VMEM((D_FF, D), jnp.bfloat16),
                        pltpu.SemaphoreType.DMA((2,))],
    )
    return pl.pallas_call(
        functools.partial(_expert_kernel, layer),
        grid_spec=grid_spec,
        out_shape=jax.ShapeDtypeStruct((NPIECE, n_slots, 128), jnp.int32),
        compiler_params=pltpu.CompilerParams(dimension_semantics=("arbitrary",),
                                             vmem_limit_bytes=VMEM_LIMIT),
        name="experts",
    )(block_expert, n_used, next_expert, group_slot, block_valid, xb, w1, w3, w2)


def _sc_gather_rows(table, idx):
    n_rows = idx.shape[0]

    @functools.partial(
        pl.kernel, mesh=_sc_mesh(),
        out_type=jax.ShapeDtypeStruct((n_rows, 128), table.dtype),
        scratch_types=[pltpu.SemaphoreType.DMA((2,))])
    def gather(table_hbm, idx_hbm, out_hbm, sem):
        def body(idx0_vmem, idx1_vmem, out0_vmem, out1_vmem):
            first = pltpu.async_copy(table_hbm.at[idx0_vmem.at[0]], out0_vmem, sem.at[0])
            second = pltpu.async_copy(table_hbm.at[idx1_vmem.at[0]], out1_vmem, sem.at[1])
            first.wait()
            second.wait()

        pltpu.emit_pipeline(
            body,
            grid=(n_rows // (2 * SC_WINDOW),),
            in_specs=[pl.BlockSpec((1, SC_WINDOW), lambda i: (0, 2 * i)),
                      pl.BlockSpec((1, SC_WINDOW), lambda i: (0, 2 * i + 1))],
            out_specs=[pl.BlockSpec((SC_WINDOW, 128), lambda i: (2 * i, 0)),
                       pl.BlockSpec((SC_WINDOW, 128), lambda i: (2 * i + 1, 0))],
            core_axis_name=("c", "s"),
            dimension_semantics=(pltpu.PARALLEL,),
        )(idx_hbm, idx_hbm, out_hbm, out_hbm)

    return gather(table, idx.reshape(1, n_rows))


def _combine_kernel(final, g_ref, x_ref, w_ref, nrm_ref, o_ref):
    w = w_ref[...].T
    ys = {}
    for j in range(NPIECE):
        hi0, lo0 = _unpack_bf16_pairs(lax.bitcast_convert_type(g_ref[0, j], jnp.uint32))
        hi1, lo1 = _unpack_bf16_pairs(lax.bitcast_convert_type(g_ref[1, j], jnp.uint32))
        c_hi, c_lo = j * 128, DP + j * 128
        ys[c_hi] = x_ref[:, c_hi:c_hi + 128] + w[:, 0:1] * hi0 + w[:, 1:2] * hi1
        ys[c_lo] = x_ref[:, c_lo:c_lo + 128] + w[:, 0:1] * lo0 + w[:, 1:2] * lo1
    if final:
        ms = sum(jnp.sum(y * y, axis=-1, keepdims=True) for y in ys.values()) * (1.0 / D)
        scale = lax.rsqrt(ms + RMS_EPS)
        ys = {c: y * scale * nrm_ref[:, c:c + 128] for c, y in ys.items()}
    for c, y in ys.items():
        o_ref[:, c:c + 128] = y


def _combine(gathered, x2, wgt, nrm, final):
    T = x2.shape[0]
    return pl.pallas_call(
        functools.partial(_combine_kernel, final),
        grid=(T // TD,),
        in_specs=[pl.BlockSpec((2, NPIECE, TD, 128), lambda i: (0, 0, i, 0)),
                  pl.BlockSpec((TD, D), lambda i: (i, 0)),
                  pl.BlockSpec((8, TD), lambda i: (0, i)),
                  pl.BlockSpec((1, D), lambda i: (0, 0))],
        out_specs=pl.BlockSpec((TD, D), lambda i: (i, 0)),
        out_shape=jax.ShapeDtypeStruct((T, D), jnp.float32),
        compiler_params=pltpu.CompilerParams(dimension_semantics=("arbitrary",)),
        name="combine",
    )(gathered, x2, wgt, nrm)


def _moe(layer, h2, idx, cnt, w1, w3, w2):
    T = idx.shape[1]
    counts = cnt[:, 0].astype(jnp.int32)
    padded = (counts + BM - 1) // BM * BM
    padded_end = jnp.cumsum(padded)
    padded_start = padded_end - padded
    eids = jnp.arange(N_EXPERTS, dtype=jnp.int32)[:, None]
    start_of = lambda e: jnp.sum(jnp.where(e[None, :] == eids, padded_start[:, None], 0), axis=0)
    dest = jnp.stack([start_of(idx[0]) + idx[2], start_of(idx[1]) + idx[3]])
    n_blocks = (2 * T) // BM + N_EXPERTS
    block_start = jnp.arange(n_blocks, dtype=jnp.int32) * BM
    block_expert = jnp.minimum(
        jnp.sum((padded_end[None, :] <= block_start[:, None]).astype(jnp.int32), axis=1),
        N_EXPERTS - 1)
    n_used = (padded_end[-1] // BM).reshape(1)
    later = jnp.logical_and(eids.T > eids, (counts > 0)[None, :])
    next_of = jnp.min(jnp.where(later, eids.T, N_EXPERTS), axis=1)
    next_of = jnp.where(next_of < N_EXPERTS, next_of, -1)
    pick = lambda table: jnp.sum(
        jnp.where(block_expert[:, None] == eids.T, table[None, :], 0), axis=1)
    next_expert = pick(next_of)
    group_slot = pick(jnp.cumsum((counts > 0).astype(jnp.int32)) - 1) % 2
    block_valid = jnp.clip(pick(padded_start + counts) - block_start, 0, BM)
    n_slots = n_blocks * BM
    piece_base = (jnp.arange(NPIECE, dtype=jnp.int32) * n_slots)[None, :, None]
    rows = (dest[:, None, :] + piece_base).reshape(-1)
    xb = _sc_scatter_rows(h2.reshape(NPIECE * T, 128), rows, 2, NPIECE * n_slots)
    yb = _experts(layer, block_expert, n_used, next_expert, group_slot, block_valid,
                  xb.reshape(NPIECE, n_slots, 128), w1, w3, w2)
    gathered = _sc_gather_rows(yb.reshape(NPIECE * n_slots, 128), rows)
    return gathered.reshape(2, NPIECE, T, 128)


def kernel(x, norm_mix, w_in, conv_a, w_a_out, conv_b, ln_b_g, ln_b_b, w_b_out, ln_c_g, ln_c_b,
           w_s, b_s, w_c_out, w_o, norm_ffn, w_group, b_group, w_router, b_router, w1, w3, w2,
           norm_final):
    bsz, seq, d = x.shape
    depth = norm_mix.shape[0]
    bf = jnp.bfloat16
    x2 = x.reshape(bsz * seq, d)
    wr = jnp.zeros((depth, d, NLOG), jnp.float32)
    wr = wr.at[:, :, 0:N_GROUPS].set(w_group).at[:, :, E_OFF:E_OFF + N_EXPERTS].set(w_router)
    wr_hi = wr.astype(bf)
    wr_lo = (wr - wr_hi.astype(jnp.float32)).astype(bf)
    br = jnp.zeros((depth, 1, NLOG), jnp.float32)
    br = br.at[:, 0, 0:N_GROUPS].set(b_group).at[:, 0, E_OFF:E_OFF + N_EXPERTS].set(b_router)
    bsb = jnp.broadcast_to(b_s[:, :, :, None], (depth, N_HEADS_C, CHUNK, CHUNK))
    row = lambda p: p[:, None, :]
    mixer_params = (row(norm_mix), w_in.astype(bf), conv_a, w_a_out.astype(bf), conv_b,
                    row(ln_b_g), row(ln_b_b), w_b_out.astype(bf), row(ln_c_g), row(ln_c_b),
                    w_s.astype(bf), bsb, w_c_out.astype(bf), w_o.astype(bf), row(norm_ffn), wr_hi, wr_lo, br)
    col = jnp.arange(SK, dtype=jnp.int32)[None, :]
    shift_sum = jnp.logical_and(
        col % PR == jnp.arange(RB, dtype=jnp.int32)[:, None] + col // PR, col < 8 * PR).astype(bf)
    pending = None
    for l in range(depth):
        x2, h2, idx, wgt, cnt = _mixer(l, x2, pending, seq, *mixer_params, shift_sum)
        gathered = _moe(l, h2, idx, cnt, w1, w3, w2)
        pending = (gathered, wgt.T)
    out = _combine(gathered, x2, wgt, norm_final[None], True)
    return out.reshape(bsz, seq, d)
```
